```python
import math
import jax, jax.numpy as jnp
from jax import lax
import numpy as np

D_MODEL = 1024
BATCH = 8
SEQ = 2048
DEPTH = 1
DEC_BATCH = 128
DEC_SEQ = 4
PAST_LEN = 16384
PAGE_SIZE = 128

D_MIX = 2 * D_MODEL
D_SSD = 3 * D_MIX // 4
SSD_HEAD_DIM = 64
N_SSD_HEADS = D_SSD // SSD_HEAD_DIM
N_SSD_GROUPS = 4
D_STATE = 128
CONV_WIDTH = 4
SSD_CHUNK = 128
D_CONV = D_SSD + 2 * N_SSD_GROUPS * D_STATE
D_POOL = D_MIX - D_SSD
POOL_WINDOWS = (2, 4, 8, 16)
N_POOL_GROUPS = len(POOL_WINDOWS)
POOL_GROUP_DIM = D_POOL // N_POOL_GROUPS
POOL_HIST = max(POOL_WINDOWS) - 1
D_IN_PROJ = D_SSD + D_CONV + N_SSD_HEADS + D_POOL
N_EXPERTS = 32
TOP_K = 4
D_FF = D_MODEL
SWIGLU_LIMIT = 7.0
SWIGLU_ALPHA = 1.702
EPS = 1e-5

kernel_name = "hymba_ssd_pool_moe_step"


def _rmsnorm(x, w):
    xf = x.astype(jnp.float32)
    xf = xf * lax.rsqrt(jnp.mean(xf * xf, axis=-1, keepdims=True) + EPS)
    return (xf * w.astype(jnp.float32)).astype(x.dtype)


def _causal_conv(xbc, buf, conv_w, conv_b):
    L = xbc.shape[1]
    ext = jnp.concatenate([buf.astype(xbc.dtype), xbc], axis=1)
    out = conv_b
    for k in range(CONV_WIDTH):
        out = out + ext[:, k:k + L] * conv_w[k]
    return jax.nn.silu(out), ext[:, L:]


def _ssd(x, dt, A, Bm, Cm, D_skip, init_state):
    b, l, h, p = x.shape
    g = N_SSD_GROUPS
    hg = h // g
    n = Bm.shape[-1]
    q = SSD_CHUNK if l % SSD_CHUNK == 0 else l
    c = l // q
    f32 = jnp.float32
    xf = x.astype(f32).reshape(b, c, q, g, hg, p)
    dtc = dt.reshape(b, c, q, g, hg)
    Bc = Bm.astype(f32).reshape(b, c, q, g, n)
    Cc = Cm.astype(f32).reshape(b, c, q, g, n)
    a_cum = jnp.cumsum(dtc * A.reshape(g, hg), axis=2)
    xdt = xf * dtc[..., None]
    seg = a_cum[:, :, :, None] - a_cum[:, :, None, :]
    mask = jnp.tril(jnp.ones((q, q), dtype=bool))[:, :, None, None]
    decay = jnp.exp(jnp.where(mask, seg, -jnp.inf))
    cb = jnp.einsum('bcign,bcjgn->bcijg', Cc, Bc)
    y_diag = jnp.einsum('bcijg,bcijgh,bcjghp->bcighp', cb, decay, xdt)
    decay_to_end = jnp.exp(a_cum[:, :, -1:] - a_cum)
    chunk_states = jnp.einsum('bcqgn,bcqgh,bcqghp->bcghpn', Bc, decay_to_end, xdt)
    chunk_decay = jnp.exp(a_cum[:, :, -1])
    s0 = init_state.astype(f32).reshape(b, g, hg, p, n)

    def step(s, inp):
        st, dec = inp
        return dec[..., None, None] * s + st, s

    final, s_enter = lax.scan(step, s0, (jnp.moveaxis(chunk_states, 1, 0), jnp.moveaxis(chunk_decay, 1, 0)))
    s_enter = jnp.moveaxis(s_enter, 0, 1)
    y_off = jnp.einsum('bcqgn,bcghpn,bcqgh->bcqghp', Cc, s_enter, jnp.exp(a_cum))
    y = y_diag + y_off + xf * D_skip.astype(f32).reshape(g, hg)[..., None]
    return y.reshape(b, l, h * p).astype(x.dtype), final.reshape(b, h, p, n).astype(init_state.dtype)


def _pool_mix(u, buf, pos0, pool_w, pool_scale):
    b, l, C = u.shape
    ext = jnp.concatenate([buf.astype(u.dtype), u], axis=1)
    csum = jnp.cumsum(ext.astype(jnp.float32), axis=1)
    csum = jnp.concatenate([jnp.zeros((b, 1, C), jnp.float32), csum], axis=1)
    end = csum[:, POOL_HIST + 1:]
    pos = pos0 + jnp.arange(l)
    uf = u.astype(jnp.float32)
    outs = []
    for gi, w in enumerate(POOL_WINDOWS):
        sl = slice(gi * POOL_GROUP_DIM, (gi + 1) * POOL_GROUP_DIM)
        start = csum[:, POOL_HIST + 1 - w:POOL_HIST + 1 - w + l, sl]
        count = jnp.minimum(pos + 1, w).astype(jnp.float32)[None, :, None]
        pooled = (end[..., sl] - start) / count - uf[..., sl]
        outs.append(pooled.astype(u.dtype) @ pool_w[gi])
    out = jnp.concatenate(outs, axis=-1) * pool_scale
    return out, ext[:, l:]


def _mixer(h, conv_buf, ssm_state, pool_buf, pos0, w_in, conv_w, conv_b, dt_bias, A_log, D_skip,
           ssd_norm_w, pool_w, pool_scale, w_out):
    b, l, _ = h.shape
    proj = h @ w_in
    z, xbc, dt_raw, u = jnp.split(proj, [D_SSD, D_SSD + D_CONV, D_SSD + D_CONV + N_SSD_HEADS], axis=-1)
    xbc, new_conv = _causal_conv(xbc, conv_buf, conv_w, conv_b)
    xs, Bm, Cm = jnp.split(xbc, [D_SSD, D_SSD + N_SSD_GROUPS * D_STATE], axis=-1)
    dt = jax.nn.softplus(dt_raw.astype(jnp.float32) + dt_bias.astype(jnp.float32))
    A = -jnp.exp(A_log.astype(jnp.float32))
    y, new_ssm = _ssd(xs.reshape(b, l, N_SSD_HEADS, SSD_HEAD_DIM), dt, A,
                      Bm.reshape(b, l, N_SSD_GROUPS, D_STATE), Cm.reshape(b, l, N_SSD_GROUPS, D_STATE),
                      D_skip, ssm_state)
    y = _rmsnorm(y * jax.nn.silu(z), ssd_norm_w)
    pool_out, new_pool = _pool_mix(u, pool_buf, pos0, pool_w, pool_scale)
    out = jnp.concatenate([y, pool_out], axis=-1) @ w_out
    return out, new_ssm, new_conv, new_pool


def _moe(h, router_w, router_b, w_gate_up, b_gate_up, w_down, b_down):
    b, l, d = h.shape
    t = h.reshape(-1, d)
    logits = (t @ router_w + router_b).astype(jnp.float32)
    top_val, top_idx = lax.top_k(logits, TOP_K)
    gates = jax.nn.softmax(top_val, axis=-1).astype(h.dtype)
    flat_e = top_idx.reshape(-1)
    order = jnp.argsort(flat_e)
    tok = order // TOP_K
    e_sorted = flat_e[order]
    sizes = jnp.bincount(flat_e, length=N_EXPERTS).astype(jnp.int32)
    xs = t[tok]
    gu = lax.ragged_dot(xs, w_gate_up, sizes) + b_gate_up[e_sorted]
    gate, up = gu[:, :D_FF], gu[:, D_FF:]
    gate = jnp.minimum(gate, SWIGLU_LIMIT)
    up = jnp.clip(up, -SWIGLU_LIMIT, SWIGLU_LIMIT)
    act = (up + 1) * (gate * jax.nn.sigmoid(SWIGLU_ALPHA * gate))
    out = lax.ragged_dot(act, w_down, sizes) + b_down[e_sorted]
    out = out * gates.reshape(-1)[order][:, None]
    y = jnp.zeros_like(t).at[tok].add(out)
    return y.reshape(b, l, d)


def _layer(x, conv_buf, ssm_state, pool_buf, pos0, norm1_w, w_in, conv_w, conv_b, dt_bias, A_log, D_skip,
           ssd_norm_w, pool_w, pool_scale, w_out, norm2_w, router_w, router_b, w_gate_up, b_gate_up,
           w_down, b_down):
    mix, new_ssm, new_conv, new_pool = _mixer(_rmsnorm(x, norm1_w), conv_buf, ssm_state, pool_buf, pos0,
                                              w_in, conv_w, conv_b, dt_bias, A_log, D_skip, ssd_norm_w,
                                              pool_w, pool_scale, w_out)
    x = x + mix
    x = x + _moe(_rmsnorm(x, norm2_w), router_w, router_b, w_gate_up, b_gate_up, w_down, b_down)
    return x, new_ssm, new_conv, new_pool


def setup_inputs(seed: int = 0) -> dict:
    key = jax.random.key(seed)
    ks = iter(jax.random.split(key, 32))
    f32 = jnp.float32

    def nrm(shape, scale):
        return jax.random.normal(next(ks), shape, f32) * scale

    x_prompt = nrm((BATCH, SEQ, D_MODEL), 1.0)
    x_sample = nrm((DEC_BATCH, DEC_SEQ, D_MODEL), 1.0)
    state_ssm = nrm((DEPTH, DEC_BATCH, N_SSD_HEADS, SSD_HEAD_DIM, D_STATE), 0.1)
    state_conv = nrm((DEPTH, DEC_BATCH, CONV_WIDTH - 1, D_CONV), 1.0)
    state_pool = nrm((DEPTH, DEC_BATCH, POOL_HIST, D_POOL), 1.0)
    norm1_w = 1.0 + nrm((DEPTH, D_MODEL), 0.02)
    w_in = nrm((DEPTH, D_MODEL, D_IN_PROJ), D_MODEL ** -0.5)
    conv_w = nrm((DEPTH, CONV_WIDTH, D_CONV), CONV_WIDTH ** -0.5)
    conv_b = nrm((DEPTH, D_CONV), 0.02)
    dt0 = jnp.exp(jax.random.uniform(next(ks), (DEPTH, N_SSD_HEADS), f32, math.log(1e-3), math.log(1e-1)))
    dt_bias = dt0 + jnp.log(-jnp.expm1(-dt0))
    A_log = jnp.log(jax.random.uniform(next(ks), (DEPTH, N_SSD_HEADS), f32, 1.0, 16.0))
    D_skip = 1.0 + nrm((DEPTH, N_SSD_HEADS), 0.1)
    ssd_norm_w = 1.0 + nrm((DEPTH, D_SSD), 0.02)
    pool_w = nrm((DEPTH, N_POOL_GROUPS, POOL_GROUP_DIM, POOL_GROUP_DIM), POOL_GROUP_DIM ** -0.5)
    pool_scale = 1.0 + nrm((DEPTH, D_POOL), 0.02)
    w_out = nrm((DEPTH, D_MIX, D_MODEL), D_MIX ** -0.5)
    norm2_w = 1.0 + nrm((DEPTH, D_MODEL), 0.02)
    router_w = nrm((DEPTH, D_MODEL, N_EXPERTS), D_MODEL ** -0.5)
    router_b = nrm((DEPTH, N_EXPERTS), 0.01)
    w_gate_up = nrm((DEPTH, N_EXPERTS, D_MODEL, 2 * D_FF), D_MODEL ** -0.5)
    b_gate_up = nrm((DEPTH, N_EXPERTS, 2 * D_FF), 0.01)
    w_down = nrm((DEPTH, N_EXPERTS, D_FF, D_MODEL), D_FF ** -0.5)
    b_down = nrm((DEPTH, N_EXPERTS, D_MODEL), 0.01)
    final_norm_w = 1.0 + nrm((D_MODEL,), 0.02)
    return {"x_prompt": x_prompt, "x_sample": x_sample, "state_ssm": state_ssm, "state_conv": state_conv,
            "state_pool": state_pool, "norm1_w": norm1_w, "w_in": w_in, "conv_w": conv_w, "conv_b": conv_b,
            "dt_bias": dt_bias, "A_log": A_log, "D_skip": D_skip, "ssd_norm_w": ssd_norm_w, "pool_w": pool_w,
            "pool_scale": pool_scale, "w_out": w_out, "norm2_w": norm2_w, "router_w": router_w,
            "router_b": router_b, "w_gate_up": w_gate_up, "b_gate_up": b_gate_up, "w_down": w_down,
            "b_down": b_down, "final_norm_w": final_norm_w}


def reference(x_prompt, x_sample, state_ssm, state_conv, state_pool, norm1_w, w_in, conv_w, conv_b, dt_bias,
              A_log, D_skip, ssd_norm_w, pool_w, pool_scale, w_out, norm2_w, router_w, router_b, w_gate_up,
              b_gate_up, w_down, b_down, final_norm_w):
    yp, ys = x_prompt, x_sample
    ssm_p, conv_p, pool_p, ssm_s, conv_s, pool_s = [], [], [], [], [], []
    bp = x_prompt.shape[0]
    dt_ = x_prompt.dtype
    for i in range(DEPTH):
        lw = (norm1_w[i], w_in[i], conv_w[i], conv_b[i], dt_bias[i], A_log[i], D_skip[i], ssd_norm_w[i],
              pool_w[i], pool_scale[i], w_out[i], norm2_w[i], router_w[i], router_b[i], w_gate_up[i],
              b_gate_up[i], w_down[i], b_down[i])
        yp, s1, c1, p1 = _layer(yp, jnp.zeros((bp, CONV_WIDTH - 1, D_CONV), dt_),
                                jnp.zeros((bp, N_SSD_HEADS, SSD_HEAD_DIM, D_STATE), dt_),
                                jnp.zeros((bp, POOL_HIST, D_POOL), dt_), 0, *lw)
        ys, s2, c2, p2 = _layer(ys, state_conv[i], state_ssm[i], state_pool[i], PAST_LEN, *lw)
        ssm_p.append(s1); conv_p.append(c1); pool_p.append(p1)
        ssm_s.append(s2); conv_s.append(c2); pool_s.append(p2)
    yp = _rmsnorm(yp, final_norm_w)
    ys = _rmsnorm(ys, final_norm_w)
    return (yp, ys, jnp.stack(ssm_p), jnp.stack(conv_p), jnp.stack(pool_p),
            jnp.stack(ssm_s), jnp.stack(conv_s), jnp.stack(pool_s))
```

```python
import functools
import math
import jax, jax.numpy as jnp
from jax import lax
import numpy as np
from jax.experimental import pallas as pl
from jax.experimental.pallas import tpu as pltpu

D_MODEL = 1024
BATCH = 8
SEQ = 2048
DEC_BATCH = 128
DEC_SEQ = 4
PAST_LEN = 16384

D_MIX = 2 * D_MODEL
D_SSD = 3 * D_MIX // 4
SSD_HEAD_DIM = 64
N_SSD_HEADS = D_SSD // SSD_HEAD_DIM
N_SSD_GROUPS = 4
D_STATE = 128
CONV_WIDTH = 4
SSD_CHUNK = 128
D_CONV = D_SSD + 2 * N_SSD_GROUPS * D_STATE
D_POOL = D_MIX - D_SSD
POOL_WINDOWS = (2, 4, 8, 16)
N_POOL_GROUPS = len(POOL_WINDOWS)
POOL_GROUP_DIM = D_POOL // N_POOL_GROUPS
POOL_HIST = max(POOL_WINDOWS) - 1
D_IN_PROJ = D_SSD + D_CONV + N_SSD_HEADS + D_POOL
N_EXPERTS = 32
TOP_K = 4
D_FF = D_MODEL
SWIGLU_LIMIT = 7.0
SWIGLU_ALPHA = 1.702
EPS = 1e-5

LANES = 128
BF16_SUBLANES = 16
VMEM_LIMIT_BYTES = 48 * 1024 * 1024

MOE_TOKEN_TILE = 512
MOE_SEG_ROWS = BF16_SUBLANES
MOE_ROW_TILE = 256
MOE_CHUNK = 256


def _rmsnorm(x, w):
    xf = x.astype(jnp.float32)
    xf = xf * lax.rsqrt(jnp.mean(xf * xf, axis=-1, keepdims=True) + EPS)
    return (xf * w.astype(jnp.float32)).astype(x.dtype)


def _causal_conv(xbc, buf, conv_w, conv_b):
    L = xbc.shape[1]
    ext = jnp.concatenate([buf.astype(xbc.dtype), xbc], axis=1)
    out = conv_b
    for k in range(CONV_WIDTH):
        out = out + ext[:, k:k + L] * conv_w[k]
    return jax.nn.silu(out), ext[:, L:]


def _ssd(x, dt, A, Bm, Cm, D_skip, init_state):
    b, l, h, p = x.shape
    g = N_SSD_GROUPS
    hg = h // g
    n = Bm.shape[-1]
    q = SSD_CHUNK if l % SSD_CHUNK == 0 else l
    c = l // q
    f32 = jnp.float32
    xf = x.astype(f32).reshape(b, c, q, g, hg, p)
    dtc = dt.reshape(b, c, q, g, hg)
    Bc = Bm.astype(f32).reshape(b, c, q, g, n)
    Cc = Cm.astype(f32).reshape(b, c, q, g, n)
    a_cum = jnp.cumsum(dtc * A.reshape(g, hg), axis=2)
    xdt = xf * dtc[..., None]
    seg = a_cum[:, :, :, None] - a_cum[:, :, None, :]
    mask = jnp.tril(jnp.ones((q, q), dtype=bool))[:, :, None, None]
    decay = jnp.exp(jnp.where(mask, seg, -jnp.inf))
    cb = jnp.einsum('bcign,bcjgn->bcijg', Cc, Bc)
    y_diag = jnp.einsum('bcijg,bcijgh,bcjghp->bcighp', cb, decay, xdt)
    decay_to_end = jnp.exp(a_cum[:, :, -1:] - a_cum)
    chunk_states = jnp.einsum('bcqgn,bcqgh,bcqghp->bcghpn', Bc, decay_to_end, xdt)
    chunk_decay = jnp.exp(a_cum[:, :, -1])
    s0 = init_state.astype(f32).reshape(b, g, hg, p, n)

    def step(s, inp):
        st, dec = inp
        return dec[..., None, None] * s + st, s

    final, s_enter = lax.scan(step, s0, (jnp.moveaxis(chunk_states, 1, 0), jnp.moveaxis(chunk_decay, 1, 0)))
    s_enter = jnp.moveaxis(s_enter, 0, 1)
    y_off = jnp.einsum('bcqgn,bcghpn,bcqgh->bcqghp', Cc, s_enter, jnp.exp(a_cum))
    y = y_diag + y_off + xf * D_skip.astype(f32).reshape(g, hg)[..., None]
    return y.reshape(b, l, h * p).astype(x.dtype), final.reshape(b, h, p, n).astype(init_state.dtype)


def _pool_mix(u, buf, pos0, pool_w, pool_scale):
    b, l, C = u.shape
    ext = jnp.concatenate([buf.astype(u.dtype), u], axis=1)
    csum = jnp.cumsum(ext.astype(jnp.float32), axis=1)
    csum = jnp.concatenate([jnp.zeros((b, 1, C), jnp.float32), csum], axis=1)
    end = csum[:, POOL_HIST + 1:]
    pos = pos0 + jnp.arange(l)
    uf = u.astype(jnp.float32)
    outs = []
    for gi, w in enumerate(POOL_WINDOWS):
        sl = slice(gi * POOL_GROUP_DIM, (gi + 1) * POOL_GROUP_DIM)
        start = csum[:, POOL_HIST + 1 - w:POOL_HIST + 1 - w + l, sl]
        count = jnp.minimum(pos + 1, w).astype(jnp.float32)[None, :, None]
        pooled = (end[..., sl] - start) / count - uf[..., sl]
        outs.append(pooled.astype(u.dtype) @ pool_w[gi])
    out = jnp.concatenate(outs, axis=-1) * pool_scale
    return out, ext[:, l:]


def _mixer(h, conv_buf, ssm_state, pool_buf, pos0, w_in, conv_w, conv_b, dt_bias, A_log, D_skip,
           ssd_norm_w, pool_w, pool_scale, w_out):
    b, l, _ = h.shape
    proj = h @ w_in
    z, xbc, dt_raw, u = jnp.split(proj, [D_SSD, D_SSD + D_CONV, D_SSD + D_CONV + N_SSD_HEADS], axis=-1)
    xbc, new_conv = _causal_conv(xbc, conv_buf, conv_w, conv_b)
    xs, Bm, Cm = jnp.split(xbc, [D_SSD, D_SSD + N_SSD_GROUPS * D_STATE], axis=-1)
    dt = jax.nn.softplus(dt_raw.astype(jnp.float32) + dt_bias.astype(jnp.float32))
    A = -jnp.exp(A_log.astype(jnp.float32))
    y, new_ssm = _ssd(xs.reshape(b, l, N_SSD_HEADS, SSD_HEAD_DIM), dt, A,
                      Bm.reshape(b, l, N_SSD_GROUPS, D_STATE), Cm.reshape(b, l, N_SSD_GROUPS, D_STATE),
                      D_skip, ssm_state)
    y = _rmsnorm(y * jax.nn.silu(z), ssd_norm_w)
    pool_out, new_pool = _pool_mix(u, pool_buf, pos0, pool_w, pool_scale)
    out = jnp.concatenate([y, pool_out], axis=-1) @ w_out
    return out, new_ssm, new_conv, new_pool


def _mix_layer(x, conv_buf, ssm_state, pool_buf, pos0, norm1_w, w_in, conv_w, conv_b, dt_bias, A_log, D_skip,
               ssd_norm_w, pool_w, pool_scale, w_out):
    mix, new_ssm, new_conv, new_pool = _mixer(_rmsnorm(x, norm1_w), conv_buf, ssm_state, pool_buf, pos0,
                                              w_in, conv_w, conv_b, dt_bias, A_log, D_skip, ssd_norm_w,
                                              pool_w, pool_scale, w_out)
    return x + mix, new_ssm, new_conv, new_pool


def _moe_sizes(n_tokens, tm):
    nt = n_tokens // tm
    lmax = -(-(TOP_K * tm + N_EXPERTS * (MOE_SEG_ROWS - 1)) // MOE_CHUNK) * MOE_CHUNK
    rows = TOP_K * n_tokens + nt * N_EXPERTS * (MOE_SEG_ROWS - 1) + N_EXPERTS * (MOE_ROW_TILE - 1)
    n_row_tiles = -(-rows // MOE_ROW_TILE)
    return nt, lmax, n_row_tiles


def _router_body(x_ref, nw_ref, rwh_ref, rwl_ref, rb_ref, h_ref, posg_ref, post_ref, cnt_ref):
    f32, bf16 = jnp.float32, jnp.bfloat16
    tm = x_ref.shape[0]
    x = x_ref[...]
    h = x * lax.rsqrt(jnp.mean(x * x, axis=-1, keepdims=True) + EPS) * nw_ref[...]
    h_hi = h.astype(bf16)
    h_ref[...] = h_hi
    h_lo = (h - h_hi.astype(f32)).astype(bf16)
    wh = rwh_ref[...]
    logits = (jnp.dot(h_hi, wh, preferred_element_type=f32)
              + jnp.dot(h_lo, wh, preferred_element_type=f32)
              + jnp.dot(h_hi, rwl_ref[...], preferred_element_type=f32)) + rb_ref[...]
    lane = lax.broadcasted_iota(jnp.int32, (tm, LANES), 1)
    lanef = lane.astype(f32)
    neg = jnp.float32(-jnp.inf)
    l = jnp.where(lane < N_EXPERTS, logits, neg)
    sels, vals = [], []
    for _ in range(TOP_K):
        m = jnp.max(l, axis=1, keepdims=True)
        idx = jnp.min(jnp.where(l == m, lanef, jnp.float32(LANES)), axis=1, keepdims=True)
        sel = lanef == idx
        l = jnp.where(sel, neg, l)
        sels.append(sel)
        vals.append(m)
    exps = [jnp.exp(v - vals[0]) for v in vals]
    denom = exps[0] + exps[1] + exps[2] + exps[3]
    gates = [e / denom for e in exps]
    chosen = jnp.where(sels[0] | sels[1] | sels[2] | sels[3], 1.0, 0.0).astype(f32)
    row = lax.broadcasted_iota(jnp.int32, (tm, tm), 0)
    col = lax.broadcasted_iota(jnp.int32, (tm, tm), 1)
    lower = jnp.where(col < row, 1.0, 0.0).astype(bf16)
    rank = jnp.dot(lower, chosen.astype(bf16), preferred_element_type=f32)
    cnt = jnp.sum(chosen, axis=0, keepdims=True)
    seg_units = jnp.floor((cnt + (MOE_SEG_ROWS - 1)) * (1.0 / MOE_SEG_ROWS))
    r2 = lax.broadcasted_iota(jnp.int32, (LANES, LANES), 0)
    c2 = lax.broadcasted_iota(jnp.int32, (LANES, LANES), 1)
    upper = jnp.where(r2 < c2, 1.0, 0.0).astype(bf16)
    lstart = jnp.dot(jnp.broadcast_to(seg_units, (8, LANES)).astype(bf16), upper,
                     preferred_element_type=f32)[0:1, :] * MOE_SEG_ROWS
    posmat = lstart + rank
    posg = jnp.zeros((tm, LANES), f32)
    for k in range(TOP_K):
        pos_k = jnp.sum(jnp.where(sels[k], posmat, 0.0), axis=1, keepdims=True)
        posg = posg + jnp.where(lane == k, pos_k, 0.0) + jnp.where(lane == TOP_K + k, gates[k], 0.0)
    posg_ref[...] = posg
    post_ref[...] = jnp.transpose(posg)[0:8, :]
    cnt_ref[0] = jnp.broadcast_to(cnt, (8, LANES)).astype(jnp.int32)


def _moe_router(x1, norm2_w, router_w, router_b, tm):
    n, d = x1.shape
    nt = n // tm
    f32, bf16 = jnp.float32, jnp.bfloat16
    rw = jnp.pad(router_w.astype(f32), ((0, 0), (0, LANES - N_EXPERTS)))
    rw_hi = rw.astype(bf16)
    rw_lo = (rw - rw_hi.astype(f32)).astype(bf16)
    rb = jnp.pad(router_b.astype(f32), (0, LANES - N_EXPERTS)).reshape(1, LANES)
    return pl.pallas_call(
        _router_body,
        grid=(nt,),
        in_specs=[pl.BlockSpec((tm, d), lambda i: (i, 0)),
                  pl.BlockSpec((1, d), lambda i: (0, 0)),
                  pl.BlockSpec((d, LANES), lambda i: (0, 0)),
                  pl.BlockSpec((d, LANES), lambda i: (0, 0)),
                  pl.BlockSpec((1, LANES), lambda i: (0, 0))],
        out_specs=[pl.BlockSpec((tm, d), lambda i: (i, 0)),
                   pl.BlockSpec((tm, LANES), lambda i: (i, 0)),
                   pl.BlockSpec((8, tm), lambda i: (0, i)),
                   pl.BlockSpec((1, 8, LANES), lambda i: (i, 0, 0))],
        out_shape=[jax.ShapeDtypeStruct((n, d), bf16),
                   jax.ShapeDtypeStruct((n, LANES), f32),
                   jax.ShapeDtypeStruct((8, n), f32),
                   jax.ShapeDtypeStruct((nt, 8, LANES), jnp.int32)],
        compiler_params=pltpu.CompilerParams(dimension_semantics=("parallel",),
                                             vmem_limit_bytes=VMEM_LIMIT_BYTES),
        name="moe_router",
    )(x1, norm2_w.reshape(1, d).astype(f32), rw_hi, rw_lo, rb)


def _moe_plan(cnt, n_row_tiles):
    i32 = jnp.int32
    pad = (cnt + (MOE_SEG_ROWS - 1)) // MOE_SEG_ROWS * MOE_SEG_ROWS
    lstart = jnp.cumsum(pad, axis=1) - pad
    lp = jnp.sum(pad, axis=1)
    tot = jnp.sum(pad, axis=0)
    reg = (tot + (MOE_ROW_TILE - 1)) // MOE_ROW_TILE * MOE_ROW_TILE
    reg_end = jnp.cumsum(reg)
    estart = reg_end - reg
    seg = estart[None, :] + jnp.cumsum(pad, axis=0) - pad
    tiles_end = reg_end // MOE_ROW_TILE
    n_active = tiles_end[-1]
    j = jnp.arange(n_row_tiles, dtype=i32)
    jc = jnp.minimum(j, n_active - 1)
    tile_e = jnp.searchsorted(tiles_end, jc, side='right').astype(i32)
    prev_e = jnp.concatenate([jnp.full((1,), -1, i32), tile_e[:-1]])
    tile_first = ((tile_e != prev_e) & (j < n_active)).astype(i32)
    return dict(
        lstart=lstart.reshape(-1).astype(i32), seg_units=(pad // MOE_SEG_ROWS).reshape(-1).astype(i32),
        seg=seg.reshape(-1).astype(i32), lp=lp.astype(i32),
        tail_start=(estart + tot).astype(i32), tail_units=((reg - tot) // MOE_SEG_ROWS).astype(i32),
        tile_e=tile_e, tile_blk=jc.astype(i32), tile_first=tile_first,
        n_active=n_active.reshape(1).astype(i32))


def _for_each_segment_copy(i, lstart_ref, units_ref, seg_ref, local_ref, global_ref, sem, to_global, fn):
    def per_expert(e, carry):
        k = i * N_EXPERTS + e
        ls, gs = lstart_ref[k], seg_ref[k]

        def per_piece(j, c):
            lo = pl.multiple_of(ls + j * MOE_SEG_ROWS, MOE_SEG_ROWS)
            go = pl.multiple_of(gs + j * MOE_SEG_ROWS, MOE_SEG_ROWS)
            loc = local_ref.at[pl.ds(lo, MOE_SEG_ROWS)]
            glo = global_ref.at[pl.ds(go, MOE_SEG_ROWS)]
            fn(pltpu.make_async_copy(loc, glo, sem) if to_global else pltpu.make_async_copy(glo, loc, sem))
            return c
        return lax.fori_loop(0, units_ref[k], per_piece, carry)
    lax.fori_loop(0, N_EXPERTS, per_expert, 0)


def _dispatch_body(lstart_ref, units_ref, seg_ref, lp_ref, tail_start_ref, tail_units_ref, n_active_ref,
                   h_ref, post_ref, xs_ref, stage_ref, zero_ref, sem):
    f32, bf16 = jnp.float32, jnp.bfloat16
    i = pl.program_id(0)
    tm = h_ref.shape[0]
    lmax = stage_ref.shape[0]
    n_row_tiles = xs_ref.shape[0] // MOE_ROW_TILE

    def for_each_fill_copy(fn):
        def per_expert(e, carry):
            ts = tail_start_ref[e]

            def per_piece(j, c):
                go = pl.multiple_of(ts + j * MOE_SEG_ROWS, MOE_SEG_ROWS)
                fn(pltpu.make_async_copy(zero_ref.at[pl.ds(0, MOE_SEG_ROWS)],
                                         xs_ref.at[pl.ds(go, MOE_SEG_ROWS)], sem))
                return c
            return lax.fori_loop(0, tail_units_ref[e], per_piece, carry)
        lax.fori_loop(0, N_EXPERTS, per_expert, 0)

        def per_unused_tile(j, c):
            go = pl.multiple_of(j * MOE_ROW_TILE, MOE_ROW_TILE)
            fn(pltpu.make_async_copy(zero_ref, xs_ref.at[pl.ds(go, MOE_ROW_TILE)], sem))
            return c
        lax.fori_loop(n_active_ref[0], n_row_tiles, per_unused_tile, 0)

    @pl.when(i == 0)
    def _():
        zero_ref[...] = jnp.zeros(zero_ref.shape, bf16)
        for_each_fill_copy(lambda cp: cp.start())
        for_each_fill_copy(lambda cp: cp.wait())

    pos = post_ref[0:TOP_K, :]
    h = h_ref[...]
    for c in range(lmax // MOE_CHUNK):
        @pl.when(c * MOE_CHUNK < lp_ref[i])
        def _():
            r = (lax.broadcasted_iota(jnp.int32, (MOE_CHUNK, tm), 0) + c * MOE_CHUNK).astype(f32)
            hit = (pos[0:1, :] == r) | (pos[1:2, :] == r) | (pos[2:3, :] == r) | (pos[3:4, :] == r)
            sel = jnp.where(hit, 1.0, 0.0).astype(bf16)
            stage_ref[c * MOE_CHUNK:(c + 1) * MOE_CHUNK, :] = jnp.dot(
                sel, h, preferred_element_type=f32).astype(bf16)

    args = (i, lstart_ref, units_ref, seg_ref, stage_ref, xs_ref, sem, True)
    _for_each_segment_copy(*args, lambda cp: cp.start())
    _for_each_segment_copy(*args, lambda cp: cp.wait())


def _moe_dispatch(h2, post, plan, tm, lmax, n_row_tiles):
    n, d = h2.shape
    nt = n // tm
    grid_spec = pltpu.PrefetchScalarGridSpec(
        num_scalar_prefetch=7,
        grid=(nt,),
        in_specs=[pl.BlockSpec((tm, d), lambda i, *_: (i, 0)),
                  pl.BlockSpec((8, tm), lambda i, *_: (0, i))],
        out_specs=pl.BlockSpec(memory_space=pl.ANY),
        scratch_shapes=[pltpu.VMEM((lmax, d), jnp.bfloat16),
                        pltpu.VMEM((MOE_ROW_TILE, d), jnp.bfloat16),
                        pltpu.SemaphoreType.DMA(())])
    return pl.pallas_call(
        _dispatch_body,
        grid_spec=grid_spec,
        out_shape=jax.ShapeDtypeStruct((n_row_tiles * MOE_ROW_TILE, d), jnp.bfloat16),
        compiler_params=pltpu.CompilerParams(dimension_semantics=("arbitrary",),
                                             vmem_limit_bytes=VMEM_LIMIT_BYTES),
        name="moe_dispatch",
    )(plan["lstart"], plan["seg_units"], plan["seg"], plan["lp"], plan["tail_start"], plan["tail_units"],
      plan["n_active"], h2, post)


def _experts_body(tile_e_ref, tile_blk_ref, tile_first_ref, n_active_ref,
                  xs_ref, wgu_ref, bgu_ref, wd_ref, bd_ref, os_ref, wgu_bf, wd_bf):
    f32, bf16 = jnp.float32, jnp.bfloat16
    j = pl.program_id(0)

    @pl.when(j < n_active_ref[0])
    def _():
        @pl.when(tile_first_ref[j] == 1)
        def _():
            wgu_bf[...] = wgu_ref[0].astype(bf16)
            wd_bf[...] = wd_ref[0].astype(bf16)

        x = xs_ref[...]
        gu = jnp.dot(x, wgu_bf[...], preferred_element_type=f32) + bgu_ref[0]
        gate = jnp.minimum(gu[:, :D_FF], SWIGLU_LIMIT)
        up = jnp.clip(gu[:, D_FF:], -SWIGLU_LIMIT, SWIGLU_LIMIT)
        act = (up + 1.0) * (gate * jax.nn.sigmoid(SWIGLU_ALPHA * gate))
        out = jnp.dot(act.astype(bf16), wd_bf[...], preferred_element_type=f32) + bd_ref[0]
        os_ref[...] = out.astype(bf16)

    @pl.when(j >= n_active_ref[0])
    def _():
        os_ref[...] = jnp.zeros(os_ref.shape, bf16)


def _moe_experts(xs, plan, w_gate_up, b_gate_up, w_down, b_down, n_row_tiles):
    d = xs.shape[1]
    grid_spec = pltpu.PrefetchScalarGridSpec(
        num_scalar_prefetch=4,
        grid=(n_row_tiles,),
        in_specs=[pl.BlockSpec((MOE_ROW_TILE, d), lambda j, te, tb, tf, na: (tb[j], 0)),
                  pl.BlockSpec((1, d, 2 * D_FF), lambda j, te, tb, tf, na: (te[j], 0, 0)),
                  pl.BlockSpec((1, 1, 2 * D_FF), lambda j, te, tb, tf, na: (te[j], 0, 0)),
                  pl.BlockSpec((1, D_FF, d), lambda j, te, tb, tf, na: (te[j], 0, 0)),
                  pl.BlockSpec((1, 1, d), lambda j, te, tb, tf, na: (te[j], 0, 0))],
        out_specs=pl.BlockSpec((MOE_ROW_TILE, d), lambda j, te, tb, tf, na: (j, 0)),
        scratch_shapes=[pltpu.VMEM((d, 2 * D_FF), jnp.bfloat16),
                        pltpu.VMEM((D_FF, d), jnp.bfloat16)])
    return pl.pallas_call(
        _experts_body,
        grid_spec=grid_spec,
        out_shape=jax.ShapeDtypeStruct(xs.shape, jnp.bfloat16),
        compiler_params=pltpu.CompilerParams(dimension_semantics=("arbitrary",),
                                             vmem_limit_bytes=VMEM_LIMIT_BYTES),
        name="moe_experts",
    )(plan["tile_e"], plan["tile_blk"], plan["tile_first"], plan["n_active"],
      xs, w_gate_up, b_gate_up.reshape(N_EXPERTS, 1, 2 * D_FF), w_down, b_down.reshape(N_EXPERTS, 1, d))


def _combine_body(lstart_ref, units_ref, seg_ref, lp_ref,
                  os_ref, posg_ref, x_ref, fw_ref, yp_ref, ys_ref, stage_ref, acc_ref, sem,
                  *, n_prompt_tiles):
    f32, bf16 = jnp.float32, jnp.bfloat16
    i = pl.program_id(0)
    tm = x_ref.shape[0]
    lmax = stage_ref.shape[0]

    @pl.when(i == 0)
    def _():
        stage_ref[...] = jnp.zeros(stage_ref.shape, bf16)

    args = (i, lstart_ref, units_ref, seg_ref, stage_ref, os_ref, sem, False)
    _for_each_segment_copy(*args, lambda cp: cp.start())
    _for_each_segment_copy(*args, lambda cp: cp.wait())

    posg = posg_ref[...]
    pos = [posg[:, k:k + 1] for k in range(TOP_K)]
    gate = [posg[:, TOP_K + k:TOP_K + k + 1] for k in range(TOP_K)]
    acc_ref[...] = x_ref[...]
    for c in range(lmax // MOE_CHUNK):
        @pl.when(c * MOE_CHUNK < lp_ref[i])
        def _():
            r = (lax.broadcasted_iota(jnp.int32, (tm, MOE_CHUNK), 1) + c * MOE_CHUNK).astype(f32)
            w = jnp.zeros((tm, MOE_CHUNK), f32)
            for k in range(TOP_K):
                w = w + jnp.where(pos[k] == r, gate[k], 0.0)
            acc_ref[...] += jnp.dot(w.astype(bf16), stage_ref[c * MOE_CHUNK:(c + 1) * MOE_CHUNK, :],
                                    preferred_element_type=f32)

    y = acc_ref[...]
    out = y * lax.rsqrt(jnp.mean(y * y, axis=-1, keepdims=True) + EPS) * fw_ref[...]

    @pl.when(i < n_prompt_tiles)
    def _():
        yp_ref[...] = out

    @pl.when(i >= n_prompt_tiles)
    def _():
        ys_ref[...] = out


def _moe_combine(os_, posg, x1, final_norm_w, plan, tm, lmax, n_prompt):
    n, d = x1.shape
    nt = n // tm
    n_prompt_tiles = n_prompt // tm
    n_sample_tiles = nt - n_prompt_tiles
    grid_spec = pltpu.PrefetchScalarGridSpec(
        num_scalar_prefetch=4,
        grid=(nt,),
        in_specs=[pl.BlockSpec(memory_space=pl.ANY),
                  pl.BlockSpec((tm, LANES), lambda i, *_: (i, 0)),
                  pl.BlockSpec((tm, d), lambda i, *_: (i, 0)),
                  pl.BlockSpec((1, d), lambda i, *_: (0, 0))],
        out_specs=[pl.BlockSpec((tm, d), lambda i, *_: (jnp.minimum(i, n_prompt_tiles - 1), 0)),
                   pl.BlockSpec((tm, d), lambda i, *_: (jnp.maximum(i - n_prompt_tiles, 0), 0))],
        scratch_shapes=[pltpu.VMEM((lmax, d), jnp.bfloat16),
                        pltpu.VMEM((tm, d), jnp.float32),
                        pltpu.SemaphoreType.DMA(())])
    return pl.pallas_call(
        functools.partial(_combine_body, n_prompt_tiles=n_prompt_tiles),
        grid_spec=grid_spec,
        out_shape=[jax.ShapeDtypeStruct((n_prompt, d), jnp.float32),
                   jax.ShapeDtypeStruct((n_sample_tiles * tm, d), jnp.float32)],
        compiler_params=pltpu.CompilerParams(dimension_semantics=("arbitrary",),
                                             vmem_limit_bytes=VMEM_LIMIT_BYTES),
        name="moe_combine",
    )(plan["lstart"], plan["seg_units"], plan["seg"], plan["lp"],
      os_, posg, x1, final_norm_w.reshape(1, d).astype(jnp.float32))


def _moe_block(x1, n_prompt, norm2_w, router_w, router_b, w_gate_up, b_gate_up, w_down, b_down,
               final_norm_w, tm=MOE_TOKEN_TILE):
    n = x1.shape[0]
    nt, lmax, n_row_tiles = _moe_sizes(n, tm)
    h2, posg, post, cnt3 = _moe_router(x1, norm2_w, router_w, router_b, tm)
    plan = _moe_plan(cnt3[:, 0, :N_EXPERTS], n_row_tiles)
    xs = _moe_dispatch(h2, post, plan, tm, lmax, n_row_tiles)
    os_ = _moe_experts(xs, plan, w_gate_up, b_gate_up, w_down, b_down, n_row_tiles)
    return _moe_combine(os_, posg, x1, final_norm_w, plan, tm, lmax, n_prompt)


def kernel(x_prompt, x_sample, state_ssm, state_conv, state_pool, norm1_w, w_in, conv_w, conv_b, dt_bias,
           A_log, D_skip, ssd_norm_w, pool_w, pool_scale, w_out, norm2_w, router_w, router_b, w_gate_up,
           b_gate_up, w_down, b_down, final_norm_w):
    bp = x_prompt.shape[0]
    dt_ = x_prompt.dtype
    mw = (norm1_w[0], w_in[0], conv_w[0], conv_b[0], dt_bias[0], A_log[0], D_skip[0], ssd_norm_w[0],
          pool_w[0], pool_scale[0], w_out[0])
    xp, s1, c1, p1 = _mix_layer(x_prompt, jnp.zeros((bp, CONV_WIDTH - 1, D_CONV), dt_),
                                jnp.zeros((bp, N_SSD_HEADS, SSD_HEAD_DIM, D_STATE), dt_),
                                jnp.zeros((bp, POOL_HIST, D_POOL), dt_), 0, *mw)
    xs_, s2, c2, p2 = _mix_layer(x_sample, state_conv[0], state_ssm[0], state_pool[0], PAST_LEN, *mw)
    n_prompt = BATCH * SEQ
    x1 = jnp.concatenate([xp.reshape(n_prompt, D_MODEL), xs_.reshape(DEC_BATCH * DEC_SEQ, D_MODEL)], axis=0)
    yp, ys = _moe_block(x1, n_prompt, norm2_w[0], router_w[0], router_b[0], w_gate_up[0], b_gate_up[0],
                        w_down[0], b_down[0], final_norm_w)
    return (yp.reshape(x_prompt.shape), ys.reshape(x_sample.shape),
            s1[None], c1[None], p1[None], s2[None], c2[None], p2[None])
```

```python
import functools
import math
import jax, jax.numpy as jnp
from jax import lax
import numpy as np
from jax.experimental import pallas as pl
from jax.experimental.pallas import tpu as pltpu

D_MODEL = 1024
BATCH = 8
SEQ = 2048
DEC_BATCH = 128
DEC_SEQ = 4
PAST_LEN = 16384

D_MIX = 2 * D_MODEL
D_SSD = 3 * D_MIX // 4
SSD_HEAD_DIM = 64
N_SSD_HEADS = D_SSD // SSD_HEAD_DIM
N_SSD_GROUPS = 4
D_STATE = 128
CONV_WIDTH = 4
SSD_CHUNK = 128
D_CONV = D_SSD + 2 * N_SSD_GROUPS * D_STATE
D_POOL = D_MIX - D_SSD
POOL_WINDOWS = (2, 4, 8, 16)
N_POOL_GROUPS = len(POOL_WINDOWS)
POOL_GROUP_DIM = D_POOL // N_POOL_GROUPS
POOL_HIST = max(POOL_WINDOWS) - 1
D_IN_PROJ = D_SSD + D_CONV + N_SSD_HEADS + D_POOL
N_EXPERTS = 32
TOP_K = 4
D_FF = D_MODEL
SWIGLU_LIMIT = 7.0
SWIGLU_ALPHA = 1.702
EPS = 1e-5

LANES = 128
BF16_SUBLANES = 16
VMEM_LIMIT_BYTES = 48 * 1024 * 1024

MOE_TOKEN_TILE = 512
MOE_SEG_ROWS = BF16_SUBLANES
MOE_ROW_TILE = 256
MOE_CHUNK = 256


def _rmsnorm(x, w):
    xf = x.astype(jnp.float32)
    xf = xf * lax.rsqrt(jnp.mean(xf * xf, axis=-1, keepdims=True) + EPS)
    return (xf * w.astype(jnp.float32)).astype(x.dtype)


def _causal_conv(xbc, buf, conv_w, conv_b):
    L = xbc.shape[1]
    ext = jnp.concatenate([buf.astype(xbc.dtype), xbc], axis=1)
    out = conv_b
    for k in range(CONV_WIDTH):
        out = out + ext[:, k:k + L] * conv_w[k]
    return jax.nn.silu(out), ext[:, L:]


def _ssd(x, dt, A, Bm, Cm, D_skip, init_state):
    b, l, h, p = x.shape
    g = N_SSD_GROUPS
    hg = h // g
    n = Bm.shape[-1]
    q = SSD_CHUNK if l % SSD_CHUNK == 0 else l
    c = l // q
    f32 = jnp.float32
    xf = x.astype(f32).reshape(b, c, q, g, hg, p)
    dtc = dt.reshape(b, c, q, g, hg)
    Bc = Bm.astype(f32).reshape(b, c, q, g, n)
    Cc = Cm.astype(f32).reshape(b, c, q, g, n)
    a_cum = jnp.cumsum(dtc * A.reshape(g, hg), axis=2)
    xdt = xf * dtc[..., None]
    seg = a_cum[:, :, :, None] - a_cum[:, :, None, :]
    mask = jnp.tril(jnp.ones((q, q), dtype=bool))[:, :, None, None]
    decay = jnp.exp(jnp.where(mask, seg, -jnp.inf))
    cb = jnp.einsum('bcign,bcjgn->bcijg', Cc, Bc)
    y_diag = jnp.einsum('bcijg,bcijgh,bcjghp->bcighp', cb, decay, xdt)
    decay_to_end = jnp.exp(a_cum[:, :, -1:] - a_cum)
    chunk_states = jnp.einsum('bcqgn,bcqgh,bcqghp->bcghpn', Bc, decay_to_end, xdt)
    chunk_decay = jnp.exp(a_cum[:, :, -1])
    s0 = init_state.astype(f32).reshape(b, g, hg, p, n)

    def step(s, inp):
        st, dec = inp
        return dec[..., None, None] * s + st, s

    final, s_enter = lax.scan(step, s0, (jnp.moveaxis(chunk_states, 1, 0), jnp.moveaxis(chunk_decay, 1, 0)))
    s_enter = jnp.moveaxis(s_enter, 0, 1)
    y_off = jnp.einsum('bcqgn,bcghpn,bcqgh->bcqghp', Cc, s_enter, jnp.exp(a_cum))
    y = y_diag + y_off + xf * D_skip.astype(f32).reshape(g, hg)[..., None]
    return y.reshape(b, l, h * p).astype(x.dtype), final.reshape(b, h, p, n).astype(init_state.dtype)


def _pool_mix(u, buf, pos0, pool_w, pool_scale):
    b, l, C = u.shape
    ext = jnp.concatenate([buf.astype(u.dtype), u], axis=1)
    csum = jnp.cumsum(ext.astype(jnp.float32), axis=1)
    csum = jnp.concatenate([jnp.zeros((b, 1, C), jnp.float32), csum], axis=1)
    end = csum[:, POOL_HIST + 1:]
    pos = pos0 + jnp.arange(l)
    uf = u.astype(jnp.float32)
    outs = []
    for gi, w in enumerate(POOL_WINDOWS):
        sl = slice(gi * POOL_GROUP_DIM, (gi + 1) * POOL_GROUP_DIM)
        start = csum[:, POOL_HIST + 1 - w:POOL_HIST + 1 - w + l, sl]
        count = jnp.minimum(pos + 1, w).astype(jnp.float32)[None, :, None]
        pooled = (end[..., sl] - start) / count - uf[..., sl]
        outs.append(pooled.astype(u.dtype) @ pool_w[gi])
    out = jnp.concatenate(outs, axis=-1) * pool_scale
    return out, ext[:, l:]


def _mixer_from_proj(z, xbc, dt_raw, u, conv_buf, ssm_state, pool_buf, pos0, conv_w, conv_b, dt_bias, A_log,
                     D_skip, ssd_norm_w, pool_w, pool_scale):
    b, l, _ = z.shape
    xbc, new_conv = _causal_conv(xbc, conv_buf, conv_w, conv_b)
    xs, Bm, Cm = jnp.split(xbc, [D_SSD, D_SSD + N_SSD_GROUPS * D_STATE], axis=-1)
    dt = jax.nn.softplus(dt_raw.astype(jnp.float32) + dt_bias.astype(jnp.float32))
    A = -jnp.exp(A_log.astype(jnp.float32))
    y, new_ssm = _ssd(xs.reshape(b, l, N_SSD_HEADS, SSD_HEAD_DIM), dt, A,
                      Bm.reshape(b, l, N_SSD_GROUPS, D_STATE), Cm.reshape(b, l, N_SSD_GROUPS, D_STATE),
                      D_skip, ssm_state)
    y = _rmsnorm(y * jax.nn.silu(z), ssd_norm_w)
    pool_out, new_pool = _pool_mix(u, pool_buf, pos0, pool_w, pool_scale)
    return jnp.concatenate([y, pool_out], axis=-1), new_ssm, new_conv, new_pool


BLK = SSD_CHUNK
PROJ_ROW_TILE = 256
HIST_ROWS = 16
CONV_TAIL_ROWS = 8
NT_DIMS = (((1,), (1,)), ((), ()))


def _split2(v):
    hi = v.astype(jnp.bfloat16)
    lo = (v - hi.astype(jnp.float32)).astype(jnp.bfloat16)
    return hi, lo


def _dot_sel_left(sel, v, passes):
    out = None
    rem = v
    for p in range(passes):
        part = rem.astype(jnp.bfloat16)
        d = jnp.dot(sel, part, preferred_element_type=jnp.float32)
        out = d if out is None else out + d
        if p + 1 < passes:
            rem = rem - part.astype(jnp.float32)
    return out


def _dot_sel_right(v, sel, passes):
    out = None
    rem = v
    for p in range(passes):
        part = rem.astype(jnp.bfloat16)
        d = jnp.dot(part, sel, preferred_element_type=jnp.float32)
        out = d if out is None else out + d
        if p + 1 < passes:
            rem = rem - part.astype(jnp.float32)
    return out


def _in_proj_body(x_ref, nw_ref, w_ref, z_ref, xbc_ref, dt_ref, u_ref):
    x = x_ref[...]
    h = (x * lax.rsqrt(jnp.mean(x * x, axis=-1, keepdims=True) + EPS) * nw_ref[...]).astype(jnp.bfloat16)
    off = 0
    for ref in (z_ref, xbc_ref, dt_ref, u_ref):
        n = ref.shape[1]
        ref[...] = jnp.dot(h, w_ref[:, off:off + n], preferred_element_type=jnp.float32)
        off += n


def _in_proj(x, norm1_w, w_in):
    n, d = x.shape
    f32, bf16 = jnp.float32, jnp.bfloat16
    s0, s1, s2 = D_SSD, D_SSD + D_CONV, D_SSD + D_CONV + N_SSD_HEADS
    w = jnp.concatenate([w_in[:, :s1], jnp.pad(w_in[:, s1:s2], ((0, 0), (0, LANES - N_SSD_HEADS))),
                         w_in[:, s2:]], axis=1).astype(bf16)
    widths = (D_SSD, D_CONV, LANES, D_POOL)
    tm = PROJ_ROW_TILE
    return pl.pallas_call(
        _in_proj_body,
        grid=(n // tm,),
        in_specs=[pl.BlockSpec((tm, d), lambda i: (i, 0)),
                  pl.BlockSpec((1, d), lambda i: (0, 0)),
                  pl.BlockSpec((d, sum(widths)), lambda i: (0, 0))],
        out_specs=[pl.BlockSpec((tm, wd), lambda i: (i, 0)) for wd in widths],
        out_shape=[jax.ShapeDtypeStruct((n, wd), f32) for wd in widths],
        compiler_params=pltpu.CompilerParams(dimension_semantics=("parallel",),
                                             vmem_limit_bytes=VMEM_LIMIT_BYTES),
        name="in_proj",
    )(x, norm1_w.reshape(1, d).astype(f32), w)


def _out_proj_body(m_ref, w_ref, x_ref, o_ref):
    o_ref[...] = x_ref[...] + jnp.dot(m_ref[...], w_ref[...], preferred_element_type=jnp.float32)


def _out_proj(mix, w_out, x):
    n, d = x.shape
    tm = PROJ_ROW_TILE
    return pl.pallas_call(
        _out_proj_body,
        grid=(n // tm,),
        in_specs=[pl.BlockSpec((tm, D_MIX), lambda i: (i, 0)),
                  pl.BlockSpec((D_MIX, d), lambda i: (0, 0)),
                  pl.BlockSpec((tm, d), lambda i: (i, 0))],
        out_specs=pl.BlockSpec((tm, d), lambda i: (i, 0)),
        out_shape=jax.ShapeDtypeStruct((n, d), jnp.float32),
        compiler_params=pltpu.CompilerParams(dimension_semantics=("parallel",),
                                             vmem_limit_bytes=VMEM_LIMIT_BYTES),
        name="out_proj",
    )(mix, w_out.astype(jnp.bfloat16), x)


def _mixer_constants():
    bf16 = jnp.bfloat16
    h = np.arange(LANES)[:, None]
    ch = np.arange(D_SSD)[None, :]
    expand = (ch // SSD_HEAD_DIM == h).astype(np.float32)
    i = np.arange(BLK)[:, None]
    j = np.arange(BLK)[None, :]
    causal = (j <= i).astype(np.float32)
    jh = np.arange(HIST_ROWS)[None, :]
    pcur = np.stack([((j <= i) & (i - j < w)) for w in POOL_WINDOWS]).astype(np.float32)
    phist = np.stack([(i + HIST_ROWS - jh < w) for w in POOL_WINDOWS]).astype(np.float32)
    return dict(expand=jnp.asarray(expand, bf16), expand_t=jnp.asarray(expand.T, bf16),
                causal=jnp.asarray(causal, bf16), pcur=jnp.asarray(pcur, bf16),
                phist=jnp.asarray(phist, bf16))


def _softplus(x):
    return jnp.maximum(x, 0.0) + jnp.log(1.0 + jnp.exp(-jnp.abs(x)))


def _conv_silu(ext_ref, cw_ref, cb_ref, first_row):
    acc = cb_ref[...] + cw_ref[0:1, :] * ext_ref[pl.ds(first_row, BLK), :]
    for k in range(1, CONV_WIDTH):
        acc = acc + cw_ref[k:k + 1, :] * ext_ref[pl.ds(first_row + k, BLK), :]
    return acc * jax.nn.sigmoid(acc)


def _ssd_intra(xbc_c, dt_raw, dtb_ref, alog_ref, causal_bf, expand_ref):
    f32 = jnp.float32
    xs = xbc_c[:, :D_SSD]
    bm = xbc_c[:, D_SSD:D_SSD + N_SSD_GROUPS * D_STATE]
    cm = xbc_c[:, D_SSD + N_SSD_GROUPS * D_STATE:]
    dt = _softplus(dt_raw + dtb_ref[...])
    a = dt * (-jnp.exp(alog_ref[...]))
    a_cum = _dot_sel_left(causal_bf, a, 3)
    dt_x = _dot_sel_right(dt, expand_ref[...], 2)
    return xs, bm, cm, dt, a_cum, xs * dt_x


def _ssd_diag_group(g, cb, a_cum, a_cum_t, keep, xdt):
    f32, bf16 = jnp.float32, jnp.bfloat16
    hg = N_SSD_HEADS // N_SSD_GROUPS
    lane = lax.broadcasted_iota(jnp.int32, (BLK, LANES), 1)
    first_head = lane < SSD_HEAD_DIM
    neg = jnp.float32(-jnp.inf)
    outs = []
    for pr in range(hg * SSD_HEAD_DIM // LANES):
        h1 = g * hg + 2 * pr
        blk = (g * hg * SSD_HEAD_DIM) // LANES + pr
        xp = xdt[:, blk * LANES:(blk + 1) * LANES]
        x1 = jnp.where(first_head, xp, 0.0).astype(bf16)
        x2 = jnp.where(first_head, 0.0, xp).astype(bf16)
        m1 = (cb * jnp.exp(jnp.where(keep, a_cum[:, h1:h1 + 1] - a_cum_t[h1:h1 + 1, :], neg))).astype(bf16)
        m2 = (cb * jnp.exp(jnp.where(keep, a_cum[:, h1 + 1:h1 + 2] - a_cum_t[h1 + 1:h1 + 2, :], neg))).astype(bf16)
        outs.append(jnp.dot(m1, x1, preferred_element_type=f32) + jnp.dot(m2, x2, preferred_element_type=f32))
    return jnp.concatenate(outs, axis=1)


def _gated_norm(y, z, nw_ref):
    yg = y * (z * jax.nn.sigmoid(z))
    return yg * lax.rsqrt(jnp.mean(yg * yg, axis=-1, keepdims=True) + EPS) * nw_ref[...]


def _prompt_mixer_body(z_ref, xbc_ref, dt_ref, u_ref, cw_ref, cb_ref, dtb_ref, alog_ref, dskip_ref, nw_ref,
                       pw_ref, ps_ref, causal_ref, expand_ref, expand_t_ref, pcur_ref, phist_ref,
                       mix_ref, ssm_ref, ext_ref, pool_tail_ref, state_ref):
    f32, bf16 = jnp.float32, jnp.bfloat16
    c = pl.program_id(1)
    gw = D_SSD // N_SSD_GROUPS

    @pl.when(c == 0)
    def _():
        ext_ref[0:CONV_TAIL_ROWS, :] = jnp.zeros((CONV_TAIL_ROWS, D_CONV), f32)
        pool_tail_ref[...] = jnp.zeros(pool_tail_ref.shape, f32)
        state_ref[...] = jnp.zeros(state_ref.shape, f32)

    ext_ref[CONV_TAIL_ROWS:CONV_TAIL_ROWS + BLK, :] = xbc_ref[...]
    xbc_c = _conv_silu(ext_ref, cw_ref, cb_ref, CONV_TAIL_ROWS - (CONV_WIDTH - 1))
    ext_ref[0:CONV_TAIL_ROWS, :] = xbc_ref[BLK - CONV_TAIL_ROWS:BLK, :]

    causal_bf = causal_ref[...]
    keep = causal_bf > 0
    xs, bm, cm, dt, a_cum, xdt = _ssd_intra(xbc_c, dt_ref[...], dtb_ref, alog_ref, causal_bf, expand_ref)
    a_cum_t = jnp.transpose(a_cum)
    a_tot = a_cum[BLK - 1:BLK, :]
    ea_x = _dot_sel_right(jnp.exp(a_cum), expand_ref[...], 2)
    dte_x = _dot_sel_right(jnp.exp(a_tot - a_cum), expand_ref[...], 2)
    cd_t = jnp.broadcast_to(jnp.exp(a_cum_t[:, BLK - 1:BLK]), (LANES, LANES))
    cd_col = _dot_sel_left(expand_t_ref[...], cd_t, 2)[:, 0:1]

    y_parts = []
    for g in range(N_SSD_GROUPS):
        cg = cm[:, g * D_STATE:(g + 1) * D_STATE].astype(bf16)
        bg = bm[:, g * D_STATE:(g + 1) * D_STATE].astype(bf16)
        cb = lax.dot_general(cg, bg, NT_DIMS, preferred_element_type=f32)
        y_diag = _ssd_diag_group(g, cb, a_cum, a_cum_t, keep, xdt)
        sg = state_ref[g * gw:(g + 1) * gw, :]
        y_off = lax.dot_general(cg, sg.astype(bf16), NT_DIMS, preferred_element_type=f32)
        y_parts.append(y_diag + y_off * ea_x[:, g * gw:(g + 1) * gw])
        xdte_t = jnp.transpose(xdt[:, g * gw:(g + 1) * gw] * dte_x[:, g * gw:(g + 1) * gw]).astype(bf16)
        state_ref[g * gw:(g + 1) * gw, :] = (sg * cd_col[g * gw:(g + 1) * gw, :]
                                             + jnp.dot(xdte_t, bg, preferred_element_type=f32))
    y = jnp.concatenate(y_parts, axis=1) + xs * dskip_ref[...]
    mix_ref[:, 0:D_SSD] = _gated_norm(y, z_ref[...], nw_ref).astype(bf16)

    @pl.when(c == pl.num_programs(1) - 1)
    def _():
        ssm_ref[0] = state_ref[...].reshape(N_SSD_HEADS, SSD_HEAD_DIM, D_STATE)

    u = u_ref[...]
    tail = pool_tail_ref[...]
    pos = (c * BLK + lax.broadcasted_iota(jnp.int32, (BLK, 1), 0) + 1).astype(f32)
    for gi, w in enumerate(POOL_WINDOWS):
        sl = slice(gi * POOL_GROUP_DIM, (gi + 1) * POOL_GROUP_DIM)
        ug = u[:, sl]
        wsum = _dot_sel_left(pcur_ref[gi], ug, 2) + _dot_sel_left(phist_ref[gi], tail[:, sl], 2)
        pooled = wsum / jnp.minimum(pos, jnp.float32(w)) - ug
        po = jnp.dot(pooled.astype(bf16), pw_ref[gi], preferred_element_type=f32) * ps_ref[:, sl]
        mix_ref[:, D_SSD + gi * POOL_GROUP_DIM:D_SSD + (gi + 1) * POOL_GROUP_DIM] = po.astype(bf16)
    pool_tail_ref[...] = u_ref[BLK - HIST_ROWS:BLK, :]


def _prompt_mixer(z, xbc, dt, u, n_seq, conv_w, conv_b, dt_bias, A_log, D_skip, ssd_norm_w, pool_w, pool_scale):
    f32, bf16 = jnp.float32, jnp.bfloat16
    n = z.shape[0]
    n_blk = n // n_seq // BLK
    k = _mixer_constants()

    def row_blk(width):
        return pl.BlockSpec((BLK, width), lambda b, c: (b * n_blk + c, 0))

    def const(shape):
        return pl.BlockSpec(shape, lambda b, c: (0,) * len(shape))

    pad_h = (0, LANES - N_SSD_HEADS)
    return pl.pallas_call(
        _prompt_mixer_body,
        grid=(n_seq, n_blk),
        in_specs=[row_blk(D_SSD), row_blk(D_CONV), row_blk(LANES), row_blk(D_POOL),
                  const((CONV_WIDTH, D_CONV)), const((1, D_CONV)), const((1, LANES)), const((1, LANES)),
                  const((1, D_SSD)), const((1, D_SSD)),
                  const((N_POOL_GROUPS, POOL_GROUP_DIM, POOL_GROUP_DIM)), const((1, D_POOL)),
                  const((BLK, BLK)), const((LANES, D_SSD)), const((D_SSD, LANES)),
                  const((N_POOL_GROUPS, BLK, BLK)), const((N_POOL_GROUPS, BLK, HIST_ROWS))],
        out_specs=[pl.BlockSpec((BLK, D_MIX), lambda b, c: (b * n_blk + c, 0)),
                   pl.BlockSpec((1, N_SSD_HEADS, SSD_HEAD_DIM, D_STATE), lambda b, c: (b, 0, 0, 0))],
        out_shape=[jax.ShapeDtypeStruct((n, D_MIX), bf16),
                   jax.ShapeDtypeStruct((n_seq, N_SSD_HEADS, SSD_HEAD_DIM, D_STATE), f32)],
        scratch_shapes=[pltpu.VMEM((CONV_TAIL_ROWS + BLK, D_CONV), f32),
                        pltpu.VMEM((HIST_ROWS, D_POOL), f32),
                        pltpu.VMEM((D_SSD, D_STATE), f32)],
        compiler_params=pltpu.CompilerParams(dimension_semantics=("parallel", "arbitrary"),
                                             vmem_limit_bytes=VMEM_LIMIT_BYTES),
        name="prompt_mixer",
    )(z, xbc, dt, u, conv_w.astype(f32), conv_b.reshape(1, D_CONV).astype(f32),
      jnp.pad(dt_bias.astype(f32), pad_h).reshape(1, LANES), jnp.pad(A_log.astype(f32), pad_h).reshape(1, LANES),
      jnp.repeat(D_skip.astype(f32), SSD_HEAD_DIM).reshape(1, D_SSD), ssd_norm_w.reshape(1, D_SSD).astype(f32),
      pool_w.astype(bf16), pool_scale.reshape(1, D_POOL).astype(f32),
      k["causal"], k["expand"], k["expand_t"], k["pcur"], k["phist"])


def _moe_sizes(n_tokens, tm):
    nt = n_tokens // tm
    lmax = -(-(TOP_K * tm + N_EXPERTS * (MOE_SEG_ROWS - 1)) // MOE_CHUNK) * MOE_CHUNK
    rows = TOP_K * n_tokens + nt * N_EXPERTS * (MOE_SEG_ROWS - 1) + N_EXPERTS * (MOE_ROW_TILE - 1)
    n_row_tiles = -(-rows // MOE_ROW_TILE)
    return nt, lmax, n_row_tiles


def _router_body(x_ref, nw_ref, rwh_ref, rwl_ref, rb_ref, h_ref, posg_ref, post_ref, cnt_ref):
    f32, bf16 = jnp.float32, jnp.bfloat16
    tm = x_ref.shape[0]
    x = x_ref[...]
    h = x * lax.rsqrt(jnp.mean(x * x, axis=-1, keepdims=True) + EPS) * nw_ref[...]
    h_hi = h.astype(bf16)
    h_ref[...] = h_hi
    h_lo = (h - h_hi.astype(f32)).astype(bf16)
    wh = rwh_ref[...]
    logits = (jnp.dot(h_hi, wh, preferred_element_type=f32)
              + jnp.dot(h_lo, wh, preferred_element_type=f32)
              + jnp.dot(h_hi, rwl_ref[...], preferred_element_type=f32)) + rb_ref[...]
    lane = lax.broadcasted_iota(jnp.int32, (tm, LANES), 1)
    lanef = lane.astype(f32)
    neg = jnp.float32(-jnp.inf)
    l = jnp.where(lane < N_EXPERTS, logits, neg)
    sels, vals = [], []
    for _ in range(TOP_K):
        m = jnp.max(l, axis=1, keepdims=True)
        idx = jnp.min(jnp.where(l == m, lanef, jnp.float32(LANES)), axis=1, keepdims=True)
        sel = lanef == idx
        l = jnp.where(sel, neg, l)
        sels.append(sel)
        vals.append(m)
    exps = [jnp.exp(v - vals[0]) for v in vals]
    denom = exps[0] + exps[1] + exps[2] + exps[3]
    gates = [e / denom for e in exps]
    chosen = jnp.where(sels[0] | sels[1] | sels[2] | sels[3], 1.0, 0.0).astype(f32)
    row = lax.broadcasted_iota(jnp.int32, (tm, tm), 0)
    col = lax.broadcasted_iota(jnp.int32, (tm, tm), 1)
    lower = jnp.where(col < row, 1.0, 0.0).astype(bf16)
    rank = jnp.dot(lower, chosen.astype(bf16), preferred_element_type=f32)
    cnt = jnp.sum(chosen, axis=0, keepdims=True)
    seg_units = jnp.floor((cnt + (MOE_SEG_ROWS - 1)) * (1.0 / MOE_SEG_ROWS))
    r2 = lax.broadcasted_iota(jnp.int32, (LANES, LANES), 0)
    c2 = lax.broadcasted_iota(jnp.int32, (LANES, LANES), 1)
    upper = jnp.where(r2 < c2, 1.0, 0.0).astype(bf16)
    lstart = jnp.dot(jnp.broadcast_to(seg_units, (8, LANES)).astype(bf16), upper,
                     preferred_element_type=f32)[0:1, :] * MOE_SEG_ROWS
    posmat = lstart + rank
    posg = jnp.zeros((tm, LANES), f32)
    for k in range(TOP_K):
        pos_k = jnp.sum(jnp.where(sels[k], posmat, 0.0), axis=1, keepdims=True)
        posg = posg + jnp.where(lane == k, pos_k, 0.0) + jnp.where(lane == TOP_K + k, gates[k], 0.0)
    posg_ref[...] = posg
    post_ref[...] = jnp.transpose(posg)[0:8, :]
    cnt_ref[0] = jnp.broadcast_to(cnt, (8, LANES)).astype(jnp.int32)


def _moe_router(x1, norm2_w, router_w, router_b, tm):
    n, d = x1.shape
    nt = n // tm
    f32, bf16 = jnp.float32, jnp.bfloat16
    rw = jnp.pad(router_w.astype(f32), ((0, 0), (0, LANES - N_EXPERTS)))
    rw_hi = rw.astype(bf16)
    rw_lo = (rw - rw_hi.astype(f32)).astype(bf16)
    rb = jnp.pad(router_b.astype(f32), (0, LANES - N_EXPERTS)).reshape(1, LANES)
    return pl.pallas_call(
        _router_body,
        grid=(nt,),
        in_specs=[pl.BlockSpec((tm, d), lambda i: (i, 0)),
                  pl.BlockSpec((1, d), lambda i: (0, 0)),
                  pl.BlockSpec((d, LANES), lambda i: (0, 0)),
                  pl.BlockSpec((d, LANES), lambda i: (0, 0)),
                  pl.BlockSpec((1, LANES), lambda i: (0, 0))],
        out_specs=[pl.BlockSpec((tm, d), lambda i: (i, 0)),
                   pl.BlockSpec((tm, LANES), lambda i: (i, 0)),
                   pl.BlockSpec((8, tm), lambda i: (0, i)),
                   pl.BlockSpec((1, 8, LANES), lambda i: (i, 0, 0))],
        out_shape=[jax.ShapeDtypeStruct((n, d), bf16),
                   jax.ShapeDtypeStruct((n, LANES), f32),
                   jax.ShapeDtypeStruct((8, n), f32),
                   jax.ShapeDtypeStruct((nt, 8, LANES), jnp.int32)],
        compiler_params=pltpu.CompilerParams(dimension_semantics=("parallel",),
                                             vmem_limit_bytes=VMEM_LIMIT_BYTES),
        name="moe_router",
    )(x1, norm2_w.reshape(1, d).astype(f32), rw_hi, rw_lo, rb)


def _moe_plan(cnt, n_row_tiles):
    i32 = jnp.int32
    pad = (cnt + (MOE_SEG_ROWS - 1)) // MOE_SEG_ROWS * MOE_SEG_ROWS
    lstart = jnp.cumsum(pad, axis=1) - pad
    lp = jnp.sum(pad, axis=1)
    tot = jnp.sum(pad, axis=0)
    reg = (tot + (MOE_ROW_TILE - 1)) // MOE_ROW_TILE * MOE_ROW_TILE
    reg_end = jnp.cumsum(reg)
    estart = reg_end - reg
    seg = estart[None, :] + jnp.cumsum(pad, axis=0) - pad
    tiles_end = reg_end // MOE_ROW_TILE
    n_active = tiles_end[-1]
    j = jnp.arange(n_row_tiles, dtype=i32)
    jc = jnp.minimum(j, n_active - 1)
    tile_e = jnp.searchsorted(tiles_end, jc, side='right').astype(i32)
    prev_e = jnp.concatenate([jnp.full((1,), -1, i32), tile_e[:-1]])
    tile_first = ((tile_e != prev_e) & (j < n_active)).astype(i32)
    return dict(
        lstart=lstart.reshape(-1).astype(i32), seg_units=(pad // MOE_SEG_ROWS).reshape(-1).astype(i32),
        seg=seg.reshape(-1).astype(i32), lp=lp.astype(i32),
        tail_start=(estart + tot).astype(i32), tail_units=((reg - tot) // MOE_SEG_ROWS).astype(i32),
        tile_e=tile_e, tile_blk=jc.astype(i32), tile_first=tile_first,
        n_active=n_active.reshape(1).astype(i32))


def _for_each_segment_copy(i, lstart_ref, units_ref, seg_ref, local_ref, global_ref, sem, to_global, fn):
    def per_expert(e, carry):
        k = i * N_EXPERTS + e
        ls, gs = lstart_ref[k], seg_ref[k]

        def per_piece(j, c):
            lo = pl.multiple_of(ls + j * MOE_SEG_ROWS, MOE_SEG_ROWS)
            go = pl.multiple_of(gs + j * MOE_SEG_ROWS, MOE_SEG_ROWS)
            loc = local_ref.at[pl.ds(lo, MOE_SEG_ROWS)]
            glo = global_ref.at[pl.ds(go, MOE_SEG_ROWS)]
            fn(pltpu.make_async_copy(loc, glo, sem) if to_global else pltpu.make_async_copy(glo, loc, sem))
            return c
        return lax.fori_loop(0, units_ref[k], per_piece, carry)
    lax.fori_loop(0, N_EXPERTS, per_expert, 0)


def _dispatch_body(lstart_ref, units_ref, seg_ref, lp_ref, tail_start_ref, tail_units_ref, n_active_ref,
                   h_ref, post_ref, xs_ref, stage_ref, zero_ref, sem):
    f32, bf16 = jnp.float32, jnp.bfloat16
    i = pl.program_id(0)
    tm = h_ref.shape[0]
    lmax = stage_ref.shape[0]
    n_row_tiles = xs_ref.shape[0] // MOE_ROW_TILE

    def for_each_fill_copy(fn):
        def per_expert(e, carry):
            ts = tail_start_ref[e]

            def per_piece(j, c):
                go = pl.multiple_of(ts + j * MOE_SEG_ROWS, MOE_SEG_ROWS)
                fn(pltpu.make_async_copy(zero_ref.at[pl.ds(0, MOE_SEG_ROWS)],
                                         xs_ref.at[pl.ds(go, MOE_SEG_ROWS)], sem))
                return c
            return lax.fori_loop(0, tail_units_ref[e], per_piece, carry)
        lax.fori_loop(0, N_EXPERTS, per_expert, 0)

        def per_unused_tile(j, c):
            go = pl.multiple_of(j * MOE_ROW_TILE, MOE_ROW_TILE)
            fn(pltpu.make_async_copy(zero_ref, xs_ref.at[pl.ds(go, MOE_ROW_TILE)], sem))
            return c
        lax.fori_loop(n_active_ref[0], n_row_tiles, per_unused_tile, 0)

    @pl.when(i == 0)
    def _():
        zero_ref[...] = jnp.zeros(zero_ref.shape, bf16)
        for_each_fill_copy(lambda cp: cp.start())
        for_each_fill_copy(lambda cp: cp.wait())

    pos = post_ref[0:TOP_K, :]
    h = h_ref[...]
    for c in range(lmax // MOE_CHUNK):
        @pl.when(c * MOE_CHUNK < lp_ref[i])
        def _():
            r = (lax.broadcasted_iota(jnp.int32, (MOE_CHUNK, tm), 0) + c * MOE_CHUNK).astype(f32)
            hit = (pos[0:1, :] == r) | (pos[1:2, :] == r) | (pos[2:3, :] == r) | (pos[3:4, :] == r)
            sel = jnp.where(hit, 1.0, 0.0).astype(bf16)
            stage_ref[c * MOE_CHUNK:(c + 1) * MOE_CHUNK, :] = jnp.dot(
                sel, h, preferred_element_type=f32).astype(bf16)

    args = (i, lstart_ref, units_ref, seg_ref, stage_ref, xs_ref, sem, True)
    _for_each_segment_copy(*args, lambda cp: cp.start())
    _for_each_segment_copy(*args, lambda cp: cp.wait())


def _moe_dispatch(h2, post, plan, tm, lmax, n_row_tiles):
    n, d = h2.shape
    nt = n // tm
    grid_spec = pltpu.PrefetchScalarGridSpec(
        num_scalar_prefetch=7,
        grid=(nt,),
        in_specs=[pl.BlockSpec((tm, d), lambda i, *_: (i, 0)),
                  pl.BlockSpec((8, tm), lambda i, *_: (0, i))],
        out_specs=pl.BlockSpec(memory_space=pl.ANY),
        scratch_shapes=[pltpu.VMEM((lmax, d), jnp.bfloat16),
                        pltpu.VMEM((MOE_ROW_TILE, d), jnp.bfloat16),
                        pltpu.SemaphoreType.DMA(())])
    return pl.pallas_call(
        _dispatch_body,
        grid_spec=grid_spec,
        out_shape=jax.ShapeDtypeStruct((n_row_tiles * MOE_ROW_TILE, d), jnp.bfloat16),
        compiler_params=pltpu.CompilerParams(dimension_semantics=("arbitrary",),
                                             vmem_limit_bytes=VMEM_LIMIT_BYTES),
        name="moe_dispatch",
    )(plan["lstart"], plan["seg_units"], plan["seg"], plan["lp"], plan["tail_start"], plan["tail_units"],
      plan["n_active"], h2, post)


def _experts_body(tile_e_ref, tile_blk_ref, tile_first_ref, n_active_ref,
                  xs_ref, wgu_ref, bgu_ref, wd_ref, bd_ref, os_ref, wgu_bf, wd_bf):
    f32, bf16 = jnp.float32, jnp.bfloat16
    j = pl.program_id(0)

    @pl.when(j < n_active_ref[0])
    def _():
        @pl.when(tile_first_ref[j] == 1)
        def _():
            wgu_bf[...] = wgu_ref[0].astype(bf16)
            wd_bf[...] = wd_ref[0].astype(bf16)

        x = xs_ref[...]
        gu = jnp.dot(x, wgu_bf[...], preferred_element_type=f32) + bgu_ref[0]
        gate = jnp.minimum(gu[:, :D_FF], SWIGLU_LIMIT)
        up = jnp.clip(gu[:, D_FF:], -SWIGLU_LIMIT, SWIGLU_LIMIT)
        act = (up + 1.0) * (gate * jax.nn.sigmoid(SWIGLU_ALPHA * gate))
        out = jnp.dot(act.astype(bf16), wd_bf[...], preferred_element_type=f32) + bd_ref[0]
        os_ref[...] = out.astype(bf16)

    @pl.when(j >= n_active_ref[0])
    def _():
        os_ref[...] = jnp.zeros(os_ref.shape, bf16)


def _moe_experts(xs, plan, w_gate_up, b_gate_up, w_down, b_down, n_row_tiles):
    d = xs.shape[1]
    grid_spec = pltpu.PrefetchScalarGridSpec(
        num_scalar_prefetch=4,
        grid=(n_row_tiles,),
        in_specs=[pl.BlockSpec((MOE_ROW_TILE, d), lambda j, te, tb, tf, na: (tb[j], 0)),
                  pl.BlockSpec((1, d, 2 * D_FF), lambda j, te, tb, tf, na: (te[j], 0, 0)),
                  pl.BlockSpec((1, 1, 2 * D_FF), lambda j, te, tb, tf, na: (te[j], 0, 0)),
                  pl.BlockSpec((1, D_FF, d), lambda j, te, tb, tf, na: (te[j], 0, 0)),
                  pl.BlockSpec((1, 1, d), lambda j, te, tb, tf, na: (te[j], 0, 0))],
        out_specs=pl.BlockSpec((MOE_ROW_TILE, d), lambda j, te, tb, tf, na: (j, 0)),
        scratch_shapes=[pltpu.VMEM((d, 2 * D_FF), jnp.bfloat16),
                        pltpu.VMEM((D_FF, d), jnp.bfloat16)])
    return pl.pallas_call(
        _experts_body,
        grid_spec=grid_spec,
        out_shape=jax.ShapeDtypeStruct(xs.shape, jnp.bfloat16),
        compiler_params=pltpu.CompilerParams(dimension_semantics=("arbitrary",),
                                             vmem_limit_bytes=VMEM_LIMIT_BYTES),
        name="moe_experts",
    )(plan["tile_e"], plan["tile_blk"], plan["tile_first"], plan["n_active"],
      xs, w_gate_up, b_gate_up.reshape(N_EXPERTS, 1, 2 * D_FF), w_down, b_down.reshape(N_EXPERTS, 1, d))


def _combine_body(lstart_ref, units_ref, seg_ref, lp_ref,
                  os_ref, posg_ref, x_ref, fw_ref, yp_ref, ys_ref, stage_ref, acc_ref, sem,
                  *, n_prompt_tiles):
    f32, bf16 = jnp.float32, jnp.bfloat16
    i = pl.program_id(0)
    tm = x_ref.shape[0]
    lmax = stage_ref.shape[0]

    @pl.when(i == 0)
    def _():
        stage_ref[...] = jnp.zeros(stage_ref.shape, bf16)

    args = (i, lstart_ref, units_ref, seg_ref, stage_ref, os_ref, sem, False)
    _for_each_segment_copy(*args, lambda cp: cp.start())
    _for_each_segment_copy(*args, lambda cp: cp.wait())

    posg = posg_ref[...]
    pos = [posg[:, k:k + 1] for k in range(TOP_K)]
    gate = [posg[:, TOP_K + k:TOP_K + k + 1] for k in range(TOP_K)]
    acc_ref[...] = x_ref[...]
    for c in range(lmax // MOE_CHUNK):
        @pl.when(c * MOE_CHUNK < lp_ref[i])
        def _():
            r = (lax.broadcasted_iota(jnp.int32, (tm, MOE_CHUNK), 1) + c * MOE_CHUNK).astype(f32)
            w = jnp.zeros((tm, MOE_CHUNK), f32)
            for k in range(TOP_K):
                w = w + jnp.where(pos[k] == r, gate[k], 0.0)
            acc_ref[...] += jnp.dot(w.astype(bf16), stage_ref[c * MOE_CHUNK:(c + 1) * MOE_CHUNK, :],
                                    preferred_element_type=f32)

    y = acc_ref[...]
    out = y * lax.rsqrt(jnp.mean(y * y, axis=-1, keepdims=True) + EPS) * fw_ref[...]

    @pl.when(i < n_prompt_tiles)
    def _():
        yp_ref[...] = out

    @pl.when(i >= n_prompt_tiles)
    def _():
        ys_ref[...] = out


def _moe_combine(os_, posg, x1, final_norm_w, plan, tm, lmax, n_prompt):
    n, d = x1.shape
    nt = n // tm
    n_prompt_tiles = n_prompt // tm
    n_sample_tiles = nt - n_prompt_tiles
    grid_spec = pltpu.PrefetchScalarGridSpec(
        num_scalar_prefetch=4,
        grid=(nt,),
        in_specs=[pl.BlockSpec(memory_space=pl.ANY),
                  pl.BlockSpec((tm, LANES), lambda i, *_: (i, 0)),
                  pl.BlockSpec((tm, d), lambda i, *_: (i, 0)),
                  pl.BlockSpec((1, d), lambda i, *_: (0, 0))],
        out_specs=[pl.BlockSpec((tm, d), lambda i, *_: (jnp.minimum(i, n_prompt_tiles - 1), 0)),
                   pl.BlockSpec((tm, d), lambda i, *_: (jnp.maximum(i - n_prompt_tiles, 0), 0))],
        scratch_shapes=[pltpu.VMEM((lmax, d), jnp.bfloat16),
                        pltpu.VMEM((tm, d), jnp.float32),
                        pltpu.SemaphoreType.DMA(())])
    return pl.pallas_call(
        functools.partial(_combine_body, n_prompt_tiles=n_prompt_tiles),
        grid_spec=grid_spec,
        out_shape=[jax.ShapeDtypeStruct((n_prompt, d), jnp.float32),
                   jax.ShapeDtypeStruct((n_sample_tiles * tm, d), jnp.float32)],
        compiler_params=pltpu.CompilerParams(dimension_semantics=("arbitrary",),
                                             vmem_limit_bytes=VMEM_LIMIT_BYTES),
        name="moe_combine",
    )(plan["lstart"], plan["seg_units"], plan["seg"], plan["lp"],
      os_, posg, x1, final_norm_w.reshape(1, d).astype(jnp.float32))


def _moe_block(x1, n_prompt, norm2_w, router_w, router_b, w_gate_up, b_gate_up, w_down, b_down,
               final_norm_w, tm=MOE_TOKEN_TILE):
    n = x1.shape[0]
    nt, lmax, n_row_tiles = _moe_sizes(n, tm)
    h2, posg, post, cnt3 = _moe_router(x1, norm2_w, router_w, router_b, tm)
    plan = _moe_plan(cnt3[:, 0, :N_EXPERTS], n_row_tiles)
    xs = _moe_dispatch(h2, post, plan, tm, lmax, n_row_tiles)
    os_ = _moe_experts(xs, plan, w_gate_up, b_gate_up, w_down, b_down, n_row_tiles)
    return _moe_combine(os_, posg, x1, final_norm_w, plan, tm, lmax, n_prompt)


def kernel(x_prompt, x_sample, state_ssm, state_conv, state_pool, norm1_w, w_in, conv_w, conv_b, dt_bias,
           A_log, D_skip, ssd_norm_w, pool_w, pool_scale, w_out, norm2_w, router_w, router_b, w_gate_up,
           b_gate_up, w_down, b_down, final_norm_w):
    n_prompt = BATCH * SEQ
    n_sample = DEC_BATCH * DEC_SEQ
    x = jnp.concatenate([x_prompt.reshape(n_prompt, D_MODEL), x_sample.reshape(n_sample, D_MODEL)], axis=0)
    z, xbc, dt_raw, u = _in_proj(x, norm1_w[0], w_in[0])
    mp = (conv_w[0], conv_b[0], dt_bias[0], A_log[0], D_skip[0], ssd_norm_w[0], pool_w[0], pool_scale[0])
    mix_p, s1 = _prompt_mixer(z[:n_prompt], xbc[:n_prompt], dt_raw[:n_prompt], u[:n_prompt], BATCH, *mp)
    xbc_p = xbc[:n_prompt].reshape(BATCH, SEQ, D_CONV)
    u_p = u[:n_prompt].reshape(BATCH, SEQ, D_POOL)
    c1 = xbc_p[:, SEQ - (CONV_WIDTH - 1):]
    p1 = u_p[:, SEQ - POOL_HIST:]

    def rows_s(a):
        return a[n_prompt:].reshape(DEC_BATCH, DEC_SEQ, a.shape[1])
    mix_s, s2, c2, p2 = _mixer_from_proj(rows_s(z), rows_s(xbc), rows_s(dt_raw)[..., :N_SSD_HEADS], rows_s(u),
                                         state_conv[0], state_ssm[0], state_pool[0], PAST_LEN, *mp)
    mix = jnp.concatenate([mix_p, mix_s.reshape(n_sample, D_MIX).astype(jnp.bfloat16)], axis=0)
    x1 = _out_proj(mix, w_out[0], x)
    yp, ys = _moe_block(x1, n_prompt, norm2_w[0], router_w[0], router_b[0], w_gate_up[0], b_gate_up[0],
                        w_down[0], b_down[0], final_norm_w)
    return (yp.reshape(x_prompt.shape), ys.reshape(x_sample.shape),
            s1[None], c1[None], p1[None], s2[None], c2[None], p2[None])
```

```python
import functools
import math
import jax, jax.numpy as jnp
from jax import lax
import numpy as np
from jax.experimental import pallas as pl
from jax.experimental.pallas import tpu as pltpu

D_MODEL = 1024
BATCH = 8
SEQ = 2048
DEC_BATCH = 128
DEC_SEQ = 4
PAST_LEN = 16384

D_MIX = 2 * D_MODEL
D_SSD = 3 * D_MIX // 4
SSD_HEAD_DIM = 64
N_SSD_HEADS = D_SSD // SSD_HEAD_DIM
N_SSD_GROUPS = 4
D_STATE = 128
CONV_WIDTH = 4
SSD_CHUNK = 128
D_CONV = D_SSD + 2 * N_SSD_GROUPS * D_STATE
D_POOL = D_MIX - D_SSD
POOL_WINDOWS = (2, 4, 8, 16)
N_POOL_GROUPS = len(POOL_WINDOWS)
POOL_GROUP_DIM = D_POOL // N_POOL_GROUPS
POOL_HIST = max(POOL_WINDOWS) - 1
D_IN_PROJ = D_SSD + D_CONV + N_SSD_HEADS + D_POOL
N_EXPERTS = 32
TOP_K = 4
D_FF = D_MODEL
SWIGLU_LIMIT = 7.0
SWIGLU_ALPHA = 1.702
EPS = 1e-5

LANES = 128
BF16_SUBLANES = 16
VMEM_LIMIT_BYTES = 48 * 1024 * 1024

MOE_TOKEN_TILE = 512
MOE_SEG_ROWS = BF16_SUBLANES
MOE_ROW_TILE = 256
MOE_CHUNK = 256


BLK = SSD_CHUNK
PROJ_ROW_TILE = 256
HIST_ROWS = 16
CONV_TAIL_ROWS = 8
NT_DIMS = (((1,), (1,)), ((), ()))


def _split2(v):
    hi = v.astype(jnp.bfloat16)
    lo = (v - hi.astype(jnp.float32)).astype(jnp.bfloat16)
    return hi, lo


def _dot_sel_left(sel, v, passes):
    out = None
    rem = v
    for p in range(passes):
        part = rem.astype(jnp.bfloat16)
        d = jnp.dot(sel, part, preferred_element_type=jnp.float32)
        out = d if out is None else out + d
        if p + 1 < passes:
            rem = rem - part.astype(jnp.float32)
    return out


def _dot_sel_right(v, sel, passes):
    out = None
    rem = v
    for p in range(passes):
        part = rem.astype(jnp.bfloat16)
        d = jnp.dot(part, sel, preferred_element_type=jnp.float32)
        out = d if out is None else out + d
        if p + 1 < passes:
            rem = rem - part.astype(jnp.float32)
    return out


def _two_part_specs(n_first, n_second, tm, width):
    t1 = n_first // tm
    t2 = n_second // tm
    return (pl.BlockSpec((tm, width), lambda i: (jnp.minimum(i, t1 - 1), 0)),
            pl.BlockSpec((tm, width), lambda i: (jnp.clip(i - t1, 0, t2 - 1), 0)))


def _in_proj_body(xa_ref, xb_ref, nw_ref, w_ref, z_ref, xbc_ref, dt_ref, u_ref, *, tiles_a):
    x = jnp.where(pl.program_id(0) < tiles_a, xa_ref[...], xb_ref[...])
    h = (x * lax.rsqrt(jnp.mean(x * x, axis=-1, keepdims=True) + EPS) * nw_ref[...]).astype(jnp.bfloat16)
    off = 0
    for ref in (z_ref, xbc_ref, dt_ref, u_ref):
        n = ref.shape[1]
        ref[...] = jnp.dot(h, w_ref[:, off:off + n], preferred_element_type=jnp.float32)
        off += n


def _in_proj(xa, xb, norm1_w, w_in):
    d = xa.shape[1]
    n = xa.shape[0] + xb.shape[0]
    f32, bf16 = jnp.float32, jnp.bfloat16
    s1, s2 = D_SSD + D_CONV, D_SSD + D_CONV + N_SSD_HEADS
    w = jnp.concatenate([w_in[:, :s1], jnp.pad(w_in[:, s1:s2], ((0, 0), (0, LANES - N_SSD_HEADS))),
                         w_in[:, s2:]], axis=1).astype(bf16)
    widths = (D_SSD, D_CONV, LANES, D_POOL)
    tm = PROJ_ROW_TILE
    return pl.pallas_call(
        functools.partial(_in_proj_body, tiles_a=xa.shape[0] // tm),
        grid=(n // tm,),
        in_specs=[*_two_part_specs(xa.shape[0], xb.shape[0], tm, d),
                  pl.BlockSpec((1, d), lambda i: (0, 0)),
                  pl.BlockSpec((d, sum(widths)), lambda i: (0, 0))],
        out_specs=[pl.BlockSpec((tm, wd), lambda i: (i, 0)) for wd in widths],
        out_shape=[jax.ShapeDtypeStruct((n, wd), f32) for wd in widths],
        compiler_params=pltpu.CompilerParams(dimension_semantics=("parallel",),
                                             vmem_limit_bytes=VMEM_LIMIT_BYTES),
        name="in_proj",
    )(xa, xb, norm1_w.reshape(1, d).astype(f32), w)


def _out_proj_body(ma_ref, mb_ref, w_ref, xa_ref, xb_ref, o_ref, *, tiles_a):
    first = pl.program_id(0) < tiles_a
    m = jnp.where(first, ma_ref[...], mb_ref[...])
    x = jnp.where(first, xa_ref[...], xb_ref[...])
    o_ref[...] = x + jnp.dot(m, w_ref[...], preferred_element_type=jnp.float32)


def _out_proj(ma, mb, w_out, xa, xb):
    d = xa.shape[1]
    n = xa.shape[0] + xb.shape[0]
    tm = PROJ_ROW_TILE
    return pl.pallas_call(
        functools.partial(_out_proj_body, tiles_a=xa.shape[0] // tm),
        grid=(n // tm,),
        in_specs=[*_two_part_specs(xa.shape[0], xb.shape[0], tm, D_MIX),
                  pl.BlockSpec((D_MIX, d), lambda i: (0, 0)),
                  *_two_part_specs(xa.shape[0], xb.shape[0], tm, d)],
        out_specs=pl.BlockSpec((tm, d), lambda i: (i, 0)),
        out_shape=jax.ShapeDtypeStruct((n, d), jnp.float32),
        compiler_params=pltpu.CompilerParams(dimension_semantics=("parallel",),
                                             vmem_limit_bytes=VMEM_LIMIT_BYTES),
        name="out_proj",
    )(ma, mb, w_out.astype(jnp.bfloat16), xa, xb)


def _mixer_constants():
    bf16 = jnp.bfloat16
    h = np.arange(LANES)[:, None]
    ch = np.arange(D_SSD)[None, :]
    expand = (ch // SSD_HEAD_DIM == h).astype(np.float32)
    i = np.arange(BLK)[:, None]
    j = np.arange(BLK)[None, :]
    causal = (j <= i).astype(np.float32)
    jh = np.arange(HIST_ROWS)[None, :]
    pcur = np.stack([((j <= i) & (i - j < w)) for w in POOL_WINDOWS]).astype(np.float32)
    phist = np.stack([(i + HIST_ROWS - jh < w) for w in POOL_WINDOWS]).astype(np.float32)
    return dict(expand=jnp.asarray(expand, bf16), expand_t=jnp.asarray(expand.T, bf16),
                causal=jnp.asarray(causal, bf16), pcur=jnp.asarray(pcur, bf16),
                phist=jnp.asarray(phist, bf16))


def _softplus(x):
    return jnp.maximum(x, 0.0) + jnp.log(1.0 + jnp.exp(-jnp.abs(x)))


def _conv_silu(ext_ref, cw_ref, cb_ref, first_row):
    acc = cb_ref[...] + cw_ref[0:1, :] * ext_ref[pl.ds(first_row, BLK), :]
    for k in range(1, CONV_WIDTH):
        acc = acc + cw_ref[k:k + 1, :] * ext_ref[pl.ds(first_row + k, BLK), :]
    return acc * jax.nn.sigmoid(acc)


def _ssd_intra(xbc_c, dt_raw, dtb_ref, alog_ref, causal_bf, expand_ref):
    f32 = jnp.float32
    xs = xbc_c[:, :D_SSD]
    bm = xbc_c[:, D_SSD:D_SSD + N_SSD_GROUPS * D_STATE]
    cm = xbc_c[:, D_SSD + N_SSD_GROUPS * D_STATE:]
    dt = _softplus(dt_raw + dtb_ref[...])
    a = dt * (-jnp.exp(alog_ref[...]))
    a_cum = _dot_sel_left(causal_bf, a, 3)
    dt_x = _dot_sel_right(dt, expand_ref[...], 2)
    return xs, bm, cm, dt, a_cum, xs * dt_x


def _ssd_diag_group(g, cb, a_cum, a_cum_t, keep, xdt):
    f32, bf16 = jnp.float32, jnp.bfloat16
    hg = N_SSD_HEADS // N_SSD_GROUPS
    lane = lax.broadcasted_iota(jnp.int32, (BLK, LANES), 1)
    first_head = lane < SSD_HEAD_DIM
    neg = jnp.float32(-jnp.inf)
    outs = []
    for pr in range(hg * SSD_HEAD_DIM // LANES):
        h1 = g * hg + 2 * pr
        blk = (g * hg * SSD_HEAD_DIM) // LANES + pr
        xp = xdt[:, blk * LANES:(blk + 1) * LANES]
        x1 = jnp.where(first_head, xp, 0.0).astype(bf16)
        x2 = jnp.where(first_head, 0.0, xp).astype(bf16)
        m1 = (cb * jnp.exp(jnp.where(keep, a_cum[:, h1:h1 + 1] - a_cum_t[h1:h1 + 1, :], neg))).astype(bf16)
        m2 = (cb * jnp.exp(jnp.where(keep, a_cum[:, h1 + 1:h1 + 2] - a_cum_t[h1 + 1:h1 + 2, :], neg))).astype(bf16)
        outs.append(jnp.dot(m1, x1, preferred_element_type=f32) + jnp.dot(m2, x2, preferred_element_type=f32))
    return jnp.concatenate(outs, axis=1)


def _gated_norm(y, z, nw_ref):
    yg = y * (z * jax.nn.sigmoid(z))
    return yg * lax.rsqrt(jnp.mean(yg * yg, axis=-1, keepdims=True) + EPS) * nw_ref[...]


def _prompt_mixer_body(z_ref, xbc_ref, dt_ref, u_ref, cw_ref, cb_ref, dtb_ref, alog_ref, dskip_ref, nw_ref,
                       pw_ref, ps_ref, causal_ref, expand_ref, expand_t_ref, pcur_ref, phist_ref,
                       mix_ref, ssm_ref, ext_ref, pool_tail_ref, state_ref):
    f32, bf16 = jnp.float32, jnp.bfloat16
    c = pl.program_id(1)
    gw = D_SSD // N_SSD_GROUPS

    @pl.when(c == 0)
    def _():
        ext_ref[0:CONV_TAIL_ROWS, :] = jnp.zeros((CONV_TAIL_ROWS, D_CONV), f32)
        pool_tail_ref[...] = jnp.zeros(pool_tail_ref.shape, f32)
        state_ref[...] = jnp.zeros(state_ref.shape, f32)

    ext_ref[CONV_TAIL_ROWS:CONV_TAIL_ROWS + BLK, :] = xbc_ref[...]
    xbc_c = _conv_silu(ext_ref, cw_ref, cb_ref, CONV_TAIL_ROWS - (CONV_WIDTH - 1))
    ext_ref[0:CONV_TAIL_ROWS, :] = xbc_ref[BLK - CONV_TAIL_ROWS:BLK, :]

    causal_bf = causal_ref[...]
    keep = causal_bf > 0
    xs, bm, cm, dt, a_cum, xdt = _ssd_intra(xbc_c, dt_ref[...], dtb_ref, alog_ref, causal_bf, expand_ref)
    a_cum_t = jnp.transpose(a_cum)
    a_tot = a_cum[BLK - 1:BLK, :]
    ea_x = _dot_sel_right(jnp.exp(a_cum), expand_ref[...], 2)
    dte_x = _dot_sel_right(jnp.exp(a_tot - a_cum), expand_ref[...], 2)
    cd_t = jnp.broadcast_to(jnp.exp(a_cum_t[:, BLK - 1:BLK]), (LANES, LANES))
    cd_col = _dot_sel_left(expand_t_ref[...], cd_t, 2)[:, 0:1]

    y_parts = []
    for g in range(N_SSD_GROUPS):
        cg = cm[:, g * D_STATE:(g + 1) * D_STATE].astype(bf16)
        bg = bm[:, g * D_STATE:(g + 1) * D_STATE].astype(bf16)
        cb = lax.dot_general(cg, bg, NT_DIMS, preferred_element_type=f32)
        y_diag = _ssd_diag_group(g, cb, a_cum, a_cum_t, keep, xdt)
        sg = state_ref[g * gw:(g + 1) * gw, :]
        y_off = lax.dot_general(cg, sg.astype(bf16), NT_DIMS, preferred_element_type=f32)
        y_parts.append(y_diag + y_off * ea_x[:, g * gw:(g + 1) * gw])
        xdte_t = jnp.transpose(xdt[:, g * gw:(g + 1) * gw] * dte_x[:, g * gw:(g + 1) * gw]).astype(bf16)
        state_ref[g * gw:(g + 1) * gw, :] = (sg * cd_col[g * gw:(g + 1) * gw, :]
                                             + jnp.dot(xdte_t, bg, preferred_element_type=f32))
    y = jnp.concatenate(y_parts, axis=1) + xs * dskip_ref[...]
    mix_ref[:, 0:D_SSD] = _gated_norm(y, z_ref[...], nw_ref).astype(bf16)

    @pl.when(c == pl.num_programs(1) - 1)
    def _():
        ssm_ref[0] = state_ref[...].reshape(N_SSD_HEADS, SSD_HEAD_DIM, D_STATE)

    u = u_ref[...]
    tail = pool_tail_ref[...]
    pos = (c * BLK + lax.broadcasted_iota(jnp.int32, (BLK, 1), 0) + 1).astype(f32)
    for gi, w in enumerate(POOL_WINDOWS):
        sl = slice(gi * POOL_GROUP_DIM, (gi + 1) * POOL_GROUP_DIM)
        ug = u[:, sl]
        wsum = _dot_sel_left(pcur_ref[gi], ug, 2) + _dot_sel_left(phist_ref[gi], tail[:, sl], 2)
        pooled = wsum / jnp.minimum(pos, jnp.float32(w)) - ug
        po = jnp.dot(pooled.astype(bf16), pw_ref[gi], preferred_element_type=f32) * ps_ref[:, sl]
        mix_ref[:, D_SSD + gi * POOL_GROUP_DIM:D_SSD + (gi + 1) * POOL_GROUP_DIM] = po.astype(bf16)
    pool_tail_ref[...] = u_ref[BLK - HIST_ROWS:BLK, :]


def _prompt_mixer(z, xbc, dt, u, n_seq, seq_len, conv_w, conv_b, dt_bias, A_log, D_skip, ssd_norm_w, pool_w,
                  pool_scale):
    f32, bf16 = jnp.float32, jnp.bfloat16
    n = n_seq * seq_len
    n_blk = seq_len // BLK
    k = _mixer_constants()

    def row_blk(width):
        return pl.BlockSpec((BLK, width), lambda b, c: (b * n_blk + c, 0))

    def const(shape):
        return pl.BlockSpec(shape, lambda b, c: (0,) * len(shape))

    pad_h = (0, LANES - N_SSD_HEADS)
    return pl.pallas_call(
        _prompt_mixer_body,
        grid=(n_seq, n_blk),
        in_specs=[row_blk(D_SSD), row_blk(D_CONV), row_blk(LANES), row_blk(D_POOL),
                  const((CONV_WIDTH, D_CONV)), const((1, D_CONV)), const((1, LANES)), const((1, LANES)),
                  const((1, D_SSD)), const((1, D_SSD)),
                  const((N_POOL_GROUPS, POOL_GROUP_DIM, POOL_GROUP_DIM)), const((1, D_POOL)),
                  const((BLK, BLK)), const((LANES, D_SSD)), const((D_SSD, LANES)),
                  const((N_POOL_GROUPS, BLK, BLK)), const((N_POOL_GROUPS, BLK, HIST_ROWS))],
        out_specs=[pl.BlockSpec((BLK, D_MIX), lambda b, c: (b * n_blk + c, 0)),
                   pl.BlockSpec((1, N_SSD_HEADS, SSD_HEAD_DIM, D_STATE), lambda b, c: (b, 0, 0, 0))],
        out_shape=[jax.ShapeDtypeStruct((n, D_MIX), bf16),
                   jax.ShapeDtypeStruct((n_seq, N_SSD_HEADS, SSD_HEAD_DIM, D_STATE), f32)],
        scratch_shapes=[pltpu.VMEM((CONV_TAIL_ROWS + BLK, D_CONV), f32),
                        pltpu.VMEM((HIST_ROWS, D_POOL), f32),
                        pltpu.VMEM((D_SSD, D_STATE), f32)],
        compiler_params=pltpu.CompilerParams(dimension_semantics=("parallel", "arbitrary"),
                                             vmem_limit_bytes=VMEM_LIMIT_BYTES),
        name="prompt_mixer",
    )(z, xbc, dt, u, conv_w.astype(f32), conv_b.reshape(1, D_CONV).astype(f32),
      jnp.pad(dt_bias.astype(f32), pad_h).reshape(1, LANES), jnp.pad(A_log.astype(f32), pad_h).reshape(1, LANES),
      jnp.repeat(D_skip.astype(f32), SSD_HEAD_DIM).reshape(1, D_SSD), ssd_norm_w.reshape(1, D_SSD).astype(f32),
      pool_w.astype(bf16), pool_scale.reshape(1, D_POOL).astype(f32),
      k["causal"], k["expand"], k["expand_t"], k["pcur"], k["phist"])


SEQ_PER_BLK = BLK // DEC_SEQ


def _sample_constants():
    bf16 = jnp.bfloat16
    r = np.arange(BLK)
    sq, st = r // DEC_SEQ, r % DEC_SEQ
    same = sq[:, None] == sq[None, :]
    causal = same & (st[None, :] <= st[:, None])
    nk = CONV_WIDTH - 1
    shift = np.stack([same & (st[None, :] == st[:, None] + k - nk) for k in range(nk)])
    cs = np.arange(SEQ_PER_BLK * nk)
    stsel = np.stack([(cs[None, :] // nk == sq[:, None]) & (cs[None, :] % nk == st[:, None] + k)
                      for k in range(nk)])
    pcur = np.stack([causal & (st[:, None] - st[None, :] < w) for w in POOL_WINDOWS])
    hs = np.arange(SEQ_PER_BLK * POOL_HIST)
    phist = np.stack([(hs[None, :] // POOL_HIST == sq[:, None])
                      & (st[:, None] + POOL_HIST - hs[None, :] % POOL_HIST < w) for w in POOL_WINDOWS])
    as_bf = lambda a: jnp.asarray(a.astype(np.float32), bf16)
    return dict(same=as_bf(same), causal=as_bf(causal), shift=as_bf(shift), stsel=as_bf(stsel),
                pcur=as_bf(pcur), phist=as_bf(phist))


def _sample_mixer_body(z_ref, xbc_ref, dt_ref, u_ref, cst_ref, pst_ref, ssm_in_ref,
                       cw_ref, cb_ref, dtb_ref, alog_ref, dskip_ref, nw_ref, pw_ref, ps_ref,
                       causal_ref, same_ref, expand_ref, expand_t_ref, shift_ref, stsel_ref, pcur_ref, phist_ref,
                       mix_ref, ssm_out_ref,
                       ydiag_ref, ea_ref, yt_ref, cdh_ref, cdl_ref, xdte_t_ref, bm_ref, cm_ref, *, pos0):
    f32, bf16 = jnp.float32, jnp.bfloat16
    s = pl.program_id(1)
    gw = D_SSD // N_SSD_GROUPS

    @pl.when(s == 0)
    def _():
        xbc = xbc_ref[...]
        cst = cst_ref[...]
        acc = cb_ref[...] + cw_ref[CONV_WIDTH - 1:CONV_WIDTH, :] * xbc
        for k in range(CONV_WIDTH - 1):
            tap = _dot_sel_left(shift_ref[k], xbc, 3) + _dot_sel_left(stsel_ref[k], cst, 3)
            acc = acc + cw_ref[k:k + 1, :] * tap
        xbc_c = acc * jax.nn.sigmoid(acc)

        causal_bf = causal_ref[...]
        keep = causal_bf > 0
        xs, bm, cm, dt, a_cum, xdt = _ssd_intra(xbc_c, dt_ref[...], dtb_ref, alog_ref, causal_bf, expand_ref)
        a_tot = _dot_sel_left(same_ref[...], dt * (-jnp.exp(alog_ref[...])), 3)
        a_cum_t = jnp.transpose(a_cum)
        ea_ref[...] = _dot_sel_right(jnp.exp(a_cum), expand_ref[...], 2)
        dte_x = _dot_sel_right(jnp.exp(a_tot - a_cum), expand_ref[...], 2)
        cd_col = _dot_sel_left(expand_t_ref[...], jnp.exp(jnp.transpose(a_tot)), 2)
        cd_hi, cd_lo = _split2(cd_col)
        cdh_ref[...] = cd_hi
        cdl_ref[...] = cd_lo
        bm_ref[...] = bm.astype(bf16)
        cm_ref[...] = cm.astype(bf16)
        for g in range(N_SSD_GROUPS):
            cg = cm[:, g * D_STATE:(g + 1) * D_STATE].astype(bf16)
            bg = bm[:, g * D_STATE:(g + 1) * D_STATE].astype(bf16)
            cb = lax.dot_general(cg, bg, NT_DIMS, preferred_element_type=f32)
            y_diag = _ssd_diag_group(g, cb, a_cum, a_cum_t, keep, xdt)
            ydiag_ref[:, g * gw:(g + 1) * gw] = y_diag + xs[:, g * gw:(g + 1) * gw] * dskip_ref[:, g * gw:(g + 1) * gw]
            xdte_t_ref[g * gw:(g + 1) * gw, :] = jnp.transpose(
                xdt[:, g * gw:(g + 1) * gw] * dte_x[:, g * gw:(g + 1) * gw]).astype(bf16)
        yt_ref[...] = jnp.zeros(yt_ref.shape, f32)

        u = u_ref[...]
        pst = pst_ref[...]
        step = lax.broadcasted_iota(jnp.int32, (BLK, 1), 0) % DEC_SEQ
        pos = (step + (pos0 + 1)).astype(f32)
        for gi, w in enumerate(POOL_WINDOWS):
            sl = slice(gi * POOL_GROUP_DIM, (gi + 1) * POOL_GROUP_DIM)
            ug = u[:, sl]
            wsum = _dot_sel_left(pcur_ref[gi], ug, 2) + _dot_sel_left(phist_ref[gi], pst[:, sl], 2)
            pooled = wsum / jnp.minimum(pos, jnp.float32(w)) - ug
            po = jnp.dot(pooled.astype(bf16), pw_ref[gi], preferred_element_type=f32) * ps_ref[:, sl]
            mix_ref[:, D_SSD + gi * POOL_GROUP_DIM:D_SSD + (gi + 1) * POOL_GROUP_DIM] = po.astype(bf16)

    rows_of_s = lax.broadcasted_iota(jnp.int32, (BLK, LANES), 0) // DEC_SEQ == s
    cols_of_s = lax.broadcasted_iota(jnp.int32, (gw, BLK), 1) // DEC_SEQ == s
    pick_s = jnp.where(lax.broadcasted_iota(jnp.int32, (BLK, LANES), 0) == DEC_SEQ * s, 1.0, 0.0).astype(bf16)
    state = ssm_in_ref[0].reshape(D_SSD, D_STATE)
    for g in range(N_SSD_GROUPS):
        rs = slice(g * gw, (g + 1) * gw)
        sg = state[rs, :]
        cg = cm_ref[:, g * D_STATE:(g + 1) * D_STATE]
        bg = bm_ref[:, g * D_STATE:(g + 1) * D_STATE]
        yt = lax.dot_general(sg.astype(bf16), cg, NT_DIMS, preferred_element_type=f32)
        yt_ref[rs, :] += jnp.where(cols_of_s, yt, 0.0)
        cd = (jnp.dot(cdh_ref[rs, :], pick_s, preferred_element_type=f32)
              + jnp.dot(cdl_ref[rs, :], pick_s, preferred_element_type=f32))
        upd = jnp.dot(xdte_t_ref[rs, :], jnp.where(rows_of_s, bg, jnp.zeros_like(bg)),
                      preferred_element_type=f32)
        ssm_out_ref[0, g * (N_SSD_HEADS // N_SSD_GROUPS):(g + 1) * (N_SSD_HEADS // N_SSD_GROUPS)] = (
            sg * cd + upd).reshape(N_SSD_HEADS // N_SSD_GROUPS, SSD_HEAD_DIM, D_STATE)

    @pl.when(s == pl.num_programs(1) - 1)
    def _():
        y = ydiag_ref[...] + jnp.transpose(yt_ref[...]) * ea_ref[...]
        mix_ref[:, 0:D_SSD] = _gated_norm(y, z_ref[...], nw_ref).astype(bf16)


def _sample_mixer(z, xbc, dt, u, row0, n_seq, state_conv, state_ssm, state_pool, pos0,
                  conv_w, conv_b, dt_bias, A_log, D_skip, ssd_norm_w, pool_w, pool_scale):
    f32, bf16 = jnp.float32, jnp.bfloat16
    n_blk = n_seq // SEQ_PER_BLK
    blk0 = row0 // BLK
    nk = CONV_WIDTH - 1
    k = _mixer_constants()
    ks = _sample_constants()

    def row_blk(width):
        return pl.BlockSpec((BLK, width), lambda j, s: (blk0 + j, 0))

    def const(shape):
        return pl.BlockSpec(shape, lambda j, s: (0,) * len(shape))

    state_spec = pl.BlockSpec((1, N_SSD_HEADS, SSD_HEAD_DIM, D_STATE), lambda j, s: (j * SEQ_PER_BLK + s, 0, 0, 0))
    pad_h = (0, LANES - N_SSD_HEADS)
    return pl.pallas_call(
        functools.partial(_sample_mixer_body, pos0=pos0),
        grid=(n_blk, SEQ_PER_BLK),
        in_specs=[row_blk(D_SSD), row_blk(D_CONV), row_blk(LANES), row_blk(D_POOL),
                  pl.BlockSpec((SEQ_PER_BLK * nk, D_CONV), lambda j, s: (j, 0)),
                  pl.BlockSpec((SEQ_PER_BLK * POOL_HIST, D_POOL), lambda j, s: (j, 0)),
                  state_spec,
                  const((CONV_WIDTH, D_CONV)), const((1, D_CONV)), const((1, LANES)), const((1, LANES)),
                  const((1, D_SSD)), const((1, D_SSD)),
                  const((N_POOL_GROUPS, POOL_GROUP_DIM, POOL_GROUP_DIM)), const((1, D_POOL)),
                  const((BLK, BLK)), const((BLK, BLK)), const((LANES, D_SSD)), const((D_SSD, LANES)),
                  const((nk, BLK, BLK)), const((nk, BLK, SEQ_PER_BLK * nk)),
                  const((N_POOL_GROUPS, BLK, BLK)), const((N_POOL_GROUPS, BLK, SEQ_PER_BLK * POOL_HIST))],
        out_specs=[pl.BlockSpec((BLK, D_MIX), lambda j, s: (j, 0)), state_spec],
        out_shape=[jax.ShapeDtypeStruct((n_seq * DEC_SEQ, D_MIX), bf16),
                   jax.ShapeDtypeStruct((n_seq, N_SSD_HEADS, SSD_HEAD_DIM, D_STATE), f32)],
        scratch_shapes=[pltpu.VMEM((BLK, D_SSD), f32), pltpu.VMEM((BLK, D_SSD), f32),
                        pltpu.VMEM((D_SSD, BLK), f32), pltpu.VMEM((D_SSD, BLK), bf16),
                        pltpu.VMEM((D_SSD, BLK), bf16), pltpu.VMEM((D_SSD, BLK), bf16),
                        pltpu.VMEM((BLK, N_SSD_GROUPS * D_STATE), bf16),
                        pltpu.VMEM((BLK, N_SSD_GROUPS * D_STATE), bf16)],
        compiler_params=pltpu.CompilerParams(dimension_semantics=("parallel", "arbitrary"),
                                             vmem_limit_bytes=VMEM_LIMIT_BYTES),
        name="sample_mixer",
    )(z, xbc, dt, u, state_conv.reshape(n_seq * nk, D_CONV), state_pool.reshape(n_seq * POOL_HIST, D_POOL),
      state_ssm, conv_w.astype(f32), conv_b.reshape(1, D_CONV).astype(f32),
      jnp.pad(dt_bias.astype(f32), pad_h).reshape(1, LANES), jnp.pad(A_log.astype(f32), pad_h).reshape(1, LANES),
      jnp.repeat(D_skip.astype(f32), SSD_HEAD_DIM).reshape(1, D_SSD), ssd_norm_w.reshape(1, D_SSD).astype(f32),
      pool_w.astype(bf16), pool_scale.reshape(1, D_POOL).astype(f32),
      ks["causal"], ks["same"], k["expand"], k["expand_t"], ks["shift"], ks["stsel"],
      ks["pcur"], ks["phist"])


def _moe_sizes(n_tokens, tm):
    nt = n_tokens // tm
    lmax = -(-(TOP_K * tm + N_EXPERTS * (MOE_SEG_ROWS - 1)) // MOE_CHUNK) * MOE_CHUNK
    rows = TOP_K * n_tokens + nt * N_EXPERTS * (MOE_SEG_ROWS - 1) + N_EXPERTS * (MOE_ROW_TILE - 1)
    n_row_tiles = -(-rows // MOE_ROW_TILE)
    return nt, lmax, n_row_tiles


def _router_body(x_ref, nw_ref, rwh_ref, rwl_ref, rb_ref, h_ref, posg_ref, post_ref, cnt_ref):
    f32, bf16 = jnp.float32, jnp.bfloat16
    tm = x_ref.shape[0]
    x = x_ref[...]
    h = x * lax.rsqrt(jnp.mean(x * x, axis=-1, keepdims=True) + EPS) * nw_ref[...]
    h_hi = h.astype(bf16)
    h_ref[...] = h_hi
    h_lo = (h - h_hi.astype(f32)).astype(bf16)
    wh = rwh_ref[...]
    logits = (jnp.dot(h_hi, wh, preferred_element_type=f32)
              + jnp.dot(h_lo, wh, preferred_element_type=f32)
              + jnp.dot(h_hi, rwl_ref[...], preferred_element_type=f32)) + rb_ref[...]
    lane = lax.broadcasted_iota(jnp.int32, (tm, LANES), 1)
    lanef = lane.astype(f32)
    neg = jnp.float32(-jnp.inf)
    l = jnp.where(lane < N_EXPERTS, logits, neg)
    sels, vals = [], []
    for _ in range(TOP_K):
        m = jnp.max(l, axis=1, keepdims=True)
        idx = jnp.min(jnp.where(l == m, lanef, jnp.float32(LANES)), axis=1, keepdims=True)
        sel = lanef == idx
        l = jnp.where(sel, neg, l)
        sels.append(sel)
        vals.append(m)
    exps = [jnp.exp(v - vals[0]) for v in vals]
    denom = exps[0] + exps[1] + exps[2] + exps[3]
    gates = [e / denom for e in exps]
    chosen = jnp.where(sels[0] | sels[1] | sels[2] | sels[3], 1.0, 0.0).astype(f32)
    row = lax.broadcasted_iota(jnp.int32, (tm, tm), 0)
    col = lax.broadcasted_iota(jnp.int32, (tm, tm), 1)
    lower = jnp.where(col < row, 1.0, 0.0).astype(bf16)
    rank = jnp.dot(lower, chosen.astype(bf16), preferred_element_type=f32)
    cnt = jnp.sum(chosen, axis=0, keepdims=True)
    seg_units = jnp.floor((cnt + (MOE_SEG_ROWS - 1)) * (1.0 / MOE_SEG_ROWS))
    r2 = lax.broadcasted_iota(jnp.int32, (LANES, LANES), 0)
    c2 = lax.broadcasted_iota(jnp.int32, (LANES, LANES), 1)
    upper = jnp.where(r2 < c2, 1.0, 0.0).astype(bf16)
    lstart = jnp.dot(jnp.broadcast_to(seg_units, (8, LANES)).astype(bf16), upper,
                     preferred_element_type=f32)[0:1, :] * MOE_SEG_ROWS
    posmat = lstart + rank
    posg = jnp.zeros((tm, LANES), f32)
    for k in range(TOP_K):
        pos_k = jnp.sum(jnp.where(sels[k], posmat, 0.0), axis=1, keepdims=True)
        posg = posg + jnp.where(lane == k, pos_k, 0.0) + jnp.where(lane == TOP_K + k, gates[k], 0.0)
    posg_ref[...] = posg
    post_ref[...] = jnp.transpose(posg)[0:8, :]
    cnt_ref[0] = jnp.broadcast_to(cnt, (8, LANES)).astype(jnp.int32)


def _moe_router(x1, norm2_w, router_w, router_b, tm):
    n, d = x1.shape
    nt = n // tm
    f32, bf16 = jnp.float32, jnp.bfloat16
    rw = jnp.pad(router_w.astype(f32), ((0, 0), (0, LANES - N_EXPERTS)))
    rw_hi = rw.astype(bf16)
    rw_lo = (rw - rw_hi.astype(f32)).astype(bf16)
    rb = jnp.pad(router_b.astype(f32), (0, LANES - N_EXPERTS)).reshape(1, LANES)
    return pl.pallas_call(
        _router_body,
        grid=(nt,),
        in_specs=[pl.BlockSpec((tm, d), lambda i: (i, 0)),
                  pl.BlockSpec((1, d), lambda i: (0, 0)),
                  pl.BlockSpec((d, LANES), lambda i: (0, 0)),
                  pl.BlockSpec((d, LANES), lambda i: (0, 0)),
                  pl.BlockSpec((1, LANES), lambda i: (0, 0))],
        out_specs=[pl.BlockSpec((tm, d), lambda i: (i, 0)),
                   pl.BlockSpec((tm, LANES), lambda i: (i, 0)),
                   pl.BlockSpec((8, tm), lambda i: (0, i)),
                   pl.BlockSpec((1, 8, LANES), lambda i: (i, 0, 0))],
        out_shape=[jax.ShapeDtypeStruct((n, d), bf16),
                   jax.ShapeDtypeStruct((n, LANES), f32),
                   jax.ShapeDtypeStruct((8, n), f32),
                   jax.ShapeDtypeStruct((nt, 8, LANES), jnp.int32)],
        compiler_params=pltpu.CompilerParams(dimension_semantics=("parallel",),
                                             vmem_limit_bytes=VMEM_LIMIT_BYTES),
        name="moe_router",
    )(x1, norm2_w.reshape(1, d).astype(f32), rw_hi, rw_lo, rb)


def _moe_plan(cnt, n_row_tiles):
    i32 = jnp.int32
    pad = (cnt + (MOE_SEG_ROWS - 1)) // MOE_SEG_ROWS * MOE_SEG_ROWS
    lstart = jnp.cumsum(pad, axis=1) - pad
    lp = jnp.sum(pad, axis=1)
    tot = jnp.sum(pad, axis=0)
    reg = (tot + (MOE_ROW_TILE - 1)) // MOE_ROW_TILE * MOE_ROW_TILE
    reg_end = jnp.cumsum(reg)
    estart = reg_end - reg
    seg = estart[None, :] + jnp.cumsum(pad, axis=0) - pad
    tiles_end = reg_end // MOE_ROW_TILE
    n_active = tiles_end[-1]
    j = jnp.arange(n_row_tiles, dtype=i32)
    jc = jnp.minimum(j, n_active - 1)
    tile_e = jnp.searchsorted(tiles_end, jc, side='right').astype(i32)
    prev_e = jnp.concatenate([jnp.full((1,), -1, i32), tile_e[:-1]])
    tile_first = ((tile_e != prev_e) & (j < n_active)).astype(i32)
    return dict(
        lstart=lstart.reshape(-1).astype(i32), seg_units=(pad // MOE_SEG_ROWS).reshape(-1).astype(i32),
        seg=seg.reshape(-1).astype(i32), lp=lp.astype(i32),
        tail_start=(estart + tot).astype(i32), tail_units=((reg - tot) // MOE_SEG_ROWS).astype(i32),
        tile_e=tile_e, tile_blk=jc.astype(i32), tile_first=tile_first,
        n_active=n_active.reshape(1).astype(i32))


def _for_each_segment_copy(i, lstart_ref, units_ref, seg_ref, local_ref, global_ref, sem, to_global, fn):
    def per_expert(e, carry):
        k = i * N_EXPERTS + e
        ls, gs = lstart_ref[k], seg_ref[k]

        def per_piece(j, c):
            lo = pl.multiple_of(ls + j * MOE_SEG_ROWS, MOE_SEG_ROWS)
            go = pl.multiple_of(gs + j * MOE_SEG_ROWS, MOE_SEG_ROWS)
            loc = local_ref.at[pl.ds(lo, MOE_SEG_ROWS)]
            glo = global_ref.at[pl.ds(go, MOE_SEG_ROWS)]
            fn(pltpu.make_async_copy(loc, glo, sem) if to_global else pltpu.make_async_copy(glo, loc, sem))
            return c
        return lax.fori_loop(0, units_ref[k], per_piece, carry)
    lax.fori_loop(0, N_EXPERTS, per_expert, 0)


def _dispatch_body(lstart_ref, units_ref, seg_ref, lp_ref, tail_start_ref, tail_units_ref, n_active_ref,
                   h_ref, post_ref, xs_ref, stage_ref, zero_ref, sem):
    f32, bf16 = jnp.float32, jnp.bfloat16
    i = pl.program_id(0)
    tm = h_ref.shape[0]
    lmax = stage_ref.shape[0]
    n_row_tiles = xs_ref.shape[0] // MOE_ROW_TILE

    def for_each_fill_copy(fn):
        def per_expert(e, carry):
            ts = tail_start_ref[e]

            def per_piece(j, c):
                go = pl.multiple_of(ts + j * MOE_SEG_ROWS, MOE_SEG_ROWS)
                fn(pltpu.make_async_copy(zero_ref.at[pl.ds(0, MOE_SEG_ROWS)],
                                         xs_ref.at[pl.ds(go, MOE_SEG_ROWS)], sem))
                return c
            return lax.fori_loop(0, tail_units_ref[e], per_piece, carry)
        lax.fori_loop(0, N_EXPERTS, per_expert, 0)

        def per_unused_tile(j, c):
            go = pl.multiple_of(j * MOE_ROW_TILE, MOE_ROW_TILE)
            fn(pltpu.make_async_copy(zero_ref, xs_ref.at[pl.ds(go, MOE_ROW_TILE)], sem))
            return c
        lax.fori_loop(n_active_ref[0], n_row_tiles, per_unused_tile, 0)

    @pl.when(i == 0)
    def _():
        zero_ref[...] = jnp.zeros(zero_ref.shape, bf16)
        for_each_fill_copy(lambda cp: cp.start())
        for_each_fill_copy(lambda cp: cp.wait())

    pos = post_ref[0:TOP_K, :]
    h = h_ref[...]
    for c in range(lmax // MOE_CHUNK):
        @pl.when(c * MOE_CHUNK < lp_ref[i])
        def _():
            r = (lax.broadcasted_iota(jnp.int32, (MOE_CHUNK, tm), 0) + c * MOE_CHUNK).astype(f32)
            hit = (pos[0:1, :] == r) | (pos[1:2, :] == r) | (pos[2:3, :] == r) | (pos[3:4, :] == r)
            sel = jnp.where(hit, 1.0, 0.0).astype(bf16)
            stage_ref[c * MOE_CHUNK:(c + 1) * MOE_CHUNK, :] = jnp.dot(
                sel, h, preferred_element_type=f32).astype(bf16)

    args = (i, lstart_ref, units_ref, seg_ref, stage_ref, xs_ref, sem, True)
    _for_each_segment_copy(*args, lambda cp: cp.start())
    _for_each_segment_copy(*args, lambda cp: cp.wait())


def _moe_dispatch(h2, post, plan, tm, lmax, n_row_tiles):
    n, d = h2.shape
    nt = n // tm
    grid_spec = pltpu.PrefetchScalarGridSpec(
        num_scalar_prefetch=7,
        grid=(nt,),
        in_specs=[pl.BlockSpec((tm, d), lambda i, *_: (i, 0)),
                  pl.BlockSpec((8, tm), lambda i, *_: (0, i))],
        out_specs=pl.BlockSpec(memory_space=pl.ANY),
        scratch_shapes=[pltpu.VMEM((lmax, d), jnp.bfloat16),
                        pltpu.VMEM((MOE_ROW_TILE, d), jnp.bfloat16),
                        pltpu.SemaphoreType.DMA(())])
    return pl.pallas_call(
        _dispatch_body,
        grid_spec=grid_spec,
        out_shape=jax.ShapeDtypeStruct((n_row_tiles * MOE_ROW_TILE, d), jnp.bfloat16),
        compiler_params=pltpu.CompilerParams(dimension_semantics=("arbitrary",),
                                             vmem_limit_bytes=VMEM_LIMIT_BYTES),
        name="moe_dispatch",
    )(plan["lstart"], plan["seg_units"], plan["seg"], plan["lp"], plan["tail_start"], plan["tail_units"],
      plan["n_active"], h2, post)


def _experts_body(tile_e_ref, tile_blk_ref, tile_first_ref, n_active_ref,
                  xs_ref, wgu_ref, bgu_ref, wd_ref, bd_ref, os_ref, wgu_bf, wd_bf):
    f32, bf16 = jnp.float32, jnp.bfloat16
    j = pl.program_id(0)

    @pl.when(j < n_active_ref[0])
    def _():
        @pl.when(tile_first_ref[j] == 1)
        def _():
            wgu_bf[...] = wgu_ref[0].astype(bf16)
            wd_bf[...] = wd_ref[0].astype(bf16)

        x = xs_ref[...]
        gu = jnp.dot(x, wgu_bf[...], preferred_element_type=f32) + bgu_ref[0]
        gate = jnp.minimum(gu[:, :D_FF], SWIGLU_LIMIT)
        up = jnp.clip(gu[:, D_FF:], -SWIGLU_LIMIT, SWIGLU_LIMIT)
        act = (up + 1.0) * (gate * jax.nn.sigmoid(SWIGLU_ALPHA * gate))
        out = jnp.dot(act.astype(bf16), wd_bf[...], preferred_element_type=f32) + bd_ref[0]
        os_ref[...] = out.astype(bf16)

    @pl.when(j >= n_active_ref[0])
    def _():
        os_ref[...] = jnp.zeros(os_ref.shape, bf16)


def _moe_experts(xs, plan, w_gate_up, b_gate_up, w_down, b_down, n_row_tiles):
    d = xs.shape[1]
    grid_spec = pltpu.PrefetchScalarGridSpec(
        num_scalar_prefetch=4,
        grid=(n_row_tiles,),
        in_specs=[pl.BlockSpec((MOE_ROW_TILE, d), lambda j, te, tb, tf, na: (tb[j], 0)),
                  pl.BlockSpec((1, d, 2 * D_FF), lambda j, te, tb, tf, na: (te[j], 0, 0)),
                  pl.BlockSpec((1, 1, 2 * D_FF), lambda j, te, tb, tf, na: (te[j], 0, 0)),
                  pl.BlockSpec((1, D_FF, d), lambda j, te, tb, tf, na: (te[j], 0, 0)),
                  pl.BlockSpec((1, 1, d), lambda j, te, tb, tf, na: (te[j], 0, 0))],
        out_specs=pl.BlockSpec((MOE_ROW_TILE, d), lambda j, te, tb, tf, na: (j, 0)),
        scratch_shapes=[pltpu.VMEM((d, 2 * D_FF), jnp.bfloat16),
                        pltpu.VMEM((D_FF, d), jnp.bfloat16)])
    return pl.pallas_call(
        _experts_body,
        grid_spec=grid_spec,
        out_shape=jax.ShapeDtypeStruct(xs.shape, jnp.bfloat16),
        compiler_params=pltpu.CompilerParams(dimension_semantics=("arbitrary",),
                                             vmem_limit_bytes=VMEM_LIMIT_BYTES),
        name="moe_experts",
    )(plan["tile_e"], plan["tile_blk"], plan["tile_first"], plan["n_active"],
      xs, w_gate_up, b_gate_up.reshape(N_EXPERTS, 1, 2 * D_FF), w_down, b_down.reshape(N_EXPERTS, 1, d))


def _combine_body(lstart_ref, units_ref, seg_ref, lp_ref,
                  os_ref, posg_ref, x_ref, fw_ref, yp_ref, ys_ref, stage_ref, acc_ref, sem,
                  *, n_prompt_tiles):
    f32, bf16 = jnp.float32, jnp.bfloat16
    i = pl.program_id(0)
    tm = x_ref.shape[0]
    lmax = stage_ref.shape[0]

    @pl.when(i == 0)
    def _():
        stage_ref[...] = jnp.zeros(stage_ref.shape, bf16)

    args = (i, lstart_ref, units_ref, seg_ref, stage_ref, os_ref, sem, False)
    _for_each_segment_copy(*args, lambda cp: cp.start())
    _for_each_segment_copy(*args, lambda cp: cp.wait())

    posg = posg_ref[...]
    pos = [posg[:, k:k + 1] for k in range(TOP_K)]
    gate = [posg[:, TOP_K + k:TOP_K + k + 1] for k in range(TOP_K)]
    acc_ref[...] = x_ref[...]
    for c in range(lmax // MOE_CHUNK):
        @pl.when(c * MOE_CHUNK < lp_ref[i])
        def _():
            r = (lax.broadcasted_iota(jnp.int32, (tm, MOE_CHUNK), 1) + c * MOE_CHUNK).astype(f32)
            w = jnp.zeros((tm, MOE_CHUNK), f32)
            for k in range(TOP_K):
                w = w + jnp.where(pos[k] == r, gate[k], 0.0)
            acc_ref[...] += jnp.dot(w.astype(bf16), stage_ref[c * MOE_CHUNK:(c + 1) * MOE_CHUNK, :],
                                    preferred_element_type=f32)

    y = acc_ref[...]
    out = y * lax.rsqrt(jnp.mean(y * y, axis=-1, keepdims=True) + EPS) * fw_ref[...]

    @pl.when(i < n_prompt_tiles)
    def _():
        yp_ref[...] = out

    @pl.when(i >= n_prompt_tiles)
    def _():
        ys_ref[...] = out


def _moe_combine(os_, posg, x1, final_norm_w, plan, tm, lmax, n_prompt):
    n, d = x1.shape
    nt = n // tm
    n_prompt_tiles = n_prompt // tm
    n_sample_tiles = nt - n_prompt_tiles
    grid_spec = pltpu.PrefetchScalarGridSpec(
        num_scalar_prefetch=4,
        grid=(nt,),
        in_specs=[pl.BlockSpec(memory_space=pl.ANY),
                  pl.BlockSpec((tm, LANES), lambda i, *_: (i, 0)),
                  pl.BlockSpec((tm, d), lambda i, *_: (i, 0)),
                  pl.BlockSpec((1, d), lambda i, *_: (0, 0))],
        out_specs=[pl.BlockSpec((tm, d), lambda i, *_: (jnp.minimum(i, n_prompt_tiles - 1), 0)),
                   pl.BlockSpec((tm, d), lambda i, *_: (jnp.maximum(i - n_prompt_tiles, 0), 0))],
        scratch_shapes=[pltpu.VMEM((lmax, d), jnp.bfloat16),
                        pltpu.VMEM((tm, d), jnp.float32),
                        pltpu.SemaphoreType.DMA(())])
    return pl.pallas_call(
        functools.partial(_combine_body, n_prompt_tiles=n_prompt_tiles),
        grid_spec=grid_spec,
        out_shape=[jax.ShapeDtypeStruct((n_prompt, d), jnp.float32),
                   jax.ShapeDtypeStruct((n_sample_tiles * tm, d), jnp.float32)],
        compiler_params=pltpu.CompilerParams(dimension_semantics=("arbitrary",),
                                             vmem_limit_bytes=VMEM_LIMIT_BYTES),
        name="moe_combine",
    )(plan["lstart"], plan["seg_units"], plan["seg"], plan["lp"],
      os_, posg, x1, final_norm_w.reshape(1, d).astype(jnp.float32))


def _moe_block(x1, n_prompt, norm2_w, router_w, router_b, w_gate_up, b_gate_up, w_down, b_down,
               final_norm_w, tm=MOE_TOKEN_TILE):
    n = x1.shape[0]
    nt, lmax, n_row_tiles = _moe_sizes(n, tm)
    h2, posg, post, cnt3 = _moe_router(x1, norm2_w, router_w, router_b, tm)
    plan = _moe_plan(cnt3[:, 0, :N_EXPERTS], n_row_tiles)
    xs = _moe_dispatch(h2, post, plan, tm, lmax, n_row_tiles)
    os_ = _moe_experts(xs, plan, w_gate_up, b_gate_up, w_down, b_down, n_row_tiles)
    return _moe_combine(os_, posg, x1, final_norm_w, plan, tm, lmax, n_prompt)


def kernel(x_prompt, x_sample, state_ssm, state_conv, state_pool, norm1_w, w_in, conv_w, conv_b, dt_bias,
           A_log, D_skip, ssd_norm_w, pool_w, pool_scale, w_out, norm2_w, router_w, router_b, w_gate_up,
           b_gate_up, w_down, b_down, final_norm_w):
    n_prompt = BATCH * SEQ
    n_sample = DEC_BATCH * DEC_SEQ
    xp = x_prompt.reshape(n_prompt, D_MODEL)
    xs = x_sample.reshape(n_sample, D_MODEL)
    z, xbc, dt_raw, u = _in_proj(xp, xs, norm1_w[0], w_in[0])
    mp = (conv_w[0], conv_b[0], dt_bias[0], A_log[0], D_skip[0], ssd_norm_w[0], pool_w[0], pool_scale[0])
    mix_p, s1 = _prompt_mixer(z, xbc, dt_raw, u, BATCH, SEQ, *mp)
    mix_s, s2 = _sample_mixer(z, xbc, dt_raw, u, n_prompt, DEC_BATCH, state_conv[0], state_ssm[0], state_pool[0],
                              PAST_LEN, *mp)
    nk = CONV_WIDTH - 1
    c1 = xbc[:n_prompt].reshape(BATCH, SEQ, D_CONV)[:, SEQ - nk:]
    p1 = u[:n_prompt].reshape(BATCH, SEQ, D_POOL)[:, SEQ - POOL_HIST:]
    c2 = xbc[n_prompt:].reshape(DEC_BATCH, DEC_SEQ, D_CONV)[:, DEC_SEQ - nk:]
    p2 = jnp.concatenate([state_pool[0][:, DEC_SEQ:], u[n_prompt:].reshape(DEC_BATCH, DEC_SEQ, D_POOL)], axis=1)
    x1 = _out_proj(mix_p, mix_s, w_out[0], xp, xs)
    yp, ys = _moe_block(x1, n_prompt, norm2_w[0], router_w[0], router_b[0], w_gate_up[0], b_gate_up[0],
                        w_down[0], b_down[0], final_norm_w)
    return (yp.reshape(x_prompt.shape), ys.reshape(x_sample.shape),
            s1[None], c1[None], p1[None], s2[None], c2[None], p2[None])
```

```python
import functools
import math
import jax, jax.numpy as jnp
from jax import lax
import numpy as np
from jax.experimental import pallas as pl
from jax.experimental.pallas import tpu as pltpu

D_MODEL = 1024
BATCH = 8
SEQ = 2048
DEC_BATCH = 128
DEC_SEQ = 4
PAST_LEN = 16384

D_MIX = 2 * D_MODEL
D_SSD = 3 * D_MIX // 4
SSD_HEAD_DIM = 64
N_SSD_HEADS = D_SSD // SSD_HEAD_DIM
N_SSD_GROUPS = 4
D_STATE = 128
CONV_WIDTH = 4
SSD_CHUNK = 128
D_CONV = D_SSD + 2 * N_SSD_GROUPS * D_STATE
D_POOL = D_MIX - D_SSD
POOL_WINDOWS = (2, 4, 8, 16)
N_POOL_GROUPS = len(POOL_WINDOWS)
POOL_GROUP_DIM = D_POOL // N_POOL_GROUPS
POOL_HIST = max(POOL_WINDOWS) - 1
D_IN_PROJ = D_SSD + D_CONV + N_SSD_HEADS + D_POOL
N_EXPERTS = 32
TOP_K = 4
D_FF = D_MODEL
SWIGLU_LIMIT = 7.0
SWIGLU_ALPHA = 1.702
EPS = 1e-5

LANES = 128
BF16_SUBLANES = 16
VMEM_LIMIT_BYTES = 48 * 1024 * 1024

MOE_TOKEN_TILE = 512
MOE_SEG_ROWS = BF16_SUBLANES
MOE_ROW_TILE = 256
MOE_CHUNK = 256


BLK = SSD_CHUNK
PROJ_ROW_TILE = 512
HIST_ROWS = 16
CONV_TAIL_ROWS = 8
NT_DIMS = (((1,), (1,)), ((), ()))


def _split2(v):
    hi = v.astype(jnp.bfloat16)
    lo = (v - hi.astype(jnp.float32)).astype(jnp.bfloat16)
    return hi, lo


def _dot_sel_left(sel, v, passes):
    out = None
    rem = v
    for p in range(passes):
        part = rem.astype(jnp.bfloat16)
        d = jnp.dot(sel, part, preferred_element_type=jnp.float32)
        out = d if out is None else out + d
        if p + 1 < passes:
            rem = rem - part.astype(jnp.float32)
    return out


def _dot_sel_right(v, sel, passes):
    out = None
    rem = v
    for p in range(passes):
        part = rem.astype(jnp.bfloat16)
        d = jnp.dot(part, sel, preferred_element_type=jnp.float32)
        out = d if out is None else out + d
        if p + 1 < passes:
            rem = rem - part.astype(jnp.float32)
    return out


def _two_part_specs(n_first, n_second, tm, width):
    t1 = n_first // tm
    t2 = n_second // tm
    return (pl.BlockSpec((tm, width), lambda i: (jnp.minimum(i, t1 - 1), 0)),
            pl.BlockSpec((tm, width), lambda i: (jnp.clip(i - t1, 0, t2 - 1), 0)))


def _in_proj_body(xa_ref, xb_ref, nw_ref, w_ref, z_ref, xbc_ref, dt_ref, u_ref, *, tiles_a):
    x = jnp.where(pl.program_id(0) < tiles_a, xa_ref[...], xb_ref[...])
    h = (x * lax.rsqrt(jnp.mean(x * x, axis=-1, keepdims=True) + EPS) * nw_ref[...]).astype(jnp.bfloat16)
    off = 0
    for ref in (z_ref, xbc_ref, dt_ref, u_ref):
        n = ref.shape[1]
        ref[...] = jnp.dot(h, w_ref[:, off:off + n], preferred_element_type=jnp.float32)
        off += n


def _in_proj(xa, xb, norm1_w, w_in):
    d = xa.shape[1]
    n = xa.shape[0] + xb.shape[0]
    f32, bf16 = jnp.float32, jnp.bfloat16
    s1, s2 = D_SSD + D_CONV, D_SSD + D_CONV + N_SSD_HEADS
    w = jnp.concatenate([w_in[:, :s1], jnp.pad(w_in[:, s1:s2], ((0, 0), (0, LANES - N_SSD_HEADS))),
                         w_in[:, s2:]], axis=1).astype(bf16)
    widths = (D_SSD, D_CONV, LANES, D_POOL)
    tm = PROJ_ROW_TILE
    return pl.pallas_call(
        functools.partial(_in_proj_body, tiles_a=xa.shape[0] // tm),
        grid=(n // tm,),
        in_specs=[*_two_part_specs(xa.shape[0], xb.shape[0], tm, d),
                  pl.BlockSpec((1, d), lambda i: (0, 0)),
                  pl.BlockSpec((d, sum(widths)), lambda i: (0, 0), pipeline_mode=pl.Buffered(1))],
        out_specs=[pl.BlockSpec((tm, wd), lambda i: (i, 0)) for wd in widths],
        out_shape=[jax.ShapeDtypeStruct((n, wd), f32) for wd in widths],
        compiler_params=pltpu.CompilerParams(dimension_semantics=("parallel",),
                                             vmem_limit_bytes=VMEM_LIMIT_BYTES),
        name="in_proj",
    )(xa, xb, norm1_w.reshape(1, d).astype(f32), w)


def _out_proj_body(ma_ref, mb_ref, w_ref, xa_ref, xb_ref, o_ref, *, tiles_a):
    first = pl.program_id(0) < tiles_a
    m = jnp.where(first, ma_ref[...], mb_ref[...])
    x = jnp.where(first, xa_ref[...], xb_ref[...])
    o_ref[...] = x + jnp.dot(m, w_ref[...], preferred_element_type=jnp.float32)


def _out_proj(ma, mb, w_out, xa, xb):
    d = xa.shape[1]
    n = xa.shape[0] + xb.shape[0]
    tm = PROJ_ROW_TILE
    return pl.pallas_call(
        functools.partial(_out_proj_body, tiles_a=xa.shape[0] // tm),
        grid=(n // tm,),
        in_specs=[*_two_part_specs(xa.shape[0], xb.shape[0], tm, D_MIX),
                  pl.BlockSpec((D_MIX, d), lambda i: (0, 0)),
                  *_two_part_specs(xa.shape[0], xb.shape[0], tm, d)],
        out_specs=pl.BlockSpec((tm, d), lambda i: (i, 0)),
        out_shape=jax.ShapeDtypeStruct((n, d), jnp.float32),
        compiler_params=pltpu.CompilerParams(dimension_semantics=("parallel",),
                                             vmem_limit_bytes=VMEM_LIMIT_BYTES),
        name="out_proj",
    )(ma, mb, w_out.astype(jnp.bfloat16), xa, xb)


def _mixer_constants():
    bf16 = jnp.bfloat16
    h = np.arange(LANES)[:, None]
    ch = np.arange(D_SSD)[None, :]
    expand = (ch // SSD_HEAD_DIM == h).astype(np.float32)
    i = np.arange(BLK)[:, None]
    j = np.arange(BLK)[None, :]
    causal = (j <= i).astype(np.float32)
    jh = np.arange(HIST_ROWS)[None, :]
    pcur = np.stack([((j <= i) & (i - j < w)) for w in POOL_WINDOWS]).astype(np.float32)
    phist = np.stack([(i + HIST_ROWS - jh < w) for w in POOL_WINDOWS]).astype(np.float32)
    return dict(expand=jnp.asarray(expand, bf16), expand_t=jnp.asarray(expand.T, bf16),
                causal=jnp.asarray(causal, bf16), pcur=jnp.asarray(pcur, bf16),
                phist=jnp.asarray(phist, bf16))


def _softplus(x):
    return jnp.maximum(x, 0.0) + jnp.log(1.0 + jnp.exp(-jnp.abs(x)))


def _conv_silu(ext_ref, cw_ref, cb_ref, first_row):
    acc = cb_ref[...] + cw_ref[0:1, :] * ext_ref[pl.ds(first_row, BLK), :]
    for k in range(1, CONV_WIDTH):
        acc = acc + cw_ref[k:k + 1, :] * ext_ref[pl.ds(first_row + k, BLK), :]
    return acc * jax.nn.sigmoid(acc)


def _ssd_intra(xbc_c, dt_raw, dtb_ref, alog_ref, causal_bf, expand_ref):
    f32 = jnp.float32
    xs = xbc_c[:, :D_SSD]
    bm = xbc_c[:, D_SSD:D_SSD + N_SSD_GROUPS * D_STATE]
    cm = xbc_c[:, D_SSD + N_SSD_GROUPS * D_STATE:]
    dt = _softplus(dt_raw + dtb_ref[...])
    a = dt * (-jnp.exp(alog_ref[...]))
    a_cum = _dot_sel_left(causal_bf, a, 3)
    dt_x = _dot_sel_right(dt, expand_ref[...], 2)
    return xs, bm, cm, dt, a_cum, xs * dt_x


def _ssd_diag_group(g, cb, a_cum, a_cum_t, keep, xdt):
    f32, bf16 = jnp.float32, jnp.bfloat16
    hg = N_SSD_HEADS // N_SSD_GROUPS
    lane = lax.broadcasted_iota(jnp.int32, (BLK, LANES), 1)
    first_head = lane < SSD_HEAD_DIM
    neg = jnp.float32(-jnp.inf)
    outs = []
    for pr in range(hg * SSD_HEAD_DIM // LANES):
        h1 = g * hg + 2 * pr
        blk = (g * hg * SSD_HEAD_DIM) // LANES + pr
        xp = xdt[:, blk * LANES:(blk + 1) * LANES]
        x1 = jnp.where(first_head, xp, 0.0).astype(bf16)
        x2 = jnp.where(first_head, 0.0, xp).astype(bf16)
        m1 = (cb * jnp.exp(jnp.where(keep, a_cum[:, h1:h1 + 1] - a_cum_t[h1:h1 + 1, :], neg))).astype(bf16)
        m2 = (cb * jnp.exp(jnp.where(keep, a_cum[:, h1 + 1:h1 + 2] - a_cum_t[h1 + 1:h1 + 2, :], neg))).astype(bf16)
        outs.append(jnp.dot(m1, x1, preferred_element_type=f32) + jnp.dot(m2, x2, preferred_element_type=f32))
    return jnp.concatenate(outs, axis=1)


def _gated_norm(y, z, nw_ref):
    yg = y * (z * jax.nn.sigmoid(z))
    return yg * lax.rsqrt(jnp.mean(yg * yg, axis=-1, keepdims=True) + EPS) * nw_ref[...]


def _prompt_mixer_body(z_ref, xbc_ref, dt_ref, u_ref, cw_ref, cb_ref, dtb_ref, alog_ref, dskip_ref, nw_ref,
                       pw_ref, ps_ref, causal_ref, expand_ref, expand_t_ref, pcur_ref, phist_ref,
                       mix_ref, ssm_ref, ext_ref, pool_tail_ref, state_ref):
    f32, bf16 = jnp.float32, jnp.bfloat16
    c = pl.program_id(1)
    gw = D_SSD // N_SSD_GROUPS

    @pl.when(c == 0)
    def _():
        ext_ref[0:CONV_TAIL_ROWS, :] = jnp.zeros((CONV_TAIL_ROWS, D_CONV), f32)
        pool_tail_ref[...] = jnp.zeros(pool_tail_ref.shape, f32)
        state_ref[...] = jnp.zeros(state_ref.shape, f32)

    ext_ref[CONV_TAIL_ROWS:CONV_TAIL_ROWS + BLK, :] = xbc_ref[...]
    xbc_c = _conv_silu(ext_ref, cw_ref, cb_ref, CONV_TAIL_ROWS - (CONV_WIDTH - 1))
    ext_ref[0:CONV_TAIL_ROWS, :] = xbc_ref[BLK - CONV_TAIL_ROWS:BLK, :]

    causal_bf = causal_ref[...]
    keep = causal_bf > 0
    xs, bm, cm, dt, a_cum, xdt = _ssd_intra(xbc_c, dt_ref[...], dtb_ref, alog_ref, causal_bf, expand_ref)
    a_cum_t = jnp.transpose(a_cum)
    a_tot = a_cum[BLK - 1:BLK, :]
    ea_x = _dot_sel_right(jnp.exp(a_cum), expand_ref[...], 2)
    dte_x = _dot_sel_right(jnp.exp(a_tot - a_cum), expand_ref[...], 2)
    cd_t = jnp.broadcast_to(jnp.exp(a_cum_t[:, BLK - 1:BLK]), (LANES, LANES))
    cd_col = _dot_sel_left(expand_t_ref[...], cd_t, 2)[:, 0:1]

    y_parts = []
    for g in range(N_SSD_GROUPS):
        cg = cm[:, g * D_STATE:(g + 1) * D_STATE].astype(bf16)
        bg = bm[:, g * D_STATE:(g + 1) * D_STATE].astype(bf16)
        cb = lax.dot_general(cg, bg, NT_DIMS, preferred_element_type=f32)
        y_diag = _ssd_diag_group(g, cb, a_cum, a_cum_t, keep, xdt)
        sg = state_ref[g * gw:(g + 1) * gw, :]
        y_off = lax.dot_general(cg, sg.astype(bf16), NT_DIMS, preferred_element_type=f32)
        y_parts.append(y_diag + y_off * ea_x[:, g * gw:(g + 1) * gw])
        xdte_t = jnp.transpose(xdt[:, g * gw:(g + 1) * gw] * dte_x[:, g * gw:(g + 1) * gw]).astype(bf16)
        state_ref[g * gw:(g + 1) * gw, :] = (sg * cd_col[g * gw:(g + 1) * gw, :]
                                             + jnp.dot(xdte_t, bg, preferred_element_type=f32))
    y = jnp.concatenate(y_parts, axis=1) + xs * dskip_ref[...]
    mix_ref[:, 0:D_SSD] = _gated_norm(y, z_ref[...], nw_ref).astype(bf16)

    @pl.when(c == pl.num_programs(1) - 1)
    def _():
        ssm_ref[0] = state_ref[...].reshape(N_SSD_HEADS, SSD_HEAD_DIM, D_STATE)

    u = u_ref[...]
    tail = pool_tail_ref[...]
    pos = (c * BLK + lax.broadcasted_iota(jnp.int32, (BLK, 1), 0) + 1).astype(f32)
    for gi, w in enumerate(POOL_WINDOWS):
        sl = slice(gi * POOL_GROUP_DIM, (gi + 1) * POOL_GROUP_DIM)
        ug = u[:, sl]
        wsum = _dot_sel_left(pcur_ref[gi], ug, 2) + _dot_sel_left(phist_ref[gi], tail[:, sl], 2)
        pooled = wsum / jnp.minimum(pos, jnp.float32(w)) - ug
        po = jnp.dot(pooled.astype(bf16), pw_ref[gi], preferred_element_type=f32) * ps_ref[:, sl]
        mix_ref[:, D_SSD + gi * POOL_GROUP_DIM:D_SSD + (gi + 1) * POOL_GROUP_DIM] = po.astype(bf16)
    pool_tail_ref[...] = u_ref[BLK - HIST_ROWS:BLK, :]


def _prompt_mixer(z, xbc, dt, u, n_seq, seq_len, conv_w, conv_b, dt_bias, A_log, D_skip, ssd_norm_w, pool_w,
                  pool_scale):
    f32, bf16 = jnp.float32, jnp.bfloat16
    n = n_seq * seq_len
    n_blk = seq_len // BLK
    k = _mixer_constants()

    def row_blk(width):
        return pl.BlockSpec((BLK, width), lambda b, c: (b * n_blk + c, 0))

    def const(shape):
        return pl.BlockSpec(shape, lambda b, c: (0,) * len(shape))

    pad_h = (0, LANES - N_SSD_HEADS)
    return pl.pallas_call(
        _prompt_mixer_body,
        grid=(n_seq, n_blk),
        in_specs=[row_blk(D_SSD), row_blk(D_CONV), row_blk(LANES), row_blk(D_POOL),
                  const((CONV_WIDTH, D_CONV)), const((1, D_CONV)), const((1, LANES)), const((1, LANES)),
                  const((1, D_SSD)), const((1, D_SSD)),
                  const((N_POOL_GROUPS, POOL_GROUP_DIM, POOL_GROUP_DIM)), const((1, D_POOL)),
                  const((BLK, BLK)), const((LANES, D_SSD)), const((D_SSD, LANES)),
                  const((N_POOL_GROUPS, BLK, BLK)), const((N_POOL_GROUPS, BLK, HIST_ROWS))],
        out_specs=[pl.BlockSpec((BLK, D_MIX), lambda b, c: (b * n_blk + c, 0)),
                   pl.BlockSpec((1, N_SSD_HEADS, SSD_HEAD_DIM, D_STATE), lambda b, c: (b, 0, 0, 0))],
        out_shape=[jax.ShapeDtypeStruct((n, D_MIX), bf16),
                   jax.ShapeDtypeStruct((n_seq, N_SSD_HEADS, SSD_HEAD_DIM, D_STATE), f32)],
        scratch_shapes=[pltpu.VMEM((CONV_TAIL_ROWS + BLK, D_CONV), f32),
                        pltpu.VMEM((HIST_ROWS, D_POOL), f32),
                        pltpu.VMEM((D_SSD, D_STATE), f32)],
        compiler_params=pltpu.CompilerParams(dimension_semantics=("parallel", "arbitrary"),
                                             vmem_limit_bytes=VMEM_LIMIT_BYTES),
        name="prompt_mixer",
    )(z, xbc, dt, u, conv_w.astype(f32), conv_b.reshape(1, D_CONV).astype(f32),
      jnp.pad(dt_bias.astype(f32), pad_h).reshape(1, LANES), jnp.pad(A_log.astype(f32), pad_h).reshape(1, LANES),
      jnp.repeat(D_skip.astype(f32), SSD_HEAD_DIM).reshape(1, D_SSD), ssd_norm_w.reshape(1, D_SSD).astype(f32),
      pool_w.astype(bf16), pool_scale.reshape(1, D_POOL).astype(f32),
      k["causal"], k["expand"], k["expand_t"], k["pcur"], k["phist"])


SEQ_PER_BLK = BLK // DEC_SEQ


def _sample_constants():
    bf16 = jnp.bfloat16
    r = np.arange(BLK)
    sq, st = r // DEC_SEQ, r % DEC_SEQ
    same = sq[:, None] == sq[None, :]
    causal = same & (st[None, :] <= st[:, None])
    nk = CONV_WIDTH - 1
    shift = np.stack([same & (st[None, :] == st[:, None] + k - nk) for k in range(nk)])
    cs = np.arange(SEQ_PER_BLK * nk)
    stsel = np.stack([(cs[None, :] // nk == sq[:, None]) & (cs[None, :] % nk == st[:, None] + k)
                      for k in range(nk)])
    pcur = np.stack([causal & (st[:, None] - st[None, :] < w) for w in POOL_WINDOWS])
    hs = np.arange(SEQ_PER_BLK * POOL_HIST)
    phist = np.stack([(hs[None, :] // POOL_HIST == sq[:, None])
                      & (st[:, None] + POOL_HIST - hs[None, :] % POOL_HIST < w) for w in POOL_WINDOWS])
    as_bf = lambda a: jnp.asarray(a.astype(np.float32), bf16)
    return dict(same=as_bf(same), causal=as_bf(causal), shift=as_bf(shift), stsel=as_bf(stsel),
                pcur=as_bf(pcur), phist=as_bf(phist))


def _sample_mixer_body(z_ref, xbc_ref, dt_ref, u_ref, cst_ref, pst_ref, ssm_in_ref,
                       cw_ref, cb_ref, dtb_ref, alog_ref, dskip_ref, nw_ref, pw_ref, ps_ref,
                       causal_ref, same_ref, expand_ref, expand_t_ref, shift_ref, stsel_ref, pcur_ref, phist_ref,
                       mix_ref, ssm_out_ref,
                       ydiag_ref, ea_ref, yt_ref, cdh_ref, cdl_ref, xdte_t_ref, bm_ref, cm_ref, *, pos0):
    f32, bf16 = jnp.float32, jnp.bfloat16
    s = pl.program_id(1)
    gw = D_SSD // N_SSD_GROUPS

    @pl.when(s == 0)
    def _():
        xbc = xbc_ref[...]
        cst = cst_ref[...]
        acc = cb_ref[...] + cw_ref[CONV_WIDTH - 1:CONV_WIDTH, :] * xbc
        for k in range(CONV_WIDTH - 1):
            tap = _dot_sel_left(shift_ref[k], xbc, 3) + _dot_sel_left(stsel_ref[k], cst, 3)
            acc = acc + cw_ref[k:k + 1, :] * tap
        xbc_c = acc * jax.nn.sigmoid(acc)

        causal_bf = causal_ref[...]
        keep = causal_bf > 0
        xs, bm, cm, dt, a_cum, xdt = _ssd_intra(xbc_c, dt_ref[...], dtb_ref, alog_ref, causal_bf, expand_ref)
        a_tot = _dot_sel_left(same_ref[...], dt * (-jnp.exp(alog_ref[...])), 3)
        a_cum_t = jnp.transpose(a_cum)
        ea_ref[...] = _dot_sel_right(jnp.exp(a_cum), expand_ref[...], 2)
        dte_x = _dot_sel_right(jnp.exp(a_tot - a_cum), expand_ref[...], 2)
        cd_col = _dot_sel_left(expand_t_ref[...], jnp.exp(jnp.transpose(a_tot)), 2)
        cd_hi, cd_lo = _split2(cd_col)
        cdh_ref[...] = cd_hi
        cdl_ref[...] = cd_lo
        bm_ref[...] = bm.astype(bf16)
        cm_ref[...] = cm.astype(bf16)
        for g in range(N_SSD_GROUPS):
            cg = cm[:, g * D_STATE:(g + 1) * D_STATE].astype(bf16)
            bg = bm[:, g * D_STATE:(g + 1) * D_STATE].astype(bf16)
            cb = lax.dot_general(cg, bg, NT_DIMS, preferred_element_type=f32)
            y_diag = _ssd_diag_group(g, cb, a_cum, a_cum_t, keep, xdt)
            ydiag_ref[:, g * gw:(g + 1) * gw] = y_diag + xs[:, g * gw:(g + 1) * gw] * dskip_ref[:, g * gw:(g + 1) * gw]
            xdte_t_ref[g * gw:(g + 1) * gw, :] = jnp.transpose(
                xdt[:, g * gw:(g + 1) * gw] * dte_x[:, g * gw:(g + 1) * gw]).astype(bf16)
        yt_ref[...] = jnp.zeros(yt_ref.shape, f32)

        u = u_ref[...]
        pst = pst_ref[...]
        step = lax.broadcasted_iota(jnp.int32, (BLK, 1), 0) % DEC_SEQ
        pos = (step + (pos0 + 1)).astype(f32)
        for gi, w in enumerate(POOL_WINDOWS):
            sl = slice(gi * POOL_GROUP_DIM, (gi + 1) * POOL_GROUP_DIM)
            ug = u[:, sl]
            wsum = _dot_sel_left(pcur_ref[gi], ug, 2) + _dot_sel_left(phist_ref[gi], pst[:, sl], 2)
            pooled = wsum / jnp.minimum(pos, jnp.float32(w)) - ug
            po = jnp.dot(pooled.astype(bf16), pw_ref[gi], preferred_element_type=f32) * ps_ref[:, sl]
            mix_ref[:, D_SSD + gi * POOL_GROUP_DIM:D_SSD + (gi + 1) * POOL_GROUP_DIM] = po.astype(bf16)

    rows_of_s = lax.broadcasted_iota(jnp.int32, (BLK, LANES), 0) // DEC_SEQ == s
    cols_of_s = lax.broadcasted_iota(jnp.int32, (gw, BLK), 1) // DEC_SEQ == s
    pick_s = jnp.where(lax.broadcasted_iota(jnp.int32, (BLK, LANES), 0) == DEC_SEQ * s, 1.0, 0.0).astype(bf16)
    state = ssm_in_ref[0].reshape(D_SSD, D_STATE)
    for g in range(N_SSD_GROUPS):
        rs = slice(g * gw, (g + 1) * gw)
        sg = state[rs, :]
        cg = cm_ref[:, g * D_STATE:(g + 1) * D_STATE]
        bg = bm_ref[:, g * D_STATE:(g + 1) * D_STATE]
        yt = lax.dot_general(sg.astype(bf16), cg, NT_DIMS, preferred_element_type=f32)
        yt_ref[rs, :] += jnp.where(cols_of_s, yt, 0.0)
        cd = (jnp.dot(cdh_ref[rs, :], pick_s, preferred_element_type=f32)
              + jnp.dot(cdl_ref[rs, :], pick_s, preferred_element_type=f32))
        upd = jnp.dot(xdte_t_ref[rs, :], jnp.where(rows_of_s, bg, jnp.zeros_like(bg)),
                      preferred_element_type=f32)
        ssm_out_ref[0, g * (N_SSD_HEADS // N_SSD_GROUPS):(g + 1) * (N_SSD_HEADS // N_SSD_GROUPS)] = (
            sg * cd + upd).reshape(N_SSD_HEADS // N_SSD_GROUPS, SSD_HEAD_DIM, D_STATE)

    @pl.when(s == pl.num_programs(1) - 1)
    def _():
        y = ydiag_ref[...] + jnp.transpose(yt_ref[...]) * ea_ref[...]
        mix_ref[:, 0:D_SSD] = _gated_norm(y, z_ref[...], nw_ref).astype(bf16)


def _sample_mixer(z, xbc, dt, u, row0, n_seq, state_conv, state_ssm, state_pool, pos0,
                  conv_w, conv_b, dt_bias, A_log, D_skip, ssd_norm_w, pool_w, pool_scale):
    f32, bf16 = jnp.float32, jnp.bfloat16
    n_blk = n_seq // SEQ_PER_BLK
    blk0 = row0 // BLK
    nk = CONV_WIDTH - 1
    k = _mixer_constants()
    ks = _sample_constants()

    def row_blk(width):
        return pl.BlockSpec((BLK, width), lambda j, s: (blk0 + j, 0))

    def const(shape):
        return pl.BlockSpec(shape, lambda j, s: (0,) * len(shape))

    state_spec = pl.BlockSpec((1, N_SSD_HEADS, SSD_HEAD_DIM, D_STATE), lambda j, s: (j * SEQ_PER_BLK + s, 0, 0, 0))
    pad_h = (0, LANES - N_SSD_HEADS)
    return pl.pallas_call(
        functools.partial(_sample_mixer_body, pos0=pos0),
        grid=(n_blk, SEQ_PER_BLK),
        in_specs=[row_blk(D_SSD), row_blk(D_CONV), row_blk(LANES), row_blk(D_POOL),
                  pl.BlockSpec((SEQ_PER_BLK * nk, D_CONV), lambda j, s: (j, 0)),
                  pl.BlockSpec((SEQ_PER_BLK * POOL_HIST, D_POOL), lambda j, s: (j, 0)),
                  state_spec,
                  const((CONV_WIDTH, D_CONV)), const((1, D_CONV)), const((1, LANES)), const((1, LANES)),
                  const((1, D_SSD)), const((1, D_SSD)),
                  const((N_POOL_GROUPS, POOL_GROUP_DIM, POOL_GROUP_DIM)), const((1, D_POOL)),
                  const((BLK, BLK)), const((BLK, BLK)), const((LANES, D_SSD)), const((D_SSD, LANES)),
                  const((nk, BLK, BLK)), const((nk, BLK, SEQ_PER_BLK * nk)),
                  const((N_POOL_GROUPS, BLK, BLK)), const((N_POOL_GROUPS, BLK, SEQ_PER_BLK * POOL_HIST))],
        out_specs=[pl.BlockSpec((BLK, D_MIX), lambda j, s: (j, 0)), state_spec],
        out_shape=[jax.ShapeDtypeStruct((n_seq * DEC_SEQ, D_MIX), bf16),
                   jax.ShapeDtypeStruct((n_seq, N_SSD_HEADS, SSD_HEAD_DIM, D_STATE), f32)],
        scratch_shapes=[pltpu.VMEM((BLK, D_SSD), f32), pltpu.VMEM((BLK, D_SSD), f32),
                        pltpu.VMEM((D_SSD, BLK), f32), pltpu.VMEM((D_SSD, BLK), bf16),
                        pltpu.VMEM((D_SSD, BLK), bf16), pltpu.VMEM((D_SSD, BLK), bf16),
                        pltpu.VMEM((BLK, N_SSD_GROUPS * D_STATE), bf16),
                        pltpu.VMEM((BLK, N_SSD_GROUPS * D_STATE), bf16)],
        compiler_params=pltpu.CompilerParams(dimension_semantics=("parallel", "arbitrary"),
                                             vmem_limit_bytes=VMEM_LIMIT_BYTES),
        name="sample_mixer",
    )(z, xbc, dt, u, state_conv.reshape(n_seq * nk, D_CONV), state_pool.reshape(n_seq * POOL_HIST, D_POOL),
      state_ssm, conv_w.astype(f32), conv_b.reshape(1, D_CONV).astype(f32),
      jnp.pad(dt_bias.astype(f32), pad_h).reshape(1, LANES), jnp.pad(A_log.astype(f32), pad_h).reshape(1, LANES),
      jnp.repeat(D_skip.astype(f32), SSD_HEAD_DIM).reshape(1, D_SSD), ssd_norm_w.reshape(1, D_SSD).astype(f32),
      pool_w.astype(bf16), pool_scale.reshape(1, D_POOL).astype(f32),
      ks["causal"], ks["same"], k["expand"], k["expand_t"], ks["shift"], ks["stsel"],
      ks["pcur"], ks["phist"])


def _moe_sizes(n_tokens, tm):
    nt = n_tokens // tm
    lmax = -(-(TOP_K * tm + N_EXPERTS * (MOE_SEG_ROWS - 1)) // MOE_CHUNK) * MOE_CHUNK
    rows = TOP_K * n_tokens + nt * N_EXPERTS * (MOE_SEG_ROWS - 1) + N_EXPERTS * (MOE_ROW_TILE - 1)
    n_row_tiles = -(-rows // MOE_ROW_TILE)
    return nt, lmax, n_row_tiles


def _router_body(x_ref, nw_ref, rwh_ref, rwl_ref, rb_ref, h_ref, posg_ref, post_ref, cnt_ref):
    f32, bf16 = jnp.float32, jnp.bfloat16
    tm = x_ref.shape[0]
    x = x_ref[...]
    h = x * lax.rsqrt(jnp.mean(x * x, axis=-1, keepdims=True) + EPS) * nw_ref[...]
    h_hi = h.astype(bf16)
    h_ref[...] = h_hi
    h_lo = (h - h_hi.astype(f32)).astype(bf16)
    wh = rwh_ref[...]
    logits = (jnp.dot(h_hi, wh, preferred_element_type=f32)
              + jnp.dot(h_lo, wh, preferred_element_type=f32)
              + jnp.dot(h_hi, rwl_ref[...], preferred_element_type=f32)) + rb_ref[...]
    lane = lax.broadcasted_iota(jnp.int32, (tm, LANES), 1)
    lanef = lane.astype(f32)
    neg = jnp.float32(-jnp.inf)
    l = jnp.where(lane < N_EXPERTS, logits, neg)
    sels, vals = [], []
    for _ in range(TOP_K):
        m = jnp.max(l, axis=1, keepdims=True)
        idx = jnp.min(jnp.where(l == m, lanef, jnp.float32(LANES)), axis=1, keepdims=True)
        sel = lanef == idx
        l = jnp.where(sel, neg, l)
        sels.append(sel)
        vals.append(m)
    exps = [jnp.exp(v - vals[0]) for v in vals]
    denom = exps[0] + exps[1] + exps[2] + exps[3]
    gates = [e / denom for e in exps]
    chosen = jnp.where(sels[0] | sels[1] | sels[2] | sels[3], 1.0, 0.0).astype(f32)
    row = lax.broadcasted_iota(jnp.int32, (tm, tm), 0)
    col = lax.broadcasted_iota(jnp.int32, (tm, tm), 1)
    lower = jnp.where(col < row, 1.0, 0.0).astype(bf16)
    rank = jnp.dot(lower, chosen.astype(bf16), preferred_element_type=f32)
    cnt = jnp.sum(chosen, axis=0, keepdims=True)
    seg_units = jnp.floor((cnt + (MOE_SEG_ROWS - 1)) * (1.0 / MOE_SEG_ROWS))
    r2 = lax.broadcasted_iota(jnp.int32, (LANES, LANES), 0)
    c2 = lax.broadcasted_iota(jnp.int32, (LANES, LANES), 1)
    upper = jnp.where(r2 < c2, 1.0, 0.0).astype(bf16)
    lstart = jnp.dot(jnp.broadcast_to(seg_units, (8, LANES)).astype(bf16), upper,
                     preferred_element_type=f32)[0:1, :] * MOE_SEG_ROWS
    posmat = lstart + rank
    posg = jnp.zeros((tm, LANES), f32)
    for k in range(TOP_K):
        pos_k = jnp.sum(jnp.where(sels[k], posmat, 0.0), axis=1, keepdims=True)
        posg = posg + jnp.where(lane == k, pos_k, 0.0) + jnp.where(lane == TOP_K + k, gates[k], 0.0)
    posg_ref[...] = posg
    post_ref[...] = jnp.transpose(posg)[0:8, :]
    cnt_ref[0] = jnp.broadcast_to(cnt, (8, LANES)).astype(jnp.int32)


def _moe_router(x1, norm2_w, router_w, router_b, tm):
    n, d = x1.shape
    nt = n // tm
    f32, bf16 = jnp.float32, jnp.bfloat16
    rw = jnp.pad(router_w.astype(f32), ((0, 0), (0, LANES - N_EXPERTS)))
    rw_hi = rw.astype(bf16)
    rw_lo = (rw - rw_hi.astype(f32)).astype(bf16)
    rb = jnp.pad(router_b.astype(f32), (0, LANES - N_EXPERTS)).reshape(1, LANES)
    return pl.pallas_call(
        _router_body,
        grid=(nt,),
        in_specs=[pl.BlockSpec((tm, d), lambda i: (i, 0)),
                  pl.BlockSpec((1, d), lambda i: (0, 0)),
                  pl.BlockSpec((d, LANES), lambda i: (0, 0)),
                  pl.BlockSpec((d, LANES), lambda i: (0, 0)),
                  pl.BlockSpec((1, LANES), lambda i: (0, 0))],
        out_specs=[pl.BlockSpec((tm, d), lambda i: (i, 0)),
                   pl.BlockSpec((tm, LANES), lambda i: (i, 0)),
                   pl.BlockSpec((8, tm), lambda i: (0, i)),
                   pl.BlockSpec((1, 8, LANES), lambda i: (i, 0, 0))],
        out_shape=[jax.ShapeDtypeStruct((n, d), bf16),
                   jax.ShapeDtypeStruct((n, LANES), f32),
                   jax.ShapeDtypeStruct((8, n), f32),
                   jax.ShapeDtypeStruct((nt, 8, LANES), jnp.int32)],
        compiler_params=pltpu.CompilerParams(dimension_semantics=("parallel",),
                                             vmem_limit_bytes=VMEM_LIMIT_BYTES),
        name="moe_router",
    )(x1, norm2_w.reshape(1, d).astype(f32), rw_hi, rw_lo, rb)


def _moe_plan(cnt, n_row_tiles):
    i32 = jnp.int32
    pad = (cnt + (MOE_SEG_ROWS - 1)) // MOE_SEG_ROWS * MOE_SEG_ROWS
    lstart = jnp.cumsum(pad, axis=1) - pad
    lp = jnp.sum(pad, axis=1)
    tot = jnp.sum(pad, axis=0)
    reg = (tot + (MOE_ROW_TILE - 1)) // MOE_ROW_TILE * MOE_ROW_TILE
    reg_end = jnp.cumsum(reg)
    estart = reg_end - reg
    seg = estart[None, :] + jnp.cumsum(pad, axis=0) - pad
    tiles_end = reg_end // MOE_ROW_TILE
    n_active = tiles_end[-1]
    j = jnp.arange(n_row_tiles, dtype=i32)
    jc = jnp.minimum(j, n_active - 1)
    tile_e = jnp.sum((tiles_end[None, :] <= jc[:, None]).astype(i32), axis=1)
    prev_e = jnp.concatenate([jnp.full((1,), -1, i32), tile_e[:-1]])
    tile_first = ((tile_e != prev_e) & (j < n_active)).astype(i32)
    return dict(
        lstart=lstart.reshape(-1).astype(i32), seg_units=(pad // MOE_SEG_ROWS).reshape(-1).astype(i32),
        seg=seg.reshape(-1).astype(i32), lp=lp.astype(i32),
        tail_start=(estart + tot).astype(i32), tail_units=((reg - tot) // MOE_SEG_ROWS).astype(i32),
        tile_e=tile_e, tile_blk=jc.astype(i32), tile_first=tile_first,
        n_active=n_active.reshape(1).astype(i32))


def _for_each_segment_copy(i, lstart_ref, units_ref, seg_ref, local_ref, global_ref, sem, to_global, fn):
    def per_expert(e, carry):
        k = i * N_EXPERTS + e
        ls, gs = lstart_ref[k], seg_ref[k]

        def per_piece(j, c):
            lo = pl.multiple_of(ls + j * MOE_SEG_ROWS, MOE_SEG_ROWS)
            go = pl.multiple_of(gs + j * MOE_SEG_ROWS, MOE_SEG_ROWS)
            loc = local_ref.at[pl.ds(lo, MOE_SEG_ROWS)]
            glo = global_ref.at[pl.ds(go, MOE_SEG_ROWS)]
            fn(pltpu.make_async_copy(loc, glo, sem) if to_global else pltpu.make_async_copy(glo, loc, sem))
            return c
        return lax.fori_loop(0, units_ref[k], per_piece, carry)
    lax.fori_loop(0, N_EXPERTS, per_expert, 0)


def _dispatch_body(lstart_ref, units_ref, seg_ref, lp_ref, tail_start_ref, tail_units_ref, n_active_ref,
                   h_ref, post_ref, xs_ref, stage_ref, zero_ref, sem):
    f32, bf16 = jnp.float32, jnp.bfloat16
    i = pl.program_id(0)
    tm = h_ref.shape[0]
    lmax = stage_ref.shape[0]
    n_row_tiles = xs_ref.shape[0] // MOE_ROW_TILE

    def for_each_fill_copy(fn):
        def per_expert(e, carry):
            ts = tail_start_ref[e]

            def per_piece(j, c):
                go = pl.multiple_of(ts + j * MOE_SEG_ROWS, MOE_SEG_ROWS)
                fn(pltpu.make_async_copy(zero_ref.at[pl.ds(0, MOE_SEG_ROWS)],
                                         xs_ref.at[pl.ds(go, MOE_SEG_ROWS)], sem))
                return c
            return lax.fori_loop(0, tail_units_ref[e], per_piece, carry)
        lax.fori_loop(0, N_EXPERTS, per_expert, 0)

        def per_unused_tile(j, c):
            go = pl.multiple_of(j * MOE_ROW_TILE, MOE_ROW_TILE)
            fn(pltpu.make_async_copy(zero_ref, xs_ref.at[pl.ds(go, MOE_ROW_TILE)], sem))
            return c
        lax.fori_loop(n_active_ref[0], n_row_tiles, per_unused_tile, 0)

    @pl.when(i == 0)
    def _():
        zero_ref[...] = jnp.zeros(zero_ref.shape, bf16)
        for_each_fill_copy(lambda cp: cp.start())
        for_each_fill_copy(lambda cp: cp.wait())

    pos = post_ref[0:TOP_K, :]
    h = h_ref[...]
    for c in range(lmax // MOE_CHUNK):
        @pl.when(c * MOE_CHUNK < lp_ref[i])
        def _():
            r = (lax.broadcasted_iota(jnp.int32, (MOE_CHUNK, tm), 0) + c * MOE_CHUNK).astype(f32)
            hit = (pos[0:1, :] == r) | (pos[1:2, :] == r) | (pos[2:3, :] == r) | (pos[3:4, :] == r)
            sel = jnp.where(hit, 1.0, 0.0).astype(bf16)
            stage_ref[c * MOE_CHUNK:(c + 1) * MOE_CHUNK, :] = jnp.dot(
                sel, h, preferred_element_type=f32).astype(bf16)

    args = (i, lstart_ref, units_ref, seg_ref, stage_ref, xs_ref, sem, True)
    _for_each_segment_copy(*args, lambda cp: cp.start())
    _for_each_segment_copy(*args, lambda cp: cp.wait())


def _moe_dispatch(h2, post, plan, tm, lmax, n_row_tiles):
    n, d = h2.shape
    nt = n // tm
    grid_spec = pltpu.PrefetchScalarGridSpec(
        num_scalar_prefetch=7,
        grid=(nt,),
        in_specs=[pl.BlockSpec((tm, d), lambda i, *_: (i, 0)),
                  pl.BlockSpec((8, tm), lambda i, *_: (0, i))],
        out_specs=pl.BlockSpec(memory_space=pl.ANY),
        scratch_shapes=[pltpu.VMEM((lmax, d), jnp.bfloat16),
                        pltpu.VMEM((MOE_ROW_TILE, d), jnp.bfloat16),
                        pltpu.SemaphoreType.DMA(())])
    return pl.pallas_call(
        _dispatch_body,
        grid_spec=grid_spec,
        out_shape=jax.ShapeDtypeStruct((n_row_tiles * MOE_ROW_TILE, d), jnp.bfloat16),
        compiler_params=pltpu.CompilerParams(dimension_semantics=("arbitrary",),
                                             vmem_limit_bytes=VMEM_LIMIT_BYTES),
        name="moe_dispatch",
    )(plan["lstart"], plan["seg_units"], plan["seg"], plan["lp"], plan["tail_start"], plan["tail_units"],
      plan["n_active"], h2, post)


def _experts_body(tile_e_ref, tile_blk_ref, tile_first_ref, n_active_ref,
                  xs_ref, wgu_ref, bgu_ref, wd_ref, bd_ref, os_ref, wgu_bf, wd_bf):
    f32, bf16 = jnp.float32, jnp.bfloat16
    j = pl.program_id(0)

    @pl.when(j < n_active_ref[0])
    def _():
        @pl.when(tile_first_ref[j] == 1)
        def _():
            wgu_bf[...] = wgu_ref[0].astype(bf16)
            wd_bf[...] = wd_ref[0].astype(bf16)

        x = xs_ref[...]
        gu = jnp.dot(x, wgu_bf[...], preferred_element_type=f32) + bgu_ref[0]
        gate = jnp.minimum(gu[:, :D_FF], SWIGLU_LIMIT)
        up = jnp.clip(gu[:, D_FF:], -SWIGLU_LIMIT, SWIGLU_LIMIT)
        act = (up + 1.0) * (gate * jax.nn.sigmoid(SWIGLU_ALPHA * gate))
        out = jnp.dot(act.astype(bf16), wd_bf[...], preferred_element_type=f32) + bd_ref[0]
        os_ref[...] = out.astype(bf16)

    @pl.when(j >= n_active_ref[0])
    def _():
        os_ref[...] = jnp.zeros(os_ref.shape, bf16)


def _moe_experts(xs, plan, w_gate_up, b_gate_up, w_down, b_down, n_row_tiles):
    d = xs.shape[1]
    grid_spec = pltpu.PrefetchScalarGridSpec(
        num_scalar_prefetch=4,
        grid=(n_row_tiles,),
        in_specs=[pl.BlockSpec((MOE_ROW_TILE, d), lambda j, te, tb, tf, na: (tb[j], 0)),
                  pl.BlockSpec((1, d, 2 * D_FF), lambda j, te, tb, tf, na: (te[j], 0, 0)),
                  pl.BlockSpec((1, 1, 2 * D_FF), lambda j, te, tb, tf, na: (te[j], 0, 0)),
                  pl.BlockSpec((1, D_FF, d), lambda j, te, tb, tf, na: (te[j], 0, 0)),
                  pl.BlockSpec((1, 1, d), lambda j, te, tb, tf, na: (te[j], 0, 0))],
        out_specs=pl.BlockSpec((MOE_ROW_TILE, d), lambda j, te, tb, tf, na: (j, 0)),
        scratch_shapes=[pltpu.VMEM((d, 2 * D_FF), jnp.bfloat16),
                        pltpu.VMEM((D_FF, d), jnp.bfloat16)])
    return pl.pallas_call(
        _experts_body,
        grid_spec=grid_spec,
        out_shape=jax.ShapeDtypeStruct(xs.shape, jnp.bfloat16),
        compiler_params=pltpu.CompilerParams(dimension_semantics=("arbitrary",),
                                             vmem_limit_bytes=VMEM_LIMIT_BYTES),
        name="moe_experts",
    )(plan["tile_e"], plan["tile_blk"], plan["tile_first"], plan["n_active"],
      xs, w_gate_up, b_gate_up.reshape(N_EXPERTS, 1, 2 * D_FF), w_down, b_down.reshape(N_EXPERTS, 1, d))


def _combine_body(lstart_ref, units_ref, seg_ref, lp_ref,
                  os_ref, posg_ref, x_ref, fw_ref, yp_ref, ys_ref, stage_ref, acc_ref, sem,
                  *, n_prompt_tiles):
    f32, bf16 = jnp.float32, jnp.bfloat16
    i = pl.program_id(0)
    tm = x_ref.shape[0]
    lmax = stage_ref.shape[0]

    @pl.when(i == 0)
    def _():
        stage_ref[...] = jnp.zeros(stage_ref.shape, bf16)

    args = (i, lstart_ref, units_ref, seg_ref, stage_ref, os_ref, sem, False)
    _for_each_segment_copy(*args, lambda cp: cp.start())
    _for_each_segment_copy(*args, lambda cp: cp.wait())

    posg = posg_ref[...]
    pos = [posg[:, k:k + 1] for k in range(TOP_K)]
    gate = [posg[:, TOP_K + k:TOP_K + k + 1] for k in range(TOP_K)]
    acc_ref[...] = x_ref[...]
    for c in range(lmax // MOE_CHUNK):
        @pl.when(c * MOE_CHUNK < lp_ref[i])
        def _():
            r = (lax.broadcasted_iota(jnp.int32, (tm, MOE_CHUNK), 1) + c * MOE_CHUNK).astype(f32)
            w = jnp.zeros((tm, MOE_CHUNK), f32)
            for k in range(TOP_K):
                w = w + jnp.where(pos[k] == r, gate[k], 0.0)
            acc_ref[...] += jnp.dot(w.astype(bf16), stage_ref[c * MOE_CHUNK:(c + 1) * MOE_CHUNK, :],
                                    preferred_element_type=f32)

    y = acc_ref[...]
    out = y * lax.rsqrt(jnp.mean(y * y, axis=-1, keepdims=True) + EPS) * fw_ref[...]

    @pl.when(i < n_prompt_tiles)
    def _():
        yp_ref[...] = out

    @pl.when(i >= n_prompt_tiles)
    def _():
        ys_ref[...] = out


def _moe_combine(os_, posg, x1, final_norm_w, plan, tm, lmax, n_prompt):
    n, d = x1.shape
    nt = n // tm
    n_prompt_tiles = n_prompt // tm
    n_sample_tiles = nt - n_prompt_tiles
    grid_spec = pltpu.PrefetchScalarGridSpec(
        num_scalar_prefetch=4,
        grid=(nt,),
        in_specs=[pl.BlockSpec(memory_space=pl.ANY),
                  pl.BlockSpec((tm, LANES), lambda i, *_: (i, 0)),
                  pl.BlockSpec((tm, d), lambda i, *_: (i, 0)),
                  pl.BlockSpec((1, d), lambda i, *_: (0, 0))],
        out_specs=[pl.BlockSpec((tm, d), lambda i, *_: (jnp.minimum(i, n_prompt_tiles - 1), 0)),
                   pl.BlockSpec((tm, d), lambda i, *_: (jnp.maximum(i - n_prompt_tiles, 0), 0))],
        scratch_shapes=[pltpu.VMEM((lmax, d), jnp.bfloat16),
                        pltpu.VMEM((tm, d), jnp.float32),
                        pltpu.SemaphoreType.DMA(())])
    return pl.pallas_call(
        functools.partial(_combine_body, n_prompt_tiles=n_prompt_tiles),
        grid_spec=grid_spec,
        out_shape=[jax.ShapeDtypeStruct((n_prompt, d), jnp.float32),
                   jax.ShapeDtypeStruct((n_sample_tiles * tm, d), jnp.float32)],
        compiler_params=pltpu.CompilerParams(dimension_semantics=("arbitrary",),
                                             vmem_limit_bytes=VMEM_LIMIT_BYTES),
        name="moe_combine",
    )(plan["lstart"], plan["seg_units"], plan["seg"], plan["lp"],
      os_, posg, x1, final_norm_w.reshape(1, d).astype(jnp.float32))


def _moe_block(x1, n_prompt, norm2_w, router_w, router_b, w_gate_up, b_gate_up, w_down, b_down,
               final_norm_w, tm=MOE_TOKEN_TILE):
    n = x1.shape[0]
    nt, lmax, n_row_tiles = _moe_sizes(n, tm)
    h2, posg, post, cnt3 = _moe_router(x1, norm2_w, router_w, router_b, tm)
    plan = _moe_plan(cnt3[:, 0, :N_EXPERTS], n_row_tiles)
    xs = _moe_dispatch(h2, post, plan, tm, lmax, n_row_tiles)
    os_ = _moe_experts(xs, plan, w_gate_up, b_gate_up, w_down, b_down, n_row_tiles)
    return _moe_combine(os_, posg, x1, final_norm_w, plan, tm, lmax, n_prompt)


def kernel(x_prompt, x_sample, state_ssm, state_conv, state_pool, norm1_w, w_in, conv_w, conv_b, dt_bias,
           A_log, D_skip, ssd_norm_w, pool_w, pool_scale, w_out, norm2_w, router_w, router_b, w_gate_up,
           b_gate_up, w_down, b_down, final_norm_w):
    n_prompt = BATCH * SEQ
    n_sample = DEC_BATCH * DEC_SEQ
    xp = x_prompt.reshape(n_prompt, D_MODEL)
    xs = x_sample.reshape(n_sample, D_MODEL)
    z, xbc, dt_raw, u = _in_proj(xp, xs, norm1_w[0], w_in[0])
    mp = (conv_w[0], conv_b[0], dt_bias[0], A_log[0], D_skip[0], ssd_norm_w[0], pool_w[0], pool_scale[0])
    mix_p, s1 = _prompt_mixer(z, xbc, dt_raw, u, BATCH, SEQ, *mp)
    mix_s, s2 = _sample_mixer(z, xbc, dt_raw, u, n_prompt, DEC_BATCH, state_conv[0], state_ssm[0], state_pool[0],
                              PAST_LEN, *mp)
    nk = CONV_WIDTH - 1
    c1 = jnp.stack([xbc[(b + 1) * SEQ - nk:(b + 1) * SEQ] for b in range(BATCH)])
    p1 = jnp.stack([u[(b + 1) * SEQ - POOL_HIST:(b + 1) * SEQ] for b in range(BATCH)])
    c2 = xbc[n_prompt:].reshape(DEC_BATCH, DEC_SEQ, D_CONV)[:, DEC_SEQ - nk:]
    p2 = jnp.concatenate([state_pool[0][:, DEC_SEQ:], u[n_prompt:].reshape(DEC_BATCH, DEC_SEQ, D_POOL)], axis=1)
    x1 = _out_proj(mix_p, mix_s, w_out[0], xp, xs)
    yp, ys = _moe_block(x1, n_prompt, norm2_w[0], router_w[0], router_b[0], w_gate_up[0], b_gate_up[0],
                        w_down[0], b_down[0], final_norm_w)
    return (yp.reshape(x_prompt.shape), ys.reshape(x_sample.shape),
            s1[None], c1[None], p1[None], s2[None], c2[None], p2[None])
```

```python
import functools
import math
import jax, jax.numpy as jnp
from jax import lax
import numpy as np
from jax.experimental import pallas as pl
from jax.experimental.pallas import tpu as pltpu

D_MODEL = 1024
BATCH = 8
SEQ = 2048
DEC_BATCH = 128
DEC_SEQ = 4
PAST_LEN = 16384

D_MIX = 2 * D_MODEL
D_SSD = 3 * D_MIX // 4
SSD_HEAD_DIM = 64
N_SSD_HEADS = D_SSD // SSD_HEAD_DIM
N_SSD_GROUPS = 4
D_STATE = 128
CONV_WIDTH = 4
SSD_CHUNK = 128
D_CONV = D_SSD + 2 * N_SSD_GROUPS * D_STATE
D_POOL = D_MIX - D_SSD
POOL_WINDOWS = (2, 4, 8, 16)
N_POOL_GROUPS = len(POOL_WINDOWS)
POOL_GROUP_DIM = D_POOL // N_POOL_GROUPS
POOL_HIST = max(POOL_WINDOWS) - 1
D_IN_PROJ = D_SSD + D_CONV + N_SSD_HEADS + D_POOL
N_EXPERTS = 32
TOP_K = 4
D_FF = D_MODEL
SWIGLU_LIMIT = 7.0
SWIGLU_ALPHA = 1.702
EPS = 1e-5

LANES = 128
BF16_SUBLANES = 16
VMEM_LIMIT_BYTES = 48 * 1024 * 1024

MOE_TOKEN_TILE = 512
MOE_SEG_ROWS = BF16_SUBLANES
MOE_PIECE = 128
MOE_ROW_CHUNK = 512
MOE_CHUNK = 256
MOE_SEL_ROWS = 1280


BLK = SSD_CHUNK
PROJ_ROW_TILE = 512
HIST_ROWS = 16
CONV_TAIL_ROWS = 8
NT_DIMS = (((1,), (1,)), ((), ()))


def _split2(v):
    hi = v.astype(jnp.bfloat16)
    lo = (v - hi.astype(jnp.float32)).astype(jnp.bfloat16)
    return hi, lo


def _dot_sel_left(sel, v, passes):
    out = None
    rem = v
    for p in range(passes):
        part = rem.astype(jnp.bfloat16)
        d = jnp.dot(sel, part, preferred_element_type=jnp.float32)
        out = d if out is None else out + d
        if p + 1 < passes:
            rem = rem - part.astype(jnp.float32)
    return out


def _dot_sel_right(v, sel, passes):
    out = None
    rem = v
    for p in range(passes):
        part = rem.astype(jnp.bfloat16)
        d = jnp.dot(part, sel, preferred_element_type=jnp.float32)
        out = d if out is None else out + d
        if p + 1 < passes:
            rem = rem - part.astype(jnp.float32)
    return out


def _two_part_specs(n_first, n_second, tm, width):
    t1 = n_first // tm
    t2 = n_second // tm
    return (pl.BlockSpec((tm, width), lambda i: (jnp.minimum(i, t1 - 1), 0)),
            pl.BlockSpec((tm, width), lambda i: (jnp.clip(i - t1, 0, t2 - 1), 0)))


def _in_proj_body(xa_ref, xb_ref, nw_ref, w_ref, z_ref, xbc_ref, dt_ref, u_ref, *, tiles_a):
    x = jnp.where(pl.program_id(0) < tiles_a, xa_ref[...], xb_ref[...])
    h = (x * lax.rsqrt(jnp.mean(x * x, axis=-1, keepdims=True) + EPS) * nw_ref[...]).astype(jnp.bfloat16)
    off = 0
    for ref in (z_ref, xbc_ref, dt_ref, u_ref):
        n = ref.shape[1]
        ref[...] = jnp.dot(h, w_ref[:, off:off + n], preferred_element_type=jnp.float32)
        off += n


def _in_proj(xa, xb, norm1_w, w_in):
    d = xa.shape[1]
    n = xa.shape[0] + xb.shape[0]
    f32, bf16 = jnp.float32, jnp.bfloat16
    s1, s2 = D_SSD + D_CONV, D_SSD + D_CONV + N_SSD_HEADS
    w = jnp.concatenate([w_in[:, :s1], jnp.pad(w_in[:, s1:s2], ((0, 0), (0, LANES - N_SSD_HEADS))),
                         w_in[:, s2:]], axis=1).astype(bf16)
    widths = (D_SSD, D_CONV, LANES, D_POOL)
    tm = PROJ_ROW_TILE
    return pl.pallas_call(
        functools.partial(_in_proj_body, tiles_a=xa.shape[0] // tm),
        grid=(n // tm,),
        in_specs=[*_two_part_specs(xa.shape[0], xb.shape[0], tm, d),
                  pl.BlockSpec((1, d), lambda i: (0, 0)),
                  pl.BlockSpec((d, sum(widths)), lambda i: (0, 0), pipeline_mode=pl.Buffered(1))],
        out_specs=[pl.BlockSpec((tm, wd), lambda i: (i, 0)) for wd in widths],
        out_shape=[jax.ShapeDtypeStruct((n, wd), f32) for wd in widths],
        compiler_params=pltpu.CompilerParams(dimension_semantics=("parallel",),
                                             vmem_limit_bytes=VMEM_LIMIT_BYTES),
        name="in_proj",
    )(xa, xb, norm1_w.reshape(1, d).astype(f32), w)


def _out_proj_body(ma_ref, mb_ref, w_ref, xa_ref, xb_ref, o_ref, *, tiles_a):
    first = pl.program_id(0) < tiles_a
    m = jnp.where(first, ma_ref[...], mb_ref[...])
    x = jnp.where(first, xa_ref[...], xb_ref[...])
    o_ref[...] = x + jnp.dot(m, w_ref[...], preferred_element_type=jnp.float32)


def _out_proj(ma, mb, w_out, xa, xb):
    d = xa.shape[1]
    n = xa.shape[0] + xb.shape[0]
    tm = PROJ_ROW_TILE
    return pl.pallas_call(
        functools.partial(_out_proj_body, tiles_a=xa.shape[0] // tm),
        grid=(n // tm,),
        in_specs=[*_two_part_specs(xa.shape[0], xb.shape[0], tm, D_MIX),
                  pl.BlockSpec((D_MIX, d), lambda i: (0, 0)),
                  *_two_part_specs(xa.shape[0], xb.shape[0], tm, d)],
        out_specs=pl.BlockSpec((tm, d), lambda i: (i, 0)),
        out_shape=jax.ShapeDtypeStruct((n, d), jnp.float32),
        compiler_params=pltpu.CompilerParams(dimension_semantics=("parallel",),
                                             vmem_limit_bytes=VMEM_LIMIT_BYTES),
        name="out_proj",
    )(ma, mb, w_out.astype(jnp.bfloat16), xa, xb)


def _mixer_constants():
    bf16 = jnp.bfloat16
    h = np.arange(LANES)[:, None]
    ch = np.arange(D_SSD)[None, :]
    expand = (ch // SSD_HEAD_DIM == h).astype(np.float32)
    i = np.arange(BLK)[:, None]
    j = np.arange(BLK)[None, :]
    causal = (j <= i).astype(np.float32)
    jh = np.arange(HIST_ROWS)[None, :]
    pcur = np.stack([((j <= i) & (i - j < w)) for w in POOL_WINDOWS]).astype(np.float32)
    phist = np.stack([(i + HIST_ROWS - jh < w) for w in POOL_WINDOWS]).astype(np.float32)
    return dict(expand=jnp.asarray(expand, bf16), expand_t=jnp.asarray(expand.T, bf16),
                causal=jnp.asarray(causal, bf16), pcur=jnp.asarray(pcur, bf16),
                phist=jnp.asarray(phist, bf16))


def _softplus(x):
    return jnp.maximum(x, 0.0) + jnp.log(1.0 + jnp.exp(-jnp.abs(x)))


def _conv_silu(ext_ref, cw_ref, cb_ref, first_row):
    acc = cb_ref[...] + cw_ref[0:1, :] * ext_ref[pl.ds(first_row, BLK), :]
    for k in range(1, CONV_WIDTH):
        acc = acc + cw_ref[k:k + 1, :] * ext_ref[pl.ds(first_row + k, BLK), :]
    return acc * jax.nn.sigmoid(acc)


def _ssd_intra(xbc_c, dt_raw, dtb_ref, alog_ref, causal_bf, expand_ref):
    f32 = jnp.float32
    xs = xbc_c[:, :D_SSD]
    bm = xbc_c[:, D_SSD:D_SSD + N_SSD_GROUPS * D_STATE]
    cm = xbc_c[:, D_SSD + N_SSD_GROUPS * D_STATE:]
    dt = _softplus(dt_raw + dtb_ref[...])
    a = dt * (-jnp.exp(alog_ref[...]))
    a_cum = _dot_sel_left(causal_bf, a, 3)
    dt_x = _dot_sel_right(dt, expand_ref[...], 2)
    return xs, bm, cm, dt, a_cum, xs * dt_x


def _ssd_diag_group(g, cb, a_cum, a_cum_t, keep, xdt):
    f32, bf16 = jnp.float32, jnp.bfloat16
    hg = N_SSD_HEADS // N_SSD_GROUPS
    lane = lax.broadcasted_iota(jnp.int32, (BLK, LANES), 1)
    first_head = lane < SSD_HEAD_DIM
    neg = jnp.float32(-jnp.inf)
    outs = []
    for pr in range(hg * SSD_HEAD_DIM // LANES):
        h1 = g * hg + 2 * pr
        blk = (g * hg * SSD_HEAD_DIM) // LANES + pr
        xp = xdt[:, blk * LANES:(blk + 1) * LANES]
        x1 = jnp.where(first_head, xp, 0.0).astype(bf16)
        x2 = jnp.where(first_head, 0.0, xp).astype(bf16)
        m1 = (cb * jnp.exp(jnp.where(keep, a_cum[:, h1:h1 + 1] - a_cum_t[h1:h1 + 1, :], neg))).astype(bf16)
        m2 = (cb * jnp.exp(jnp.where(keep, a_cum[:, h1 + 1:h1 + 2] - a_cum_t[h1 + 1:h1 + 2, :], neg))).astype(bf16)
        outs.append(jnp.dot(m1, x1, preferred_element_type=f32) + jnp.dot(m2, x2, preferred_element_type=f32))
    return jnp.concatenate(outs, axis=1)


def _gated_norm(y, z, nw_ref):
    yg = y * (z * jax.nn.sigmoid(z))
    return yg * lax.rsqrt(jnp.mean(yg * yg, axis=-1, keepdims=True) + EPS) * nw_ref[...]


def _prompt_mixer_body(z_ref, xbc_ref, dt_ref, u_ref, cw_ref, cb_ref, dtb_ref, alog_ref, dskip_ref, nw_ref,
                       pw_ref, ps_ref, causal_ref, expand_ref, expand_t_ref, pcur_ref, phist_ref,
                       mix_ref, ssm_ref, ext_ref, pool_tail_ref, state_ref):
    f32, bf16 = jnp.float32, jnp.bfloat16
    c = pl.program_id(1)
    gw = D_SSD // N_SSD_GROUPS

    @pl.when(c == 0)
    def _():
        ext_ref[0:CONV_TAIL_ROWS, :] = jnp.zeros((CONV_TAIL_ROWS, D_CONV), f32)
        pool_tail_ref[...] = jnp.zeros(pool_tail_ref.shape, f32)
        state_ref[...] = jnp.zeros(state_ref.shape, f32)

    ext_ref[CONV_TAIL_ROWS:CONV_TAIL_ROWS + BLK, :] = xbc_ref[...]
    xbc_c = _conv_silu(ext_ref, cw_ref, cb_ref, CONV_TAIL_ROWS - (CONV_WIDTH - 1))
    ext_ref[0:CONV_TAIL_ROWS, :] = xbc_ref[BLK - CONV_TAIL_ROWS:BLK, :]

    causal_bf = causal_ref[...]
    keep = causal_bf > 0
    xs, bm, cm, dt, a_cum, xdt = _ssd_intra(xbc_c, dt_ref[...], dtb_ref, alog_ref, causal_bf, expand_ref)
    a_cum_t = jnp.transpose(a_cum)
    a_tot = a_cum[BLK - 1:BLK, :]
    ea_x = _dot_sel_right(jnp.exp(a_cum), expand_ref[...], 2)
    dte_x = _dot_sel_right(jnp.exp(a_tot - a_cum), expand_ref[...], 2)
    cd_t = jnp.broadcast_to(jnp.exp(a_cum_t[:, BLK - 1:BLK]), (LANES, LANES))
    cd_col = _dot_sel_left(expand_t_ref[...], cd_t, 2)[:, 0:1]

    y_parts = []
    for g in range(N_SSD_GROUPS):
        cg = cm[:, g * D_STATE:(g + 1) * D_STATE].astype(bf16)
        bg = bm[:, g * D_STATE:(g + 1) * D_STATE].astype(bf16)
        cb = lax.dot_general(cg, bg, NT_DIMS, preferred_element_type=f32)
        y_diag = _ssd_diag_group(g, cb, a_cum, a_cum_t, keep, xdt)
        sg = state_ref[g * gw:(g + 1) * gw, :]
        y_off = lax.dot_general(cg, sg.astype(bf16), NT_DIMS, preferred_element_type=f32)
        y_parts.append(y_diag + y_off * ea_x[:, g * gw:(g + 1) * gw])
        xdte_t = jnp.transpose(xdt[:, g * gw:(g + 1) * gw] * dte_x[:, g * gw:(g + 1) * gw]).astype(bf16)
        state_ref[g * gw:(g + 1) * gw, :] = (sg * cd_col[g * gw:(g + 1) * gw, :]
                                             + jnp.dot(xdte_t, bg, preferred_element_type=f32))
    y = jnp.concatenate(y_parts, axis=1) + xs * dskip_ref[...]
    mix_ref[:, 0:D_SSD] = _gated_norm(y, z_ref[...], nw_ref).astype(bf16)

    @pl.when(c == pl.num_programs(1) - 1)
    def _():
        ssm_ref[0] = state_ref[...].reshape(N_SSD_HEADS, SSD_HEAD_DIM, D_STATE)

    u = u_ref[...]
    tail = pool_tail_ref[...]
    pos = (c * BLK + lax.broadcasted_iota(jnp.int32, (BLK, 1), 0) + 1).astype(f32)
    for gi, w in enumerate(POOL_WINDOWS):
        sl = slice(gi * POOL_GROUP_DIM, (gi + 1) * POOL_GROUP_DIM)
        ug = u[:, sl]
        wsum = _dot_sel_left(pcur_ref[gi], ug, 2) + _dot_sel_left(phist_ref[gi], tail[:, sl], 2)
        pooled = wsum / jnp.minimum(pos, jnp.float32(w)) - ug
        po = jnp.dot(pooled.astype(bf16), pw_ref[gi], preferred_element_type=f32) * ps_ref[:, sl]
        mix_ref[:, D_SSD + gi * POOL_GROUP_DIM:D_SSD + (gi + 1) * POOL_GROUP_DIM] = po.astype(bf16)
    pool_tail_ref[...] = u_ref[BLK - HIST_ROWS:BLK, :]


def _prompt_mixer(z, xbc, dt, u, n_seq, seq_len, conv_w, conv_b, dt_bias, A_log, D_skip, ssd_norm_w, pool_w,
                  pool_scale):
    f32, bf16 = jnp.float32, jnp.bfloat16
    n = n_seq * seq_len
    n_blk = seq_len // BLK
    k = _mixer_constants()

    def row_blk(width):
        return pl.BlockSpec((BLK, width), lambda b, c: (b * n_blk + c, 0))

    def const(shape):
        return pl.BlockSpec(shape, lambda b, c: (0,) * len(shape))

    pad_h = (0, LANES - N_SSD_HEADS)
    return pl.pallas_call(
        _prompt_mixer_body,
        grid=(n_seq, n_blk),
        in_specs=[row_blk(D_SSD), row_blk(D_CONV), row_blk(LANES), row_blk(D_POOL),
                  const((CONV_WIDTH, D_CONV)), const((1, D_CONV)), const((1, LANES)), const((1, LANES)),
                  const((1, D_SSD)), const((1, D_SSD)),
                  const((N_POOL_GROUPS, POOL_GROUP_DIM, POOL_GROUP_DIM)), const((1, D_POOL)),
                  const((BLK, BLK)), const((LANES, D_SSD)), const((D_SSD, LANES)),
                  const((N_POOL_GROUPS, BLK, BLK)), const((N_POOL_GROUPS, BLK, HIST_ROWS))],
        out_specs=[pl.BlockSpec((BLK, D_MIX), lambda b, c: (b * n_blk + c, 0)),
                   pl.BlockSpec((1, N_SSD_HEADS, SSD_HEAD_DIM, D_STATE), lambda b, c: (b, 0, 0, 0))],
        out_shape=[jax.ShapeDtypeStruct((n, D_MIX), bf16),
                   jax.ShapeDtypeStruct((n_seq, N_SSD_HEADS, SSD_HEAD_DIM, D_STATE), f32)],
        scratch_shapes=[pltpu.VMEM((CONV_TAIL_ROWS + BLK, D_CONV), f32),
                        pltpu.VMEM((HIST_ROWS, D_POOL), f32),
                        pltpu.VMEM((D_SSD, D_STATE), f32)],
        compiler_params=pltpu.CompilerParams(dimension_semantics=("parallel", "arbitrary"),
                                             vmem_limit_bytes=VMEM_LIMIT_BYTES),
        name="prompt_mixer",
    )(z, xbc, dt, u, conv_w.astype(f32), conv_b.reshape(1, D_CONV).astype(f32),
      jnp.pad(dt_bias.astype(f32), pad_h).reshape(1, LANES), jnp.pad(A_log.astype(f32), pad_h).reshape(1, LANES),
      jnp.repeat(D_skip.astype(f32), SSD_HEAD_DIM).reshape(1, D_SSD), ssd_norm_w.reshape(1, D_SSD).astype(f32),
      pool_w.astype(bf16), pool_scale.reshape(1, D_POOL).astype(f32),
      k["causal"], k["expand"], k["expand_t"], k["pcur"], k["phist"])


SEQ_PER_BLK = BLK // DEC_SEQ


def _sample_constants():
    bf16 = jnp.bfloat16
    r = np.arange(BLK)
    sq, st = r // DEC_SEQ, r % DEC_SEQ
    same = sq[:, None] == sq[None, :]
    causal = same & (st[None, :] <= st[:, None])
    nk = CONV_WIDTH - 1
    shift = np.stack([same & (st[None, :] == st[:, None] + k - nk) for k in range(nk)])
    cs = np.arange(SEQ_PER_BLK * nk)
    stsel = np.stack([(cs[None, :] // nk == sq[:, None]) & (cs[None, :] % nk == st[:, None] + k)
                      for k in range(nk)])
    pcur = np.stack([causal & (st[:, None] - st[None, :] < w) for w in POOL_WINDOWS])
    hs = np.arange(SEQ_PER_BLK * POOL_HIST)
    phist = np.stack([(hs[None, :] // POOL_HIST == sq[:, None])
                      & (st[:, None] + POOL_HIST - hs[None, :] % POOL_HIST < w) for w in POOL_WINDOWS])
    as_bf = lambda a: jnp.asarray(a.astype(np.float32), bf16)
    return dict(same=as_bf(same), causal=as_bf(causal), shift=as_bf(shift), stsel=as_bf(stsel),
                pcur=as_bf(pcur), phist=as_bf(phist))


def _sample_mixer_body(z_ref, xbc_ref, dt_ref, u_ref, cst_ref, pst_ref, ssm_in_ref,
                       cw_ref, cb_ref, dtb_ref, alog_ref, dskip_ref, nw_ref, pw_ref, ps_ref,
                       causal_ref, same_ref, expand_ref, expand_t_ref, shift_ref, stsel_ref, pcur_ref, phist_ref,
                       mix_ref, ssm_out_ref,
                       ydiag_ref, ea_ref, yt_ref, cdh_ref, cdl_ref, xdte_t_ref, bm_ref, cm_ref, *, pos0):
    f32, bf16 = jnp.float32, jnp.bfloat16
    s = pl.program_id(1)
    gw = D_SSD // N_SSD_GROUPS

    @pl.when(s == 0)
    def _():
        xbc = xbc_ref[...]
        cst = cst_ref[...]
        acc = cb_ref[...] + cw_ref[CONV_WIDTH - 1:CONV_WIDTH, :] * xbc
        for k in range(CONV_WIDTH - 1):
            tap = _dot_sel_left(shift_ref[k], xbc, 3) + _dot_sel_left(stsel_ref[k], cst, 3)
            acc = acc + cw_ref[k:k + 1, :] * tap
        xbc_c = acc * jax.nn.sigmoid(acc)

        causal_bf = causal_ref[...]
        keep = causal_bf > 0
        xs, bm, cm, dt, a_cum, xdt = _ssd_intra(xbc_c, dt_ref[...], dtb_ref, alog_ref, causal_bf, expand_ref)
        a_tot = _dot_sel_left(same_ref[...], dt * (-jnp.exp(alog_ref[...])), 3)
        a_cum_t = jnp.transpose(a_cum)
        ea_ref[...] = _dot_sel_right(jnp.exp(a_cum), expand_ref[...], 2)
        dte_x = _dot_sel_right(jnp.exp(a_tot - a_cum), expand_ref[...], 2)
        cd_col = _dot_sel_left(expand_t_ref[...], jnp.exp(jnp.transpose(a_tot)), 2)
        cd_hi, cd_lo = _split2(cd_col)
        cdh_ref[...] = cd_hi
        cdl_ref[...] = cd_lo
        bm_ref[...] = bm.astype(bf16)
        cm_ref[...] = cm.astype(bf16)
        for g in range(N_SSD_GROUPS):
            cg = cm[:, g * D_STATE:(g + 1) * D_STATE].astype(bf16)
            bg = bm[:, g * D_STATE:(g + 1) * D_STATE].astype(bf16)
            cb = lax.dot_general(cg, bg, NT_DIMS, preferred_element_type=f32)
            y_diag = _ssd_diag_group(g, cb, a_cum, a_cum_t, keep, xdt)
            ydiag_ref[:, g * gw:(g + 1) * gw] = y_diag + xs[:, g * gw:(g + 1) * gw] * dskip_ref[:, g * gw:(g + 1) * gw]
            xdte_t_ref[g * gw:(g + 1) * gw, :] = jnp.transpose(
                xdt[:, g * gw:(g + 1) * gw] * dte_x[:, g * gw:(g + 1) * gw]).astype(bf16)
        yt_ref[...] = jnp.zeros(yt_ref.shape, f32)

        u = u_ref[...]
        pst = pst_ref[...]
        step = lax.broadcasted_iota(jnp.int32, (BLK, 1), 0) % DEC_SEQ
        pos = (step + (pos0 + 1)).astype(f32)
        for gi, w in enumerate(POOL_WINDOWS):
            sl = slice(gi * POOL_GROUP_DIM, (gi + 1) * POOL_GROUP_DIM)
            ug = u[:, sl]
            wsum = _dot_sel_left(pcur_ref[gi], ug, 2) + _dot_sel_left(phist_ref[gi], pst[:, sl], 2)
            pooled = wsum / jnp.minimum(pos, jnp.float32(w)) - ug
            po = jnp.dot(pooled.astype(bf16), pw_ref[gi], preferred_element_type=f32) * ps_ref[:, sl]
            mix_ref[:, D_SSD + gi * POOL_GROUP_DIM:D_SSD + (gi + 1) * POOL_GROUP_DIM] = po.astype(bf16)

    rows_of_s = lax.broadcasted_iota(jnp.int32, (BLK, LANES), 0) // DEC_SEQ == s
    cols_of_s = lax.broadcasted_iota(jnp.int32, (gw, BLK), 1) // DEC_SEQ == s
    pick_s = jnp.where(lax.broadcasted_iota(jnp.int32, (BLK, LANES), 0) == DEC_SEQ * s, 1.0, 0.0).astype(bf16)
    state = ssm_in_ref[0].reshape(D_SSD, D_STATE)
    for g in range(N_SSD_GROUPS):
        rs = slice(g * gw, (g + 1) * gw)
        sg = state[rs, :]
        cg = cm_ref[:, g * D_STATE:(g + 1) * D_STATE]
        bg = bm_ref[:, g * D_STATE:(g + 1) * D_STATE]
        yt = lax.dot_general(sg.astype(bf16), cg, NT_DIMS, preferred_element_type=f32)
        yt_ref[rs, :] += jnp.where(cols_of_s, yt, 0.0)
        cd = (jnp.dot(cdh_ref[rs, :], pick_s, preferred_element_type=f32)
              + jnp.dot(cdl_ref[rs, :], pick_s, preferred_element_type=f32))
        upd = jnp.dot(xdte_t_ref[rs, :], jnp.where(rows_of_s, bg, jnp.zeros_like(bg)),
                      preferred_element_type=f32)
        ssm_out_ref[0, g * (N_SSD_HEADS // N_SSD_GROUPS):(g + 1) * (N_SSD_HEADS // N_SSD_GROUPS)] = (
            sg * cd + upd).reshape(N_SSD_HEADS // N_SSD_GROUPS, SSD_HEAD_DIM, D_STATE)

    @pl.when(s == pl.num_programs(1) - 1)
    def _():
        y = ydiag_ref[...] + jnp.transpose(yt_ref[...]) * ea_ref[...]
        mix_ref[:, 0:D_SSD] = _gated_norm(y, z_ref[...], nw_ref).astype(bf16)


def _sample_mixer(z, xbc, dt, u, row0, n_seq, state_conv, state_ssm, state_pool, pos0,
                  conv_w, conv_b, dt_bias, A_log, D_skip, ssd_norm_w, pool_w, pool_scale):
    f32, bf16 = jnp.float32, jnp.bfloat16
    n_blk = n_seq // SEQ_PER_BLK
    blk0 = row0 // BLK
    nk = CONV_WIDTH - 1
    k = _mixer_constants()
    ks = _sample_constants()

    def row_blk(width):
        return pl.BlockSpec((BLK, width), lambda j, s: (blk0 + j, 0))

    def const(shape):
        return pl.BlockSpec(shape, lambda j, s: (0,) * len(shape))

    state_spec = pl.BlockSpec((1, N_SSD_HEADS, SSD_HEAD_DIM, D_STATE), lambda j, s: (j * SEQ_PER_BLK + s, 0, 0, 0))
    pad_h = (0, LANES - N_SSD_HEADS)
    return pl.pallas_call(
        functools.partial(_sample_mixer_body, pos0=pos0),
        grid=(n_blk, SEQ_PER_BLK),
        in_specs=[row_blk(D_SSD), row_blk(D_CONV), row_blk(LANES), row_blk(D_POOL),
                  pl.BlockSpec((SEQ_PER_BLK * nk, D_CONV), lambda j, s: (j, 0)),
                  pl.BlockSpec((SEQ_PER_BLK * POOL_HIST, D_POOL), lambda j, s: (j, 0)),
                  state_spec,
                  const((CONV_WIDTH, D_CONV)), const((1, D_CONV)), const((1, LANES)), const((1, LANES)),
                  const((1, D_SSD)), const((1, D_SSD)),
                  const((N_POOL_GROUPS, POOL_GROUP_DIM, POOL_GROUP_DIM)), const((1, D_POOL)),
                  const((BLK, BLK)), const((BLK, BLK)), const((LANES, D_SSD)), const((D_SSD, LANES)),
                  const((nk, BLK, BLK)), const((nk, BLK, SEQ_PER_BLK * nk)),
                  const((N_POOL_GROUPS, BLK, BLK)), const((N_POOL_GROUPS, BLK, SEQ_PER_BLK * POOL_HIST))],
        out_specs=[pl.BlockSpec((BLK, D_MIX), lambda j, s: (j, 0)), state_spec],
        out_shape=[jax.ShapeDtypeStruct((n_seq * DEC_SEQ, D_MIX), bf16),
                   jax.ShapeDtypeStruct((n_seq, N_SSD_HEADS, SSD_HEAD_DIM, D_STATE), f32)],
        scratch_shapes=[pltpu.VMEM((BLK, D_SSD), f32), pltpu.VMEM((BLK, D_SSD), f32),
                        pltpu.VMEM((D_SSD, BLK), f32), pltpu.VMEM((D_SSD, BLK), bf16),
                        pltpu.VMEM((D_SSD, BLK), bf16), pltpu.VMEM((D_SSD, BLK), bf16),
                        pltpu.VMEM((BLK, N_SSD_GROUPS * D_STATE), bf16),
                        pltpu.VMEM((BLK, N_SSD_GROUPS * D_STATE), bf16)],
        compiler_params=pltpu.CompilerParams(dimension_semantics=("parallel", "arbitrary"),
                                             vmem_limit_bytes=VMEM_LIMIT_BYTES),
        name="sample_mixer",
    )(z, xbc, dt, u, state_conv.reshape(n_seq * nk, D_CONV), state_pool.reshape(n_seq * POOL_HIST, D_POOL),
      state_ssm, conv_w.astype(f32), conv_b.reshape(1, D_CONV).astype(f32),
      jnp.pad(dt_bias.astype(f32), pad_h).reshape(1, LANES), jnp.pad(A_log.astype(f32), pad_h).reshape(1, LANES),
      jnp.repeat(D_skip.astype(f32), SSD_HEAD_DIM).reshape(1, D_SSD), ssd_norm_w.reshape(1, D_SSD).astype(f32),
      pool_w.astype(bf16), pool_scale.reshape(1, D_POOL).astype(f32),
      ks["causal"], ks["same"], k["expand"], k["expand_t"], ks["shift"], ks["stsel"],
      ks["pcur"], ks["phist"])


def _moe_sizes(n_tokens, tm):
    nt = n_tokens // tm
    lmax = -(-(TOP_K * tm + N_EXPERTS * (MOE_SEG_ROWS - 1)) // MOE_SEL_ROWS) * MOE_SEL_ROWS
    rows = (TOP_K * n_tokens + nt * N_EXPERTS * (MOE_SEG_ROWS - 1) + N_EXPERTS * (MOE_PIECE - 1)
            + MOE_ROW_CHUNK)
    n_rows = -(-rows // MOE_PIECE) * MOE_PIECE
    return nt, lmax, n_rows


def _router_body(x_ref, nw_ref, rwh_ref, rwl_ref, rb_ref, h_ref, posg_ref, post_ref, cnt_ref):
    f32, bf16 = jnp.float32, jnp.bfloat16
    tm = x_ref.shape[0]
    x = x_ref[...]
    h = x * lax.rsqrt(jnp.mean(x * x, axis=-1, keepdims=True) + EPS) * nw_ref[...]
    h_hi = h.astype(bf16)
    h_ref[...] = h_hi
    h_lo = (h - h_hi.astype(f32)).astype(bf16)
    wh = rwh_ref[...]
    logits = (jnp.dot(h_hi, wh, preferred_element_type=f32)
              + jnp.dot(h_lo, wh, preferred_element_type=f32)
              + jnp.dot(h_hi, rwl_ref[...], preferred_element_type=f32)) + rb_ref[...]
    lane = lax.broadcasted_iota(jnp.int32, (tm, LANES), 1)
    lanef = lane.astype(f32)
    neg = jnp.float32(-jnp.inf)
    l = jnp.where(lane < N_EXPERTS, logits, neg)
    sels, vals = [], []
    for _ in range(TOP_K):
        m = jnp.max(l, axis=1, keepdims=True)
        idx = jnp.min(jnp.where(l == m, lanef, jnp.float32(LANES)), axis=1, keepdims=True)
        sel = lanef == idx
        l = jnp.where(sel, neg, l)
        sels.append(sel)
        vals.append(m)
    exps = [jnp.exp(v - vals[0]) for v in vals]
    denom = exps[0] + exps[1] + exps[2] + exps[3]
    gates = [e / denom for e in exps]
    chosen = jnp.where(sels[0] | sels[1] | sels[2] | sels[3], 1.0, 0.0).astype(f32)
    row = lax.broadcasted_iota(jnp.int32, (tm, tm), 0)
    col = lax.broadcasted_iota(jnp.int32, (tm, tm), 1)
    lower = jnp.where(col < row, 1.0, 0.0).astype(bf16)
    rank = jnp.dot(lower, chosen.astype(bf16), preferred_element_type=f32)
    cnt = jnp.sum(chosen, axis=0, keepdims=True)
    seg_units = jnp.floor((cnt + (MOE_SEG_ROWS - 1)) * (1.0 / MOE_SEG_ROWS))
    r2 = lax.broadcasted_iota(jnp.int32, (LANES, LANES), 0)
    c2 = lax.broadcasted_iota(jnp.int32, (LANES, LANES), 1)
    upper = jnp.where(r2 < c2, 1.0, 0.0).astype(bf16)
    lstart = jnp.dot(jnp.broadcast_to(seg_units, (8, LANES)).astype(bf16), upper,
                     preferred_element_type=f32)[0:1, :] * MOE_SEG_ROWS
    posmat = lstart + rank
    posg = jnp.zeros((tm, LANES), f32)
    for k in range(TOP_K):
        pos_k = jnp.sum(jnp.where(sels[k], posmat, 0.0), axis=1, keepdims=True)
        posg = posg + jnp.where(lane == k, pos_k, 0.0) + jnp.where(lane == TOP_K + k, gates[k], 0.0)
    posg_ref[...] = posg
    post_ref[...] = jnp.transpose(posg)[0:8, :]
    cnt_ref[0] = jnp.broadcast_to(cnt, (8, LANES)).astype(jnp.int32)


def _moe_router(x1, norm2_w, router_w, router_b, tm):
    n, d = x1.shape
    nt = n // tm
    f32, bf16 = jnp.float32, jnp.bfloat16
    rw = jnp.pad(router_w.astype(f32), ((0, 0), (0, LANES - N_EXPERTS)))
    rw_hi = rw.astype(bf16)
    rw_lo = (rw - rw_hi.astype(f32)).astype(bf16)
    rb = jnp.pad(router_b.astype(f32), (0, LANES - N_EXPERTS)).reshape(1, LANES)
    return pl.pallas_call(
        _router_body,
        grid=(nt,),
        in_specs=[pl.BlockSpec((tm, d), lambda i: (i, 0)),
                  pl.BlockSpec((1, d), lambda i: (0, 0)),
                  pl.BlockSpec((d, LANES), lambda i: (0, 0)),
                  pl.BlockSpec((d, LANES), lambda i: (0, 0)),
                  pl.BlockSpec((1, LANES), lambda i: (0, 0))],
        out_specs=[pl.BlockSpec((tm, d), lambda i: (i, 0)),
                   pl.BlockSpec((tm, LANES), lambda i: (i, 0)),
                   pl.BlockSpec((8, tm), lambda i: (0, i)),
                   pl.BlockSpec((1, 8, LANES), lambda i: (i, 0, 0))],
        out_shape=[jax.ShapeDtypeStruct((n, d), bf16),
                   jax.ShapeDtypeStruct((n, LANES), f32),
                   jax.ShapeDtypeStruct((8, n), f32),
                   jax.ShapeDtypeStruct((nt, 8, LANES), jnp.int32)],
        compiler_params=pltpu.CompilerParams(dimension_semantics=("parallel",),
                                             vmem_limit_bytes=VMEM_LIMIT_BYTES),
        name="moe_router",
    )(x1, norm2_w.reshape(1, d).astype(f32), rw_hi, rw_lo, rb)


def _moe_plan(cnt):
    i32 = jnp.int32
    pad = (cnt + (MOE_SEG_ROWS - 1)) // MOE_SEG_ROWS * MOE_SEG_ROWS
    lstart = jnp.cumsum(pad, axis=1) - pad
    lp = jnp.sum(pad, axis=1)
    tot = jnp.sum(pad, axis=0)
    reg = (tot + (MOE_PIECE - 1)) // MOE_PIECE * MOE_PIECE
    reg_end = jnp.cumsum(reg)
    estart = reg_end - reg
    seg = estart[None, :] + jnp.cumsum(pad, axis=0) - pad
    return dict(
        lstart=lstart.reshape(-1).astype(i32), seg_units=(pad // MOE_SEG_ROWS).reshape(-1).astype(i32),
        seg=seg.reshape(-1).astype(i32), lp=lp.astype(i32),
        tail_start=(estart + tot).astype(i32), tail_units=((reg - tot) // MOE_SEG_ROWS).astype(i32),
        estart=estart.astype(i32), erows=reg.astype(i32), used=reg_end[-1].reshape(1).astype(i32))


def _for_each_segment_copy(i, lstart_ref, units_ref, seg_ref, local_ref, global_ref, sem, to_global, fn):
    def per_expert(e, carry):
        k = i * N_EXPERTS + e
        ls, gs = lstart_ref[k], seg_ref[k]

        def per_piece(j, c):
            lo = pl.multiple_of(ls + j * MOE_SEG_ROWS, MOE_SEG_ROWS)
            go = pl.multiple_of(gs + j * MOE_SEG_ROWS, MOE_SEG_ROWS)
            loc = local_ref.at[pl.ds(lo, MOE_SEG_ROWS)]
            glo = global_ref.at[pl.ds(go, MOE_SEG_ROWS)]
            fn(pltpu.make_async_copy(loc, glo, sem) if to_global else pltpu.make_async_copy(glo, loc, sem))
            return c
        return lax.fori_loop(0, units_ref[k], per_piece, carry)
    lax.fori_loop(0, N_EXPERTS, per_expert, 0)


def _for_each_unused_piece(used_ref, zero_ref, rows_ref, sem, fn):
    def per_piece(j, c):
        go = pl.multiple_of(j * MOE_PIECE, MOE_PIECE)
        fn(pltpu.make_async_copy(zero_ref, rows_ref.at[pl.ds(go, MOE_PIECE)], sem))
        return c
    lax.fori_loop(used_ref[0] // MOE_PIECE, rows_ref.shape[0] // MOE_PIECE, per_piece, 0)


def _dispatch_body(lstart_ref, units_ref, seg_ref, lp_ref, tail_start_ref, tail_units_ref, used_ref,
                   h_ref, post_ref, xs_ref, stage_ref, sel_ref, zero_ref, sems, fill_sem):
    f32, bf16 = jnp.float32, jnp.bfloat16
    i = pl.program_id(0)
    nt = pl.num_programs(0)
    slot = i % 2
    tm = h_ref.shape[0]
    lmax = stage_ref.shape[1]

    def for_each_fill_copy(fn):
        def per_expert(e, carry):
            ts = tail_start_ref[e]

            def per_piece(j, c):
                go = pl.multiple_of(ts + j * MOE_SEG_ROWS, MOE_SEG_ROWS)
                fn(pltpu.make_async_copy(zero_ref.at[pl.ds(0, MOE_SEG_ROWS)],
                                         xs_ref.at[pl.ds(go, MOE_SEG_ROWS)], fill_sem))
                return c
            return lax.fori_loop(0, tail_units_ref[e], per_piece, carry)
        lax.fori_loop(0, N_EXPERTS, per_expert, 0)
        _for_each_unused_piece(used_ref, zero_ref, xs_ref, fill_sem, fn)

    @pl.when(i == 0)
    def _():
        zero_ref[...] = jnp.zeros(zero_ref.shape, bf16)
        for_each_fill_copy(lambda cp: cp.start())
        for_each_fill_copy(lambda cp: cp.wait())

    def segment_copies(tile, slot_, fn):
        _for_each_segment_copy(tile, lstart_ref, units_ref, seg_ref, stage_ref.at[slot_], xs_ref,
                               sems.at[slot_], True, fn)

    @pl.when(i >= 2)
    def _():
        segment_copies(i - 2, slot, lambda cp: cp.wait())

    pos = post_ref[0:TOP_K, :]
    for part in range(lmax // MOE_SEL_ROWS):
        @pl.when(part * MOE_SEL_ROWS < lp_ref[i])
        def _():
            for c in range(MOE_SEL_ROWS // MOE_CHUNK):
                r0 = part * MOE_SEL_ROWS + c * MOE_CHUNK
                r = (lax.broadcasted_iota(jnp.int32, (MOE_CHUNK, tm), 0) + r0).astype(f32)
                hit = (pos[0:1, :] == r) | (pos[1:2, :] == r) | (pos[2:3, :] == r) | (pos[3:4, :] == r)
                sel_ref[c * MOE_CHUNK:(c + 1) * MOE_CHUNK, :] = jnp.where(hit, 1.0, 0.0).astype(bf16)
            stage_ref[slot, part * MOE_SEL_ROWS:(part + 1) * MOE_SEL_ROWS, :] = jnp.dot(
                sel_ref[...], h_ref[...], preferred_element_type=f32).astype(bf16)

    segment_copies(i, slot, lambda cp: cp.start())

    @pl.when(i == nt - 1)
    def _():
        @pl.when(nt >= 2)
        def _():
            segment_copies(i - 1, 1 - slot, lambda cp: cp.wait())
        segment_copies(i, slot, lambda cp: cp.wait())


def _moe_dispatch(h2, post, plan, tm, lmax, n_rows):
    n, d = h2.shape
    nt = n // tm
    grid_spec = pltpu.PrefetchScalarGridSpec(
        num_scalar_prefetch=7,
        grid=(nt,),
        in_specs=[pl.BlockSpec((tm, d), lambda i, *_: (i, 0)),
                  pl.BlockSpec((8, tm), lambda i, *_: (0, i))],
        out_specs=pl.BlockSpec(memory_space=pl.ANY),
        scratch_shapes=[pltpu.VMEM((2, lmax, d), jnp.bfloat16),
                        pltpu.VMEM((MOE_SEL_ROWS, tm), jnp.bfloat16),
                        pltpu.VMEM((MOE_PIECE, d), jnp.bfloat16),
                        pltpu.SemaphoreType.DMA((2,)),
                        pltpu.SemaphoreType.DMA(())])
    return pl.pallas_call(
        _dispatch_body,
        grid_spec=grid_spec,
        out_shape=jax.ShapeDtypeStruct((n_rows, d), jnp.bfloat16),
        compiler_params=pltpu.CompilerParams(dimension_semantics=("arbitrary",),
                                             vmem_limit_bytes=VMEM_LIMIT_BYTES),
        name="moe_dispatch",
    )(plan["lstart"], plan["seg_units"], plan["seg"], plan["lp"], plan["tail_start"], plan["tail_units"],
      plan["used"], h2, post)


def _experts_body(estart_ref, erows_ref, used_ref,
                  xs_ref, wgu_ref, bgu_ref, wd_ref, bd_ref, os_ref,
                  wgu_bf, wd_bf, xbuf, obuf, zero_ref, in_sems, out_sems, fill_sem):
    f32, bf16 = jnp.float32, jnp.bfloat16
    e = pl.program_id(0)
    start = estart_ref[e]
    rows = erows_ref[e]
    n_chunks = (rows + (MOE_ROW_CHUNK - 1)) // MOE_ROW_CHUNK
    half = D_FF // 2

    def in_copy(c, slot):
        src = xs_ref.at[pl.ds(pl.multiple_of(start + c * MOE_ROW_CHUNK, MOE_PIECE), MOE_ROW_CHUNK)]
        return pltpu.make_async_copy(src, xbuf.at[slot], in_sems.at[slot])

    def for_each_out_copy(c, slot, fn):
        n_pieces = jnp.minimum(MOE_ROW_CHUNK // MOE_PIECE, (rows - c * MOE_ROW_CHUNK) // MOE_PIECE)

        def per_piece(p, carry):
            lo = pl.multiple_of(p * MOE_PIECE, MOE_PIECE)
            go = pl.multiple_of(start + c * MOE_ROW_CHUNK + p * MOE_PIECE, MOE_PIECE)
            fn(pltpu.make_async_copy(obuf.at[slot, pl.ds(lo, MOE_PIECE)], os_ref.at[pl.ds(go, MOE_PIECE)],
                                     out_sems.at[slot]))
            return carry
        lax.fori_loop(0, n_pieces, per_piece, 0)

    @pl.when(e == 0)
    def _():
        zero_ref[...] = jnp.zeros(zero_ref.shape, bf16)
        _for_each_unused_piece(used_ref, zero_ref, os_ref, fill_sem, lambda cp: cp.start())
        _for_each_unused_piece(used_ref, zero_ref, os_ref, fill_sem, lambda cp: cp.wait())

    @pl.when(n_chunks > 0)
    def _():
        in_copy(0, 0).start()
        wgu_bf[...] = wgu_ref[0].astype(bf16)
        wd_bf[...] = wd_ref[0].astype(bf16)

        def chunk(c, carry):
            slot = c % 2
            in_copy(c, slot).wait()

            @pl.when(c + 1 < n_chunks)
            def _():
                in_copy(c + 1, 1 - slot).start()

            @pl.when(c >= 2)
            def _():
                for_each_out_copy(c - 2, slot, lambda cp: cp.wait())

            x = xbuf[slot]
            out = bd_ref[0]
            for hf in range(2):
                gate = jnp.dot(x, wgu_bf[:, hf * half:(hf + 1) * half], preferred_element_type=f32)
                gate = jnp.minimum(gate + bgu_ref[0, :, hf * half:(hf + 1) * half], SWIGLU_LIMIT)
                up = jnp.dot(x, wgu_bf[:, D_FF + hf * half:D_FF + (hf + 1) * half], preferred_element_type=f32)
                up = jnp.clip(up + bgu_ref[0, :, D_FF + hf * half:D_FF + (hf + 1) * half],
                              -SWIGLU_LIMIT, SWIGLU_LIMIT)
                act = (up + 1.0) * (gate * jax.nn.sigmoid(SWIGLU_ALPHA * gate))
                out = out + jnp.dot(act.astype(bf16), wd_bf[hf * half:(hf + 1) * half, :],
                                    preferred_element_type=f32)
            obuf[slot] = out.astype(bf16)
            for_each_out_copy(c, slot, lambda cp: cp.start())
            return carry
        lax.fori_loop(0, n_chunks, chunk, 0)

        @pl.when(n_chunks >= 2)
        def _():
            for_each_out_copy(n_chunks - 2, n_chunks % 2, lambda cp: cp.wait())
        for_each_out_copy(n_chunks - 1, (n_chunks - 1) % 2, lambda cp: cp.wait())


def _moe_experts(xs, plan, w_gate_up, b_gate_up, w_down, b_down):
    d = xs.shape[1]
    grid_spec = pltpu.PrefetchScalarGridSpec(
        num_scalar_prefetch=3,
        grid=(N_EXPERTS,),
        in_specs=[pl.BlockSpec(memory_space=pl.ANY),
                  pl.BlockSpec((1, d, 2 * D_FF), lambda e, *_: (e, 0, 0)),
                  pl.BlockSpec((1, 1, 2 * D_FF), lambda e, *_: (e, 0, 0)),
                  pl.BlockSpec((1, D_FF, d), lambda e, *_: (e, 0, 0)),
                  pl.BlockSpec((1, 1, d), lambda e, *_: (e, 0, 0))],
        out_specs=pl.BlockSpec(memory_space=pl.ANY),
        scratch_shapes=[pltpu.VMEM((d, 2 * D_FF), jnp.bfloat16),
                        pltpu.VMEM((D_FF, d), jnp.bfloat16),
                        pltpu.VMEM((2, MOE_ROW_CHUNK, d), jnp.bfloat16),
                        pltpu.VMEM((2, MOE_ROW_CHUNK, d), jnp.bfloat16),
                        pltpu.VMEM((MOE_PIECE, d), jnp.bfloat16),
                        pltpu.SemaphoreType.DMA((2,)),
                        pltpu.SemaphoreType.DMA((2,)),
                        pltpu.SemaphoreType.DMA(())])
    return pl.pallas_call(
        _experts_body,
        grid_spec=grid_spec,
        out_shape=jax.ShapeDtypeStruct(xs.shape, jnp.bfloat16),
        compiler_params=pltpu.CompilerParams(dimension_semantics=("arbitrary",),
                                             vmem_limit_bytes=VMEM_LIMIT_BYTES),
        name="moe_experts",
    )(plan["estart"], plan["erows"], plan["used"],
      xs, w_gate_up, b_gate_up.reshape(N_EXPERTS, 1, 2 * D_FF), w_down, b_down.reshape(N_EXPERTS, 1, d))


def _combine_body(lstart_ref, units_ref, seg_ref, lp_ref,
                  os_ref, posg_ref, x_ref, fw_ref, yp_ref, ys_ref, stage_ref, w_ref, sems,
                  *, n_prompt_tiles):
    f32, bf16 = jnp.float32, jnp.bfloat16
    i = pl.program_id(0)
    nt = pl.num_programs(0)
    slot = i % 2
    tm = x_ref.shape[0]
    lmax = stage_ref.shape[1]

    def segment_copies(tile, slot_, fn):
        _for_each_segment_copy(tile, lstart_ref, units_ref, seg_ref, stage_ref.at[slot_], os_ref,
                               sems.at[slot_], False, fn)

    @pl.when(i == 0)
    def _():
        stage_ref[...] = jnp.zeros(stage_ref.shape, bf16)
        segment_copies(0, 0, lambda cp: cp.start())

    @pl.when(i + 1 < nt)
    def _():
        segment_copies(i + 1, 1 - slot, lambda cp: cp.start())

    posg = posg_ref[...]
    pos = [posg[:, k:k + 1] for k in range(TOP_K)]
    gate = [posg[:, TOP_K + k:TOP_K + k + 1] for k in range(TOP_K)]
    for c in range(lmax // MOE_CHUNK):
        r = (lax.broadcasted_iota(jnp.int32, (tm, MOE_CHUNK), 1) + c * MOE_CHUNK).astype(f32)
        w = jnp.zeros((tm, MOE_CHUNK), f32)
        for k in range(TOP_K):
            w = jnp.where(pos[k] == r, gate[k], w)
        w_ref[:, c * MOE_CHUNK:(c + 1) * MOE_CHUNK] = w.astype(bf16)

    segment_copies(i, slot, lambda cp: cp.wait())
    y = x_ref[...] + jnp.dot(w_ref[...], stage_ref[slot], preferred_element_type=f32)
    out = y * lax.rsqrt(jnp.mean(y * y, axis=-1, keepdims=True) + EPS) * fw_ref[...]

    @pl.when(i < n_prompt_tiles)
    def _():
        yp_ref[...] = out

    @pl.when(i >= n_prompt_tiles)
    def _():
        ys_ref[...] = out


def _moe_combine(os_, posg, x1, final_norm_w, plan, tm, lmax, n_prompt):
    n, d = x1.shape
    nt = n // tm
    n_prompt_tiles = n_prompt // tm
    n_sample_tiles = nt - n_prompt_tiles
    grid_spec = pltpu.PrefetchScalarGridSpec(
        num_scalar_prefetch=4,
        grid=(nt,),
        in_specs=[pl.BlockSpec(memory_space=pl.ANY),
                  pl.BlockSpec((tm, LANES), lambda i, *_: (i, 0)),
                  pl.BlockSpec((tm, d), lambda i, *_: (i, 0)),
                  pl.BlockSpec((1, d), lambda i, *_: (0, 0))],
        out_specs=[pl.BlockSpec((tm, d), lambda i, *_: (jnp.minimum(i, n_prompt_tiles - 1), 0)),
                   pl.BlockSpec((tm, d), lambda i, *_: (jnp.maximum(i - n_prompt_tiles, 0), 0))],
        scratch_shapes=[pltpu.VMEM((2, lmax, d), jnp.bfloat16),
                        pltpu.VMEM((tm, lmax), jnp.bfloat16),
                        pltpu.SemaphoreType.DMA((2,))])
    return pl.pallas_call(
        functools.partial(_combine_body, n_prompt_tiles=n_prompt_tiles),
        grid_spec=grid_spec,
        out_shape=[jax.ShapeDtypeStruct((n_prompt, d), jnp.float32),
                   jax.ShapeDtypeStruct((n_sample_tiles * tm, d), jnp.float32)],
        compiler_params=pltpu.CompilerParams(dimension_semantics=("arbitrary",),
                                             vmem_limit_bytes=VMEM_LIMIT_BYTES),
        name="moe_combine",
    )(plan["lstart"], plan["seg_units"], plan["seg"], plan["lp"],
      os_, posg, x1, final_norm_w.reshape(1, d).astype(jnp.float32))


def _moe_block(x1, n_prompt, norm2_w, router_w, router_b, w_gate_up, b_gate_up, w_down, b_down,
               final_norm_w, tm=MOE_TOKEN_TILE):
    n = x1.shape[0]
    nt, lmax, n_rows = _moe_sizes(n, tm)
    h2, posg, post, cnt3 = _moe_router(x1, norm2_w, router_w, router_b, tm)
    plan = _moe_plan(cnt3[:, 0, :N_EXPERTS])
    xs = _moe_dispatch(h2, post, plan, tm, lmax, n_rows)
    os_ = _moe_experts(xs, plan, w_gate_up, b_gate_up, w_down, b_down)
    return _moe_combine(os_, posg, x1, final_norm_w, plan, tm, lmax, n_prompt)


def kernel(x_prompt, x_sample, state_ssm, state_conv, state_pool, norm1_w, w_in, conv_w, conv_b, dt_bias,
           A_log, D_skip, ssd_norm_w, pool_w, pool_scale, w_out, norm2_w, router_w, router_b, w_gate_up,
           b_gate_up, w_down, b_down, final_norm_w):
    n_prompt = BATCH * SEQ
    n_sample = DEC_BATCH * DEC_SEQ
    xp = x_prompt.reshape(n_prompt, D_MODEL)
    xs = x_sample.reshape(n_sample, D_MODEL)
    z, xbc, dt_raw, u = _in_proj(xp, xs, norm1_w[0], w_in[0])
    mp = (conv_w[0], conv_b[0], dt_bias[0], A_log[0], D_skip[0], ssd_norm_w[0], pool_w[0], pool_scale[0])
    mix_p, s1 = _prompt_mixer(z, xbc, dt_raw, u, BATCH, SEQ, *mp)
    mix_s, s2 = _sample_mixer(z, xbc, dt_raw, u, n_prompt, DEC_BATCH, state_conv[0], state_ssm[0], state_pool[0],
                              PAST_LEN, *mp)
    nk = CONV_WIDTH - 1
    c1 = jnp.stack([xbc[(b + 1) * SEQ - nk:(b + 1) * SEQ] for b in range(BATCH)])
    p1 = jnp.stack([u[(b + 1) * SEQ - POOL_HIST:(b + 1) * SEQ] for b in range(BATCH)])
    c2 = xbc[n_prompt:].reshape(DEC_BATCH, DEC_SEQ, D_CONV)[:, DEC_SEQ - nk:]
    p2 = jnp.concatenate([state_pool[0][:, DEC_SEQ:], u[n_prompt:].reshape(DEC_BATCH, DEC_SEQ, D_POOL)], axis=1)
    x1 = _out_proj(mix_p, mix_s, w_out[0], xp, xs)
    yp, ys = _moe_block(x1, n_prompt, norm2_w[0], router_w[0], router_b[0], w_gate_up[0], b_gate_up[0],
                        w_down[0], b_down[0], final_norm_w)
    return (yp.reshape(x_prompt.shape), ys.reshape(x_sample.shape),
            s1[None], c1[None], p1[None], s2[None], c2[None], p2[None])
```

```python
import functools
import math
import jax, jax.numpy as jnp
from jax import lax
import numpy as np
from jax.experimental import pallas as pl
from jax.experimental.pallas import tpu as pltpu

D_MODEL = 1024
BATCH = 8
SEQ = 2048
DEC_BATCH = 128
DEC_SEQ = 4
PAST_LEN = 16384

D_MIX = 2 * D_MODEL
D_SSD = 3 * D_MIX // 4
SSD_HEAD_DIM = 64
N_SSD_HEADS = D_SSD // SSD_HEAD_DIM
N_SSD_GROUPS = 4
D_STATE = 128
CONV_WIDTH = 4
SSD_CHUNK = 128
D_CONV = D_SSD + 2 * N_SSD_GROUPS * D_STATE
D_POOL = D_MIX - D_SSD
POOL_WINDOWS = (2, 4, 8, 16)
N_POOL_GROUPS = len(POOL_WINDOWS)
POOL_GROUP_DIM = D_POOL // N_POOL_GROUPS
POOL_HIST = max(POOL_WINDOWS) - 1
D_IN_PROJ = D_SSD + D_CONV + N_SSD_HEADS + D_POOL
N_EXPERTS = 32
TOP_K = 4
D_FF = D_MODEL
SWIGLU_LIMIT = 7.0
SWIGLU_ALPHA = 1.702
EPS = 1e-5

LANES = 128
BF16_SUBLANES = 16
VMEM_LIMIT_BYTES = 48 * 1024 * 1024

MOE_TOKEN_TILE = 512
MOE_SEG_ROWS = BF16_SUBLANES
MOE_PIECE = 128
MOE_ROW_CHUNK = 512
MOE_CHUNK = 256
MOE_SEL_ROWS = 1280


BLK = SSD_CHUNK
PROJ_ROW_TILE = 512
HIST_ROWS = 16
CONV_TAIL_ROWS = 8
NT_DIMS = (((1,), (1,)), ((), ()))


def _split2(v):
    hi = v.astype(jnp.bfloat16)
    lo = (v - hi.astype(jnp.float32)).astype(jnp.bfloat16)
    return hi, lo


def _dot_sel_left(sel, v, passes):
    out = None
    rem = v
    for p in range(passes):
        part = rem.astype(jnp.bfloat16)
        d = jnp.dot(sel, part, preferred_element_type=jnp.float32)
        out = d if out is None else out + d
        if p + 1 < passes:
            rem = rem - part.astype(jnp.float32)
    return out


def _dot_sel_right(v, sel, passes):
    out = None
    rem = v
    for p in range(passes):
        part = rem.astype(jnp.bfloat16)
        d = jnp.dot(part, sel, preferred_element_type=jnp.float32)
        out = d if out is None else out + d
        if p + 1 < passes:
            rem = rem - part.astype(jnp.float32)
    return out


def _two_part_specs(n_first, n_second, tm, width):
    t1 = n_first // tm
    t2 = n_second // tm
    return (pl.BlockSpec((tm, width), lambda i: (jnp.minimum(i, t1 - 1), 0)),
            pl.BlockSpec((tm, width), lambda i: (jnp.clip(i - t1, 0, t2 - 1), 0)))


def _in_proj_body(xa_ref, xb_ref, nw_ref, w_ref, z_ref, xbc_ref, dt_ref, u_ref, *, tiles_a):
    x = jnp.where(pl.program_id(0) < tiles_a, xa_ref[...], xb_ref[...])
    h = (x * lax.rsqrt(jnp.mean(x * x, axis=-1, keepdims=True) + EPS) * nw_ref[...]).astype(jnp.bfloat16)
    off = 0
    for ref in (z_ref, xbc_ref, dt_ref, u_ref):
        n = ref.shape[1]
        ref[...] = jnp.dot(h, w_ref[:, off:off + n], preferred_element_type=jnp.float32)
        off += n


def _in_proj(xa, xb, norm1_w, w_in):
    d = xa.shape[1]
    n = xa.shape[0] + xb.shape[0]
    f32, bf16 = jnp.float32, jnp.bfloat16
    s1, s2 = D_SSD + D_CONV, D_SSD + D_CONV + N_SSD_HEADS
    w = jnp.concatenate([w_in[:, :s1], jnp.pad(w_in[:, s1:s2], ((0, 0), (0, LANES - N_SSD_HEADS))),
                         w_in[:, s2:]], axis=1).astype(bf16)
    widths = (D_SSD, D_CONV, LANES, D_POOL)
    tm = PROJ_ROW_TILE
    return pl.pallas_call(
        functools.partial(_in_proj_body, tiles_a=xa.shape[0] // tm),
        grid=(n // tm,),
        in_specs=[*_two_part_specs(xa.shape[0], xb.shape[0], tm, d),
                  pl.BlockSpec((1, d), lambda i: (0, 0)),
                  pl.BlockSpec((d, sum(widths)), lambda i: (0, 0), pipeline_mode=pl.Buffered(1))],
        out_specs=[pl.BlockSpec((tm, wd), lambda i: (i, 0)) for wd in widths],
        out_shape=[jax.ShapeDtypeStruct((n, wd), f32) for wd in widths],
        compiler_params=pltpu.CompilerParams(dimension_semantics=("parallel",),
                                             vmem_limit_bytes=VMEM_LIMIT_BYTES),
        name="in_proj",
    )(xa, xb, norm1_w.reshape(1, d).astype(f32), w)


def _out_proj_body(ma_ref, mb_ref, w_ref, xa_ref, xb_ref, o_ref, *, tiles_a):
    first = pl.program_id(0) < tiles_a
    m = jnp.where(first, ma_ref[...], mb_ref[...])
    x = jnp.where(first, xa_ref[...], xb_ref[...])
    o_ref[...] = x + jnp.dot(m, w_ref[...], preferred_element_type=jnp.float32)


def _out_proj(ma, mb, w_out, xa, xb):
    d = xa.shape[1]
    n = xa.shape[0] + xb.shape[0]
    tm = PROJ_ROW_TILE
    return pl.pallas_call(
        functools.partial(_out_proj_body, tiles_a=xa.shape[0] // tm),
        grid=(n // tm,),
        in_specs=[*_two_part_specs(xa.shape[0], xb.shape[0], tm, D_MIX),
                  pl.BlockSpec((D_MIX, d), lambda i: (0, 0)),
                  *_two_part_specs(xa.shape[0], xb.shape[0], tm, d)],
        out_specs=pl.BlockSpec((tm, d), lambda i: (i, 0)),
        out_shape=jax.ShapeDtypeStruct((n, d), jnp.float32),
        compiler_params=pltpu.CompilerParams(dimension_semantics=("parallel",),
                                             vmem_limit_bytes=VMEM_LIMIT_BYTES),
        name="out_proj",
    )(ma, mb, w_out.astype(jnp.bfloat16), xa, xb)


def _mixer_constants():
    bf16 = jnp.bfloat16
    h = np.arange(LANES)[:, None]
    ch = np.arange(D_SSD)[None, :]
    expand = (ch // SSD_HEAD_DIM == h).astype(np.float32)
    i = np.arange(BLK)[:, None]
    j = np.arange(BLK)[None, :]
    causal = (j <= i).astype(np.float32)
    jh = np.arange(HIST_ROWS)[None, :]
    pcur = np.stack([((j <= i) & (i - j < w)) for w in POOL_WINDOWS]).astype(np.float32)
    phist = np.stack([(i + HIST_ROWS - jh < w) for w in POOL_WINDOWS]).astype(np.float32)
    return dict(expand=jnp.asarray(expand, bf16), expand_t=jnp.asarray(expand.T, bf16),
                causal=jnp.asarray(causal, bf16), pcur=jnp.asarray(pcur, bf16),
                phist=jnp.asarray(phist, bf16))


def _softplus(x):
    return jnp.maximum(x, 0.0) + jnp.log(1.0 + jnp.exp(-jnp.abs(x)))


def _conv_silu(ext_ref, cw_ref, cb_ref, first_row):
    acc = cb_ref[...] + cw_ref[0:1, :] * ext_ref[pl.ds(first_row, BLK), :]
    for k in range(1, CONV_WIDTH):
        acc = acc + cw_ref[k:k + 1, :] * ext_ref[pl.ds(first_row + k, BLK), :]
    return acc * jax.nn.sigmoid(acc)


def _ssd_intra(xbc_c, dt_raw, dtb_ref, alog_ref, causal_bf, expand_ref):
    f32 = jnp.float32
    xs = xbc_c[:, :D_SSD]
    bm = xbc_c[:, D_SSD:D_SSD + N_SSD_GROUPS * D_STATE]
    cm = xbc_c[:, D_SSD + N_SSD_GROUPS * D_STATE:]
    dt = _softplus(dt_raw + dtb_ref[...])
    a = dt * (-jnp.exp(alog_ref[...]))
    a_cum = _dot_sel_left(causal_bf, a, 3)
    dt_x = _dot_sel_right(dt, expand_ref[...], 2)
    return xs, bm, cm, dt, a_cum, xs * dt_x


def _ssd_diag_group(g, cb, a_cum, a_cum_t, keep, xdt):
    f32, bf16 = jnp.float32, jnp.bfloat16
    hg = N_SSD_HEADS // N_SSD_GROUPS
    lane = lax.broadcasted_iota(jnp.int32, (BLK, LANES), 1)
    first_head = lane < SSD_HEAD_DIM
    neg = jnp.float32(-jnp.inf)
    outs = []
    for pr in range(hg * SSD_HEAD_DIM // LANES):
        h1 = g * hg + 2 * pr
        blk = (g * hg * SSD_HEAD_DIM) // LANES + pr
        xp = xdt[:, blk * LANES:(blk + 1) * LANES]
        x1 = jnp.where(first_head, xp, 0.0).astype(bf16)
        x2 = jnp.where(first_head, 0.0, xp).astype(bf16)
        m1 = (cb * jnp.exp(jnp.where(keep, a_cum[:, h1:h1 + 1] - a_cum_t[h1:h1 + 1, :], neg))).astype(bf16)
        m2 = (cb * jnp.exp(jnp.where(keep, a_cum[:, h1 + 1:h1 + 2] - a_cum_t[h1 + 1:h1 + 2, :], neg))).astype(bf16)
        outs.append(jnp.dot(m1, x1, preferred_element_type=f32) + jnp.dot(m2, x2, preferred_element_type=f32))
    return jnp.concatenate(outs, axis=1)


def _gated_norm(y, z, nw_ref):
    yg = y * (z * jax.nn.sigmoid(z))
    return yg * lax.rsqrt(jnp.mean(yg * yg, axis=-1, keepdims=True) + EPS) * nw_ref[...]


def _prompt_mixer_body(z_ref, xbc_ref, dt_ref, u_ref, cw_ref, cb_ref, dtb_ref, alog_ref, dskip_ref, nw_ref,
                       pw_ref, ps_ref, causal_ref, expand_ref, expand_t_ref, pcur_ref, phist_ref,
                       mix_ref, ssm_ref, ext_ref, pool_tail_ref, state_ref):
    f32, bf16 = jnp.float32, jnp.bfloat16
    c = pl.program_id(1)
    gw = D_SSD // N_SSD_GROUPS

    @pl.when(c == 0)
    def _():
        ext_ref[0:CONV_TAIL_ROWS, :] = jnp.zeros((CONV_TAIL_ROWS, D_CONV), f32)
        pool_tail_ref[...] = jnp.zeros(pool_tail_ref.shape, f32)
        state_ref[...] = jnp.zeros(state_ref.shape, f32)

    ext_ref[CONV_TAIL_ROWS:CONV_TAIL_ROWS + BLK, :] = xbc_ref[...]
    xbc_c = _conv_silu(ext_ref, cw_ref, cb_ref, CONV_TAIL_ROWS - (CONV_WIDTH - 1))
    ext_ref[0:CONV_TAIL_ROWS, :] = xbc_ref[BLK - CONV_TAIL_ROWS:BLK, :]

    causal_bf = causal_ref[...]
    keep = causal_bf > 0
    xs, bm, cm, dt, a_cum, xdt = _ssd_intra(xbc_c, dt_ref[...], dtb_ref, alog_ref, causal_bf, expand_ref)
    a_cum_t = jnp.transpose(a_cum)
    a_tot = a_cum[BLK - 1:BLK, :]
    ea_x = _dot_sel_right(jnp.exp(a_cum), expand_ref[...], 2)
    dte_x = _dot_sel_right(jnp.exp(a_tot - a_cum), expand_ref[...], 2)
    cd_t = jnp.broadcast_to(jnp.exp(a_cum_t[:, BLK - 1:BLK]), (LANES, LANES))
    cd_col = _dot_sel_left(expand_t_ref[...], cd_t, 2)[:, 0:1]

    y_parts = []
    for g in range(N_SSD_GROUPS):
        cg = cm[:, g * D_STATE:(g + 1) * D_STATE].astype(bf16)
        bg = bm[:, g * D_STATE:(g + 1) * D_STATE].astype(bf16)
        cb = lax.dot_general(cg, bg, NT_DIMS, preferred_element_type=f32)
        y_diag = _ssd_diag_group(g, cb, a_cum, a_cum_t, keep, xdt)
        sg = state_ref[g * gw:(g + 1) * gw, :]
        y_off = lax.dot_general(cg, sg.astype(bf16), NT_DIMS, preferred_element_type=f32)
        y_parts.append(y_diag + y_off * ea_x[:, g * gw:(g + 1) * gw])
        xdte_t = jnp.transpose(xdt[:, g * gw:(g + 1) * gw] * dte_x[:, g * gw:(g + 1) * gw]).astype(bf16)
        state_ref[g * gw:(g + 1) * gw, :] = (sg * cd_col[g * gw:(g + 1) * gw, :]
                                             + jnp.dot(xdte_t, bg, preferred_element_type=f32))
    y = jnp.concatenate(y_parts, axis=1) + xs * dskip_ref[...]
    mix_ref[:, 0:D_SSD] = _gated_norm(y, z_ref[...], nw_ref).astype(bf16)

    @pl.when(c == pl.num_programs(1) - 1)
    def _():
        ssm_ref[0] = state_ref[...].reshape(N_SSD_HEADS, SSD_HEAD_DIM, D_STATE)

    u = u_ref[...]
    tail = pool_tail_ref[...]
    pos = (c * BLK + lax.broadcasted_iota(jnp.int32, (BLK, 1), 0) + 1).astype(f32)
    for gi, w in enumerate(POOL_WINDOWS):
        sl = slice(gi * POOL_GROUP_DIM, (gi + 1) * POOL_GROUP_DIM)
        ug = u[:, sl]
        wsum = _dot_sel_left(pcur_ref[gi], ug, 2) + _dot_sel_left(phist_ref[gi], tail[:, sl], 2)
        pooled = wsum / jnp.minimum(pos, jnp.float32(w)) - ug
        po = jnp.dot(pooled.astype(bf16), pw_ref[gi], preferred_element_type=f32) * ps_ref[:, sl]
        mix_ref[:, D_SSD + gi * POOL_GROUP_DIM:D_SSD + (gi + 1) * POOL_GROUP_DIM] = po.astype(bf16)
    pool_tail_ref[...] = u_ref[BLK - HIST_ROWS:BLK, :]


def _prompt_mixer(z, xbc, dt, u, n_seq, seq_len, conv_w, conv_b, dt_bias, A_log, D_skip, ssd_norm_w, pool_w,
                  pool_scale):
    f32, bf16 = jnp.float32, jnp.bfloat16
    n = n_seq * seq_len
    n_blk = seq_len // BLK
    k = _mixer_constants()

    def row_blk(width):
        return pl.BlockSpec((BLK, width), lambda b, c: (b * n_blk + c, 0))

    def const(shape):
        return pl.BlockSpec(shape, lambda b, c: (0,) * len(shape))

    pad_h = (0, LANES - N_SSD_HEADS)
    return pl.pallas_call(
        _prompt_mixer_body,
        grid=(n_seq, n_blk),
        in_specs=[row_blk(D_SSD), row_blk(D_CONV), row_blk(LANES), row_blk(D_POOL),
                  const((CONV_WIDTH, D_CONV)), const((1, D_CONV)), const((1, LANES)), const((1, LANES)),
                  const((1, D_SSD)), const((1, D_SSD)),
                  const((N_POOL_GROUPS, POOL_GROUP_DIM, POOL_GROUP_DIM)), const((1, D_POOL)),
                  const((BLK, BLK)), const((LANES, D_SSD)), const((D_SSD, LANES)),
                  const((N_POOL_GROUPS, BLK, BLK)), const((N_POOL_GROUPS, BLK, HIST_ROWS))],
        out_specs=[pl.BlockSpec((BLK, D_MIX), lambda b, c: (b * n_blk + c, 0)),
                   pl.BlockSpec((1, N_SSD_HEADS, SSD_HEAD_DIM, D_STATE), lambda b, c: (b, 0, 0, 0))],
        out_shape=[jax.ShapeDtypeStruct((n, D_MIX), bf16),
                   jax.ShapeDtypeStruct((n_seq, N_SSD_HEADS, SSD_HEAD_DIM, D_STATE), f32)],
        scratch_shapes=[pltpu.VMEM((CONV_TAIL_ROWS + BLK, D_CONV), f32),
                        pltpu.VMEM((HIST_ROWS, D_POOL), f32),
                        pltpu.VMEM((D_SSD, D_STATE), f32)],
        compiler_params=pltpu.CompilerParams(dimension_semantics=("parallel", "arbitrary"),
                                             vmem_limit_bytes=VMEM_LIMIT_BYTES),
        name="prompt_mixer",
    )(z, xbc, dt, u, conv_w.astype(f32), conv_b.reshape(1, D_CONV).astype(f32),
      jnp.pad(dt_bias.astype(f32), pad_h).reshape(1, LANES), jnp.pad(A_log.astype(f32), pad_h).reshape(1, LANES),
      jnp.repeat(D_skip.astype(f32), SSD_HEAD_DIM).reshape(1, D_SSD), ssd_norm_w.reshape(1, D_SSD).astype(f32),
      pool_w.astype(bf16), pool_scale.reshape(1, D_POOL).astype(f32),
      k["causal"], k["expand"], k["expand_t"], k["pcur"], k["phist"])


SEQ_PER_BLK = BLK // DEC_SEQ


def _sample_constants():
    bf16 = jnp.bfloat16
    r = np.arange(BLK)
    sq, st = r // DEC_SEQ, r % DEC_SEQ
    same = sq[:, None] == sq[None, :]
    causal = same & (st[None, :] <= st[:, None])
    nk = CONV_WIDTH - 1
    shift = np.stack([same & (st[None, :] == st[:, None] + k - nk) for k in range(nk)])
    cs = np.arange(SEQ_PER_BLK * nk)
    stsel = np.stack([(cs[None, :] // nk == sq[:, None]) & (cs[None, :] % nk == st[:, None] + k)
                      for k in range(nk)])
    pcur = np.stack([causal & (st[:, None] - st[None, :] < w) for w in POOL_WINDOWS])
    hs = np.arange(SEQ_PER_BLK * POOL_HIST)
    phist = np.stack([(hs[None, :] // POOL_HIST == sq[:, None])
                      & (st[:, None] + POOL_HIST - hs[None, :] % POOL_HIST < w) for w in POOL_WINDOWS])
    as_bf = lambda a: jnp.asarray(a.astype(np.float32), bf16)
    return dict(same=as_bf(same), causal=as_bf(causal), shift=as_bf(shift), stsel=as_bf(stsel),
                pcur=as_bf(pcur), phist=as_bf(phist))


def _sample_mixer_body(z_ref, xbc_ref, dt_ref, u_ref, cst_ref, pst_ref, ssm_in_ref,
                       cw_ref, cb_ref, dtb_ref, alog_ref, dskip_ref, nw_ref, pw_ref, ps_ref,
                       causal_ref, same_ref, expand_ref, expand_t_ref, shift_ref, stsel_ref, pcur_ref, phist_ref,
                       mix_ref, ssm_out_ref,
                       ydiag_ref, ea_ref, yt_ref, cdh_ref, cdl_ref, xdte_t_ref, bm_ref, cm_ref, *, pos0):
    f32, bf16 = jnp.float32, jnp.bfloat16
    s = pl.program_id(1)
    gw = D_SSD // N_SSD_GROUPS

    @pl.when(s == 0)
    def _():
        xbc = xbc_ref[...]
        cst = cst_ref[...]
        acc = cb_ref[...] + cw_ref[CONV_WIDTH - 1:CONV_WIDTH, :] * xbc
        for k in range(CONV_WIDTH - 1):
            tap = _dot_sel_left(shift_ref[k], xbc, 3) + _dot_sel_left(stsel_ref[k], cst, 3)
            acc = acc + cw_ref[k:k + 1, :] * tap
        xbc_c = acc * jax.nn.sigmoid(acc)

        causal_bf = causal_ref[...]
        keep = causal_bf > 0
        xs, bm, cm, dt, a_cum, xdt = _ssd_intra(xbc_c, dt_ref[...], dtb_ref, alog_ref, causal_bf, expand_ref)
        a_tot = _dot_sel_left(same_ref[...], dt * (-jnp.exp(alog_ref[...])), 3)
        a_cum_t = jnp.transpose(a_cum)
        ea_ref[...] = _dot_sel_right(jnp.exp(a_cum), expand_ref[...], 2)
        dte_x = _dot_sel_right(jnp.exp(a_tot - a_cum), expand_ref[...], 2)
        cd_col = _dot_sel_left(expand_t_ref[...], jnp.exp(jnp.transpose(a_tot)), 2)
        cd_hi, cd_lo = _split2(cd_col)
        cdh_ref[...] = cd_hi
        cdl_ref[...] = cd_lo
        bm_ref[...] = bm.astype(bf16)
        cm_ref[...] = cm.astype(bf16)
        for g in range(N_SSD_GROUPS):
            cg = cm[:, g * D_STATE:(g + 1) * D_STATE].astype(bf16)
            bg = bm[:, g * D_STATE:(g + 1) * D_STATE].astype(bf16)
            cb = lax.dot_general(cg, bg, NT_DIMS, preferred_element_type=f32)
            y_diag = _ssd_diag_group(g, cb, a_cum, a_cum_t, keep, xdt)
            ydiag_ref[:, g * gw:(g + 1) * gw] = y_diag + xs[:, g * gw:(g + 1) * gw] * dskip_ref[:, g * gw:(g + 1) * gw]
            xdte_t_ref[g * gw:(g + 1) * gw, :] = jnp.transpose(
                xdt[:, g * gw:(g + 1) * gw] * dte_x[:, g * gw:(g + 1) * gw]).astype(bf16)
        yt_ref[...] = jnp.zeros(yt_ref.shape, f32)

        u = u_ref[...]
        pst = pst_ref[...]
        step = lax.broadcasted_iota(jnp.int32, (BLK, 1), 0) % DEC_SEQ
        pos = (step + (pos0 + 1)).astype(f32)
        for gi, w in enumerate(POOL_WINDOWS):
            sl = slice(gi * POOL_GROUP_DIM, (gi + 1) * POOL_GROUP_DIM)
            ug = u[:, sl]
            wsum = _dot_sel_left(pcur_ref[gi], ug, 2) + _dot_sel_left(phist_ref[gi], pst[:, sl], 2)
            pooled = wsum / jnp.minimum(pos, jnp.float32(w)) - ug
            po = jnp.dot(pooled.astype(bf16), pw_ref[gi], preferred_element_type=f32) * ps_ref[:, sl]
            mix_ref[:, D_SSD + gi * POOL_GROUP_DIM:D_SSD + (gi + 1) * POOL_GROUP_DIM] = po.astype(bf16)

    rows_of_s = lax.broadcasted_iota(jnp.int32, (BLK, LANES), 0) // DEC_SEQ == s
    cols_of_s = lax.broadcasted_iota(jnp.int32, (gw, BLK), 1) // DEC_SEQ == s
    pick_s = jnp.where(lax.broadcasted_iota(jnp.int32, (BLK, LANES), 0) == DEC_SEQ * s, 1.0, 0.0).astype(bf16)
    state = ssm_in_ref[0].reshape(D_SSD, D_STATE)
    for g in range(N_SSD_GROUPS):
        rs = slice(g * gw, (g + 1) * gw)
        sg = state[rs, :]
        cg = cm_ref[:, g * D_STATE:(g + 1) * D_STATE]
        bg = bm_ref[:, g * D_STATE:(g + 1) * D_STATE]
        yt = lax.dot_general(sg.astype(bf16), cg, NT_DIMS, preferred_element_type=f32)
        yt_ref[rs, :] += jnp.where(cols_of_s, yt, 0.0)
        cd = (jnp.dot(cdh_ref[rs, :], pick_s, preferred_element_type=f32)
              + jnp.dot(cdl_ref[rs, :], pick_s, preferred_element_type=f32))
        upd = jnp.dot(xdte_t_ref[rs, :], jnp.where(rows_of_s, bg, jnp.zeros_like(bg)),
                      preferred_element_type=f32)
        ssm_out_ref[0, g * (N_SSD_HEADS // N_SSD_GROUPS):(g + 1) * (N_SSD_HEADS // N_SSD_GROUPS)] = (
            sg * cd + upd).reshape(N_SSD_HEADS // N_SSD_GROUPS, SSD_HEAD_DIM, D_STATE)

    @pl.when(s == pl.num_programs(1) - 1)
    def _():
        y = ydiag_ref[...] + jnp.transpose(yt_ref[...]) * ea_ref[...]
        mix_ref[:, 0:D_SSD] = _gated_norm(y, z_ref[...], nw_ref).astype(bf16)


def _sample_mixer(z, xbc, dt, u, row0, n_seq, state_conv, state_ssm, state_pool, pos0,
                  conv_w, conv_b, dt_bias, A_log, D_skip, ssd_norm_w, pool_w, pool_scale):
    f32, bf16 = jnp.float32, jnp.bfloat16
    n_blk = n_seq // SEQ_PER_BLK
    blk0 = row0 // BLK
    nk = CONV_WIDTH - 1
    k = _mixer_constants()
    ks = _sample_constants()

    def row_blk(width):
        return pl.BlockSpec((BLK, width), lambda j, s: (blk0 + j, 0))

    def const(shape):
        return pl.BlockSpec(shape, lambda j, s: (0,) * len(shape))

    state_spec = pl.BlockSpec((1, N_SSD_HEADS, SSD_HEAD_DIM, D_STATE), lambda j, s: (j * SEQ_PER_BLK + s, 0, 0, 0))
    pad_h = (0, LANES - N_SSD_HEADS)
    return pl.pallas_call(
        functools.partial(_sample_mixer_body, pos0=pos0),
        grid=(n_blk, SEQ_PER_BLK),
        in_specs=[row_blk(D_SSD), row_blk(D_CONV), row_blk(LANES), row_blk(D_POOL),
                  pl.BlockSpec((SEQ_PER_BLK * nk, D_CONV), lambda j, s: (j, 0)),
                  pl.BlockSpec((SEQ_PER_BLK * POOL_HIST, D_POOL), lambda j, s: (j, 0)),
                  state_spec,
                  const((CONV_WIDTH, D_CONV)), const((1, D_CONV)), const((1, LANES)), const((1, LANES)),
                  const((1, D_SSD)), const((1, D_SSD)),
                  const((N_POOL_GROUPS, POOL_GROUP_DIM, POOL_GROUP_DIM)), const((1, D_POOL)),
                  const((BLK, BLK)), const((BLK, BLK)), const((LANES, D_SSD)), const((D_SSD, LANES)),
                  const((nk, BLK, BLK)), const((nk, BLK, SEQ_PER_BLK * nk)),
                  const((N_POOL_GROUPS, BLK, BLK)), const((N_POOL_GROUPS, BLK, SEQ_PER_BLK * POOL_HIST))],
        out_specs=[pl.BlockSpec((BLK, D_MIX), lambda j, s: (j, 0)), state_spec],
        out_shape=[jax.ShapeDtypeStruct((n_seq * DEC_SEQ, D_MIX), bf16),
                   jax.ShapeDtypeStruct((n_seq, N_SSD_HEADS, SSD_HEAD_DIM, D_STATE), f32)],
        scratch_shapes=[pltpu.VMEM((BLK, D_SSD), f32), pltpu.VMEM((BLK, D_SSD), f32),
                        pltpu.VMEM((D_SSD, BLK), f32), pltpu.VMEM((D_SSD, BLK), bf16),
                        pltpu.VMEM((D_SSD, BLK), bf16), pltpu.VMEM((D_SSD, BLK), bf16),
                        pltpu.VMEM((BLK, N_SSD_GROUPS * D_STATE), bf16),
                        pltpu.VMEM((BLK, N_SSD_GROUPS * D_STATE), bf16)],
        compiler_params=pltpu.CompilerParams(dimension_semantics=("parallel", "arbitrary"),
                                             vmem_limit_bytes=VMEM_LIMIT_BYTES),
        name="sample_mixer",
    )(z, xbc, dt, u, state_conv.reshape(n_seq * nk, D_CONV), state_pool.reshape(n_seq * POOL_HIST, D_POOL),
      state_ssm, conv_w.astype(f32), conv_b.reshape(1, D_CONV).astype(f32),
      jnp.pad(dt_bias.astype(f32), pad_h).reshape(1, LANES), jnp.pad(A_log.astype(f32), pad_h).reshape(1, LANES),
      jnp.repeat(D_skip.astype(f32), SSD_HEAD_DIM).reshape(1, D_SSD), ssd_norm_w.reshape(1, D_SSD).astype(f32),
      pool_w.astype(bf16), pool_scale.reshape(1, D_POOL).astype(f32),
      ks["causal"], ks["same"], k["expand"], k["expand_t"], ks["shift"], ks["stsel"],
      ks["pcur"], ks["phist"])


def _moe_sizes(n_tokens, tm):
    nt = n_tokens // tm
    lmax = -(-(TOP_K * tm + N_EXPERTS * (MOE_SEG_ROWS - 1)) // MOE_SEL_ROWS) * MOE_SEL_ROWS
    rows = (TOP_K * n_tokens + nt * N_EXPERTS * (MOE_SEG_ROWS - 1) + N_EXPERTS * (MOE_PIECE - 1)
            + MOE_ROW_CHUNK)
    n_rows = -(-rows // MOE_PIECE) * MOE_PIECE
    return nt, lmax, n_rows


def _router_body(x_ref, nw_ref, rwh_ref, rwl_ref, rb_ref, h_ref, posg_ref, post_ref, cnt_ref):
    f32, bf16 = jnp.float32, jnp.bfloat16
    tm = x_ref.shape[0]
    x = x_ref[...]
    h = x * lax.rsqrt(jnp.mean(x * x, axis=-1, keepdims=True) + EPS) * nw_ref[...]
    h_hi = h.astype(bf16)
    h_ref[...] = h_hi
    h_lo = (h - h_hi.astype(f32)).astype(bf16)
    wh = rwh_ref[...]
    logits = (jnp.dot(h_hi, wh, preferred_element_type=f32)
              + jnp.dot(h_lo, wh, preferred_element_type=f32)
              + jnp.dot(h_hi, rwl_ref[...], preferred_element_type=f32)) + rb_ref[...]
    lane = lax.broadcasted_iota(jnp.int32, (tm, LANES), 1)
    lanef = lane.astype(f32)
    neg = jnp.float32(-jnp.inf)
    l = jnp.where(lane < N_EXPERTS, logits, neg)
    sels, vals = [], []
    for _ in range(TOP_K):
        m = jnp.max(l, axis=1, keepdims=True)
        idx = jnp.min(jnp.where(l == m, lanef, jnp.float32(LANES)), axis=1, keepdims=True)
        sel = lanef == idx
        l = jnp.where(sel, neg, l)
        sels.append(sel)
        vals.append(m)
    exps = [jnp.exp(v - vals[0]) for v in vals]
    denom = exps[0] + exps[1] + exps[2] + exps[3]
    gates = [e / denom for e in exps]
    chosen = jnp.where(sels[0] | sels[1] | sels[2] | sels[3], 1.0, 0.0).astype(f32)
    row = lax.broadcasted_iota(jnp.int32, (tm, tm), 0)
    col = lax.broadcasted_iota(jnp.int32, (tm, tm), 1)
    lower = jnp.where(col < row, 1.0, 0.0).astype(bf16)
    rank = jnp.dot(lower, chosen.astype(bf16), preferred_element_type=f32)
    cnt = jnp.sum(chosen, axis=0, keepdims=True)
    seg_units = jnp.floor((cnt + (MOE_SEG_ROWS - 1)) * (1.0 / MOE_SEG_ROWS))
    r2 = lax.broadcasted_iota(jnp.int32, (LANES, LANES), 0)
    c2 = lax.broadcasted_iota(jnp.int32, (LANES, LANES), 1)
    upper = jnp.where(r2 < c2, 1.0, 0.0).astype(bf16)
    lstart = jnp.dot(jnp.broadcast_to(seg_units, (8, LANES)).astype(bf16), upper,
                     preferred_element_type=f32)[0:1, :] * MOE_SEG_ROWS
    posmat = lstart + rank
    posg = jnp.zeros((tm, LANES), f32)
    for k in range(TOP_K):
        pos_k = jnp.sum(jnp.where(sels[k], posmat, 0.0), axis=1, keepdims=True)
        posg = posg + jnp.where(lane == k, pos_k, 0.0) + jnp.where(lane == TOP_K + k, gates[k], 0.0)
    posg_ref[...] = posg
    post_ref[...] = jnp.transpose(posg)[0:8, :]
    cnt_ref[0] = jnp.broadcast_to(cnt, (8, LANES)).astype(jnp.int32)


def _moe_router(x1, norm2_w, router_w, router_b, tm):
    n, d = x1.shape
    nt = n // tm
    f32, bf16 = jnp.float32, jnp.bfloat16
    rw = jnp.pad(router_w.astype(f32), ((0, 0), (0, LANES - N_EXPERTS)))
    rw_hi = rw.astype(bf16)
    rw_lo = (rw - rw_hi.astype(f32)).astype(bf16)
    rb = jnp.pad(router_b.astype(f32), (0, LANES - N_EXPERTS)).reshape(1, LANES)
    return pl.pallas_call(
        _router_body,
        grid=(nt,),
        in_specs=[pl.BlockSpec((tm, d), lambda i: (i, 0)),
                  pl.BlockSpec((1, d), lambda i: (0, 0)),
                  pl.BlockSpec((d, LANES), lambda i: (0, 0)),
                  pl.BlockSpec((d, LANES), lambda i: (0, 0)),
                  pl.BlockSpec((1, LANES), lambda i: (0, 0))],
        out_specs=[pl.BlockSpec((tm, d), lambda i: (i, 0)),
                   pl.BlockSpec((tm, LANES), lambda i: (i, 0)),
                   pl.BlockSpec((8, tm), lambda i: (0, i)),
                   pl.BlockSpec((1, 8, LANES), lambda i: (i, 0, 0))],
        out_shape=[jax.ShapeDtypeStruct((n, d), bf16),
                   jax.ShapeDtypeStruct((n, LANES), f32),
                   jax.ShapeDtypeStruct((8, n), f32),
                   jax.ShapeDtypeStruct((nt, 8, LANES), jnp.int32)],
        compiler_params=pltpu.CompilerParams(dimension_semantics=("parallel",),
                                             vmem_limit_bytes=VMEM_LIMIT_BYTES),
        name="moe_router",
    )(x1, norm2_w.reshape(1, d).astype(f32), rw_hi, rw_lo, rb)


def _moe_plan(cnt):
    i32 = jnp.int32
    pad = (cnt + (MOE_SEG_ROWS - 1)) // MOE_SEG_ROWS * MOE_SEG_ROWS
    lstart = jnp.cumsum(pad, axis=1) - pad
    lp = jnp.sum(pad, axis=1)
    tot = jnp.sum(pad, axis=0)
    reg = (tot + (MOE_PIECE - 1)) // MOE_PIECE * MOE_PIECE
    reg_end = jnp.cumsum(reg)
    estart = reg_end - reg
    seg = estart[None, :] + jnp.cumsum(pad, axis=0) - pad
    return dict(
        lstart=lstart.reshape(-1).astype(i32), seg_units=(pad // MOE_SEG_ROWS).reshape(-1).astype(i32),
        seg=seg.reshape(-1).astype(i32), lp=lp.astype(i32),
        tail_start=(estart + tot).astype(i32), tail_units=((reg - tot) // MOE_SEG_ROWS).astype(i32),
        estart=estart.astype(i32), erows=reg.astype(i32), used=reg_end[-1].reshape(1).astype(i32))


def _for_each_segment_copy(i, lstart_ref, units_ref, seg_ref, local_ref, global_ref, sem, to_global, fn):
    def per_expert(e, carry):
        k = i * N_EXPERTS + e

        @pl.when(units_ref[k] > 0)
        def _():
            n = pl.multiple_of(units_ref[k] * MOE_SEG_ROWS, MOE_SEG_ROWS)
            loc = local_ref.at[pl.ds(pl.multiple_of(lstart_ref[k], MOE_SEG_ROWS), n)]
            glo = global_ref.at[pl.ds(pl.multiple_of(seg_ref[k], MOE_SEG_ROWS), n)]
            fn(pltpu.make_async_copy(loc, glo, sem) if to_global else pltpu.make_async_copy(glo, loc, sem))
        return carry
    lax.fori_loop(0, N_EXPERTS, per_expert, 0)


def _for_each_unused_piece(used_ref, zero_ref, rows_ref, sem, fn):
    def per_piece(j, c):
        go = pl.multiple_of(j * MOE_PIECE, MOE_PIECE)
        fn(pltpu.make_async_copy(zero_ref, rows_ref.at[pl.ds(go, MOE_PIECE)], sem))
        return c
    lax.fori_loop(used_ref[0] // MOE_PIECE, rows_ref.shape[0] // MOE_PIECE, per_piece, 0)


def _dispatch_body(lstart_ref, units_ref, seg_ref, lp_ref, tail_start_ref, tail_units_ref, used_ref,
                   h_ref, post_ref, xs_ref, stage_ref, sel_ref, zero_ref, sems, fill_sem):
    f32, bf16 = jnp.float32, jnp.bfloat16
    i = pl.program_id(0)
    nt = pl.num_programs(0)
    slot = i % 2
    tm = h_ref.shape[0]
    lmax = stage_ref.shape[1]

    def for_each_fill_copy(fn):
        def per_expert(e, carry):
            @pl.when(tail_units_ref[e] > 0)
            def _():
                n = pl.multiple_of(tail_units_ref[e] * MOE_SEG_ROWS, MOE_SEG_ROWS)
                go = pl.multiple_of(tail_start_ref[e], MOE_SEG_ROWS)
                fn(pltpu.make_async_copy(zero_ref.at[pl.ds(0, n)], xs_ref.at[pl.ds(go, n)], fill_sem))
            return carry
        lax.fori_loop(0, N_EXPERTS, per_expert, 0)
        _for_each_unused_piece(used_ref, zero_ref, xs_ref, fill_sem, fn)

    @pl.when(i == 0)
    def _():
        zero_ref[...] = jnp.zeros(zero_ref.shape, bf16)
        for_each_fill_copy(lambda cp: cp.start())
        for_each_fill_copy(lambda cp: cp.wait())

    def segment_copies(tile, slot_, fn):
        _for_each_segment_copy(tile, lstart_ref, units_ref, seg_ref, stage_ref.at[slot_], xs_ref,
                               sems.at[slot_], True, fn)

    @pl.when(i >= 2)
    def _():
        segment_copies(i - 2, slot, lambda cp: cp.wait())

    pos = post_ref[0:TOP_K, :]
    for part in range(lmax // MOE_SEL_ROWS):
        @pl.when(part * MOE_SEL_ROWS < lp_ref[i])
        def _():
            for c in range(MOE_SEL_ROWS // MOE_CHUNK):
                r0 = part * MOE_SEL_ROWS + c * MOE_CHUNK
                r = (lax.broadcasted_iota(jnp.int32, (MOE_CHUNK, tm), 0) + r0).astype(f32)
                hit = (pos[0:1, :] == r) | (pos[1:2, :] == r) | (pos[2:3, :] == r) | (pos[3:4, :] == r)
                sel_ref[c * MOE_CHUNK:(c + 1) * MOE_CHUNK, :] = jnp.where(hit, 1.0, 0.0).astype(bf16)
            stage_ref[slot, part * MOE_SEL_ROWS:(part + 1) * MOE_SEL_ROWS, :] = jnp.dot(
                sel_ref[...], h_ref[...], preferred_element_type=f32).astype(bf16)

    segment_copies(i, slot, lambda cp: cp.start())

    @pl.when(i == nt - 1)
    def _():
        @pl.when(nt >= 2)
        def _():
            segment_copies(i - 1, 1 - slot, lambda cp: cp.wait())
        segment_copies(i, slot, lambda cp: cp.wait())


def _moe_dispatch(h2, post, plan, tm, lmax, n_rows):
    n, d = h2.shape
    nt = n // tm
    grid_spec = pltpu.PrefetchScalarGridSpec(
        num_scalar_prefetch=7,
        grid=(nt,),
        in_specs=[pl.BlockSpec((tm, d), lambda i, *_: (i, 0)),
                  pl.BlockSpec((8, tm), lambda i, *_: (0, i))],
        out_specs=pl.BlockSpec(memory_space=pl.ANY),
        scratch_shapes=[pltpu.VMEM((2, lmax, d), jnp.bfloat16),
                        pltpu.VMEM((MOE_SEL_ROWS, tm), jnp.bfloat16),
                        pltpu.VMEM((MOE_PIECE, d), jnp.bfloat16),
                        pltpu.SemaphoreType.DMA((2,)),
                        pltpu.SemaphoreType.DMA(())])
    return pl.pallas_call(
        _dispatch_body,
        grid_spec=grid_spec,
        out_shape=jax.ShapeDtypeStruct((n_rows, d), jnp.bfloat16),
        compiler_params=pltpu.CompilerParams(dimension_semantics=("arbitrary",),
                                             vmem_limit_bytes=VMEM_LIMIT_BYTES),
        name="moe_dispatch",
    )(plan["lstart"], plan["seg_units"], plan["seg"], plan["lp"], plan["tail_start"], plan["tail_units"],
      plan["used"], h2, post)


def _experts_body(estart_ref, erows_ref, used_ref,
                  xs_ref, wgu_ref, bgu_ref, wd_ref, bd_ref, os_ref,
                  wgu_bf, wd_bf, xbuf, obuf, zero_ref, in_sems, out_sems, fill_sem):
    f32, bf16 = jnp.float32, jnp.bfloat16
    e = pl.program_id(0)
    start = estart_ref[e]
    rows = erows_ref[e]
    n_chunks = (rows + (MOE_ROW_CHUNK - 1)) // MOE_ROW_CHUNK
    half = D_FF // 2

    def in_copy(c, slot):
        src = xs_ref.at[pl.ds(pl.multiple_of(start + c * MOE_ROW_CHUNK, MOE_PIECE), MOE_ROW_CHUNK)]
        return pltpu.make_async_copy(src, xbuf.at[slot], in_sems.at[slot])

    def out_copy(c, slot):
        n = pl.multiple_of(jnp.minimum(MOE_ROW_CHUNK, rows - c * MOE_ROW_CHUNK), MOE_PIECE)
        go = pl.multiple_of(start + c * MOE_ROW_CHUNK, MOE_PIECE)
        return pltpu.make_async_copy(obuf.at[slot, pl.ds(0, n)], os_ref.at[pl.ds(go, n)], out_sems.at[slot])

    @pl.when(e == 0)
    def _():
        zero_ref[...] = jnp.zeros(zero_ref.shape, bf16)
        _for_each_unused_piece(used_ref, zero_ref, os_ref, fill_sem, lambda cp: cp.start())
        _for_each_unused_piece(used_ref, zero_ref, os_ref, fill_sem, lambda cp: cp.wait())

    @pl.when(n_chunks > 0)
    def _():
        in_copy(0, 0).start(priority=1)
        wgu_bf[...] = wgu_ref[0].astype(bf16)
        wd_bf[...] = wd_ref[0].astype(bf16)

        def chunk(c, carry):
            slot = c % 2
            in_copy(c, slot).wait()

            @pl.when(c + 1 < n_chunks)
            def _():
                in_copy(c + 1, 1 - slot).start(priority=1)

            @pl.when(c >= 2)
            def _():
                out_copy(c - 2, slot).wait()

            x = xbuf[slot]
            out = bd_ref[0]
            for hf in range(2):
                gate = jnp.dot(x, wgu_bf[:, hf * half:(hf + 1) * half], preferred_element_type=f32)
                gate = jnp.minimum(gate + bgu_ref[0, :, hf * half:(hf + 1) * half], SWIGLU_LIMIT)
                up = jnp.dot(x, wgu_bf[:, D_FF + hf * half:D_FF + (hf + 1) * half], preferred_element_type=f32)
                up = jnp.clip(up + bgu_ref[0, :, D_FF + hf * half:D_FF + (hf + 1) * half],
                              -SWIGLU_LIMIT, SWIGLU_LIMIT)
                act = (up + 1.0) * (gate * jax.nn.sigmoid(SWIGLU_ALPHA * gate))
                out = out + jnp.dot(act.astype(bf16), wd_bf[hf * half:(hf + 1) * half, :],
                                    preferred_element_type=f32)
            obuf[slot] = out.astype(bf16)
            out_copy(c, slot).start(priority=1)
            return carry
        lax.fori_loop(0, n_chunks, chunk, 0)

        @pl.when(n_chunks >= 2)
        def _():
            out_copy(n_chunks - 2, n_chunks % 2).wait()
        out_copy(n_chunks - 1, (n_chunks - 1) % 2).wait()


def _moe_experts(xs, plan, w_gate_up, b_gate_up, w_down, b_down):
    d = xs.shape[1]
    grid_spec = pltpu.PrefetchScalarGridSpec(
        num_scalar_prefetch=3,
        grid=(N_EXPERTS,),
        in_specs=[pl.BlockSpec(memory_space=pl.ANY),
                  pl.BlockSpec((1, d, 2 * D_FF), lambda e, *_: (e, 0, 0)),
                  pl.BlockSpec((1, 1, 2 * D_FF), lambda e, *_: (e, 0, 0)),
                  pl.BlockSpec((1, D_FF, d), lambda e, *_: (e, 0, 0)),
                  pl.BlockSpec((1, 1, d), lambda e, *_: (e, 0, 0))],
        out_specs=pl.BlockSpec(memory_space=pl.ANY),
        scratch_shapes=[pltpu.VMEM((d, 2 * D_FF), jnp.bfloat16),
                        pltpu.VMEM((D_FF, d), jnp.bfloat16),
                        pltpu.VMEM((2, MOE_ROW_CHUNK, d), jnp.bfloat16),
                        pltpu.VMEM((2, MOE_ROW_CHUNK, d), jnp.bfloat16),
                        pltpu.VMEM((MOE_PIECE, d), jnp.bfloat16),
                        pltpu.SemaphoreType.DMA((2,)),
                        pltpu.SemaphoreType.DMA((2,)),
                        pltpu.SemaphoreType.DMA(())])
    return pl.pallas_call(
        _experts_body,
        grid_spec=grid_spec,
        out_shape=jax.ShapeDtypeStruct(xs.shape, jnp.bfloat16),
        compiler_params=pltpu.CompilerParams(dimension_semantics=("arbitrary",),
                                             vmem_limit_bytes=VMEM_LIMIT_BYTES),
        name="moe_experts",
    )(plan["estart"], plan["erows"], plan["used"],
      xs, w_gate_up, b_gate_up.reshape(N_EXPERTS, 1, 2 * D_FF), w_down, b_down.reshape(N_EXPERTS, 1, d))


def _combine_body(lstart_ref, units_ref, seg_ref, lp_ref,
                  os_ref, posg_ref, x_ref, fw_ref, yp_ref, ys_ref, stage_ref, w_ref, sems,
                  *, n_prompt_tiles):
    f32, bf16 = jnp.float32, jnp.bfloat16
    i = pl.program_id(0)
    nt = pl.num_programs(0)
    slot = i % 2
    tm = x_ref.shape[0]
    lmax = stage_ref.shape[1]

    def segment_copies(tile, slot_, fn):
        _for_each_segment_copy(tile, lstart_ref, units_ref, seg_ref, stage_ref.at[slot_], os_ref,
                               sems.at[slot_], False, fn)

    @pl.when(i == 0)
    def _():
        stage_ref[...] = jnp.zeros(stage_ref.shape, bf16)
        segment_copies(0, 0, lambda cp: cp.start())

    @pl.when(i + 1 < nt)
    def _():
        segment_copies(i + 1, 1 - slot, lambda cp: cp.start())

    posg = posg_ref[...]
    pos = [posg[:, k:k + 1] for k in range(TOP_K)]
    gate = [posg[:, TOP_K + k:TOP_K + k + 1] for k in range(TOP_K)]
    for c in range(lmax // MOE_CHUNK):
        r = (lax.broadcasted_iota(jnp.int32, (tm, MOE_CHUNK), 1) + c * MOE_CHUNK).astype(f32)
        w = jnp.zeros((tm, MOE_CHUNK), f32)
        for k in range(TOP_K):
            w = jnp.where(pos[k] == r, gate[k], w)
        w_ref[:, c * MOE_CHUNK:(c + 1) * MOE_CHUNK] = w.astype(bf16)

    segment_copies(i, slot, lambda cp: cp.wait())
    y = x_ref[...] + jnp.dot(w_ref[...], stage_ref[slot], preferred_element_type=f32)
    out = y * lax.rsqrt(jnp.mean(y * y, axis=-1, keepdims=True) + EPS) * fw_ref[...]

    @pl.when(i < n_prompt_tiles)
    def _():
        yp_ref[...] = out

    @pl.when(i >= n_prompt_tiles)
    def _():
        ys_ref[...] = out


def _moe_combine(os_, posg, x1, final_norm_w, plan, tm, lmax, n_prompt):
    n, d = x1.shape
    nt = n // tm
    n_prompt_tiles = n_prompt // tm
    n_sample_tiles = nt - n_prompt_tiles
    grid_spec = pltpu.PrefetchScalarGridSpec(
        num_scalar_prefetch=4,
        grid=(nt,),
        in_specs=[pl.BlockSpec(memory_space=pl.ANY),
                  pl.BlockSpec((tm, LANES), lambda i, *_: (i, 0)),
                  pl.BlockSpec((tm, d), lambda i, *_: (i, 0)),
                  pl.BlockSpec((1, d), lambda i, *_: (0, 0))],
        out_specs=[pl.BlockSpec((tm, d), lambda i, *_: (jnp.minimum(i, n_prompt_tiles - 1), 0)),
                   pl.BlockSpec((tm, d), lambda i, *_: (jnp.maximum(i - n_prompt_tiles, 0), 0))],
        scratch_shapes=[pltpu.VMEM((2, lmax, d), jnp.bfloat16),
                        pltpu.VMEM((tm, lmax), jnp.bfloat16),
                        pltpu.SemaphoreType.DMA((2,))])
    return pl.pallas_call(
        functools.partial(_combine_body, n_prompt_tiles=n_prompt_tiles),
        grid_spec=grid_spec,
        out_shape=[jax.ShapeDtypeStruct((n_prompt, d), jnp.float32),
                   jax.ShapeDtypeStruct((n_sample_tiles * tm, d), jnp.float32)],
        compiler_params=pltpu.CompilerParams(dimension_semantics=("arbitrary",),
                                             vmem_limit_bytes=VMEM_LIMIT_BYTES),
        name="moe_combine",
    )(plan["lstart"], plan["seg_units"], plan["seg"], plan["lp"],
      os_, posg, x1, final_norm_w.reshape(1, d).astype(jnp.float32))


def _moe_block(x1, n_prompt, norm2_w, router_w, router_b, w_gate_up, b_gate_up, w_down, b_down,
               final_norm_w, tm=MOE_TOKEN_TILE):
    n = x1.shape[0]
    nt, lmax, n_rows = _moe_sizes(n, tm)
    h2, posg, post, cnt3 = _moe_router(x1, norm2_w, router_w, router_b, tm)
    plan = _moe_plan(cnt3[:, 0, :N_EXPERTS])
    xs = _moe_dispatch(h2, post, plan, tm, lmax, n_rows)
    os_ = _moe_experts(xs, plan, w_gate_up, b_gate_up, w_down, b_down)
    return _moe_combine(os_, posg, x1, final_norm_w, plan, tm, lmax, n_prompt)


def kernel(x_prompt, x_sample, state_ssm, state_conv, state_pool, norm1_w, w_in, conv_w, conv_b, dt_bias,
           A_log, D_skip, ssd_norm_w, pool_w, pool_scale, w_out, norm2_w, router_w, router_b, w_gate_up,
           b_gate_up, w_down, b_down, final_norm_w):
    n_prompt = BATCH * SEQ
    n_sample = DEC_BATCH * DEC_SEQ
    xp = x_prompt.reshape(n_prompt, D_MODEL)
    xs = x_sample.reshape(n_sample, D_MODEL)
    z, xbc, dt_raw, u = _in_proj(xp, xs, norm1_w[0], w_in[0])
    mp = (conv_w[0], conv_b[0], dt_bias[0], A_log[0], D_skip[0], ssd_norm_w[0], pool_w[0], pool_scale[0])
    mix_p, s1 = _prompt_mixer(z, xbc, dt_raw, u, BATCH, SEQ, *mp)
    mix_s, s2 = _sample_mixer(z, xbc, dt_raw, u, n_prompt, DEC_BATCH, state_conv[0], state_ssm[0], state_pool[0],
                              PAST_LEN, *mp)
    nk = CONV_WIDTH - 1
    c1 = jnp.stack([xbc[(b + 1) * SEQ - nk:(b + 1) * SEQ] for b in range(BATCH)])
    p1 = jnp.stack([u[(b + 1) * SEQ - POOL_HIST:(b + 1) * SEQ] for b in range(BATCH)])
    c2 = xbc[n_prompt:].reshape(DEC_BATCH, DEC_SEQ, D_CONV)[:, DEC_SEQ - nk:]
    p2 = jnp.concatenate([state_pool[0][:, DEC_SEQ:], u[n_prompt:].reshape(DEC_BATCH, DEC_SEQ, D_POOL)], axis=1)
    x1 = _out_proj(mix_p, mix_s, w_out[0], xp, xs)
    yp, ys = _moe_block(x1, n_prompt, norm2_w[0], router_w[0], router_b[0], w_gate_up[0], b_gate_up[0],
                        w_down[0], b_down[0], final_norm_w)
    return (yp.reshape(x_prompt.shape), ys.reshape(x_sample.shape),
            s1[None], c1[None], p1[None], s2[None], c2[None], p2[None])
```

```python
import functools
import math
import jax, jax.numpy as jnp
from jax import lax
import numpy as np
from jax.experimental import pallas as pl
from jax.experimental.pallas import tpu as pltpu

D_MODEL = 1024
BATCH = 8
SEQ = 2048
DEC_BATCH = 128
DEC_SEQ = 4
PAST_LEN = 16384

D_MIX = 2 * D_MODEL
D_SSD = 3 * D_MIX // 4
SSD_HEAD_DIM = 64
N_SSD_HEADS = D_SSD // SSD_HEAD_DIM
N_SSD_GROUPS = 4
D_STATE = 128
CONV_WIDTH = 4
SSD_CHUNK = 128
D_CONV = D_SSD + 2 * N_SSD_GROUPS * D_STATE
D_POOL = D_MIX - D_SSD
POOL_WINDOWS = (2, 4, 8, 16)
N_POOL_GROUPS = len(POOL_WINDOWS)
POOL_GROUP_DIM = D_POOL // N_POOL_GROUPS
POOL_HIST = max(POOL_WINDOWS) - 1
D_IN_PROJ = D_SSD + D_CONV + N_SSD_HEADS + D_POOL
N_EXPERTS = 32
TOP_K = 4
D_FF = D_MODEL
SWIGLU_LIMIT = 7.0
SWIGLU_ALPHA = 1.702
EPS = 1e-5

LANES = 128
BF16_SUBLANES = 16
VMEM_LIMIT_BYTES = 48 * 1024 * 1024

MOE_TOKEN_TILE = 512
MOE_SEG_ROWS = BF16_SUBLANES
MOE_PIECE = 128
MOE_ROW_CHUNK = 512
MOE_CHUNK = 256
MOE_SEL_ROWS = 1280


BLK = SSD_CHUNK
PROJ_ROW_TILE = 512
HIST_ROWS = 16
CONV_TAIL_ROWS = 8
NT_DIMS = (((1,), (1,)), ((), ()))


def _split2(v):
    hi = v.astype(jnp.bfloat16)
    lo = (v - hi.astype(jnp.float32)).astype(jnp.bfloat16)
    return hi, lo


def _dot_sel_left(sel, v, passes):
    out = None
    rem = v
    for p in range(passes):
        part = rem.astype(jnp.bfloat16)
        d = jnp.dot(sel, part, preferred_element_type=jnp.float32)
        out = d if out is None else out + d
        if p + 1 < passes:
            rem = rem - part.astype(jnp.float32)
    return out


def _dot_sel_right(v, sel, passes):
    out = None
    rem = v
    for p in range(passes):
        part = rem.astype(jnp.bfloat16)
        d = jnp.dot(part, sel, preferred_element_type=jnp.float32)
        out = d if out is None else out + d
        if p + 1 < passes:
            rem = rem - part.astype(jnp.float32)
    return out


def _two_part_specs(n_first, n_second, tm, width):
    t1 = n_first // tm
    t2 = n_second // tm
    return (pl.BlockSpec((tm, width), lambda i: (jnp.minimum(i, t1 - 1), 0)),
            pl.BlockSpec((tm, width), lambda i: (jnp.clip(i - t1, 0, t2 - 1), 0)))


def _in_proj_body(xa_ref, xb_ref, nw_ref, w_ref, z_ref, xbc_ref, dt_ref, u_ref, *, tiles_a):
    x = jnp.where(pl.program_id(0) < tiles_a, xa_ref[...], xb_ref[...])
    h = (x * lax.rsqrt(jnp.mean(x * x, axis=-1, keepdims=True) + EPS) * nw_ref[...]).astype(jnp.bfloat16)
    off = 0
    for ref in (z_ref, xbc_ref, dt_ref, u_ref):
        n = ref.shape[1]
        ref[...] = jnp.dot(h, w_ref[:, off:off + n], preferred_element_type=jnp.float32)
        off += n


def _in_proj(xa, xb, norm1_w, w_in):
    d = xa.shape[1]
    n = xa.shape[0] + xb.shape[0]
    f32, bf16 = jnp.float32, jnp.bfloat16
    s1, s2 = D_SSD + D_CONV, D_SSD + D_CONV + N_SSD_HEADS
    w = jnp.concatenate([w_in[:, :s1], jnp.pad(w_in[:, s1:s2], ((0, 0), (0, LANES - N_SSD_HEADS))),
                         w_in[:, s2:]], axis=1).astype(bf16)
    widths = (D_SSD, D_CONV, LANES, D_POOL)
    tm = PROJ_ROW_TILE
    return pl.pallas_call(
        functools.partial(_in_proj_body, tiles_a=xa.shape[0] // tm),
        grid=(n // tm,),
        in_specs=[*_two_part_specs(xa.shape[0], xb.shape[0], tm, d),
                  pl.BlockSpec((1, d), lambda i: (0, 0)),
                  pl.BlockSpec((d, sum(widths)), lambda i: (0, 0), pipeline_mode=pl.Buffered(1))],
        out_specs=[pl.BlockSpec((tm, wd), lambda i: (i, 0)) for wd in widths],
        out_shape=[jax.ShapeDtypeStruct((n, wd), f32) for wd in widths],
        compiler_params=pltpu.CompilerParams(dimension_semantics=("parallel",),
                                             vmem_limit_bytes=VMEM_LIMIT_BYTES),
        name="in_proj",
    )(xa, xb, norm1_w.reshape(1, d).astype(f32), w)


def _out_proj_body(ma_ref, mb_ref, w_ref, xa_ref, xb_ref, o_ref, *, tiles_a):
    first = pl.program_id(0) < tiles_a
    m = jnp.where(first, ma_ref[...], mb_ref[...])
    x = jnp.where(first, xa_ref[...], xb_ref[...])
    o_ref[...] = x + jnp.dot(m, w_ref[...], preferred_element_type=jnp.float32)


def _out_proj(ma, mb, w_out, xa, xb):
    d = xa.shape[1]
    n = xa.shape[0] + xb.shape[0]
    tm = PROJ_ROW_TILE
    return pl.pallas_call(
        functools.partial(_out_proj_body, tiles_a=xa.shape[0] // tm),
        grid=(n // tm,),
        in_specs=[*_two_part_specs(xa.shape[0], xb.shape[0], tm, D_MIX),
                  pl.BlockSpec((D_MIX, d), lambda i: (0, 0)),
                  *_two_part_specs(xa.shape[0], xb.shape[0], tm, d)],
        out_specs=pl.BlockSpec((tm, d), lambda i: (i, 0)),
        out_shape=jax.ShapeDtypeStruct((n, d), jnp.float32),
        compiler_params=pltpu.CompilerParams(dimension_semantics=("parallel",),
                                             vmem_limit_bytes=VMEM_LIMIT_BYTES),
        name="out_proj",
    )(ma, mb, w_out.astype(jnp.bfloat16), xa, xb)


def _mixer_constants():
    bf16 = jnp.bfloat16
    h = np.arange(LANES)[:, None]
    ch = np.arange(D_SSD)[None, :]
    expand = (ch // SSD_HEAD_DIM == h).astype(np.float32)
    i = np.arange(BLK)[:, None]
    j = np.arange(BLK)[None, :]
    causal = (j <= i).astype(np.float32)
    jh = np.arange(HIST_ROWS)[None, :]
    pcur = np.stack([((j <= i) & (i - j < w)) for w in POOL_WINDOWS]).astype(np.float32)
    phist = np.stack([(i + HIST_ROWS - jh < w) for w in POOL_WINDOWS]).astype(np.float32)
    return dict(expand=jnp.asarray(expand, bf16), expand_t=jnp.asarray(expand.T, bf16),
                causal=jnp.asarray(causal, bf16), pcur=jnp.asarray(pcur, bf16),
                phist=jnp.asarray(phist, bf16))


def _softplus(x):
    return jnp.maximum(x, 0.0) + jnp.log(1.0 + jnp.exp(-jnp.abs(x)))


def _conv_silu(ext_ref, cw_ref, cb_ref, first_row):
    acc = cb_ref[...] + cw_ref[0:1, :] * ext_ref[pl.ds(first_row, BLK), :]
    for k in range(1, CONV_WIDTH):
        acc = acc + cw_ref[k:k + 1, :] * ext_ref[pl.ds(first_row + k, BLK), :]
    return acc * jax.nn.sigmoid(acc)


def _ssd_intra(xbc_c, dt_raw, dtb_ref, alog_ref, causal_bf, expand_ref):
    f32 = jnp.float32
    xs = xbc_c[:, :D_SSD]
    bm = xbc_c[:, D_SSD:D_SSD + N_SSD_GROUPS * D_STATE]
    cm = xbc_c[:, D_SSD + N_SSD_GROUPS * D_STATE:]
    dt = _softplus(dt_raw + dtb_ref[...])
    a = dt * (-jnp.exp(alog_ref[...]))
    a_cum = _dot_sel_left(causal_bf, a, 3)
    dt_x = _dot_sel_right(dt, expand_ref[...], 2)
    return xs, bm, cm, dt, a_cum, xs * dt_x


def _ssd_diag_group(g, cb, a_cum, a_cum_t, keep, xdt):
    f32, bf16 = jnp.float32, jnp.bfloat16
    hg = N_SSD_HEADS // N_SSD_GROUPS
    lane = lax.broadcasted_iota(jnp.int32, (BLK, LANES), 1)
    first_head = lane < SSD_HEAD_DIM
    neg = jnp.float32(-jnp.inf)
    outs = []
    for pr in range(hg * SSD_HEAD_DIM // LANES):
        h1 = g * hg + 2 * pr
        blk = (g * hg * SSD_HEAD_DIM) // LANES + pr
        xp = xdt[:, blk * LANES:(blk + 1) * LANES]
        x1 = jnp.where(first_head, xp, 0.0).astype(bf16)
        x2 = jnp.where(first_head, 0.0, xp).astype(bf16)
        m1 = (cb * jnp.exp(jnp.where(keep, a_cum[:, h1:h1 + 1] - a_cum_t[h1:h1 + 1, :], neg))).astype(bf16)
        m2 = (cb * jnp.exp(jnp.where(keep, a_cum[:, h1 + 1:h1 + 2] - a_cum_t[h1 + 1:h1 + 2, :], neg))).astype(bf16)
        outs.append(jnp.dot(m1, x1, preferred_element_type=f32) + jnp.dot(m2, x2, preferred_element_type=f32))
    return jnp.concatenate(outs, axis=1)


def _gated_norm(y, z, nw_ref):
    yg = y * (z * jax.nn.sigmoid(z))
    return yg * lax.rsqrt(jnp.mean(yg * yg, axis=-1, keepdims=True) + EPS) * nw_ref[...]


def _prompt_mixer_body(z_ref, xbc_ref, dt_ref, u_ref, cw_ref, cb_ref, dtb_ref, alog_ref, dskip_ref, nw_ref,
                       pw_ref, ps_ref, causal_ref, pcur_ref, phist_ref,
                       mix_ref, ssm_ref, ext_ref, pool_tail_ref, state_ref):
    f32, bf16 = jnp.float32, jnp.bfloat16
    c = pl.program_id(1)
    gw = D_SSD // N_SSD_GROUPS

    @pl.when(c == 0)
    def _():
        ext_ref[0:CONV_TAIL_ROWS, :] = jnp.zeros((CONV_TAIL_ROWS, D_CONV), f32)
        pool_tail_ref[...] = jnp.zeros(pool_tail_ref.shape, f32)
        state_ref[...] = jnp.zeros(state_ref.shape, f32)

    ext_ref[CONV_TAIL_ROWS:CONV_TAIL_ROWS + BLK, :] = xbc_ref[...]
    xbc_c = _conv_silu(ext_ref, cw_ref, cb_ref, CONV_TAIL_ROWS - (CONV_WIDTH - 1))
    ext_ref[0:CONV_TAIL_ROWS, :] = xbc_ref[BLK - CONV_TAIL_ROWS:BLK, :]

    causal_bf = causal_ref[...]
    keep = causal_bf > 0
    xs = xbc_c[:, :D_SSD]
    bm = xbc_c[:, D_SSD:D_SSD + N_SSD_GROUPS * D_STATE]
    cm = xbc_c[:, D_SSD + N_SSD_GROUPS * D_STATE:]
    dt = _softplus(dt_ref[...] + dtb_ref[...])
    a_cum = _dot_sel_left(causal_bf, dt * (-jnp.exp(alog_ref[...])), 3)
    a_cum_t = jnp.transpose(a_cum)
    a_tot = a_cum[BLK - 1:BLK, :]
    ea = jnp.exp(a_cum)
    dte = jnp.exp(a_tot - a_cum)
    cd = jnp.exp(a_tot)
    hg = N_SSD_HEADS // N_SSD_GROUPS
    first_head = lax.broadcasted_iota(jnp.int32, (BLK, LANES), 1) < SSD_HEAD_DIM
    neg = jnp.float32(-jnp.inf)

    def head_cols(v, h1):
        return jnp.where(first_head, v[:, h1:h1 + 1], v[:, h1 + 1:h1 + 2])

    def decay_from(h):
        return jnp.exp(jnp.where(keep, a_cum[:, h:h + 1] - a_cum_t[h:h + 1, :], neg))

    y_parts = []
    for g in range(N_SSD_GROUPS):
        cg = cm[:, g * D_STATE:(g + 1) * D_STATE].astype(bf16)
        bg = bm[:, g * D_STATE:(g + 1) * D_STATE].astype(bf16)
        cb = lax.dot_general(cg, bg, NT_DIMS, preferred_element_type=f32)
        sg = state_ref[g * gw:(g + 1) * gw, :]
        y_off = lax.dot_general(cg, sg.astype(bf16), NT_DIMS, preferred_element_type=f32)
        xdte_parts = []
        for pr in range(gw // LANES):
            h1 = g * hg + 2 * pr
            sl = slice(h1 * SSD_HEAD_DIM, h1 * SSD_HEAD_DIM + LANES)
            xp = xs[:, sl] * head_cols(dt, h1)
            x1 = jnp.where(first_head, xp, 0.0).astype(bf16)
            x2 = jnp.where(first_head, 0.0, xp).astype(bf16)
            y_diag = (jnp.dot((cb * decay_from(h1)).astype(bf16), x1, preferred_element_type=f32)
                      + jnp.dot((cb * decay_from(h1 + 1)).astype(bf16), x2, preferred_element_type=f32))
            y_parts.append(y_diag + y_off[:, pr * LANES:(pr + 1) * LANES] * head_cols(ea, h1)
                           + xs[:, sl] * dskip_ref[:, sl])
            xdte_parts.append(xp * head_cols(dte, h1))
        xdte_t = jnp.transpose(jnp.concatenate(xdte_parts, axis=1)).astype(bf16)
        cd_rows = jnp.concatenate([jnp.broadcast_to(cd[:, h:h + 1], (SSD_HEAD_DIM, D_STATE))
                                   for h in range(g * hg, (g + 1) * hg)], axis=0)
        state_ref[g * gw:(g + 1) * gw, :] = sg * cd_rows + jnp.dot(xdte_t, bg, preferred_element_type=f32)
    y = jnp.concatenate(y_parts, axis=1)
    mix_ref[:, 0:D_SSD] = _gated_norm(y, z_ref[...], nw_ref).astype(bf16)

    @pl.when(c == pl.num_programs(1) - 1)
    def _():
        ssm_ref[0] = state_ref[...].reshape(N_SSD_HEADS, SSD_HEAD_DIM, D_STATE)

    u = u_ref[...]
    tail = pool_tail_ref[...]
    pos = (c * BLK + lax.broadcasted_iota(jnp.int32, (BLK, 1), 0) + 1).astype(f32)
    for gi, w in enumerate(POOL_WINDOWS):
        sl = slice(gi * POOL_GROUP_DIM, (gi + 1) * POOL_GROUP_DIM)
        ug = u[:, sl]
        wsum = _dot_sel_left(pcur_ref[gi], ug, 2) + _dot_sel_left(phist_ref[gi], tail[:, sl], 2)
        pooled = wsum / jnp.minimum(pos, jnp.float32(w)) - ug
        po = jnp.dot(pooled.astype(bf16), pw_ref[gi], preferred_element_type=f32) * ps_ref[:, sl]
        mix_ref[:, D_SSD + gi * POOL_GROUP_DIM:D_SSD + (gi + 1) * POOL_GROUP_DIM] = po.astype(bf16)
    pool_tail_ref[...] = u_ref[BLK - HIST_ROWS:BLK, :]


def _prompt_mixer(z, xbc, dt, u, n_seq, seq_len, conv_w, conv_b, dt_bias, A_log, D_skip, ssd_norm_w, pool_w,
                  pool_scale):
    f32, bf16 = jnp.float32, jnp.bfloat16
    n = n_seq * seq_len
    n_blk = seq_len // BLK
    k = _mixer_constants()

    def row_blk(width):
        return pl.BlockSpec((BLK, width), lambda b, c: (b * n_blk + c, 0))

    def const(shape):
        return pl.BlockSpec(shape, lambda b, c: (0,) * len(shape))

    pad_h = (0, LANES - N_SSD_HEADS)
    return pl.pallas_call(
        _prompt_mixer_body,
        grid=(n_seq, n_blk),
        in_specs=[row_blk(D_SSD), row_blk(D_CONV), row_blk(LANES), row_blk(D_POOL),
                  const((CONV_WIDTH, D_CONV)), const((1, D_CONV)), const((1, LANES)), const((1, LANES)),
                  const((1, D_SSD)), const((1, D_SSD)),
                  const((N_POOL_GROUPS, POOL_GROUP_DIM, POOL_GROUP_DIM)), const((1, D_POOL)),
                  const((BLK, BLK)),
                  const((N_POOL_GROUPS, BLK, BLK)), const((N_POOL_GROUPS, BLK, HIST_ROWS))],
        out_specs=[pl.BlockSpec((BLK, D_MIX), lambda b, c: (b * n_blk + c, 0)),
                   pl.BlockSpec((1, N_SSD_HEADS, SSD_HEAD_DIM, D_STATE), lambda b, c: (b, 0, 0, 0))],
        out_shape=[jax.ShapeDtypeStruct((n, D_MIX), bf16),
                   jax.ShapeDtypeStruct((n_seq, N_SSD_HEADS, SSD_HEAD_DIM, D_STATE), f32)],
        scratch_shapes=[pltpu.VMEM((CONV_TAIL_ROWS + BLK, D_CONV), f32),
                        pltpu.VMEM((HIST_ROWS, D_POOL), f32),
                        pltpu.VMEM((D_SSD, D_STATE), f32)],
        compiler_params=pltpu.CompilerParams(dimension_semantics=("parallel", "arbitrary"),
                                             vmem_limit_bytes=VMEM_LIMIT_BYTES),
        name="prompt_mixer",
    )(z, xbc, dt, u, conv_w.astype(f32), conv_b.reshape(1, D_CONV).astype(f32),
      jnp.pad(dt_bias.astype(f32), pad_h).reshape(1, LANES), jnp.pad(A_log.astype(f32), pad_h).reshape(1, LANES),
      jnp.repeat(D_skip.astype(f32), SSD_HEAD_DIM).reshape(1, D_SSD), ssd_norm_w.reshape(1, D_SSD).astype(f32),
      pool_w.astype(bf16), pool_scale.reshape(1, D_POOL).astype(f32),
      k["causal"], k["pcur"], k["phist"])


SEQ_PER_BLK = BLK // DEC_SEQ


def _sample_constants():
    bf16 = jnp.bfloat16
    r = np.arange(BLK)
    sq, st = r // DEC_SEQ, r % DEC_SEQ
    same = sq[:, None] == sq[None, :]
    causal = same & (st[None, :] <= st[:, None])
    nk = CONV_WIDTH - 1
    shift = np.stack([same & (st[None, :] == st[:, None] + k - nk) for k in range(nk)])
    cs = np.arange(SEQ_PER_BLK * nk)
    stsel = np.stack([(cs[None, :] // nk == sq[:, None]) & (cs[None, :] % nk == st[:, None] + k)
                      for k in range(nk)])
    pcur = np.stack([causal & (st[:, None] - st[None, :] < w) for w in POOL_WINDOWS])
    hs = np.arange(SEQ_PER_BLK * POOL_HIST)
    phist = np.stack([(hs[None, :] // POOL_HIST == sq[:, None])
                      & (st[:, None] + POOL_HIST - hs[None, :] % POOL_HIST < w) for w in POOL_WINDOWS])
    as_bf = lambda a: jnp.asarray(a.astype(np.float32), bf16)
    return dict(same=as_bf(same), causal=as_bf(causal), shift=as_bf(shift), stsel=as_bf(stsel),
                pcur=as_bf(pcur), phist=as_bf(phist))


def _sample_mixer_body(z_ref, xbc_ref, dt_ref, u_ref, cst_ref, pst_ref, ssm_in_ref,
                       cw_ref, cb_ref, dtb_ref, alog_ref, dskip_ref, nw_ref, pw_ref, ps_ref,
                       causal_ref, same_ref, expand_ref, expand_t_ref, shift_ref, stsel_ref, pcur_ref, phist_ref,
                       mix_ref, ssm_out_ref,
                       ydiag_ref, ea_ref, yt_ref, cdh_ref, cdl_ref, xdte_t_ref, bm_ref, cm_ref, *, pos0):
    f32, bf16 = jnp.float32, jnp.bfloat16
    s = pl.program_id(1)
    gw = D_SSD // N_SSD_GROUPS

    @pl.when(s == 0)
    def _():
        xbc = xbc_ref[...]
        cst = cst_ref[...]
        acc = cb_ref[...] + cw_ref[CONV_WIDTH - 1:CONV_WIDTH, :] * xbc
        for k in range(CONV_WIDTH - 1):
            tap = _dot_sel_left(shift_ref[k], xbc, 3) + _dot_sel_left(stsel_ref[k], cst, 3)
            acc = acc + cw_ref[k:k + 1, :] * tap
        xbc_c = acc * jax.nn.sigmoid(acc)

        causal_bf = causal_ref[...]
        keep = causal_bf > 0
        xs, bm, cm, dt, a_cum, xdt = _ssd_intra(xbc_c, dt_ref[...], dtb_ref, alog_ref, causal_bf, expand_ref)
        a_tot = _dot_sel_left(same_ref[...], dt * (-jnp.exp(alog_ref[...])), 3)
        a_cum_t = jnp.transpose(a_cum)
        ea_ref[...] = _dot_sel_right(jnp.exp(a_cum), expand_ref[...], 2)
        dte_x = _dot_sel_right(jnp.exp(a_tot - a_cum), expand_ref[...], 2)
        cd_col = _dot_sel_left(expand_t_ref[...], jnp.exp(jnp.transpose(a_tot)), 2)
        cd_hi, cd_lo = _split2(cd_col)
        cdh_ref[...] = cd_hi
        cdl_ref[...] = cd_lo
        bm_ref[...] = bm.astype(bf16)
        cm_ref[...] = cm.astype(bf16)
        for g in range(N_SSD_GROUPS):
            cg = cm[:, g * D_STATE:(g + 1) * D_STATE].astype(bf16)
            bg = bm[:, g * D_STATE:(g + 1) * D_STATE].astype(bf16)
            cb = lax.dot_general(cg, bg, NT_DIMS, preferred_element_type=f32)
            y_diag = _ssd_diag_group(g, cb, a_cum, a_cum_t, keep, xdt)
            ydiag_ref[:, g * gw:(g + 1) * gw] = y_diag + xs[:, g * gw:(g + 1) * gw] * dskip_ref[:, g * gw:(g + 1) * gw]
            xdte_t_ref[g * gw:(g + 1) * gw, :] = jnp.transpose(
                xdt[:, g * gw:(g + 1) * gw] * dte_x[:, g * gw:(g + 1) * gw]).astype(bf16)
        yt_ref[...] = jnp.zeros(yt_ref.shape, f32)

        u = u_ref[...]
        pst = pst_ref[...]
        step = lax.broadcasted_iota(jnp.int32, (BLK, 1), 0) % DEC_SEQ
        pos = (step + (pos0 + 1)).astype(f32)
        for gi, w in enumerate(POOL_WINDOWS):
            sl = slice(gi * POOL_GROUP_DIM, (gi + 1) * POOL_GROUP_DIM)
            ug = u[:, sl]
            wsum = _dot_sel_left(pcur_ref[gi], ug, 2) + _dot_sel_left(phist_ref[gi], pst[:, sl], 2)
            pooled = wsum / jnp.minimum(pos, jnp.float32(w)) - ug
            po = jnp.dot(pooled.astype(bf16), pw_ref[gi], preferred_element_type=f32) * ps_ref[:, sl]
            mix_ref[:, D_SSD + gi * POOL_GROUP_DIM:D_SSD + (gi + 1) * POOL_GROUP_DIM] = po.astype(bf16)

    rows_of_s = lax.broadcasted_iota(jnp.int32, (BLK, LANES), 0) // DEC_SEQ == s
    cols_of_s = lax.broadcasted_iota(jnp.int32, (gw, BLK), 1) // DEC_SEQ == s
    pick_s = jnp.where(lax.broadcasted_iota(jnp.int32, (BLK, LANES), 0) == DEC_SEQ * s, 1.0, 0.0).astype(bf16)
    state = ssm_in_ref[0].reshape(D_SSD, D_STATE)
    for g in range(N_SSD_GROUPS):
        rs = slice(g * gw, (g + 1) * gw)
        sg = state[rs, :]
        cg = cm_ref[:, g * D_STATE:(g + 1) * D_STATE]
        bg = bm_ref[:, g * D_STATE:(g + 1) * D_STATE]
        yt = lax.dot_general(sg.astype(bf16), cg, NT_DIMS, preferred_element_type=f32)
        yt_ref[rs, :] += jnp.where(cols_of_s, yt, 0.0)
        cd = (jnp.dot(cdh_ref[rs, :], pick_s, preferred_element_type=f32)
              + jnp.dot(cdl_ref[rs, :], pick_s, preferred_element_type=f32))
        upd = jnp.dot(xdte_t_ref[rs, :], jnp.where(rows_of_s, bg, jnp.zeros_like(bg)),
                      preferred_element_type=f32)
        ssm_out_ref[0, g * (N_SSD_HEADS // N_SSD_GROUPS):(g + 1) * (N_SSD_HEADS // N_SSD_GROUPS)] = (
            sg * cd + upd).reshape(N_SSD_HEADS // N_SSD_GROUPS, SSD_HEAD_DIM, D_STATE)

    @pl.when(s == pl.num_programs(1) - 1)
    def _():
        y = ydiag_ref[...] + jnp.transpose(yt_ref[...]) * ea_ref[...]
        mix_ref[:, 0:D_SSD] = _gated_norm(y, z_ref[...], nw_ref).astype(bf16)


def _sample_mixer(z, xbc, dt, u, row0, n_seq, state_conv, state_ssm, state_pool, pos0,
                  conv_w, conv_b, dt_bias, A_log, D_skip, ssd_norm_w, pool_w, pool_scale):
    f32, bf16 = jnp.float32, jnp.bfloat16
    n_blk = n_seq // SEQ_PER_BLK
    blk0 = row0 // BLK
    nk = CONV_WIDTH - 1
    k = _mixer_constants()
    ks = _sample_constants()

    def row_blk(width):
        return pl.BlockSpec((BLK, width), lambda j, s: (blk0 + j, 0))

    def const(shape):
        return pl.BlockSpec(shape, lambda j, s: (0,) * len(shape))

    state_spec = pl.BlockSpec((1, N_SSD_HEADS, SSD_HEAD_DIM, D_STATE), lambda j, s: (j * SEQ_PER_BLK + s, 0, 0, 0))
    pad_h = (0, LANES - N_SSD_HEADS)
    return pl.pallas_call(
        functools.partial(_sample_mixer_body, pos0=pos0),
        grid=(n_blk, SEQ_PER_BLK),
        in_specs=[row_blk(D_SSD), row_blk(D_CONV), row_blk(LANES), row_blk(D_POOL),
                  pl.BlockSpec((SEQ_PER_BLK * nk, D_CONV), lambda j, s: (j, 0)),
                  pl.BlockSpec((SEQ_PER_BLK * POOL_HIST, D_POOL), lambda j, s: (j, 0)),
                  state_spec,
                  const((CONV_WIDTH, D_CONV)), const((1, D_CONV)), const((1, LANES)), const((1, LANES)),
                  const((1, D_SSD)), const((1, D_SSD)),
                  const((N_POOL_GROUPS, POOL_GROUP_DIM, POOL_GROUP_DIM)), const((1, D_POOL)),
                  const((BLK, BLK)), const((BLK, BLK)), const((LANES, D_SSD)), const((D_SSD, LANES)),
                  const((nk, BLK, BLK)), const((nk, BLK, SEQ_PER_BLK * nk)),
                  const((N_POOL_GROUPS, BLK, BLK)), const((N_POOL_GROUPS, BLK, SEQ_PER_BLK * POOL_HIST))],
        out_specs=[pl.BlockSpec((BLK, D_MIX), lambda j, s: (j, 0)), state_spec],
        out_shape=[jax.ShapeDtypeStruct((n_seq * DEC_SEQ, D_MIX), bf16),
                   jax.ShapeDtypeStruct((n_seq, N_SSD_HEADS, SSD_HEAD_DIM, D_STATE), f32)],
        scratch_shapes=[pltpu.VMEM((BLK, D_SSD), f32), pltpu.VMEM((BLK, D_SSD), f32),
                        pltpu.VMEM((D_SSD, BLK), f32), pltpu.VMEM((D_SSD, BLK), bf16),
                        pltpu.VMEM((D_SSD, BLK), bf16), pltpu.VMEM((D_SSD, BLK), bf16),
                        pltpu.VMEM((BLK, N_SSD_GROUPS * D_STATE), bf16),
                        pltpu.VMEM((BLK, N_SSD_GROUPS * D_STATE), bf16)],
        compiler_params=pltpu.CompilerParams(dimension_semantics=("parallel", "arbitrary"),
                                             vmem_limit_bytes=VMEM_LIMIT_BYTES),
        name="sample_mixer",
    )(z, xbc, dt, u, state_conv.reshape(n_seq * nk, D_CONV), state_pool.reshape(n_seq * POOL_HIST, D_POOL),
      state_ssm, conv_w.astype(f32), conv_b.reshape(1, D_CONV).astype(f32),
      jnp.pad(dt_bias.astype(f32), pad_h).reshape(1, LANES), jnp.pad(A_log.astype(f32), pad_h).reshape(1, LANES),
      jnp.repeat(D_skip.astype(f32), SSD_HEAD_DIM).reshape(1, D_SSD), ssd_norm_w.reshape(1, D_SSD).astype(f32),
      pool_w.astype(bf16), pool_scale.reshape(1, D_POOL).astype(f32),
      ks["causal"], ks["same"], k["expand"], k["expand_t"], ks["shift"], ks["stsel"],
      ks["pcur"], ks["phist"])


def _moe_sizes(n_tokens, tm):
    nt = n_tokens // tm
    lmax = -(-(TOP_K * tm + N_EXPERTS * (MOE_SEG_ROWS - 1)) // MOE_SEL_ROWS) * MOE_SEL_ROWS
    rows = (TOP_K * n_tokens + nt * N_EXPERTS * (MOE_SEG_ROWS - 1) + N_EXPERTS * (MOE_PIECE - 1)
            + MOE_ROW_CHUNK)
    n_rows = -(-rows // MOE_PIECE) * MOE_PIECE
    return nt, lmax, n_rows


def _router_body(x_ref, nw_ref, rwh_ref, rwl_ref, rb_ref, h_ref, posg_ref, post_ref, cnt_ref):
    f32, bf16 = jnp.float32, jnp.bfloat16
    tm = x_ref.shape[0]
    x = x_ref[...]
    h = x * lax.rsqrt(jnp.mean(x * x, axis=-1, keepdims=True) + EPS) * nw_ref[...]
    h_hi = h.astype(bf16)
    h_ref[...] = h_hi
    h_lo = (h - h_hi.astype(f32)).astype(bf16)
    wh = rwh_ref[...]
    logits = (jnp.dot(h_hi, wh, preferred_element_type=f32)
              + jnp.dot(h_lo, wh, preferred_element_type=f32)
              + jnp.dot(h_hi, rwl_ref[...], preferred_element_type=f32)) + rb_ref[...]
    lane = lax.broadcasted_iota(jnp.int32, (tm, LANES), 1)
    lanef = lane.astype(f32)
    neg = jnp.float32(-jnp.inf)
    l = jnp.where(lane < N_EXPERTS, logits, neg)
    sels, vals = [], []
    for _ in range(TOP_K):
        m = jnp.max(l, axis=1, keepdims=True)
        idx = jnp.min(jnp.where(l == m, lanef, jnp.float32(LANES)), axis=1, keepdims=True)
        sel = lanef == idx
        l = jnp.where(sel, neg, l)
        sels.append(sel)
        vals.append(m)
    exps = [jnp.exp(v - vals[0]) for v in vals]
    denom = exps[0] + exps[1] + exps[2] + exps[3]
    gates = [e / denom for e in exps]
    chosen = jnp.where(sels[0] | sels[1] | sels[2] | sels[3], 1.0, 0.0).astype(f32)
    row = lax.broadcasted_iota(jnp.int32, (tm, tm), 0)
    col = lax.broadcasted_iota(jnp.int32, (tm, tm), 1)
    lower = jnp.where(col < row, 1.0, 0.0).astype(bf16)
    rank = jnp.dot(lower, chosen.astype(bf16), preferred_element_type=f32)
    cnt = jnp.sum(chosen, axis=0, keepdims=True)
    seg_units = jnp.floor((cnt + (MOE_SEG_ROWS - 1)) * (1.0 / MOE_SEG_ROWS))
    r2 = lax.broadcasted_iota(jnp.int32, (LANES, LANES), 0)
    c2 = lax.broadcasted_iota(jnp.int32, (LANES, LANES), 1)
    upper = jnp.where(r2 < c2, 1.0, 0.0).astype(bf16)
    lstart = jnp.dot(jnp.broadcast_to(seg_units, (8, LANES)).astype(bf16), upper,
                     preferred_element_type=f32)[0:1, :] * MOE_SEG_ROWS
    posmat = lstart + rank
    posg = jnp.zeros((tm, LANES), f32)
    for k in range(TOP_K):
        pos_k = jnp.sum(jnp.where(sels[k], posmat, 0.0), axis=1, keepdims=True)
        posg = posg + jnp.where(lane == k, pos_k, 0.0) + jnp.where(lane == TOP_K + k, gates[k], 0.0)
    posg_ref[...] = posg
    post_ref[...] = jnp.transpose(posg)[0:8, :]
    cnt_ref[0] = jnp.broadcast_to(cnt, (8, LANES)).astype(jnp.int32)


def _moe_router(x1, norm2_w, router_w, router_b, tm):
    n, d = x1.shape
    nt = n // tm
    f32, bf16 = jnp.float32, jnp.bfloat16
    rw = jnp.pad(router_w.astype(f32), ((0, 0), (0, LANES - N_EXPERTS)))
    rw_hi = rw.astype(bf16)
    rw_lo = (rw - rw_hi.astype(f32)).astype(bf16)
    rb = jnp.pad(router_b.astype(f32), (0, LANES - N_EXPERTS)).reshape(1, LANES)
    return pl.pallas_call(
        _router_body,
        grid=(nt,),
        in_specs=[pl.BlockSpec((tm, d), lambda i: (i, 0)),
                  pl.BlockSpec((1, d), lambda i: (0, 0)),
                  pl.BlockSpec((d, LANES), lambda i: (0, 0)),
                  pl.BlockSpec((d, LANES), lambda i: (0, 0)),
                  pl.BlockSpec((1, LANES), lambda i: (0, 0))],
        out_specs=[pl.BlockSpec((tm, d), lambda i: (i, 0)),
                   pl.BlockSpec((tm, LANES), lambda i: (i, 0)),
                   pl.BlockSpec((8, tm), lambda i: (0, i)),
                   pl.BlockSpec((1, 8, LANES), lambda i: (i, 0, 0))],
        out_shape=[jax.ShapeDtypeStruct((n, d), bf16),
                   jax.ShapeDtypeStruct((n, LANES), f32),
                   jax.ShapeDtypeStruct((8, n), f32),
                   jax.ShapeDtypeStruct((nt, 8, LANES), jnp.int32)],
        compiler_params=pltpu.CompilerParams(dimension_semantics=("parallel",),
                                             vmem_limit_bytes=VMEM_LIMIT_BYTES),
        name="moe_router",
    )(x1, norm2_w.reshape(1, d).astype(f32), rw_hi, rw_lo, rb)


def _moe_plan(cnt):
    i32 = jnp.int32
    pad = (cnt + (MOE_SEG_ROWS - 1)) // MOE_SEG_ROWS * MOE_SEG_ROWS
    lstart = jnp.cumsum(pad, axis=1) - pad
    lp = jnp.sum(pad, axis=1)
    tot = jnp.sum(pad, axis=0)
    reg = (tot + (MOE_PIECE - 1)) // MOE_PIECE * MOE_PIECE
    reg_end = jnp.cumsum(reg)
    estart = reg_end - reg
    seg = estart[None, :] + jnp.cumsum(pad, axis=0) - pad
    return dict(
        lstart=lstart.reshape(-1).astype(i32), seg_units=(pad // MOE_SEG_ROWS).reshape(-1).astype(i32),
        seg=seg.reshape(-1).astype(i32), lp=lp.astype(i32),
        tail_start=(estart + tot).astype(i32), tail_units=((reg - tot) // MOE_SEG_ROWS).astype(i32),
        estart=estart.astype(i32), erows=reg.astype(i32), used=reg_end[-1].reshape(1).astype(i32))


def _for_each_segment_copy(i, lstart_ref, units_ref, seg_ref, local_ref, global_ref, sem, to_global, fn):
    def per_expert(e, carry):
        k = i * N_EXPERTS + e

        @pl.when(units_ref[k] > 0)
        def _():
            n = pl.multiple_of(units_ref[k] * MOE_SEG_ROWS, MOE_SEG_ROWS)
            loc = local_ref.at[pl.ds(pl.multiple_of(lstart_ref[k], MOE_SEG_ROWS), n)]
            glo = global_ref.at[pl.ds(pl.multiple_of(seg_ref[k], MOE_SEG_ROWS), n)]
            fn(pltpu.make_async_copy(loc, glo, sem) if to_global else pltpu.make_async_copy(glo, loc, sem))
        return carry
    lax.fori_loop(0, N_EXPERTS, per_expert, 0)


def _for_each_unused_piece(used_ref, zero_ref, rows_ref, sem, fn):
    def per_piece(j, c):
        go = pl.multiple_of(j * MOE_PIECE, MOE_PIECE)
        fn(pltpu.make_async_copy(zero_ref, rows_ref.at[pl.ds(go, MOE_PIECE)], sem))
        return c
    lax.fori_loop(used_ref[0] // MOE_PIECE, rows_ref.shape[0] // MOE_PIECE, per_piece, 0)


def _dispatch_body(lstart_ref, units_ref, seg_ref, lp_ref, tail_start_ref, tail_units_ref, used_ref,
                   h_ref, post_ref, xs_ref, stage_ref, sel_ref, zero_ref, sems, fill_sem):
    f32, bf16 = jnp.float32, jnp.bfloat16
    i = pl.program_id(0)
    nt = pl.num_programs(0)
    slot = i % 2
    tm = h_ref.shape[0]
    lmax = stage_ref.shape[1]

    def for_each_fill_copy(fn):
        def per_expert(e, carry):
            @pl.when(tail_units_ref[e] > 0)
            def _():
                n = pl.multiple_of(tail_units_ref[e] * MOE_SEG_ROWS, MOE_SEG_ROWS)
                go = pl.multiple_of(tail_start_ref[e], MOE_SEG_ROWS)
                fn(pltpu.make_async_copy(zero_ref.at[pl.ds(0, n)], xs_ref.at[pl.ds(go, n)], fill_sem))
            return carry
        lax.fori_loop(0, N_EXPERTS, per_expert, 0)
        _for_each_unused_piece(used_ref, zero_ref, xs_ref, fill_sem, fn)

    @pl.when(i == 0)
    def _():
        zero_ref[...] = jnp.zeros(zero_ref.shape, bf16)
        for_each_fill_copy(lambda cp: cp.start())
        for_each_fill_copy(lambda cp: cp.wait())

    def segment_copies(tile, slot_, fn):
        _for_each_segment_copy(tile, lstart_ref, units_ref, seg_ref, stage_ref.at[slot_], xs_ref,
                               sems.at[slot_], True, fn)

    @pl.when(i >= 2)
    def _():
        segment_copies(i - 2, slot, lambda cp: cp.wait())

    pos = post_ref[0:TOP_K, :]
    for part in range(lmax // MOE_SEL_ROWS):
        @pl.when(part * MOE_SEL_ROWS < lp_ref[i])
        def _():
            for c in range(MOE_SEL_ROWS // MOE_CHUNK):
                r0 = part * MOE_SEL_ROWS + c * MOE_CHUNK
                r = (lax.broadcasted_iota(jnp.int32, (MOE_CHUNK, tm), 0) + r0).astype(f32)
                hit = (pos[0:1, :] == r) | (pos[1:2, :] == r) | (pos[2:3, :] == r) | (pos[3:4, :] == r)
                sel_ref[c * MOE_CHUNK:(c + 1) * MOE_CHUNK, :] = jnp.where(hit, 1.0, 0.0).astype(bf16)
            stage_ref[slot, part * MOE_SEL_ROWS:(part + 1) * MOE_SEL_ROWS, :] = jnp.dot(
                sel_ref[...], h_ref[...], preferred_element_type=f32).astype(bf16)

    segment_copies(i, slot, lambda cp: cp.start())

    @pl.when(i == nt - 1)
    def _():
        @pl.when(nt >= 2)
        def _():
            segment_copies(i - 1, 1 - slot, lambda cp: cp.wait())
        segment_copies(i, slot, lambda cp: cp.wait())


def _moe_dispatch(h2, post, plan, tm, lmax, n_rows):
    n, d = h2.shape
    nt = n // tm
    grid_spec = pltpu.PrefetchScalarGridSpec(
        num_scalar_prefetch=7,
        grid=(nt,),
        in_specs=[pl.BlockSpec((tm, d), lambda i, *_: (i, 0)),
                  pl.BlockSpec((8, tm), lambda i, *_: (0, i))],
        out_specs=pl.BlockSpec(memory_space=pl.ANY),
        scratch_shapes=[pltpu.VMEM((2, lmax, d), jnp.bfloat16),
                        pltpu.VMEM((MOE_SEL_ROWS, tm), jnp.bfloat16),
                        pltpu.VMEM((MOE_PIECE, d), jnp.bfloat16),
                        pltpu.SemaphoreType.DMA((2,)),
                        pltpu.SemaphoreType.DMA(())])
    return pl.pallas_call(
        _dispatch_body,
        grid_spec=grid_spec,
        out_shape=jax.ShapeDtypeStruct((n_rows, d), jnp.bfloat16),
        compiler_params=pltpu.CompilerParams(dimension_semantics=("arbitrary",),
                                             vmem_limit_bytes=VMEM_LIMIT_BYTES),
        name="moe_dispatch",
    )(plan["lstart"], plan["seg_units"], plan["seg"], plan["lp"], plan["tail_start"], plan["tail_units"],
      plan["used"], h2, post)


def _experts_body(first_ref, count_ref, cstart_ref, cvalid_ref, total_ref, used_ref,
                  xs_ref, wgu_ref, bgu_ref, wd_ref, bd_ref, os_ref,
                  wgu_bf, wd_bf, xbuf, obuf, zero_ref, in_sems, out_sems, fill_sem):
    f32, bf16 = jnp.float32, jnp.bfloat16
    e = pl.program_id(0)
    total = total_ref[0]
    half = D_FF // 2

    def in_copy(j):
        src = xs_ref.at[pl.ds(pl.multiple_of(cstart_ref[j], MOE_PIECE), MOE_ROW_CHUNK)]
        return pltpu.make_async_copy(src, xbuf.at[j % 3], in_sems.at[j % 3])

    def out_copy(j):
        n = pl.multiple_of(cvalid_ref[j], MOE_PIECE)
        go = pl.multiple_of(cstart_ref[j], MOE_PIECE)
        return pltpu.make_async_copy(obuf.at[j % 2, pl.ds(0, n)], os_ref.at[pl.ds(go, n)], out_sems.at[j % 2])

    @pl.when(e == 0)
    def _():
        for j0 in range(2):
            @pl.when(j0 < total)
            def _():
                in_copy(j0).start()
        zero_ref[...] = jnp.zeros(zero_ref.shape, bf16)
        _for_each_unused_piece(used_ref, zero_ref, os_ref, fill_sem, lambda cp: cp.start())
        _for_each_unused_piece(used_ref, zero_ref, os_ref, fill_sem, lambda cp: cp.wait())

    @pl.when(count_ref[e] > 0)
    def _():
        wgu_bf[...] = wgu_ref[0].astype(bf16)
        wd_bf[...] = wd_ref[0].astype(bf16)

        def chunk(j, carry):
            in_copy(j).wait()

            @pl.when(j + 2 < total)
            def _():
                in_copy(j + 2).start()

            @pl.when(j >= 2)
            def _():
                out_copy(j - 2).wait()

            x = xbuf[j % 3]
            out = bd_ref[0]
            for hf in range(2):
                gate = jnp.dot(x, wgu_bf[:, hf * half:(hf + 1) * half], preferred_element_type=f32)
                gate = jnp.minimum(gate + bgu_ref[0, :, hf * half:(hf + 1) * half], SWIGLU_LIMIT)
                up = jnp.dot(x, wgu_bf[:, D_FF + hf * half:D_FF + (hf + 1) * half], preferred_element_type=f32)
                up = jnp.clip(up + bgu_ref[0, :, D_FF + hf * half:D_FF + (hf + 1) * half],
                              -SWIGLU_LIMIT, SWIGLU_LIMIT)
                act = (up + 1.0) * (gate * jax.nn.sigmoid(SWIGLU_ALPHA * gate))
                out = out + jnp.dot(act.astype(bf16), wd_bf[hf * half:(hf + 1) * half, :],
                                    preferred_element_type=f32)
            obuf[j % 2] = out.astype(bf16)
            out_copy(j).start()
            return carry
        lax.fori_loop(first_ref[e], first_ref[e] + count_ref[e], chunk, 0)

    @pl.when(e == pl.num_programs(0) - 1)
    def _():
        @pl.when(total >= 2)
        def _():
            out_copy(total - 2).wait()

        @pl.when(total >= 1)
        def _():
            out_copy(total - 1).wait()


def _expert_chunks(plan, n_rows):
    i32 = jnp.int32
    max_chunks = n_rows // MOE_ROW_CHUNK + N_EXPERTS
    count = (plan["erows"] + (MOE_ROW_CHUNK - 1)) // MOE_ROW_CHUNK
    end = jnp.cumsum(count)
    first = end - count
    j = jnp.arange(max_chunks, dtype=i32)
    ej = jnp.minimum(jnp.sum((end[None, :] <= j[:, None]).astype(i32), axis=1), N_EXPERTS - 1)
    c = j - first[ej]
    live = j < end[-1]
    cstart = jnp.where(live, plan["estart"][ej] + c * MOE_ROW_CHUNK, 0)
    cvalid = jnp.where(live, jnp.minimum(MOE_ROW_CHUNK, plan["erows"][ej] - c * MOE_ROW_CHUNK), 0)
    return (first.astype(i32), count.astype(i32), cstart.astype(i32), cvalid.astype(i32),
            end[-1].reshape(1).astype(i32))


def _moe_experts(xs, plan, w_gate_up, b_gate_up, w_down, b_down):
    d = xs.shape[1]
    grid_spec = pltpu.PrefetchScalarGridSpec(
        num_scalar_prefetch=6,
        grid=(N_EXPERTS,),
        in_specs=[pl.BlockSpec(memory_space=pl.ANY),
                  pl.BlockSpec((1, d, 2 * D_FF), lambda e, *_: (e, 0, 0)),
                  pl.BlockSpec((1, 1, 2 * D_FF), lambda e, *_: (e, 0, 0)),
                  pl.BlockSpec((1, D_FF, d), lambda e, *_: (e, 0, 0)),
                  pl.BlockSpec((1, 1, d), lambda e, *_: (e, 0, 0))],
        out_specs=pl.BlockSpec(memory_space=pl.ANY),
        scratch_shapes=[pltpu.VMEM((d, 2 * D_FF), jnp.bfloat16),
                        pltpu.VMEM((D_FF, d), jnp.bfloat16),
                        pltpu.VMEM((3, MOE_ROW_CHUNK, d), jnp.bfloat16),
                        pltpu.VMEM((2, MOE_ROW_CHUNK, d), jnp.bfloat16),
                        pltpu.VMEM((MOE_PIECE, d), jnp.bfloat16),
                        pltpu.SemaphoreType.DMA((3,)),
                        pltpu.SemaphoreType.DMA((2,)),
                        pltpu.SemaphoreType.DMA(())])
    return pl.pallas_call(
        _experts_body,
        grid_spec=grid_spec,
        out_shape=jax.ShapeDtypeStruct(xs.shape, jnp.bfloat16),
        compiler_params=pltpu.CompilerParams(dimension_semantics=("arbitrary",),
                                             vmem_limit_bytes=VMEM_LIMIT_BYTES),
        name="moe_experts",
    )(*_expert_chunks(plan, xs.shape[0]), plan["used"],
      xs, w_gate_up, b_gate_up.reshape(N_EXPERTS, 1, 2 * D_FF), w_down, b_down.reshape(N_EXPERTS, 1, d))


def _combine_body(lstart_ref, units_ref, seg_ref, lp_ref,
                  os_ref, posg_ref, x_ref, fw_ref, yp_ref, ys_ref, stage_ref, w_ref, sems,
                  *, n_prompt_tiles):
    f32, bf16 = jnp.float32, jnp.bfloat16
    i = pl.program_id(0)
    nt = pl.num_programs(0)
    slot = i % 2
    tm = x_ref.shape[0]
    lmax = stage_ref.shape[1]

    def segment_copies(tile, slot_, fn):
        _for_each_segment_copy(tile, lstart_ref, units_ref, seg_ref, stage_ref.at[slot_], os_ref,
                               sems.at[slot_], False, fn)

    @pl.when(i == 0)
    def _():
        stage_ref[...] = jnp.zeros(stage_ref.shape, bf16)
        segment_copies(0, 0, lambda cp: cp.start())

    @pl.when(i + 1 < nt)
    def _():
        segment_copies(i + 1, 1 - slot, lambda cp: cp.start())

    posg = posg_ref[...]
    pos = [posg[:, k:k + 1] for k in range(TOP_K)]
    gate = [posg[:, TOP_K + k:TOP_K + k + 1] for k in range(TOP_K)]
    for c in range(lmax // MOE_CHUNK):
        r = (lax.broadcasted_iota(jnp.int32, (tm, MOE_CHUNK), 1) + c * MOE_CHUNK).astype(f32)
        w = jnp.zeros((tm, MOE_CHUNK), f32)
        for k in range(TOP_K):
            w = jnp.where(pos[k] == r, gate[k], w)
        w_ref[:, c * MOE_CHUNK:(c + 1) * MOE_CHUNK] = w.astype(bf16)

    segment_copies(i, slot, lambda cp: cp.wait())
    y = x_ref[...] + jnp.dot(w_ref[...], stage_ref[slot], preferred_element_type=f32)
    out = y * lax.rsqrt(jnp.mean(y * y, axis=-1, keepdims=True) + EPS) * fw_ref[...]

    @pl.when(i < n_prompt_tiles)
    def _():
        yp_ref[...] = out

    @pl.when(i >= n_prompt_tiles)
    def _():
        ys_ref[...] = out


def _moe_combine(os_, posg, x1, final_norm_w, plan, tm, lmax, n_prompt):
    n, d = x1.shape
    nt = n // tm
    n_prompt_tiles = n_prompt // tm
    n_sample_tiles = nt - n_prompt_tiles
    grid_spec = pltpu.PrefetchScalarGridSpec(
        num_scalar_prefetch=4,
        grid=(nt,),
        in_specs=[pl.BlockSpec(memory_space=pl.ANY),
                  pl.BlockSpec((tm, LANES), lambda i, *_: (i, 0)),
                  pl.BlockSpec((tm, d), lambda i, *_: (i, 0)),
                  pl.BlockSpec((1, d), lambda i, *_: (0, 0))],
        out_specs=[pl.BlockSpec((tm, d), lambda i, *_: (jnp.minimum(i, n_prompt_tiles - 1), 0)),
                   pl.BlockSpec((tm, d), lambda i, *_: (jnp.maximum(i - n_prompt_tiles, 0), 0))],
        scratch_shapes=[pltpu.VMEM((2, lmax, d), jnp.bfloat16),
                        pltpu.VMEM((tm, lmax), jnp.bfloat16),
                        pltpu.SemaphoreType.DMA((2,))])
    return pl.pallas_call(
        functools.partial(_combine_body, n_prompt_tiles=n_prompt_tiles),
        grid_spec=grid_spec,
        out_shape=[jax.ShapeDtypeStruct((n_prompt, d), jnp.float32),
                   jax.ShapeDtypeStruct((n_sample_tiles * tm, d), jnp.float32)],
        compiler_params=pltpu.CompilerParams(dimension_semantics=("arbitrary",),
                                             vmem_limit_bytes=VMEM_LIMIT_BYTES),
        name="moe_combine",
    )(plan["lstart"], plan["seg_units"], plan["seg"], plan["lp"],
      os_, posg, x1, final_norm_w.reshape(1, d).astype(jnp.float32))


def _moe_block(x1, n_prompt, norm2_w, router_w, router_b, w_gate_up, b_gate_up, w_down, b_down,
               final_norm_w, tm=MOE_TOKEN_TILE):
    n = x1.shape[0]
    nt, lmax, n_rows = _moe_sizes(n, tm)
    h2, posg, post, cnt3 = _moe_router(x1, norm2_w, router_w, router_b, tm)
    plan = _moe_plan(cnt3[:, 0, :N_EXPERTS])
    xs = _moe_dispatch(h2, post, plan, tm, lmax, n_rows)
    os_ = _moe_experts(xs, plan, w_gate_up, b_gate_up, w_down, b_down)
    return _moe_combine(os_, posg, x1, final_norm_w, plan, tm, lmax, n_prompt)


def kernel(x_prompt, x_sample, state_ssm, state_conv, state_pool, norm1_w, w_in, conv_w, conv_b, dt_bias,
           A_log, D_skip, ssd_norm_w, pool_w, pool_scale, w_out, norm2_w, router_w, router_b, w_gate_up,
           b_gate_up, w_down, b_down, final_norm_w):
    n_prompt = BATCH * SEQ
    n_sample = DEC_BATCH * DEC_SEQ
    xp = x_prompt.reshape(n_prompt, D_MODEL)
    xs = x_sample.reshape(n_sample, D_MODEL)
    z, xbc, dt_raw, u = _in_proj(xp, xs, norm1_w[0], w_in[0])
    mp = (conv_w[0], conv_b[0], dt_bias[0], A_log[0], D_skip[0], ssd_norm_w[0], pool_w[0], pool_scale[0])
    mix_p, s1 = _prompt_mixer(z, xbc, dt_raw, u, BATCH, SEQ, *mp)
    mix_s, s2 = _sample_mixer(z, xbc, dt_raw, u, n_prompt, DEC_BATCH, state_conv[0], state_ssm[0], state_pool[0],
                              PAST_LEN, *mp)
    nk = CONV_WIDTH - 1
    c1 = jnp.stack([xbc[(b + 1) * SEQ - nk:(b + 1) * SEQ] for b in range(BATCH)])
    p1 = jnp.stack([u[(b + 1) * SEQ - POOL_HIST:(b + 1) * SEQ] for b in range(BATCH)])
    c2 = xbc[n_prompt:].reshape(DEC_BATCH, DEC_SEQ, D_CONV)[:, DEC_SEQ - nk:]
    p2 = jnp.concatenate([state_pool[0][:, DEC_SEQ:], u[n_prompt:].reshape(DEC_BATCH, DEC_SEQ, D_POOL)], axis=1)
    x1 = _out_proj(mix_p, mix_s, w_out[0], xp, xs)
    yp, ys = _moe_block(x1, n_prompt, norm2_w[0], router_w[0], router_b[0], w_gate_up[0], b_gate_up[0],
                        w_down[0], b_down[0], final_norm_w)
    return (yp.reshape(x_prompt.shape), ys.reshape(x_sample.shape),
            s1[None], c1[None], p1[None], s2[None], c2[None], p2[None])
```

```python
import functools
import math
import jax, jax.numpy as jnp
from jax import lax
import numpy as np
from jax.experimental import pallas as pl
from jax.experimental.pallas import tpu as pltpu

D_MODEL = 1024
BATCH = 8
SEQ = 2048
DEC_BATCH = 128
DEC_SEQ = 4
PAST_LEN = 16384

D_MIX = 2 * D_MODEL
D_SSD = 3 * D_MIX // 4
SSD_HEAD_DIM = 64
N_SSD_HEADS = D_SSD // SSD_HEAD_DIM
N_SSD_GROUPS = 4
D_STATE = 128
CONV_WIDTH = 4
SSD_CHUNK = 128
D_CONV = D_SSD + 2 * N_SSD_GROUPS * D_STATE
D_POOL = D_MIX - D_SSD
POOL_WINDOWS = (2, 4, 8, 16)
N_POOL_GROUPS = len(POOL_WINDOWS)
POOL_GROUP_DIM = D_POOL // N_POOL_GROUPS
POOL_HIST = max(POOL_WINDOWS) - 1
D_IN_PROJ = D_SSD + D_CONV + N_SSD_HEADS + D_POOL
N_EXPERTS = 32
TOP_K = 4
D_FF = D_MODEL
SWIGLU_LIMIT = 7.0
SWIGLU_ALPHA = 1.702
EPS = 1e-5

LANES = 128
BF16_SUBLANES = 16
VMEM_LIMIT_BYTES = 48 * 1024 * 1024

MOE_TOKEN_TILE = 512
MOE_SEG_ROWS = BF16_SUBLANES
MOE_PIECE = 128
MOE_ROW_CHUNK = 512
MOE_CHUNK = 256
MOE_SEL_ROWS = 1280


BLK = SSD_CHUNK
PROJ_ROW_TILE = 512
HIST_ROWS = 16
CONV_TAIL_ROWS = 8
NT_DIMS = (((1,), (1,)), ((), ()))


def _split2(v):
    hi = v.astype(jnp.bfloat16)
    lo = (v - hi.astype(jnp.float32)).astype(jnp.bfloat16)
    return hi, lo


def _dot_sel_left(sel, v, passes):
    out = None
    rem = v
    for p in range(passes):
        part = rem.astype(jnp.bfloat16)
        d = jnp.dot(sel, part, preferred_element_type=jnp.float32)
        out = d if out is None else out + d
        if p + 1 < passes:
            rem = rem - part.astype(jnp.float32)
    return out


def _dot_sel_right(v, sel, passes):
    out = None
    rem = v
    for p in range(passes):
        part = rem.astype(jnp.bfloat16)
        d = jnp.dot(part, sel, preferred_element_type=jnp.float32)
        out = d if out is None else out + d
        if p + 1 < passes:
            rem = rem - part.astype(jnp.float32)
    return out


def _two_part_specs(n_first, n_second, tm, width):
    t1 = n_first // tm
    t2 = n_second // tm
    return (pl.BlockSpec((tm, width), lambda i: (jnp.minimum(i, t1 - 1), 0)),
            pl.BlockSpec((tm, width), lambda i: (jnp.clip(i - t1, 0, t2 - 1), 0)))


def _in_proj_body(xa_ref, xb_ref, nw_ref, w_ref, z_ref, xbc_ref, dt_ref, u_ref, *, tiles_a):
    x = jnp.where(pl.program_id(0) < tiles_a, xa_ref[...], xb_ref[...])
    h = (x * lax.rsqrt(jnp.mean(x * x, axis=-1, keepdims=True) + EPS) * nw_ref[...]).astype(jnp.bfloat16)
    off = 0
    for ref in (z_ref, xbc_ref, dt_ref, u_ref):
        n = ref.shape[1]
        ref[...] = jnp.dot(h, w_ref[:, off:off + n], preferred_element_type=jnp.float32)
        off += n


def _in_proj(xa, xb, norm1_w, w_in):
    d = xa.shape[1]
    n = xa.shape[0] + xb.shape[0]
    f32, bf16 = jnp.float32, jnp.bfloat16
    s1, s2 = D_SSD + D_CONV, D_SSD + D_CONV + N_SSD_HEADS
    w = jnp.concatenate([w_in[:, :s1], jnp.pad(w_in[:, s1:s2], ((0, 0), (0, LANES - N_SSD_HEADS))),
                         w_in[:, s2:]], axis=1).astype(bf16)
    widths = (D_SSD, D_CONV, LANES, D_POOL)
    tm = PROJ_ROW_TILE
    return pl.pallas_call(
        functools.partial(_in_proj_body, tiles_a=xa.shape[0] // tm),
        grid=(n // tm,),
        in_specs=[*_two_part_specs(xa.shape[0], xb.shape[0], tm, d),
                  pl.BlockSpec((1, d), lambda i: (0, 0)),
                  pl.BlockSpec((d, sum(widths)), lambda i: (0, 0), pipeline_mode=pl.Buffered(1))],
        out_specs=[pl.BlockSpec((tm, wd), lambda i: (i, 0)) for wd in widths],
        out_shape=[jax.ShapeDtypeStruct((n, wd), f32) for wd in widths],
        compiler_params=pltpu.CompilerParams(dimension_semantics=("parallel",),
                                             vmem_limit_bytes=VMEM_LIMIT_BYTES),
        name="in_proj",
    )(xa, xb, norm1_w.reshape(1, d).astype(f32), w)


def _out_proj_body(ma_ref, mb_ref, w_ref, xa_ref, xb_ref, o_ref, *, tiles_a):
    first = pl.program_id(0) < tiles_a
    m = jnp.where(first, ma_ref[...], mb_ref[...])
    x = jnp.where(first, xa_ref[...], xb_ref[...])
    o_ref[...] = x + jnp.dot(m, w_ref[...], preferred_element_type=jnp.float32)


def _out_proj(ma, mb, w_out, xa, xb):
    d = xa.shape[1]
    n = xa.shape[0] + xb.shape[0]
    tm = PROJ_ROW_TILE
    return pl.pallas_call(
        functools.partial(_out_proj_body, tiles_a=xa.shape[0] // tm),
        grid=(n // tm,),
        in_specs=[*_two_part_specs(xa.shape[0], xb.shape[0], tm, D_MIX),
                  pl.BlockSpec((D_MIX, d), lambda i: (0, 0)),
                  *_two_part_specs(xa.shape[0], xb.shape[0], tm, d)],
        out_specs=pl.BlockSpec((tm, d), lambda i: (i, 0)),
        out_shape=jax.ShapeDtypeStruct((n, d), jnp.float32),
        compiler_params=pltpu.CompilerParams(dimension_semantics=("parallel",),
                                             vmem_limit_bytes=VMEM_LIMIT_BYTES),
        name="out_proj",
    )(ma, mb, w_out.astype(jnp.bfloat16), xa, xb)


def _mixer_constants():
    bf16 = jnp.bfloat16
    h = np.arange(LANES)[:, None]
    ch = np.arange(D_SSD)[None, :]
    expand = (ch // SSD_HEAD_DIM == h).astype(np.float32)
    i = np.arange(BLK)[:, None]
    j = np.arange(BLK)[None, :]
    causal = (j <= i).astype(np.float32)
    jh = np.arange(HIST_ROWS)[None, :]
    pcur = np.stack([((j <= i) & (i - j < w)) for w in POOL_WINDOWS]).astype(np.float32)
    phist = np.stack([(i + HIST_ROWS - jh < w) for w in POOL_WINDOWS]).astype(np.float32)
    return dict(expand=jnp.asarray(expand, bf16), expand_t=jnp.asarray(expand.T, bf16),
                causal=jnp.asarray(causal, bf16), pcur=jnp.asarray(pcur, bf16),
                phist=jnp.asarray(phist, bf16))


def _softplus(x):
    return jnp.maximum(x, 0.0) + jnp.log(1.0 + jnp.exp(-jnp.abs(x)))


def _conv_silu(ext_ref, cw_ref, cb_ref, first_row):
    ext = ext_ref[...]
    last = first_row + CONV_WIDTH - 1
    acc = cb_ref[...] + cw_ref[CONV_WIDTH - 1:CONV_WIDTH, :] * ext[last:last + BLK, :]
    for k in range(CONV_WIDTH - 1):
        tap = pltpu.roll(ext, CONV_WIDTH - 1 - k, axis=0)[last:last + BLK, :]
        acc = acc + cw_ref[k:k + 1, :] * tap
    return acc * jax.nn.sigmoid(acc)


def _ssd_intra(xbc_c, dt_raw, dtb_ref, alog_ref, causal_bf, expand_ref):
    f32 = jnp.float32
    xs = xbc_c[:, :D_SSD]
    bm = xbc_c[:, D_SSD:D_SSD + N_SSD_GROUPS * D_STATE]
    cm = xbc_c[:, D_SSD + N_SSD_GROUPS * D_STATE:]
    dt = _softplus(dt_raw + dtb_ref[...])
    a = dt * (-jnp.exp(alog_ref[...]))
    a_cum = _dot_sel_left(causal_bf, a, 3)
    dt_x = _dot_sel_right(dt, expand_ref[...], 2)
    return xs, bm, cm, dt, a_cum, xs * dt_x


def _ssd_diag_group(g, cb, a_cum, a_cum_t, keep, xdt):
    f32, bf16 = jnp.float32, jnp.bfloat16
    hg = N_SSD_HEADS // N_SSD_GROUPS
    lane = lax.broadcasted_iota(jnp.int32, (BLK, LANES), 1)
    first_head = lane < SSD_HEAD_DIM
    neg = jnp.float32(-jnp.inf)
    outs = []
    for pr in range(hg * SSD_HEAD_DIM // LANES):
        h1 = g * hg + 2 * pr
        blk = (g * hg * SSD_HEAD_DIM) // LANES + pr
        xp = xdt[:, blk * LANES:(blk + 1) * LANES]
        x1 = jnp.where(first_head, xp, 0.0).astype(bf16)
        x2 = jnp.where(first_head, 0.0, xp).astype(bf16)
        m1 = (cb * jnp.exp(jnp.where(keep, a_cum[:, h1:h1 + 1] - a_cum_t[h1:h1 + 1, :], neg))).astype(bf16)
        m2 = (cb * jnp.exp(jnp.where(keep, a_cum[:, h1 + 1:h1 + 2] - a_cum_t[h1 + 1:h1 + 2, :], neg))).astype(bf16)
        outs.append(jnp.dot(m1, x1, preferred_element_type=f32) + jnp.dot(m2, x2, preferred_element_type=f32))
    return jnp.concatenate(outs, axis=1)


def _gated_norm(y, z, nw_ref):
    yg = y * (z * jax.nn.sigmoid(z))
    return yg * lax.rsqrt(jnp.mean(yg * yg, axis=-1, keepdims=True) + EPS) * nw_ref[...]


def _prompt_mixer_body(z_ref, xbc_ref, dt_ref, u_ref, cw_ref, cb_ref, dtb_ref, alog_ref, dskip_ref, nw_ref,
                       pw_ref, ps_ref, causal_ref, pcur_ref, phist_ref,
                       mix_ref, ssm_ref, ext_ref, pool_tail_ref, state_ref):
    f32, bf16 = jnp.float32, jnp.bfloat16
    c = pl.program_id(1)
    gw = D_SSD // N_SSD_GROUPS

    @pl.when(c == 0)
    def _():
        ext_ref[0:CONV_TAIL_ROWS, :] = jnp.zeros((CONV_TAIL_ROWS, D_CONV), f32)
        pool_tail_ref[...] = jnp.zeros(pool_tail_ref.shape, f32)
        state_ref[...] = jnp.zeros(state_ref.shape, f32)

    ext_ref[CONV_TAIL_ROWS:CONV_TAIL_ROWS + BLK, :] = xbc_ref[...]
    xbc_c = _conv_silu(ext_ref, cw_ref, cb_ref, CONV_TAIL_ROWS - (CONV_WIDTH - 1))
    ext_ref[0:CONV_TAIL_ROWS, :] = xbc_ref[BLK - CONV_TAIL_ROWS:BLK, :]

    causal_bf = causal_ref[...]
    keep = causal_bf > 0
    xs = xbc_c[:, :D_SSD]
    bm = xbc_c[:, D_SSD:D_SSD + N_SSD_GROUPS * D_STATE]
    cm = xbc_c[:, D_SSD + N_SSD_GROUPS * D_STATE:]
    dt = _softplus(dt_ref[...] + dtb_ref[...])
    a_cum = _dot_sel_left(causal_bf, dt * (-jnp.exp(alog_ref[...])), 3)
    a_cum_t = jnp.transpose(a_cum)
    a_tot = a_cum[BLK - 1:BLK, :]
    ea = jnp.exp(a_cum)
    dte = jnp.exp(a_tot - a_cum)
    cd = jnp.exp(a_tot)
    hg = N_SSD_HEADS // N_SSD_GROUPS
    first_head = lax.broadcasted_iota(jnp.int32, (BLK, LANES), 1) < SSD_HEAD_DIM
    neg = jnp.float32(-jnp.inf)

    def head_cols(v, h1):
        return jnp.where(first_head, v[:, h1:h1 + 1], v[:, h1 + 1:h1 + 2])

    def decay_from(h):
        return jnp.exp(jnp.where(keep, a_cum[:, h:h + 1] - a_cum_t[h:h + 1, :], neg))

    y_parts = []
    for g in range(N_SSD_GROUPS):
        cg = cm[:, g * D_STATE:(g + 1) * D_STATE].astype(bf16)
        bg = bm[:, g * D_STATE:(g + 1) * D_STATE].astype(bf16)
        cb = lax.dot_general(cg, bg, NT_DIMS, preferred_element_type=f32)
        sg = state_ref[g * gw:(g + 1) * gw, :]
        y_off = lax.dot_general(cg, sg.astype(bf16), NT_DIMS, preferred_element_type=f32)
        xdte_parts = []
        for pr in range(gw // LANES):
            h1 = g * hg + 2 * pr
            sl = slice(h1 * SSD_HEAD_DIM, h1 * SSD_HEAD_DIM + LANES)
            xp = xs[:, sl] * head_cols(dt, h1)
            x1 = jnp.where(first_head, xp, 0.0).astype(bf16)
            x2 = jnp.where(first_head, 0.0, xp).astype(bf16)
            y_diag = (jnp.dot((cb * decay_from(h1)).astype(bf16), x1, preferred_element_type=f32)
                      + jnp.dot((cb * decay_from(h1 + 1)).astype(bf16), x2, preferred_element_type=f32))
            y_parts.append(y_diag + y_off[:, pr * LANES:(pr + 1) * LANES] * head_cols(ea, h1)
                           + xs[:, sl] * dskip_ref[:, sl])
            xdte_parts.append(xp * head_cols(dte, h1))
        xdte_t = jnp.transpose(jnp.concatenate(xdte_parts, axis=1)).astype(bf16)
        cd_rows = jnp.concatenate([jnp.broadcast_to(cd[:, h:h + 1], (SSD_HEAD_DIM, D_STATE))
                                   for h in range(g * hg, (g + 1) * hg)], axis=0)
        state_ref[g * gw:(g + 1) * gw, :] = sg * cd_rows + jnp.dot(xdte_t, bg, preferred_element_type=f32)
    y = jnp.concatenate(y_parts, axis=1)
    mix_ref[:, 0:D_SSD] = _gated_norm(y, z_ref[...], nw_ref).astype(bf16)

    @pl.when(c == pl.num_programs(1) - 1)
    def _():
        ssm_ref[0] = state_ref[...].reshape(N_SSD_HEADS, SSD_HEAD_DIM, D_STATE)

    u = u_ref[...]
    tail = pool_tail_ref[...]
    pos = (c * BLK + lax.broadcasted_iota(jnp.int32, (BLK, 1), 0) + 1).astype(f32)
    for gi, w in enumerate(POOL_WINDOWS):
        sl = slice(gi * POOL_GROUP_DIM, (gi + 1) * POOL_GROUP_DIM)
        ug = u[:, sl]
        wsum = _dot_sel_left(pcur_ref[gi], ug, 2) + _dot_sel_left(phist_ref[gi], tail[:, sl], 2)
        pooled = wsum / jnp.minimum(pos, jnp.float32(w)) - ug
        po = jnp.dot(pooled.astype(bf16), pw_ref[gi], preferred_element_type=f32) * ps_ref[:, sl]
        mix_ref[:, D_SSD + gi * POOL_GROUP_DIM:D_SSD + (gi + 1) * POOL_GROUP_DIM] = po.astype(bf16)
    pool_tail_ref[...] = u_ref[BLK - HIST_ROWS:BLK, :]


def _prompt_mixer(z, xbc, dt, u, n_seq, seq_len, conv_w, conv_b, dt_bias, A_log, D_skip, ssd_norm_w, pool_w,
                  pool_scale):
    f32, bf16 = jnp.float32, jnp.bfloat16
    n = n_seq * seq_len
    n_blk = seq_len // BLK
    k = _mixer_constants()

    def row_blk(width):
        return pl.BlockSpec((BLK, width), lambda b, c: (b * n_blk + c, 0))

    def const(shape):
        return pl.BlockSpec(shape, lambda b, c: (0,) * len(shape))

    pad_h = (0, LANES - N_SSD_HEADS)
    return pl.pallas_call(
        _prompt_mixer_body,
        grid=(n_seq, n_blk),
        in_specs=[row_blk(D_SSD), row_blk(D_CONV), row_blk(LANES), row_blk(D_POOL),
                  const((CONV_WIDTH, D_CONV)), const((1, D_CONV)), const((1, LANES)), const((1, LANES)),
                  const((1, D_SSD)), const((1, D_SSD)),
                  const((N_POOL_GROUPS, POOL_GROUP_DIM, POOL_GROUP_DIM)), const((1, D_POOL)),
                  const((BLK, BLK)),
                  const((N_POOL_GROUPS, BLK, BLK)), const((N_POOL_GROUPS, BLK, HIST_ROWS))],
        out_specs=[pl.BlockSpec((BLK, D_MIX), lambda b, c: (b * n_blk + c, 0)),
                   pl.BlockSpec((1, N_SSD_HEADS, SSD_HEAD_DIM, D_STATE), lambda b, c: (b, 0, 0, 0))],
        out_shape=[jax.ShapeDtypeStruct((n, D_MIX), bf16),
                   jax.ShapeDtypeStruct((n_seq, N_SSD_HEADS, SSD_HEAD_DIM, D_STATE), f32)],
        scratch_shapes=[pltpu.VMEM((CONV_TAIL_ROWS + BLK, D_CONV), f32),
                        pltpu.VMEM((HIST_ROWS, D_POOL), f32),
                        pltpu.VMEM((D_SSD, D_STATE), f32)],
        compiler_params=pltpu.CompilerParams(dimension_semantics=("parallel", "arbitrary"),
                                             vmem_limit_bytes=VMEM_LIMIT_BYTES),
        name="prompt_mixer",
    )(z, xbc, dt, u, conv_w.astype(f32), conv_b.reshape(1, D_CONV).astype(f32),
      jnp.pad(dt_bias.astype(f32), pad_h).reshape(1, LANES), jnp.pad(A_log.astype(f32), pad_h).reshape(1, LANES),
      jnp.repeat(D_skip.astype(f32), SSD_HEAD_DIM).reshape(1, D_SSD), ssd_norm_w.reshape(1, D_SSD).astype(f32),
      pool_w.astype(bf16), pool_scale.reshape(1, D_POOL).astype(f32),
      k["causal"], k["pcur"], k["phist"])


SEQ_PER_BLK = BLK // DEC_SEQ


def _sample_constants():
    bf16 = jnp.bfloat16
    r = np.arange(BLK)
    sq, st = r // DEC_SEQ, r % DEC_SEQ
    same = sq[:, None] == sq[None, :]
    causal = same & (st[None, :] <= st[:, None])
    nk = CONV_WIDTH - 1
    shift = np.stack([same & (st[None, :] == st[:, None] + k - nk) for k in range(nk)])
    cs = np.arange(SEQ_PER_BLK * nk)
    stsel = np.stack([(cs[None, :] // nk == sq[:, None]) & (cs[None, :] % nk == st[:, None] + k)
                      for k in range(nk)])
    pcur = np.stack([causal & (st[:, None] - st[None, :] < w) for w in POOL_WINDOWS])
    hs = np.arange(SEQ_PER_BLK * POOL_HIST)
    phist = np.stack([(hs[None, :] // POOL_HIST == sq[:, None])
                      & (st[:, None] + POOL_HIST - hs[None, :] % POOL_HIST < w) for w in POOL_WINDOWS])
    as_bf = lambda a: jnp.asarray(a.astype(np.float32), bf16)
    return dict(same=as_bf(same), causal=as_bf(causal), shift=as_bf(shift), stsel=as_bf(stsel),
                pcur=as_bf(pcur), phist=as_bf(phist))


def _sample_mixer_body(z_ref, xbc_ref, dt_ref, u_ref, cst_ref, pst_ref, ssm_in_ref,
                       cw_ref, cb_ref, dtb_ref, alog_ref, dskip_ref, nw_ref, pw_ref, ps_ref,
                       causal_ref, same_ref, expand_ref, expand_t_ref, shift_ref, stsel_ref, pcur_ref, phist_ref,
                       mix_ref, ssm_out_ref,
                       ydiag_ref, ea_ref, yt_ref, cdh_ref, cdl_ref, xdte_t_ref, bm_ref, cm_ref, *, pos0):
    f32, bf16 = jnp.float32, jnp.bfloat16
    s = pl.program_id(1)
    gw = D_SSD // N_SSD_GROUPS

    @pl.when(s == 0)
    def _():
        xbc = xbc_ref[...]
        cst = cst_ref[...]
        acc = cb_ref[...] + cw_ref[CONV_WIDTH - 1:CONV_WIDTH, :] * xbc
        for k in range(CONV_WIDTH - 1):
            tap = _dot_sel_left(shift_ref[k], xbc, 3) + _dot_sel_left(stsel_ref[k], cst, 3)
            acc = acc + cw_ref[k:k + 1, :] * tap
        xbc_c = acc * jax.nn.sigmoid(acc)

        causal_bf = causal_ref[...]
        keep = causal_bf > 0
        xs, bm, cm, dt, a_cum, xdt = _ssd_intra(xbc_c, dt_ref[...], dtb_ref, alog_ref, causal_bf, expand_ref)
        a_tot = _dot_sel_left(same_ref[...], dt * (-jnp.exp(alog_ref[...])), 3)
        a_cum_t = jnp.transpose(a_cum)
        ea_ref[...] = _dot_sel_right(jnp.exp(a_cum), expand_ref[...], 2)
        dte_x = _dot_sel_right(jnp.exp(a_tot - a_cum), expand_ref[...], 2)
        cd_col = _dot_sel_left(expand_t_ref[...], jnp.exp(jnp.transpose(a_tot)), 2)
        cd_hi, cd_lo = _split2(cd_col)
        cdh_ref[...] = cd_hi
        cdl_ref[...] = cd_lo
        bm_ref[...] = bm.astype(bf16)
        cm_ref[...] = cm.astype(bf16)
        for g in range(N_SSD_GROUPS):
            cg = cm[:, g * D_STATE:(g + 1) * D_STATE].astype(bf16)
            bg = bm[:, g * D_STATE:(g + 1) * D_STATE].astype(bf16)
            cb = lax.dot_general(cg, bg, NT_DIMS, preferred_element_type=f32)
            y_diag = _ssd_diag_group(g, cb, a_cum, a_cum_t, keep, xdt)
            ydiag_ref[:, g * gw:(g + 1) * gw] = y_diag + xs[:, g * gw:(g + 1) * gw] * dskip_ref[:, g * gw:(g + 1) * gw]
            xdte_t_ref[g * gw:(g + 1) * gw, :] = jnp.transpose(
                xdt[:, g * gw:(g + 1) * gw] * dte_x[:, g * gw:(g + 1) * gw]).astype(bf16)
        yt_ref[...] = jnp.zeros(yt_ref.shape, f32)

        u = u_ref[...]
        pst = pst_ref[...]
        step = lax.broadcasted_iota(jnp.int32, (BLK, 1), 0) % DEC_SEQ
        pos = (step + (pos0 + 1)).astype(f32)
        for gi, w in enumerate(POOL_WINDOWS):
            sl = slice(gi * POOL_GROUP_DIM, (gi + 1) * POOL_GROUP_DIM)
            ug = u[:, sl]
            wsum = _dot_sel_left(pcur_ref[gi], ug, 2) + _dot_sel_left(phist_ref[gi], pst[:, sl], 2)
            pooled = wsum / jnp.minimum(pos, jnp.float32(w)) - ug
            po = jnp.dot(pooled.astype(bf16), pw_ref[gi], preferred_element_type=f32) * ps_ref[:, sl]
            mix_ref[:, D_SSD + gi * POOL_GROUP_DIM:D_SSD + (gi + 1) * POOL_GROUP_DIM] = po.astype(bf16)

    rows_of_s = lax.broadcasted_iota(jnp.int32, (BLK, LANES), 0) // DEC_SEQ == s
    cols_of_s = lax.broadcasted_iota(jnp.int32, (gw, BLK), 1) // DEC_SEQ == s
    pick_s = jnp.where(lax.broadcasted_iota(jnp.int32, (BLK, LANES), 0) == DEC_SEQ * s, 1.0, 0.0).astype(bf16)
    state = ssm_in_ref[0].reshape(D_SSD, D_STATE)
    for g in range(N_SSD_GROUPS):
        rs = slice(g * gw, (g + 1) * gw)
        sg = state[rs, :]
        cg = cm_ref[:, g * D_STATE:(g + 1) * D_STATE]
        bg = bm_ref[:, g * D_STATE:(g + 1) * D_STATE]
        yt = lax.dot_general(sg.astype(bf16), cg, NT_DIMS, preferred_element_type=f32)
        yt_ref[rs, :] += jnp.where(cols_of_s, yt, 0.0)
        cd = (jnp.dot(cdh_ref[rs, :], pick_s, preferred_element_type=f32)
              + jnp.dot(cdl_ref[rs, :], pick_s, preferred_element_type=f32))
        upd = jnp.dot(xdte_t_ref[rs, :], jnp.where(rows_of_s, bg, jnp.zeros_like(bg)),
                      preferred_element_type=f32)
        ssm_out_ref[0, g * (N_SSD_HEADS // N_SSD_GROUPS):(g + 1) * (N_SSD_HEADS // N_SSD_GROUPS)] = (
            sg * cd + upd).reshape(N_SSD_HEADS // N_SSD_GROUPS, SSD_HEAD_DIM, D_STATE)

    @pl.when(s == pl.num_programs(1) - 1)
    def _():
        y = ydiag_ref[...] + jnp.transpose(yt_ref[...]) * ea_ref[...]
        mix_ref[:, 0:D_SSD] = _gated_norm(y, z_ref[...], nw_ref).astype(bf16)


def _sample_mixer(z, xbc, dt, u, row0, n_seq, state_conv, state_ssm, state_pool, pos0,
                  conv_w, conv_b, dt_bias, A_log, D_skip, ssd_norm_w, pool_w, pool_scale):
    f32, bf16 = jnp.float32, jnp.bfloat16
    n_blk = n_seq // SEQ_PER_BLK
    blk0 = row0 // BLK
    nk = CONV_WIDTH - 1
    k = _mixer_constants()
    ks = _sample_constants()

    def row_blk(width):
        return pl.BlockSpec((BLK, width), lambda j, s: (blk0 + j, 0))

    def const(shape):
        return pl.BlockSpec(shape, lambda j, s: (0,) * len(shape))

    state_spec = pl.BlockSpec((1, N_SSD_HEADS, SSD_HEAD_DIM, D_STATE), lambda j, s: (j * SEQ_PER_BLK + s, 0, 0, 0))
    pad_h = (0, LANES - N_SSD_HEADS)
    return pl.pallas_call(
        functools.partial(_sample_mixer_body, pos0=pos0),
        grid=(n_blk, SEQ_PER_BLK),
        in_specs=[row_blk(D_SSD), row_blk(D_CONV), row_blk(LANES), row_blk(D_POOL),
                  pl.BlockSpec((SEQ_PER_BLK * nk, D_CONV), lambda j, s: (j, 0)),
                  pl.BlockSpec((SEQ_PER_BLK * POOL_HIST, D_POOL), lambda j, s: (j, 0)),
                  state_spec,
                  const((CONV_WIDTH, D_CONV)), const((1, D_CONV)), const((1, LANES)), const((1, LANES)),
                  const((1, D_SSD)), const((1, D_SSD)),
                  const((N_POOL_GROUPS, POOL_GROUP_DIM, POOL_GROUP_DIM)), const((1, D_POOL)),
                  const((BLK, BLK)), const((BLK, BLK)), const((LANES, D_SSD)), const((D_SSD, LANES)),
                  const((nk, BLK, BLK)), const((nk, BLK, SEQ_PER_BLK * nk)),
                  const((N_POOL_GROUPS, BLK, BLK)), const((N_POOL_GROUPS, BLK, SEQ_PER_BLK * POOL_HIST))],
        out_specs=[pl.BlockSpec((BLK, D_MIX), lambda j, s: (j, 0)), state_spec],
        out_shape=[jax.ShapeDtypeStruct((n_seq * DEC_SEQ, D_MIX), bf16),
                   jax.ShapeDtypeStruct((n_seq, N_SSD_HEADS, SSD_HEAD_DIM, D_STATE), f32)],
        scratch_shapes=[pltpu.VMEM((BLK, D_SSD), f32), pltpu.VMEM((BLK, D_SSD), f32),
                        pltpu.VMEM((D_SSD, BLK), f32), pltpu.VMEM((D_SSD, BLK), bf16),
                        pltpu.VMEM((D_SSD, BLK), bf16), pltpu.VMEM((D_SSD, BLK), bf16),
                        pltpu.VMEM((BLK, N_SSD_GROUPS * D_STATE), bf16),
                        pltpu.VMEM((BLK, N_SSD_GROUPS * D_STATE), bf16)],
        compiler_params=pltpu.CompilerParams(dimension_semantics=("parallel", "arbitrary"),
                                             vmem_limit_bytes=VMEM_LIMIT_BYTES),
        name="sample_mixer",
    )(z, xbc, dt, u, state_conv.reshape(n_seq * nk, D_CONV), state_pool.reshape(n_seq * POOL_HIST, D_POOL),
      state_ssm, conv_w.astype(f32), conv_b.reshape(1, D_CONV).astype(f32),
      jnp.pad(dt_bias.astype(f32), pad_h).reshape(1, LANES), jnp.pad(A_log.astype(f32), pad_h).reshape(1, LANES),
      jnp.repeat(D_skip.astype(f32), SSD_HEAD_DIM).reshape(1, D_SSD), ssd_norm_w.reshape(1, D_SSD).astype(f32),
      pool_w.astype(bf16), pool_scale.reshape(1, D_POOL).astype(f32),
      ks["causal"], ks["same"], k["expand"], k["expand_t"], ks["shift"], ks["stsel"],
      ks["pcur"], ks["phist"])


def _moe_sizes(n_tokens, tm):
    nt = n_tokens // tm
    lmax = -(-(TOP_K * tm + N_EXPERTS * (MOE_SEG_ROWS - 1)) // MOE_SEL_ROWS) * MOE_SEL_ROWS
    rows = (TOP_K * n_tokens + nt * N_EXPERTS * (MOE_SEG_ROWS - 1) + N_EXPERTS * (MOE_PIECE - 1)
            + MOE_ROW_CHUNK)
    n_rows = -(-rows // MOE_PIECE) * MOE_PIECE
    return nt, lmax, n_rows


def _router_body(x_ref, nw_ref, rwh_ref, rwl_ref, rb_ref, h_ref, posg_ref, post_ref, cnt_ref):
    f32, bf16 = jnp.float32, jnp.bfloat16
    tm = x_ref.shape[0]
    x = x_ref[...]
    h = x * lax.rsqrt(jnp.mean(x * x, axis=-1, keepdims=True) + EPS) * nw_ref[...]
    h_hi = h.astype(bf16)
    h_ref[...] = h_hi
    h_lo = (h - h_hi.astype(f32)).astype(bf16)
    wh = rwh_ref[...]
    logits = (jnp.dot(h_hi, wh, preferred_element_type=f32)
              + jnp.dot(h_lo, wh, preferred_element_type=f32)
              + jnp.dot(h_hi, rwl_ref[...], preferred_element_type=f32)) + rb_ref[...]
    lane = lax.broadcasted_iota(jnp.int32, (tm, LANES), 1)
    lanef = lane.astype(f32)
    neg = jnp.float32(-jnp.inf)
    l = jnp.where(lane < N_EXPERTS, logits, neg)
    sels, vals = [], []
    for _ in range(TOP_K):
        m = jnp.max(l, axis=1, keepdims=True)
        idx = jnp.min(jnp.where(l == m, lanef, jnp.float32(LANES)), axis=1, keepdims=True)
        sel = lanef == idx
        l = jnp.where(sel, neg, l)
        sels.append(sel)
        vals.append(m)
    exps = [jnp.exp(v - vals[0]) for v in vals]
    denom = exps[0] + exps[1] + exps[2] + exps[3]
    gates = [e / denom for e in exps]
    chosen = jnp.where(sels[0] | sels[1] | sels[2] | sels[3], 1.0, 0.0).astype(f32)
    row = lax.broadcasted_iota(jnp.int32, (tm, tm), 0)
    col = lax.broadcasted_iota(jnp.int32, (tm, tm), 1)
    lower = jnp.where(col < row, 1.0, 0.0).astype(bf16)
    rank = jnp.dot(lower, chosen.astype(bf16), preferred_element_type=f32)
    cnt = jnp.sum(chosen, axis=0, keepdims=True)
    seg_units = jnp.floor((cnt + (MOE_SEG_ROWS - 1)) * (1.0 / MOE_SEG_ROWS))
    r2 = lax.broadcasted_iota(jnp.int32, (LANES, LANES), 0)
    c2 = lax.broadcasted_iota(jnp.int32, (LANES, LANES), 1)
    upper = jnp.where(r2 < c2, 1.0, 0.0).astype(bf16)
    lstart = jnp.dot(jnp.broadcast_to(seg_units, (8, LANES)).astype(bf16), upper,
                     preferred_element_type=f32)[0:1, :] * MOE_SEG_ROWS
    posmat = lstart + rank
    posg = jnp.zeros((tm, LANES), f32)
    for k in range(TOP_K):
        pos_k = jnp.sum(jnp.where(sels[k], posmat, 0.0), axis=1, keepdims=True)
        posg = posg + jnp.where(lane == k, pos_k, 0.0) + jnp.where(lane == TOP_K + k, gates[k], 0.0)
    posg_ref[...] = posg
    post_ref[...] = jnp.transpose(posg)[0:8, :]
    cnt_ref[0] = jnp.broadcast_to(cnt, (8, LANES)).astype(jnp.int32)


def _moe_router(x1, norm2_w, router_w, router_b, tm):
    n, d = x1.shape
    nt = n // tm
    f32, bf16 = jnp.float32, jnp.bfloat16
    rw = jnp.pad(router_w.astype(f32), ((0, 0), (0, LANES - N_EXPERTS)))
    rw_hi = rw.astype(bf16)
    rw_lo = (rw - rw_hi.astype(f32)).astype(bf16)
    rb = jnp.pad(router_b.astype(f32), (0, LANES - N_EXPERTS)).reshape(1, LANES)
    return pl.pallas_call(
        _router_body,
        grid=(nt,),
        in_specs=[pl.BlockSpec((tm, d), lambda i: (i, 0)),
                  pl.BlockSpec((1, d), lambda i: (0, 0)),
                  pl.BlockSpec((d, LANES), lambda i: (0, 0)),
                  pl.BlockSpec((d, LANES), lambda i: (0, 0)),
                  pl.BlockSpec((1, LANES), lambda i: (0, 0))],
        out_specs=[pl.BlockSpec((tm, d), lambda i: (i, 0)),
                   pl.BlockSpec((tm, LANES), lambda i: (i, 0)),
                   pl.BlockSpec((8, tm), lambda i: (0, i)),
                   pl.BlockSpec((1, 8, LANES), lambda i: (i, 0, 0))],
        out_shape=[jax.ShapeDtypeStruct((n, d), bf16),
                   jax.ShapeDtypeStruct((n, LANES), f32),
                   jax.ShapeDtypeStruct((8, n), f32),
                   jax.ShapeDtypeStruct((nt, 8, LANES), jnp.int32)],
        compiler_params=pltpu.CompilerParams(dimension_semantics=("parallel",),
                                             vmem_limit_bytes=VMEM_LIMIT_BYTES),
        name="moe_router",
    )(x1, norm2_w.reshape(1, d).astype(f32), rw_hi, rw_lo, rb)


def _moe_plan(cnt):
    i32 = jnp.int32
    pad = (cnt + (MOE_SEG_ROWS - 1)) // MOE_SEG_ROWS * MOE_SEG_ROWS
    lstart = jnp.cumsum(pad, axis=1) - pad
    lp = jnp.sum(pad, axis=1)
    tot = jnp.sum(pad, axis=0)
    reg = (tot + (MOE_PIECE - 1)) // MOE_PIECE * MOE_PIECE
    reg_end = jnp.cumsum(reg)
    estart = reg_end - reg
    seg = estart[None, :] + jnp.cumsum(pad, axis=0) - pad
    return dict(
        lstart=lstart.reshape(-1).astype(i32), seg_units=(pad // MOE_SEG_ROWS).reshape(-1).astype(i32),
        seg=seg.reshape(-1).astype(i32), lp=lp.astype(i32),
        tail_start=(estart + tot).astype(i32), tail_units=((reg - tot) // MOE_SEG_ROWS).astype(i32),
        estart=estart.astype(i32), erows=reg.astype(i32), used=reg_end[-1].reshape(1).astype(i32))


def _for_each_segment_copy(i, lstart_ref, units_ref, seg_ref, local_ref, global_ref, sem, to_global, fn):
    def per_expert(e, carry):
        k = i * N_EXPERTS + e

        @pl.when(units_ref[k] > 0)
        def _():
            n = pl.multiple_of(units_ref[k] * MOE_SEG_ROWS, MOE_SEG_ROWS)
            loc = local_ref.at[pl.ds(pl.multiple_of(lstart_ref[k], MOE_SEG_ROWS), n)]
            glo = global_ref.at[pl.ds(pl.multiple_of(seg_ref[k], MOE_SEG_ROWS), n)]
            fn(pltpu.make_async_copy(loc, glo, sem) if to_global else pltpu.make_async_copy(glo, loc, sem))
        return carry
    lax.fori_loop(0, N_EXPERTS, per_expert, 0)


def _for_each_unused_piece(used_ref, zero_ref, rows_ref, sem, fn):
    def per_piece(j, c):
        go = pl.multiple_of(j * MOE_PIECE, MOE_PIECE)
        fn(pltpu.make_async_copy(zero_ref, rows_ref.at[pl.ds(go, MOE_PIECE)], sem))
        return c
    lax.fori_loop(used_ref[0] // MOE_PIECE, rows_ref.shape[0] // MOE_PIECE, per_piece, 0)


def _dispatch_body(lstart_ref, units_ref, seg_ref, lp_ref, tail_start_ref, tail_units_ref, used_ref,
                   h_ref, post_ref, xs_ref, stage_ref, sel_ref, zero_ref, sems, fill_sem):
    f32, bf16 = jnp.float32, jnp.bfloat16
    i = pl.program_id(0)
    nt = pl.num_programs(0)
    slot = i % 2
    tm = h_ref.shape[0]
    lmax = stage_ref.shape[1]

    def for_each_fill_copy(fn):
        def per_expert(e, carry):
            @pl.when(tail_units_ref[e] > 0)
            def _():
                n = pl.multiple_of(tail_units_ref[e] * MOE_SEG_ROWS, MOE_SEG_ROWS)
                go = pl.multiple_of(tail_start_ref[e], MOE_SEG_ROWS)
                fn(pltpu.make_async_copy(zero_ref.at[pl.ds(0, n)], xs_ref.at[pl.ds(go, n)], fill_sem))
            return carry
        lax.fori_loop(0, N_EXPERTS, per_expert, 0)
        _for_each_unused_piece(used_ref, zero_ref, xs_ref, fill_sem, fn)

    @pl.when(i == 0)
    def _():
        zero_ref[...] = jnp.zeros(zero_ref.shape, bf16)
        for_each_fill_copy(lambda cp: cp.start())
        for_each_fill_copy(lambda cp: cp.wait())

    def segment_copies(tile, slot_, fn):
        _for_each_segment_copy(tile, lstart_ref, units_ref, seg_ref, stage_ref.at[slot_], xs_ref,
                               sems.at[slot_], True, fn)

    @pl.when(i >= 2)
    def _():
        segment_copies(i - 2, slot, lambda cp: cp.wait())

    pos = [post_ref[k:k + 1, :] for k in range(TOP_K)]
    r_local = lax.broadcasted_iota(jnp.int32, (MOE_CHUNK, tm), 0).astype(f32).astype(bf16)
    one, zero = jnp.ones((), bf16), jnp.zeros((), bf16)
    for part in range(lmax // MOE_SEL_ROWS):
        @pl.when(part * MOE_SEL_ROWS < lp_ref[i])
        def _():
            for c in range(MOE_SEL_ROWS // MOE_CHUNK):
                r0 = part * MOE_SEL_ROWS + c * MOE_CHUNK
                loc = [jnp.clip(p - r0, -1.0, float(MOE_CHUNK)).astype(bf16) for p in pos]
                hit = (loc[0] == r_local) | (loc[1] == r_local) | (loc[2] == r_local) | (loc[3] == r_local)
                sel_ref[c * MOE_CHUNK:(c + 1) * MOE_CHUNK, :] = jnp.where(hit, one, zero)
            stage_ref[slot, part * MOE_SEL_ROWS:(part + 1) * MOE_SEL_ROWS, :] = jnp.dot(
                sel_ref[...], h_ref[...], preferred_element_type=f32).astype(bf16)

    segment_copies(i, slot, lambda cp: cp.start())

    @pl.when(i == nt - 1)
    def _():
        @pl.when(nt >= 2)
        def _():
            segment_copies(i - 1, 1 - slot, lambda cp: cp.wait())
        segment_copies(i, slot, lambda cp: cp.wait())


def _moe_dispatch(h2, post, plan, tm, lmax, n_rows):
    n, d = h2.shape
    nt = n // tm
    grid_spec = pltpu.PrefetchScalarGridSpec(
        num_scalar_prefetch=7,
        grid=(nt,),
        in_specs=[pl.BlockSpec((tm, d), lambda i, *_: (i, 0)),
                  pl.BlockSpec((8, tm), lambda i, *_: (0, i))],
        out_specs=pl.BlockSpec(memory_space=pl.ANY),
        scratch_shapes=[pltpu.VMEM((2, lmax, d), jnp.bfloat16),
                        pltpu.VMEM((MOE_SEL_ROWS, tm), jnp.bfloat16),
                        pltpu.VMEM((MOE_PIECE, d), jnp.bfloat16),
                        pltpu.SemaphoreType.DMA((2,)),
                        pltpu.SemaphoreType.DMA(())])
    return pl.pallas_call(
        _dispatch_body,
        grid_spec=grid_spec,
        out_shape=jax.ShapeDtypeStruct((n_rows, d), jnp.bfloat16),
        compiler_params=pltpu.CompilerParams(dimension_semantics=("arbitrary",),
                                             vmem_limit_bytes=VMEM_LIMIT_BYTES),
        name="moe_dispatch",
    )(plan["lstart"], plan["seg_units"], plan["seg"], plan["lp"], plan["tail_start"], plan["tail_units"],
      plan["used"], h2, post)


def _experts_body(first_ref, count_ref, cstart_ref, cvalid_ref, total_ref, used_ref,
                  xs_ref, wgu_ref, bgu_ref, wd_ref, bd_ref, os_ref,
                  wgu_bf, wd_bf, xbuf, obuf, zero_ref, in_sems, out_sems, fill_sem):
    f32, bf16 = jnp.float32, jnp.bfloat16
    e = pl.program_id(0)
    total = total_ref[0]
    half = D_FF // 2

    def in_copy(j):
        src = xs_ref.at[pl.ds(pl.multiple_of(cstart_ref[j], MOE_PIECE), MOE_ROW_CHUNK)]
        return pltpu.make_async_copy(src, xbuf.at[j % 3], in_sems.at[j % 3])

    def out_copy(j):
        n = pl.multiple_of(cvalid_ref[j], MOE_PIECE)
        go = pl.multiple_of(cstart_ref[j], MOE_PIECE)
        return pltpu.make_async_copy(obuf.at[j % 2, pl.ds(0, n)], os_ref.at[pl.ds(go, n)], out_sems.at[j % 2])

    @pl.when(e == 0)
    def _():
        for j0 in range(2):
            @pl.when(j0 < total)
            def _():
                in_copy(j0).start()
        zero_ref[...] = jnp.zeros(zero_ref.shape, bf16)
        _for_each_unused_piece(used_ref, zero_ref, os_ref, fill_sem, lambda cp: cp.start())
        _for_each_unused_piece(used_ref, zero_ref, os_ref, fill_sem, lambda cp: cp.wait())

    @pl.when(count_ref[e] > 0)
    def _():
        wgu_bf[...] = wgu_ref[0].astype(bf16)
        wd_bf[...] = wd_ref[0].astype(bf16)

        def chunk(j, carry):
            in_copy(j).wait()

            @pl.when(j + 2 < total)
            def _():
                in_copy(j + 2).start()

            @pl.when(j >= 2)
            def _():
                out_copy(j - 2).wait()

            x = xbuf[j % 3]
            out = bd_ref[0]
            for hf in range(2):
                gate = jnp.dot(x, wgu_bf[:, hf * half:(hf + 1) * half], preferred_element_type=f32)
                gate = jnp.minimum(gate + bgu_ref[0, :, hf * half:(hf + 1) * half], SWIGLU_LIMIT)
                up = jnp.dot(x, wgu_bf[:, D_FF + hf * half:D_FF + (hf + 1) * half], preferred_element_type=f32)
                up = jnp.clip(up + bgu_ref[0, :, D_FF + hf * half:D_FF + (hf + 1) * half],
                              -SWIGLU_LIMIT, SWIGLU_LIMIT)
                act = (up + 1.0) * (gate * jax.nn.sigmoid(SWIGLU_ALPHA * gate))
                out = out + jnp.dot(act.astype(bf16), wd_bf[hf * half:(hf + 1) * half, :],
                                    preferred_element_type=f32)
            obuf[j % 2] = out.astype(bf16)
            out_copy(j).start()
            return carry
        lax.fori_loop(first_ref[e], first_ref[e] + count_ref[e], chunk, 0)

    @pl.when(e == pl.num_programs(0) - 1)
    def _():
        @pl.when(total >= 2)
        def _():
            out_copy(total - 2).wait()

        @pl.when(total >= 1)
        def _():
            out_copy(total - 1).wait()


def _expert_chunks(plan, n_rows):
    i32 = jnp.int32
    max_chunks = n_rows // MOE_ROW_CHUNK + N_EXPERTS
    count = (plan["erows"] + (MOE_ROW_CHUNK - 1)) // MOE_ROW_CHUNK
    end = jnp.cumsum(count)
    first = end - count
    j = jnp.arange(max_chunks, dtype=i32)
    mine = ((first[None, :] <= j[:, None]) & (j[:, None] < end[None, :])).astype(i32)
    c = j - jnp.sum(mine * first[None, :], axis=1)
    cstart = jnp.sum(mine * plan["estart"][None, :], axis=1) + jnp.sum(mine, axis=1) * c * MOE_ROW_CHUNK
    cvalid = jnp.sum(mine * jnp.clip(plan["erows"][None, :] - c[:, None] * MOE_ROW_CHUNK, 0, MOE_ROW_CHUNK),
                     axis=1)
    return (first.astype(i32), count.astype(i32), cstart.astype(i32), cvalid.astype(i32),
            end[-1].reshape(1).astype(i32))


def _moe_experts(xs, plan, w_gate_up, b_gate_up, w_down, b_down):
    d = xs.shape[1]
    grid_spec = pltpu.PrefetchScalarGridSpec(
        num_scalar_prefetch=6,
        grid=(N_EXPERTS,),
        in_specs=[pl.BlockSpec(memory_space=pl.ANY),
                  pl.BlockSpec((1, d, 2 * D_FF), lambda e, *_: (e, 0, 0)),
                  pl.BlockSpec((1, 1, 2 * D_FF), lambda e, *_: (e, 0, 0)),
                  pl.BlockSpec((1, D_FF, d), lambda e, *_: (e, 0, 0)),
                  pl.BlockSpec((1, 1, d), lambda e, *_: (e, 0, 0))],
        out_specs=pl.BlockSpec(memory_space=pl.ANY),
        scratch_shapes=[pltpu.VMEM((d, 2 * D_FF), jnp.bfloat16),
                        pltpu.VMEM((D_FF, d), jnp.bfloat16),
                        pltpu.VMEM((3, MOE_ROW_CHUNK, d), jnp.bfloat16),
                        pltpu.VMEM((2, MOE_ROW_CHUNK, d), jnp.bfloat16),
                        pltpu.VMEM((MOE_PIECE, d), jnp.bfloat16),
                        pltpu.SemaphoreType.DMA((3,)),
                        pltpu.SemaphoreType.DMA((2,)),
                        pltpu.SemaphoreType.DMA(())])
    return pl.pallas_call(
        _experts_body,
        grid_spec=grid_spec,
        out_shape=jax.ShapeDtypeStruct(xs.shape, jnp.bfloat16),
        compiler_params=pltpu.CompilerParams(dimension_semantics=("arbitrary",),
                                             vmem_limit_bytes=VMEM_LIMIT_BYTES),
        name="moe_experts",
    )(*_expert_chunks(plan, xs.shape[0]), plan["used"],
      xs, w_gate_up, b_gate_up.reshape(N_EXPERTS, 1, 2 * D_FF), w_down, b_down.reshape(N_EXPERTS, 1, d))


def _combine_body(lstart_ref, units_ref, seg_ref, lp_ref,
                  os_ref, posg_ref, x_ref, fw_ref, yp_ref, ys_ref, stage_ref, w_ref, sems,
                  *, n_prompt_tiles):
    f32, bf16 = jnp.float32, jnp.bfloat16
    i = pl.program_id(0)
    nt = pl.num_programs(0)
    slot = i % 2
    tm = x_ref.shape[0]
    lmax = stage_ref.shape[1]

    def segment_copies(tile, slot_, fn):
        _for_each_segment_copy(tile, lstart_ref, units_ref, seg_ref, stage_ref.at[slot_], os_ref,
                               sems.at[slot_], False, fn)

    @pl.when(i == 0)
    def _():
        stage_ref[...] = jnp.zeros(stage_ref.shape, bf16)
        segment_copies(0, 0, lambda cp: cp.start())

    @pl.when(i + 1 < nt)
    def _():
        segment_copies(i + 1, 1 - slot, lambda cp: cp.start())

    posg = posg_ref[...]
    pos = [posg[:, k:k + 1] for k in range(TOP_K)]
    gate = [posg[:, TOP_K + k:TOP_K + k + 1] for k in range(TOP_K)]
    gate_bf = [g.astype(bf16) for g in gate]
    r_local = lax.broadcasted_iota(jnp.int32, (tm, MOE_CHUNK), 1).astype(f32).astype(bf16)
    for c in range(lmax // MOE_CHUNK):
        w = jnp.zeros((tm, MOE_CHUNK), bf16)
        for k in range(TOP_K):
            loc = jnp.clip(pos[k] - c * MOE_CHUNK, -1.0, float(MOE_CHUNK)).astype(bf16)
            w = jnp.where(loc == r_local, gate_bf[k], w)
        w_ref[:, c * MOE_CHUNK:(c + 1) * MOE_CHUNK] = w

    segment_copies(i, slot, lambda cp: cp.wait())
    y = x_ref[...] + jnp.dot(w_ref[...], stage_ref[slot], preferred_element_type=f32)
    out = y * lax.rsqrt(jnp.mean(y * y, axis=-1, keepdims=True) + EPS) * fw_ref[...]

    @pl.when(i < n_prompt_tiles)
    def _():
        yp_ref[...] = out

    @pl.when(i >= n_prompt_tiles)
    def _():
        ys_ref[...] = out


def _moe_combine(os_, posg, x1, final_norm_w, plan, tm, lmax, n_prompt):
    n, d = x1.shape
    nt = n // tm
    n_prompt_tiles = n_prompt // tm
    n_sample_tiles = nt - n_prompt_tiles
    grid_spec = pltpu.PrefetchScalarGridSpec(
        num_scalar_prefetch=4,
        grid=(nt,),
        in_specs=[pl.BlockSpec(memory_space=pl.ANY),
                  pl.BlockSpec((tm, LANES), lambda i, *_: (i, 0)),
                  pl.BlockSpec((tm, d), lambda i, *_: (i, 0)),
                  pl.BlockSpec((1, d), lambda i, *_: (0, 0))],
        out_specs=[pl.BlockSpec((tm, d), lambda i, *_: (jnp.minimum(i, n_prompt_tiles - 1), 0)),
                   pl.BlockSpec((tm, d), lambda i, *_: (jnp.maximum(i - n_prompt_tiles, 0), 0))],
        scratch_shapes=[pltpu.VMEM((2, lmax, d), jnp.bfloat16),
                        pltpu.VMEM((tm, lmax), jnp.bfloat16),
                        pltpu.SemaphoreType.DMA((2,))])
    return pl.pallas_call(
        functools.partial(_combine_body, n_prompt_tiles=n_prompt_tiles),
        grid_spec=grid_spec,
        out_shape=[jax.ShapeDtypeStruct((n_prompt, d), jnp.float32),
                   jax.ShapeDtypeStruct((n_sample_tiles * tm, d), jnp.float32)],
        compiler_params=pltpu.CompilerParams(dimension_semantics=("arbitrary",),
                                             vmem_limit_bytes=VMEM_LIMIT_BYTES),
        name="moe_combine",
    )(plan["lstart"], plan["seg_units"], plan["seg"], plan["lp"],
      os_, posg, x1, final_norm_w.reshape(1, d).astype(jnp.float32))


def _moe_block(x1, n_prompt, norm2_w, router_w, router_b, w_gate_up, b_gate_up, w_down, b_down,
               final_norm_w, tm=MOE_TOKEN_TILE):
    n = x1.shape[0]
    nt, lmax, n_rows = _moe_sizes(n, tm)
    h2, posg, post, cnt3 = _moe_router(x1, norm2_w, router_w, router_b, tm)
    plan = _moe_plan(cnt3[:, 0, :N_EXPERTS])
    xs = _moe_dispatch(h2, post, plan, tm, lmax, n_rows)
    os_ = _moe_experts(xs, plan, w_gate_up, b_gate_up, w_down, b_down)
    return _moe_combine(os_, posg, x1, final_norm_w, plan, tm, lmax, n_prompt)


def kernel(x_prompt, x_sample, state_ssm, state_conv, state_pool, norm1_w, w_in, conv_w, conv_b, dt_bias,
           A_log, D_skip, ssd_norm_w, pool_w, pool_scale, w_out, norm2_w, router_w, router_b, w_gate_up,
           b_gate_up, w_down, b_down, final_norm_w):
    n_prompt = BATCH * SEQ
    n_sample = DEC_BATCH * DEC_SEQ
    xp = x_prompt.reshape(n_prompt, D_MODEL)
    xs = x_sample.reshape(n_sample, D_MODEL)
    z, xbc, dt_raw, u = _in_proj(xp, xs, norm1_w[0], w_in[0])
    mp = (conv_w[0], conv_b[0], dt_bias[0], A_log[0], D_skip[0], ssd_norm_w[0], pool_w[0], pool_scale[0])
    mix_p, s1 = _prompt_mixer(z, xbc, dt_raw, u, BATCH, SEQ, *mp)
    mix_s, s2 = _sample_mixer(z, xbc, dt_raw, u, n_prompt, DEC_BATCH, state_conv[0], state_ssm[0], state_pool[0],
                              PAST_LEN, *mp)
    nk = CONV_WIDTH - 1
    c1 = jnp.stack([xbc[(b + 1) * SEQ - nk:(b + 1) * SEQ] for b in range(BATCH)])
    p1 = jnp.stack([u[(b + 1) * SEQ - POOL_HIST:(b + 1) * SEQ] for b in range(BATCH)])
    c2 = xbc[n_prompt:].reshape(DEC_BATCH, DEC_SEQ, D_CONV)[:, DEC_SEQ - nk:]
    p2 = jnp.concatenate([state_pool[0][:, DEC_SEQ:], u[n_prompt:].reshape(DEC_BATCH, DEC_SEQ, D_POOL)], axis=1)
    x1 = _out_proj(mix_p, mix_s, w_out[0], xp, xs)
    yp, ys = _moe_block(x1, n_prompt, norm2_w[0], router_w[0], router_b[0], w_gate_up[0], b_gate_up[0],
                        w_down[0], b_down[0], final_norm_w)
    return (yp.reshape(x_prompt.shape), ys.reshape(x_sample.shape),
            s1[None], c1[None], p1[None], s2[None], c2[None], p2[None])
```

```python
import functools
import math
import jax, jax.numpy as jnp
from jax import lax
import numpy as np
from jax.experimental import pallas as pl
from jax.experimental.pallas import tpu as pltpu

D_MODEL = 1024
BATCH = 8
SEQ = 2048
DEC_BATCH = 128
DEC_SEQ = 4
PAST_LEN = 16384

D_MIX = 2 * D_MODEL
D_SSD = 3 * D_MIX // 4
SSD_HEAD_DIM = 64
N_SSD_HEADS = D_SSD // SSD_HEAD_DIM
N_SSD_GROUPS = 4
D_STATE = 128
CONV_WIDTH = 4
SSD_CHUNK = 128
D_CONV = D_SSD + 2 * N_SSD_GROUPS * D_STATE
D_POOL = D_MIX - D_SSD
POOL_WINDOWS = (2, 4, 8, 16)
N_POOL_GROUPS = len(POOL_WINDOWS)
POOL_GROUP_DIM = D_POOL // N_POOL_GROUPS
POOL_HIST = max(POOL_WINDOWS) - 1
D_IN_PROJ = D_SSD + D_CONV + N_SSD_HEADS + D_POOL
N_EXPERTS = 32
TOP_K = 4
D_FF = D_MODEL
SWIGLU_LIMIT = 7.0
SWIGLU_ALPHA = 1.702
EPS = 1e-5

LANES = 128
BF16_SUBLANES = 16
VMEM_LIMIT_BYTES = 48 * 1024 * 1024

MOE_TOKEN_TILE = 512
MOE_SEG_ROWS = BF16_SUBLANES
MOE_PIECE = 128
MOE_ROW_CHUNK = 1280
MOE_CHUNK = 256
MOE_SEL_ROWS = 1280


BLK = SSD_CHUNK
PROJ_ROW_TILE = 512
HIST_ROWS = 16
CONV_TAIL_ROWS = 8
NT_DIMS = (((1,), (1,)), ((), ()))


def _split2(v):
    hi = v.astype(jnp.bfloat16)
    lo = (v - hi.astype(jnp.float32)).astype(jnp.bfloat16)
    return hi, lo


def _dot_sel_left(sel, v, passes):
    out = None
    rem = v
    for p in range(passes):
        part = rem.astype(jnp.bfloat16)
        d = jnp.dot(sel, part, preferred_element_type=jnp.float32)
        out = d if out is None else out + d
        if p + 1 < passes:
            rem = rem - part.astype(jnp.float32)
    return out


def _dot_sel_right(v, sel, passes):
    out = None
    rem = v
    for p in range(passes):
        part = rem.astype(jnp.bfloat16)
        d = jnp.dot(part, sel, preferred_element_type=jnp.float32)
        out = d if out is None else out + d
        if p + 1 < passes:
            rem = rem - part.astype(jnp.float32)
    return out


def _two_part_specs(n_first, n_second, tm, width):
    t1 = n_first // tm
    t2 = n_second // tm
    return (pl.BlockSpec((tm, width), lambda i: (jnp.minimum(i, t1 - 1), 0)),
            pl.BlockSpec((tm, width), lambda i: (jnp.clip(i - t1, 0, t2 - 1), 0)))


def _in_proj_body(xa_ref, xb_ref, nw_ref, w_ref, z_ref, xbc_ref, dt_ref, u_ref, *, tiles_a):
    x = jnp.where(pl.program_id(0) < tiles_a, xa_ref[...], xb_ref[...])
    h = (x * lax.rsqrt(jnp.mean(x * x, axis=-1, keepdims=True) + EPS) * nw_ref[...]).astype(jnp.bfloat16)
    off = 0
    for ref in (z_ref, xbc_ref, dt_ref, u_ref):
        n = ref.shape[1]
        ref[...] = jnp.dot(h, w_ref[:, off:off + n], preferred_element_type=jnp.float32)
        off += n


def _in_proj(xa, xb, norm1_w, w_in):
    d = xa.shape[1]
    n = xa.shape[0] + xb.shape[0]
    f32, bf16 = jnp.float32, jnp.bfloat16
    s1, s2 = D_SSD + D_CONV, D_SSD + D_CONV + N_SSD_HEADS
    w = jnp.concatenate([w_in[:, :s1], jnp.pad(w_in[:, s1:s2], ((0, 0), (0, LANES - N_SSD_HEADS))),
                         w_in[:, s2:]], axis=1).astype(bf16)
    widths = (D_SSD, D_CONV, LANES, D_POOL)
    tm = PROJ_ROW_TILE
    return pl.pallas_call(
        functools.partial(_in_proj_body, tiles_a=xa.shape[0] // tm),
        grid=(n // tm,),
        in_specs=[*_two_part_specs(xa.shape[0], xb.shape[0], tm, d),
                  pl.BlockSpec((1, d), lambda i: (0, 0)),
                  pl.BlockSpec((d, sum(widths)), lambda i: (0, 0), pipeline_mode=pl.Buffered(1))],
        out_specs=[pl.BlockSpec((tm, wd), lambda i: (i, 0)) for wd in widths],
        out_shape=[jax.ShapeDtypeStruct((n, wd), f32) for wd in widths],
        compiler_params=pltpu.CompilerParams(dimension_semantics=("parallel",),
                                             vmem_limit_bytes=VMEM_LIMIT_BYTES),
        name="in_proj",
    )(xa, xb, norm1_w.reshape(1, d).astype(f32), w)


def _out_proj_body(ma_ref, mb_ref, w_ref, xa_ref, xb_ref, o_ref, *, tiles_a):
    first = pl.program_id(0) < tiles_a
    m = jnp.where(first, ma_ref[...], mb_ref[...])
    x = jnp.where(first, xa_ref[...], xb_ref[...])
    o_ref[...] = x + jnp.dot(m, w_ref[...], preferred_element_type=jnp.float32)


def _out_proj(ma, mb, w_out, xa, xb):
    d = xa.shape[1]
    n = xa.shape[0] + xb.shape[0]
    tm = PROJ_ROW_TILE
    return pl.pallas_call(
        functools.partial(_out_proj_body, tiles_a=xa.shape[0] // tm),
        grid=(n // tm,),
        in_specs=[*_two_part_specs(xa.shape[0], xb.shape[0], tm, D_MIX),
                  pl.BlockSpec((D_MIX, d), lambda i: (0, 0)),
                  *_two_part_specs(xa.shape[0], xb.shape[0], tm, d)],
        out_specs=pl.BlockSpec((tm, d), lambda i: (i, 0)),
        out_shape=jax.ShapeDtypeStruct((n, d), jnp.float32),
        compiler_params=pltpu.CompilerParams(dimension_semantics=("parallel",),
                                             vmem_limit_bytes=VMEM_LIMIT_BYTES),
        name="out_proj",
    )(ma, mb, w_out.astype(jnp.bfloat16), xa, xb)


def _mixer_constants():
    bf16 = jnp.bfloat16
    h = np.arange(LANES)[:, None]
    ch = np.arange(D_SSD)[None, :]
    expand = (ch // SSD_HEAD_DIM == h).astype(np.float32)
    i = np.arange(BLK)[:, None]
    j = np.arange(BLK)[None, :]
    causal = (j <= i).astype(np.float32)
    jh = np.arange(HIST_ROWS)[None, :]
    pcur = np.stack([((j <= i) & (i - j < w)) for w in POOL_WINDOWS]).astype(np.float32)
    phist = np.stack([(i + HIST_ROWS - jh < w) for w in POOL_WINDOWS]).astype(np.float32)
    return dict(expand=jnp.asarray(expand, bf16), expand_t=jnp.asarray(expand.T, bf16),
                causal=jnp.asarray(causal, bf16), pcur=jnp.asarray(pcur, bf16),
                phist=jnp.asarray(phist, bf16))


def _softplus(x):
    return jnp.maximum(x, 0.0) + jnp.log(1.0 + jnp.exp(-jnp.abs(x)))


def _conv_silu(ext_ref, cw_ref, cb_ref, first_row):
    ext = ext_ref[...]
    last = first_row + CONV_WIDTH - 1
    acc = cb_ref[...] + cw_ref[CONV_WIDTH - 1:CONV_WIDTH, :] * ext[last:last + BLK, :]
    for k in range(CONV_WIDTH - 1):
        tap = pltpu.roll(ext, CONV_WIDTH - 1 - k, axis=0)[last:last + BLK, :]
        acc = acc + cw_ref[k:k + 1, :] * tap
    return acc * jax.nn.sigmoid(acc)


def _ssd_intra(xbc_c, dt_raw, dtb_ref, alog_ref, causal_bf, expand_ref):
    f32 = jnp.float32
    xs = xbc_c[:, :D_SSD]
    bm = xbc_c[:, D_SSD:D_SSD + N_SSD_GROUPS * D_STATE]
    cm = xbc_c[:, D_SSD + N_SSD_GROUPS * D_STATE:]
    dt = _softplus(dt_raw + dtb_ref[...])
    a = dt * (-jnp.exp(alog_ref[...]))
    a_cum = _dot_sel_left(causal_bf, a, 3)
    dt_x = _dot_sel_right(dt, expand_ref[...], 2)
    return xs, bm, cm, dt, a_cum, xs * dt_x


def _ssd_diag_group(g, cb, a_cum, a_cum_t, keep, xdt):
    f32, bf16 = jnp.float32, jnp.bfloat16
    hg = N_SSD_HEADS // N_SSD_GROUPS
    lane = lax.broadcasted_iota(jnp.int32, (BLK, LANES), 1)
    first_head = lane < SSD_HEAD_DIM
    neg = jnp.float32(-jnp.inf)
    outs = []
    for pr in range(hg * SSD_HEAD_DIM // LANES):
        h1 = g * hg + 2 * pr
        blk = (g * hg * SSD_HEAD_DIM) // LANES + pr
        xp = xdt[:, blk * LANES:(blk + 1) * LANES]
        x1 = jnp.where(first_head, xp, 0.0).astype(bf16)
        x2 = jnp.where(first_head, 0.0, xp).astype(bf16)
        m1 = (cb * jnp.exp(jnp.where(keep, a_cum[:, h1:h1 + 1] - a_cum_t[h1:h1 + 1, :], neg))).astype(bf16)
        m2 = (cb * jnp.exp(jnp.where(keep, a_cum[:, h1 + 1:h1 + 2] - a_cum_t[h1 + 1:h1 + 2, :], neg))).astype(bf16)
        outs.append(jnp.dot(m1, x1, preferred_element_type=f32) + jnp.dot(m2, x2, preferred_element_type=f32))
    return jnp.concatenate(outs, axis=1)


def _gated_norm(y, z, nw_ref):
    yg = y * (z * jax.nn.sigmoid(z))
    return yg * lax.rsqrt(jnp.mean(yg * yg, axis=-1, keepdims=True) + EPS) * nw_ref[...]


def _prompt_mixer_body(z_ref, xbc_ref, dt_ref, u_ref, cw_ref, cb_ref, dtb_ref, alog_ref, dskip_ref, nw_ref,
                       pw_ref, ps_ref, causal_ref, pcur_ref, phist_ref,
                       mix_ref, ssm_ref, ext_ref, pool_tail_ref, state_ref):
    f32, bf16 = jnp.float32, jnp.bfloat16
    c = pl.program_id(1)
    gw = D_SSD // N_SSD_GROUPS

    @pl.when(c == 0)
    def _():
        ext_ref[0:CONV_TAIL_ROWS, :] = jnp.zeros((CONV_TAIL_ROWS, D_CONV), f32)
        pool_tail_ref[...] = jnp.zeros(pool_tail_ref.shape, f32)
        state_ref[...] = jnp.zeros(state_ref.shape, f32)

    ext_ref[CONV_TAIL_ROWS:CONV_TAIL_ROWS + BLK, :] = xbc_ref[...]
    xbc_c = _conv_silu(ext_ref, cw_ref, cb_ref, CONV_TAIL_ROWS - (CONV_WIDTH - 1))
    ext_ref[0:CONV_TAIL_ROWS, :] = xbc_ref[BLK - CONV_TAIL_ROWS:BLK, :]

    causal_bf = causal_ref[...]
    keep = causal_bf > 0
    xs = xbc_c[:, :D_SSD]
    bm = xbc_c[:, D_SSD:D_SSD + N_SSD_GROUPS * D_STATE]
    cm = xbc_c[:, D_SSD + N_SSD_GROUPS * D_STATE:]
    dt = _softplus(dt_ref[...] + dtb_ref[...])
    a_cum = _dot_sel_left(causal_bf, dt * (-jnp.exp(alog_ref[...])), 3)
    a_cum_t = jnp.transpose(a_cum)
    a_tot = a_cum[BLK - 1:BLK, :]
    ea = jnp.exp(a_cum)
    dte = jnp.exp(a_tot - a_cum)
    cd = jnp.exp(a_tot)
    hg = N_SSD_HEADS // N_SSD_GROUPS
    first_head = lax.broadcasted_iota(jnp.int32, (BLK, LANES), 1) < SSD_HEAD_DIM
    neg = jnp.float32(-jnp.inf)

    def head_cols(v, h1):
        return jnp.where(first_head, v[:, h1:h1 + 1], v[:, h1 + 1:h1 + 2])

    def decay_from(h):
        return jnp.exp(jnp.where(keep, a_cum[:, h:h + 1] - a_cum_t[h:h + 1, :], neg))

    y_parts = []
    for g in range(N_SSD_GROUPS):
        cg = cm[:, g * D_STATE:(g + 1) * D_STATE].astype(bf16)
        bg = bm[:, g * D_STATE:(g + 1) * D_STATE].astype(bf16)
        cb = lax.dot_general(cg, bg, NT_DIMS, preferred_element_type=f32)
        sg = state_ref[g * gw:(g + 1) * gw, :]
        y_off = lax.dot_general(cg, sg.astype(bf16), NT_DIMS, preferred_element_type=f32)
        xdte_parts = []
        for pr in range(gw // LANES):
            h1 = g * hg + 2 * pr
            sl = slice(h1 * SSD_HEAD_DIM, h1 * SSD_HEAD_DIM + LANES)
            xp = xs[:, sl] * head_cols(dt, h1)
            x1 = jnp.where(first_head, xp, 0.0).astype(bf16)
            x2 = jnp.where(first_head, 0.0, xp).astype(bf16)
            y_diag = (jnp.dot((cb * decay_from(h1)).astype(bf16), x1, preferred_element_type=f32)
                      + jnp.dot((cb * decay_from(h1 + 1)).astype(bf16), x2, preferred_element_type=f32))
            y_parts.append(y_diag + y_off[:, pr * LANES:(pr + 1) * LANES] * head_cols(ea, h1)
                           + xs[:, sl] * dskip_ref[:, sl])
            xdte_parts.append(xp * head_cols(dte, h1))
        xdte_t = jnp.transpose(jnp.concatenate(xdte_parts, axis=1)).astype(bf16)
        cd_rows = jnp.concatenate([jnp.broadcast_to(cd[:, h:h + 1], (SSD_HEAD_DIM, D_STATE))
                                   for h in range(g * hg, (g + 1) * hg)], axis=0)
        state_ref[g * gw:(g + 1) * gw, :] = sg * cd_rows + jnp.dot(xdte_t, bg, preferred_element_type=f32)
    y = jnp.concatenate(y_parts, axis=1)
    mix_ref[:, 0:D_SSD] = _gated_norm(y, z_ref[...], nw_ref).astype(bf16)

    @pl.when(c == pl.num_programs(1) - 1)
    def _():
        ssm_ref[0] = state_ref[...].reshape(N_SSD_HEADS, SSD_HEAD_DIM, D_STATE)

    u = u_ref[...]
    tail = pool_tail_ref[...]
    pos = (c * BLK + lax.broadcasted_iota(jnp.int32, (BLK, 1), 0) + 1).astype(f32)
    for gi, w in enumerate(POOL_WINDOWS):
        sl = slice(gi * POOL_GROUP_DIM, (gi + 1) * POOL_GROUP_DIM)
        ug = u[:, sl]
        wsum = _dot_sel_left(pcur_ref[gi], ug, 2) + _dot_sel_left(phist_ref[gi], tail[:, sl], 2)
        pooled = wsum / jnp.minimum(pos, jnp.float32(w)) - ug
        po = jnp.dot(pooled.astype(bf16), pw_ref[gi], preferred_element_type=f32) * ps_ref[:, sl]
        mix_ref[:, D_SSD + gi * POOL_GROUP_DIM:D_SSD + (gi + 1) * POOL_GROUP_DIM] = po.astype(bf16)
    pool_tail_ref[...] = u_ref[BLK - HIST_ROWS:BLK, :]


def _prompt_mixer(z, xbc, dt, u, n_seq, seq_len, conv_w, conv_b, dt_bias, A_log, D_skip, ssd_norm_w, pool_w,
                  pool_scale):
    f32, bf16 = jnp.float32, jnp.bfloat16
    n = n_seq * seq_len
    n_blk = seq_len // BLK
    k = _mixer_constants()

    def row_blk(width):
        return pl.BlockSpec((BLK, width), lambda b, c: (b * n_blk + c, 0))

    def const(shape):
        return pl.BlockSpec(shape, lambda b, c: (0,) * len(shape))

    pad_h = (0, LANES - N_SSD_HEADS)
    return pl.pallas_call(
        _prompt_mixer_body,
        grid=(n_seq, n_blk),
        in_specs=[row_blk(D_SSD), row_blk(D_CONV), row_blk(LANES), row_blk(D_POOL),
                  const((CONV_WIDTH, D_CONV)), const((1, D_CONV)), const((1, LANES)), const((1, LANES)),
                  const((1, D_SSD)), const((1, D_SSD)),
                  const((N_POOL_GROUPS, POOL_GROUP_DIM, POOL_GROUP_DIM)), const((1, D_POOL)),
                  const((BLK, BLK)),
                  const((N_POOL_GROUPS, BLK, BLK)), const((N_POOL_GROUPS, BLK, HIST_ROWS))],
        out_specs=[pl.BlockSpec((BLK, D_MIX), lambda b, c: (b * n_blk + c, 0)),
                   pl.BlockSpec((1, N_SSD_HEADS, SSD_HEAD_DIM, D_STATE), lambda b, c: (b, 0, 0, 0))],
        out_shape=[jax.ShapeDtypeStruct((n, D_MIX), bf16),
                   jax.ShapeDtypeStruct((n_seq, N_SSD_HEADS, SSD_HEAD_DIM, D_STATE), f32)],
        scratch_shapes=[pltpu.VMEM((CONV_TAIL_ROWS + BLK, D_CONV), f32),
                        pltpu.VMEM((HIST_ROWS, D_POOL), f32),
                        pltpu.VMEM((D_SSD, D_STATE), f32)],
        compiler_params=pltpu.CompilerParams(dimension_semantics=("parallel", "arbitrary"),
                                             vmem_limit_bytes=VMEM_LIMIT_BYTES),
        name="prompt_mixer",
    )(z, xbc, dt, u, conv_w.astype(f32), conv_b.reshape(1, D_CONV).astype(f32),
      jnp.pad(dt_bias.astype(f32), pad_h).reshape(1, LANES), jnp.pad(A_log.astype(f32), pad_h).reshape(1, LANES),
      jnp.repeat(D_skip.astype(f32), SSD_HEAD_DIM).reshape(1, D_SSD), ssd_norm_w.reshape(1, D_SSD).astype(f32),
      pool_w.astype(bf16), pool_scale.reshape(1, D_POOL).astype(f32),
      k["causal"], k["pcur"], k["phist"])


SEQ_PER_BLK = BLK // DEC_SEQ


def _sample_constants():
    bf16 = jnp.bfloat16
    r = np.arange(BLK)
    sq, st = r // DEC_SEQ, r % DEC_SEQ
    same = sq[:, None] == sq[None, :]
    causal = same & (st[None, :] <= st[:, None])
    nk = CONV_WIDTH - 1
    shift = np.stack([same & (st[None, :] == st[:, None] + k - nk) for k in range(nk)])
    cs = np.arange(SEQ_PER_BLK * nk)
    stsel = np.stack([(cs[None, :] // nk == sq[:, None]) & (cs[None, :] % nk == st[:, None] + k)
                      for k in range(nk)])
    pcur = np.stack([causal & (st[:, None] - st[None, :] < w) for w in POOL_WINDOWS])
    hs = np.arange(SEQ_PER_BLK * POOL_HIST)
    phist = np.stack([(hs[None, :] // POOL_HIST == sq[:, None])
                      & (st[:, None] + POOL_HIST - hs[None, :] % POOL_HIST < w) for w in POOL_WINDOWS])
    as_bf = lambda a: jnp.asarray(a.astype(np.float32), bf16)
    return dict(same=as_bf(same), causal=as_bf(causal), shift=as_bf(shift), stsel=as_bf(stsel),
                pcur=as_bf(pcur), phist=as_bf(phist))


def _sample_mixer_body(z_ref, xbc_ref, dt_ref, u_ref, cst_ref, pst_ref, ssm_in_ref,
                       cw_ref, cb_ref, dtb_ref, alog_ref, dskip_ref, nw_ref, pw_ref, ps_ref,
                       causal_ref, same_ref, expand_ref, expand_t_ref, shift_ref, stsel_ref, pcur_ref, phist_ref,
                       mix_ref, ssm_out_ref,
                       ydiag_ref, ea_ref, yt_ref, cdh_ref, cdl_ref, xdte_t_ref, bm_ref, cm_ref, *, pos0):
    f32, bf16 = jnp.float32, jnp.bfloat16
    s = pl.program_id(1)
    gw = D_SSD // N_SSD_GROUPS

    @pl.when(s == 0)
    def _():
        xbc = xbc_ref[...]
        cst = cst_ref[...]
        acc = cb_ref[...] + cw_ref[CONV_WIDTH - 1:CONV_WIDTH, :] * xbc
        for k in range(CONV_WIDTH - 1):
            tap = _dot_sel_left(shift_ref[k], xbc, 3) + _dot_sel_left(stsel_ref[k], cst, 3)
            acc = acc + cw_ref[k:k + 1, :] * tap
        xbc_c = acc * jax.nn.sigmoid(acc)

        causal_bf = causal_ref[...]
        keep = causal_bf > 0
        xs, bm, cm, dt, a_cum, xdt = _ssd_intra(xbc_c, dt_ref[...], dtb_ref, alog_ref, causal_bf, expand_ref)
        a_tot = _dot_sel_left(same_ref[...], dt * (-jnp.exp(alog_ref[...])), 3)
        a_cum_t = jnp.transpose(a_cum)
        ea_ref[...] = _dot_sel_right(jnp.exp(a_cum), expand_ref[...], 2)
        dte_x = _dot_sel_right(jnp.exp(a_tot - a_cum), expand_ref[...], 2)
        cd_col = _dot_sel_left(expand_t_ref[...], jnp.exp(jnp.transpose(a_tot)), 2)
        cd_hi, cd_lo = _split2(cd_col)
        cdh_ref[...] = cd_hi
        cdl_ref[...] = cd_lo
        bm_ref[...] = bm.astype(bf16)
        cm_ref[...] = cm.astype(bf16)
        for g in range(N_SSD_GROUPS):
            cg = cm[:, g * D_STATE:(g + 1) * D_STATE].astype(bf16)
            bg = bm[:, g * D_STATE:(g + 1) * D_STATE].astype(bf16)
            cb = lax.dot_general(cg, bg, NT_DIMS, preferred_element_type=f32)
            y_diag = _ssd_diag_group(g, cb, a_cum, a_cum_t, keep, xdt)
            ydiag_ref[:, g * gw:(g + 1) * gw] = y_diag + xs[:, g * gw:(g + 1) * gw] * dskip_ref[:, g * gw:(g + 1) * gw]
            xdte_t_ref[g * gw:(g + 1) * gw, :] = jnp.transpose(
                xdt[:, g * gw:(g + 1) * gw] * dte_x[:, g * gw:(g + 1) * gw]).astype(bf16)
        yt_ref[...] = jnp.zeros(yt_ref.shape, f32)

        u = u_ref[...]
        pst = pst_ref[...]
        step = lax.broadcasted_iota(jnp.int32, (BLK, 1), 0) % DEC_SEQ
        pos = (step + (pos0 + 1)).astype(f32)
        for gi, w in enumerate(POOL_WINDOWS):
            sl = slice(gi * POOL_GROUP_DIM, (gi + 1) * POOL_GROUP_DIM)
            ug = u[:, sl]
            wsum = _dot_sel_left(pcur_ref[gi], ug, 2) + _dot_sel_left(phist_ref[gi], pst[:, sl], 2)
            pooled = wsum / jnp.minimum(pos, jnp.float32(w)) - ug
            po = jnp.dot(pooled.astype(bf16), pw_ref[gi], preferred_element_type=f32) * ps_ref[:, sl]
            mix_ref[:, D_SSD + gi * POOL_GROUP_DIM:D_SSD + (gi + 1) * POOL_GROUP_DIM] = po.astype(bf16)

    rows_of_s = lax.broadcasted_iota(jnp.int32, (BLK, LANES), 0) // DEC_SEQ == s
    cols_of_s = lax.broadcasted_iota(jnp.int32, (gw, BLK), 1) // DEC_SEQ == s
    pick_s = jnp.where(lax.broadcasted_iota(jnp.int32, (BLK, LANES), 0) == DEC_SEQ * s, 1.0, 0.0).astype(bf16)
    state = ssm_in_ref[0].reshape(D_SSD, D_STATE)
    for g in range(N_SSD_GROUPS):
        rs = slice(g * gw, (g + 1) * gw)
        sg = state[rs, :]
        cg = cm_ref[:, g * D_STATE:(g + 1) * D_STATE]
        bg = bm_ref[:, g * D_STATE:(g + 1) * D_STATE]
        yt = lax.dot_general(sg.astype(bf16), cg, NT_DIMS, preferred_element_type=f32)
        yt_ref[rs, :] += jnp.where(cols_of_s, yt, 0.0)
        cd = (jnp.dot(cdh_ref[rs, :], pick_s, preferred_element_type=f32)
              + jnp.dot(cdl_ref[rs, :], pick_s, preferred_element_type=f32))
        upd = jnp.dot(xdte_t_ref[rs, :], jnp.where(rows_of_s, bg, jnp.zeros_like(bg)),
                      preferred_element_type=f32)
        ssm_out_ref[0, g * (N_SSD_HEADS // N_SSD_GROUPS):(g + 1) * (N_SSD_HEADS // N_SSD_GROUPS)] = (
            sg * cd + upd).reshape(N_SSD_HEADS // N_SSD_GROUPS, SSD_HEAD_DIM, D_STATE)

    @pl.when(s == pl.num_programs(1) - 1)
    def _():
        y = ydiag_ref[...] + jnp.transpose(yt_ref[...]) * ea_ref[...]
        mix_ref[:, 0:D_SSD] = _gated_norm(y, z_ref[...], nw_ref).astype(bf16)


def _sample_mixer(z, xbc, dt, u, row0, n_seq, state_conv, state_ssm, state_pool, pos0,
                  conv_w, conv_b, dt_bias, A_log, D_skip, ssd_norm_w, pool_w, pool_scale):
    f32, bf16 = jnp.float32, jnp.bfloat16
    n_blk = n_seq // SEQ_PER_BLK
    blk0 = row0 // BLK
    nk = CONV_WIDTH - 1
    k = _mixer_constants()
    ks = _sample_constants()

    def row_blk(width):
        return pl.BlockSpec((BLK, width), lambda j, s: (blk0 + j, 0))

    def const(shape):
        return pl.BlockSpec(shape, lambda j, s: (0,) * len(shape))

    state_spec = pl.BlockSpec((1, N_SSD_HEADS, SSD_HEAD_DIM, D_STATE), lambda j, s: (j * SEQ_PER_BLK + s, 0, 0, 0))
    pad_h = (0, LANES - N_SSD_HEADS)
    return pl.pallas_call(
        functools.partial(_sample_mixer_body, pos0=pos0),
        grid=(n_blk, SEQ_PER_BLK),
        in_specs=[row_blk(D_SSD), row_blk(D_CONV), row_blk(LANES), row_blk(D_POOL),
                  pl.BlockSpec((SEQ_PER_BLK * nk, D_CONV), lambda j, s: (j, 0)),
                  pl.BlockSpec((SEQ_PER_BLK * POOL_HIST, D_POOL), lambda j, s: (j, 0)),
                  state_spec,
                  const((CONV_WIDTH, D_CONV)), const((1, D_CONV)), const((1, LANES)), const((1, LANES)),
                  const((1, D_SSD)), const((1, D_SSD)),
                  const((N_POOL_GROUPS, POOL_GROUP_DIM, POOL_GROUP_DIM)), const((1, D_POOL)),
                  const((BLK, BLK)), const((BLK, BLK)), const((LANES, D_SSD)), const((D_SSD, LANES)),
                  const((nk, BLK, BLK)), const((nk, BLK, SEQ_PER_BLK * nk)),
                  const((N_POOL_GROUPS, BLK, BLK)), const((N_POOL_GROUPS, BLK, SEQ_PER_BLK * POOL_HIST))],
        out_specs=[pl.BlockSpec((BLK, D_MIX), lambda j, s: (j, 0)), state_spec],
        out_shape=[jax.ShapeDtypeStruct((n_seq * DEC_SEQ, D_MIX), bf16),
                   jax.ShapeDtypeStruct((n_seq, N_SSD_HEADS, SSD_HEAD_DIM, D_STATE), f32)],
        scratch_shapes=[pltpu.VMEM((BLK, D_SSD), f32), pltpu.VMEM((BLK, D_SSD), f32),
                        pltpu.VMEM((D_SSD, BLK), f32), pltpu.VMEM((D_SSD, BLK), bf16),
                        pltpu.VMEM((D_SSD, BLK), bf16), pltpu.VMEM((D_SSD, BLK), bf16),
                        pltpu.VMEM((BLK, N_SSD_GROUPS * D_STATE), bf16),
                        pltpu.VMEM((BLK, N_SSD_GROUPS * D_STATE), bf16)],
        compiler_params=pltpu.CompilerParams(dimension_semantics=("parallel", "arbitrary"),
                                             vmem_limit_bytes=VMEM_LIMIT_BYTES),
        name="sample_mixer",
    )(z, xbc, dt, u, state_conv.reshape(n_seq * nk, D_CONV), state_pool.reshape(n_seq * POOL_HIST, D_POOL),
      state_ssm, conv_w.astype(f32), conv_b.reshape(1, D_CONV).astype(f32),
      jnp.pad(dt_bias.astype(f32), pad_h).reshape(1, LANES), jnp.pad(A_log.astype(f32), pad_h).reshape(1, LANES),
      jnp.repeat(D_skip.astype(f32), SSD_HEAD_DIM).reshape(1, D_SSD), ssd_norm_w.reshape(1, D_SSD).astype(f32),
      pool_w.astype(bf16), pool_scale.reshape(1, D_POOL).astype(f32),
      ks["causal"], ks["same"], k["expand"], k["expand_t"], ks["shift"], ks["stsel"],
      ks["pcur"], ks["phist"])


def _moe_sizes(n_tokens, tm):
    nt = n_tokens // tm
    lmax = -(-(TOP_K * tm + N_EXPERTS * (MOE_SEG_ROWS - 1)) // MOE_SEL_ROWS) * MOE_SEL_ROWS
    rows = (TOP_K * n_tokens + nt * N_EXPERTS * (MOE_SEG_ROWS - 1) + N_EXPERTS * (MOE_PIECE - 1)
            + MOE_ROW_CHUNK)
    n_rows = -(-rows // MOE_PIECE) * MOE_PIECE
    return nt, lmax, n_rows


def _router_body(x_ref, nw_ref, rwh_ref, rwl_ref, rb_ref, h_ref, posg_ref, post_ref, cnt_ref):
    f32, bf16 = jnp.float32, jnp.bfloat16
    tm = x_ref.shape[0]
    x = x_ref[...]
    h = x * lax.rsqrt(jnp.mean(x * x, axis=-1, keepdims=True) + EPS) * nw_ref[...]
    h_hi = h.astype(bf16)
    h_ref[...] = h_hi
    h_lo = (h - h_hi.astype(f32)).astype(bf16)
    wh = rwh_ref[...]
    logits = (jnp.dot(h_hi, wh, preferred_element_type=f32)
              + jnp.dot(h_lo, wh, preferred_element_type=f32)
              + jnp.dot(h_hi, rwl_ref[...], preferred_element_type=f32)) + rb_ref[...]
    lane = lax.broadcasted_iota(jnp.int32, (tm, LANES), 1)
    lanef = lane.astype(f32)
    neg = jnp.float32(-jnp.inf)
    l = jnp.where(lane < N_EXPERTS, logits, neg)
    sels, vals = [], []
    for _ in range(TOP_K):
        m = jnp.max(l, axis=1, keepdims=True)
        idx = jnp.min(jnp.where(l == m, lanef, jnp.float32(LANES)), axis=1, keepdims=True)
        sel = lanef == idx
        l = jnp.where(sel, neg, l)
        sels.append(sel)
        vals.append(m)
    exps = [jnp.exp(v - vals[0]) for v in vals]
    denom = exps[0] + exps[1] + exps[2] + exps[3]
    gates = [e / denom for e in exps]
    chosen = jnp.where(sels[0] | sels[1] | sels[2] | sels[3], 1.0, 0.0).astype(f32)
    row = lax.broadcasted_iota(jnp.int32, (tm, tm), 0)
    col = lax.broadcasted_iota(jnp.int32, (tm, tm), 1)
    lower = jnp.where(col < row, 1.0, 0.0).astype(bf16)
    rank = jnp.dot(lower, chosen.astype(bf16), preferred_element_type=f32)
    cnt = jnp.sum(chosen, axis=0, keepdims=True)
    seg_units = jnp.floor((cnt + (MOE_SEG_ROWS - 1)) * (1.0 / MOE_SEG_ROWS))
    r2 = lax.broadcasted_iota(jnp.int32, (LANES, LANES), 0)
    c2 = lax.broadcasted_iota(jnp.int32, (LANES, LANES), 1)
    upper = jnp.where(r2 < c2, 1.0, 0.0).astype(bf16)
    lstart = jnp.dot(jnp.broadcast_to(seg_units, (8, LANES)).astype(bf16), upper,
                     preferred_element_type=f32)[0:1, :] * MOE_SEG_ROWS
    posmat = lstart + rank
    posg = jnp.zeros((tm, LANES), f32)
    for k in range(TOP_K):
        pos_k = jnp.sum(jnp.where(sels[k], posmat, 0.0), axis=1, keepdims=True)
        posg = posg + jnp.where(lane == k, pos_k, 0.0) + jnp.where(lane == TOP_K + k, gates[k], 0.0)
    posg_ref[...] = posg
    post_ref[...] = jnp.transpose(posg)[0:8, :]
    cnt_ref[0] = jnp.broadcast_to(cnt, (8, LANES)).astype(jnp.int32)


def _moe_router(x1, norm2_w, router_w, router_b, tm):
    n, d = x1.shape
    nt = n // tm
    f32, bf16 = jnp.float32, jnp.bfloat16
    rw = jnp.pad(router_w.astype(f32), ((0, 0), (0, LANES - N_EXPERTS)))
    rw_hi = rw.astype(bf16)
    rw_lo = (rw - rw_hi.astype(f32)).astype(bf16)
    rb = jnp.pad(router_b.astype(f32), (0, LANES - N_EXPERTS)).reshape(1, LANES)
    return pl.pallas_call(
        _router_body,
        grid=(nt,),
        in_specs=[pl.BlockSpec((tm, d), lambda i: (i, 0)),
                  pl.BlockSpec((1, d), lambda i: (0, 0)),
                  pl.BlockSpec((d, LANES), lambda i: (0, 0)),
                  pl.BlockSpec((d, LANES), lambda i: (0, 0)),
                  pl.BlockSpec((1, LANES), lambda i: (0, 0))],
        out_specs=[pl.BlockSpec((tm, d), lambda i: (i, 0)),
                   pl.BlockSpec((tm, LANES), lambda i: (i, 0)),
                   pl.BlockSpec((8, tm), lambda i: (0, i)),
                   pl.BlockSpec((1, 8, LANES), lambda i: (i, 0, 0))],
        out_shape=[jax.ShapeDtypeStruct((n, d), bf16),
                   jax.ShapeDtypeStruct((n, LANES), f32),
                   jax.ShapeDtypeStruct((8, n), f32),
                   jax.ShapeDtypeStruct((nt, 8, LANES), jnp.int32)],
        compiler_params=pltpu.CompilerParams(dimension_semantics=("parallel",),
                                             vmem_limit_bytes=VMEM_LIMIT_BYTES),
        name="moe_router",
    )(x1, norm2_w.reshape(1, d).astype(f32), rw_hi, rw_lo, rb)


def _moe_plan(cnt):
    i32 = jnp.int32
    pad = (cnt + (MOE_SEG_ROWS - 1)) // MOE_SEG_ROWS * MOE_SEG_ROWS
    lstart = jnp.cumsum(pad, axis=1) - pad
    lp = jnp.sum(pad, axis=1)
    tot = jnp.sum(pad, axis=0)
    reg = (tot + (MOE_PIECE - 1)) // MOE_PIECE * MOE_PIECE
    reg_end = jnp.cumsum(reg)
    estart = reg_end - reg
    seg = estart[None, :] + jnp.cumsum(pad, axis=0) - pad
    return dict(
        lstart=lstart.reshape(-1).astype(i32), seg_units=(pad // MOE_SEG_ROWS).reshape(-1).astype(i32),
        seg=seg.reshape(-1).astype(i32), lp=lp.astype(i32),
        tail_start=(estart + tot).astype(i32), tail_units=((reg - tot) // MOE_SEG_ROWS).astype(i32),
        estart=estart.astype(i32), erows=reg.astype(i32), used=reg_end[-1].reshape(1).astype(i32))


def _for_each_segment_copy(i, lstart_ref, units_ref, seg_ref, local_ref, global_ref, sem, to_global, fn):
    def per_expert(e, carry):
        k = i * N_EXPERTS + e

        @pl.when(units_ref[k] > 0)
        def _():
            n = pl.multiple_of(units_ref[k] * MOE_SEG_ROWS, MOE_SEG_ROWS)
            loc = local_ref.at[pl.ds(pl.multiple_of(lstart_ref[k], MOE_SEG_ROWS), n)]
            glo = global_ref.at[pl.ds(pl.multiple_of(seg_ref[k], MOE_SEG_ROWS), n)]
            fn(pltpu.make_async_copy(loc, glo, sem) if to_global else pltpu.make_async_copy(glo, loc, sem))
        return carry
    lax.fori_loop(0, N_EXPERTS, per_expert, 0)


def _for_each_unused_piece(used_ref, zero_ref, rows_ref, sem, fn):
    def per_piece(j, c):
        go = pl.multiple_of(j * MOE_PIECE, MOE_PIECE)
        fn(pltpu.make_async_copy(zero_ref, rows_ref.at[pl.ds(go, MOE_PIECE)], sem))
        return c
    lax.fori_loop(used_ref[0] // MOE_PIECE, rows_ref.shape[0] // MOE_PIECE, per_piece, 0)


def _dispatch_body(lstart_ref, units_ref, seg_ref, lp_ref, tail_start_ref, tail_units_ref, used_ref,
                   h_ref, post_ref, xs_ref, stage_ref, sel_ref, zero_ref, sems, fill_sem):
    f32, bf16 = jnp.float32, jnp.bfloat16
    i = pl.program_id(0)
    nt = pl.num_programs(0)
    slot = i % 2
    tm = h_ref.shape[0]
    lmax = stage_ref.shape[1]

    def for_each_fill_copy(fn):
        def per_expert(e, carry):
            @pl.when(tail_units_ref[e] > 0)
            def _():
                n = pl.multiple_of(tail_units_ref[e] * MOE_SEG_ROWS, MOE_SEG_ROWS)
                go = pl.multiple_of(tail_start_ref[e], MOE_SEG_ROWS)
                fn(pltpu.make_async_copy(zero_ref.at[pl.ds(0, n)], xs_ref.at[pl.ds(go, n)], fill_sem))
            return carry
        lax.fori_loop(0, N_EXPERTS, per_expert, 0)
        _for_each_unused_piece(used_ref, zero_ref, xs_ref, fill_sem, fn)

    @pl.when(i == 0)
    def _():
        zero_ref[...] = jnp.zeros(zero_ref.shape, bf16)
        for_each_fill_copy(lambda cp: cp.start())
        for_each_fill_copy(lambda cp: cp.wait())

    def segment_copies(tile, slot_, fn):
        _for_each_segment_copy(tile, lstart_ref, units_ref, seg_ref, stage_ref.at[slot_], xs_ref,
                               sems.at[slot_], True, fn)

    @pl.when(i >= 2)
    def _():
        segment_copies(i - 2, slot, lambda cp: cp.wait())

    pos = [post_ref[k:k + 1, :] for k in range(TOP_K)]
    r_local = lax.broadcasted_iota(jnp.int32, (MOE_CHUNK, tm), 0).astype(f32).astype(bf16)
    one, zero = jnp.ones((), bf16), jnp.zeros((), bf16)
    for part in range(lmax // MOE_SEL_ROWS):
        @pl.when(part * MOE_SEL_ROWS < lp_ref[i])
        def _():
            for c in range(MOE_SEL_ROWS // MOE_CHUNK):
                r0 = part * MOE_SEL_ROWS + c * MOE_CHUNK
                loc = [jnp.clip(p - r0, -1.0, float(MOE_CHUNK)).astype(bf16) for p in pos]
                hit = (loc[0] == r_local) | (loc[1] == r_local) | (loc[2] == r_local) | (loc[3] == r_local)
                sel_ref[c * MOE_CHUNK:(c + 1) * MOE_CHUNK, :] = jnp.where(hit, one, zero)
            stage_ref[slot, part * MOE_SEL_ROWS:(part + 1) * MOE_SEL_ROWS, :] = jnp.dot(
                sel_ref[...], h_ref[...], preferred_element_type=f32).astype(bf16)

    segment_copies(i, slot, lambda cp: cp.start())

    @pl.when(i == nt - 1)
    def _():
        @pl.when(nt >= 2)
        def _():
            segment_copies(i - 1, 1 - slot, lambda cp: cp.wait())
        segment_copies(i, slot, lambda cp: cp.wait())


def _moe_dispatch(h2, post, plan, tm, lmax, n_rows):
    n, d = h2.shape
    nt = n // tm
    grid_spec = pltpu.PrefetchScalarGridSpec(
        num_scalar_prefetch=7,
        grid=(nt,),
        in_specs=[pl.BlockSpec((tm, d), lambda i, *_: (i, 0)),
                  pl.BlockSpec((8, tm), lambda i, *_: (0, i))],
        out_specs=pl.BlockSpec(memory_space=pl.ANY),
        scratch_shapes=[pltpu.VMEM((2, lmax, d), jnp.bfloat16),
                        pltpu.VMEM((MOE_SEL_ROWS, tm), jnp.bfloat16),
                        pltpu.VMEM((MOE_PIECE, d), jnp.bfloat16),
                        pltpu.SemaphoreType.DMA((2,)),
                        pltpu.SemaphoreType.DMA(())])
    return pl.pallas_call(
        _dispatch_body,
        grid_spec=grid_spec,
        out_shape=jax.ShapeDtypeStruct((n_rows, d), jnp.bfloat16),
        compiler_params=pltpu.CompilerParams(dimension_semantics=("arbitrary",),
                                             vmem_limit_bytes=VMEM_LIMIT_BYTES),
        name="moe_dispatch",
    )(plan["lstart"], plan["seg_units"], plan["seg"], plan["lp"], plan["tail_start"], plan["tail_units"],
      plan["used"], h2, post)


def _experts_body(first_ref, count_ref, cstart_ref, cvalid_ref, total_ref, used_ref,
                  xs_ref, wgu_ref, bgu_ref, wd_ref, bd_ref, os_ref,
                  wgu_f32, wd_f32, wgu_bf, wd_bf, xbuf, obuf, zero_ref, w_sems, in_sems, out_sems, fill_sem):
    f32, bf16 = jnp.float32, jnp.bfloat16
    e = pl.program_id(0)
    total = total_ref[0]
    half = D_FF // 2

    def weight_copies(ex):
        return (pltpu.make_async_copy(wgu_ref.at[ex], wgu_f32, w_sems.at[0]),
                pltpu.make_async_copy(wd_ref.at[ex], wd_f32, w_sems.at[1]))

    def in_copy(j):
        src = xs_ref.at[pl.ds(pl.multiple_of(cstart_ref[j], MOE_PIECE), MOE_ROW_CHUNK)]
        return pltpu.make_async_copy(src, xbuf.at[j % 2], in_sems.at[j % 2])

    def out_copy(j):
        n = pl.multiple_of(cvalid_ref[j], MOE_PIECE)
        go = pl.multiple_of(cstart_ref[j], MOE_PIECE)
        return pltpu.make_async_copy(obuf.at[j % 2, pl.ds(0, n)], os_ref.at[pl.ds(go, n)], out_sems.at[j % 2])

    @pl.when(e == 0)
    def _():
        for cp in weight_copies(0):
            cp.start()

        @pl.when(total > 0)
        def _():
            in_copy(0).start()
        zero_ref[...] = jnp.zeros(zero_ref.shape, bf16)
        _for_each_unused_piece(used_ref, zero_ref, os_ref, fill_sem, lambda cp: cp.start())
        _for_each_unused_piece(used_ref, zero_ref, os_ref, fill_sem, lambda cp: cp.wait())

    for cp in weight_copies(e):
        cp.wait()

    @pl.when(count_ref[e] > 0)
    def _():
        wgu_bf[...] = wgu_f32[...].astype(bf16)
        wd_bf[...] = wd_f32[...].astype(bf16)

    @pl.when(e + 1 < pl.num_programs(0))
    def _():
        for cp in weight_copies(e + 1):
            cp.start()

    @pl.when(count_ref[e] > 0)
    def _():
        def chunk(j, carry):
            in_copy(j).wait()

            @pl.when(j + 1 < total)
            def _():
                in_copy(j + 1).start()

            @pl.when(j >= 2)
            def _():
                out_copy(j - 2).wait()

            x = xbuf[j % 2]
            out = bd_ref[0]
            for hf in range(2):
                gate = jnp.dot(x, wgu_bf[:, hf * half:(hf + 1) * half], preferred_element_type=f32)
                gate = jnp.minimum(gate + bgu_ref[0, :, hf * half:(hf + 1) * half], SWIGLU_LIMIT)
                up = jnp.dot(x, wgu_bf[:, D_FF + hf * half:D_FF + (hf + 1) * half], preferred_element_type=f32)
                up = jnp.clip(up + bgu_ref[0, :, D_FF + hf * half:D_FF + (hf + 1) * half],
                              -SWIGLU_LIMIT, SWIGLU_LIMIT)
                act = (up + 1.0) * (gate * jax.nn.sigmoid(SWIGLU_ALPHA * gate))
                out = out + jnp.dot(act.astype(bf16), wd_bf[hf * half:(hf + 1) * half, :],
                                    preferred_element_type=f32)
            obuf[j % 2] = out.astype(bf16)
            out_copy(j).start()
            return carry
        lax.fori_loop(first_ref[e], first_ref[e] + count_ref[e], chunk, 0)

    @pl.when(e == pl.num_programs(0) - 1)
    def _():
        @pl.when(total >= 2)
        def _():
            out_copy(total - 2).wait()

        @pl.when(total >= 1)
        def _():
            out_copy(total - 1).wait()


def _expert_chunks(plan, n_rows):
    i32 = jnp.int32
    max_chunks = n_rows // MOE_ROW_CHUNK + N_EXPERTS
    count = (plan["erows"] + (MOE_ROW_CHUNK - 1)) // MOE_ROW_CHUNK
    end = jnp.cumsum(count)
    first = end - count
    j = jnp.arange(max_chunks, dtype=i32)
    mine = ((first[None, :] <= j[:, None]) & (j[:, None] < end[None, :])).astype(i32)
    c = j - jnp.sum(mine * first[None, :], axis=1)
    cstart = jnp.sum(mine * plan["estart"][None, :], axis=1) + jnp.sum(mine, axis=1) * c * MOE_ROW_CHUNK
    cvalid = jnp.sum(mine * jnp.clip(plan["erows"][None, :] - c[:, None] * MOE_ROW_CHUNK, 0, MOE_ROW_CHUNK),
                     axis=1)
    return (first.astype(i32), count.astype(i32), cstart.astype(i32), cvalid.astype(i32),
            end[-1].reshape(1).astype(i32))


def _moe_experts(xs, plan, w_gate_up, b_gate_up, w_down, b_down):
    d = xs.shape[1]
    grid_spec = pltpu.PrefetchScalarGridSpec(
        num_scalar_prefetch=6,
        grid=(N_EXPERTS,),
        in_specs=[pl.BlockSpec(memory_space=pl.ANY),
                  pl.BlockSpec(memory_space=pl.ANY),
                  pl.BlockSpec((1, 1, 2 * D_FF), lambda e, *_: (e, 0, 0)),
                  pl.BlockSpec(memory_space=pl.ANY),
                  pl.BlockSpec((1, 1, d), lambda e, *_: (e, 0, 0))],
        out_specs=pl.BlockSpec(memory_space=pl.ANY),
        scratch_shapes=[pltpu.VMEM((d, 2 * D_FF), jnp.float32),
                        pltpu.VMEM((D_FF, d), jnp.float32),
                        pltpu.VMEM((d, 2 * D_FF), jnp.bfloat16),
                        pltpu.VMEM((D_FF, d), jnp.bfloat16),
                        pltpu.VMEM((2, MOE_ROW_CHUNK, d), jnp.bfloat16),
                        pltpu.VMEM((2, MOE_ROW_CHUNK, d), jnp.bfloat16),
                        pltpu.VMEM((MOE_PIECE, d), jnp.bfloat16),
                        pltpu.SemaphoreType.DMA((2,)),
                        pltpu.SemaphoreType.DMA((2,)),
                        pltpu.SemaphoreType.DMA((2,)),
                        pltpu.SemaphoreType.DMA(())])
    return pl.pallas_call(
        _experts_body,
        grid_spec=grid_spec,
        out_shape=jax.ShapeDtypeStruct(xs.shape, jnp.bfloat16),
        compiler_params=pltpu.CompilerParams(dimension_semantics=("arbitrary",),
                                             vmem_limit_bytes=VMEM_LIMIT_BYTES),
        name="moe_experts",
    )(*_expert_chunks(plan, xs.shape[0]), plan["used"],
      xs, w_gate_up, b_gate_up.reshape(N_EXPERTS, 1, 2 * D_FF), w_down, b_down.reshape(N_EXPERTS, 1, d))


def _combine_body(lstart_ref, units_ref, seg_ref, lp_ref,
                  os_ref, posg_ref, x_ref, fw_ref, yp_ref, ys_ref, stage_ref, w_ref, sems,
                  *, n_prompt_tiles):
    f32, bf16 = jnp.float32, jnp.bfloat16
    i = pl.program_id(0)
    nt = pl.num_programs(0)
    slot = i % 2
    tm = x_ref.shape[0]
    lmax = stage_ref.shape[1]

    def segment_copies(tile, slot_, fn):
        _for_each_segment_copy(tile, lstart_ref, units_ref, seg_ref, stage_ref.at[slot_], os_ref,
                               sems.at[slot_], False, fn)

    @pl.when(i == 0)
    def _():
        stage_ref[...] = jnp.zeros(stage_ref.shape, bf16)
        segment_copies(0, 0, lambda cp: cp.start())

    @pl.when(i + 1 < nt)
    def _():
        segment_copies(i + 1, 1 - slot, lambda cp: cp.start())

    posg = posg_ref[...]
    pos = [posg[:, k:k + 1] for k in range(TOP_K)]
    gate = [posg[:, TOP_K + k:TOP_K + k + 1] for k in range(TOP_K)]
    gate_bf = [g.astype(bf16) for g in gate]
    r_local = lax.broadcasted_iota(jnp.int32, (tm, MOE_CHUNK), 1).astype(f32).astype(bf16)
    for c in range(lmax // MOE_CHUNK):
        w = jnp.zeros((tm, MOE_CHUNK), bf16)
        for k in range(TOP_K):
            loc = jnp.clip(pos[k] - c * MOE_CHUNK, -1.0, float(MOE_CHUNK)).astype(bf16)
            w = jnp.where(loc == r_local, gate_bf[k], w)
        w_ref[:, c * MOE_CHUNK:(c + 1) * MOE_CHUNK] = w

    segment_copies(i, slot, lambda cp: cp.wait())
    y = x_ref[...] + jnp.dot(w_ref[...], stage_ref[slot], preferred_element_type=f32)
    out = y * lax.rsqrt(jnp.mean(y * y, axis=-1, keepdims=True) + EPS) * fw_ref[...]

    @pl.when(i < n_prompt_tiles)
    def _():
        yp_ref[...] = out

    @pl.when(i >= n_prompt_tiles)
    def _():
        ys_ref[...] = out


def _moe_combine(os_, posg, x1, final_norm_w, plan, tm, lmax, n_prompt):
    n, d = x1.shape
    nt = n // tm
    n_prompt_tiles = n_prompt // tm
    n_sample_tiles = nt - n_prompt_tiles
    grid_spec = pltpu.PrefetchScalarGridSpec(
        num_scalar_prefetch=4,
        grid=(nt,),
        in_specs=[pl.BlockSpec(memory_space=pl.ANY),
                  pl.BlockSpec((tm, LANES), lambda i, *_: (i, 0)),
                  pl.BlockSpec((tm, d), lambda i, *_: (i, 0)),
                  pl.BlockSpec((1, d), lambda i, *_: (0, 0))],
        out_specs=[pl.BlockSpec((tm, d), lambda i, *_: (jnp.minimum(i, n_prompt_tiles - 1), 0)),
                   pl.BlockSpec((tm, d), lambda i, *_: (jnp.maximum(i - n_prompt_tiles, 0), 0))],
        scratch_shapes=[pltpu.VMEM((2, lmax, d), jnp.bfloat16),
                        pltpu.VMEM((tm, lmax), jnp.bfloat16),
                        pltpu.SemaphoreType.DMA((2,))])
    return pl.pallas_call(
        functools.partial(_combine_body, n_prompt_tiles=n_prompt_tiles),
        grid_spec=grid_spec,
        out_shape=[jax.ShapeDtypeStruct((n_prompt, d), jnp.float32),
                   jax.ShapeDtypeStruct((n_sample_tiles * tm, d), jnp.float32)],
        compiler_params=pltpu.CompilerParams(dimension_semantics=("arbitrary",),
                                             vmem_limit_bytes=VMEM_LIMIT_BYTES),
        name="moe_combine",
    )(plan["lstart"], plan["seg_units"], plan["seg"], plan["lp"],
      os_, posg, x1, final_norm_w.reshape(1, d).astype(jnp.float32))


def _moe_block(x1, n_prompt, norm2_w, router_w, router_b, w_gate_up, b_gate_up, w_down, b_down,
               final_norm_w, tm=MOE_TOKEN_TILE):
    n = x1.shape[0]
    nt, lmax, n_rows = _moe_sizes(n, tm)
    h2, posg, post, cnt3 = _moe_router(x1, norm2_w, router_w, router_b, tm)
    plan = _moe_plan(cnt3[:, 0, :N_EXPERTS])
    xs = _moe_dispatch(h2, post, plan, tm, lmax, n_rows)
    os_ = _moe_experts(xs, plan, w_gate_up, b_gate_up, w_down, b_down)
    return _moe_combine(os_, posg, x1, final_norm_w, plan, tm, lmax, n_prompt)


def kernel(x_prompt, x_sample, state_ssm, state_conv, state_pool, norm1_w, w_in, conv_w, conv_b, dt_bias,
           A_log, D_skip, ssd_norm_w, pool_w, pool_scale, w_out, norm2_w, router_w, router_b, w_gate_up,
           b_gate_up, w_down, b_down, final_norm_w):
    n_prompt = BATCH * SEQ
    n_sample = DEC_BATCH * DEC_SEQ
    xp = x_prompt.reshape(n_prompt, D_MODEL)
    xs = x_sample.reshape(n_sample, D_MODEL)
    z, xbc, dt_raw, u = _in_proj(xp, xs, norm1_w[0], w_in[0])
    mp = (conv_w[0], conv_b[0], dt_bias[0], A_log[0], D_skip[0], ssd_norm_w[0], pool_w[0], pool_scale[0])
    mix_p, s1 = _prompt_mixer(z, xbc, dt_raw, u, BATCH, SEQ, *mp)
    mix_s, s2 = _sample_mixer(z, xbc, dt_raw, u, n_prompt, DEC_BATCH, state_conv[0], state_ssm[0], state_pool[0],
                              PAST_LEN, *mp)
    nk = CONV_WIDTH - 1
    c1 = jnp.stack([xbc[(b + 1) * SEQ - nk:(b + 1) * SEQ] for b in range(BATCH)])
    p1 = jnp.stack([u[(b + 1) * SEQ - POOL_HIST:(b + 1) * SEQ] for b in range(BATCH)])
    c2 = xbc[n_prompt:].reshape(DEC_BATCH, DEC_SEQ, D_CONV)[:, DEC_SEQ - nk:]
    p2 = jnp.concatenate([state_pool[0][:, DEC_SEQ:], u[n_prompt:].reshape(DEC_BATCH, DEC_SEQ, D_POOL)], axis=1)
    x1 = _out_proj(mix_p, mix_s, w_out[0], xp, xs)
    yp, ys = _moe_block(x1, n_prompt, norm2_w[0], router_w[0], router_b[0], w_gate_up[0], b_gate_up[0],
                        w_down[0], b_down[0], final_norm_w)
    return (yp.reshape(x_prompt.shape), ys.reshape(x_sample.shape),
            s1[None], c1[None], p1[None], s2[None], c2[None], p2[None])
```

```python
import functools
import math
import jax, jax.numpy as jnp
from jax import lax
import numpy as np
from jax.experimental import pallas as pl
from jax.experimental.pallas import tpu as pltpu

D_MODEL = 1024
BATCH = 8
SEQ = 2048
DEC_BATCH = 128
DEC_SEQ = 4
PAST_LEN = 16384

D_MIX = 2 * D_MODEL
D_SSD = 3 * D_MIX // 4
SSD_HEAD_DIM = 64
N_SSD_HEADS = D_SSD // SSD_HEAD_DIM
N_SSD_GROUPS = 4
D_STATE = 128
CONV_WIDTH = 4
SSD_CHUNK = 128
D_CONV = D_SSD + 2 * N_SSD_GROUPS * D_STATE
D_POOL = D_MIX - D_SSD
POOL_WINDOWS = (2, 4, 8, 16)
N_POOL_GROUPS = len(POOL_WINDOWS)
POOL_GROUP_DIM = D_POOL // N_POOL_GROUPS
POOL_HIST = max(POOL_WINDOWS) - 1
D_IN_PROJ = D_SSD + D_CONV + N_SSD_HEADS + D_POOL
N_EXPERTS = 32
TOP_K = 4
D_FF = D_MODEL
SWIGLU_LIMIT = 7.0
SWIGLU_ALPHA = 1.702
EPS = 1e-5

LANES = 128
BF16_SUBLANES = 16
VMEM_LIMIT_BYTES = 48 * 1024 * 1024

MOE_TOKEN_TILE = 512
MOE_SEG_ROWS = BF16_SUBLANES
MOE_PIECE = 128
MOE_ROW_CHUNK = 512
MOE_CHUNK = 256
MOE_SEL_ROWS = 1280


BLK = SSD_CHUNK
PROJ_ROW_TILE = 512
HIST_ROWS = 16
CONV_TAIL_ROWS = 8
NT_DIMS = (((1,), (1,)), ((), ()))


def _split2(v):
    hi = v.astype(jnp.bfloat16)
    lo = (v - hi.astype(jnp.float32)).astype(jnp.bfloat16)
    return hi, lo


def _dot_sel_left(sel, v, passes):
    out = None
    rem = v
    for p in range(passes):
        part = rem.astype(jnp.bfloat16)
        d = jnp.dot(sel, part, preferred_element_type=jnp.float32)
        out = d if out is None else out + d
        if p + 1 < passes:
            rem = rem - part.astype(jnp.float32)
    return out


def _dot_sel_right(v, sel, passes):
    out = None
    rem = v
    for p in range(passes):
        part = rem.astype(jnp.bfloat16)
        d = jnp.dot(part, sel, preferred_element_type=jnp.float32)
        out = d if out is None else out + d
        if p + 1 < passes:
            rem = rem - part.astype(jnp.float32)
    return out


def _two_part_specs(n_first, n_second, tm, width):
    t1 = n_first // tm
    t2 = n_second // tm
    return (pl.BlockSpec((tm, width), lambda i: (jnp.minimum(i, t1 - 1), 0)),
            pl.BlockSpec((tm, width), lambda i: (jnp.clip(i - t1, 0, t2 - 1), 0)))


def _in_proj_body(xa_ref, xb_ref, nw_ref, w_ref, z_ref, xbc_ref, dt_ref, u_ref, *, tiles_a):
    x = jnp.where(pl.program_id(0) < tiles_a, xa_ref[...], xb_ref[...])
    h = (x * lax.rsqrt(jnp.mean(x * x, axis=-1, keepdims=True) + EPS) * nw_ref[...]).astype(jnp.bfloat16)
    off = 0
    for ref in (z_ref, xbc_ref, dt_ref, u_ref):
        n = ref.shape[1]
        ref[...] = jnp.dot(h, w_ref[:, off:off + n], preferred_element_type=jnp.float32)
        off += n


def _in_proj(xa, xb, norm1_w, w_in):
    d = xa.shape[1]
    n = xa.shape[0] + xb.shape[0]
    f32, bf16 = jnp.float32, jnp.bfloat16
    s1, s2 = D_SSD + D_CONV, D_SSD + D_CONV + N_SSD_HEADS
    w = jnp.concatenate([w_in[:, :s1], jnp.pad(w_in[:, s1:s2], ((0, 0), (0, LANES - N_SSD_HEADS))),
                         w_in[:, s2:]], axis=1).astype(bf16)
    widths = (D_SSD, D_CONV, LANES, D_POOL)
    tm = PROJ_ROW_TILE
    return pl.pallas_call(
        functools.partial(_in_proj_body, tiles_a=xa.shape[0] // tm),
        grid=(n // tm,),
        in_specs=[*_two_part_specs(xa.shape[0], xb.shape[0], tm, d),
                  pl.BlockSpec((1, d), lambda i: (0, 0)),
                  pl.BlockSpec((d, sum(widths)), lambda i: (0, 0), pipeline_mode=pl.Buffered(1))],
        out_specs=[pl.BlockSpec((tm, wd), lambda i: (i, 0)) for wd in widths],
        out_shape=[jax.ShapeDtypeStruct((n, wd), f32) for wd in widths],
        compiler_params=pltpu.CompilerParams(dimension_semantics=("parallel",),
                                             vmem_limit_bytes=VMEM_LIMIT_BYTES),
        name="in_proj",
    )(xa, xb, norm1_w.reshape(1, d).astype(f32), w)


def _out_proj_body(ma_ref, mb_ref, w_ref, xa_ref, xb_ref, o_ref, *, tiles_a):
    first = pl.program_id(0) < tiles_a
    m = jnp.where(first, ma_ref[...], mb_ref[...])
    x = jnp.where(first, xa_ref[...], xb_ref[...])
    o_ref[...] = x + jnp.dot(m, w_ref[...], preferred_element_type=jnp.float32)


def _out_proj(ma, mb, w_out, xa, xb):
    d = xa.shape[1]
    n = xa.shape[0] + xb.shape[0]
    tm = PROJ_ROW_TILE
    return pl.pallas_call(
        functools.partial(_out_proj_body, tiles_a=xa.shape[0] // tm),
        grid=(n // tm,),
        in_specs=[*_two_part_specs(xa.shape[0], xb.shape[0], tm, D_MIX),
                  pl.BlockSpec((D_MIX, d), lambda i: (0, 0)),
                  *_two_part_specs(xa.shape[0], xb.shape[0], tm, d)],
        out_specs=pl.BlockSpec((tm, d), lambda i: (i, 0)),
        out_shape=jax.ShapeDtypeStruct((n, d), jnp.float32),
        compiler_params=pltpu.CompilerParams(dimension_semantics=("parallel",),
                                             vmem_limit_bytes=VMEM_LIMIT_BYTES),
        name="out_proj",
    )(ma, mb, w_out.astype(jnp.bfloat16), xa, xb)


def _mixer_constants():
    bf16 = jnp.bfloat16
    h = np.arange(LANES)[:, None]
    ch = np.arange(D_SSD)[None, :]
    expand = (ch // SSD_HEAD_DIM == h).astype(np.float32)
    i = np.arange(BLK)[:, None]
    j = np.arange(BLK)[None, :]
    causal = (j <= i).astype(np.float32)
    jh = np.arange(HIST_ROWS)[None, :]
    pcur = np.stack([((j <= i) & (i - j < w)) for w in POOL_WINDOWS]).astype(np.float32)
    phist = np.stack([(i + HIST_ROWS - jh < w) for w in POOL_WINDOWS]).astype(np.float32)
    return dict(expand=jnp.asarray(expand, bf16), expand_t=jnp.asarray(expand.T, bf16),
                causal=jnp.asarray(causal, bf16), pcur=jnp.asarray(pcur, bf16),
                phist=jnp.asarray(phist, bf16))


def _softplus(x):
    return jnp.maximum(x, 0.0) + jnp.log(1.0 + jnp.exp(-jnp.abs(x)))


def _conv_silu(ext_ref, cw_ref, cb_ref, first_row):
    ext = ext_ref[...]
    last = first_row + CONV_WIDTH - 1
    acc = cb_ref[...] + cw_ref[CONV_WIDTH - 1:CONV_WIDTH, :] * ext[last:last + BLK, :]
    for k in range(CONV_WIDTH - 1):
        tap = pltpu.roll(ext, CONV_WIDTH - 1 - k, axis=0)[last:last + BLK, :]
        acc = acc + cw_ref[k:k + 1, :] * tap
    return acc * jax.nn.sigmoid(acc)


def _ssd_intra(xbc_c, dt_raw, dtb_ref, alog_ref, causal_bf, expand_ref):
    f32 = jnp.float32
    xs = xbc_c[:, :D_SSD]
    bm = xbc_c[:, D_SSD:D_SSD + N_SSD_GROUPS * D_STATE]
    cm = xbc_c[:, D_SSD + N_SSD_GROUPS * D_STATE:]
    dt = _softplus(dt_raw + dtb_ref[...])
    a = dt * (-jnp.exp(alog_ref[...]))
    a_cum = _dot_sel_left(causal_bf, a, 3)
    dt_x = _dot_sel_right(dt, expand_ref[...], 2)
    return xs, bm, cm, dt, a_cum, xs * dt_x


def _ssd_diag_group(g, cb, a_cum, a_cum_t, keep, xdt):
    f32, bf16 = jnp.float32, jnp.bfloat16
    hg = N_SSD_HEADS // N_SSD_GROUPS
    lane = lax.broadcasted_iota(jnp.int32, (BLK, LANES), 1)
    first_head = lane < SSD_HEAD_DIM
    neg = jnp.float32(-jnp.inf)
    outs = []
    for pr in range(hg * SSD_HEAD_DIM // LANES):
        h1 = g * hg + 2 * pr
        blk = (g * hg * SSD_HEAD_DIM) // LANES + pr
        xp = xdt[:, blk * LANES:(blk + 1) * LANES]
        x1 = jnp.where(first_head, xp, 0.0).astype(bf16)
        x2 = jnp.where(first_head, 0.0, xp).astype(bf16)
        m1 = (cb * jnp.exp(jnp.where(keep, a_cum[:, h1:h1 + 1] - a_cum_t[h1:h1 + 1, :], neg))).astype(bf16)
        m2 = (cb * jnp.exp(jnp.where(keep, a_cum[:, h1 + 1:h1 + 2] - a_cum_t[h1 + 1:h1 + 2, :], neg))).astype(bf16)
        outs.append(jnp.dot(m1, x1, preferred_element_type=f32) + jnp.dot(m2, x2, preferred_element_type=f32))
    return jnp.concatenate(outs, axis=1)


def _gated_norm(y, z, nw_ref):
    yg = y * (z * jax.nn.sigmoid(z))
    return yg * lax.rsqrt(jnp.mean(yg * yg, axis=-1, keepdims=True) + EPS) * nw_ref[...]


def _prompt_mixer_body(z_ref, xbc_ref, dt_ref, u_ref, cw_ref, cb_ref, dtb_ref, alog_ref, dskip_ref, nw_ref,
                       pw_ref, ps_ref, causal_ref, pcur_ref, phist_ref,
                       mix_ref, ssm_ref, ext_ref, pool_tail_ref, state_ref):
    f32, bf16 = jnp.float32, jnp.bfloat16
    c = pl.program_id(1)
    gw = D_SSD // N_SSD_GROUPS

    @pl.when(c == 0)
    def _():
        ext_ref[0:CONV_TAIL_ROWS, :] = jnp.zeros((CONV_TAIL_ROWS, D_CONV), f32)
        pool_tail_ref[...] = jnp.zeros(pool_tail_ref.shape, f32)
        state_ref[...] = jnp.zeros(state_ref.shape, f32)

    ext_ref[CONV_TAIL_ROWS:CONV_TAIL_ROWS + BLK, :] = xbc_ref[...]
    xbc_c = _conv_silu(ext_ref, cw_ref, cb_ref, CONV_TAIL_ROWS - (CONV_WIDTH - 1))
    ext_ref[0:CONV_TAIL_ROWS, :] = xbc_ref[BLK - CONV_TAIL_ROWS:BLK, :]

    causal_bf = causal_ref[...]
    keep = causal_bf > 0
    xs = xbc_c[:, :D_SSD]
    bm = xbc_c[:, D_SSD:D_SSD + N_SSD_GROUPS * D_STATE]
    cm = xbc_c[:, D_SSD + N_SSD_GROUPS * D_STATE:]
    dt = _softplus(dt_ref[...] + dtb_ref[...])
    a_cum = _dot_sel_left(causal_bf, dt * (-jnp.exp(alog_ref[...])), 3)
    a_cum_t = jnp.transpose(a_cum)
    a_tot = a_cum[BLK - 1:BLK, :]
    ea = jnp.exp(a_cum)
    dte = jnp.exp(a_tot - a_cum)
    cd = jnp.exp(a_tot)
    hg = N_SSD_HEADS // N_SSD_GROUPS
    first_head = lax.broadcasted_iota(jnp.int32, (BLK, LANES), 1) < SSD_HEAD_DIM
    neg = jnp.float32(-jnp.inf)

    def head_cols(v, h1):
        return jnp.where(first_head, v[:, h1:h1 + 1], v[:, h1 + 1:h1 + 2])

    def decay_from(h):
        return jnp.exp(jnp.where(keep, a_cum[:, h:h + 1] - a_cum_t[h:h + 1, :], neg))

    y_parts = []
    for g in range(N_SSD_GROUPS):
        cg = cm[:, g * D_STATE:(g + 1) * D_STATE].astype(bf16)
        bg = bm[:, g * D_STATE:(g + 1) * D_STATE].astype(bf16)
        cb = lax.dot_general(cg, bg, NT_DIMS, preferred_element_type=f32)
        sg = state_ref[g * gw:(g + 1) * gw, :]
        y_off = lax.dot_general(cg, sg.astype(bf16), NT_DIMS, preferred_element_type=f32)
        xdte_parts = []
        for pr in range(gw // LANES):
            h1 = g * hg + 2 * pr
            sl = slice(h1 * SSD_HEAD_DIM, h1 * SSD_HEAD_DIM + LANES)
            xp = xs[:, sl] * head_cols(dt, h1)
            x1 = jnp.where(first_head, xp, 0.0).astype(bf16)
            x2 = jnp.where(first_head, 0.0, xp).astype(bf16)
            y_diag = (jnp.dot((cb * decay_from(h1)).astype(bf16), x1, preferred_element_type=f32)
                      + jnp.dot((cb * decay_from(h1 + 1)).astype(bf16), x2, preferred_element_type=f32))
            y_parts.append(y_diag + y_off[:, pr * LANES:(pr + 1) * LANES] * head_cols(ea, h1)
                           + xs[:, sl] * dskip_ref[:, sl])
            xdte_parts.append(xp * head_cols(dte, h1))
        xdte_t = jnp.transpose(jnp.concatenate(xdte_parts, axis=1)).astype(bf16)
        cd_rows = jnp.concatenate([jnp.broadcast_to(cd[:, h:h + 1], (SSD_HEAD_DIM, D_STATE))
                                   for h in range(g * hg, (g + 1) * hg)], axis=0)
        state_ref[g * gw:(g + 1) * gw, :] = sg * cd_rows + jnp.dot(xdte_t, bg, preferred_element_type=f32)
    y = jnp.concatenate(y_parts, axis=1)
    mix_ref[:, 0:D_SSD] = _gated_norm(y, z_ref[...], nw_ref).astype(bf16)

    @pl.when(c == pl.num_programs(1) - 1)
    def _():
        ssm_ref[0] = state_ref[...].reshape(N_SSD_HEADS, SSD_HEAD_DIM, D_STATE)

    u = u_ref[...]
    tail = pool_tail_ref[...]
    pos = (c * BLK + lax.broadcasted_iota(jnp.int32, (BLK, 1), 0) + 1).astype(f32)
    for gi, w in enumerate(POOL_WINDOWS):
        sl = slice(gi * POOL_GROUP_DIM, (gi + 1) * POOL_GROUP_DIM)
        ug = u[:, sl]
        wsum = _dot_sel_left(pcur_ref[gi], ug, 2) + _dot_sel_left(phist_ref[gi], tail[:, sl], 2)
        pooled = wsum / jnp.minimum(pos, jnp.float32(w)) - ug
        po = jnp.dot(pooled.astype(bf16), pw_ref[gi], preferred_element_type=f32) * ps_ref[:, sl]
        mix_ref[:, D_SSD + gi * POOL_GROUP_DIM:D_SSD + (gi + 1) * POOL_GROUP_DIM] = po.astype(bf16)
    pool_tail_ref[...] = u_ref[BLK - HIST_ROWS:BLK, :]


def _prompt_mixer(z, xbc, dt, u, n_seq, seq_len, conv_w, conv_b, dt_bias, A_log, D_skip, ssd_norm_w, pool_w,
                  pool_scale):
    f32, bf16 = jnp.float32, jnp.bfloat16
    n = n_seq * seq_len
    n_blk = seq_len // BLK
    k = _mixer_constants()

    def row_blk(width):
        return pl.BlockSpec((BLK, width), lambda b, c: (b * n_blk + c, 0))

    def const(shape):
        return pl.BlockSpec(shape, lambda b, c: (0,) * len(shape))

    pad_h = (0, LANES - N_SSD_HEADS)
    return pl.pallas_call(
        _prompt_mixer_body,
        grid=(n_seq, n_blk),
        in_specs=[row_blk(D_SSD), row_blk(D_CONV), row_blk(LANES), row_blk(D_POOL),
                  const((CONV_WIDTH, D_CONV)), const((1, D_CONV)), const((1, LANES)), const((1, LANES)),
                  const((1, D_SSD)), const((1, D_SSD)),
                  const((N_POOL_GROUPS, POOL_GROUP_DIM, POOL_GROUP_DIM)), const((1, D_POOL)),
                  const((BLK, BLK)),
                  const((N_POOL_GROUPS, BLK, BLK)), const((N_POOL_GROUPS, BLK, HIST_ROWS))],
        out_specs=[pl.BlockSpec((BLK, D_MIX), lambda b, c: (b * n_blk + c, 0)),
                   pl.BlockSpec((1, N_SSD_HEADS, SSD_HEAD_DIM, D_STATE), lambda b, c: (b, 0, 0, 0))],
        out_shape=[jax.ShapeDtypeStruct((n, D_MIX), bf16),
                   jax.ShapeDtypeStruct((n_seq, N_SSD_HEADS, SSD_HEAD_DIM, D_STATE), f32)],
        scratch_shapes=[pltpu.VMEM((CONV_TAIL_ROWS + BLK, D_CONV), f32),
                        pltpu.VMEM((HIST_ROWS, D_POOL), f32),
                        pltpu.VMEM((D_SSD, D_STATE), f32)],
        compiler_params=pltpu.CompilerParams(dimension_semantics=("parallel", "arbitrary"),
                                             vmem_limit_bytes=VMEM_LIMIT_BYTES),
        name="prompt_mixer",
    )(z, xbc, dt, u, conv_w.astype(f32), conv_b.reshape(1, D_CONV).astype(f32),
      jnp.pad(dt_bias.astype(f32), pad_h).reshape(1, LANES), jnp.pad(A_log.astype(f32), pad_h).reshape(1, LANES),
      jnp.repeat(D_skip.astype(f32), SSD_HEAD_DIM).reshape(1, D_SSD), ssd_norm_w.reshape(1, D_SSD).astype(f32),
      pool_w.astype(bf16), pool_scale.reshape(1, D_POOL).astype(f32),
      k["causal"], k["pcur"], k["phist"])


SEQ_PER_BLK = BLK // DEC_SEQ
SEQ_PER_STEP = 4


def _sample_constants():
    bf16 = jnp.bfloat16
    r = np.arange(BLK)
    sq, st = r // DEC_SEQ, r % DEC_SEQ
    same = sq[:, None] == sq[None, :]
    causal = same & (st[None, :] <= st[:, None])
    nk = CONV_WIDTH - 1
    shift = np.stack([same & (st[None, :] == st[:, None] + k - nk) for k in range(nk)])
    cs = np.arange(SEQ_PER_BLK * nk)
    stsel = np.stack([(cs[None, :] // nk == sq[:, None]) & (cs[None, :] % nk == st[:, None] + k)
                      for k in range(nk)])
    pcur = np.stack([causal & (st[:, None] - st[None, :] < w) for w in POOL_WINDOWS])
    hs = np.arange(SEQ_PER_BLK * POOL_HIST)
    phist = np.stack([(hs[None, :] // POOL_HIST == sq[:, None])
                      & (st[:, None] + POOL_HIST - hs[None, :] % POOL_HIST < w) for w in POOL_WINDOWS])
    as_bf = lambda a: jnp.asarray(a.astype(np.float32), bf16)
    return dict(same=as_bf(same), causal=as_bf(causal), shift=as_bf(shift), stsel=as_bf(stsel),
                pcur=as_bf(pcur), phist=as_bf(phist))


def _sample_mixer_body(z_ref, xbc_ref, dt_ref, u_ref, cst_ref, pst_ref, ssm_in_ref,
                       cw_ref, cb_ref, dtb_ref, alog_ref, dskip_ref, nw_ref, pw_ref, ps_ref,
                       causal_ref, same_ref, expand_ref, expand_t_ref, shift_ref, stsel_ref, pcur_ref, phist_ref,
                       mix_ref, ssm_out_ref,
                       ydiag_ref, ea_ref, yt_ref, cdh_ref, cdl_ref, xdte_t_ref, bm_ref, cm_ref, *, pos0):
    f32, bf16 = jnp.float32, jnp.bfloat16
    s = pl.program_id(1)
    gw = D_SSD // N_SSD_GROUPS

    @pl.when(s == 0)
    def _():
        xbc = xbc_ref[...]
        cst = cst_ref[...]
        acc = cb_ref[...] + cw_ref[CONV_WIDTH - 1:CONV_WIDTH, :] * xbc
        for k in range(CONV_WIDTH - 1):
            tap = _dot_sel_left(shift_ref[k], xbc, 3) + _dot_sel_left(stsel_ref[k], cst, 3)
            acc = acc + cw_ref[k:k + 1, :] * tap
        xbc_c = acc * jax.nn.sigmoid(acc)

        causal_bf = causal_ref[...]
        keep = causal_bf > 0
        xs, bm, cm, dt, a_cum, xdt = _ssd_intra(xbc_c, dt_ref[...], dtb_ref, alog_ref, causal_bf, expand_ref)
        a_tot = _dot_sel_left(same_ref[...], dt * (-jnp.exp(alog_ref[...])), 3)
        a_cum_t = jnp.transpose(a_cum)
        ea_ref[...] = _dot_sel_right(jnp.exp(a_cum), expand_ref[...], 2)
        dte_x = _dot_sel_right(jnp.exp(a_tot - a_cum), expand_ref[...], 2)
        cd_col = _dot_sel_left(expand_t_ref[...], jnp.exp(jnp.transpose(a_tot)), 2)
        cd_hi, cd_lo = _split2(cd_col)
        cdh_ref[...] = cd_hi
        cdl_ref[...] = cd_lo
        bm_ref[...] = bm.astype(bf16)
        cm_ref[...] = cm.astype(bf16)
        for g in range(N_SSD_GROUPS):
            cg = cm[:, g * D_STATE:(g + 1) * D_STATE].astype(bf16)
            bg = bm[:, g * D_STATE:(g + 1) * D_STATE].astype(bf16)
            cb = lax.dot_general(cg, bg, NT_DIMS, preferred_element_type=f32)
            y_diag = _ssd_diag_group(g, cb, a_cum, a_cum_t, keep, xdt)
            ydiag_ref[:, g * gw:(g + 1) * gw] = y_diag + xs[:, g * gw:(g + 1) * gw] * dskip_ref[:, g * gw:(g + 1) * gw]
            xdte_t_ref[g * gw:(g + 1) * gw, :] = jnp.transpose(
                xdt[:, g * gw:(g + 1) * gw] * dte_x[:, g * gw:(g + 1) * gw]).astype(bf16)
        yt_ref[...] = jnp.zeros(yt_ref.shape, f32)

        u = u_ref[...]
        pst = pst_ref[...]
        step = lax.broadcasted_iota(jnp.int32, (BLK, 1), 0) % DEC_SEQ
        pos = (step + (pos0 + 1)).astype(f32)
        for gi, w in enumerate(POOL_WINDOWS):
            sl = slice(gi * POOL_GROUP_DIM, (gi + 1) * POOL_GROUP_DIM)
            ug = u[:, sl]
            wsum = _dot_sel_left(pcur_ref[gi], ug, 2) + _dot_sel_left(phist_ref[gi], pst[:, sl], 2)
            pooled = wsum / jnp.minimum(pos, jnp.float32(w)) - ug
            po = jnp.dot(pooled.astype(bf16), pw_ref[gi], preferred_element_type=f32) * ps_ref[:, sl]
            mix_ref[:, D_SSD + gi * POOL_GROUP_DIM:D_SSD + (gi + 1) * POOL_GROUP_DIM] = po.astype(bf16)

    hg = N_SSD_HEADS // N_SSD_GROUPS
    row_seq = lax.broadcasted_iota(jnp.int32, (BLK, LANES), 0) // DEC_SEQ
    col_seq = lax.broadcasted_iota(jnp.int32, (gw, BLK), 1) // DEC_SEQ
    row_idx = lax.broadcasted_iota(jnp.int32, (BLK, LANES), 0)
    for q in range(SEQ_PER_STEP):
        sq = s * SEQ_PER_STEP + q
        rows_of_s = row_seq == sq
        cols_of_s = col_seq == sq
        pick_s = jnp.where(row_idx == DEC_SEQ * sq, 1.0, 0.0).astype(bf16)
        state = ssm_in_ref[q].reshape(D_SSD, D_STATE)
        for g in range(N_SSD_GROUPS):
            rs = slice(g * gw, (g + 1) * gw)
            sg = state[rs, :]
            cg = cm_ref[:, g * D_STATE:(g + 1) * D_STATE]
            bg = bm_ref[:, g * D_STATE:(g + 1) * D_STATE]
            yt = lax.dot_general(sg.astype(bf16), cg, NT_DIMS, preferred_element_type=f32)
            yt_ref[rs, :] += jnp.where(cols_of_s, yt, 0.0)
            cd = (jnp.dot(cdh_ref[rs, :], pick_s, preferred_element_type=f32)
                  + jnp.dot(cdl_ref[rs, :], pick_s, preferred_element_type=f32))
            upd = jnp.dot(xdte_t_ref[rs, :], jnp.where(rows_of_s, bg, jnp.zeros_like(bg)),
                          preferred_element_type=f32)
            ssm_out_ref[q, g * hg:(g + 1) * hg] = (sg * cd + upd).reshape(hg, SSD_HEAD_DIM, D_STATE)

    @pl.when(s == pl.num_programs(1) - 1)
    def _():
        y = ydiag_ref[...] + jnp.transpose(yt_ref[...]) * ea_ref[...]
        mix_ref[:, 0:D_SSD] = _gated_norm(y, z_ref[...], nw_ref).astype(bf16)


def _sample_mixer(z, xbc, dt, u, row0, n_seq, state_conv, state_ssm, state_pool, pos0,
                  conv_w, conv_b, dt_bias, A_log, D_skip, ssd_norm_w, pool_w, pool_scale):
    f32, bf16 = jnp.float32, jnp.bfloat16
    n_blk = n_seq // SEQ_PER_BLK
    blk0 = row0 // BLK
    nk = CONV_WIDTH - 1
    k = _mixer_constants()
    ks = _sample_constants()

    def row_blk(width):
        return pl.BlockSpec((BLK, width), lambda j, s: (blk0 + j, 0))

    def const(shape):
        return pl.BlockSpec(shape, lambda j, s: (0,) * len(shape))

    steps = SEQ_PER_BLK // SEQ_PER_STEP
    state_spec = pl.BlockSpec((SEQ_PER_STEP, N_SSD_HEADS, SSD_HEAD_DIM, D_STATE),
                              lambda j, s: (j * steps + s, 0, 0, 0))
    pad_h = (0, LANES - N_SSD_HEADS)
    return pl.pallas_call(
        functools.partial(_sample_mixer_body, pos0=pos0),
        grid=(n_blk, steps),
        in_specs=[row_blk(D_SSD), row_blk(D_CONV), row_blk(LANES), row_blk(D_POOL),
                  pl.BlockSpec((SEQ_PER_BLK * nk, D_CONV), lambda j, s: (j, 0)),
                  pl.BlockSpec((SEQ_PER_BLK * POOL_HIST, D_POOL), lambda j, s: (j, 0)),
                  state_spec,
                  const((CONV_WIDTH, D_CONV)), const((1, D_CONV)), const((1, LANES)), const((1, LANES)),
                  const((1, D_SSD)), const((1, D_SSD)),
                  const((N_POOL_GROUPS, POOL_GROUP_DIM, POOL_GROUP_DIM)), const((1, D_POOL)),
                  const((BLK, BLK)), const((BLK, BLK)), const((LANES, D_SSD)), const((D_SSD, LANES)),
                  const((nk, BLK, BLK)), const((nk, BLK, SEQ_PER_BLK * nk)),
                  const((N_POOL_GROUPS, BLK, BLK)), const((N_POOL_GROUPS, BLK, SEQ_PER_BLK * POOL_HIST))],
        out_specs=[pl.BlockSpec((BLK, D_MIX), lambda j, s: (j, 0)), state_spec],
        out_shape=[jax.ShapeDtypeStruct((n_seq * DEC_SEQ, D_MIX), bf16),
                   jax.ShapeDtypeStruct((n_seq, N_SSD_HEADS, SSD_HEAD_DIM, D_STATE), f32)],
        scratch_shapes=[pltpu.VMEM((BLK, D_SSD), f32), pltpu.VMEM((BLK, D_SSD), f32),
                        pltpu.VMEM((D_SSD, BLK), f32), pltpu.VMEM((D_SSD, BLK), bf16),
                        pltpu.VMEM((D_SSD, BLK), bf16), pltpu.VMEM((D_SSD, BLK), bf16),
                        pltpu.VMEM((BLK, N_SSD_GROUPS * D_STATE), bf16),
                        pltpu.VMEM((BLK, N_SSD_GROUPS * D_STATE), bf16)],
        compiler_params=pltpu.CompilerParams(dimension_semantics=("parallel", "arbitrary"),
                                             vmem_limit_bytes=VMEM_LIMIT_BYTES),
        name="sample_mixer",
    )(z, xbc, dt, u, state_conv.reshape(n_seq * nk, D_CONV), state_pool.reshape(n_seq * POOL_HIST, D_POOL),
      state_ssm, conv_w.astype(f32), conv_b.reshape(1, D_CONV).astype(f32),
      jnp.pad(dt_bias.astype(f32), pad_h).reshape(1, LANES), jnp.pad(A_log.astype(f32), pad_h).reshape(1, LANES),
      jnp.repeat(D_skip.astype(f32), SSD_HEAD_DIM).reshape(1, D_SSD), ssd_norm_w.reshape(1, D_SSD).astype(f32),
      pool_w.astype(bf16), pool_scale.reshape(1, D_POOL).astype(f32),
      ks["causal"], ks["same"], k["expand"], k["expand_t"], ks["shift"], ks["stsel"],
      ks["pcur"], ks["phist"])


def _moe_sizes(n_tokens, tm):
    nt = n_tokens // tm
    lmax = -(-(TOP_K * tm + N_EXPERTS * (MOE_SEG_ROWS - 1)) // MOE_SEL_ROWS) * MOE_SEL_ROWS
    rows = (TOP_K * n_tokens + nt * N_EXPERTS * (MOE_SEG_ROWS - 1) + N_EXPERTS * (MOE_PIECE - 1)
            + MOE_ROW_CHUNK)
    n_rows = -(-rows // MOE_PIECE) * MOE_PIECE
    return nt, lmax, n_rows


def _router_body(x_ref, nw_ref, rwh_ref, rwl_ref, rb_ref, h_ref, posg_ref, post_ref, cnt_ref):
    f32, bf16 = jnp.float32, jnp.bfloat16
    tm = x_ref.shape[0]
    x = x_ref[...]
    h = x * lax.rsqrt(jnp.mean(x * x, axis=-1, keepdims=True) + EPS) * nw_ref[...]
    h_hi = h.astype(bf16)
    h_ref[...] = h_hi
    h_lo = (h - h_hi.astype(f32)).astype(bf16)
    wh = rwh_ref[...]
    logits = (jnp.dot(h_hi, wh, preferred_element_type=f32)
              + jnp.dot(h_lo, wh, preferred_element_type=f32)
              + jnp.dot(h_hi, rwl_ref[...], preferred_element_type=f32)) + rb_ref[...]
    lane = lax.broadcasted_iota(jnp.int32, (tm, LANES), 1)
    lanef = lane.astype(f32)
    neg = jnp.float32(-jnp.inf)
    l = jnp.where(lane < N_EXPERTS, logits, neg)
    sels, vals = [], []
    for _ in range(TOP_K):
        m = jnp.max(l, axis=1, keepdims=True)
        idx = jnp.min(jnp.where(l == m, lanef, jnp.float32(LANES)), axis=1, keepdims=True)
        sel = lanef == idx
        l = jnp.where(sel, neg, l)
        sels.append(sel)
        vals.append(m)
    exps = [jnp.exp(v - vals[0]) for v in vals]
    denom = exps[0] + exps[1] + exps[2] + exps[3]
    gates = [e / denom for e in exps]
    chosen = jnp.where(sels[0] | sels[1] | sels[2] | sels[3], 1.0, 0.0).astype(f32)
    row = lax.broadcasted_iota(jnp.int32, (tm, tm), 0)
    col = lax.broadcasted_iota(jnp.int32, (tm, tm), 1)
    lower = jnp.where(col < row, 1.0, 0.0).astype(bf16)
    rank = jnp.dot(lower, chosen.astype(bf16), preferred_element_type=f32)
    cnt = jnp.sum(chosen, axis=0, keepdims=True)
    seg_units = jnp.floor((cnt + (MOE_SEG_ROWS - 1)) * (1.0 / MOE_SEG_ROWS))
    r2 = lax.broadcasted_iota(jnp.int32, (LANES, LANES), 0)
    c2 = lax.broadcasted_iota(jnp.int32, (LANES, LANES), 1)
    upper = jnp.where(r2 < c2, 1.0, 0.0).astype(bf16)
    lstart = jnp.dot(jnp.broadcast_to(seg_units, (8, LANES)).astype(bf16), upper,
                     preferred_element_type=f32)[0:1, :] * MOE_SEG_ROWS
    posmat = lstart + rank
    posg = jnp.zeros((tm, LANES), f32)
    for k in range(TOP_K):
        pos_k = jnp.sum(jnp.where(sels[k], posmat, 0.0), axis=1, keepdims=True)
        posg = posg + jnp.where(lane == k, pos_k, 0.0) + jnp.where(lane == TOP_K + k, gates[k], 0.0)
    posg_ref[...] = posg
    post_ref[...] = jnp.transpose(posg)[0:8, :]
    cnt_ref[0] = jnp.broadcast_to(cnt, (8, LANES)).astype(jnp.int32)


def _moe_router(x1, norm2_w, router_w, router_b, tm):
    n, d = x1.shape
    nt = n // tm
    f32, bf16 = jnp.float32, jnp.bfloat16
    rw = jnp.pad(router_w.astype(f32), ((0, 0), (0, LANES - N_EXPERTS)))
    rw_hi = rw.astype(bf16)
    rw_lo = (rw - rw_hi.astype(f32)).astype(bf16)
    rb = jnp.pad(router_b.astype(f32), (0, LANES - N_EXPERTS)).reshape(1, LANES)
    return pl.pallas_call(
        _router_body,
        grid=(nt,),
        in_specs=[pl.BlockSpec((tm, d), lambda i: (i, 0)),
                  pl.BlockSpec((1, d), lambda i: (0, 0)),
                  pl.BlockSpec((d, LANES), lambda i: (0, 0)),
                  pl.BlockSpec((d, LANES), lambda i: (0, 0)),
                  pl.BlockSpec((1, LANES), lambda i: (0, 0))],
        out_specs=[pl.BlockSpec((tm, d), lambda i: (i, 0)),
                   pl.BlockSpec((tm, LANES), lambda i: (i, 0)),
                   pl.BlockSpec((8, tm), lambda i: (0, i)),
                   pl.BlockSpec((1, 8, LANES), lambda i: (i, 0, 0))],
        out_shape=[jax.ShapeDtypeStruct((n, d), bf16),
                   jax.ShapeDtypeStruct((n, LANES), f32),
                   jax.ShapeDtypeStruct((8, n), f32),
                   jax.ShapeDtypeStruct((nt, 8, LANES), jnp.int32)],
        compiler_params=pltpu.CompilerParams(dimension_semantics=("parallel",),
                                             vmem_limit_bytes=VMEM_LIMIT_BYTES),
        name="moe_router",
    )(x1, norm2_w.reshape(1, d).astype(f32), rw_hi, rw_lo, rb)


def _moe_plan(cnt):
    i32 = jnp.int32
    pad = (cnt + (MOE_SEG_ROWS - 1)) // MOE_SEG_ROWS * MOE_SEG_ROWS
    lstart = jnp.cumsum(pad, axis=1) - pad
    lp = jnp.sum(pad, axis=1)
    tot = jnp.sum(pad, axis=0)
    reg = (tot + (MOE_PIECE - 1)) // MOE_PIECE * MOE_PIECE
    reg_end = jnp.cumsum(reg)
    estart = reg_end - reg
    seg = estart[None, :] + jnp.cumsum(pad, axis=0) - pad
    return dict(
        lstart=lstart.reshape(-1).astype(i32), seg_units=(pad // MOE_SEG_ROWS).reshape(-1).astype(i32),
        seg=seg.reshape(-1).astype(i32), lp=lp.astype(i32),
        tail_start=(estart + tot).astype(i32), tail_units=((reg - tot) // MOE_SEG_ROWS).astype(i32),
        estart=estart.astype(i32), erows=reg.astype(i32), used=reg_end[-1].reshape(1).astype(i32))


def _for_each_segment_copy(i, lstart_ref, units_ref, seg_ref, local_ref, global_ref, sem, to_global, fn):
    def per_expert(e, carry):
        k = i * N_EXPERTS + e

        @pl.when(units_ref[k] > 0)
        def _():
            n = pl.multiple_of(units_ref[k] * MOE_SEG_ROWS, MOE_SEG_ROWS)
            loc = local_ref.at[pl.ds(pl.multiple_of(lstart_ref[k], MOE_SEG_ROWS), n)]
            glo = global_ref.at[pl.ds(pl.multiple_of(seg_ref[k], MOE_SEG_ROWS), n)]
            fn(pltpu.make_async_copy(loc, glo, sem) if to_global else pltpu.make_async_copy(glo, loc, sem))
        return carry
    lax.fori_loop(0, N_EXPERTS, per_expert, 0)


def _for_each_unused_piece(used_ref, zero_ref, rows_ref, sem, fn):
    def per_piece(j, c):
        go = pl.multiple_of(j * MOE_PIECE, MOE_PIECE)
        fn(pltpu.make_async_copy(zero_ref, rows_ref.at[pl.ds(go, MOE_PIECE)], sem))
        return c
    lax.fori_loop(used_ref[0] // MOE_PIECE, rows_ref.shape[0] // MOE_PIECE, per_piece, 0)


def _dispatch_body(lstart_ref, units_ref, seg_ref, lp_ref, tail_start_ref, tail_units_ref, used_ref,
                   h_ref, post_ref, xs_ref, stage_ref, sel_ref, zero_ref, sems, fill_sem):
    f32, bf16 = jnp.float32, jnp.bfloat16
    i = pl.program_id(0)
    nt = pl.num_programs(0)
    slot = i % 2
    tm = h_ref.shape[0]
    lmax = stage_ref.shape[1]

    def for_each_fill_copy(fn):
        def per_expert(e, carry):
            @pl.when(tail_units_ref[e] > 0)
            def _():
                n = pl.multiple_of(tail_units_ref[e] * MOE_SEG_ROWS, MOE_SEG_ROWS)
                go = pl.multiple_of(tail_start_ref[e], MOE_SEG_ROWS)
                fn(pltpu.make_async_copy(zero_ref.at[pl.ds(0, n)], xs_ref.at[pl.ds(go, n)], fill_sem))
            return carry
        lax.fori_loop(0, N_EXPERTS, per_expert, 0)
        _for_each_unused_piece(used_ref, zero_ref, xs_ref, fill_sem, fn)

    @pl.when(i == 0)
    def _():
        zero_ref[...] = jnp.zeros(zero_ref.shape, bf16)
        for_each_fill_copy(lambda cp: cp.start())
        for_each_fill_copy(lambda cp: cp.wait())

    def segment_copies(tile, slot_, fn):
        _for_each_segment_copy(tile, lstart_ref, units_ref, seg_ref, stage_ref.at[slot_], xs_ref,
                               sems.at[slot_], True, fn)

    @pl.when(i >= 2)
    def _():
        segment_copies(i - 2, slot, lambda cp: cp.wait())

    pos = [post_ref[k:k + 1, :] for k in range(TOP_K)]
    r_local = lax.broadcasted_iota(jnp.int32, (MOE_CHUNK, tm), 0).astype(f32).astype(bf16)
    one, zero = jnp.ones((), bf16), jnp.zeros((), bf16)
    for part in range(lmax // MOE_SEL_ROWS):
        @pl.when(part * MOE_SEL_ROWS < lp_ref[i])
        def _():
            for c in range(MOE_SEL_ROWS // MOE_CHUNK):
                r0 = part * MOE_SEL_ROWS + c * MOE_CHUNK
                loc = [jnp.clip(p - r0, -1.0, float(MOE_CHUNK)).astype(bf16) for p in pos]
                hit = (loc[0] == r_local) | (loc[1] == r_local) | (loc[2] == r_local) | (loc[3] == r_local)
                sel_ref[c * MOE_CHUNK:(c + 1) * MOE_CHUNK, :] = jnp.where(hit, one, zero)
            stage_ref[slot, part * MOE_SEL_ROWS:(part + 1) * MOE_SEL_ROWS, :] = jnp.dot(
                sel_ref[...], h_ref[...], preferred_element_type=f32).astype(bf16)

    segment_copies(i, slot, lambda cp: cp.start())

    @pl.when(i == nt - 1)
    def _():
        @pl.when(nt >= 2)
        def _():
            segment_copies(i - 1, 1 - slot, lambda cp: cp.wait())
        segment_copies(i, slot, lambda cp: cp.wait())


def _moe_dispatch(h2, post, plan, tm, lmax, n_rows):
    n, d = h2.shape
    nt = n // tm
    grid_spec = pltpu.PrefetchScalarGridSpec(
        num_scalar_prefetch=7,
        grid=(nt,),
        in_specs=[pl.BlockSpec((tm, d), lambda i, *_: (i, 0)),
                  pl.BlockSpec((8, tm), lambda i, *_: (0, i))],
        out_specs=pl.BlockSpec(memory_space=pl.ANY),
        scratch_shapes=[pltpu.VMEM((2, lmax, d), jnp.bfloat16),
                        pltpu.VMEM((MOE_SEL_ROWS, tm), jnp.bfloat16),
                        pltpu.VMEM((MOE_PIECE, d), jnp.bfloat16),
                        pltpu.SemaphoreType.DMA((2,)),
                        pltpu.SemaphoreType.DMA(())])
    return pl.pallas_call(
        _dispatch_body,
        grid_spec=grid_spec,
        out_shape=jax.ShapeDtypeStruct((n_rows, d), jnp.bfloat16),
        compiler_params=pltpu.CompilerParams(dimension_semantics=("arbitrary",),
                                             vmem_limit_bytes=VMEM_LIMIT_BYTES),
        name="moe_dispatch",
    )(plan["lstart"], plan["seg_units"], plan["seg"], plan["lp"], plan["tail_start"], plan["tail_units"],
      plan["used"], h2, post)


def _experts_body(first_ref, count_ref, cstart_ref, cvalid_ref, total_ref, used_ref,
                  xs_ref, wgu_ref, bgu_ref, wd_ref, bd_ref, os_ref,
                  wgu_bf, wd_bf, xbuf, obuf, zero_ref, in_sems, out_sems, fill_sem):
    f32, bf16 = jnp.float32, jnp.bfloat16
    e = pl.program_id(0)
    total = total_ref[0]
    half = D_FF // 2

    def in_copy(j):
        src = xs_ref.at[pl.ds(pl.multiple_of(cstart_ref[j], MOE_PIECE), MOE_ROW_CHUNK)]
        return pltpu.make_async_copy(src, xbuf.at[j % 3], in_sems.at[j % 3])

    def out_copy(j):
        n = pl.multiple_of(cvalid_ref[j], MOE_PIECE)
        go = pl.multiple_of(cstart_ref[j], MOE_PIECE)
        return pltpu.make_async_copy(obuf.at[j % 2, pl.ds(0, n)], os_ref.at[pl.ds(go, n)], out_sems.at[j % 2])

    @pl.when(e == 0)
    def _():
        for j0 in range(2):
            @pl.when(j0 < total)
            def _():
                in_copy(j0).start()
        zero_ref[...] = jnp.zeros(zero_ref.shape, bf16)
        _for_each_unused_piece(used_ref, zero_ref, os_ref, fill_sem, lambda cp: cp.start())
        _for_each_unused_piece(used_ref, zero_ref, os_ref, fill_sem, lambda cp: cp.wait())

    @pl.when(count_ref[e] > 0)
    def _():
        wgu_bf[...] = wgu_ref[0].astype(bf16)
        wd_bf[...] = wd_ref[0].astype(bf16)

        def chunk(j, carry):
            in_copy(j).wait()

            @pl.when(j + 2 < total)
            def _():
                in_copy(j + 2).start()

            @pl.when(j >= 2)
            def _():
                out_copy(j - 2).wait()

            def mlp(n_rows):
                x = xbuf[j % 3, 0:n_rows, :]
                out = bd_ref[0]
                for hf in range(2):
                    gate = jnp.dot(x, wgu_bf[:, hf * half:(hf + 1) * half], preferred_element_type=f32)
                    gate = jnp.minimum(gate + bgu_ref[0, :, hf * half:(hf + 1) * half], SWIGLU_LIMIT)
                    up = jnp.dot(x, wgu_bf[:, D_FF + hf * half:D_FF + (hf + 1) * half],
                                 preferred_element_type=f32)
                    up = jnp.clip(up + bgu_ref[0, :, D_FF + hf * half:D_FF + (hf + 1) * half],
                                  -SWIGLU_LIMIT, SWIGLU_LIMIT)
                    act = (up + 1.0) * (gate * jax.nn.sigmoid(SWIGLU_ALPHA * gate))
                    out = out + jnp.dot(act.astype(bf16), wd_bf[hf * half:(hf + 1) * half, :],
                                        preferred_element_type=f32)
                obuf[j % 2, 0:n_rows, :] = out.astype(bf16)

            @pl.when(cvalid_ref[j] > MOE_ROW_CHUNK // 2)
            def _():
                mlp(MOE_ROW_CHUNK)

            @pl.when(cvalid_ref[j] <= MOE_ROW_CHUNK // 2)
            def _():
                mlp(MOE_ROW_CHUNK // 2)
            out_copy(j).start()
            return carry
        lax.fori_loop(first_ref[e], first_ref[e] + count_ref[e], chunk, 0)

    @pl.when(e == pl.num_programs(0) - 1)
    def _():
        @pl.when(total >= 2)
        def _():
            out_copy(total - 2).wait()

        @pl.when(total >= 1)
        def _():
            out_copy(total - 1).wait()


def _expert_chunks(plan, n_rows):
    i32 = jnp.int32
    max_chunks = n_rows // MOE_ROW_CHUNK + N_EXPERTS
    count = (plan["erows"] + (MOE_ROW_CHUNK - 1)) // MOE_ROW_CHUNK
    end = jnp.cumsum(count)
    first = end - count
    j = jnp.arange(max_chunks, dtype=i32)
    mine = ((first[None, :] <= j[:, None]) & (j[:, None] < end[None, :])).astype(i32)
    c = j - jnp.sum(mine * first[None, :], axis=1)
    cstart = jnp.sum(mine * plan["estart"][None, :], axis=1) + jnp.sum(mine, axis=1) * c * MOE_ROW_CHUNK
    cvalid = jnp.sum(mine * jnp.clip(plan["erows"][None, :] - c[:, None] * MOE_ROW_CHUNK, 0, MOE_ROW_CHUNK),
                     axis=1)
    return (first.astype(i32), count.astype(i32), cstart.astype(i32), cvalid.astype(i32),
            end[-1].reshape(1).astype(i32))


def _moe_experts(xs, plan, w_gate_up, b_gate_up, w_down, b_down):
    d = xs.shape[1]
    grid_spec = pltpu.PrefetchScalarGridSpec(
        num_scalar_prefetch=6,
        grid=(N_EXPERTS,),
        in_specs=[pl.BlockSpec(memory_space=pl.ANY),
                  pl.BlockSpec((1, d, 2 * D_FF), lambda e, *_: (e, 0, 0)),
                  pl.BlockSpec((1, 1, 2 * D_FF), lambda e, *_: (e, 0, 0)),
                  pl.BlockSpec((1, D_FF, d), lambda e, *_: (e, 0, 0)),
                  pl.BlockSpec((1, 1, d), lambda e, *_: (e, 0, 0))],
        out_specs=pl.BlockSpec(memory_space=pl.ANY),
        scratch_shapes=[pltpu.VMEM((d, 2 * D_FF), jnp.bfloat16),
                        pltpu.VMEM((D_FF, d), jnp.bfloat16),
                        pltpu.VMEM((3, MOE_ROW_CHUNK, d), jnp.bfloat16),
                        pltpu.VMEM((2, MOE_ROW_CHUNK, d), jnp.bfloat16),
                        pltpu.VMEM((MOE_PIECE, d), jnp.bfloat16),
                        pltpu.SemaphoreType.DMA((3,)),
                        pltpu.SemaphoreType.DMA((2,)),
                        pltpu.SemaphoreType.DMA(())])
    return pl.pallas_call(
        _experts_body,
        grid_spec=grid_spec,
        out_shape=jax.ShapeDtypeStruct(xs.shape, jnp.bfloat16),
        compiler_params=pltpu.CompilerParams(dimension_semantics=("arbitrary",),
                                             vmem_limit_bytes=VMEM_LIMIT_BYTES),
        name="moe_experts",
    )(*_expert_chunks(plan, xs.shape[0]), plan["used"],
      xs, w_gate_up, b_gate_up.reshape(N_EXPERTS, 1, 2 * D_FF), w_down, b_down.reshape(N_EXPERTS, 1, d))


def _combine_body(lstart_ref, units_ref, seg_ref, lp_ref,
                  os_ref, posg_ref, x_ref, fw_ref, yp_ref, ys_ref, stage_ref, w_ref, sems,
                  *, n_prompt_tiles):
    f32, bf16 = jnp.float32, jnp.bfloat16
    i = pl.program_id(0)
    nt = pl.num_programs(0)
    slot = i % 2
    tm = x_ref.shape[0]
    lmax = stage_ref.shape[1]

    def segment_copies(tile, slot_, fn):
        _for_each_segment_copy(tile, lstart_ref, units_ref, seg_ref, stage_ref.at[slot_], os_ref,
                               sems.at[slot_], False, fn)

    @pl.when(i == 0)
    def _():
        stage_ref[...] = jnp.zeros(stage_ref.shape, bf16)
        segment_copies(0, 0, lambda cp: cp.start())

    @pl.when(i + 1 < nt)
    def _():
        segment_copies(i + 1, 1 - slot, lambda cp: cp.start())

    posg = posg_ref[...]
    pos = [posg[:, k:k + 1] for k in range(TOP_K)]
    gate = [posg[:, TOP_K + k:TOP_K + k + 1] for k in range(TOP_K)]
    gate_bf = [g.astype(bf16) for g in gate]
    r_local = lax.broadcasted_iota(jnp.int32, (tm, MOE_CHUNK), 1).astype(f32).astype(bf16)
    for c in range(lmax // MOE_CHUNK):
        w = jnp.zeros((tm, MOE_CHUNK), bf16)
        for k in range(TOP_K):
            loc = jnp.clip(pos[k] - c * MOE_CHUNK, -1.0, float(MOE_CHUNK)).astype(bf16)
            w = jnp.where(loc == r_local, gate_bf[k], w)
        w_ref[:, c * MOE_CHUNK:(c + 1) * MOE_CHUNK] = w

    segment_copies(i, slot, lambda cp: cp.wait())
    y = x_ref[...] + jnp.dot(w_ref[...], stage_ref[slot], preferred_element_type=f32)
    out = y * lax.rsqrt(jnp.mean(y * y, axis=-1, keepdims=True) + EPS) * fw_ref[...]

    @pl.when(i < n_prompt_tiles)
    def _():
        yp_ref[...] = out

    @pl.when(i >= n_prompt_tiles)
    def _():
        ys_ref[...] = out


def _moe_combine(os_, posg, x1, final_norm_w, plan, tm, lmax, n_prompt):
    n, d = x1.shape
    nt = n // tm
    n_prompt_tiles = n_prompt // tm
    n_sample_tiles = nt - n_prompt_tiles
    grid_spec = pltpu.PrefetchScalarGridSpec(
        num_scalar_prefetch=4,
        grid=(nt,),
        in_specs=[pl.BlockSpec(memory_space=pl.ANY),
                  pl.BlockSpec((tm, LANES), lambda i, *_: (i, 0)),
                  pl.BlockSpec((tm, d), lambda i, *_: (i, 0)),
                  pl.BlockSpec((1, d), lambda i, *_: (0, 0))],
        out_specs=[pl.BlockSpec((tm, d), lambda i, *_: (jnp.minimum(i, n_prompt_tiles - 1), 0)),
                   pl.BlockSpec((tm, d), lambda i, *_: (jnp.maximum(i - n_prompt_tiles, 0), 0))],
        scratch_shapes=[pltpu.VMEM((2, lmax, d), jnp.bfloat16),
                        pltpu.VMEM((tm, lmax), jnp.bfloat16),
                        pltpu.SemaphoreType.DMA((2,))])
    return pl.pallas_call(
        functools.partial(_combine_body, n_prompt_tiles=n_prompt_tiles),
        grid_spec=grid_spec,
        out_shape=[jax.ShapeDtypeStruct((n_prompt, d), jnp.float32),
                   jax.ShapeDtypeStruct((n_sample_tiles * tm, d), jnp.float32)],
        compiler_params=pltpu.CompilerParams(dimension_semantics=("arbitrary",),
                                             vmem_limit_bytes=VMEM_LIMIT_BYTES),
        name="moe_combine",
    )(plan["lstart"], plan["seg_units"], plan["seg"], plan["lp"],
      os_, posg, x1, final_norm_w.reshape(1, d).astype(jnp.float32))


def _moe_block(x1, n_prompt, norm2_w, router_w, router_b, w_gate_up, b_gate_up, w_down, b_down,
               final_norm_w, tm=MOE_TOKEN_TILE):
    n = x1.shape[0]
    nt, lmax, n_rows = _moe_sizes(n, tm)
    h2, posg, post, cnt3 = _moe_router(x1, norm2_w, router_w, router_b, tm)
    plan = _moe_plan(cnt3[:, 0, :N_EXPERTS])
    xs = _moe_dispatch(h2, post, plan, tm, lmax, n_rows)
    os_ = _moe_experts(xs, plan, w_gate_up, b_gate_up, w_down, b_down)
    return _moe_combine(os_, posg, x1, final_norm_w, plan, tm, lmax, n_prompt)


def kernel(x_prompt, x_sample, state_ssm, state_conv, state_pool, norm1_w, w_in, conv_w, conv_b, dt_bias,
           A_log, D_skip, ssd_norm_w, pool_w, pool_scale, w_out, norm2_w, router_w, router_b, w_gate_up,
           b_gate_up, w_down, b_down, final_norm_w):
    n_prompt = BATCH * SEQ
    n_sample = DEC_BATCH * DEC_SEQ
    xp = x_prompt.reshape(n_prompt, D_MODEL)
    xs = x_sample.reshape(n_sample, D_MODEL)
    z, xbc, dt_raw, u = _in_proj(xp, xs, norm1_w[0], w_in[0])
    mp = (conv_w[0], conv_b[0], dt_bias[0], A_log[0], D_skip[0], ssd_norm_w[0], pool_w[0], pool_scale[0])
    mix_p, s1 = _prompt_mixer(z, xbc, dt_raw, u, BATCH, SEQ, *mp)
    mix_s, s2 = _sample_mixer(z, xbc, dt_raw, u, n_prompt, DEC_BATCH, state_conv[0], state_ssm[0], state_pool[0],
                              PAST_LEN, *mp)
    nk = CONV_WIDTH - 1
    c1 = jnp.stack([xbc[(b + 1) * SEQ - nk:(b + 1) * SEQ] for b in range(BATCH)])
    p1 = jnp.stack([u[(b + 1) * SEQ - POOL_HIST:(b + 1) * SEQ] for b in range(BATCH)])
    c2 = xbc[n_prompt:].reshape(DEC_BATCH, DEC_SEQ, D_CONV)[:, DEC_SEQ - nk:]
    p2 = jnp.concatenate([state_pool[0][:, DEC_SEQ:], u[n_prompt:].reshape(DEC_BATCH, DEC_SEQ, D_POOL)], axis=1)
    x1 = _out_proj(mix_p, mix_s, w_out[0], xp, xs)
    yp, ys = _moe_block(x1, n_prompt, norm2_w[0], router_w[0], router_b[0], w_gate_up[0], b_gate_up[0],
                        w_down[0], b_down[0], final_norm_w)
    return (yp.reshape(x_prompt.shape), ys.reshape(x_sample.shape),
            s1[None], c1[None], p1[None], s2[None], c2[None], p2[None])
```

```python
import functools
import math
import jax, jax.numpy as jnp
from jax import lax
import numpy as np
from jax.experimental import pallas as pl
from jax.experimental.pallas import tpu as pltpu

D_MODEL = 1024
BATCH = 8
SEQ = 2048
DEC_BATCH = 128
DEC_SEQ = 4
PAST_LEN = 16384

D_MIX = 2 * D_MODEL
D_SSD = 3 * D_MIX // 4
SSD_HEAD_DIM = 64
N_SSD_HEADS = D_SSD // SSD_HEAD_DIM
N_SSD_GROUPS = 4
D_STATE = 128
CONV_WIDTH = 4
SSD_CHUNK = 128
D_CONV = D_SSD + 2 * N_SSD_GROUPS * D_STATE
D_POOL = D_MIX - D_SSD
POOL_WINDOWS = (2, 4, 8, 16)
N_POOL_GROUPS = len(POOL_WINDOWS)
POOL_GROUP_DIM = D_POOL // N_POOL_GROUPS
POOL_HIST = max(POOL_WINDOWS) - 1
D_IN_PROJ = D_SSD + D_CONV + N_SSD_HEADS + D_POOL
N_EXPERTS = 32
TOP_K = 4
D_FF = D_MODEL
SWIGLU_LIMIT = 7.0
SWIGLU_ALPHA = 1.702
EPS = 1e-5

LANES = 128
BF16_SUBLANES = 16
VMEM_LIMIT_BYTES = 48 * 1024 * 1024

MOE_TOKEN_TILE = 512
MOE_SEG_ROWS = BF16_SUBLANES
MOE_PIECE = 128
MOE_ROW_CHUNK = 512
MOE_CHUNK = 256
MOE_SEL_ROWS = 1280


BLK = SSD_CHUNK
PROJ_ROW_TILE = 512
HIST_ROWS = 16
CONV_TAIL_ROWS = 8
NT_DIMS = (((1,), (1,)), ((), ()))


def _split2(v):
    hi = v.astype(jnp.bfloat16)
    lo = (v - hi.astype(jnp.float32)).astype(jnp.bfloat16)
    return hi, lo


def _dot_sel_left(sel, v, passes):
    out = None
    rem = v
    for p in range(passes):
        part = rem.astype(jnp.bfloat16)
        d = jnp.dot(sel, part, preferred_element_type=jnp.float32)
        out = d if out is None else out + d
        if p + 1 < passes:
            rem = rem - part.astype(jnp.float32)
    return out


def _dot_sel_right(v, sel, passes):
    out = None
    rem = v
    for p in range(passes):
        part = rem.astype(jnp.bfloat16)
        d = jnp.dot(part, sel, preferred_element_type=jnp.float32)
        out = d if out is None else out + d
        if p + 1 < passes:
            rem = rem - part.astype(jnp.float32)
    return out


def _two_part_specs(n_first, n_second, tm, width):
    t1 = n_first // tm
    t2 = n_second // tm
    return (pl.BlockSpec((tm, width), lambda i: (jnp.minimum(i, t1 - 1), 0)),
            pl.BlockSpec((tm, width), lambda i: (jnp.clip(i - t1, 0, t2 - 1), 0)))


def _in_proj_body(xa_ref, xb_ref, nw_ref, w_ref, z_ref, xbc_ref, dt_ref, u_ref, *, tiles_a):
    x = jnp.where(pl.program_id(0) < tiles_a, xa_ref[...], xb_ref[...])
    h = (x * lax.rsqrt(jnp.mean(x * x, axis=-1, keepdims=True) + EPS) * nw_ref[...]).astype(jnp.bfloat16)
    off = 0
    for ref in (z_ref, xbc_ref, dt_ref, u_ref):
        n = ref.shape[1]
        ref[...] = jnp.dot(h, w_ref[:, off:off + n], preferred_element_type=jnp.float32)
        off += n


def _in_proj(xa, xb, norm1_w, w_in):
    d = xa.shape[1]
    n = xa.shape[0] + xb.shape[0]
    f32, bf16 = jnp.float32, jnp.bfloat16
    s1, s2 = D_SSD + D_CONV, D_SSD + D_CONV + N_SSD_HEADS
    w = jnp.concatenate([w_in[:, :s1], jnp.pad(w_in[:, s1:s2], ((0, 0), (0, LANES - N_SSD_HEADS))),
                         w_in[:, s2:]], axis=1).astype(bf16)
    widths = (D_SSD, D_CONV, LANES, D_POOL)
    tm = PROJ_ROW_TILE
    return pl.pallas_call(
        functools.partial(_in_proj_body, tiles_a=xa.shape[0] // tm),
        grid=(n // tm,),
        in_specs=[*_two_part_specs(xa.shape[0], xb.shape[0], tm, d),
                  pl.BlockSpec((1, d), lambda i: (0, 0)),
                  pl.BlockSpec((d, sum(widths)), lambda i: (0, 0), pipeline_mode=pl.Buffered(1))],
        out_specs=[pl.BlockSpec((tm, wd), lambda i: (i, 0)) for wd in widths],
        out_shape=[jax.ShapeDtypeStruct((n, wd), f32) for wd in widths],
        compiler_params=pltpu.CompilerParams(dimension_semantics=("parallel",),
                                             vmem_limit_bytes=VMEM_LIMIT_BYTES),
        name="in_proj",
    )(xa, xb, norm1_w.reshape(1, d).astype(f32), w)


def _out_proj_body(ma_ref, mb_ref, w_ref, xa_ref, xb_ref, o_ref, *, tiles_a):
    first = pl.program_id(0) < tiles_a
    m = jnp.where(first, ma_ref[...], mb_ref[...])
    x = jnp.where(first, xa_ref[...], xb_ref[...])
    o_ref[...] = x + jnp.dot(m, w_ref[...], preferred_element_type=jnp.float32)


def _out_proj(ma, mb, w_out, xa, xb):
    d = xa.shape[1]
    n = xa.shape[0] + xb.shape[0]
    tm = PROJ_ROW_TILE
    return pl.pallas_call(
        functools.partial(_out_proj_body, tiles_a=xa.shape[0] // tm),
        grid=(n // tm,),
        in_specs=[*_two_part_specs(xa.shape[0], xb.shape[0], tm, D_MIX),
                  pl.BlockSpec((D_MIX, d), lambda i: (0, 0)),
                  *_two_part_specs(xa.shape[0], xb.shape[0], tm, d)],
        out_specs=pl.BlockSpec((tm, d), lambda i: (i, 0)),
        out_shape=jax.ShapeDtypeStruct((n, d), jnp.float32),
        compiler_params=pltpu.CompilerParams(dimension_semantics=("parallel",),
                                             vmem_limit_bytes=VMEM_LIMIT_BYTES),
        name="out_proj",
    )(ma, mb, w_out.astype(jnp.bfloat16), xa, xb)


def _mixer_constants():
    bf16 = jnp.bfloat16
    h = np.arange(LANES)[:, None]
    ch = np.arange(D_SSD)[None, :]
    expand = (ch // SSD_HEAD_DIM == h).astype(np.float32)
    i = np.arange(BLK)[:, None]
    j = np.arange(BLK)[None, :]
    causal = (j <= i).astype(np.float32)
    return dict(expand=jnp.asarray(expand, bf16), expand_t=jnp.asarray(expand.T, bf16),
                causal=jnp.asarray(causal, bf16))


def _softplus(x):
    return jnp.maximum(x, 0.0) + jnp.log(1.0 + jnp.exp(-jnp.abs(x)))


def _conv_silu(ext_ref, cw_ref, cb_ref, first_row):
    ext = ext_ref[...]
    last = first_row + CONV_WIDTH - 1
    acc = cb_ref[...] + cw_ref[CONV_WIDTH - 1:CONV_WIDTH, :] * ext[last:last + BLK, :]
    for k in range(CONV_WIDTH - 1):
        tap = pltpu.roll(ext, CONV_WIDTH - 1 - k, axis=0)[last:last + BLK, :]
        acc = acc + cw_ref[k:k + 1, :] * tap
    return acc * jax.nn.sigmoid(acc)


def _ssd_intra(xbc_c, dt_raw, dtb_ref, alog_ref, causal_bf, expand_ref):
    f32 = jnp.float32
    xs = xbc_c[:, :D_SSD]
    bm = xbc_c[:, D_SSD:D_SSD + N_SSD_GROUPS * D_STATE]
    cm = xbc_c[:, D_SSD + N_SSD_GROUPS * D_STATE:]
    dt = _softplus(dt_raw + dtb_ref[...])
    a = dt * (-jnp.exp(alog_ref[...]))
    a_cum = _dot_sel_left(causal_bf, a, 3)
    dt_x = _dot_sel_right(dt, expand_ref[...], 2)
    return xs, bm, cm, dt, a_cum, xs * dt_x


def _ssd_diag_group(g, cb, a_cum, a_cum_t, keep, xdt):
    f32, bf16 = jnp.float32, jnp.bfloat16
    hg = N_SSD_HEADS // N_SSD_GROUPS
    lane = lax.broadcasted_iota(jnp.int32, (BLK, LANES), 1)
    first_head = lane < SSD_HEAD_DIM
    neg = jnp.float32(-jnp.inf)
    outs = []
    for pr in range(hg * SSD_HEAD_DIM // LANES):
        h1 = g * hg + 2 * pr
        blk = (g * hg * SSD_HEAD_DIM) // LANES + pr
        xp = xdt[:, blk * LANES:(blk + 1) * LANES]
        x1 = jnp.where(first_head, xp, 0.0).astype(bf16)
        x2 = jnp.where(first_head, 0.0, xp).astype(bf16)
        m1 = (cb * jnp.exp(jnp.where(keep, a_cum[:, h1:h1 + 1] - a_cum_t[h1:h1 + 1, :], neg))).astype(bf16)
        m2 = (cb * jnp.exp(jnp.where(keep, a_cum[:, h1 + 1:h1 + 2] - a_cum_t[h1 + 1:h1 + 2, :], neg))).astype(bf16)
        outs.append(jnp.dot(m1, x1, preferred_element_type=f32) + jnp.dot(m2, x2, preferred_element_type=f32))
    return jnp.concatenate(outs, axis=1)


def _gated_norm(y, z, nw_ref):
    yg = y * (z * jax.nn.sigmoid(z))
    return yg * lax.rsqrt(jnp.mean(yg * yg, axis=-1, keepdims=True) + EPS) * nw_ref[...]


def _prompt_mixer_body(z_ref, xbc_ref, dt_ref, u_ref, cw_ref, cb_ref, dtb_ref, alog_ref, dskip_ref, nw_ref,
                       pw_ref, ps_ref, causal_ref,
                       mix_ref, ssm_ref, ext_ref, pool_tail_ref, state_ref):
    f32, bf16 = jnp.float32, jnp.bfloat16
    c = pl.program_id(1)
    gw = D_SSD // N_SSD_GROUPS

    @pl.when(c == 0)
    def _():
        ext_ref[0:CONV_TAIL_ROWS, :] = jnp.zeros((CONV_TAIL_ROWS, D_CONV), f32)
        pool_tail_ref[...] = jnp.zeros(pool_tail_ref.shape, f32)
        state_ref[...] = jnp.zeros(state_ref.shape, f32)

    u = u_ref[...]
    ext_u = jnp.concatenate([pool_tail_ref[...], u], axis=0)
    pos = (c * BLK + lax.broadcasted_iota(jnp.int32, (BLK, 1), 0) + 1).astype(f32)
    for gi, w in enumerate(POOL_WINDOWS):
        assert w & (w - 1) == 0 and w <= HIST_ROWS
        sl = slice(gi * POOL_GROUP_DIM, (gi + 1) * POOL_GROUP_DIM)
        ug = u[:, sl]
        acc = ext_u[:, sl]
        span = 1
        while span < w:
            acc = acc + pltpu.roll(acc, span, axis=0)
            span *= 2
        wsum = acc[HIST_ROWS:, :]
        pooled = wsum / jnp.minimum(pos, jnp.float32(w)) - ug
        po = jnp.dot(pooled.astype(bf16), pw_ref[gi], preferred_element_type=f32) * ps_ref[:, sl]
        mix_ref[:, D_SSD + gi * POOL_GROUP_DIM:D_SSD + (gi + 1) * POOL_GROUP_DIM] = po.astype(bf16)
    pool_tail_ref[...] = u_ref[BLK - HIST_ROWS:BLK, :]

    ext_ref[CONV_TAIL_ROWS:CONV_TAIL_ROWS + BLK, :] = xbc_ref[...]
    xbc_c = _conv_silu(ext_ref, cw_ref, cb_ref, CONV_TAIL_ROWS - (CONV_WIDTH - 1))
    ext_ref[0:CONV_TAIL_ROWS, :] = xbc_ref[BLK - CONV_TAIL_ROWS:BLK, :]

    causal_bf = causal_ref[...]
    keep = causal_bf > 0
    xs = xbc_c[:, :D_SSD]
    bm = xbc_c[:, D_SSD:D_SSD + N_SSD_GROUPS * D_STATE]
    cm = xbc_c[:, D_SSD + N_SSD_GROUPS * D_STATE:]
    dt = _softplus(dt_ref[...] + dtb_ref[...])
    a_cum = dt * (-jnp.exp(alog_ref[...]))
    row = lax.broadcasted_iota(jnp.int32, (BLK, LANES), 0)
    span = 1
    while span < BLK:
        a_cum = a_cum + jnp.where(row >= span, pltpu.roll(a_cum, span, axis=0), 0.0)
        span *= 2
    a_cum_t = jnp.transpose(a_cum)
    a_tot = a_cum[BLK - 1:BLK, :]
    ea = jnp.exp(a_cum)
    dte = jnp.exp(a_tot - a_cum)
    cd = jnp.exp(a_tot)
    hg = N_SSD_HEADS // N_SSD_GROUPS
    first_head = lax.broadcasted_iota(jnp.int32, (BLK, LANES), 1) < SSD_HEAD_DIM
    neg = jnp.float32(-jnp.inf)

    def head_cols(v, h1):
        return jnp.where(first_head, v[:, h1:h1 + 1], v[:, h1 + 1:h1 + 2])

    def decay_from(h):
        return jnp.exp(jnp.where(keep, a_cum[:, h:h + 1] - a_cum_t[h:h + 1, :], neg))

    y_parts = []
    for g in range(N_SSD_GROUPS):
        cg = cm[:, g * D_STATE:(g + 1) * D_STATE].astype(bf16)
        bg = bm[:, g * D_STATE:(g + 1) * D_STATE].astype(bf16)
        cb = lax.dot_general(cg, bg, NT_DIMS, preferred_element_type=f32)
        sg = state_ref[g * gw:(g + 1) * gw, :]
        y_off = lax.dot_general(cg, sg.astype(bf16), NT_DIMS, preferred_element_type=f32)
        xdte_parts = []
        for pr in range(gw // LANES):
            h1 = g * hg + 2 * pr
            sl = slice(h1 * SSD_HEAD_DIM, h1 * SSD_HEAD_DIM + LANES)
            xp = xs[:, sl] * head_cols(dt, h1)
            x1 = jnp.where(first_head, xp, 0.0).astype(bf16)
            x2 = jnp.where(first_head, 0.0, xp).astype(bf16)
            y_diag = (jnp.dot((cb * decay_from(h1)).astype(bf16), x1, preferred_element_type=f32)
                      + jnp.dot((cb * decay_from(h1 + 1)).astype(bf16), x2, preferred_element_type=f32))
            y_parts.append(y_diag + y_off[:, pr * LANES:(pr + 1) * LANES] * head_cols(ea, h1)
                           + xs[:, sl] * dskip_ref[:, sl])
            xdte_parts.append(xp * head_cols(dte, h1))
        xdte_t = jnp.transpose(jnp.concatenate(xdte_parts, axis=1)).astype(bf16)
        cd_rows = jnp.concatenate([jnp.broadcast_to(cd[:, h:h + 1], (SSD_HEAD_DIM, D_STATE))
                                   for h in range(g * hg, (g + 1) * hg)], axis=0)
        state_ref[g * gw:(g + 1) * gw, :] = sg * cd_rows + jnp.dot(xdte_t, bg, preferred_element_type=f32)
    y = jnp.concatenate(y_parts, axis=1)
    mix_ref[:, 0:D_SSD] = _gated_norm(y, z_ref[...], nw_ref).astype(bf16)

    @pl.when(c == pl.num_programs(1) - 1)
    def _():
        ssm_ref[0] = state_ref[...].reshape(N_SSD_HEADS, SSD_HEAD_DIM, D_STATE)


def _prompt_mixer(z, xbc, dt, u, n_seq, seq_len, conv_w, conv_b, dt_bias, A_log, D_skip, ssd_norm_w, pool_w,
                  pool_scale):
    f32, bf16 = jnp.float32, jnp.bfloat16
    n = n_seq * seq_len
    n_blk = seq_len // BLK
    k = _mixer_constants()

    def row_blk(width):
        return pl.BlockSpec((BLK, width), lambda b, c: (b * n_blk + c, 0))

    def const(shape):
        return pl.BlockSpec(shape, lambda b, c: (0,) * len(shape))

    pad_h = (0, LANES - N_SSD_HEADS)
    return pl.pallas_call(
        _prompt_mixer_body,
        grid=(n_seq, n_blk),
        in_specs=[row_blk(D_SSD), row_blk(D_CONV), row_blk(LANES), row_blk(D_POOL),
                  const((CONV_WIDTH, D_CONV)), const((1, D_CONV)), const((1, LANES)), const((1, LANES)),
                  const((1, D_SSD)), const((1, D_SSD)),
                  const((N_POOL_GROUPS, POOL_GROUP_DIM, POOL_GROUP_DIM)), const((1, D_POOL)),
                  const((BLK, BLK))],
        out_specs=[pl.BlockSpec((BLK, D_MIX), lambda b, c: (b * n_blk + c, 0)),
                   pl.BlockSpec((1, N_SSD_HEADS, SSD_HEAD_DIM, D_STATE), lambda b, c: (b, 0, 0, 0))],
        out_shape=[jax.ShapeDtypeStruct((n, D_MIX), bf16),
                   jax.ShapeDtypeStruct((n_seq, N_SSD_HEADS, SSD_HEAD_DIM, D_STATE), f32)],
        scratch_shapes=[pltpu.VMEM((CONV_TAIL_ROWS + BLK, D_CONV), f32),
                        pltpu.VMEM((HIST_ROWS, D_POOL), f32),
                        pltpu.VMEM((D_SSD, D_STATE), f32)],
        compiler_params=pltpu.CompilerParams(dimension_semantics=("parallel", "arbitrary"),
                                             vmem_limit_bytes=VMEM_LIMIT_BYTES),
        name="prompt_mixer",
    )(z, xbc, dt, u, conv_w.astype(f32), conv_b.reshape(1, D_CONV).astype(f32),
      jnp.pad(dt_bias.astype(f32), pad_h).reshape(1, LANES), jnp.pad(A_log.astype(f32), pad_h).reshape(1, LANES),
      jnp.repeat(D_skip.astype(f32), SSD_HEAD_DIM).reshape(1, D_SSD), ssd_norm_w.reshape(1, D_SSD).astype(f32),
      pool_w.astype(bf16), pool_scale.reshape(1, D_POOL).astype(f32),
      k["causal"])


SEQ_PER_BLK = BLK // DEC_SEQ
SEQ_PER_STEP = 8


def _sample_constants():
    bf16 = jnp.bfloat16
    r = np.arange(BLK)
    sq, st = r // DEC_SEQ, r % DEC_SEQ
    same = sq[:, None] == sq[None, :]
    causal = same & (st[None, :] <= st[:, None])
    nk = CONV_WIDTH - 1
    shift = np.stack([same & (st[None, :] == st[:, None] + k - nk) for k in range(nk)])
    cs = np.arange(SEQ_PER_BLK * nk)
    stsel = np.stack([(cs[None, :] // nk == sq[:, None]) & (cs[None, :] % nk == st[:, None] + k)
                      for k in range(nk)])
    pcur = np.stack([causal & (st[:, None] - st[None, :] < w) for w in POOL_WINDOWS])
    hs = np.arange(SEQ_PER_BLK * POOL_HIST)
    phist = np.stack([(hs[None, :] // POOL_HIST == sq[:, None])
                      & (st[:, None] + POOL_HIST - hs[None, :] % POOL_HIST < w) for w in POOL_WINDOWS])
    as_bf = lambda a: jnp.asarray(a.astype(np.float32), bf16)
    return dict(same=as_bf(same), causal=as_bf(causal), shift=as_bf(shift), stsel=as_bf(stsel),
                pcur=as_bf(pcur), phist=as_bf(phist))


def _sample_mixer_body(z_ref, xbc_ref, dt_ref, u_ref, cst_ref, pst_ref, ssm_in_ref,
                       cw_ref, cb_ref, dtb_ref, alog_ref, dskip_ref, nw_ref, pw_ref, ps_ref,
                       causal_ref, same_ref, expand_ref, expand_t_ref, shift_ref, stsel_ref, pcur_ref, phist_ref,
                       mix_ref, ssm_out_ref,
                       ydiag_ref, ea_ref, yt_ref, cdh_ref, cdl_ref, xdte_t_ref, bm_ref, cm_ref, *, pos0):
    f32, bf16 = jnp.float32, jnp.bfloat16
    s = pl.program_id(1)
    gw = D_SSD // N_SSD_GROUPS

    @pl.when(s == 0)
    def _():
        xbc = xbc_ref[...]
        cst = cst_ref[...]
        acc = cb_ref[...] + cw_ref[CONV_WIDTH - 1:CONV_WIDTH, :] * xbc
        for k in range(CONV_WIDTH - 1):
            tap = _dot_sel_left(shift_ref[k], xbc, 3) + _dot_sel_left(stsel_ref[k], cst, 3)
            acc = acc + cw_ref[k:k + 1, :] * tap
        xbc_c = acc * jax.nn.sigmoid(acc)

        causal_bf = causal_ref[...]
        keep = causal_bf > 0
        xs, bm, cm, dt, a_cum, xdt = _ssd_intra(xbc_c, dt_ref[...], dtb_ref, alog_ref, causal_bf, expand_ref)
        a_tot = _dot_sel_left(same_ref[...], dt * (-jnp.exp(alog_ref[...])), 3)
        a_cum_t = jnp.transpose(a_cum)
        ea_ref[...] = _dot_sel_right(jnp.exp(a_cum), expand_ref[...], 2)
        dte_x = _dot_sel_right(jnp.exp(a_tot - a_cum), expand_ref[...], 2)
        cd_col = _dot_sel_left(expand_t_ref[...], jnp.exp(jnp.transpose(a_tot)), 2)
        cd_hi, cd_lo = _split2(cd_col)
        cdh_ref[...] = cd_hi
        cdl_ref[...] = cd_lo
        bm_ref[...] = bm.astype(bf16)
        cm_ref[...] = cm.astype(bf16)
        for g in range(N_SSD_GROUPS):
            cg = cm[:, g * D_STATE:(g + 1) * D_STATE].astype(bf16)
            bg = bm[:, g * D_STATE:(g + 1) * D_STATE].astype(bf16)
            cb = lax.dot_general(cg, bg, NT_DIMS, preferred_element_type=f32)
            y_diag = _ssd_diag_group(g, cb, a_cum, a_cum_t, keep, xdt)
            ydiag_ref[:, g * gw:(g + 1) * gw] = y_diag + xs[:, g * gw:(g + 1) * gw] * dskip_ref[:, g * gw:(g + 1) * gw]
            xdte_t_ref[g * gw:(g + 1) * gw, :] = jnp.transpose(
                xdt[:, g * gw:(g + 1) * gw] * dte_x[:, g * gw:(g + 1) * gw]).astype(bf16)
        yt_ref[...] = jnp.zeros(yt_ref.shape, f32)

        u = u_ref[...]
        pst = pst_ref[...]
        step = lax.broadcasted_iota(jnp.int32, (BLK, 1), 0) % DEC_SEQ
        pos = (step + (pos0 + 1)).astype(f32)
        for gi, w in enumerate(POOL_WINDOWS):
            sl = slice(gi * POOL_GROUP_DIM, (gi + 1) * POOL_GROUP_DIM)
            ug = u[:, sl]
            wsum = _dot_sel_left(pcur_ref[gi], ug, 2) + _dot_sel_left(phist_ref[gi], pst[:, sl], 2)
            pooled = wsum / jnp.minimum(pos, jnp.float32(w)) - ug
            po = jnp.dot(pooled.astype(bf16), pw_ref[gi], preferred_element_type=f32) * ps_ref[:, sl]
            mix_ref[:, D_SSD + gi * POOL_GROUP_DIM:D_SSD + (gi + 1) * POOL_GROUP_DIM] = po.astype(bf16)

    hg = N_SSD_HEADS // N_SSD_GROUPS
    row_seq = lax.broadcasted_iota(jnp.int32, (BLK, LANES), 0) // DEC_SEQ
    col_seq = lax.broadcasted_iota(jnp.int32, (gw, BLK), 1) // DEC_SEQ
    row_idx = lax.broadcasted_iota(jnp.int32, (BLK, LANES), 0)
    for q in range(SEQ_PER_STEP):
        sq = s * SEQ_PER_STEP + q
        rows_of_s = row_seq == sq
        cols_of_s = col_seq == sq
        pick_s = jnp.where(row_idx == DEC_SEQ * sq, 1.0, 0.0).astype(bf16)
        state = ssm_in_ref[q].reshape(D_SSD, D_STATE)
        for g in range(N_SSD_GROUPS):
            rs = slice(g * gw, (g + 1) * gw)
            sg = state[rs, :]
            cg = cm_ref[:, g * D_STATE:(g + 1) * D_STATE]
            bg = bm_ref[:, g * D_STATE:(g + 1) * D_STATE]
            yt = lax.dot_general(sg.astype(bf16), cg, NT_DIMS, preferred_element_type=f32)
            yt_ref[rs, :] += jnp.where(cols_of_s, yt, 0.0)
            cd = (jnp.dot(cdh_ref[rs, :], pick_s, preferred_element_type=f32)
                  + jnp.dot(cdl_ref[rs, :], pick_s, preferred_element_type=f32))
            upd = jnp.dot(xdte_t_ref[rs, :], jnp.where(rows_of_s, bg, jnp.zeros_like(bg)),
                          preferred_element_type=f32)
            ssm_out_ref[q, g * hg:(g + 1) * hg] = (sg * cd + upd).reshape(hg, SSD_HEAD_DIM, D_STATE)

    @pl.when(s == pl.num_programs(1) - 1)
    def _():
        y = ydiag_ref[...] + jnp.transpose(yt_ref[...]) * ea_ref[...]
        mix_ref[:, 0:D_SSD] = _gated_norm(y, z_ref[...], nw_ref).astype(bf16)


def _sample_mixer(z, xbc, dt, u, row0, n_seq, state_conv, state_ssm, state_pool, pos0,
                  conv_w, conv_b, dt_bias, A_log, D_skip, ssd_norm_w, pool_w, pool_scale):
    f32, bf16 = jnp.float32, jnp.bfloat16
    n_blk = n_seq // SEQ_PER_BLK
    blk0 = row0 // BLK
    nk = CONV_WIDTH - 1
    k = _mixer_constants()
    ks = _sample_constants()

    def row_blk(width):
        return pl.BlockSpec((BLK, width), lambda j, s: (blk0 + j, 0))

    def const(shape):
        return pl.BlockSpec(shape, lambda j, s: (0,) * len(shape))

    steps = SEQ_PER_BLK // SEQ_PER_STEP
    state_spec = pl.BlockSpec((SEQ_PER_STEP, N_SSD_HEADS, SSD_HEAD_DIM, D_STATE),
                              lambda j, s: (j * steps + s, 0, 0, 0))
    pad_h = (0, LANES - N_SSD_HEADS)
    return pl.pallas_call(
        functools.partial(_sample_mixer_body, pos0=pos0),
        grid=(n_blk, steps),
        in_specs=[row_blk(D_SSD), row_blk(D_CONV), row_blk(LANES), row_blk(D_POOL),
                  pl.BlockSpec((SEQ_PER_BLK * nk, D_CONV), lambda j, s: (j, 0)),
                  pl.BlockSpec((SEQ_PER_BLK * POOL_HIST, D_POOL), lambda j, s: (j, 0)),
                  state_spec,
                  const((CONV_WIDTH, D_CONV)), const((1, D_CONV)), const((1, LANES)), const((1, LANES)),
                  const((1, D_SSD)), const((1, D_SSD)),
                  const((N_POOL_GROUPS, POOL_GROUP_DIM, POOL_GROUP_DIM)), const((1, D_POOL)),
                  const((BLK, BLK)), const((BLK, BLK)), const((LANES, D_SSD)), const((D_SSD, LANES)),
                  const((nk, BLK, BLK)), const((nk, BLK, SEQ_PER_BLK * nk)),
                  const((N_POOL_GROUPS, BLK, BLK)), const((N_POOL_GROUPS, BLK, SEQ_PER_BLK * POOL_HIST))],
        out_specs=[pl.BlockSpec((BLK, D_MIX), lambda j, s: (j, 0)), state_spec],
        out_shape=[jax.ShapeDtypeStruct((n_seq * DEC_SEQ, D_MIX), bf16),
                   jax.ShapeDtypeStruct((n_seq, N_SSD_HEADS, SSD_HEAD_DIM, D_STATE), f32)],
        scratch_shapes=[pltpu.VMEM((BLK, D_SSD), f32), pltpu.VMEM((BLK, D_SSD), f32),
                        pltpu.VMEM((D_SSD, BLK), f32), pltpu.VMEM((D_SSD, BLK), bf16),
                        pltpu.VMEM((D_SSD, BLK), bf16), pltpu.VMEM((D_SSD, BLK), bf16),
                        pltpu.VMEM((BLK, N_SSD_GROUPS * D_STATE), bf16),
                        pltpu.VMEM((BLK, N_SSD_GROUPS * D_STATE), bf16)],
        compiler_params=pltpu.CompilerParams(dimension_semantics=("parallel", "arbitrary"),
                                             vmem_limit_bytes=VMEM_LIMIT_BYTES),
        name="sample_mixer",
    )(z, xbc, dt, u, state_conv.reshape(n_seq * nk, D_CONV), state_pool.reshape(n_seq * POOL_HIST, D_POOL),
      state_ssm, conv_w.astype(f32), conv_b.reshape(1, D_CONV).astype(f32),
      jnp.pad(dt_bias.astype(f32), pad_h).reshape(1, LANES), jnp.pad(A_log.astype(f32), pad_h).reshape(1, LANES),
      jnp.repeat(D_skip.astype(f32), SSD_HEAD_DIM).reshape(1, D_SSD), ssd_norm_w.reshape(1, D_SSD).astype(f32),
      pool_w.astype(bf16), pool_scale.reshape(1, D_POOL).astype(f32),
      ks["causal"], ks["same"], k["expand"], k["expand_t"], ks["shift"], ks["stsel"],
      ks["pcur"], ks["phist"])


def _moe_sizes(n_tokens, tm):
    nt = n_tokens // tm
    lmax = -(-(TOP_K * tm + N_EXPERTS * (MOE_SEG_ROWS - 1)) // MOE_SEL_ROWS) * MOE_SEL_ROWS
    rows = (TOP_K * n_tokens + nt * N_EXPERTS * (MOE_SEG_ROWS - 1) + N_EXPERTS * (MOE_PIECE - 1)
            + MOE_ROW_CHUNK)
    n_rows = -(-rows // MOE_PIECE) * MOE_PIECE
    return nt, lmax, n_rows


def _router_body(x_ref, nw_ref, rwh_ref, rwl_ref, rb_ref, h_ref, posg_ref, post_ref, cnt_ref):
    f32, bf16 = jnp.float32, jnp.bfloat16
    tm = x_ref.shape[0]
    x = x_ref[...]
    h = x * lax.rsqrt(jnp.mean(x * x, axis=-1, keepdims=True) + EPS) * nw_ref[...]
    h_hi = h.astype(bf16)
    h_ref[...] = h_hi
    h_lo = (h - h_hi.astype(f32)).astype(bf16)
    wh = rwh_ref[...]
    logits = (jnp.dot(h_hi, wh, preferred_element_type=f32)
              + jnp.dot(h_lo, wh, preferred_element_type=f32)
              + jnp.dot(h_hi, rwl_ref[...], preferred_element_type=f32)) + rb_ref[...]
    lane = lax.broadcasted_iota(jnp.int32, (tm, LANES), 1)
    lanef = lane.astype(f32)
    neg = jnp.float32(-jnp.inf)
    l = jnp.where(lane < N_EXPERTS, logits, neg)
    sels, vals = [], []
    for _ in range(TOP_K):
        m = jnp.max(l, axis=1, keepdims=True)
        idx = jnp.min(jnp.where(l == m, lanef, jnp.float32(LANES)), axis=1, keepdims=True)
        sel = lanef == idx
        l = jnp.where(sel, neg, l)
        sels.append(sel)
        vals.append(m)
    exps = [jnp.exp(v - vals[0]) for v in vals]
    denom = exps[0] + exps[1] + exps[2] + exps[3]
    gates = [e / denom for e in exps]
    chosen = jnp.where(sels[0] | sels[1] | sels[2] | sels[3], 1.0, 0.0).astype(f32)
    row = lax.broadcasted_iota(jnp.int32, (tm, tm), 0)
    col = lax.broadcasted_iota(jnp.int32, (tm, tm), 1)
    lower = jnp.where(col < row, 1.0, 0.0).astype(bf16)
    rank = jnp.dot(lower, chosen.astype(bf16), preferred_element_type=f32)
    cnt = jnp.sum(chosen, axis=0, keepdims=True)
    seg_units = jnp.floor((cnt + (MOE_SEG_ROWS - 1)) * (1.0 / MOE_SEG_ROWS))
    r2 = lax.broadcasted_iota(jnp.int32, (LANES, LANES), 0)
    c2 = lax.broadcasted_iota(jnp.int32, (LANES, LANES), 1)
    upper = jnp.where(r2 < c2, 1.0, 0.0).astype(bf16)
    lstart = jnp.dot(jnp.broadcast_to(seg_units, (8, LANES)).astype(bf16), upper,
                     preferred_element_type=f32)[0:1, :] * MOE_SEG_ROWS
    posmat = lstart + rank
    posg = jnp.zeros((tm, LANES), f32)
    for k in range(TOP_K):
        pos_k = jnp.sum(jnp.where(sels[k], posmat, 0.0), axis=1, keepdims=True)
        posg = posg + jnp.where(lane == k, pos_k, 0.0) + jnp.where(lane == TOP_K + k, gates[k], 0.0)
    posg_ref[...] = posg
    post_ref[...] = jnp.transpose(posg)[0:8, :]
    cnt_ref[0] = jnp.broadcast_to(cnt, (8, LANES)).astype(jnp.int32)


def _moe_router(x1, norm2_w, router_w, router_b, tm):
    n, d = x1.shape
    nt = n // tm
    f32, bf16 = jnp.float32, jnp.bfloat16
    rw = jnp.pad(router_w.astype(f32), ((0, 0), (0, LANES - N_EXPERTS)))
    rw_hi = rw.astype(bf16)
    rw_lo = (rw - rw_hi.astype(f32)).astype(bf16)
    rb = jnp.pad(router_b.astype(f32), (0, LANES - N_EXPERTS)).reshape(1, LANES)
    return pl.pallas_call(
        _router_body,
        grid=(nt,),
        in_specs=[pl.BlockSpec((tm, d), lambda i: (i, 0)),
                  pl.BlockSpec((1, d), lambda i: (0, 0)),
                  pl.BlockSpec((d, LANES), lambda i: (0, 0)),
                  pl.BlockSpec((d, LANES), lambda i: (0, 0)),
                  pl.BlockSpec((1, LANES), lambda i: (0, 0))],
        out_specs=[pl.BlockSpec((tm, d), lambda i: (i, 0)),
                   pl.BlockSpec((tm, LANES), lambda i: (i, 0)),
                   pl.BlockSpec((8, tm), lambda i: (0, i)),
                   pl.BlockSpec((1, 8, LANES), lambda i: (i, 0, 0))],
        out_shape=[jax.ShapeDtypeStruct((n, d), bf16),
                   jax.ShapeDtypeStruct((n, LANES), f32),
                   jax.ShapeDtypeStruct((8, n), f32),
                   jax.ShapeDtypeStruct((nt, 8, LANES), jnp.int32)],
        compiler_params=pltpu.CompilerParams(dimension_semantics=("parallel",),
                                             vmem_limit_bytes=VMEM_LIMIT_BYTES),
        name="moe_router",
    )(x1, norm2_w.reshape(1, d).astype(f32), rw_hi, rw_lo, rb)


def _moe_plan(cnt):
    i32 = jnp.int32
    pad = (cnt + (MOE_SEG_ROWS - 1)) // MOE_SEG_ROWS * MOE_SEG_ROWS
    lstart = jnp.cumsum(pad, axis=1) - pad
    lp = jnp.sum(pad, axis=1)
    tot = jnp.sum(pad, axis=0)
    reg = (tot + (MOE_PIECE - 1)) // MOE_PIECE * MOE_PIECE
    reg_end = jnp.cumsum(reg)
    estart = reg_end - reg
    seg = estart[None, :] + jnp.cumsum(pad, axis=0) - pad
    return dict(
        lstart=lstart.reshape(-1).astype(i32), seg_units=(pad // MOE_SEG_ROWS).reshape(-1).astype(i32),
        seg=seg.reshape(-1).astype(i32), lp=lp.astype(i32),
        tail_start=(estart + tot).astype(i32), tail_units=((reg - tot) // MOE_SEG_ROWS).astype(i32),
        estart=estart.astype(i32), erows=reg.astype(i32), used=reg_end[-1].reshape(1).astype(i32))


def _for_each_segment_copy(i, lstart_ref, units_ref, seg_ref, local_ref, global_ref, sem, to_global, fn):
    def per_expert(e, carry):
        k = i * N_EXPERTS + e

        @pl.when(units_ref[k] > 0)
        def _():
            n = pl.multiple_of(units_ref[k] * MOE_SEG_ROWS, MOE_SEG_ROWS)
            loc = local_ref.at[pl.ds(pl.multiple_of(lstart_ref[k], MOE_SEG_ROWS), n)]
            glo = global_ref.at[pl.ds(pl.multiple_of(seg_ref[k], MOE_SEG_ROWS), n)]
            fn(pltpu.make_async_copy(loc, glo, sem) if to_global else pltpu.make_async_copy(glo, loc, sem))
        return carry
    lax.fori_loop(0, N_EXPERTS, per_expert, 0)


def _for_each_unused_piece(used_ref, zero_ref, rows_ref, sem, fn):
    def per_piece(j, c):
        go = pl.multiple_of(j * MOE_PIECE, MOE_PIECE)
        fn(pltpu.make_async_copy(zero_ref, rows_ref.at[pl.ds(go, MOE_PIECE)], sem))
        return c
    lax.fori_loop(used_ref[0] // MOE_PIECE, rows_ref.shape[0] // MOE_PIECE, per_piece, 0)


def _dispatch_body(lstart_ref, units_ref, seg_ref, lp_ref, tail_start_ref, tail_units_ref, used_ref,
                   h_ref, post_ref, xs_ref, stage_ref, sel_ref, zero_ref, sems, fill_sem):
    f32, bf16 = jnp.float32, jnp.bfloat16
    i = pl.program_id(0)
    nt = pl.num_programs(0)
    slot = i % 2
    tm = h_ref.shape[0]
    lmax = stage_ref.shape[1]

    def for_each_fill_copy(fn):
        def per_expert(e, carry):
            @pl.when(tail_units_ref[e] > 0)
            def _():
                n = pl.multiple_of(tail_units_ref[e] * MOE_SEG_ROWS, MOE_SEG_ROWS)
                go = pl.multiple_of(tail_start_ref[e], MOE_SEG_ROWS)
                fn(pltpu.make_async_copy(zero_ref.at[pl.ds(0, n)], xs_ref.at[pl.ds(go, n)], fill_sem))
            return carry
        lax.fori_loop(0, N_EXPERTS, per_expert, 0)
        _for_each_unused_piece(used_ref, zero_ref, xs_ref, fill_sem, fn)

    @pl.when(i == 0)
    def _():
        zero_ref[...] = jnp.zeros(zero_ref.shape, bf16)
        for_each_fill_copy(lambda cp: cp.start())
        for_each_fill_copy(lambda cp: cp.wait())

    def segment_copies(tile, slot_, fn):
        _for_each_segment_copy(tile, lstart_ref, units_ref, seg_ref, stage_ref.at[slot_], xs_ref,
                               sems.at[slot_], True, fn)

    @pl.when(i >= 2)
    def _():
        segment_copies(i - 2, slot, lambda cp: cp.wait())

    pos = [post_ref[k:k + 1, :] for k in range(TOP_K)]
    r_local = lax.broadcasted_iota(jnp.int32, (MOE_CHUNK, tm), 0).astype(f32).astype(bf16)
    one, zero = jnp.ones((), bf16), jnp.zeros((), bf16)
    for part in range(lmax // MOE_SEL_ROWS):
        @pl.when(part * MOE_SEL_ROWS < lp_ref[i])
        def _():
            for c in range(MOE_SEL_ROWS // MOE_CHUNK):
                r0 = part * MOE_SEL_ROWS + c * MOE_CHUNK
                loc = [jnp.clip(p - r0, -1.0, float(MOE_CHUNK)).astype(bf16) for p in pos]
                hit = (loc[0] == r_local) | (loc[1] == r_local) | (loc[2] == r_local) | (loc[3] == r_local)
                sel_ref[c * MOE_CHUNK:(c + 1) * MOE_CHUNK, :] = jnp.where(hit, one, zero)
            stage_ref[slot, part * MOE_SEL_ROWS:(part + 1) * MOE_SEL_ROWS, :] = jnp.dot(
                sel_ref[...], h_ref[...], preferred_element_type=f32).astype(bf16)

    segment_copies(i, slot, lambda cp: cp.start())

    @pl.when(i == nt - 1)
    def _():
        @pl.when(nt >= 2)
        def _():
            segment_copies(i - 1, 1 - slot, lambda cp: cp.wait())
        segment_copies(i, slot, lambda cp: cp.wait())


def _moe_dispatch(h2, post, plan, tm, lmax, n_rows):
    n, d = h2.shape
    nt = n // tm
    grid_spec = pltpu.PrefetchScalarGridSpec(
        num_scalar_prefetch=7,
        grid=(nt,),
        in_specs=[pl.BlockSpec((tm, d), lambda i, *_: (i, 0)),
                  pl.BlockSpec((8, tm), lambda i, *_: (0, i))],
        out_specs=pl.BlockSpec(memory_space=pl.ANY),
        scratch_shapes=[pltpu.VMEM((2, lmax, d), jnp.bfloat16),
                        pltpu.VMEM((MOE_SEL_ROWS, tm), jnp.bfloat16),
                        pltpu.VMEM((MOE_PIECE, d), jnp.bfloat16),
                        pltpu.SemaphoreType.DMA((2,)),
                        pltpu.SemaphoreType.DMA(())])
    return pl.pallas_call(
        _dispatch_body,
        grid_spec=grid_spec,
        out_shape=jax.ShapeDtypeStruct((n_rows, d), jnp.bfloat16),
        compiler_params=pltpu.CompilerParams(dimension_semantics=("arbitrary",),
                                             vmem_limit_bytes=VMEM_LIMIT_BYTES),
        name="moe_dispatch",
    )(plan["lstart"], plan["seg_units"], plan["seg"], plan["lp"], plan["tail_start"], plan["tail_units"],
      plan["used"], h2, post)


def _experts_body(first_ref, count_ref, cstart_ref, cvalid_ref, total_ref, used_ref,
                  xs_ref, wgu_ref, bgu_ref, wd_ref, bd_ref, os_ref,
                  wgu_bf, wd_bf, xbuf, obuf, zero_ref, in_sems, out_sems, fill_sem):
    f32, bf16 = jnp.float32, jnp.bfloat16
    e = pl.program_id(0)
    total = total_ref[0]
    half = D_FF // 2

    def in_copy(j):
        src = xs_ref.at[pl.ds(pl.multiple_of(cstart_ref[j], MOE_PIECE), MOE_ROW_CHUNK)]
        return pltpu.make_async_copy(src, xbuf.at[j % 3], in_sems.at[j % 3])

    def out_copy(j):
        n = pl.multiple_of(cvalid_ref[j], MOE_PIECE)
        go = pl.multiple_of(cstart_ref[j], MOE_PIECE)
        return pltpu.make_async_copy(obuf.at[j % 2, pl.ds(0, n)], os_ref.at[pl.ds(go, n)], out_sems.at[j % 2])

    @pl.when(e == 0)
    def _():
        for j0 in range(2):
            @pl.when(j0 < total)
            def _():
                in_copy(j0).start()
        zero_ref[...] = jnp.zeros(zero_ref.shape, bf16)
        _for_each_unused_piece(used_ref, zero_ref, os_ref, fill_sem, lambda cp: cp.start())
        _for_each_unused_piece(used_ref, zero_ref, os_ref, fill_sem, lambda cp: cp.wait())

    @pl.when(count_ref[e] > 0)
    def _():
        wgu_bf[...] = wgu_ref[0].astype(bf16)
        wd_bf[...] = wd_ref[0].astype(bf16)

        def chunk(j, carry):
            in_copy(j).wait()

            @pl.when(j + 2 < total)
            def _():
                in_copy(j + 2).start()

            @pl.when(j >= 2)
            def _():
                out_copy(j - 2).wait()

            def mlp(n_rows):
                x = xbuf[j % 3, 0:n_rows, :]
                out = bd_ref[0]
                for hf in range(2):
                    gate = jnp.dot(x, wgu_bf[:, hf * half:(hf + 1) * half], preferred_element_type=f32)
                    gate = jnp.minimum(gate + bgu_ref[0, :, hf * half:(hf + 1) * half], SWIGLU_LIMIT)
                    up = jnp.dot(x, wgu_bf[:, D_FF + hf * half:D_FF + (hf + 1) * half],
                                 preferred_element_type=f32)
                    up = jnp.clip(up + bgu_ref[0, :, D_FF + hf * half:D_FF + (hf + 1) * half],
                                  -SWIGLU_LIMIT, SWIGLU_LIMIT)
                    act = (up + 1.0) * (gate * jax.nn.sigmoid(SWIGLU_ALPHA * gate))
                    out = out + jnp.dot(act.astype(bf16), wd_bf[hf * half:(hf + 1) * half, :],
                                        preferred_element_type=f32)
                obuf[j % 2, 0:n_rows, :] = out.astype(bf16)

            @pl.when(cvalid_ref[j] > MOE_ROW_CHUNK // 2)
            def _():
                mlp(MOE_ROW_CHUNK)

            @pl.when(cvalid_ref[j] <= MOE_ROW_CHUNK // 2)
            def _():
                mlp(MOE_ROW_CHUNK // 2)
            out_copy(j).start()
            return carry
        lax.fori_loop(first_ref[e], first_ref[e] + count_ref[e], chunk, 0)

    @pl.when(e == pl.num_programs(0) - 1)
    def _():
        @pl.when(total >= 2)
        def _():
            out_copy(total - 2).wait()

        @pl.when(total >= 1)
        def _():
            out_copy(total - 1).wait()


def _expert_chunks(plan, n_rows):
    i32 = jnp.int32
    max_chunks = n_rows // MOE_ROW_CHUNK + N_EXPERTS
    count = (plan["erows"] + (MOE_ROW_CHUNK - 1)) // MOE_ROW_CHUNK
    end = jnp.cumsum(count)
    first = end - count
    j = jnp.arange(max_chunks, dtype=i32)
    mine = ((first[None, :] <= j[:, None]) & (j[:, None] < end[None, :])).astype(i32)
    c = j - jnp.sum(mine * first[None, :], axis=1)
    cstart = jnp.sum(mine * plan["estart"][None, :], axis=1) + jnp.sum(mine, axis=1) * c * MOE_ROW_CHUNK
    cvalid = jnp.sum(mine * jnp.clip(plan["erows"][None, :] - c[:, None] * MOE_ROW_CHUNK, 0, MOE_ROW_CHUNK),
                     axis=1)
    return (first.astype(i32), count.astype(i32), cstart.astype(i32), cvalid.astype(i32),
            end[-1].reshape(1).astype(i32))


def _moe_experts(xs, plan, w_gate_up, b_gate_up, w_down, b_down):
    d = xs.shape[1]
    grid_spec = pltpu.PrefetchScalarGridSpec(
        num_scalar_prefetch=6,
        grid=(N_EXPERTS,),
        in_specs=[pl.BlockSpec(memory_space=pl.ANY),
                  pl.BlockSpec((1, d, 2 * D_FF), lambda e, *_: (e, 0, 0)),
                  pl.BlockSpec((1, 1, 2 * D_FF), lambda e, *_: (e, 0, 0)),
                  pl.BlockSpec((1, D_FF, d), lambda e, *_: (e, 0, 0)),
                  pl.BlockSpec((1, 1, d), lambda e, *_: (e, 0, 0))],
        out_specs=pl.BlockSpec(memory_space=pl.ANY),
        scratch_shapes=[pltpu.VMEM((d, 2 * D_FF), jnp.bfloat16),
                        pltpu.VMEM((D_FF, d), jnp.bfloat16),
                        pltpu.VMEM((3, MOE_ROW_CHUNK, d), jnp.bfloat16),
                        pltpu.VMEM((2, MOE_ROW_CHUNK, d), jnp.bfloat16),
                        pltpu.VMEM((MOE_PIECE, d), jnp.bfloat16),
                        pltpu.SemaphoreType.DMA((3,)),
                        pltpu.SemaphoreType.DMA((2,)),
                        pltpu.SemaphoreType.DMA(())])
    return pl.pallas_call(
        _experts_body,
        grid_spec=grid_spec,
        out_shape=jax.ShapeDtypeStruct(xs.shape, jnp.bfloat16),
        compiler_params=pltpu.CompilerParams(dimension_semantics=("arbitrary",),
                                             vmem_limit_bytes=VMEM_LIMIT_BYTES),
        name="moe_experts",
    )(*_expert_chunks(plan, xs.shape[0]), plan["used"],
      xs, w_gate_up, b_gate_up.reshape(N_EXPERTS, 1, 2 * D_FF), w_down, b_down.reshape(N_EXPERTS, 1, d))


def _combine_body(lstart_ref, units_ref, seg_ref, lp_ref,
                  os_ref, posg_ref, x_ref, fw_ref, yp_ref, ys_ref, stage_ref, w_ref, sems,
                  *, n_prompt_tiles):
    f32, bf16 = jnp.float32, jnp.bfloat16
    i = pl.program_id(0)
    nt = pl.num_programs(0)
    slot = i % 2
    tm = x_ref.shape[0]
    lmax = stage_ref.shape[1]

    def segment_copies(tile, slot_, fn):
        _for_each_segment_copy(tile, lstart_ref, units_ref, seg_ref, stage_ref.at[slot_], os_ref,
                               sems.at[slot_], False, fn)

    @pl.when(i == 0)
    def _():
        stage_ref[...] = jnp.zeros(stage_ref.shape, bf16)
        segment_copies(0, 0, lambda cp: cp.start())

    @pl.when(i + 1 < nt)
    def _():
        segment_copies(i + 1, 1 - slot, lambda cp: cp.start())

    posg = posg_ref[...]
    pos = [posg[:, k:k + 1] for k in range(TOP_K)]
    gate = [posg[:, TOP_K + k:TOP_K + k + 1] for k in range(TOP_K)]
    gate_bf = [g.astype(bf16) for g in gate]
    r_local = lax.broadcasted_iota(jnp.int32, (tm, MOE_CHUNK), 1).astype(f32).astype(bf16)
    for c in range(lmax // MOE_CHUNK):
        w = jnp.zeros((tm, MOE_CHUNK), bf16)
        for k in range(TOP_K):
            loc = jnp.clip(pos[k] - c * MOE_CHUNK, -1.0, float(MOE_CHUNK)).astype(bf16)
            w = jnp.where(loc == r_local, gate_bf[k], w)
        w_ref[:, c * MOE_CHUNK:(c + 1) * MOE_CHUNK] = w

    segment_copies(i, slot, lambda cp: cp.wait())
    y = x_ref[...] + jnp.dot(w_ref[...], stage_ref[slot], preferred_element_type=f32)
    out = y * lax.rsqrt(jnp.mean(y * y, axis=-1, keepdims=True) + EPS) * fw_ref[...]

    @pl.when(i < n_prompt_tiles)
    def _():
        yp_ref[...] = out

    @pl.when(i >= n_prompt_tiles)
    def _():
        ys_ref[...] = out


def _moe_combine(os_, posg, x1, final_norm_w, plan, tm, lmax, n_prompt):
    n, d = x1.shape
    nt = n // tm
    n_prompt_tiles = n_prompt // tm
    n_sample_tiles = nt - n_prompt_tiles
    grid_spec = pltpu.PrefetchScalarGridSpec(
        num_scalar_prefetch=4,
        grid=(nt,),
        in_specs=[pl.BlockSpec(memory_space=pl.ANY),
                  pl.BlockSpec((tm, LANES), lambda i, *_: (i, 0)),
                  pl.BlockSpec((tm, d), lambda i, *_: (i, 0)),
                  pl.BlockSpec((1, d), lambda i, *_: (0, 0))],
        out_specs=[pl.BlockSpec((tm, d), lambda i, *_: (jnp.minimum(i, n_prompt_tiles - 1), 0)),
                   pl.BlockSpec((tm, d), lambda i, *_: (jnp.maximum(i - n_prompt_tiles, 0), 0))],
        scratch_shapes=[pltpu.VMEM((2, lmax, d), jnp.bfloat16),
                        pltpu.VMEM((tm, lmax), jnp.bfloat16),
                        pltpu.SemaphoreType.DMA((2,))])
    return pl.pallas_call(
        functools.partial(_combine_body, n_prompt_tiles=n_prompt_tiles),
        grid_spec=grid_spec,
        out_shape=[jax.ShapeDtypeStruct((n_prompt, d), jnp.float32),
                   jax.ShapeDtypeStruct((n_sample_tiles * tm, d), jnp.float32)],
        compiler_params=pltpu.CompilerParams(dimension_semantics=("arbitrary",),
                                             vmem_limit_bytes=VMEM_LIMIT_BYTES),
        name="moe_combine",
    )(plan["lstart"], plan["seg_units"], plan["seg"], plan["lp"],
      os_, posg, x1, final_norm_w.reshape(1, d).astype(jnp.float32))


def _moe_block(x1, n_prompt, norm2_w, router_w, router_b, w_gate_up, b_gate_up, w_down, b_down,
               final_norm_w, tm=MOE_TOKEN_TILE):
    n = x1.shape[0]
    nt, lmax, n_rows = _moe_sizes(n, tm)
    h2, posg, post, cnt3 = _moe_router(x1, norm2_w, router_w, router_b, tm)
    plan = _moe_plan(cnt3[:, 0, :N_EXPERTS])
    xs = _moe_dispatch(h2, post, plan, tm, lmax, n_rows)
    os_ = _moe_experts(xs, plan, w_gate_up, b_gate_up, w_down, b_down)
    return _moe_combine(os_, posg, x1, final_norm_w, plan, tm, lmax, n_prompt)


def kernel(x_prompt, x_sample, state_ssm, state_conv, state_pool, norm1_w, w_in, conv_w, conv_b, dt_bias,
           A_log, D_skip, ssd_norm_w, pool_w, pool_scale, w_out, norm2_w, router_w, router_b, w_gate_up,
           b_gate_up, w_down, b_down, final_norm_w):
    n_prompt = BATCH * SEQ
    n_sample = DEC_BATCH * DEC_SEQ
    xp = x_prompt.reshape(n_prompt, D_MODEL)
    xs = x_sample.reshape(n_sample, D_MODEL)
    z, xbc, dt_raw, u = _in_proj(xp, xs, norm1_w[0], w_in[0])
    mp = (conv_w[0], conv_b[0], dt_bias[0], A_log[0], D_skip[0], ssd_norm_w[0], pool_w[0], pool_scale[0])
    mix_p, s1 = _prompt_mixer(z, xbc, dt_raw, u, BATCH, SEQ, *mp)
    mix_s, s2 = _sample_mixer(z, xbc, dt_raw, u, n_prompt, DEC_BATCH, state_conv[0], state_ssm[0], state_pool[0],
                              PAST_LEN, *mp)
    nk = CONV_WIDTH - 1
    c1 = jnp.stack([xbc[(b + 1) * SEQ - nk:(b + 1) * SEQ] for b in range(BATCH)])
    p1 = jnp.stack([u[(b + 1) * SEQ - POOL_HIST:(b + 1) * SEQ] for b in range(BATCH)])
    c2 = xbc[n_prompt:].reshape(DEC_BATCH, DEC_SEQ, D_CONV)[:, DEC_SEQ - nk:]
    p2 = jnp.concatenate([state_pool[0][:, DEC_SEQ:], u[n_prompt:].reshape(DEC_BATCH, DEC_SEQ, D_POOL)], axis=1)
    x1 = _out_proj(mix_p, mix_s, w_out[0], xp, xs)
    yp, ys = _moe_block(x1, n_prompt, norm2_w[0], router_w[0], router_b[0], w_gate_up[0], b_gate_up[0],
                        w_down[0], b_down[0], final_norm_w)
    return (yp.reshape(x_prompt.shape), ys.reshape(x_sample.shape),
            s1[None], c1[None], p1[None], s2[None], c2[None], p2[None])
```

```python
import functools
import math
import jax, jax.numpy as jnp
from jax import lax
import numpy as np
from jax.experimental import pallas as pl
from jax.experimental.pallas import tpu as pltpu

D_MODEL = 1024
BATCH = 8
SEQ = 2048
DEC_BATCH = 128
DEC_SEQ = 4
PAST_LEN = 16384

D_MIX = 2 * D_MODEL
D_SSD = 3 * D_MIX // 4
SSD_HEAD_DIM = 64
N_SSD_HEADS = D_SSD // SSD_HEAD_DIM
N_SSD_GROUPS = 4
D_STATE = 128
CONV_WIDTH = 4
SSD_CHUNK = 128
D_CONV = D_SSD + 2 * N_SSD_GROUPS * D_STATE
D_POOL = D_MIX - D_SSD
POOL_WINDOWS = (2, 4, 8, 16)
N_POOL_GROUPS = len(POOL_WINDOWS)
POOL_GROUP_DIM = D_POOL // N_POOL_GROUPS
POOL_HIST = max(POOL_WINDOWS) - 1
D_IN_PROJ = D_SSD + D_CONV + N_SSD_HEADS + D_POOL
N_EXPERTS = 32
TOP_K = 4
D_FF = D_MODEL
SWIGLU_LIMIT = 7.0
SWIGLU_ALPHA = 1.702
EPS = 1e-5

LANES = 128
BF16_SUBLANES = 16
VMEM_LIMIT_BYTES = 48 * 1024 * 1024

MOE_TOKEN_TILE = 512
MOE_SEG_ROWS = BF16_SUBLANES
MOE_PIECE = 128
MOE_ROW_CHUNK = 512
MOE_CHUNK = 256
MOE_SEL_ROWS = 512


BLK = SSD_CHUNK
PROJ_ROW_TILE = 512
HIST_ROWS = 16
CONV_TAIL_ROWS = 8
NT_DIMS = (((1,), (1,)), ((), ()))


def _split2(v):
    hi = v.astype(jnp.bfloat16)
    lo = (v - hi.astype(jnp.float32)).astype(jnp.bfloat16)
    return hi, lo


def _dot_sel_left(sel, v, passes):
    out = None
    rem = v
    for p in range(passes):
        part = rem.astype(jnp.bfloat16)
        d = jnp.dot(sel, part, preferred_element_type=jnp.float32)
        out = d if out is None else out + d
        if p + 1 < passes:
            rem = rem - part.astype(jnp.float32)
    return out


def _dot_sel_right(v, sel, passes):
    out = None
    rem = v
    for p in range(passes):
        part = rem.astype(jnp.bfloat16)
        d = jnp.dot(part, sel, preferred_element_type=jnp.float32)
        out = d if out is None else out + d
        if p + 1 < passes:
            rem = rem - part.astype(jnp.float32)
    return out


def _two_part_specs(n_first, n_second, tm, width):
    t1 = n_first // tm
    t2 = n_second // tm
    return (pl.BlockSpec((tm, width), lambda i: (jnp.minimum(i, t1 - 1), 0)),
            pl.BlockSpec((tm, width), lambda i: (jnp.clip(i - t1, 0, t2 - 1), 0)))


def _in_proj_body(xa_ref, xb_ref, nw_ref, w_ref, z_ref, xbc_ref, dt_ref, u_ref, *, tiles_a):
    x = jnp.where(pl.program_id(0) < tiles_a, xa_ref[...], xb_ref[...])
    h = (x * lax.rsqrt(jnp.mean(x * x, axis=-1, keepdims=True) + EPS) * nw_ref[...]).astype(jnp.bfloat16)
    off = 0
    for ref in (z_ref, xbc_ref, dt_ref, u_ref):
        n = ref.shape[1]
        ref[...] = jnp.dot(h, w_ref[:, off:off + n], preferred_element_type=jnp.float32)
        off += n


def _in_proj(xa, xb, norm1_w, w_in):
    d = xa.shape[1]
    n = xa.shape[0] + xb.shape[0]
    f32, bf16 = jnp.float32, jnp.bfloat16
    s1, s2 = D_SSD + D_CONV, D_SSD + D_CONV + N_SSD_HEADS
    w = jnp.concatenate([w_in[:, :s2].astype(bf16), jnp.zeros((d, LANES - N_SSD_HEADS), bf16),
                         w_in[:, s2:].astype(bf16)], axis=1)
    widths = (D_SSD, D_CONV, LANES, D_POOL)
    tm = PROJ_ROW_TILE
    return pl.pallas_call(
        functools.partial(_in_proj_body, tiles_a=xa.shape[0] // tm),
        grid=(n // tm,),
        in_specs=[*_two_part_specs(xa.shape[0], xb.shape[0], tm, d),
                  pl.BlockSpec((1, d), lambda i: (0, 0)),
                  pl.BlockSpec((d, sum(widths)), lambda i: (0, 0), pipeline_mode=pl.Buffered(1))],
        out_specs=[pl.BlockSpec((tm, wd), lambda i: (i, 0)) for wd in widths],
        out_shape=[jax.ShapeDtypeStruct((n, wd), f32) for wd in widths],
        compiler_params=pltpu.CompilerParams(dimension_semantics=("parallel",),
                                             vmem_limit_bytes=VMEM_LIMIT_BYTES),
        name="in_proj",
    )(xa, xb, norm1_w.reshape(1, d).astype(f32), w)


def _out_proj_body(ma_ref, mb_ref, w_ref, xa_ref, xb_ref, o_ref, *, tiles_a):
    first = pl.program_id(0) < tiles_a
    m = jnp.where(first, ma_ref[...], mb_ref[...])
    x = jnp.where(first, xa_ref[...], xb_ref[...])
    o_ref[...] = x + jnp.dot(m, w_ref[...], preferred_element_type=jnp.float32)


def _out_proj(ma, mb, w_out, xa, xb):
    d = xa.shape[1]
    n = xa.shape[0] + xb.shape[0]
    tm = PROJ_ROW_TILE
    return pl.pallas_call(
        functools.partial(_out_proj_body, tiles_a=xa.shape[0] // tm),
        grid=(n // tm,),
        in_specs=[*_two_part_specs(xa.shape[0], xb.shape[0], tm, D_MIX),
                  pl.BlockSpec((D_MIX, d), lambda i: (0, 0)),
                  *_two_part_specs(xa.shape[0], xb.shape[0], tm, d)],
        out_specs=pl.BlockSpec((tm, d), lambda i: (i, 0)),
        out_shape=jax.ShapeDtypeStruct((n, d), jnp.float32),
        compiler_params=pltpu.CompilerParams(dimension_semantics=("parallel",),
                                             vmem_limit_bytes=VMEM_LIMIT_BYTES),
        name="out_proj",
    )(ma, mb, w_out.astype(jnp.bfloat16), xa, xb)


def _mixer_constants():
    bf16 = jnp.bfloat16
    h = np.arange(LANES)[:, None]
    ch = np.arange(D_SSD)[None, :]
    expand = (ch // SSD_HEAD_DIM == h).astype(np.float32)
    i = np.arange(BLK)[:, None]
    j = np.arange(BLK)[None, :]
    causal = (j <= i).astype(np.float32)
    return dict(expand=jnp.asarray(expand, bf16), expand_t=jnp.asarray(expand.T, bf16),
                causal=jnp.asarray(causal, bf16))


def _softplus(x):
    return jnp.maximum(x, 0.0) + jnp.log(1.0 + jnp.exp(-jnp.abs(x)))


def _conv_silu(ext_ref, cw_ref, cb_ref, first_row):
    ext = ext_ref[...]
    last = first_row + CONV_WIDTH - 1
    acc = cb_ref[...] + cw_ref[CONV_WIDTH - 1:CONV_WIDTH, :] * ext[last:last + BLK, :]
    for k in range(CONV_WIDTH - 1):
        tap = pltpu.roll(ext, CONV_WIDTH - 1 - k, axis=0)[last:last + BLK, :]
        acc = acc + cw_ref[k:k + 1, :] * tap
    return acc * jax.nn.sigmoid(acc)


def _ssd_intra(xbc_c, dt_raw, dtb_ref, alog_ref, causal_bf, expand_ref):
    f32 = jnp.float32
    xs = xbc_c[:, :D_SSD]
    bm = xbc_c[:, D_SSD:D_SSD + N_SSD_GROUPS * D_STATE]
    cm = xbc_c[:, D_SSD + N_SSD_GROUPS * D_STATE:]
    dt = _softplus(dt_raw + dtb_ref[...])
    a = dt * (-jnp.exp(alog_ref[...]))
    a_cum = _dot_sel_left(causal_bf, a, 3)
    dt_x = _dot_sel_right(dt, expand_ref[...], 2)
    return xs, bm, cm, dt, a_cum, xs * dt_x


def _ssd_diag_group(g, cb, a_cum, a_cum_t, keep, xdt):
    f32, bf16 = jnp.float32, jnp.bfloat16
    hg = N_SSD_HEADS // N_SSD_GROUPS
    lane = lax.broadcasted_iota(jnp.int32, (BLK, LANES), 1)
    first_head = lane < SSD_HEAD_DIM
    neg = jnp.float32(-jnp.inf)
    outs = []
    for pr in range(hg * SSD_HEAD_DIM // LANES):
        h1 = g * hg + 2 * pr
        blk = (g * hg * SSD_HEAD_DIM) // LANES + pr
        xp = xdt[:, blk * LANES:(blk + 1) * LANES]
        x1 = jnp.where(first_head, xp, 0.0).astype(bf16)
        x2 = jnp.where(first_head, 0.0, xp).astype(bf16)
        m1 = (cb * jnp.exp(jnp.where(keep, a_cum[:, h1:h1 + 1] - a_cum_t[h1:h1 + 1, :], neg))).astype(bf16)
        m2 = (cb * jnp.exp(jnp.where(keep, a_cum[:, h1 + 1:h1 + 2] - a_cum_t[h1 + 1:h1 + 2, :], neg))).astype(bf16)
        outs.append(jnp.dot(m1, x1, preferred_element_type=f32) + jnp.dot(m2, x2, preferred_element_type=f32))
    return jnp.concatenate(outs, axis=1)


def _gated_norm(y, z, nw_ref):
    yg = y * (z * jax.nn.sigmoid(z))
    return yg * lax.rsqrt(jnp.mean(yg * yg, axis=-1, keepdims=True) + EPS) * nw_ref[...]


def _prompt_mixer_body(z_ref, xbc_ref, dt_ref, u_ref, cw_ref, cb_ref, dtb_ref, alog_ref, dskip_ref, nw_ref,
                       pw_ref, ps_ref, causal_ref,
                       mix_ref, ssm_ref, ext_ref, pool_tail_ref, state_ref):
    f32, bf16 = jnp.float32, jnp.bfloat16
    c = pl.program_id(1)
    gw = D_SSD // N_SSD_GROUPS

    @pl.when(c == 0)
    def _():
        ext_ref[0:CONV_TAIL_ROWS, :] = jnp.zeros((CONV_TAIL_ROWS, D_CONV), f32)
        pool_tail_ref[...] = jnp.zeros(pool_tail_ref.shape, f32)
        state_ref[...] = jnp.zeros(state_ref.shape, f32)

    u = u_ref[...]
    ext_u = jnp.concatenate([pool_tail_ref[...], u], axis=0)
    pos = (c * BLK + lax.broadcasted_iota(jnp.int32, (BLK, 1), 0) + 1).astype(f32)
    for gi, w in enumerate(POOL_WINDOWS):
        assert w & (w - 1) == 0 and w <= HIST_ROWS
        sl = slice(gi * POOL_GROUP_DIM, (gi + 1) * POOL_GROUP_DIM)
        ug = u[:, sl]
        acc = ext_u[:, sl]
        span = 1
        while span < w:
            acc = acc + pltpu.roll(acc, span, axis=0)
            span *= 2
        wsum = acc[HIST_ROWS:, :]
        pooled = wsum / jnp.minimum(pos, jnp.float32(w)) - ug
        po = jnp.dot(pooled.astype(bf16), pw_ref[gi], preferred_element_type=f32) * ps_ref[:, sl]
        mix_ref[:, D_SSD + gi * POOL_GROUP_DIM:D_SSD + (gi + 1) * POOL_GROUP_DIM] = po.astype(bf16)
    pool_tail_ref[...] = u_ref[BLK - HIST_ROWS:BLK, :]

    ext_ref[CONV_TAIL_ROWS:CONV_TAIL_ROWS + BLK, :] = xbc_ref[...]
    xbc_c = _conv_silu(ext_ref, cw_ref, cb_ref, CONV_TAIL_ROWS - (CONV_WIDTH - 1))
    ext_ref[0:CONV_TAIL_ROWS, :] = xbc_ref[BLK - CONV_TAIL_ROWS:BLK, :]

    causal_bf = causal_ref[...]
    keep = causal_bf > 0
    xs = xbc_c[:, :D_SSD]
    bm = xbc_c[:, D_SSD:D_SSD + N_SSD_GROUPS * D_STATE]
    cm = xbc_c[:, D_SSD + N_SSD_GROUPS * D_STATE:]
    dt = _softplus(dt_ref[...] + dtb_ref[...])
    a_cum = dt * (-jnp.exp(alog_ref[...]))
    row = lax.broadcasted_iota(jnp.int32, (BLK, LANES), 0)
    span = 1
    while span < BLK:
        a_cum = a_cum + jnp.where(row >= span, pltpu.roll(a_cum, span, axis=0), 0.0)
        span *= 2
    a_cum_t = jnp.transpose(a_cum)
    a_tot = a_cum[BLK - 1:BLK, :]
    ea = jnp.exp(a_cum)
    dte = jnp.exp(a_tot - a_cum)
    cd = jnp.exp(a_tot)
    hg = N_SSD_HEADS // N_SSD_GROUPS
    first_head = lax.broadcasted_iota(jnp.int32, (BLK, LANES), 1) < SSD_HEAD_DIM
    neg = jnp.float32(-jnp.inf)

    def head_cols(v, h1):
        return jnp.where(first_head, v[:, h1:h1 + 1], v[:, h1 + 1:h1 + 2])

    def decay_from(h):
        return jnp.exp(jnp.where(keep, a_cum[:, h:h + 1] - a_cum_t[h:h + 1, :], neg))

    y_parts = []
    for g in range(N_SSD_GROUPS):
        cg = cm[:, g * D_STATE:(g + 1) * D_STATE].astype(bf16)
        bg = bm[:, g * D_STATE:(g + 1) * D_STATE].astype(bf16)
        cb = lax.dot_general(cg, bg, NT_DIMS, preferred_element_type=f32)
        sg = state_ref[g * gw:(g + 1) * gw, :]
        y_off = lax.dot_general(cg, sg.astype(bf16), NT_DIMS, preferred_element_type=f32)
        xdte_parts = []
        for pr in range(gw // LANES):
            h1 = g * hg + 2 * pr
            sl = slice(h1 * SSD_HEAD_DIM, h1 * SSD_HEAD_DIM + LANES)
            xp = xs[:, sl] * head_cols(dt, h1)
            x1 = jnp.where(first_head, xp, 0.0).astype(bf16)
            x2 = jnp.where(first_head, 0.0, xp).astype(bf16)
            y_diag = (jnp.dot((cb * decay_from(h1)).astype(bf16), x1, preferred_element_type=f32)
                      + jnp.dot((cb * decay_from(h1 + 1)).astype(bf16), x2, preferred_element_type=f32))
            y_parts.append(y_diag + y_off[:, pr * LANES:(pr + 1) * LANES] * head_cols(ea, h1)
                           + xs[:, sl] * dskip_ref[:, sl])
            xdte_parts.append(xp * head_cols(dte, h1))
        xdte_t = jnp.transpose(jnp.concatenate(xdte_parts, axis=1)).astype(bf16)
        cd_rows = jnp.concatenate([jnp.broadcast_to(cd[:, h:h + 1], (SSD_HEAD_DIM, D_STATE))
                                   for h in range(g * hg, (g + 1) * hg)], axis=0)
        state_ref[g * gw:(g + 1) * gw, :] = sg * cd_rows + jnp.dot(xdte_t, bg, preferred_element_type=f32)
    y = jnp.concatenate(y_parts, axis=1)
    mix_ref[:, 0:D_SSD] = _gated_norm(y, z_ref[...], nw_ref).astype(bf16)

    @pl.when(c == pl.num_programs(1) - 1)
    def _():
        ssm_ref[0] = state_ref[...].reshape(N_SSD_HEADS, SSD_HEAD_DIM, D_STATE)


def _prompt_mixer(z, xbc, dt, u, n_seq, seq_len, conv_w, conv_b, dt_bias, A_log, D_skip, ssd_norm_w, pool_w,
                  pool_scale):
    f32, bf16 = jnp.float32, jnp.bfloat16
    n = n_seq * seq_len
    n_blk = seq_len // BLK
    k = _mixer_constants()

    def row_blk(width):
        return pl.BlockSpec((BLK, width), lambda b, c: (b * n_blk + c, 0))

    def const(shape):
        return pl.BlockSpec(shape, lambda b, c: (0,) * len(shape))

    pad_h = (0, LANES - N_SSD_HEADS)
    return pl.pallas_call(
        _prompt_mixer_body,
        grid=(n_seq, n_blk),
        in_specs=[row_blk(D_SSD), row_blk(D_CONV), row_blk(LANES), row_blk(D_POOL),
                  const((CONV_WIDTH, D_CONV)), const((1, D_CONV)), const((1, LANES)), const((1, LANES)),
                  const((1, D_SSD)), const((1, D_SSD)),
                  const((N_POOL_GROUPS, POOL_GROUP_DIM, POOL_GROUP_DIM)), const((1, D_POOL)),
                  const((BLK, BLK))],
        out_specs=[pl.BlockSpec((BLK, D_MIX), lambda b, c: (b * n_blk + c, 0)),
                   pl.BlockSpec((1, N_SSD_HEADS, SSD_HEAD_DIM, D_STATE), lambda b, c: (b, 0, 0, 0))],
        out_shape=[jax.ShapeDtypeStruct((n, D_MIX), bf16),
                   jax.ShapeDtypeStruct((n_seq, N_SSD_HEADS, SSD_HEAD_DIM, D_STATE), f32)],
        scratch_shapes=[pltpu.VMEM((CONV_TAIL_ROWS + BLK, D_CONV), f32),
                        pltpu.VMEM((HIST_ROWS, D_POOL), f32),
                        pltpu.VMEM((D_SSD, D_STATE), f32)],
        compiler_params=pltpu.CompilerParams(dimension_semantics=("parallel", "arbitrary"),
                                             vmem_limit_bytes=VMEM_LIMIT_BYTES),
        name="prompt_mixer",
    )(z, xbc, dt, u, conv_w.astype(f32), conv_b.reshape(1, D_CONV).astype(f32),
      jnp.pad(dt_bias.astype(f32), pad_h).reshape(1, LANES), jnp.pad(A_log.astype(f32), pad_h).reshape(1, LANES),
      jnp.repeat(D_skip.astype(f32), SSD_HEAD_DIM).reshape(1, D_SSD), ssd_norm_w.reshape(1, D_SSD).astype(f32),
      pool_w.astype(bf16), pool_scale.reshape(1, D_POOL).astype(f32),
      k["causal"])


SEQ_PER_BLK = BLK // DEC_SEQ
SEQ_PER_STEP = 8


def _sample_constants():
    bf16 = jnp.bfloat16
    r = np.arange(BLK)
    sq, st = r // DEC_SEQ, r % DEC_SEQ
    same = sq[:, None] == sq[None, :]
    causal = same & (st[None, :] <= st[:, None])
    nk = CONV_WIDTH - 1
    shift = np.stack([same & (st[None, :] == st[:, None] + k - nk) for k in range(nk)])
    cs = np.arange(SEQ_PER_BLK * nk)
    stsel = np.stack([(cs[None, :] // nk == sq[:, None]) & (cs[None, :] % nk == st[:, None] + k)
                      for k in range(nk)])
    pcur = np.stack([causal & (st[:, None] - st[None, :] < w) for w in POOL_WINDOWS])
    hs = np.arange(SEQ_PER_BLK * POOL_HIST)
    phist = np.stack([(hs[None, :] // POOL_HIST == sq[:, None])
                      & (st[:, None] + POOL_HIST - hs[None, :] % POOL_HIST < w) for w in POOL_WINDOWS])
    as_bf = lambda a: jnp.asarray(a.astype(np.float32), bf16)
    return dict(same=as_bf(same), causal=as_bf(causal), shift=as_bf(shift), stsel=as_bf(stsel),
                pcur=as_bf(pcur), phist=as_bf(phist))


def _sample_mixer_body(z_ref, xbc_ref, dt_ref, u_ref, cst_ref, pst_ref, ssm_in_ref,
                       cw_ref, cb_ref, dtb_ref, alog_ref, dskip_ref, nw_ref, pw_ref, ps_ref,
                       causal_ref, same_ref, expand_ref, expand_t_ref, shift_ref, stsel_ref, pcur_ref, phist_ref,
                       mix_ref, ssm_out_ref,
                       ydiag_ref, ea_ref, yt_ref, cdh_ref, cdl_ref, xdte_t_ref, bm_ref, cm_ref, *, pos0):
    f32, bf16 = jnp.float32, jnp.bfloat16
    s = pl.program_id(1)
    gw = D_SSD // N_SSD_GROUPS

    @pl.when(s == 0)
    def _():
        xbc = xbc_ref[...]
        cst = cst_ref[...]
        acc = cb_ref[...] + cw_ref[CONV_WIDTH - 1:CONV_WIDTH, :] * xbc
        for k in range(CONV_WIDTH - 1):
            tap = _dot_sel_left(shift_ref[k], xbc, 3) + _dot_sel_left(stsel_ref[k], cst, 3)
            acc = acc + cw_ref[k:k + 1, :] * tap
        xbc_c = acc * jax.nn.sigmoid(acc)

        causal_bf = causal_ref[...]
        keep = causal_bf > 0
        xs, bm, cm, dt, a_cum, xdt = _ssd_intra(xbc_c, dt_ref[...], dtb_ref, alog_ref, causal_bf, expand_ref)
        a_tot = _dot_sel_left(same_ref[...], dt * (-jnp.exp(alog_ref[...])), 3)
        a_cum_t = jnp.transpose(a_cum)
        ea_ref[...] = _dot_sel_right(jnp.exp(a_cum), expand_ref[...], 2)
        dte_x = _dot_sel_right(jnp.exp(a_tot - a_cum), expand_ref[...], 2)
        cd_col = _dot_sel_left(expand_t_ref[...], jnp.exp(jnp.transpose(a_tot)), 2)
        cd_hi, cd_lo = _split2(cd_col)
        cdh_ref[...] = cd_hi
        cdl_ref[...] = cd_lo
        bm_ref[...] = bm.astype(bf16)
        cm_ref[...] = cm.astype(bf16)
        for g in range(N_SSD_GROUPS):
            cg = cm[:, g * D_STATE:(g + 1) * D_STATE].astype(bf16)
            bg = bm[:, g * D_STATE:(g + 1) * D_STATE].astype(bf16)
            cb = lax.dot_general(cg, bg, NT_DIMS, preferred_element_type=f32)
            y_diag = _ssd_diag_group(g, cb, a_cum, a_cum_t, keep, xdt)
            ydiag_ref[:, g * gw:(g + 1) * gw] = y_diag + xs[:, g * gw:(g + 1) * gw] * dskip_ref[:, g * gw:(g + 1) * gw]
            xdte_t_ref[g * gw:(g + 1) * gw, :] = jnp.transpose(
                xdt[:, g * gw:(g + 1) * gw] * dte_x[:, g * gw:(g + 1) * gw]).astype(bf16)
        yt_ref[...] = jnp.zeros(yt_ref.shape, f32)

        u = u_ref[...]
        pst = pst_ref[...]
        step = lax.broadcasted_iota(jnp.int32, (BLK, 1), 0) % DEC_SEQ
        pos = (step + (pos0 + 1)).astype(f32)
        for gi, w in enumerate(POOL_WINDOWS):
            sl = slice(gi * POOL_GROUP_DIM, (gi + 1) * POOL_GROUP_DIM)
            ug = u[:, sl]
            wsum = _dot_sel_left(pcur_ref[gi], ug, 2) + _dot_sel_left(phist_ref[gi], pst[:, sl], 2)
            pooled = wsum / jnp.minimum(pos, jnp.float32(w)) - ug
            po = jnp.dot(pooled.astype(bf16), pw_ref[gi], preferred_element_type=f32) * ps_ref[:, sl]
            mix_ref[:, D_SSD + gi * POOL_GROUP_DIM:D_SSD + (gi + 1) * POOL_GROUP_DIM] = po.astype(bf16)

    hg = N_SSD_HEADS // N_SSD_GROUPS
    row_seq = lax.broadcasted_iota(jnp.int32, (BLK, LANES), 0) // DEC_SEQ
    col_seq = lax.broadcasted_iota(jnp.int32, (gw, BLK), 1) // DEC_SEQ
    row_idx = lax.broadcasted_iota(jnp.int32, (BLK, LANES), 0)
    for q in range(SEQ_PER_STEP):
        sq = s * SEQ_PER_STEP + q
        rows_of_s = row_seq == sq
        cols_of_s = col_seq == sq
        pick_s = jnp.where(row_idx == DEC_SEQ * sq, 1.0, 0.0).astype(bf16)
        state = ssm_in_ref[q].reshape(D_SSD, D_STATE)
        for g in range(N_SSD_GROUPS):
            rs = slice(g * gw, (g + 1) * gw)
            sg = state[rs, :]
            cg = cm_ref[:, g * D_STATE:(g + 1) * D_STATE]
            bg = bm_ref[:, g * D_STATE:(g + 1) * D_STATE]
            yt = lax.dot_general(sg.astype(bf16), cg, NT_DIMS, preferred_element_type=f32)
            yt_ref[rs, :] += jnp.where(cols_of_s, yt, 0.0)
            cd = (jnp.dot(cdh_ref[rs, :], pick_s, preferred_element_type=f32)
                  + jnp.dot(cdl_ref[rs, :], pick_s, preferred_element_type=f32))
            upd = jnp.dot(xdte_t_ref[rs, :], jnp.where(rows_of_s, bg, jnp.zeros_like(bg)),
                          preferred_element_type=f32)
            ssm_out_ref[q, g * hg:(g + 1) * hg] = (sg * cd + upd).reshape(hg, SSD_HEAD_DIM, D_STATE)

    @pl.when(s == pl.num_programs(1) - 1)
    def _():
        y = ydiag_ref[...] + jnp.transpose(yt_ref[...]) * ea_ref[...]
        mix_ref[:, 0:D_SSD] = _gated_norm(y, z_ref[...], nw_ref).astype(bf16)


def _sample_mixer(z, xbc, dt, u, row0, n_seq, state_conv, state_ssm, state_pool, pos0,
                  conv_w, conv_b, dt_bias, A_log, D_skip, ssd_norm_w, pool_w, pool_scale):
    f32, bf16 = jnp.float32, jnp.bfloat16
    n_blk = n_seq // SEQ_PER_BLK
    blk0 = row0 // BLK
    nk = CONV_WIDTH - 1
    k = _mixer_constants()
    ks = _sample_constants()

    def row_blk(width):
        return pl.BlockSpec((BLK, width), lambda j, s: (blk0 + j, 0))

    def const(shape):
        return pl.BlockSpec(shape, lambda j, s: (0,) * len(shape))

    steps = SEQ_PER_BLK // SEQ_PER_STEP
    state_spec = pl.BlockSpec((SEQ_PER_STEP, N_SSD_HEADS, SSD_HEAD_DIM, D_STATE),
                              lambda j, s: (j * steps + s, 0, 0, 0))
    pad_h = (0, LANES - N_SSD_HEADS)
    return pl.pallas_call(
        functools.partial(_sample_mixer_body, pos0=pos0),
        grid=(n_blk, steps),
        in_specs=[row_blk(D_SSD), row_blk(D_CONV), row_blk(LANES), row_blk(D_POOL),
                  pl.BlockSpec((SEQ_PER_BLK * nk, D_CONV), lambda j, s: (j, 0)),
                  pl.BlockSpec((SEQ_PER_BLK * POOL_HIST, D_POOL), lambda j, s: (j, 0)),
                  state_spec,
                  const((CONV_WIDTH, D_CONV)), const((1, D_CONV)), const((1, LANES)), const((1, LANES)),
                  const((1, D_SSD)), const((1, D_SSD)),
                  const((N_POOL_GROUPS, POOL_GROUP_DIM, POOL_GROUP_DIM)), const((1, D_POOL)),
                  const((BLK, BLK)), const((BLK, BLK)), const((LANES, D_SSD)), const((D_SSD, LANES)),
                  const((nk, BLK, BLK)), const((nk, BLK, SEQ_PER_BLK * nk)),
                  const((N_POOL_GROUPS, BLK, BLK)), const((N_POOL_GROUPS, BLK, SEQ_PER_BLK * POOL_HIST))],
        out_specs=[pl.BlockSpec((BLK, D_MIX), lambda j, s: (j, 0)), state_spec],
        out_shape=[jax.ShapeDtypeStruct((n_seq * DEC_SEQ, D_MIX), bf16),
                   jax.ShapeDtypeStruct((n_seq, N_SSD_HEADS, SSD_HEAD_DIM, D_STATE), f32)],
        scratch_shapes=[pltpu.VMEM((BLK, D_SSD), f32), pltpu.VMEM((BLK, D_SSD), f32),
                        pltpu.VMEM((D_SSD, BLK), f32), pltpu.VMEM((D_SSD, BLK), bf16),
                        pltpu.VMEM((D_SSD, BLK), bf16), pltpu.VMEM((D_SSD, BLK), bf16),
                        pltpu.VMEM((BLK, N_SSD_GROUPS * D_STATE), bf16),
                        pltpu.VMEM((BLK, N_SSD_GROUPS * D_STATE), bf16)],
        compiler_params=pltpu.CompilerParams(dimension_semantics=("parallel", "arbitrary"),
                                             vmem_limit_bytes=VMEM_LIMIT_BYTES),
        name="sample_mixer",
    )(z, xbc, dt, u, state_conv.reshape(n_seq * nk, D_CONV), state_pool.reshape(n_seq * POOL_HIST, D_POOL),
      state_ssm, conv_w.astype(f32), conv_b.reshape(1, D_CONV).astype(f32),
      jnp.pad(dt_bias.astype(f32), pad_h).reshape(1, LANES), jnp.pad(A_log.astype(f32), pad_h).reshape(1, LANES),
      jnp.repeat(D_skip.astype(f32), SSD_HEAD_DIM).reshape(1, D_SSD), ssd_norm_w.reshape(1, D_SSD).astype(f32),
      pool_w.astype(bf16), pool_scale.reshape(1, D_POOL).astype(f32),
      ks["causal"], ks["same"], k["expand"], k["expand_t"], ks["shift"], ks["stsel"],
      ks["pcur"], ks["phist"])


def _moe_sizes(n_tokens, tm):
    nt = n_tokens // tm
    lmax = -(-(TOP_K * tm + N_EXPERTS * (MOE_SEG_ROWS - 1)) // MOE_SEL_ROWS) * MOE_SEL_ROWS
    rows = (TOP_K * n_tokens + nt * N_EXPERTS * (MOE_SEG_ROWS - 1) + N_EXPERTS * (MOE_PIECE - 1)
            + MOE_ROW_CHUNK)
    n_rows = -(-rows // MOE_PIECE) * MOE_PIECE
    return nt, lmax, n_rows


def _router_body(x_ref, nw_ref, rwh_ref, rwl_ref, rb_ref, h_ref, posg_ref, post_ref, cnt_ref):
    f32, bf16 = jnp.float32, jnp.bfloat16
    tm = x_ref.shape[0]
    x = x_ref[...]
    h = x * lax.rsqrt(jnp.mean(x * x, axis=-1, keepdims=True) + EPS) * nw_ref[...]
    h_hi = h.astype(bf16)
    h_ref[...] = h_hi
    h_lo = (h - h_hi.astype(f32)).astype(bf16)
    wh = rwh_ref[...]
    logits = (jnp.dot(h_hi, wh, preferred_element_type=f32)
              + jnp.dot(h_lo, wh, preferred_element_type=f32)
              + jnp.dot(h_hi, rwl_ref[...], preferred_element_type=f32)) + rb_ref[...]
    lane = lax.broadcasted_iota(jnp.int32, (tm, LANES), 1)
    lanef = lane.astype(f32)
    neg = jnp.float32(-jnp.inf)
    l = jnp.where(lane < N_EXPERTS, logits, neg)
    sels, vals = [], []
    for _ in range(TOP_K):
        m = jnp.max(l, axis=1, keepdims=True)
        idx = jnp.min(jnp.where(l == m, lanef, jnp.float32(LANES)), axis=1, keepdims=True)
        sel = lanef == idx
        l = jnp.where(sel, neg, l)
        sels.append(sel)
        vals.append(m)
    exps = [jnp.exp(v - vals[0]) for v in vals]
    denom = exps[0] + exps[1] + exps[2] + exps[3]
    gates = [e / denom for e in exps]
    chosen = jnp.where(sels[0] | sels[1] | sels[2] | sels[3], 1.0, 0.0).astype(f32)
    row = lax.broadcasted_iota(jnp.int32, (tm, tm), 0)
    col = lax.broadcasted_iota(jnp.int32, (tm, tm), 1)
    lower = jnp.where(col < row, 1.0, 0.0).astype(bf16)
    rank = jnp.dot(lower, chosen.astype(bf16), preferred_element_type=f32)
    cnt = jnp.sum(chosen, axis=0, keepdims=True)
    seg_units = jnp.floor((cnt + (MOE_SEG_ROWS - 1)) * (1.0 / MOE_SEG_ROWS))
    r2 = lax.broadcasted_iota(jnp.int32, (LANES, LANES), 0)
    c2 = lax.broadcasted_iota(jnp.int32, (LANES, LANES), 1)
    upper = jnp.where(r2 < c2, 1.0, 0.0).astype(bf16)
    lstart = jnp.dot(jnp.broadcast_to(seg_units, (8, LANES)).astype(bf16), upper,
                     preferred_element_type=f32)[0:1, :] * MOE_SEG_ROWS
    posmat = lstart + rank
    posg = jnp.zeros((tm, LANES), f32)
    for k in range(TOP_K):
        pos_k = jnp.sum(jnp.where(sels[k], posmat, 0.0), axis=1, keepdims=True)
        posg = posg + jnp.where(lane == k, pos_k, 0.0) + jnp.where(lane == TOP_K + k, gates[k], 0.0)
    posg_ref[...] = posg
    post_ref[...] = jnp.transpose(posg)[0:8, :]
    cnt_ref[0] = jnp.broadcast_to(cnt, (8, LANES)).astype(jnp.int32)


def _moe_router(x1, norm2_w, router_w, router_b, tm):
    n, d = x1.shape
    nt = n // tm
    f32, bf16 = jnp.float32, jnp.bfloat16
    rw = jnp.pad(router_w.astype(f32), ((0, 0), (0, LANES - N_EXPERTS)))
    rw_hi = rw.astype(bf16)
    rw_lo = (rw - rw_hi.astype(f32)).astype(bf16)
    rb = jnp.pad(router_b.astype(f32), (0, LANES - N_EXPERTS)).reshape(1, LANES)
    return pl.pallas_call(
        _router_body,
        grid=(nt,),
        in_specs=[pl.BlockSpec((tm, d), lambda i: (i, 0)),
                  pl.BlockSpec((1, d), lambda i: (0, 0)),
                  pl.BlockSpec((d, LANES), lambda i: (0, 0)),
                  pl.BlockSpec((d, LANES), lambda i: (0, 0)),
                  pl.BlockSpec((1, LANES), lambda i: (0, 0))],
        out_specs=[pl.BlockSpec((tm, d), lambda i: (i, 0)),
                   pl.BlockSpec((tm, LANES), lambda i: (i, 0)),
                   pl.BlockSpec((8, tm), lambda i: (0, i)),
                   pl.BlockSpec((1, 8, LANES), lambda i: (i, 0, 0))],
        out_shape=[jax.ShapeDtypeStruct((n, d), bf16),
                   jax.ShapeDtypeStruct((n, LANES), f32),
                   jax.ShapeDtypeStruct((8, n), f32),
                   jax.ShapeDtypeStruct((nt, 8, LANES), jnp.int32)],
        compiler_params=pltpu.CompilerParams(dimension_semantics=("parallel",),
                                             vmem_limit_bytes=VMEM_LIMIT_BYTES),
        name="moe_router",
    )(x1, norm2_w.reshape(1, d).astype(f32), rw_hi, rw_lo, rb)


def _moe_plan(cnt):
    i32 = jnp.int32
    pad = (cnt + (MOE_SEG_ROWS - 1)) // MOE_SEG_ROWS * MOE_SEG_ROWS
    lstart = jnp.cumsum(pad, axis=1) - pad
    lp = jnp.sum(pad, axis=1)
    tot = jnp.sum(pad, axis=0)
    reg = (tot + (MOE_PIECE - 1)) // MOE_PIECE * MOE_PIECE
    reg_end = jnp.cumsum(reg)
    estart = reg_end - reg
    seg = estart[None, :] + jnp.cumsum(pad, axis=0) - pad
    return dict(
        lstart=lstart.reshape(-1).astype(i32), seg_units=(pad // MOE_SEG_ROWS).reshape(-1).astype(i32),
        seg=seg.reshape(-1).astype(i32), lp=lp.astype(i32),
        tail_start=(estart + tot).astype(i32), tail_units=((reg - tot) // MOE_SEG_ROWS).astype(i32),
        estart=estart.astype(i32), erows=reg.astype(i32), used=reg_end[-1].reshape(1).astype(i32))


def _for_each_segment_copy(i, lstart_ref, units_ref, seg_ref, local_ref, global_ref, sem, to_global, fn):
    def per_expert(e, carry):
        k = i * N_EXPERTS + e

        @pl.when(units_ref[k] > 0)
        def _():
            n = pl.multiple_of(units_ref[k] * MOE_SEG_ROWS, MOE_SEG_ROWS)
            loc = local_ref.at[pl.ds(pl.multiple_of(lstart_ref[k], MOE_SEG_ROWS), n)]
            glo = global_ref.at[pl.ds(pl.multiple_of(seg_ref[k], MOE_SEG_ROWS), n)]
            fn(pltpu.make_async_copy(loc, glo, sem) if to_global else pltpu.make_async_copy(glo, loc, sem))
        return carry
    lax.fori_loop(0, N_EXPERTS, per_expert, 0)


def _for_each_unused_piece(used_ref, zero_ref, rows_ref, sem, fn):
    def per_piece(j, c):
        go = pl.multiple_of(j * MOE_PIECE, MOE_PIECE)
        fn(pltpu.make_async_copy(zero_ref, rows_ref.at[pl.ds(go, MOE_PIECE)], sem))
        return c
    lax.fori_loop(used_ref[0] // MOE_PIECE, rows_ref.shape[0] // MOE_PIECE, per_piece, 0)


def _dispatch_body(lstart_ref, units_ref, seg_ref, lp_ref, tail_start_ref, tail_units_ref, used_ref,
                   h_ref, post_ref, xs_ref, stage_ref, sel_ref, zero_ref, sems, fill_sem):
    f32, bf16 = jnp.float32, jnp.bfloat16
    i = pl.program_id(0)
    nt = pl.num_programs(0)
    slot = i % 2
    tm = h_ref.shape[0]
    lmax = stage_ref.shape[1]

    def for_each_fill_copy(fn):
        def per_expert(e, carry):
            @pl.when(tail_units_ref[e] > 0)
            def _():
                n = pl.multiple_of(tail_units_ref[e] * MOE_SEG_ROWS, MOE_SEG_ROWS)
                go = pl.multiple_of(tail_start_ref[e], MOE_SEG_ROWS)
                fn(pltpu.make_async_copy(zero_ref.at[pl.ds(0, n)], xs_ref.at[pl.ds(go, n)], fill_sem))
            return carry
        lax.fori_loop(0, N_EXPERTS, per_expert, 0)
        _for_each_unused_piece(used_ref, zero_ref, xs_ref, fill_sem, fn)

    @pl.when(i == 0)
    def _():
        zero_ref[...] = jnp.zeros(zero_ref.shape, bf16)
        for_each_fill_copy(lambda cp: cp.start())
        for_each_fill_copy(lambda cp: cp.wait())

    def segment_copies(tile, slot_, fn):
        _for_each_segment_copy(tile, lstart_ref, units_ref, seg_ref, stage_ref.at[slot_], xs_ref,
                               sems.at[slot_], True, fn)

    @pl.when(i >= 2)
    def _():
        segment_copies(i - 2, slot, lambda cp: cp.wait())

    pos = [post_ref[k:k + 1, :] for k in range(TOP_K)]
    r_local = lax.broadcasted_iota(jnp.int32, (MOE_CHUNK, tm), 0).astype(f32).astype(bf16)
    one, zero = jnp.ones((), bf16), jnp.zeros((), bf16)
    for part in range(lmax // MOE_SEL_ROWS):
        for c in range(MOE_SEL_ROWS // MOE_CHUNK):
            r0 = part * MOE_SEL_ROWS + c * MOE_CHUNK
            loc = [jnp.clip(p - r0, -1.0, float(MOE_CHUNK)).astype(bf16) for p in pos]
            hit = (loc[0] == r_local) | (loc[1] == r_local) | (loc[2] == r_local) | (loc[3] == r_local)
            sel_ref[part, c * MOE_CHUNK:(c + 1) * MOE_CHUNK, :] = jnp.where(hit, one, zero)
        stage_ref[slot, part * MOE_SEL_ROWS:(part + 1) * MOE_SEL_ROWS, :] = jnp.dot(
            sel_ref[part], h_ref[...], preferred_element_type=f32).astype(bf16)

    segment_copies(i, slot, lambda cp: cp.start())

    @pl.when(i == nt - 1)
    def _():
        @pl.when(nt >= 2)
        def _():
            segment_copies(i - 1, 1 - slot, lambda cp: cp.wait())
        segment_copies(i, slot, lambda cp: cp.wait())


def _moe_dispatch(h2, post, plan, tm, lmax, n_rows):
    n, d = h2.shape
    nt = n // tm
    grid_spec = pltpu.PrefetchScalarGridSpec(
        num_scalar_prefetch=7,
        grid=(nt,),
        in_specs=[pl.BlockSpec((tm, d), lambda i, *_: (i, 0)),
                  pl.BlockSpec((8, tm), lambda i, *_: (0, i))],
        out_specs=pl.BlockSpec(memory_space=pl.ANY),
        scratch_shapes=[pltpu.VMEM((2, lmax, d), jnp.bfloat16),
                        pltpu.VMEM((lmax // MOE_SEL_ROWS, MOE_SEL_ROWS, tm), jnp.bfloat16),
                        pltpu.VMEM((MOE_PIECE, d), jnp.bfloat16),
                        pltpu.SemaphoreType.DMA((2,)),
                        pltpu.SemaphoreType.DMA(())])
    return pl.pallas_call(
        _dispatch_body,
        grid_spec=grid_spec,
        out_shape=jax.ShapeDtypeStruct((n_rows, d), jnp.bfloat16),
        compiler_params=pltpu.CompilerParams(dimension_semantics=("arbitrary",),
                                             vmem_limit_bytes=VMEM_LIMIT_BYTES),
        name="moe_dispatch",
    )(plan["lstart"], plan["seg_units"], plan["seg"], plan["lp"], plan["tail_start"], plan["tail_units"],
      plan["used"], h2, post)


def _experts_body(first_ref, count_ref, cstart_ref, cvalid_ref, total_ref, used_ref,
                  xs_ref, wgu_ref, bgu_ref, wd_ref, bd_ref, os_ref,
                  wgu_bf, wd_bf, xbuf, obuf, zero_ref, in_sems, out_sems, fill_sem):
    f32, bf16 = jnp.float32, jnp.bfloat16
    e = pl.program_id(0)
    total = total_ref[0]
    half = D_FF // 2

    def in_copy(j):
        src = xs_ref.at[pl.ds(pl.multiple_of(cstart_ref[j], MOE_PIECE), MOE_ROW_CHUNK)]
        return pltpu.make_async_copy(src, xbuf.at[j % 3], in_sems.at[j % 3])

    def out_copy(j):
        n = pl.multiple_of(cvalid_ref[j], MOE_PIECE)
        go = pl.multiple_of(cstart_ref[j], MOE_PIECE)
        return pltpu.make_async_copy(obuf.at[j % 2, pl.ds(0, n)], os_ref.at[pl.ds(go, n)], out_sems.at[j % 2])

    @pl.when(e == 0)
    def _():
        for j0 in range(2):
            @pl.when(j0 < total)
            def _():
                in_copy(j0).start()
        zero_ref[...] = jnp.zeros(zero_ref.shape, bf16)
        _for_each_unused_piece(used_ref, zero_ref, os_ref, fill_sem, lambda cp: cp.start())
        _for_each_unused_piece(used_ref, zero_ref, os_ref, fill_sem, lambda cp: cp.wait())

    @pl.when(count_ref[e] > 0)
    def _():
        wgu_bf[...] = wgu_ref[0].astype(bf16)
        wd_bf[...] = wd_ref[0].astype(bf16)

        def chunk(j, carry):
            in_copy(j).wait()

            @pl.when(j + 2 < total)
            def _():
                in_copy(j + 2).start()

            @pl.when(j >= 2)
            def _():
                out_copy(j - 2).wait()

            def mlp(n_rows):
                x = xbuf[j % 3, 0:n_rows, :]
                out = bd_ref[0]
                for hf in range(2):
                    gate = jnp.dot(x, wgu_bf[:, hf * half:(hf + 1) * half], preferred_element_type=f32)
                    gate = jnp.minimum(gate + bgu_ref[0, :, hf * half:(hf + 1) * half], SWIGLU_LIMIT)
                    up = jnp.dot(x, wgu_bf[:, D_FF + hf * half:D_FF + (hf + 1) * half],
                                 preferred_element_type=f32)
                    up = jnp.clip(up + bgu_ref[0, :, D_FF + hf * half:D_FF + (hf + 1) * half],
                                  -SWIGLU_LIMIT, SWIGLU_LIMIT)
                    act = (up + 1.0) * (gate * jax.nn.sigmoid(SWIGLU_ALPHA * gate))
                    out = out + jnp.dot(act.astype(bf16), wd_bf[hf * half:(hf + 1) * half, :],
                                        preferred_element_type=f32)
                obuf[j % 2, 0:n_rows, :] = out.astype(bf16)

            @pl.when(cvalid_ref[j] > MOE_ROW_CHUNK // 2)
            def _():
                mlp(MOE_ROW_CHUNK)

            @pl.when(cvalid_ref[j] <= MOE_ROW_CHUNK // 2)
            def _():
                mlp(MOE_ROW_CHUNK // 2)
            out_copy(j).start()
            return carry
        lax.fori_loop(first_ref[e], first_ref[e] + count_ref[e], chunk, 0)

    @pl.when(e == pl.num_programs(0) - 1)
    def _():
        @pl.when(total >= 2)
        def _():
            out_copy(total - 2).wait()

        @pl.when(total >= 1)
        def _():
            out_copy(total - 1).wait()


def _expert_chunks(plan, n_rows):
    i32 = jnp.int32
    max_chunks = n_rows // MOE_ROW_CHUNK + N_EXPERTS
    count = (plan["erows"] + (MOE_ROW_CHUNK - 1)) // MOE_ROW_CHUNK
    end = jnp.cumsum(count)
    first = end - count
    j = jnp.arange(max_chunks, dtype=i32)
    mine = ((first[None, :] <= j[:, None]) & (j[:, None] < end[None, :])).astype(i32)
    c = j - jnp.sum(mine * first[None, :], axis=1)
    cstart = jnp.sum(mine * plan["estart"][None, :], axis=1) + jnp.sum(mine, axis=1) * c * MOE_ROW_CHUNK
    cvalid = jnp.sum(mine * jnp.clip(plan["erows"][None, :] - c[:, None] * MOE_ROW_CHUNK, 0, MOE_ROW_CHUNK),
                     axis=1)
    return (first.astype(i32), count.astype(i32), cstart.astype(i32), cvalid.astype(i32),
            end[-1].reshape(1).astype(i32))


def _moe_experts(xs, plan, w_gate_up, b_gate_up, w_down, b_down):
    d = xs.shape[1]
    grid_spec = pltpu.PrefetchScalarGridSpec(
        num_scalar_prefetch=6,
        grid=(N_EXPERTS,),
        in_specs=[pl.BlockSpec(memory_space=pl.ANY),
                  pl.BlockSpec((1, d, 2 * D_FF), lambda e, *_: (e, 0, 0)),
                  pl.BlockSpec((1, 1, 2 * D_FF), lambda e, *_: (e, 0, 0)),
                  pl.BlockSpec((1, D_FF, d), lambda e, *_: (e, 0, 0)),
                  pl.BlockSpec((1, 1, d), lambda e, *_: (e, 0, 0))],
        out_specs=pl.BlockSpec(memory_space=pl.ANY),
        scratch_shapes=[pltpu.VMEM((d, 2 * D_FF), jnp.bfloat16),
                        pltpu.VMEM((D_FF, d), jnp.bfloat16),
                        pltpu.VMEM((3, MOE_ROW_CHUNK, d), jnp.bfloat16),
                        pltpu.VMEM((2, MOE_ROW_CHUNK, d), jnp.bfloat16),
                        pltpu.VMEM((MOE_PIECE, d), jnp.bfloat16),
                        pltpu.SemaphoreType.DMA((3,)),
                        pltpu.SemaphoreType.DMA((2,)),
                        pltpu.SemaphoreType.DMA(())])
    return pl.pallas_call(
        _experts_body,
        grid_spec=grid_spec,
        out_shape=jax.ShapeDtypeStruct(xs.shape, jnp.bfloat16),
        compiler_params=pltpu.CompilerParams(dimension_semantics=("arbitrary",),
                                             vmem_limit_bytes=VMEM_LIMIT_BYTES),
        name="moe_experts",
    )(*_expert_chunks(plan, xs.shape[0]), plan["used"],
      xs, w_gate_up, b_gate_up.reshape(N_EXPERTS, 1, 2 * D_FF), w_down, b_down.reshape(N_EXPERTS, 1, d))


def _combine_body(lstart_ref, units_ref, seg_ref, lp_ref,
                  os_ref, posg_ref, x_ref, fw_ref, yp_ref, ys_ref, stage_ref, w_ref, sems,
                  *, n_prompt_tiles):
    f32, bf16 = jnp.float32, jnp.bfloat16
    i = pl.program_id(0)
    nt = pl.num_programs(0)
    slot = i % 2
    tm = x_ref.shape[0]
    lmax = stage_ref.shape[1]

    def segment_copies(tile, slot_, fn):
        _for_each_segment_copy(tile, lstart_ref, units_ref, seg_ref, stage_ref.at[slot_], os_ref,
                               sems.at[slot_], False, fn)

    @pl.when(i == 0)
    def _():
        stage_ref[...] = jnp.zeros(stage_ref.shape, bf16)
        segment_copies(0, 0, lambda cp: cp.start())

    @pl.when(i + 1 < nt)
    def _():
        segment_copies(i + 1, 1 - slot, lambda cp: cp.start())

    posg = posg_ref[...]
    pos = [posg[:, k:k + 1] for k in range(TOP_K)]
    gate = [posg[:, TOP_K + k:TOP_K + k + 1] for k in range(TOP_K)]
    gate_bf = [g.astype(bf16) for g in gate]
    r_local = lax.broadcasted_iota(jnp.int32, (tm, MOE_CHUNK), 1).astype(f32).astype(bf16)
    segment_copies(i, slot, lambda cp: cp.wait())
    y = x_ref[...]
    part_cols = lmax // 2
    for part in range(2):
        for c in range(part * part_cols // MOE_CHUNK, (part + 1) * part_cols // MOE_CHUNK):
            w = jnp.zeros((tm, MOE_CHUNK), bf16)
            for k in range(TOP_K):
                loc = jnp.clip(pos[k] - c * MOE_CHUNK, -1.0, float(MOE_CHUNK)).astype(bf16)
                w = jnp.where(loc == r_local, gate_bf[k], w)
            w_ref[:, c * MOE_CHUNK:(c + 1) * MOE_CHUNK] = w
        y = y + jnp.dot(w_ref[:, part * part_cols:(part + 1) * part_cols],
                        stage_ref[slot, part * part_cols:(part + 1) * part_cols, :],
                        preferred_element_type=f32)
    out = y * lax.rsqrt(jnp.mean(y * y, axis=-1, keepdims=True) + EPS) * fw_ref[...]

    @pl.when(i < n_prompt_tiles)
    def _():
        yp_ref[...] = out

    @pl.when(i >= n_prompt_tiles)
    def _():
        ys_ref[...] = out


def _moe_combine(os_, posg, x1, final_norm_w, plan, tm, lmax, n_prompt):
    n, d = x1.shape
    nt = n // tm
    n_prompt_tiles = n_prompt // tm
    n_sample_tiles = nt - n_prompt_tiles
    grid_spec = pltpu.PrefetchScalarGridSpec(
        num_scalar_prefetch=4,
        grid=(nt,),
        in_specs=[pl.BlockSpec(memory_space=pl.ANY),
                  pl.BlockSpec((tm, LANES), lambda i, *_: (i, 0)),
                  pl.BlockSpec((tm, d), lambda i, *_: (i, 0)),
                  pl.BlockSpec((1, d), lambda i, *_: (0, 0))],
        out_specs=[pl.BlockSpec((tm, d), lambda i, *_: (jnp.minimum(i, n_prompt_tiles - 1), 0)),
                   pl.BlockSpec((tm, d), lambda i, *_: (jnp.maximum(i - n_prompt_tiles, 0), 0))],
        scratch_shapes=[pltpu.VMEM((2, lmax, d), jnp.bfloat16),
                        pltpu.VMEM((tm, lmax), jnp.bfloat16),
                        pltpu.SemaphoreType.DMA((2,))])
    return pl.pallas_call(
        functools.partial(_combine_body, n_prompt_tiles=n_prompt_tiles),
        grid_spec=grid_spec,
        out_shape=[jax.ShapeDtypeStruct((n_prompt, d), jnp.float32),
                   jax.ShapeDtypeStruct((n_sample_tiles * tm, d), jnp.float32)],
        compiler_params=pltpu.CompilerParams(dimension_semantics=("arbitrary",),
                                             vmem_limit_bytes=VMEM_LIMIT_BYTES),
        name="moe_combine",
    )(plan["lstart"], plan["seg_units"], plan["seg"], plan["lp"],
      os_, posg, x1, final_norm_w.reshape(1, d).astype(jnp.float32))


def _moe_block(x1, n_prompt, norm2_w, router_w, router_b, w_gate_up, b_gate_up, w_down, b_down,
               final_norm_w, tm=MOE_TOKEN_TILE):
    n = x1.shape[0]
    nt, lmax, n_rows = _moe_sizes(n, tm)
    h2, posg, post, cnt3 = _moe_router(x1, norm2_w, router_w, router_b, tm)
    plan = _moe_plan(cnt3[:, 0, :N_EXPERTS])
    xs = _moe_dispatch(h2, post, plan, tm, lmax, n_rows)
    os_ = _moe_experts(xs, plan, w_gate_up, b_gate_up, w_down, b_down)
    return _moe_combine(os_, posg, x1, final_norm_w, plan, tm, lmax, n_prompt)


def kernel(x_prompt, x_sample, state_ssm, state_conv, state_pool, norm1_w, w_in, conv_w, conv_b, dt_bias,
           A_log, D_skip, ssd_norm_w, pool_w, pool_scale, w_out, norm2_w, router_w, router_b, w_gate_up,
           b_gate_up, w_down, b_down, final_norm_w):
    n_prompt = BATCH * SEQ
    n_sample = DEC_BATCH * DEC_SEQ
    xp = x_prompt.reshape(n_prompt, D_MODEL)
    xs = x_sample.reshape(n_sample, D_MODEL)
    z, xbc, dt_raw, u = _in_proj(xp, xs, norm1_w[0], w_in[0])
    mp = (conv_w[0], conv_b[0], dt_bias[0], A_log[0], D_skip[0], ssd_norm_w[0], pool_w[0], pool_scale[0])
    mix_p, s1 = _prompt_mixer(z, xbc, dt_raw, u, BATCH, SEQ, *mp)
    mix_s, s2 = _sample_mixer(z, xbc, dt_raw, u, n_prompt, DEC_BATCH, state_conv, state_ssm[0], state_pool,
                              PAST_LEN, *mp)
    nk = CONV_WIDTH - 1
    c1 = jnp.stack([xbc[(b + 1) * SEQ - nk:(b + 1) * SEQ] for b in range(BATCH)])
    p1 = jnp.stack([u[(b + 1) * SEQ - POOL_HIST:(b + 1) * SEQ] for b in range(BATCH)])
    c2 = xbc[n_prompt:].reshape(DEC_BATCH, DEC_SEQ, D_CONV)[:, DEC_SEQ - nk:]
    p2 = jnp.concatenate([state_pool[0][:, DEC_SEQ:], u[n_prompt:].reshape(DEC_BATCH, DEC_SEQ, D_POOL)], axis=1)
    x1 = _out_proj(mix_p, mix_s, w_out[0], xp, xs)
    yp, ys = _moe_block(x1, n_prompt, norm2_w[0], router_w[0], router_b[0], w_gate_up[0], b_gate_up[0],
                        w_down[0], b_down[0], final_norm_w)
    return (yp.reshape(x_prompt.shape), ys.reshape(x_sample.shape),
            s1[None], c1[None], p1[None], s2[None], c2[None], p2[None])
```

```python
import functools
import math
import jax, jax.numpy as jnp
from jax import lax
import numpy as np
from jax.experimental import pallas as pl
from jax.experimental.pallas import tpu as pltpu

D_MODEL = 1024
BATCH = 8
SEQ = 2048
DEC_BATCH = 128
DEC_SEQ = 4
PAST_LEN = 16384

D_MIX = 2 * D_MODEL
D_SSD = 3 * D_MIX // 4
SSD_HEAD_DIM = 64
N_SSD_HEADS = D_SSD // SSD_HEAD_DIM
N_SSD_GROUPS = 4
D_STATE = 128
CONV_WIDTH = 4
SSD_CHUNK = 128
D_CONV = D_SSD + 2 * N_SSD_GROUPS * D_STATE
D_POOL = D_MIX - D_SSD
POOL_WINDOWS = (2, 4, 8, 16)
N_POOL_GROUPS = len(POOL_WINDOWS)
POOL_GROUP_DIM = D_POOL // N_POOL_GROUPS
POOL_HIST = max(POOL_WINDOWS) - 1
D_IN_PROJ = D_SSD + D_CONV + N_SSD_HEADS + D_POOL
N_EXPERTS = 32
TOP_K = 4
D_FF = D_MODEL
SWIGLU_LIMIT = 7.0
SWIGLU_ALPHA = 1.702
EPS = 1e-5

LANES = 128
BF16_SUBLANES = 16
VMEM_LIMIT_BYTES = 48 * 1024 * 1024

MOE_TOKEN_TILE = 512
MOE_SEG_ROWS = BF16_SUBLANES
MOE_PIECE = 128
MOE_ROW_CHUNK = 512
MOE_CHUNK = 256
MOE_SEL_ROWS = 512


BLK = SSD_CHUNK
PROJ_ROW_TILE = 512
PROMPT_BLKS_PER_STEP = 2
HIST_ROWS = 16
CONV_TAIL_ROWS = 8
NT_DIMS = (((1,), (1,)), ((), ()))


def _split2(v):
    hi = v.astype(jnp.bfloat16)
    lo = (v - hi.astype(jnp.float32)).astype(jnp.bfloat16)
    return hi, lo


def _dot_sel_left(sel, v, passes):
    out = None
    rem = v
    for p in range(passes):
        part = rem.astype(jnp.bfloat16)
        d = jnp.dot(sel, part, preferred_element_type=jnp.float32)
        out = d if out is None else out + d
        if p + 1 < passes:
            rem = rem - part.astype(jnp.float32)
    return out


def _dot_sel_right(v, sel, passes):
    out = None
    rem = v
    for p in range(passes):
        part = rem.astype(jnp.bfloat16)
        d = jnp.dot(part, sel, preferred_element_type=jnp.float32)
        out = d if out is None else out + d
        if p + 1 < passes:
            rem = rem - part.astype(jnp.float32)
    return out


def _two_part_specs(n_first, n_second, tm, width):
    t1 = n_first // tm
    t2 = n_second // tm
    return (pl.BlockSpec((tm, width), lambda i: (jnp.minimum(i, t1 - 1), 0)),
            pl.BlockSpec((tm, width), lambda i: (jnp.clip(i - t1, 0, t2 - 1), 0)))


IN_PROJ_COLS = 512


def _in_proj_body(xa_ref, xb_ref, nw_ref, wmain_ref, wtail_ref, z_ref, xbc_ref, dt_ref, u_ref, *, tiles_a):
    f32, bf16 = jnp.float32, jnp.bfloat16
    x = jnp.where(pl.program_id(0) < tiles_a, xa_ref[...], xb_ref[...])
    h = (x * lax.rsqrt(jnp.mean(x * x, axis=-1, keepdims=True) + EPS) * nw_ref[...]).astype(bf16)
    off = 0
    for ref in (z_ref, xbc_ref):
        for c0 in range(0, ref.shape[1], IN_PROJ_COLS):
            w = wmain_ref[0, :, off + c0:off + c0 + IN_PROJ_COLS].astype(bf16)
            ref[:, c0:c0 + IN_PROJ_COLS] = jnp.dot(h, w, preferred_element_type=f32)
        off += ref.shape[1]
    dt_ref[...] = jnp.dot(h, wtail_ref[:, 0:LANES], preferred_element_type=f32)
    u_ref[...] = jnp.dot(h, wtail_ref[:, LANES:], preferred_element_type=f32)


def _in_proj(xa, xb, norm1_w, w_in3):
    d = xa.shape[1]
    n = xa.shape[0] + xb.shape[0]
    f32, bf16 = jnp.float32, jnp.bfloat16
    s1, s2 = D_SSD + D_CONV, D_SSD + D_CONV + N_SSD_HEADS
    wtail = jnp.concatenate([w_in3[0, :, s1:s2], jnp.zeros((d, LANES - N_SSD_HEADS), f32), w_in3[0, :, s2:]],
                            axis=1).astype(bf16)
    widths = (D_SSD, D_CONV, LANES, D_POOL)
    tm = PROJ_ROW_TILE
    return pl.pallas_call(
        functools.partial(_in_proj_body, tiles_a=xa.shape[0] // tm),
        grid=(n // tm,),
        in_specs=[*_two_part_specs(xa.shape[0], xb.shape[0], tm, d),
                  pl.BlockSpec((1, d), lambda i: (0, 0)),
                  pl.BlockSpec((1, d, s1), lambda i: (0, 0, 0), pipeline_mode=pl.Buffered(1)),
                  pl.BlockSpec((d, LANES + D_POOL), lambda i: (0, 0), pipeline_mode=pl.Buffered(1))],
        out_specs=[pl.BlockSpec((tm, wd), lambda i: (i, 0)) for wd in widths],
        out_shape=[jax.ShapeDtypeStruct((n, wd), f32) for wd in widths],
        compiler_params=pltpu.CompilerParams(dimension_semantics=("parallel",),
                                             vmem_limit_bytes=VMEM_LIMIT_BYTES),
        name="in_proj",
    )(xa, xb, norm1_w.reshape(1, d).astype(f32), w_in3, wtail)


def _out_proj_body(ma_ref, mb_ref, w_ref, xa_ref, xb_ref, o_ref, *, tiles_a):
    first = pl.program_id(0) < tiles_a
    m = jnp.where(first, ma_ref[...], mb_ref[...])
    x = jnp.where(first, xa_ref[...], xb_ref[...])
    half = w_ref.shape[0] // 2
    o_ref[...] = (x + jnp.dot(m[:, :half], w_ref[:half, :].astype(jnp.bfloat16),
                              preferred_element_type=jnp.float32)
                  + jnp.dot(m[:, half:], w_ref[half:, :].astype(jnp.bfloat16),
                            preferred_element_type=jnp.float32))


def _out_proj(ma, mb, w_out, xa, xb):
    d = xa.shape[1]
    n = xa.shape[0] + xb.shape[0]
    tm = PROJ_ROW_TILE
    return pl.pallas_call(
        functools.partial(_out_proj_body, tiles_a=xa.shape[0] // tm),
        grid=(n // tm,),
        in_specs=[*_two_part_specs(xa.shape[0], xb.shape[0], tm, D_MIX),
                  pl.BlockSpec((D_MIX, d), lambda i: (0, 0), pipeline_mode=pl.Buffered(1)),
                  *_two_part_specs(xa.shape[0], xb.shape[0], tm, d)],
        out_specs=pl.BlockSpec((tm, d), lambda i: (i, 0)),
        out_shape=jax.ShapeDtypeStruct((n, d), jnp.float32),
        compiler_params=pltpu.CompilerParams(dimension_semantics=("parallel",),
                                             vmem_limit_bytes=VMEM_LIMIT_BYTES),
        name="out_proj",
    )(ma, mb, w_out.astype(jnp.float32), xa, xb)


def _mixer_constants():
    bf16 = jnp.bfloat16
    h = np.arange(LANES)[:, None]
    ch = np.arange(D_SSD)[None, :]
    expand = (ch // SSD_HEAD_DIM == h).astype(np.float32)
    i = np.arange(BLK)[:, None]
    j = np.arange(BLK)[None, :]
    causal = (j <= i).astype(np.float32)
    return dict(expand=jnp.asarray(expand, bf16), expand_t=jnp.asarray(expand.T, bf16),
                causal=jnp.asarray(causal, bf16))


def _softplus(x):
    return jnp.maximum(x, 0.0) + jnp.log(1.0 + jnp.exp(-jnp.abs(x)))


def _conv_silu(ext_ref, cw_ref, cb_ref, first_row):
    ext = ext_ref[...]
    last = first_row + CONV_WIDTH - 1
    acc = cb_ref[...] + cw_ref[CONV_WIDTH - 1:CONV_WIDTH, :] * ext[last:last + BLK, :]
    for k in range(CONV_WIDTH - 1):
        tap = pltpu.roll(ext, CONV_WIDTH - 1 - k, axis=0)[last:last + BLK, :]
        acc = acc + cw_ref[k:k + 1, :] * tap
    return acc * jax.nn.sigmoid(acc)


def _ssd_intra(xbc_c, dt_raw, dtb_ref, alog_ref, causal_bf, expand_ref):
    f32 = jnp.float32
    xs = xbc_c[:, :D_SSD]
    bm = xbc_c[:, D_SSD:D_SSD + N_SSD_GROUPS * D_STATE]
    cm = xbc_c[:, D_SSD + N_SSD_GROUPS * D_STATE:]
    dt = _softplus(dt_raw + dtb_ref[...])
    a = dt * (-jnp.exp(alog_ref[...]))
    a_cum = _dot_sel_left(causal_bf, a, 3)
    dt_x = _dot_sel_right(dt, expand_ref[...], 2)
    return xs, bm, cm, dt, a_cum, xs * dt_x


def _ssd_diag_group(g, cb, a_cum, a_cum_t, keep, xdt):
    f32, bf16 = jnp.float32, jnp.bfloat16
    hg = N_SSD_HEADS // N_SSD_GROUPS
    lane = lax.broadcasted_iota(jnp.int32, (BLK, LANES), 1)
    first_head = lane < SSD_HEAD_DIM
    neg = jnp.float32(-jnp.inf)
    outs = []
    for pr in range(hg * SSD_HEAD_DIM // LANES):
        h1 = g * hg + 2 * pr
        blk = (g * hg * SSD_HEAD_DIM) // LANES + pr
        xp = xdt[:, blk * LANES:(blk + 1) * LANES]
        x1 = jnp.where(first_head, xp, 0.0).astype(bf16)
        x2 = jnp.where(first_head, 0.0, xp).astype(bf16)
        m1 = (cb * jnp.exp(jnp.where(keep, a_cum[:, h1:h1 + 1] - a_cum_t[h1:h1 + 1, :], neg))).astype(bf16)
        m2 = (cb * jnp.exp(jnp.where(keep, a_cum[:, h1 + 1:h1 + 2] - a_cum_t[h1 + 1:h1 + 2, :], neg))).astype(bf16)
        outs.append(jnp.dot(m1, x1, preferred_element_type=f32) + jnp.dot(m2, x2, preferred_element_type=f32))
    return jnp.concatenate(outs, axis=1)


def _gated_norm(y, z, nw_ref):
    yg = y * (z * jax.nn.sigmoid(z))
    return yg * lax.rsqrt(jnp.mean(yg * yg, axis=-1, keepdims=True) + EPS) * nw_ref[...]


def _prompt_mixer_body(z_ref, xbc_ref, dt_ref, u_ref, cw_ref, cb_ref, dtb_ref, alog_ref, dskip_ref, nw_ref,
                       pw_ref, ps_ref, causal_ref,
                       mix_ref, ssm_ref, ext_ref, pool_tail_ref, state_ref):
    n_blk = pl.num_programs(1) * PROMPT_BLKS_PER_STEP
    for ci in range(PROMPT_BLKS_PER_STEP):
        rows = pl.ds(ci * BLK, BLK)
        _prompt_block(pl.program_id(1) * PROMPT_BLKS_PER_STEP + ci, n_blk,
                      z_ref.at[rows], xbc_ref.at[rows], dt_ref.at[rows], u_ref.at[rows],
                      cw_ref, cb_ref, dtb_ref, alog_ref, dskip_ref, nw_ref, pw_ref, ps_ref, causal_ref,
                      mix_ref.at[rows], ssm_ref, ext_ref, pool_tail_ref, state_ref)


def _prompt_block(c, n_blk, z_ref, xbc_ref, dt_ref, u_ref, cw_ref, cb_ref, dtb_ref, alog_ref, dskip_ref, nw_ref,
                  pw_ref, ps_ref, causal_ref,
                  mix_ref, ssm_ref, ext_ref, pool_tail_ref, state_ref):
    f32, bf16 = jnp.float32, jnp.bfloat16
    gw = D_SSD // N_SSD_GROUPS

    @pl.when(c == 0)
    def _():
        ext_ref[0:CONV_TAIL_ROWS, :] = jnp.zeros((CONV_TAIL_ROWS, D_CONV), f32)
        pool_tail_ref[...] = jnp.zeros(pool_tail_ref.shape, f32)
        state_ref[...] = jnp.zeros(state_ref.shape, f32)

    u = u_ref[...]
    ext_u = jnp.concatenate([pool_tail_ref[...], u], axis=0)
    pos = (c * BLK + lax.broadcasted_iota(jnp.int32, (BLK, 1), 0) + 1).astype(f32)
    for gi, w in enumerate(POOL_WINDOWS):
        assert w & (w - 1) == 0 and w <= HIST_ROWS
        sl = slice(gi * POOL_GROUP_DIM, (gi + 1) * POOL_GROUP_DIM)
        ug = u[:, sl]
        acc = ext_u[:, sl]
        span = 1
        while span < w:
            acc = acc + pltpu.roll(acc, span, axis=0)
            span *= 2
        wsum = acc[HIST_ROWS:, :]
        pooled = wsum / jnp.minimum(pos, jnp.float32(w)) - ug
        po = jnp.dot(pooled.astype(bf16), pw_ref[gi], preferred_element_type=f32) * ps_ref[:, sl]
        mix_ref[:, D_SSD + gi * POOL_GROUP_DIM:D_SSD + (gi + 1) * POOL_GROUP_DIM] = po.astype(bf16)
    pool_tail_ref[...] = u_ref[BLK - HIST_ROWS:BLK, :]

    ext_ref[CONV_TAIL_ROWS:CONV_TAIL_ROWS + BLK, :] = xbc_ref[...]
    xbc_c = _conv_silu(ext_ref, cw_ref, cb_ref, CONV_TAIL_ROWS - (CONV_WIDTH - 1))
    ext_ref[0:CONV_TAIL_ROWS, :] = xbc_ref[BLK - CONV_TAIL_ROWS:BLK, :]

    causal_bf = causal_ref[...]
    keep = causal_bf > 0
    xs = xbc_c[:, :D_SSD]
    bm = xbc_c[:, D_SSD:D_SSD + N_SSD_GROUPS * D_STATE]
    cm = xbc_c[:, D_SSD + N_SSD_GROUPS * D_STATE:]
    dt = _softplus(dt_ref[...] + dtb_ref[...])
    a_cum = dt * (-jnp.exp(alog_ref[...]))
    row = lax.broadcasted_iota(jnp.int32, (BLK, LANES), 0)
    span = 1
    while span < BLK:
        a_cum = a_cum + jnp.where(row >= span, pltpu.roll(a_cum, span, axis=0), 0.0)
        span *= 2
    a_cum_t = jnp.transpose(a_cum)
    a_tot = a_cum[BLK - 1:BLK, :]
    ea = jnp.exp(a_cum)
    dte = jnp.exp(a_tot - a_cum)
    cd = jnp.exp(a_tot)
    hg = N_SSD_HEADS // N_SSD_GROUPS
    first_head = lax.broadcasted_iota(jnp.int32, (BLK, LANES), 1) < SSD_HEAD_DIM
    neg = jnp.float32(-jnp.inf)

    def head_cols(v, h1):
        return jnp.where(first_head, v[:, h1:h1 + 1], v[:, h1 + 1:h1 + 2])

    def decay_from(h):
        return jnp.exp(jnp.where(keep, a_cum[:, h:h + 1] - a_cum_t[h:h + 1, :], neg))

    y_parts = []
    for g in range(N_SSD_GROUPS):
        cg = cm[:, g * D_STATE:(g + 1) * D_STATE].astype(bf16)
        bg = bm[:, g * D_STATE:(g + 1) * D_STATE].astype(bf16)
        cb = lax.dot_general(cg, bg, NT_DIMS, preferred_element_type=f32)
        sg = state_ref[g * gw:(g + 1) * gw, :]
        y_off = lax.dot_general(cg, sg.astype(bf16), NT_DIMS, preferred_element_type=f32)
        xdte_parts = []
        for pr in range(gw // LANES):
            h1 = g * hg + 2 * pr
            sl = slice(h1 * SSD_HEAD_DIM, h1 * SSD_HEAD_DIM + LANES)
            xp = xs[:, sl] * head_cols(dt, h1)
            x1 = jnp.where(first_head, xp, 0.0).astype(bf16)
            x2 = jnp.where(first_head, 0.0, xp).astype(bf16)
            y_diag = (jnp.dot((cb * decay_from(h1)).astype(bf16), x1, preferred_element_type=f32)
                      + jnp.dot((cb * decay_from(h1 + 1)).astype(bf16), x2, preferred_element_type=f32))
            y_parts.append(y_diag + y_off[:, pr * LANES:(pr + 1) * LANES] * head_cols(ea, h1)
                           + xs[:, sl] * dskip_ref[:, sl])
            xdte_parts.append(xp * head_cols(dte, h1))
        xdte_t = jnp.transpose(jnp.concatenate(xdte_parts, axis=1)).astype(bf16)
        cd_rows = jnp.concatenate([jnp.broadcast_to(cd[:, h:h + 1], (SSD_HEAD_DIM, D_STATE))
                                   for h in range(g * hg, (g + 1) * hg)], axis=0)
        state_ref[g * gw:(g + 1) * gw, :] = sg * cd_rows + jnp.dot(xdte_t, bg, preferred_element_type=f32)
    y = jnp.concatenate(y_parts, axis=1)
    mix_ref[:, 0:D_SSD] = _gated_norm(y, z_ref[...], nw_ref).astype(bf16)

    @pl.when(c == n_blk - 1)
    def _():
        ssm_ref[0] = state_ref[...].reshape(N_SSD_HEADS, SSD_HEAD_DIM, D_STATE)


def _prompt_mixer(z, xbc, dt, u, n_seq, seq_len, conv_w, conv_b, dt_bias, A_log, D_skip, ssd_norm_w, pool_w,
                  pool_scale):
    f32, bf16 = jnp.float32, jnp.bfloat16
    n = n_seq * seq_len
    step_rows = PROMPT_BLKS_PER_STEP * BLK
    n_steps = seq_len // step_rows
    k = _mixer_constants()

    def row_blk(width):
        return pl.BlockSpec((step_rows, width), lambda b, c: (b * n_steps + c, 0))

    def const(shape):
        return pl.BlockSpec(shape, lambda b, c: (0,) * len(shape))

    pad_h = (0, LANES - N_SSD_HEADS)
    return pl.pallas_call(
        _prompt_mixer_body,
        grid=(n_seq, n_steps),
        in_specs=[row_blk(D_SSD), row_blk(D_CONV), row_blk(LANES), row_blk(D_POOL),
                  const((CONV_WIDTH, D_CONV)), const((1, D_CONV)), const((1, LANES)), const((1, LANES)),
                  const((1, D_SSD)), const((1, D_SSD)),
                  const((N_POOL_GROUPS, POOL_GROUP_DIM, POOL_GROUP_DIM)), const((1, D_POOL)),
                  const((BLK, BLK))],
        out_specs=[pl.BlockSpec((step_rows, D_MIX), lambda b, c: (b * n_steps + c, 0)),
                   pl.BlockSpec((1, N_SSD_HEADS, SSD_HEAD_DIM, D_STATE), lambda b, c: (b, 0, 0, 0))],
        out_shape=[jax.ShapeDtypeStruct((n, D_MIX), bf16),
                   jax.ShapeDtypeStruct((n_seq, N_SSD_HEADS, SSD_HEAD_DIM, D_STATE), f32)],
        scratch_shapes=[pltpu.VMEM((CONV_TAIL_ROWS + BLK, D_CONV), f32),
                        pltpu.VMEM((HIST_ROWS, D_POOL), f32),
                        pltpu.VMEM((D_SSD, D_STATE), f32)],
        compiler_params=pltpu.CompilerParams(dimension_semantics=("parallel", "arbitrary"),
                                             vmem_limit_bytes=VMEM_LIMIT_BYTES),
        name="prompt_mixer",
    )(z, xbc, dt, u, conv_w.astype(f32), conv_b.reshape(1, D_CONV).astype(f32),
      jnp.pad(dt_bias.astype(f32), pad_h).reshape(1, LANES), jnp.pad(A_log.astype(f32), pad_h).reshape(1, LANES),
      jnp.repeat(D_skip.astype(f32), SSD_HEAD_DIM).reshape(1, D_SSD), ssd_norm_w.reshape(1, D_SSD).astype(f32),
      pool_w.astype(bf16), pool_scale.reshape(1, D_POOL).astype(f32),
      k["causal"])


SEQ_PER_BLK = BLK // DEC_SEQ
SEQ_PER_STEP = 8


def _sample_constants():
    bf16 = jnp.bfloat16
    r = np.arange(BLK)
    sq, st = r // DEC_SEQ, r % DEC_SEQ
    same = sq[:, None] == sq[None, :]
    causal = same & (st[None, :] <= st[:, None])
    nk = CONV_WIDTH - 1
    shift = np.stack([same & (st[None, :] == st[:, None] + k - nk) for k in range(nk)])
    cs = np.arange(SEQ_PER_BLK * nk)
    stsel = np.stack([(cs[None, :] // nk == sq[:, None]) & (cs[None, :] % nk == st[:, None] + k)
                      for k in range(nk)])
    pcur = np.stack([causal & (st[:, None] - st[None, :] < w) for w in POOL_WINDOWS])
    hs = np.arange(SEQ_PER_BLK * POOL_HIST)
    phist = np.stack([(hs[None, :] // POOL_HIST == sq[:, None])
                      & (st[:, None] + POOL_HIST - hs[None, :] % POOL_HIST < w) for w in POOL_WINDOWS])
    as_bf = lambda a: jnp.asarray(a.astype(np.float32), bf16)
    return dict(same=as_bf(same), causal=as_bf(causal), shift=as_bf(shift), stsel=as_bf(stsel),
                pcur=as_bf(pcur), phist=as_bf(phist))


def _sample_mixer_body(z_ref, xbc_ref, dt_ref, u_ref, cst_ref, pst_ref, ssm_in_ref,
                       cw_ref, cb_ref, dtb_ref, alog_ref, dskip_ref, nw_ref, pw_ref, ps_ref,
                       causal_ref, same_ref, expand_ref, expand_t_ref, shift_ref, stsel_ref, pcur_ref, phist_ref,
                       mix_ref, ssm_out_ref,
                       ydiag_ref, ea_ref, yt_ref, cdh_ref, cdl_ref, xdte_t_ref, bm_ref, cm_ref, *, pos0):
    f32, bf16 = jnp.float32, jnp.bfloat16
    s = pl.program_id(1)
    gw = D_SSD // N_SSD_GROUPS

    @pl.when(s == 0)
    def _():
        xbc = xbc_ref[...]
        cst = cst_ref[...]
        acc = cb_ref[...] + cw_ref[CONV_WIDTH - 1:CONV_WIDTH, :] * xbc
        for k in range(CONV_WIDTH - 1):
            tap = _dot_sel_left(shift_ref[k], xbc, 3) + _dot_sel_left(stsel_ref[k], cst, 3)
            acc = acc + cw_ref[k:k + 1, :] * tap
        xbc_c = acc * jax.nn.sigmoid(acc)

        causal_bf = causal_ref[...]
        keep = causal_bf > 0
        xs, bm, cm, dt, a_cum, xdt = _ssd_intra(xbc_c, dt_ref[...], dtb_ref, alog_ref, causal_bf, expand_ref)
        a_tot = _dot_sel_left(same_ref[...], dt * (-jnp.exp(alog_ref[...])), 3)
        a_cum_t = jnp.transpose(a_cum)
        ea_ref[...] = _dot_sel_right(jnp.exp(a_cum), expand_ref[...], 2)
        dte_x = _dot_sel_right(jnp.exp(a_tot - a_cum), expand_ref[...], 2)
        cd_col = _dot_sel_left(expand_t_ref[...], jnp.exp(jnp.transpose(a_tot)), 2)
        cd_hi, cd_lo = _split2(cd_col)
        cdh_ref[...] = cd_hi
        cdl_ref[...] = cd_lo
        bm_ref[...] = bm.astype(bf16)
        cm_ref[...] = cm.astype(bf16)
        for g in range(N_SSD_GROUPS):
            cg = cm[:, g * D_STATE:(g + 1) * D_STATE].astype(bf16)
            bg = bm[:, g * D_STATE:(g + 1) * D_STATE].astype(bf16)
            cb = lax.dot_general(cg, bg, NT_DIMS, preferred_element_type=f32)
            y_diag = _ssd_diag_group(g, cb, a_cum, a_cum_t, keep, xdt)
            ydiag_ref[:, g * gw:(g + 1) * gw] = y_diag + xs[:, g * gw:(g + 1) * gw] * dskip_ref[:, g * gw:(g + 1) * gw]
            xdte_t_ref[g * gw:(g + 1) * gw, :] = jnp.transpose(
                xdt[:, g * gw:(g + 1) * gw] * dte_x[:, g * gw:(g + 1) * gw]).astype(bf16)
        yt_ref[...] = jnp.zeros(yt_ref.shape, f32)

        u = u_ref[...]
        pst = pst_ref[...]
        step = lax.broadcasted_iota(jnp.int32, (BLK, 1), 0) % DEC_SEQ
        pos = (step + (pos0 + 1)).astype(f32)
        for gi, w in enumerate(POOL_WINDOWS):
            sl = slice(gi * POOL_GROUP_DIM, (gi + 1) * POOL_GROUP_DIM)
            ug = u[:, sl]
            wsum = _dot_sel_left(pcur_ref[gi], ug, 2) + _dot_sel_left(phist_ref[gi], pst[:, sl], 2)
            pooled = wsum / jnp.minimum(pos, jnp.float32(w)) - ug
            po = jnp.dot(pooled.astype(bf16), pw_ref[gi], preferred_element_type=f32) * ps_ref[:, sl]
            mix_ref[:, D_SSD + gi * POOL_GROUP_DIM:D_SSD + (gi + 1) * POOL_GROUP_DIM] = po.astype(bf16)

    hg = N_SSD_HEADS // N_SSD_GROUPS
    row_seq = lax.broadcasted_iota(jnp.int32, (BLK, LANES), 0) // DEC_SEQ
    col_seq = lax.broadcasted_iota(jnp.int32, (gw, BLK), 1) // DEC_SEQ
    row_idx = lax.broadcasted_iota(jnp.int32, (BLK, LANES), 0)
    for q in range(SEQ_PER_STEP):
        sq = s * SEQ_PER_STEP + q
        rows_of_s = row_seq == sq
        cols_of_s = col_seq == sq
        pick_s = jnp.where(row_idx == DEC_SEQ * sq, 1.0, 0.0).astype(bf16)
        state = ssm_in_ref[q].reshape(D_SSD, D_STATE)
        for g in range(N_SSD_GROUPS):
            rs = slice(g * gw, (g + 1) * gw)
            sg = state[rs, :]
            cg = cm_ref[:, g * D_STATE:(g + 1) * D_STATE]
            bg = bm_ref[:, g * D_STATE:(g + 1) * D_STATE]
            yt = lax.dot_general(sg.astype(bf16), cg, NT_DIMS, preferred_element_type=f32)
            yt_ref[rs, :] += jnp.where(cols_of_s, yt, 0.0)
            cd = (jnp.dot(cdh_ref[rs, :], pick_s, preferred_element_type=f32)
                  + jnp.dot(cdl_ref[rs, :], pick_s, preferred_element_type=f32))
            upd = jnp.dot(xdte_t_ref[rs, :], jnp.where(rows_of_s, bg, jnp.zeros_like(bg)),
                          preferred_element_type=f32)
            ssm_out_ref[q, g * hg:(g + 1) * hg] = (sg * cd + upd).reshape(hg, SSD_HEAD_DIM, D_STATE)

    @pl.when(s == pl.num_programs(1) - 1)
    def _():
        y = ydiag_ref[...] + jnp.transpose(yt_ref[...]) * ea_ref[...]
        mix_ref[:, 0:D_SSD] = _gated_norm(y, z_ref[...], nw_ref).astype(bf16)


def _sample_mixer(z, xbc, dt, u, row0, n_seq, state_conv, state_ssm, state_pool, pos0,
                  conv_w, conv_b, dt_bias, A_log, D_skip, ssd_norm_w, pool_w, pool_scale):
    f32, bf16 = jnp.float32, jnp.bfloat16
    n_blk = n_seq // SEQ_PER_BLK
    blk0 = row0 // BLK
    nk = CONV_WIDTH - 1
    k = _mixer_constants()
    ks = _sample_constants()

    def row_blk(width):
        return pl.BlockSpec((BLK, width), lambda j, s: (blk0 + j, 0))

    def const(shape):
        return pl.BlockSpec(shape, lambda j, s: (0,) * len(shape))

    steps = SEQ_PER_BLK // SEQ_PER_STEP
    state_spec = pl.BlockSpec((SEQ_PER_STEP, N_SSD_HEADS, SSD_HEAD_DIM, D_STATE),
                              lambda j, s: (j * steps + s, 0, 0, 0))
    pad_h = (0, LANES - N_SSD_HEADS)
    return pl.pallas_call(
        functools.partial(_sample_mixer_body, pos0=pos0),
        grid=(n_blk, steps),
        in_specs=[row_blk(D_SSD), row_blk(D_CONV), row_blk(LANES), row_blk(D_POOL),
                  pl.BlockSpec((SEQ_PER_BLK * nk, D_CONV), lambda j, s: (j, 0)),
                  pl.BlockSpec((SEQ_PER_BLK * POOL_HIST, D_POOL), lambda j, s: (j, 0)),
                  state_spec,
                  const((CONV_WIDTH, D_CONV)), const((1, D_CONV)), const((1, LANES)), const((1, LANES)),
                  const((1, D_SSD)), const((1, D_SSD)),
                  const((N_POOL_GROUPS, POOL_GROUP_DIM, POOL_GROUP_DIM)), const((1, D_POOL)),
                  const((BLK, BLK)), const((BLK, BLK)), const((LANES, D_SSD)), const((D_SSD, LANES)),
                  const((nk, BLK, BLK)), const((nk, BLK, SEQ_PER_BLK * nk)),
                  const((N_POOL_GROUPS, BLK, BLK)), const((N_POOL_GROUPS, BLK, SEQ_PER_BLK * POOL_HIST))],
        out_specs=[pl.BlockSpec((BLK, D_MIX), lambda j, s: (j, 0)), state_spec],
        out_shape=[jax.ShapeDtypeStruct((n_seq * DEC_SEQ, D_MIX), bf16),
                   jax.ShapeDtypeStruct((n_seq, N_SSD_HEADS, SSD_HEAD_DIM, D_STATE), f32)],
        scratch_shapes=[pltpu.VMEM((BLK, D_SSD), f32), pltpu.VMEM((BLK, D_SSD), f32),
                        pltpu.VMEM((D_SSD, BLK), f32), pltpu.VMEM((D_SSD, BLK), bf16),
                        pltpu.VMEM((D_SSD, BLK), bf16), pltpu.VMEM((D_SSD, BLK), bf16),
                        pltpu.VMEM((BLK, N_SSD_GROUPS * D_STATE), bf16),
                        pltpu.VMEM((BLK, N_SSD_GROUPS * D_STATE), bf16)],
        compiler_params=pltpu.CompilerParams(dimension_semantics=("parallel", "arbitrary"),
                                             vmem_limit_bytes=VMEM_LIMIT_BYTES),
        name="sample_mixer",
    )(z, xbc, dt, u, state_conv.reshape(n_seq * nk, D_CONV), state_pool.reshape(n_seq * POOL_HIST, D_POOL),
      state_ssm, conv_w.astype(f32), conv_b.reshape(1, D_CONV).astype(f32),
      jnp.pad(dt_bias.astype(f32), pad_h).reshape(1, LANES), jnp.pad(A_log.astype(f32), pad_h).reshape(1, LANES),
      jnp.repeat(D_skip.astype(f32), SSD_HEAD_DIM).reshape(1, D_SSD), ssd_norm_w.reshape(1, D_SSD).astype(f32),
      pool_w.astype(bf16), pool_scale.reshape(1, D_POOL).astype(f32),
      ks["causal"], ks["same"], k["expand"], k["expand_t"], ks["shift"], ks["stsel"],
      ks["pcur"], ks["phist"])


def _moe_sizes(n_tokens, tm):
    nt = n_tokens // tm
    lmax = -(-(TOP_K * tm + N_EXPERTS * (MOE_SEG_ROWS - 1)) // MOE_SEL_ROWS) * MOE_SEL_ROWS
    rows = (TOP_K * n_tokens + nt * N_EXPERTS * (MOE_SEG_ROWS - 1) + N_EXPERTS * (MOE_PIECE - 1)
            + MOE_ROW_CHUNK)
    n_rows = -(-rows // MOE_PIECE) * MOE_PIECE
    return nt, lmax, n_rows


def _router_body(x_ref, nw_ref, rwh_ref, rwl_ref, rb_ref, h_ref, posg_ref, post_ref, cnt_ref):
    f32, bf16 = jnp.float32, jnp.bfloat16
    tm = x_ref.shape[0]
    x = x_ref[...]
    h = x * lax.rsqrt(jnp.mean(x * x, axis=-1, keepdims=True) + EPS) * nw_ref[...]
    h_hi = h.astype(bf16)
    h_ref[...] = h_hi
    h_lo = (h - h_hi.astype(f32)).astype(bf16)
    wh = rwh_ref[...]
    logits = (jnp.dot(h_hi, wh, preferred_element_type=f32)
              + jnp.dot(h_lo, wh, preferred_element_type=f32)
              + jnp.dot(h_hi, rwl_ref[...], preferred_element_type=f32)) + rb_ref[...]
    lane = lax.broadcasted_iota(jnp.int32, (tm, LANES), 1)
    lanef = lane.astype(f32)
    neg = jnp.float32(-jnp.inf)
    l = jnp.where(lane < N_EXPERTS, logits, neg)
    sels, vals = [], []
    for _ in range(TOP_K):
        m = jnp.max(l, axis=1, keepdims=True)
        idx = jnp.min(jnp.where(l == m, lanef, jnp.float32(LANES)), axis=1, keepdims=True)
        sel = lanef == idx
        l = jnp.where(sel, neg, l)
        sels.append(sel)
        vals.append(m)
    exps = [jnp.exp(v - vals[0]) for v in vals]
    denom = exps[0] + exps[1] + exps[2] + exps[3]
    gates = [e / denom for e in exps]
    chosen = jnp.where(sels[0] | sels[1] | sels[2] | sels[3], 1.0, 0.0).astype(f32)
    row = lax.broadcasted_iota(jnp.int32, (tm, tm), 0)
    col = lax.broadcasted_iota(jnp.int32, (tm, tm), 1)
    lower = jnp.where(col < row, 1.0, 0.0).astype(bf16)
    rank = jnp.dot(lower, chosen.astype(bf16), preferred_element_type=f32)
    cnt = jnp.sum(chosen, axis=0, keepdims=True)
    seg_units = jnp.floor((cnt + (MOE_SEG_ROWS - 1)) * (1.0 / MOE_SEG_ROWS))
    r2 = lax.broadcasted_iota(jnp.int32, (LANES, LANES), 0)
    c2 = lax.broadcasted_iota(jnp.int32, (LANES, LANES), 1)
    upper = jnp.where(r2 < c2, 1.0, 0.0).astype(bf16)
    lstart = jnp.dot(jnp.broadcast_to(seg_units, (8, LANES)).astype(bf16), upper,
                     preferred_element_type=f32)[0:1, :] * MOE_SEG_ROWS
    posmat = lstart + rank
    posg = jnp.zeros((tm, LANES), f32)
    for k in range(TOP_K):
        pos_k = jnp.sum(jnp.where(sels[k], posmat, 0.0), axis=1, keepdims=True)
        posg = posg + jnp.where(lane == k, pos_k, 0.0) + jnp.where(lane == TOP_K + k, gates[k], 0.0)
    posg_ref[...] = posg
    post_ref[...] = jnp.transpose(posg)[0:8, :]
    cnt_ref[0] = jnp.broadcast_to(cnt, (8, LANES)).astype(jnp.int32)


def _moe_router(x1, norm2_w, router_w, router_b, tm):
    n, d = x1.shape
    nt = n // tm
    f32, bf16 = jnp.float32, jnp.bfloat16
    rw = jnp.pad(router_w.astype(f32), ((0, 0), (0, LANES - N_EXPERTS)))
    rw_hi = rw.astype(bf16)
    rw_lo = (rw - rw_hi.astype(f32)).astype(bf16)
    rb = jnp.pad(router_b.astype(f32), (0, LANES - N_EXPERTS)).reshape(1, LANES)
    return pl.pallas_call(
        _router_body,
        grid=(nt,),
        in_specs=[pl.BlockSpec((tm, d), lambda i: (i, 0)),
                  pl.BlockSpec((1, d), lambda i: (0, 0)),
                  pl.BlockSpec((d, LANES), lambda i: (0, 0)),
                  pl.BlockSpec((d, LANES), lambda i: (0, 0)),
                  pl.BlockSpec((1, LANES), lambda i: (0, 0))],
        out_specs=[pl.BlockSpec((tm, d), lambda i: (i, 0)),
                   pl.BlockSpec((tm, LANES), lambda i: (i, 0)),
                   pl.BlockSpec((8, tm), lambda i: (0, i)),
                   pl.BlockSpec((1, 8, LANES), lambda i: (i, 0, 0))],
        out_shape=[jax.ShapeDtypeStruct((n, d), bf16),
                   jax.ShapeDtypeStruct((n, LANES), f32),
                   jax.ShapeDtypeStruct((8, n), f32),
                   jax.ShapeDtypeStruct((nt, 8, LANES), jnp.int32)],
        compiler_params=pltpu.CompilerParams(dimension_semantics=("parallel",),
                                             vmem_limit_bytes=VMEM_LIMIT_BYTES),
        name="moe_router",
    )(x1, norm2_w.reshape(1, d).astype(f32), rw_hi, rw_lo, rb)


def _moe_plan(cnt):
    i32 = jnp.int32
    pad = (cnt + (MOE_SEG_ROWS - 1)) // MOE_SEG_ROWS * MOE_SEG_ROWS
    lstart = jnp.cumsum(pad, axis=1) - pad
    lp = jnp.sum(pad, axis=1)
    tot = jnp.sum(pad, axis=0)
    reg = (tot + (MOE_PIECE - 1)) // MOE_PIECE * MOE_PIECE
    reg_end = jnp.cumsum(reg)
    estart = reg_end - reg
    seg = estart[None, :] + jnp.cumsum(pad, axis=0) - pad
    return dict(
        lstart=lstart.reshape(-1).astype(i32), seg_units=(pad // MOE_SEG_ROWS).reshape(-1).astype(i32),
        seg=seg.reshape(-1).astype(i32), lp=lp.astype(i32),
        tail_start=(estart + tot).astype(i32), tail_units=((reg - tot) // MOE_SEG_ROWS).astype(i32),
        estart=estart.astype(i32), erows=reg.astype(i32), used=reg_end[-1].reshape(1).astype(i32))


def _for_each_segment_copy(i, lstart_ref, units_ref, seg_ref, local_ref, global_ref, sem, to_global, fn):
    def per_expert(e, carry):
        k = i * N_EXPERTS + e

        @pl.when(units_ref[k] > 0)
        def _():
            n = pl.multiple_of(units_ref[k] * MOE_SEG_ROWS, MOE_SEG_ROWS)
            loc = local_ref.at[pl.ds(pl.multiple_of(lstart_ref[k], MOE_SEG_ROWS), n)]
            glo = global_ref.at[pl.ds(pl.multiple_of(seg_ref[k], MOE_SEG_ROWS), n)]
            fn(pltpu.make_async_copy(loc, glo, sem) if to_global else pltpu.make_async_copy(glo, loc, sem))
        return carry
    lax.fori_loop(0, N_EXPERTS, per_expert, 0)


def _for_each_unused_piece(used_ref, zero_ref, rows_ref, sem, fn):
    def per_piece(j, c):
        go = pl.multiple_of(j * MOE_PIECE, MOE_PIECE)
        fn(pltpu.make_async_copy(zero_ref, rows_ref.at[pl.ds(go, MOE_PIECE)], sem))
        return c
    lax.fori_loop(used_ref[0] // MOE_PIECE, rows_ref.shape[0] // MOE_PIECE, per_piece, 0)


def _dispatch_body(lstart_ref, units_ref, seg_ref, lp_ref, tail_start_ref, tail_units_ref, used_ref,
                   h_ref, post_ref, xs_ref, stage_ref, sel_ref, zero_ref, sems, fill_sem):
    f32, bf16 = jnp.float32, jnp.bfloat16
    i = pl.program_id(0)
    nt = pl.num_programs(0)
    slot = i % 2
    tm = h_ref.shape[0]
    lmax = stage_ref.shape[1]

    def for_each_fill_copy(fn):
        def per_expert(e, carry):
            @pl.when(tail_units_ref[e] > 0)
            def _():
                n = pl.multiple_of(tail_units_ref[e] * MOE_SEG_ROWS, MOE_SEG_ROWS)
                go = pl.multiple_of(tail_start_ref[e], MOE_SEG_ROWS)
                fn(pltpu.make_async_copy(zero_ref.at[pl.ds(0, n)], xs_ref.at[pl.ds(go, n)], fill_sem))
            return carry
        lax.fori_loop(0, N_EXPERTS, per_expert, 0)
        _for_each_unused_piece(used_ref, zero_ref, xs_ref, fill_sem, fn)

    @pl.when(i == 0)
    def _():
        zero_ref[...] = jnp.zeros(zero_ref.shape, bf16)
        for_each_fill_copy(lambda cp: cp.start())
        for_each_fill_copy(lambda cp: cp.wait())

    def segment_copies(tile, slot_, fn):
        _for_each_segment_copy(tile, lstart_ref, units_ref, seg_ref, stage_ref.at[slot_], xs_ref,
                               sems.at[slot_], True, fn)

    @pl.when(i >= 2)
    def _():
        segment_copies(i - 2, slot, lambda cp: cp.wait())

    pos = [post_ref[k:k + 1, :] for k in range(TOP_K)]
    r_local = lax.broadcasted_iota(jnp.int32, (MOE_CHUNK, tm), 0).astype(f32).astype(bf16)
    one, zero = jnp.ones((), bf16), jnp.zeros((), bf16)
    for part in range(lmax // MOE_SEL_ROWS):
        for c in range(MOE_SEL_ROWS // MOE_CHUNK):
            r0 = part * MOE_SEL_ROWS + c * MOE_CHUNK
            loc = [jnp.clip(p - r0, -1.0, float(MOE_CHUNK)).astype(bf16) for p in pos]
            hit = (loc[0] == r_local) | (loc[1] == r_local) | (loc[2] == r_local) | (loc[3] == r_local)
            sel_ref[part, c * MOE_CHUNK:(c + 1) * MOE_CHUNK, :] = jnp.where(hit, one, zero)
        stage_ref[slot, part * MOE_SEL_ROWS:(part + 1) * MOE_SEL_ROWS, :] = jnp.dot(
            sel_ref[part], h_ref[...], preferred_element_type=f32).astype(bf16)

    segment_copies(i, slot, lambda cp: cp.start())

    @pl.when(i == nt - 1)
    def _():
        @pl.when(nt >= 2)
        def _():
            segment_copies(i - 1, 1 - slot, lambda cp: cp.wait())
        segment_copies(i, slot, lambda cp: cp.wait())


def _moe_dispatch(h2, post, plan, tm, lmax, n_rows):
    n, d = h2.shape
    nt = n // tm
    grid_spec = pltpu.PrefetchScalarGridSpec(
        num_scalar_prefetch=7,
        grid=(nt,),
        in_specs=[pl.BlockSpec((tm, d), lambda i, *_: (i, 0)),
                  pl.BlockSpec((8, tm), lambda i, *_: (0, i))],
        out_specs=pl.BlockSpec(memory_space=pl.ANY),
        scratch_shapes=[pltpu.VMEM((2, lmax, d), jnp.bfloat16),
                        pltpu.VMEM((lmax // MOE_SEL_ROWS, MOE_SEL_ROWS, tm), jnp.bfloat16),
                        pltpu.VMEM((MOE_PIECE, d), jnp.bfloat16),
                        pltpu.SemaphoreType.DMA((2,)),
                        pltpu.SemaphoreType.DMA(())])
    return pl.pallas_call(
        _dispatch_body,
        grid_spec=grid_spec,
        out_shape=jax.ShapeDtypeStruct((n_rows, d), jnp.bfloat16),
        compiler_params=pltpu.CompilerParams(dimension_semantics=("arbitrary",),
                                             vmem_limit_bytes=VMEM_LIMIT_BYTES),
        name="moe_dispatch",
    )(plan["lstart"], plan["seg_units"], plan["seg"], plan["lp"], plan["tail_start"], plan["tail_units"],
      plan["used"], h2, post)


def _experts_body(first_ref, count_ref, cstart_ref, cvalid_ref, total_ref, used_ref,
                  xs_ref, wgu_ref, bgu_ref, wd_ref, bd_ref, os_ref,
                  wgu_bf, wd_bf, xbuf, obuf, zero_ref, in_sems, out_sems, fill_sem):
    f32, bf16 = jnp.float32, jnp.bfloat16
    e = pl.program_id(0)
    total = total_ref[0]
    half = D_FF // 2

    def in_copy(j):
        src = xs_ref.at[pl.ds(pl.multiple_of(cstart_ref[j], MOE_PIECE), MOE_ROW_CHUNK)]
        return pltpu.make_async_copy(src, xbuf.at[j % 3], in_sems.at[j % 3])

    def out_copy(j):
        n = pl.multiple_of(cvalid_ref[j], MOE_PIECE)
        go = pl.multiple_of(cstart_ref[j], MOE_PIECE)
        return pltpu.make_async_copy(obuf.at[j % 2, pl.ds(0, n)], os_ref.at[pl.ds(go, n)], out_sems.at[j % 2])

    @pl.when(e == 0)
    def _():
        for j0 in range(2):
            @pl.when(j0 < total)
            def _():
                in_copy(j0).start()
        zero_ref[...] = jnp.zeros(zero_ref.shape, bf16)
        _for_each_unused_piece(used_ref, zero_ref, os_ref, fill_sem, lambda cp: cp.start())
        _for_each_unused_piece(used_ref, zero_ref, os_ref, fill_sem, lambda cp: cp.wait())

    @pl.when(count_ref[e] > 0)
    def _():
        wgu_bf[...] = wgu_ref[0].astype(bf16)
        wd_bf[...] = wd_ref[0].astype(bf16)

        def chunk(j, carry):
            in_copy(j).wait()

            @pl.when(j + 2 < total)
            def _():
                in_copy(j + 2).start()

            @pl.when(j >= 2)
            def _():
                out_copy(j - 2).wait()

            def mlp(n_rows):
                x = xbuf[j % 3, 0:n_rows, :]
                out = bd_ref[0]
                for hf in range(2):
                    gate = jnp.dot(x, wgu_bf[:, hf * half:(hf + 1) * half], preferred_element_type=f32)
                    gate = jnp.minimum(gate + bgu_ref[0, :, hf * half:(hf + 1) * half], SWIGLU_LIMIT)
                    up = jnp.dot(x, wgu_bf[:, D_FF + hf * half:D_FF + (hf + 1) * half],
                                 preferred_element_type=f32)
                    up = jnp.clip(up + bgu_ref[0, :, D_FF + hf * half:D_FF + (hf + 1) * half],
                                  -SWIGLU_LIMIT, SWIGLU_LIMIT)
                    act = (up + 1.0) * (gate * jax.nn.sigmoid(SWIGLU_ALPHA * gate))
                    out = out + jnp.dot(act.astype(bf16), wd_bf[hf * half:(hf + 1) * half, :],
                                        preferred_element_type=f32)
                obuf[j % 2, 0:n_rows, :] = out.astype(bf16)

            @pl.when(cvalid_ref[j] > MOE_ROW_CHUNK // 2)
            def _():
                mlp(MOE_ROW_CHUNK)

            @pl.when(cvalid_ref[j] <= MOE_ROW_CHUNK // 2)
            def _():
                mlp(MOE_ROW_CHUNK // 2)
            out_copy(j).start()
            return carry
        lax.fori_loop(first_ref[e], first_ref[e] + count_ref[e], chunk, 0)

    @pl.when(e == pl.num_programs(0) - 1)
    def _():
        @pl.when(total >= 2)
        def _():
            out_copy(total - 2).wait()

        @pl.when(total >= 1)
        def _():
            out_copy(total - 1).wait()


def _expert_chunks(plan, n_rows):
    i32 = jnp.int32
    max_chunks = n_rows // MOE_ROW_CHUNK + N_EXPERTS
    count = (plan["erows"] + (MOE_ROW_CHUNK - 1)) // MOE_ROW_CHUNK
    end = jnp.cumsum(count)
    first = end - count
    j = jnp.arange(max_chunks, dtype=i32)
    mine = ((first[None, :] <= j[:, None]) & (j[:, None] < end[None, :])).astype(i32)
    c = j - jnp.sum(mine * first[None, :], axis=1)
    cstart = jnp.sum(mine * plan["estart"][None, :], axis=1) + jnp.sum(mine, axis=1) * c * MOE_ROW_CHUNK
    cvalid = jnp.sum(mine * jnp.clip(plan["erows"][None, :] - c[:, None] * MOE_ROW_CHUNK, 0, MOE_ROW_CHUNK),
                     axis=1)
    return (first.astype(i32), count.astype(i32), cstart.astype(i32), cvalid.astype(i32),
            end[-1].reshape(1).astype(i32))


def _moe_experts(xs, plan, w_gate_up, b_gate_up, w_down, b_down):
    d = xs.shape[1]
    grid_spec = pltpu.PrefetchScalarGridSpec(
        num_scalar_prefetch=6,
        grid=(N_EXPERTS,),
        in_specs=[pl.BlockSpec(memory_space=pl.ANY),
                  pl.BlockSpec((1, d, 2 * D_FF), lambda e, *_: (e, 0, 0)),
                  pl.BlockSpec((1, 1, 2 * D_FF), lambda e, *_: (e, 0, 0)),
                  pl.BlockSpec((1, D_FF, d), lambda e, *_: (e, 0, 0)),
                  pl.BlockSpec((1, 1, d), lambda e, *_: (e, 0, 0))],
        out_specs=pl.BlockSpec(memory_space=pl.ANY),
        scratch_shapes=[pltpu.VMEM((d, 2 * D_FF), jnp.bfloat16),
                        pltpu.VMEM((D_FF, d), jnp.bfloat16),
                        pltpu.VMEM((3, MOE_ROW_CHUNK, d), jnp.bfloat16),
                        pltpu.VMEM((2, MOE_ROW_CHUNK, d), jnp.bfloat16),
                        pltpu.VMEM((MOE_PIECE, d), jnp.bfloat16),
                        pltpu.SemaphoreType.DMA((3,)),
                        pltpu.SemaphoreType.DMA((2,)),
                        pltpu.SemaphoreType.DMA(())])
    return pl.pallas_call(
        _experts_body,
        grid_spec=grid_spec,
        out_shape=jax.ShapeDtypeStruct(xs.shape, jnp.bfloat16),
        compiler_params=pltpu.CompilerParams(dimension_semantics=("arbitrary",),
                                             vmem_limit_bytes=VMEM_LIMIT_BYTES),
        name="moe_experts",
    )(*_expert_chunks(plan, xs.shape[0]), plan["used"],
      xs, w_gate_up, b_gate_up.reshape(N_EXPERTS, 1, 2 * D_FF), w_down, b_down.reshape(N_EXPERTS, 1, d))


def _combine_body(lstart_ref, units_ref, seg_ref, lp_ref,
                  os_ref, posg_ref, x_ref, fw_ref, yp_ref, ys_ref, stage_ref, w_ref, sems,
                  *, n_prompt_tiles):
    f32, bf16 = jnp.float32, jnp.bfloat16
    i = pl.program_id(0)
    nt = pl.num_programs(0)
    slot = i % 2
    tm = x_ref.shape[0]
    lmax = stage_ref.shape[1]

    def segment_copies(tile, slot_, fn):
        _for_each_segment_copy(tile, lstart_ref, units_ref, seg_ref, stage_ref.at[slot_], os_ref,
                               sems.at[slot_], False, fn)

    @pl.when(i == 0)
    def _():
        stage_ref[...] = jnp.zeros(stage_ref.shape, bf16)
        segment_copies(0, 0, lambda cp: cp.start())

    @pl.when(i + 1 < nt)
    def _():
        segment_copies(i + 1, 1 - slot, lambda cp: cp.start())

    posg = posg_ref[...]
    pos = [posg[:, k:k + 1] for k in range(TOP_K)]
    gate = [posg[:, TOP_K + k:TOP_K + k + 1] for k in range(TOP_K)]
    gate_bf = [g.astype(bf16) for g in gate]
    r_local = lax.broadcasted_iota(jnp.int32, (tm, MOE_CHUNK), 1).astype(f32).astype(bf16)
    segment_copies(i, slot, lambda cp: cp.wait())
    y = x_ref[...]
    part_cols = lmax // 2
    for part in range(2):
        for c in range(part * part_cols // MOE_CHUNK, (part + 1) * part_cols // MOE_CHUNK):
            w = jnp.zeros((tm, MOE_CHUNK), bf16)
            for k in range(TOP_K):
                loc = jnp.clip(pos[k] - c * MOE_CHUNK, -1.0, float(MOE_CHUNK)).astype(bf16)
                w = jnp.where(loc == r_local, gate_bf[k], w)
            w_ref[:, c * MOE_CHUNK:(c + 1) * MOE_CHUNK] = w
        y = y + jnp.dot(w_ref[:, part * part_cols:(part + 1) * part_cols],
                        stage_ref[slot, part * part_cols:(part + 1) * part_cols, :],
                        preferred_element_type=f32)
    out = y * lax.rsqrt(jnp.mean(y * y, axis=-1, keepdims=True) + EPS) * fw_ref[...]

    @pl.when(i < n_prompt_tiles)
    def _():
        yp_ref[...] = out

    @pl.when(i >= n_prompt_tiles)
    def _():
        ys_ref[...] = out


def _moe_combine(os_, posg, x1, final_norm_w, plan, tm, lmax, n_prompt):
    n, d = x1.shape
    nt = n // tm
    n_prompt_tiles = n_prompt // tm
    n_sample_tiles = nt - n_prompt_tiles
    grid_spec = pltpu.PrefetchScalarGridSpec(
        num_scalar_prefetch=4,
        grid=(nt,),
        in_specs=[pl.BlockSpec(memory_space=pl.ANY),
                  pl.BlockSpec((tm, LANES), lambda i, *_: (i, 0)),
                  pl.BlockSpec((tm, d), lambda i, *_: (i, 0)),
                  pl.BlockSpec((1, d), lambda i, *_: (0, 0))],
        out_specs=[pl.BlockSpec((tm, d), lambda i, *_: (jnp.minimum(i, n_prompt_tiles - 1), 0)),
                   pl.BlockSpec((tm, d), lambda i, *_: (jnp.maximum(i - n_prompt_tiles, 0), 0))],
        scratch_shapes=[pltpu.VMEM((2, lmax, d), jnp.bfloat16),
                        pltpu.VMEM((tm, lmax), jnp.bfloat16),
                        pltpu.SemaphoreType.DMA((2,))])
    return pl.pallas_call(
        functools.partial(_combine_body, n_prompt_tiles=n_prompt_tiles),
        grid_spec=grid_spec,
        out_shape=[jax.ShapeDtypeStruct((n_prompt, d), jnp.float32),
                   jax.ShapeDtypeStruct((n_sample_tiles * tm, d), jnp.float32)],
        compiler_params=pltpu.CompilerParams(dimension_semantics=("arbitrary",),
                                             vmem_limit_bytes=VMEM_LIMIT_BYTES),
        name="moe_combine",
    )(plan["lstart"], plan["seg_units"], plan["seg"], plan["lp"],
      os_, posg, x1, final_norm_w.reshape(1, d).astype(jnp.float32))


def _moe_block(x1, n_prompt, norm2_w, router_w, router_b, w_gate_up, b_gate_up, w_down, b_down,
               final_norm_w, tm=MOE_TOKEN_TILE):
    n = x1.shape[0]
    nt, lmax, n_rows = _moe_sizes(n, tm)
    h2, posg, post, cnt3 = _moe_router(x1, norm2_w, router_w, router_b, tm)
    plan = _moe_plan(cnt3[:, 0, :N_EXPERTS])
    xs = _moe_dispatch(h2, post, plan, tm, lmax, n_rows)
    os_ = _moe_experts(xs, plan, w_gate_up, b_gate_up, w_down, b_down)
    return _moe_combine(os_, posg, x1, final_norm_w, plan, tm, lmax, n_prompt)


def kernel(x_prompt, x_sample, state_ssm, state_conv, state_pool, norm1_w, w_in, conv_w, conv_b, dt_bias,
           A_log, D_skip, ssd_norm_w, pool_w, pool_scale, w_out, norm2_w, router_w, router_b, w_gate_up,
           b_gate_up, w_down, b_down, final_norm_w):
    n_prompt = BATCH * SEQ
    n_sample = DEC_BATCH * DEC_SEQ
    xp = x_prompt.reshape(n_prompt, D_MODEL)
    xs = x_sample.reshape(n_sample, D_MODEL)
    z, xbc, dt_raw, u = _in_proj(xp, xs, norm1_w[0], w_in)
    mp = (conv_w[0], conv_b[0], dt_bias[0], A_log[0], D_skip[0], ssd_norm_w[0], pool_w[0], pool_scale[0])
    mix_p, s1 = _prompt_mixer(z, xbc, dt_raw, u, BATCH, SEQ, *mp)
    mix_s, s2 = _sample_mixer(z, xbc, dt_raw, u, n_prompt, DEC_BATCH, state_conv, state_ssm[0], state_pool,
                              PAST_LEN, *mp)
    nk = CONV_WIDTH - 1
    c1 = jnp.stack([xbc[(b + 1) * SEQ - nk:(b + 1) * SEQ] for b in range(BATCH)])
    p1 = jnp.stack([u[(b + 1) * SEQ - POOL_HIST:(b + 1) * SEQ] for b in range(BATCH)])
    c2 = xbc[n_prompt:].reshape(DEC_BATCH, DEC_SEQ, D_CONV)[:, DEC_SEQ - nk:]
    p2 = jnp.concatenate([state_pool[0][:, DEC_SEQ:], u[n_prompt:].reshape(DEC_BATCH, DEC_SEQ, D_POOL)], axis=1)
    x1 = _out_proj(mix_p, mix_s, w_out[0], xp, xs)
    yp, ys = _moe_block(x1, n_prompt, norm2_w[0], router_w[0], router_b[0], w_gate_up[0], b_gate_up[0],
                        w_down[0], b_down[0], final_norm_w)
    return (yp.reshape(x_prompt.shape), ys.reshape(x_sample.shape),
            s1[None], c1[None], p1[None], s2[None], c2[None], p2[None])
```

```python
import functools
import math
import jax, jax.numpy as jnp
from jax import lax
import numpy as np
from jax.experimental import pallas as pl
from jax.experimental.pallas import tpu as pltpu

D_MODEL = 1024
BATCH = 8
SEQ = 2048
DEC_BATCH = 128
DEC_SEQ = 4
PAST_LEN = 16384

D_MIX = 2 * D_MODEL
D_SSD = 3 * D_MIX // 4
SSD_HEAD_DIM = 64
N_SSD_HEADS = D_SSD // SSD_HEAD_DIM
N_SSD_GROUPS = 4
D_STATE = 128
CONV_WIDTH = 4
SSD_CHUNK = 128
D_CONV = D_SSD + 2 * N_SSD_GROUPS * D_STATE
D_POOL = D_MIX - D_SSD
POOL_WINDOWS = (2, 4, 8, 16)
N_POOL_GROUPS = len(POOL_WINDOWS)
POOL_GROUP_DIM = D_POOL // N_POOL_GROUPS
POOL_HIST = max(POOL_WINDOWS) - 1
D_IN_PROJ = D_SSD + D_CONV + N_SSD_HEADS + D_POOL
N_EXPERTS = 32
TOP_K = 4
D_FF = D_MODEL
SWIGLU_LIMIT = 7.0
SWIGLU_ALPHA = 1.702
EPS = 1e-5

LANES = 128
BF16_SUBLANES = 16
VMEM_LIMIT_BYTES = 48 * 1024 * 1024

MOE_TOKEN_TILE = 512
MOE_SEG_ROWS = BF16_SUBLANES
MOE_PIECE = 128
MOE_ROW_CHUNK = 512
MOE_CHUNK = 256
MOE_SEL_ROWS = 512


BLK = SSD_CHUNK
PROJ_ROW_TILE = 512
PROMPT_BLKS_PER_STEP = 2
HIST_ROWS = 16
CONV_TAIL_ROWS = 8
NT_DIMS = (((1,), (1,)), ((), ()))


def _split2(v):
    hi = v.astype(jnp.bfloat16)
    lo = (v - hi.astype(jnp.float32)).astype(jnp.bfloat16)
    return hi, lo


def _dot_sel_left(sel, v, passes):
    out = None
    rem = v
    for p in range(passes):
        part = rem.astype(jnp.bfloat16)
        d = jnp.dot(sel, part, preferred_element_type=jnp.float32)
        out = d if out is None else out + d
        if p + 1 < passes:
            rem = rem - part.astype(jnp.float32)
    return out


def _dot_sel_right(v, sel, passes):
    out = None
    rem = v
    for p in range(passes):
        part = rem.astype(jnp.bfloat16)
        d = jnp.dot(part, sel, preferred_element_type=jnp.float32)
        out = d if out is None else out + d
        if p + 1 < passes:
            rem = rem - part.astype(jnp.float32)
    return out


def _two_part_specs(n_first, n_second, tm, width):
    t1 = n_first // tm
    t2 = n_second // tm
    return (pl.BlockSpec((tm, width), lambda i: (jnp.minimum(i, t1 - 1), 0)),
            pl.BlockSpec((tm, width), lambda i: (jnp.clip(i - t1, 0, t2 - 1), 0)))


IN_PROJ_COLS = 512


def _in_proj_body(xa_ref, xb_ref, nw_ref, wmain_ref, wtail_ref, z_ref, xbc_ref, dt_ref, u_ref, *, tiles_a):
    f32, bf16 = jnp.float32, jnp.bfloat16
    x = jnp.where(pl.program_id(0) < tiles_a, xa_ref[...], xb_ref[...])
    h = (x * lax.rsqrt(jnp.mean(x * x, axis=-1, keepdims=True) + EPS) * nw_ref[...]).astype(bf16)
    off = 0
    for ref in (z_ref, xbc_ref):
        for c0 in range(0, ref.shape[1], IN_PROJ_COLS):
            w = wmain_ref[0, :, off + c0:off + c0 + IN_PROJ_COLS].astype(bf16)
            ref[:, c0:c0 + IN_PROJ_COLS] = jnp.dot(h, w, preferred_element_type=f32)
        off += ref.shape[1]
    tail = jnp.dot(h, wtail_ref[0, :, 0:N_SSD_HEADS + D_POOL].astype(bf16), preferred_element_type=f32)
    lane = lax.broadcasted_iota(jnp.int32, (tail.shape[0], LANES), 1)
    dt_ref[...] = jnp.where(lane < N_SSD_HEADS, tail[:, 0:LANES], 0.0)
    u_ref[...] = tail[:, N_SSD_HEADS:N_SSD_HEADS + D_POOL]


def _in_proj(xa, xb, norm1_w, w_in3):
    d = xa.shape[1]
    n = xa.shape[0] + xb.shape[0]
    f32, bf16 = jnp.float32, jnp.bfloat16
    s1 = D_SSD + D_CONV
    tail_blk = 1024
    assert s1 % tail_blk == 0 and N_SSD_HEADS + D_POOL <= tail_blk
    widths = (D_SSD, D_CONV, LANES, D_POOL)
    tm = PROJ_ROW_TILE
    return pl.pallas_call(
        functools.partial(_in_proj_body, tiles_a=xa.shape[0] // tm),
        grid=(n // tm,),
        in_specs=[*_two_part_specs(xa.shape[0], xb.shape[0], tm, d),
                  pl.BlockSpec((1, d), lambda i: (0, 0)),
                  pl.BlockSpec((1, d, s1), lambda i: (0, 0, 0), pipeline_mode=pl.Buffered(1)),
                  pl.BlockSpec((1, d, tail_blk), lambda i: (0, 0, s1 // tail_blk), pipeline_mode=pl.Buffered(1))],
        out_specs=[pl.BlockSpec((tm, wd), lambda i: (i, 0)) for wd in widths],
        out_shape=[jax.ShapeDtypeStruct((n, wd), f32) for wd in widths],
        compiler_params=pltpu.CompilerParams(dimension_semantics=("parallel",),
                                             vmem_limit_bytes=VMEM_LIMIT_BYTES),
        name="in_proj",
    )(xa, xb, norm1_w.reshape(1, d).astype(f32), w_in3, w_in3)


def _out_proj_body(ma_ref, mb_ref, w_ref, xa_ref, xb_ref, o_ref, *, tiles_a):
    first = pl.program_id(0) < tiles_a
    m = jnp.where(first, ma_ref[...], mb_ref[...])
    x = jnp.where(first, xa_ref[...], xb_ref[...])
    half = w_ref.shape[0] // 2
    o_ref[...] = (x + jnp.dot(m[:, :half], w_ref[:half, :].astype(jnp.bfloat16),
                              preferred_element_type=jnp.float32)
                  + jnp.dot(m[:, half:], w_ref[half:, :].astype(jnp.bfloat16),
                            preferred_element_type=jnp.float32))


def _out_proj(ma, mb, w_out, xa, xb):
    d = xa.shape[1]
    n = xa.shape[0] + xb.shape[0]
    tm = PROJ_ROW_TILE
    return pl.pallas_call(
        functools.partial(_out_proj_body, tiles_a=xa.shape[0] // tm),
        grid=(n // tm,),
        in_specs=[*_two_part_specs(xa.shape[0], xb.shape[0], tm, D_MIX),
                  pl.BlockSpec((D_MIX, d), lambda i: (0, 0), pipeline_mode=pl.Buffered(1)),
                  *_two_part_specs(xa.shape[0], xb.shape[0], tm, d)],
        out_specs=pl.BlockSpec((tm, d), lambda i: (i, 0)),
        out_shape=jax.ShapeDtypeStruct((n, d), jnp.float32),
        compiler_params=pltpu.CompilerParams(dimension_semantics=("parallel",),
                                             vmem_limit_bytes=VMEM_LIMIT_BYTES),
        name="out_proj",
    )(ma, mb, w_out.astype(jnp.float32), xa, xb)


def _mixer_constants():
    bf16 = jnp.bfloat16
    h = np.arange(LANES)[:, None]
    ch = np.arange(D_SSD)[None, :]
    expand = (ch // SSD_HEAD_DIM == h).astype(np.float32)
    i = np.arange(BLK)[:, None]
    j = np.arange(BLK)[None, :]
    causal = (j <= i).astype(np.float32)
    return dict(expand=jnp.asarray(expand, bf16), expand_t=jnp.asarray(expand.T, bf16),
                causal=jnp.asarray(causal, bf16))


def _softplus(x):
    return jnp.maximum(x, 0.0) + jnp.log(1.0 + jnp.exp(-jnp.abs(x)))


def _conv_silu(ext_ref, cw_ref, cb_ref, first_row):
    ext = ext_ref[...]
    last = first_row + CONV_WIDTH - 1
    acc = cb_ref[...] + cw_ref[CONV_WIDTH - 1:CONV_WIDTH, :] * ext[last:last + BLK, :]
    for k in range(CONV_WIDTH - 1):
        tap = pltpu.roll(ext, CONV_WIDTH - 1 - k, axis=0)[last:last + BLK, :]
        acc = acc + cw_ref[k:k + 1, :] * tap
    return acc * jax.nn.sigmoid(acc)


def _ssd_intra(xbc_c, dt_raw, dtb_ref, alog_ref, causal_bf, expand_ref):
    f32 = jnp.float32
    xs = xbc_c[:, :D_SSD]
    bm = xbc_c[:, D_SSD:D_SSD + N_SSD_GROUPS * D_STATE]
    cm = xbc_c[:, D_SSD + N_SSD_GROUPS * D_STATE:]
    dt = _softplus(dt_raw + dtb_ref[...])
    a = dt * (-jnp.exp(alog_ref[...]))
    a_cum = _dot_sel_left(causal_bf, a, 3)
    dt_x = _dot_sel_right(dt, expand_ref[...], 2)
    return xs, bm, cm, dt, a_cum, xs * dt_x


def _ssd_diag_group(g, cb, a_cum, a_cum_t, keep, xdt):
    f32, bf16 = jnp.float32, jnp.bfloat16
    hg = N_SSD_HEADS // N_SSD_GROUPS
    lane = lax.broadcasted_iota(jnp.int32, (BLK, LANES), 1)
    first_head = lane < SSD_HEAD_DIM
    neg = jnp.float32(-jnp.inf)
    outs = []
    for pr in range(hg * SSD_HEAD_DIM // LANES):
        h1 = g * hg + 2 * pr
        blk = (g * hg * SSD_HEAD_DIM) // LANES + pr
        xp = xdt[:, blk * LANES:(blk + 1) * LANES]
        x1 = jnp.where(first_head, xp, 0.0).astype(bf16)
        x2 = jnp.where(first_head, 0.0, xp).astype(bf16)
        m1 = (cb * jnp.exp(jnp.where(keep, a_cum[:, h1:h1 + 1] - a_cum_t[h1:h1 + 1, :], neg))).astype(bf16)
        m2 = (cb * jnp.exp(jnp.where(keep, a_cum[:, h1 + 1:h1 + 2] - a_cum_t[h1 + 1:h1 + 2, :], neg))).astype(bf16)
        outs.append(jnp.dot(m1, x1, preferred_element_type=f32) + jnp.dot(m2, x2, preferred_element_type=f32))
    return jnp.concatenate(outs, axis=1)


def _gated_norm(y, z, nw_ref):
    yg = y * (z * jax.nn.sigmoid(z))
    return yg * lax.rsqrt(jnp.mean(yg * yg, axis=-1, keepdims=True) + EPS) * nw_ref[...]


def _prompt_mixer_body(z_ref, xbc_ref, dt_ref, u_ref, cw_ref, cb_ref, dtb_ref, alog_ref, dskip_ref, nw_ref,
                       pw_ref, ps_ref, causal_ref,
                       mix_ref, ssm_ref, ext_ref, pool_tail_ref, state_ref):
    n_blk = pl.num_programs(1) * PROMPT_BLKS_PER_STEP
    for ci in range(PROMPT_BLKS_PER_STEP):
        rows = pl.ds(ci * BLK, BLK)
        _prompt_block(pl.program_id(1) * PROMPT_BLKS_PER_STEP + ci, n_blk,
                      z_ref.at[rows], xbc_ref.at[rows], dt_ref.at[rows], u_ref.at[rows],
                      cw_ref, cb_ref, dtb_ref, alog_ref, dskip_ref, nw_ref, pw_ref, ps_ref, causal_ref,
                      mix_ref.at[rows], ssm_ref, ext_ref, pool_tail_ref, state_ref)


def _prompt_block(c, n_blk, z_ref, xbc_ref, dt_ref, u_ref, cw_ref, cb_ref, dtb_ref, alog_ref, dskip_ref, nw_ref,
                  pw_ref, ps_ref, causal_ref,
                  mix_ref, ssm_ref, ext_ref, pool_tail_ref, state_ref):
    f32, bf16 = jnp.float32, jnp.bfloat16
    gw = D_SSD // N_SSD_GROUPS

    @pl.when(c == 0)
    def _():
        ext_ref[0:CONV_TAIL_ROWS, :] = jnp.zeros((CONV_TAIL_ROWS, D_CONV), f32)
        pool_tail_ref[...] = jnp.zeros(pool_tail_ref.shape, f32)
        state_ref[...] = jnp.zeros(state_ref.shape, f32)

    u = u_ref[...]
    ext_u = jnp.concatenate([pool_tail_ref[...], u], axis=0)
    pos = (c * BLK + lax.broadcasted_iota(jnp.int32, (BLK, 1), 0) + 1).astype(f32)
    for gi, w in enumerate(POOL_WINDOWS):
        assert w & (w - 1) == 0 and w <= HIST_ROWS
        sl = slice(gi * POOL_GROUP_DIM, (gi + 1) * POOL_GROUP_DIM)
        ug = u[:, sl]
        acc = ext_u[:, sl]
        span = 1
        while span < w:
            acc = acc + pltpu.roll(acc, span, axis=0)
            span *= 2
        wsum = acc[HIST_ROWS:, :]
        pooled = wsum / jnp.minimum(pos, jnp.float32(w)) - ug
        po = jnp.dot(pooled.astype(bf16), pw_ref[gi], preferred_element_type=f32) * ps_ref[:, sl]
        mix_ref[:, D_SSD + gi * POOL_GROUP_DIM:D_SSD + (gi + 1) * POOL_GROUP_DIM] = po.astype(bf16)
    pool_tail_ref[...] = u_ref[BLK - HIST_ROWS:BLK, :]

    ext_ref[CONV_TAIL_ROWS:CONV_TAIL_ROWS + BLK, :] = xbc_ref[...]
    xbc_c = _conv_silu(ext_ref, cw_ref, cb_ref, CONV_TAIL_ROWS - (CONV_WIDTH - 1))
    ext_ref[0:CONV_TAIL_ROWS, :] = xbc_ref[BLK - CONV_TAIL_ROWS:BLK, :]

    causal_bf = causal_ref[...]
    keep = causal_bf > 0
    xs = xbc_c[:, :D_SSD]
    bm = xbc_c[:, D_SSD:D_SSD + N_SSD_GROUPS * D_STATE]
    cm = xbc_c[:, D_SSD + N_SSD_GROUPS * D_STATE:]
    dt = _softplus(dt_ref[...] + dtb_ref[...])
    a_cum = dt * (-jnp.exp(alog_ref[...]))
    row = lax.broadcasted_iota(jnp.int32, (BLK, LANES), 0)
    span = 1
    while span < BLK:
        a_cum = a_cum + jnp.where(row >= span, pltpu.roll(a_cum, span, axis=0), 0.0)
        span *= 2
    a_cum_t = jnp.transpose(a_cum)
    a_tot = a_cum[BLK - 1:BLK, :]
    ea = jnp.exp(a_cum)
    dte = jnp.exp(a_tot - a_cum)
    cd = jnp.exp(a_tot)
    hg = N_SSD_HEADS // N_SSD_GROUPS
    first_head = lax.broadcasted_iota(jnp.int32, (BLK, LANES), 1) < SSD_HEAD_DIM
    neg = jnp.float32(-jnp.inf)

    def head_cols(v, h1):
        return jnp.where(first_head, v[:, h1:h1 + 1], v[:, h1 + 1:h1 + 2])

    def decay_from(h):
        return jnp.exp(jnp.where(keep, a_cum[:, h:h + 1] - a_cum_t[h:h + 1, :], neg))

    y_parts = []
    for g in range(N_SSD_GROUPS):
        cg = cm[:, g * D_STATE:(g + 1) * D_STATE].astype(bf16)
        bg = bm[:, g * D_STATE:(g + 1) * D_STATE].astype(bf16)
        cb = lax.dot_general(cg, bg, NT_DIMS, preferred_element_type=f32)
        sg = state_ref[g * gw:(g + 1) * gw, :]
        y_off = lax.dot_general(cg, sg.astype(bf16), NT_DIMS, preferred_element_type=f32)
        xdte_parts = []
        for pr in range(gw // LANES):
            h1 = g * hg + 2 * pr
            sl = slice(h1 * SSD_HEAD_DIM, h1 * SSD_HEAD_DIM + LANES)
            xp = xs[:, sl] * head_cols(dt, h1)
            x1 = jnp.where(first_head, xp, 0.0).astype(bf16)
            x2 = jnp.where(first_head, 0.0, xp).astype(bf16)
            y_diag = (jnp.dot((cb * decay_from(h1)).astype(bf16), x1, preferred_element_type=f32)
                      + jnp.dot((cb * decay_from(h1 + 1)).astype(bf16), x2, preferred_element_type=f32))
            y_parts.append(y_diag + y_off[:, pr * LANES:(pr + 1) * LANES] * head_cols(ea, h1)
                           + xs[:, sl] * dskip_ref[:, sl])
            xdte_parts.append(xp * head_cols(dte, h1))
        xdte_t = jnp.transpose(jnp.concatenate(xdte_parts, axis=1)).astype(bf16)
        cd_rows = jnp.concatenate([jnp.broadcast_to(cd[:, h:h + 1], (SSD_HEAD_DIM, D_STATE))
                                   for h in range(g * hg, (g + 1) * hg)], axis=0)
        state_ref[g * gw:(g + 1) * gw, :] = sg * cd_rows + jnp.dot(xdte_t, bg, preferred_element_type=f32)
    y = jnp.concatenate(y_parts, axis=1)
    mix_ref[:, 0:D_SSD] = _gated_norm(y, z_ref[...], nw_ref).astype(bf16)

    @pl.when(c == n_blk - 1)
    def _():
        ssm_ref[0] = state_ref[...].reshape(N_SSD_HEADS, SSD_HEAD_DIM, D_STATE)


def _prompt_mixer(z, xbc, dt, u, n_seq, seq_len, conv_w, conv_b, dt_bias, A_log, D_skip, ssd_norm_w, pool_w,
                  pool_scale):
    f32, bf16 = jnp.float32, jnp.bfloat16
    n = n_seq * seq_len
    step_rows = PROMPT_BLKS_PER_STEP * BLK
    n_steps = seq_len // step_rows
    k = _mixer_constants()

    def row_blk(width):
        return pl.BlockSpec((step_rows, width), lambda b, c: (b * n_steps + c, 0))

    def const(shape):
        return pl.BlockSpec(shape, lambda b, c: (0,) * len(shape))

    pad_h = (0, LANES - N_SSD_HEADS)
    return pl.pallas_call(
        _prompt_mixer_body,
        grid=(n_seq, n_steps),
        in_specs=[row_blk(D_SSD), row_blk(D_CONV), row_blk(LANES), row_blk(D_POOL),
                  const((CONV_WIDTH, D_CONV)), const((1, D_CONV)), const((1, LANES)), const((1, LANES)),
                  const((1, D_SSD)), const((1, D_SSD)),
                  const((N_POOL_GROUPS, POOL_GROUP_DIM, POOL_GROUP_DIM)), const((1, D_POOL)),
                  const((BLK, BLK))],
        out_specs=[pl.BlockSpec((step_rows, D_MIX), lambda b, c: (b * n_steps + c, 0)),
                   pl.BlockSpec((1, N_SSD_HEADS, SSD_HEAD_DIM, D_STATE), lambda b, c: (b, 0, 0, 0))],
        out_shape=[jax.ShapeDtypeStruct((n, D_MIX), bf16),
                   jax.ShapeDtypeStruct((n_seq, N_SSD_HEADS, SSD_HEAD_DIM, D_STATE), f32)],
        scratch_shapes=[pltpu.VMEM((CONV_TAIL_ROWS + BLK, D_CONV), f32),
                        pltpu.VMEM((HIST_ROWS, D_POOL), f32),
                        pltpu.VMEM((D_SSD, D_STATE), f32)],
        compiler_params=pltpu.CompilerParams(dimension_semantics=("parallel", "arbitrary"),
                                             vmem_limit_bytes=VMEM_LIMIT_BYTES),
        name="prompt_mixer",
    )(z, xbc, dt, u, conv_w.astype(f32), conv_b.reshape(1, D_CONV).astype(f32),
      jnp.pad(dt_bias.astype(f32), pad_h).reshape(1, LANES), jnp.pad(A_log.astype(f32), pad_h).reshape(1, LANES),
      jnp.repeat(D_skip.astype(f32), SSD_HEAD_DIM).reshape(1, D_SSD), ssd_norm_w.reshape(1, D_SSD).astype(f32),
      pool_w.astype(bf16), pool_scale.reshape(1, D_POOL).astype(f32),
      k["causal"])


SEQ_PER_BLK = BLK // DEC_SEQ
SEQ_PER_STEP = 8


def _sample_constants():
    bf16 = jnp.bfloat16
    r = np.arange(BLK)
    sq, st = r // DEC_SEQ, r % DEC_SEQ
    same = sq[:, None] == sq[None, :]
    causal = same & (st[None, :] <= st[:, None])
    nk = CONV_WIDTH - 1
    shift = np.stack([same & (st[None, :] == st[:, None] + k - nk) for k in range(nk)])
    cs = np.arange(SEQ_PER_BLK * nk)
    stsel = np.stack([(cs[None, :] // nk == sq[:, None]) & (cs[None, :] % nk == st[:, None] + k)
                      for k in range(nk)])
    pcur = np.stack([causal & (st[:, None] - st[None, :] < w) for w in POOL_WINDOWS])
    hs = np.arange(SEQ_PER_BLK * POOL_HIST)
    phist = np.stack([(hs[None, :] // POOL_HIST == sq[:, None])
                      & (st[:, None] + POOL_HIST - hs[None, :] % POOL_HIST < w) for w in POOL_WINDOWS])
    as_bf = lambda a: jnp.asarray(a.astype(np.float32), bf16)
    return dict(same=as_bf(same), causal=as_bf(causal), shift=as_bf(shift), stsel=as_bf(stsel),
                pcur=as_bf(pcur), phist=as_bf(phist))


def _sample_mixer_body(z_ref, xbc_ref, dt_ref, u_ref, cst_ref, pst_ref, ssm_in_ref,
                       cw_ref, cb_ref, dtb_ref, alog_ref, dskip_ref, nw_ref, pw_ref, ps_ref,
                       causal_ref, same_ref, expand_ref, expand_t_ref, shift_ref, stsel_ref, pcur_ref, phist_ref,
                       mix_ref, ssm_out_ref,
                       ydiag_ref, ea_ref, yt_ref, cdh_ref, cdl_ref, xdte_t_ref, bm_ref, cm_ref, *, pos0):
    f32, bf16 = jnp.float32, jnp.bfloat16
    s = pl.program_id(1)
    gw = D_SSD // N_SSD_GROUPS

    @pl.when(s == 0)
    def _():
        xbc = xbc_ref[...]
        cst = cst_ref[...]
        acc = cb_ref[...] + cw_ref[CONV_WIDTH - 1:CONV_WIDTH, :] * xbc
        for k in range(CONV_WIDTH - 1):
            tap = _dot_sel_left(shift_ref[k], xbc, 3) + _dot_sel_left(stsel_ref[k], cst, 3)
            acc = acc + cw_ref[k:k + 1, :] * tap
        xbc_c = acc * jax.nn.sigmoid(acc)

        causal_bf = causal_ref[...]
        keep = causal_bf > 0
        xs, bm, cm, dt, a_cum, xdt = _ssd_intra(xbc_c, dt_ref[...], dtb_ref, alog_ref, causal_bf, expand_ref)
        a_tot = _dot_sel_left(same_ref[...], dt * (-jnp.exp(alog_ref[...])), 3)
        a_cum_t = jnp.transpose(a_cum)
        ea_ref[...] = _dot_sel_right(jnp.exp(a_cum), expand_ref[...], 2)
        dte_x = _dot_sel_right(jnp.exp(a_tot - a_cum), expand_ref[...], 2)
        cd_col = _dot_sel_left(expand_t_ref[...], jnp.exp(jnp.transpose(a_tot)), 2)
        cd_hi, cd_lo = _split2(cd_col)
        cdh_ref[...] = cd_hi
        cdl_ref[...] = cd_lo
        bm_ref[...] = bm.astype(bf16)
        cm_ref[...] = cm.astype(bf16)
        for g in range(N_SSD_GROUPS):
            cg = cm[:, g * D_STATE:(g + 1) * D_STATE].astype(bf16)
            bg = bm[:, g * D_STATE:(g + 1) * D_STATE].astype(bf16)
            cb = lax.dot_general(cg, bg, NT_DIMS, preferred_element_type=f32)
            y_diag = _ssd_diag_group(g, cb, a_cum, a_cum_t, keep, xdt)
            ydiag_ref[:, g * gw:(g + 1) * gw] = y_diag + xs[:, g * gw:(g + 1) * gw] * dskip_ref[:, g * gw:(g + 1) * gw]
            xdte_t_ref[g * gw:(g + 1) * gw, :] = jnp.transpose(
                xdt[:, g * gw:(g + 1) * gw] * dte_x[:, g * gw:(g + 1) * gw]).astype(bf16)
        yt_ref[...] = jnp.zeros(yt_ref.shape, f32)

        u = u_ref[...]
        pst = pst_ref[...]
        step = lax.broadcasted_iota(jnp.int32, (BLK, 1), 0) % DEC_SEQ
        pos = (step + (pos0 + 1)).astype(f32)
        for gi, w in enumerate(POOL_WINDOWS):
            sl = slice(gi * POOL_GROUP_DIM, (gi + 1) * POOL_GROUP_DIM)
            ug = u[:, sl]
            wsum = _dot_sel_left(pcur_ref[gi], ug, 2) + _dot_sel_left(phist_ref[gi], pst[:, sl], 2)
            pooled = wsum / jnp.minimum(pos, jnp.float32(w)) - ug
            po = jnp.dot(pooled.astype(bf16), pw_ref[gi], preferred_element_type=f32) * ps_ref[:, sl]
            mix_ref[:, D_SSD + gi * POOL_GROUP_DIM:D_SSD + (gi + 1) * POOL_GROUP_DIM] = po.astype(bf16)

    hg = N_SSD_HEADS // N_SSD_GROUPS
    row_seq = lax.broadcasted_iota(jnp.int32, (BLK, LANES), 0) // DEC_SEQ
    col_seq = lax.broadcasted_iota(jnp.int32, (gw, BLK), 1) // DEC_SEQ
    row_idx = lax.broadcasted_iota(jnp.int32, (BLK, LANES), 0)
    for q in range(SEQ_PER_STEP):
        sq = s * SEQ_PER_STEP + q
        rows_of_s = row_seq == sq
        cols_of_s = col_seq == sq
        pick_s = jnp.where(row_idx == DEC_SEQ * sq, 1.0, 0.0).astype(bf16)
        state = ssm_in_ref[q].reshape(D_SSD, D_STATE)
        for g in range(N_SSD_GROUPS):
            rs = slice(g * gw, (g + 1) * gw)
            sg = state[rs, :]
            cg = cm_ref[:, g * D_STATE:(g + 1) * D_STATE]
            bg = bm_ref[:, g * D_STATE:(g + 1) * D_STATE]
            yt = lax.dot_general(sg.astype(bf16), cg, NT_DIMS, preferred_element_type=f32)
            yt_ref[rs, :] += jnp.where(cols_of_s, yt, 0.0)
            cd = (jnp.dot(cdh_ref[rs, :], pick_s, preferred_element_type=f32)
                  + jnp.dot(cdl_ref[rs, :], pick_s, preferred_element_type=f32))
            upd = jnp.dot(xdte_t_ref[rs, :], jnp.where(rows_of_s, bg, jnp.zeros_like(bg)),
                          preferred_element_type=f32)
            ssm_out_ref[q, g * hg:(g + 1) * hg] = (sg * cd + upd).reshape(hg, SSD_HEAD_DIM, D_STATE)

    @pl.when(s == pl.num_programs(1) - 1)
    def _():
        y = ydiag_ref[...] + jnp.transpose(yt_ref[...]) * ea_ref[...]
        mix_ref[:, 0:D_SSD] = _gated_norm(y, z_ref[...], nw_ref).astype(bf16)


def _sample_mixer(z, xbc, dt, u, row0, n_seq, state_conv, state_ssm, state_pool, pos0,
                  conv_w, conv_b, dt_bias, A_log, D_skip, ssd_norm_w, pool_w, pool_scale):
    f32, bf16 = jnp.float32, jnp.bfloat16
    n_blk = n_seq // SEQ_PER_BLK
    blk0 = row0 // BLK
    nk = CONV_WIDTH - 1
    k = _mixer_constants()
    ks = _sample_constants()

    def row_blk(width):
        return pl.BlockSpec((BLK, width), lambda j, s: (blk0 + j, 0))

    def const(shape):
        return pl.BlockSpec(shape, lambda j, s: (0,) * len(shape))

    steps = SEQ_PER_BLK // SEQ_PER_STEP
    state_spec = pl.BlockSpec((SEQ_PER_STEP, N_SSD_HEADS, SSD_HEAD_DIM, D_STATE),
                              lambda j, s: (j * steps + s, 0, 0, 0))
    pad_h = (0, LANES - N_SSD_HEADS)
    return pl.pallas_call(
        functools.partial(_sample_mixer_body, pos0=pos0),
        grid=(n_blk, steps),
        in_specs=[row_blk(D_SSD), row_blk(D_CONV), row_blk(LANES), row_blk(D_POOL),
                  pl.BlockSpec((SEQ_PER_BLK * nk, D_CONV), lambda j, s: (j, 0)),
                  pl.BlockSpec((SEQ_PER_BLK * POOL_HIST, D_POOL), lambda j, s: (j, 0)),
                  state_spec,
                  const((CONV_WIDTH, D_CONV)), const((1, D_CONV)), const((1, LANES)), const((1, LANES)),
                  const((1, D_SSD)), const((1, D_SSD)),
                  const((N_POOL_GROUPS, POOL_GROUP_DIM, POOL_GROUP_DIM)), const((1, D_POOL)),
                  const((BLK, BLK)), const((BLK, BLK)), const((LANES, D_SSD)), const((D_SSD, LANES)),
                  const((nk, BLK, BLK)), const((nk, BLK, SEQ_PER_BLK * nk)),
                  const((N_POOL_GROUPS, BLK, BLK)), const((N_POOL_GROUPS, BLK, SEQ_PER_BLK * POOL_HIST))],
        out_specs=[pl.BlockSpec((BLK, D_MIX), lambda j, s: (j, 0)), state_spec],
        out_shape=[jax.ShapeDtypeStruct((n_seq * DEC_SEQ, D_MIX), bf16),
                   jax.ShapeDtypeStruct((n_seq, N_SSD_HEADS, SSD_HEAD_DIM, D_STATE), f32)],
        scratch_shapes=[pltpu.VMEM((BLK, D_SSD), f32), pltpu.VMEM((BLK, D_SSD), f32),
                        pltpu.VMEM((D_SSD, BLK), f32), pltpu.VMEM((D_SSD, BLK), bf16),
                        pltpu.VMEM((D_SSD, BLK), bf16), pltpu.VMEM((D_SSD, BLK), bf16),
                        pltpu.VMEM((BLK, N_SSD_GROUPS * D_STATE), bf16),
                        pltpu.VMEM((BLK, N_SSD_GROUPS * D_STATE), bf16)],
        compiler_params=pltpu.CompilerParams(dimension_semantics=("parallel", "arbitrary"),
                                             vmem_limit_bytes=VMEM_LIMIT_BYTES),
        name="sample_mixer",
    )(z, xbc, dt, u, state_conv.reshape(n_seq * nk, D_CONV), state_pool.reshape(n_seq * POOL_HIST, D_POOL),
      state_ssm, conv_w.astype(f32), conv_b.reshape(1, D_CONV).astype(f32),
      jnp.pad(dt_bias.astype(f32), pad_h).reshape(1, LANES), jnp.pad(A_log.astype(f32), pad_h).reshape(1, LANES),
      jnp.repeat(D_skip.astype(f32), SSD_HEAD_DIM).reshape(1, D_SSD), ssd_norm_w.reshape(1, D_SSD).astype(f32),
      pool_w.astype(bf16), pool_scale.reshape(1, D_POOL).astype(f32),
      ks["causal"], ks["same"], k["expand"], k["expand_t"], ks["shift"], ks["stsel"],
      ks["pcur"], ks["phist"])


def _moe_sizes(n_tokens, tm):
    nt = n_tokens // tm
    lmax = -(-(TOP_K * tm + N_EXPERTS * MOE_SEG_ROWS) // MOE_SEL_ROWS) * MOE_SEL_ROWS
    rows = (TOP_K * n_tokens + nt * N_EXPERTS * MOE_SEG_ROWS + N_EXPERTS * (MOE_PIECE - 1)
            + MOE_ROW_CHUNK)
    n_rows = -(-rows // MOE_PIECE) * MOE_PIECE
    return nt, lmax, n_rows


def _router_body(x_ref, nw_ref, rwh_ref, rwl_ref, rb_ref, h_ref, posg_ref, post_ref, cnt_ref):
    f32, bf16 = jnp.float32, jnp.bfloat16
    tm = x_ref.shape[0]
    x = x_ref[...]
    h = x * lax.rsqrt(jnp.mean(x * x, axis=-1, keepdims=True) + EPS) * nw_ref[...]
    h_hi = h.astype(bf16)
    h_ref[...] = h_hi
    h_lo = (h - h_hi.astype(f32)).astype(bf16)
    wh = rwh_ref[...]
    logits = (jnp.dot(h_hi, wh, preferred_element_type=f32)
              + jnp.dot(h_lo, wh, preferred_element_type=f32)
              + jnp.dot(h_hi, rwl_ref[...], preferred_element_type=f32)) + rb_ref[...]
    lane = lax.broadcasted_iota(jnp.int32, (tm, LANES), 1)
    lanef = lane.astype(f32)
    neg = jnp.float32(-jnp.inf)
    l = jnp.where(lane < N_EXPERTS, logits, neg)
    sels, vals = [], []
    for _ in range(TOP_K):
        m = jnp.max(l, axis=1, keepdims=True)
        idx = jnp.min(jnp.where(l == m, lanef, jnp.float32(LANES)), axis=1, keepdims=True)
        sel = lanef == idx
        l = jnp.where(sel, neg, l)
        sels.append(sel)
        vals.append(m)
    exps = [jnp.exp(v - vals[0]) for v in vals]
    denom = exps[0] + exps[1] + exps[2] + exps[3]
    gates = [e / denom for e in exps]
    chosen = jnp.where(sels[0] | sels[1] | sels[2] | sels[3], 1.0, 0.0).astype(f32)
    row = lax.broadcasted_iota(jnp.int32, (tm, tm), 0)
    col = lax.broadcasted_iota(jnp.int32, (tm, tm), 1)
    lower = jnp.where(col < row, 1.0, 0.0).astype(bf16)
    rank = jnp.dot(lower, chosen.astype(bf16), preferred_element_type=f32)
    cnt = jnp.sum(chosen, axis=0, keepdims=True)
    seg_units = jnp.maximum(jnp.floor((cnt + (MOE_SEG_ROWS - 1)) * (1.0 / MOE_SEG_ROWS)), 1.0)
    r2 = lax.broadcasted_iota(jnp.int32, (LANES, LANES), 0)
    c2 = lax.broadcasted_iota(jnp.int32, (LANES, LANES), 1)
    upper = jnp.where(r2 < c2, 1.0, 0.0).astype(bf16)
    lstart = jnp.dot(jnp.broadcast_to(seg_units, (8, LANES)).astype(bf16), upper,
                     preferred_element_type=f32)[0:1, :] * MOE_SEG_ROWS
    posmat = lstart + rank
    posg = jnp.zeros((tm, LANES), f32)
    for k in range(TOP_K):
        pos_k = jnp.sum(jnp.where(sels[k], posmat, 0.0), axis=1, keepdims=True)
        posg = posg + jnp.where(lane == k, pos_k, 0.0) + jnp.where(lane == TOP_K + k, gates[k], 0.0)
    posg_ref[...] = posg
    post_ref[...] = jnp.transpose(posg)[0:8, :]
    cnt_ref[0] = jnp.broadcast_to(cnt, (8, LANES)).astype(jnp.int32)


def _moe_router(x1, norm2_w, router_w, router_b, tm):
    n, d = x1.shape
    nt = n // tm
    f32, bf16 = jnp.float32, jnp.bfloat16
    rw = jnp.pad(router_w.astype(f32), ((0, 0), (0, LANES - N_EXPERTS)))
    rw_hi = rw.astype(bf16)
    rw_lo = (rw - rw_hi.astype(f32)).astype(bf16)
    rb = jnp.pad(router_b.astype(f32), (0, LANES - N_EXPERTS)).reshape(1, LANES)
    return pl.pallas_call(
        _router_body,
        grid=(nt,),
        in_specs=[pl.BlockSpec((tm, d), lambda i: (i, 0)),
                  pl.BlockSpec((1, d), lambda i: (0, 0)),
                  pl.BlockSpec((d, LANES), lambda i: (0, 0)),
                  pl.BlockSpec((d, LANES), lambda i: (0, 0)),
                  pl.BlockSpec((1, LANES), lambda i: (0, 0))],
        out_specs=[pl.BlockSpec((tm, d), lambda i: (i, 0)),
                   pl.BlockSpec((tm, LANES), lambda i: (i, 0)),
                   pl.BlockSpec((8, tm), lambda i: (0, i)),
                   pl.BlockSpec((1, 8, LANES), lambda i: (i, 0, 0))],
        out_shape=[jax.ShapeDtypeStruct((n, d), bf16),
                   jax.ShapeDtypeStruct((n, LANES), f32),
                   jax.ShapeDtypeStruct((8, n), f32),
                   jax.ShapeDtypeStruct((nt, 8, LANES), jnp.int32)],
        compiler_params=pltpu.CompilerParams(dimension_semantics=("parallel",),
                                             vmem_limit_bytes=VMEM_LIMIT_BYTES),
        name="moe_router",
    )(x1, norm2_w.reshape(1, d).astype(f32), rw_hi, rw_lo, rb)


def _moe_plan(cnt):
    i32 = jnp.int32
    pad = jnp.maximum((cnt + (MOE_SEG_ROWS - 1)) // MOE_SEG_ROWS, 1) * MOE_SEG_ROWS
    lstart = jnp.cumsum(pad, axis=1) - pad
    lp = jnp.sum(pad, axis=1)
    tot = jnp.sum(pad, axis=0)
    reg = (tot + (MOE_PIECE - 1)) // MOE_PIECE * MOE_PIECE
    reg_end = jnp.cumsum(reg)
    estart = reg_end - reg
    seg = estart[None, :] + jnp.cumsum(pad, axis=0) - pad
    return dict(
        lstart=lstart.reshape(-1).astype(i32), seg_units=(pad // MOE_SEG_ROWS).reshape(-1).astype(i32),
        seg=seg.reshape(-1).astype(i32), lp=lp.astype(i32),
        tail_start=(estart + tot).astype(i32), tail_units=((reg - tot) // MOE_SEG_ROWS).astype(i32),
        estart=estart.astype(i32), erows=reg.astype(i32), used=reg_end[-1].reshape(1).astype(i32))


def _for_each_segment_copy(i, lstart_ref, units_ref, seg_ref, local_ref, global_ref, sem, to_global, fn):
    def per_expert(e, carry):
        k = i * N_EXPERTS + e
        n = pl.multiple_of(units_ref[k] * MOE_SEG_ROWS, MOE_SEG_ROWS)
        loc = local_ref.at[pl.ds(pl.multiple_of(lstart_ref[k], MOE_SEG_ROWS), n)]
        glo = global_ref.at[pl.ds(pl.multiple_of(seg_ref[k], MOE_SEG_ROWS), n)]
        fn(pltpu.make_async_copy(loc, glo, sem) if to_global else pltpu.make_async_copy(glo, loc, sem))
        return carry
    lax.fori_loop(0, N_EXPERTS, per_expert, 0)


def _wait_segment_copies(i, lp_ref, local_ref, global_ref, sem):
    n = pl.multiple_of(lp_ref[i], MOE_SEG_ROWS)
    pltpu.make_async_copy(global_ref.at[pl.ds(0, n)], local_ref.at[pl.ds(0, n)], sem).wait()


def _for_each_unused_piece(used_ref, zero_ref, rows_ref, sem, fn):
    def per_piece(j, c):
        go = pl.multiple_of(j * MOE_PIECE, MOE_PIECE)
        fn(pltpu.make_async_copy(zero_ref, rows_ref.at[pl.ds(go, MOE_PIECE)], sem))
        return c
    lax.fori_loop(used_ref[0] // MOE_PIECE, rows_ref.shape[0] // MOE_PIECE, per_piece, 0)


def _dispatch_body(lstart_ref, units_ref, seg_ref, lp_ref, tail_start_ref, tail_units_ref, used_ref,
                   h_ref, post_ref, xs_ref, stage_ref, sel_ref, zero_ref, sems, fill_sem):
    f32, bf16 = jnp.float32, jnp.bfloat16
    i = pl.program_id(0)
    nt = pl.num_programs(0)
    slot = i % 2
    tm = h_ref.shape[0]
    lmax = stage_ref.shape[1]

    def for_each_fill_copy(fn):
        def per_expert(e, carry):
            @pl.when(tail_units_ref[e] > 0)
            def _():
                n = pl.multiple_of(tail_units_ref[e] * MOE_SEG_ROWS, MOE_SEG_ROWS)
                go = pl.multiple_of(tail_start_ref[e], MOE_SEG_ROWS)
                fn(pltpu.make_async_copy(zero_ref.at[pl.ds(0, n)], xs_ref.at[pl.ds(go, n)], fill_sem))
            return carry
        lax.fori_loop(0, N_EXPERTS, per_expert, 0)
        _for_each_unused_piece(used_ref, zero_ref, xs_ref, fill_sem, fn)

    @pl.when(i == 0)
    def _():
        zero_ref[...] = jnp.zeros(zero_ref.shape, bf16)
        for_each_fill_copy(lambda cp: cp.start())
        for_each_fill_copy(lambda cp: cp.wait())

    def segment_copies(tile, slot_, fn):
        _for_each_segment_copy(tile, lstart_ref, units_ref, seg_ref, stage_ref.at[slot_], xs_ref,
                               sems.at[slot_], True, fn)

    @pl.when(i >= 2)
    def _():
        _wait_segment_copies(i - 2, lp_ref, stage_ref.at[slot], xs_ref, sems.at[slot])

    pos = [post_ref[k:k + 1, :] for k in range(TOP_K)]
    r_local = lax.broadcasted_iota(jnp.int32, (MOE_CHUNK, tm), 0).astype(f32).astype(bf16)
    one, zero = jnp.ones((), bf16), jnp.zeros((), bf16)
    for part in range(lmax // MOE_SEL_ROWS):
        for c in range(MOE_SEL_ROWS // MOE_CHUNK):
            r0 = part * MOE_SEL_ROWS + c * MOE_CHUNK
            loc = [jnp.clip(p - r0, -1.0, float(MOE_CHUNK)).astype(bf16) for p in pos]
            hit = (loc[0] == r_local) | (loc[1] == r_local) | (loc[2] == r_local) | (loc[3] == r_local)
            sel_ref[part, c * MOE_CHUNK:(c + 1) * MOE_CHUNK, :] = jnp.where(hit, one, zero)
        stage_ref[slot, part * MOE_SEL_ROWS:(part + 1) * MOE_SEL_ROWS, :] = jnp.dot(
            sel_ref[part], h_ref[...], preferred_element_type=f32).astype(bf16)

    segment_copies(i, slot, lambda cp: cp.start())

    @pl.when(i == nt - 1)
    def _():
        @pl.when(nt >= 2)
        def _():
            _wait_segment_copies(i - 1, lp_ref, stage_ref.at[1 - slot], xs_ref, sems.at[1 - slot])
        _wait_segment_copies(i, lp_ref, stage_ref.at[slot], xs_ref, sems.at[slot])


def _moe_dispatch(h2, post, plan, tm, lmax, n_rows):
    n, d = h2.shape
    nt = n // tm
    grid_spec = pltpu.PrefetchScalarGridSpec(
        num_scalar_prefetch=7,
        grid=(nt,),
        in_specs=[pl.BlockSpec((tm, d), lambda i, *_: (i, 0)),
                  pl.BlockSpec((8, tm), lambda i, *_: (0, i))],
        out_specs=pl.BlockSpec(memory_space=pl.ANY),
        scratch_shapes=[pltpu.VMEM((2, lmax, d), jnp.bfloat16),
                        pltpu.VMEM((lmax // MOE_SEL_ROWS, MOE_SEL_ROWS, tm), jnp.bfloat16),
                        pltpu.VMEM((MOE_PIECE, d), jnp.bfloat16),
                        pltpu.SemaphoreType.DMA((2,)),
                        pltpu.SemaphoreType.DMA(())])
    return pl.pallas_call(
        _dispatch_body,
        grid_spec=grid_spec,
        out_shape=jax.ShapeDtypeStruct((n_rows, d), jnp.bfloat16),
        compiler_params=pltpu.CompilerParams(dimension_semantics=("arbitrary",),
                                             vmem_limit_bytes=VMEM_LIMIT_BYTES),
        name="moe_dispatch",
    )(plan["lstart"], plan["seg_units"], plan["seg"], plan["lp"], plan["tail_start"], plan["tail_units"],
      plan["used"], h2, post)


def _experts_body(first_ref, count_ref, cstart_ref, cvalid_ref, total_ref, used_ref,
                  xs_ref, wgu_ref, bgu_ref, wd_ref, bd_ref, os_ref,
                  wgu_bf, wd_bf, xbuf, obuf, zero_ref, in_sems, out_sems, fill_sem):
    f32, bf16 = jnp.float32, jnp.bfloat16
    e = pl.program_id(0)
    total = total_ref[0]
    half = D_FF // 2

    def in_copy(j):
        src = xs_ref.at[pl.ds(pl.multiple_of(cstart_ref[j], MOE_PIECE), MOE_ROW_CHUNK)]
        return pltpu.make_async_copy(src, xbuf.at[j % 3], in_sems.at[j % 3])

    def out_copy(j):
        n = pl.multiple_of(cvalid_ref[j], MOE_PIECE)
        go = pl.multiple_of(cstart_ref[j], MOE_PIECE)
        return pltpu.make_async_copy(obuf.at[j % 2, pl.ds(0, n)], os_ref.at[pl.ds(go, n)], out_sems.at[j % 2])

    @pl.when(e == 0)
    def _():
        for j0 in range(2):
            @pl.when(j0 < total)
            def _():
                in_copy(j0).start()
        zero_ref[...] = jnp.zeros(zero_ref.shape, bf16)
        _for_each_unused_piece(used_ref, zero_ref, os_ref, fill_sem, lambda cp: cp.start())
        _for_each_unused_piece(used_ref, zero_ref, os_ref, fill_sem, lambda cp: cp.wait())

    @pl.when(count_ref[e] > 0)
    def _():
        wgu_bf[...] = wgu_ref[0].astype(bf16)
        wd_bf[...] = wd_ref[0].astype(bf16)

        def chunk(j, carry):
            in_copy(j).wait()

            @pl.when(j + 2 < total)
            def _():
                in_copy(j + 2).start()

            @pl.when(j >= 2)
            def _():
                out_copy(j - 2).wait()

            def mlp(n_rows):
                x = xbuf[j % 3, 0:n_rows, :]
                out = bd_ref[0]
                for hf in range(2):
                    gate = jnp.dot(x, wgu_bf[:, hf * half:(hf + 1) * half], preferred_element_type=f32)
                    gate = jnp.minimum(gate + bgu_ref[0, :, hf * half:(hf + 1) * half], SWIGLU_LIMIT)
                    up = jnp.dot(x, wgu_bf[:, D_FF + hf * half:D_FF + (hf + 1) * half],
                                 preferred_element_type=f32)
                    up = jnp.clip(up + bgu_ref[0, :, D_FF + hf * half:D_FF + (hf + 1) * half],
                                  -SWIGLU_LIMIT, SWIGLU_LIMIT)
                    act = (up + 1.0) * (gate * jax.nn.sigmoid(SWIGLU_ALPHA * gate))
                    out = out + jnp.dot(act.astype(bf16), wd_bf[hf * half:(hf + 1) * half, :],
                                        preferred_element_type=f32)
                obuf[j % 2, 0:n_rows, :] = out.astype(bf16)

            @pl.when(cvalid_ref[j] > MOE_ROW_CHUNK // 2)
            def _():
                mlp(MOE_ROW_CHUNK)

            @pl.when(cvalid_ref[j] <= MOE_ROW_CHUNK // 2)
            def _():
                mlp(MOE_ROW_CHUNK // 2)
            out_copy(j).start()
            return carry
        lax.fori_loop(first_ref[e], first_ref[e] + count_ref[e], chunk, 0)

    @pl.when(e == pl.num_programs(0) - 1)
    def _():
        @pl.when(total >= 2)
        def _():
            out_copy(total - 2).wait()

        @pl.when(total >= 1)
        def _():
            out_copy(total - 1).wait()


def _expert_chunks(plan, n_rows):
    i32 = jnp.int32
    max_chunks = n_rows // MOE_ROW_CHUNK + N_EXPERTS
    count = (plan["erows"] + (MOE_ROW_CHUNK - 1)) // MOE_ROW_CHUNK
    end = jnp.cumsum(count)
    first = end - count
    j = jnp.arange(max_chunks, dtype=i32)
    mine = ((first[None, :] <= j[:, None]) & (j[:, None] < end[None, :])).astype(i32)
    c = j - jnp.sum(mine * first[None, :], axis=1)
    cstart = jnp.sum(mine * plan["estart"][None, :], axis=1) + jnp.sum(mine, axis=1) * c * MOE_ROW_CHUNK
    cvalid = jnp.sum(mine * jnp.clip(plan["erows"][None, :] - c[:, None] * MOE_ROW_CHUNK, 0, MOE_ROW_CHUNK),
                     axis=1)
    return (first.astype(i32), count.astype(i32), cstart.astype(i32), cvalid.astype(i32),
            end[-1].reshape(1).astype(i32))


def _moe_experts(xs, plan, w_gate_up, b_gate_up, w_down, b_down):
    d = xs.shape[1]
    grid_spec = pltpu.PrefetchScalarGridSpec(
        num_scalar_prefetch=6,
        grid=(N_EXPERTS,),
        in_specs=[pl.BlockSpec(memory_space=pl.ANY),
                  pl.BlockSpec((1, d, 2 * D_FF), lambda e, *_: (e, 0, 0)),
                  pl.BlockSpec((1, 1, 2 * D_FF), lambda e, *_: (e, 0, 0)),
                  pl.BlockSpec((1, D_FF, d), lambda e, *_: (e, 0, 0)),
                  pl.BlockSpec((1, 1, d), lambda e, *_: (e, 0, 0))],
        out_specs=pl.BlockSpec(memory_space=pl.ANY),
        scratch_shapes=[pltpu.VMEM((d, 2 * D_FF), jnp.bfloat16),
                        pltpu.VMEM((D_FF, d), jnp.bfloat16),
                        pltpu.VMEM((3, MOE_ROW_CHUNK, d), jnp.bfloat16),
                        pltpu.VMEM((2, MOE_ROW_CHUNK, d), jnp.bfloat16),
                        pltpu.VMEM((MOE_PIECE, d), jnp.bfloat16),
                        pltpu.SemaphoreType.DMA((3,)),
                        pltpu.SemaphoreType.DMA((2,)),
                        pltpu.SemaphoreType.DMA(())])
    return pl.pallas_call(
        _experts_body,
        grid_spec=grid_spec,
        out_shape=jax.ShapeDtypeStruct(xs.shape, jnp.bfloat16),
        compiler_params=pltpu.CompilerParams(dimension_semantics=("arbitrary",),
                                             vmem_limit_bytes=VMEM_LIMIT_BYTES),
        name="moe_experts",
    )(*_expert_chunks(plan, xs.shape[0]), plan["used"],
      xs, w_gate_up, b_gate_up.reshape(N_EXPERTS, 1, 2 * D_FF), w_down, b_down.reshape(N_EXPERTS, 1, d))


def _combine_body(lstart_ref, units_ref, seg_ref, lp_ref,
                  os_ref, posg_ref, x_ref, fw_ref, yp_ref, ys_ref, stage_ref, w_ref, sems,
                  *, n_prompt_tiles):
    f32, bf16 = jnp.float32, jnp.bfloat16
    i = pl.program_id(0)
    nt = pl.num_programs(0)
    slot = i % 2
    tm = x_ref.shape[0]
    lmax = stage_ref.shape[1]

    def segment_copies(tile, slot_, fn):
        _for_each_segment_copy(tile, lstart_ref, units_ref, seg_ref, stage_ref.at[slot_], os_ref,
                               sems.at[slot_], False, fn)

    @pl.when(i == 0)
    def _():
        stage_ref[...] = jnp.zeros(stage_ref.shape, bf16)
        segment_copies(0, 0, lambda cp: cp.start())

    @pl.when(i + 1 < nt)
    def _():
        segment_copies(i + 1, 1 - slot, lambda cp: cp.start())

    posg = posg_ref[...]
    pos = [posg[:, k:k + 1] for k in range(TOP_K)]
    gate = [posg[:, TOP_K + k:TOP_K + k + 1] for k in range(TOP_K)]
    gate_bf = [g.astype(bf16) for g in gate]
    r_local = lax.broadcasted_iota(jnp.int32, (tm, MOE_CHUNK), 1).astype(f32).astype(bf16)
    _wait_segment_copies(i, lp_ref, stage_ref.at[slot], os_ref, sems.at[slot])
    y = x_ref[...]
    part_cols = lmax // 2
    for part in range(2):
        for c in range(part * part_cols // MOE_CHUNK, (part + 1) * part_cols // MOE_CHUNK):
            w = jnp.zeros((tm, MOE_CHUNK), bf16)
            for k in range(TOP_K):
                loc = jnp.clip(pos[k] - c * MOE_CHUNK, -1.0, float(MOE_CHUNK)).astype(bf16)
                w = jnp.where(loc == r_local, gate_bf[k], w)
            w_ref[:, c * MOE_CHUNK:(c + 1) * MOE_CHUNK] = w
        y = y + jnp.dot(w_ref[:, part * part_cols:(part + 1) * part_cols],
                        stage_ref[slot, part * part_cols:(part + 1) * part_cols, :],
                        preferred_element_type=f32)
    out = y * lax.rsqrt(jnp.mean(y * y, axis=-1, keepdims=True) + EPS) * fw_ref[...]

    @pl.when(i < n_prompt_tiles)
    def _():
        yp_ref[...] = out

    @pl.when(i >= n_prompt_tiles)
    def _():
        ys_ref[...] = out


def _moe_combine(os_, posg, x1, final_norm_w, plan, tm, lmax, n_prompt):
    n, d = x1.shape
    nt = n // tm
    n_prompt_tiles = n_prompt // tm
    n_sample_tiles = nt - n_prompt_tiles
    grid_spec = pltpu.PrefetchScalarGridSpec(
        num_scalar_prefetch=4,
        grid=(nt,),
        in_specs=[pl.BlockSpec(memory_space=pl.ANY),
                  pl.BlockSpec((tm, LANES), lambda i, *_: (i, 0)),
                  pl.BlockSpec((tm, d), lambda i, *_: (i, 0)),
                  pl.BlockSpec((1, d), lambda i, *_: (0, 0))],
        out_specs=[pl.BlockSpec((tm, d), lambda i, *_: (jnp.minimum(i, n_prompt_tiles - 1), 0)),
                   pl.BlockSpec((tm, d), lambda i, *_: (jnp.maximum(i - n_prompt_tiles, 0), 0))],
        scratch_shapes=[pltpu.VMEM((2, lmax, d), jnp.bfloat16),
                        pltpu.VMEM((tm, lmax), jnp.bfloat16),
                        pltpu.SemaphoreType.DMA((2,))])
    return pl.pallas_call(
        functools.partial(_combine_body, n_prompt_tiles=n_prompt_tiles),
        grid_spec=grid_spec,
        out_shape=[jax.ShapeDtypeStruct((n_prompt, d), jnp.float32),
                   jax.ShapeDtypeStruct((n_sample_tiles * tm, d), jnp.float32)],
        compiler_params=pltpu.CompilerParams(dimension_semantics=("arbitrary",),
                                             vmem_limit_bytes=VMEM_LIMIT_BYTES),
        name="moe_combine",
    )(plan["lstart"], plan["seg_units"], plan["seg"], plan["lp"],
      os_, posg, x1, final_norm_w.reshape(1, d).astype(jnp.float32))


def _moe_block(x1, n_prompt, norm2_w, router_w, router_b, w_gate_up, b_gate_up, w_down, b_down,
               final_norm_w, tm=MOE_TOKEN_TILE):
    n = x1.shape[0]
    nt, lmax, n_rows = _moe_sizes(n, tm)
    h2, posg, post, cnt3 = _moe_router(x1, norm2_w, router_w, router_b, tm)
    plan = _moe_plan(cnt3[:, 0, :N_EXPERTS])
    xs = _moe_dispatch(h2, post, plan, tm, lmax, n_rows)
    os_ = _moe_experts(xs, plan, w_gate_up, b_gate_up, w_down, b_down)
    return _moe_combine(os_, posg, x1, final_norm_w, plan, tm, lmax, n_prompt)


def kernel(x_prompt, x_sample, state_ssm, state_conv, state_pool, norm1_w, w_in, conv_w, conv_b, dt_bias,
           A_log, D_skip, ssd_norm_w, pool_w, pool_scale, w_out, norm2_w, router_w, router_b, w_gate_up,
           b_gate_up, w_down, b_down, final_norm_w):
    n_prompt = BATCH * SEQ
    n_sample = DEC_BATCH * DEC_SEQ
    xp = x_prompt.reshape(n_prompt, D_MODEL)
    xs = x_sample.reshape(n_sample, D_MODEL)
    z, xbc, dt_raw, u = _in_proj(xp, xs, norm1_w[0], w_in)
    mp = (conv_w[0], conv_b[0], dt_bias[0], A_log[0], D_skip[0], ssd_norm_w[0], pool_w[0], pool_scale[0])
    mix_p, s1 = _prompt_mixer(z, xbc, dt_raw, u, BATCH, SEQ, *mp)
    mix_s, s2 = _sample_mixer(z, xbc, dt_raw, u, n_prompt, DEC_BATCH, state_conv, state_ssm[0], state_pool,
                              PAST_LEN, *mp)
    nk = CONV_WIDTH - 1
    c1 = jnp.stack([xbc[(b + 1) * SEQ - nk:(b + 1) * SEQ] for b in range(BATCH)])
    p1 = jnp.stack([u[(b + 1) * SEQ - POOL_HIST:(b + 1) * SEQ] for b in range(BATCH)])
    c2 = xbc[n_prompt:].reshape(DEC_BATCH, DEC_SEQ, D_CONV)[:, DEC_SEQ - nk:]
    p2 = jnp.concatenate([state_pool[0][:, DEC_SEQ:], u[n_prompt:].reshape(DEC_BATCH, DEC_SEQ, D_POOL)], axis=1)
    x1 = _out_proj(mix_p, mix_s, w_out[0], xp, xs)
    yp, ys = _moe_block(x1, n_prompt, norm2_w[0], router_w[0], router_b[0], w_gate_up[0], b_gate_up[0],
                        w_down[0], b_down[0], final_norm_w)
    return (yp.reshape(x_prompt.shape), ys.reshape(x_sample.shape),
            s1[None], c1[None], p1[None], s2[None], c2[None], p2[None])
```

```python
import functools
import math
import jax, jax.numpy as jnp
from jax import lax
import numpy as np
from jax.experimental import pallas as pl
from jax.experimental.pallas import tpu as pltpu

D_MODEL = 1024
BATCH = 8
SEQ = 2048
DEC_BATCH = 128
DEC_SEQ = 4
PAST_LEN = 16384

D_MIX = 2 * D_MODEL
D_SSD = 3 * D_MIX // 4
SSD_HEAD_DIM = 64
N_SSD_HEADS = D_SSD // SSD_HEAD_DIM
N_SSD_GROUPS = 4
D_STATE = 128
CONV_WIDTH = 4
SSD_CHUNK = 128
D_CONV = D_SSD + 2 * N_SSD_GROUPS * D_STATE
D_POOL = D_MIX - D_SSD
POOL_WINDOWS = (2, 4, 8, 16)
N_POOL_GROUPS = len(POOL_WINDOWS)
POOL_GROUP_DIM = D_POOL // N_POOL_GROUPS
POOL_HIST = max(POOL_WINDOWS) - 1
D_IN_PROJ = D_SSD + D_CONV + N_SSD_HEADS + D_POOL
N_EXPERTS = 32
TOP_K = 4
D_FF = D_MODEL
SWIGLU_LIMIT = 7.0
SWIGLU_ALPHA = 1.702
EPS = 1e-5

LANES = 128
BF16_SUBLANES = 16
VMEM_LIMIT_BYTES = 48 * 1024 * 1024

MOE_TOKEN_TILE = 512
MOE_SEG_ROWS = BF16_SUBLANES
MOE_PIECE = 128
MOE_ROW_CHUNK = 512
MOE_CHUNK = 256
MOE_SEL_ROWS = 512


BLK = SSD_CHUNK
PROJ_ROW_TILE = 512
PROMPT_BLKS_PER_STEP = 2
HIST_ROWS = 16
CONV_TAIL_ROWS = 8
NT_DIMS = (((1,), (1,)), ((), ()))


def _split2(v):
    hi = v.astype(jnp.bfloat16)
    lo = (v - hi.astype(jnp.float32)).astype(jnp.bfloat16)
    return hi, lo


def _dot_sel_left(sel, v, passes):
    out = None
    rem = v
    for p in range(passes):
        part = rem.astype(jnp.bfloat16)
        d = jnp.dot(sel, part, preferred_element_type=jnp.float32)
        out = d if out is None else out + d
        if p + 1 < passes:
            rem = rem - part.astype(jnp.float32)
    return out


def _dot_sel_right(v, sel, passes):
    out = None
    rem = v
    for p in range(passes):
        part = rem.astype(jnp.bfloat16)
        d = jnp.dot(part, sel, preferred_element_type=jnp.float32)
        out = d if out is None else out + d
        if p + 1 < passes:
            rem = rem - part.astype(jnp.float32)
    return out


def _two_part_specs(n_first, n_second, tm, width):
    t1 = n_first // tm
    t2 = n_second // tm
    return (pl.BlockSpec((tm, width), lambda i: (jnp.minimum(i, t1 - 1), 0)),
            pl.BlockSpec((tm, width), lambda i: (jnp.clip(i - t1, 0, t2 - 1), 0)))


IN_PROJ_COLS = 512


def _in_proj_body(xa_ref, xb_ref, nw_ref, wmain_ref, wtail_ref, z_ref, xbc_ref, dt_ref, u_ref, *, tiles_a):
    f32, bf16 = jnp.float32, jnp.bfloat16
    x = jnp.where(pl.program_id(0) < tiles_a, xa_ref[...], xb_ref[...])
    h = (x * lax.rsqrt(jnp.mean(x * x, axis=-1, keepdims=True) + EPS) * nw_ref[...]).astype(bf16)
    off = 0
    for ref in (z_ref, xbc_ref):
        for c0 in range(0, ref.shape[1], IN_PROJ_COLS):
            w = wmain_ref[0, :, off + c0:off + c0 + IN_PROJ_COLS].astype(bf16)
            ref[:, c0:c0 + IN_PROJ_COLS] = jnp.dot(h, w, preferred_element_type=f32)
        off += ref.shape[1]
    tail = jnp.dot(h, wtail_ref[0, :, 0:N_SSD_HEADS + D_POOL].astype(bf16), preferred_element_type=f32)
    lane = lax.broadcasted_iota(jnp.int32, (tail.shape[0], LANES), 1)
    dt_ref[...] = jnp.where(lane < N_SSD_HEADS, tail[:, 0:LANES], 0.0)
    u_ref[...] = tail[:, N_SSD_HEADS:N_SSD_HEADS + D_POOL]


def _in_proj(xa, xb, norm1_w, w_in3, tm=PROJ_ROW_TILE):
    d = xa.shape[1]
    n = xa.shape[0] + xb.shape[0]
    f32, bf16 = jnp.float32, jnp.bfloat16
    s1 = D_SSD + D_CONV
    tail_blk = 1024
    assert s1 % tail_blk == 0 and N_SSD_HEADS + D_POOL <= tail_blk
    widths = (D_SSD, D_CONV, LANES, D_POOL)
    return pl.pallas_call(
        functools.partial(_in_proj_body, tiles_a=xa.shape[0] // tm),
        grid=(n // tm,),
        in_specs=[*_two_part_specs(xa.shape[0], xb.shape[0], tm, d),
                  pl.BlockSpec((1, d), lambda i: (0, 0)),
                  pl.BlockSpec((1, d, s1), lambda i: (0, 0, 0), pipeline_mode=pl.Buffered(1)),
                  pl.BlockSpec((1, d, tail_blk), lambda i: (0, 0, s1 // tail_blk), pipeline_mode=pl.Buffered(1))],
        out_specs=[pl.BlockSpec((tm, wd), lambda i: (i, 0)) for wd in widths],
        out_shape=[jax.ShapeDtypeStruct((n, wd), f32) for wd in widths],
        compiler_params=pltpu.CompilerParams(dimension_semantics=("parallel",),
                                             vmem_limit_bytes=VMEM_LIMIT_BYTES),
        name="in_proj",
    )(xa, xb, norm1_w.reshape(1, d).astype(f32), w_in3, w_in3)


def _out_proj_body(ma_ref, mb_ref, w_ref, xa_ref, xb_ref, o_ref, *, tiles_a):
    first = pl.program_id(0) < tiles_a
    m = jnp.where(first, ma_ref[...], mb_ref[...])
    x = jnp.where(first, xa_ref[...], xb_ref[...])
    half = w_ref.shape[0] // 2
    o_ref[...] = (x + jnp.dot(m[:, :half], w_ref[:half, :].astype(jnp.bfloat16),
                              preferred_element_type=jnp.float32)
                  + jnp.dot(m[:, half:], w_ref[half:, :].astype(jnp.bfloat16),
                            preferred_element_type=jnp.float32))


def _out_proj(ma, mb, w_out, xa, xb):
    d = xa.shape[1]
    n = xa.shape[0] + xb.shape[0]
    tm = PROJ_ROW_TILE
    return pl.pallas_call(
        functools.partial(_out_proj_body, tiles_a=xa.shape[0] // tm),
        grid=(n // tm,),
        in_specs=[*_two_part_specs(xa.shape[0], xb.shape[0], tm, D_MIX),
                  pl.BlockSpec((D_MIX, d), lambda i: (0, 0), pipeline_mode=pl.Buffered(1)),
                  *_two_part_specs(xa.shape[0], xb.shape[0], tm, d)],
        out_specs=pl.BlockSpec((tm, d), lambda i: (i, 0)),
        out_shape=jax.ShapeDtypeStruct((n, d), jnp.float32),
        compiler_params=pltpu.CompilerParams(dimension_semantics=("parallel",),
                                             vmem_limit_bytes=VMEM_LIMIT_BYTES),
        name="out_proj",
    )(ma, mb, w_out.astype(jnp.float32), xa, xb)


def _mixer_constants():
    bf16 = jnp.bfloat16
    h = np.arange(LANES)[:, None]
    ch = np.arange(D_SSD)[None, :]
    expand = (ch // SSD_HEAD_DIM == h).astype(np.float32)
    i = np.arange(BLK)[:, None]
    j = np.arange(BLK)[None, :]
    causal = (j <= i).astype(np.float32)
    return dict(expand=jnp.asarray(expand, bf16), expand_t=jnp.asarray(expand.T, bf16),
                causal=jnp.asarray(causal, bf16))


def _softplus(x):
    return jnp.maximum(x, 0.0) + jnp.log(1.0 + jnp.exp(-jnp.abs(x)))


def _conv_silu(ext_ref, cw_ref, cb_ref, first_row):
    ext = ext_ref[...]
    last = first_row + CONV_WIDTH - 1
    acc = cb_ref[...] + cw_ref[CONV_WIDTH - 1:CONV_WIDTH, :] * ext[last:last + BLK, :]
    for k in range(CONV_WIDTH - 1):
        tap = pltpu.roll(ext, CONV_WIDTH - 1 - k, axis=0)[last:last + BLK, :]
        acc = acc + cw_ref[k:k + 1, :] * tap
    return acc * jax.nn.sigmoid(acc)


def _ssd_intra(xbc_c, dt_raw, dtb_ref, alog_ref, causal_bf, expand_ref):
    f32 = jnp.float32
    xs = xbc_c[:, :D_SSD]
    bm = xbc_c[:, D_SSD:D_SSD + N_SSD_GROUPS * D_STATE]
    cm = xbc_c[:, D_SSD + N_SSD_GROUPS * D_STATE:]
    dt = _softplus(dt_raw + dtb_ref[...])
    a = dt * (-jnp.exp(alog_ref[...]))
    a_cum = _dot_sel_left(causal_bf, a, 3)
    dt_x = _dot_sel_right(dt, expand_ref[...], 2)
    return xs, bm, cm, dt, a_cum, xs * dt_x


def _ssd_diag_group(g, cb, a_cum, a_cum_t, keep, xdt):
    f32, bf16 = jnp.float32, jnp.bfloat16
    hg = N_SSD_HEADS // N_SSD_GROUPS
    lane = lax.broadcasted_iota(jnp.int32, (BLK, LANES), 1)
    first_head = lane < SSD_HEAD_DIM
    neg = jnp.float32(-jnp.inf)
    outs = []
    for pr in range(hg * SSD_HEAD_DIM // LANES):
        h1 = g * hg + 2 * pr
        blk = (g * hg * SSD_HEAD_DIM) // LANES + pr
        xp = xdt[:, blk * LANES:(blk + 1) * LANES]
        x1 = jnp.where(first_head, xp, 0.0).astype(bf16)
        x2 = jnp.where(first_head, 0.0, xp).astype(bf16)
        m1 = (cb * jnp.exp(jnp.where(keep, a_cum[:, h1:h1 + 1] - a_cum_t[h1:h1 + 1, :], neg))).astype(bf16)
        m2 = (cb * jnp.exp(jnp.where(keep, a_cum[:, h1 + 1:h1 + 2] - a_cum_t[h1 + 1:h1 + 2, :], neg))).astype(bf16)
        outs.append(jnp.dot(m1, x1, preferred_element_type=f32) + jnp.dot(m2, x2, preferred_element_type=f32))
    return jnp.concatenate(outs, axis=1)


def _gated_norm(y, z, nw_ref):
    yg = y * (z * jax.nn.sigmoid(z))
    return yg * lax.rsqrt(jnp.mean(yg * yg, axis=-1, keepdims=True) + EPS) * nw_ref[...]


def _prompt_mixer_body(z_ref, xbc_ref, dt_ref, u_ref, cw_ref, cb_ref, dtb_ref, alog_ref, dskip_ref, nw_ref,
                       pw_ref, ps_ref, causal_ref,
                       mix_ref, ssm_ref, ext_ref, pool_tail_ref, state_ref):
    n_blk = pl.num_programs(1) * PROMPT_BLKS_PER_STEP
    for ci in range(PROMPT_BLKS_PER_STEP):
        rows = pl.ds(ci * BLK, BLK)
        _prompt_block(pl.program_id(1) * PROMPT_BLKS_PER_STEP + ci, n_blk,
                      z_ref.at[rows], xbc_ref.at[rows], dt_ref.at[rows], u_ref.at[rows],
                      cw_ref, cb_ref, dtb_ref, alog_ref, dskip_ref, nw_ref, pw_ref, ps_ref, causal_ref,
                      mix_ref.at[rows], ssm_ref, ext_ref, pool_tail_ref, state_ref)


def _reset_history(ext_ref, pool_tail_ref, state_ref):
    ext_ref[0:CONV_TAIL_ROWS, :] = jnp.zeros((CONV_TAIL_ROWS, D_CONV), jnp.float32)
    pool_tail_ref[...] = jnp.zeros(pool_tail_ref.shape, jnp.float32)
    state_ref[...] = jnp.zeros(state_ref.shape, jnp.float32)


def _prompt_block(c, n_blk, z_ref, xbc_ref, dt_ref, u_ref, cw_ref, cb_ref, dtb_ref, alog_ref, dskip_ref, nw_ref,
                  pw_ref, ps_ref, causal_ref,
                  mix_ref, ssm_ref, ext_ref, pool_tail_ref, state_ref, *, edges=True, side_work=()):
    f32, bf16 = jnp.float32, jnp.bfloat16
    gw = D_SSD // N_SSD_GROUPS
    side_work = iter(side_work)

    def do_side_work():
        piece = next(side_work, None)
        if piece is not None:
            piece()

    if edges:
        @pl.when(c == 0)
        def _():
            _reset_history(ext_ref, pool_tail_ref, state_ref)

    u = u_ref[...]
    ext_u = jnp.concatenate([pool_tail_ref[...], u], axis=0)
    pos = (c * BLK + lax.broadcasted_iota(jnp.int32, (BLK, 1), 0) + 1).astype(f32)
    for gi, w in enumerate(POOL_WINDOWS):
        assert w & (w - 1) == 0 and w <= HIST_ROWS
        sl = slice(gi * POOL_GROUP_DIM, (gi + 1) * POOL_GROUP_DIM)
        ug = u[:, sl]
        acc = ext_u[:, sl]
        span = 1
        while span < w:
            acc = acc + pltpu.roll(acc, span, axis=0)
            span *= 2
        wsum = acc[HIST_ROWS:, :]
        pooled = wsum / jnp.minimum(pos, jnp.float32(w)) - ug
        po = jnp.dot(pooled.astype(bf16), pw_ref[gi], preferred_element_type=f32) * ps_ref[:, sl]
        mix_ref[:, D_SSD + gi * POOL_GROUP_DIM:D_SSD + (gi + 1) * POOL_GROUP_DIM] = po.astype(bf16)
    pool_tail_ref[...] = u_ref[BLK - HIST_ROWS:BLK, :]
    do_side_work()

    ext_ref[CONV_TAIL_ROWS:CONV_TAIL_ROWS + BLK, :] = xbc_ref[...]
    xbc_c = _conv_silu(ext_ref, cw_ref, cb_ref, CONV_TAIL_ROWS - (CONV_WIDTH - 1))
    ext_ref[0:CONV_TAIL_ROWS, :] = xbc_ref[BLK - CONV_TAIL_ROWS:BLK, :]
    do_side_work()

    causal_bf = causal_ref[...]
    keep = causal_bf > 0
    xs = xbc_c[:, :D_SSD]
    bm = xbc_c[:, D_SSD:D_SSD + N_SSD_GROUPS * D_STATE]
    cm = xbc_c[:, D_SSD + N_SSD_GROUPS * D_STATE:]
    dt = _softplus(dt_ref[...] + dtb_ref[...])
    a_cum = dt * (-jnp.exp(alog_ref[...]))
    row = lax.broadcasted_iota(jnp.int32, (BLK, LANES), 0)
    span = 1
    while span < BLK:
        a_cum = a_cum + jnp.where(row >= span, pltpu.roll(a_cum, span, axis=0), 0.0)
        span *= 2
    a_cum_t = jnp.transpose(a_cum)
    a_tot = a_cum[BLK - 1:BLK, :]
    ea = jnp.exp(a_cum)
    dte = jnp.exp(a_tot - a_cum)
    cd = jnp.exp(a_tot)
    hg = N_SSD_HEADS // N_SSD_GROUPS
    first_head = lax.broadcasted_iota(jnp.int32, (BLK, LANES), 1) < SSD_HEAD_DIM
    neg = jnp.float32(-jnp.inf)

    def head_cols(v, h1):
        return jnp.where(first_head, v[:, h1:h1 + 1], v[:, h1 + 1:h1 + 2])

    def decay_from(h):
        return jnp.exp(jnp.where(keep, a_cum[:, h:h + 1] - a_cum_t[h:h + 1, :], neg))

    y_parts = []
    for g in range(N_SSD_GROUPS):
        do_side_work()
        cg = cm[:, g * D_STATE:(g + 1) * D_STATE].astype(bf16)
        bg = bm[:, g * D_STATE:(g + 1) * D_STATE].astype(bf16)
        cb = lax.dot_general(cg, bg, NT_DIMS, preferred_element_type=f32)
        sg = state_ref[g * gw:(g + 1) * gw, :]
        y_off = lax.dot_general(cg, sg.astype(bf16), NT_DIMS, preferred_element_type=f32)
        xdte_parts = []
        for pr in range(gw // LANES):
            h1 = g * hg + 2 * pr
            sl = slice(h1 * SSD_HEAD_DIM, h1 * SSD_HEAD_DIM + LANES)
            xp = xs[:, sl] * head_cols(dt, h1)
            x1 = jnp.where(first_head, xp, 0.0).astype(bf16)
            x2 = jnp.where(first_head, 0.0, xp).astype(bf16)
            y_diag = (jnp.dot((cb * decay_from(h1)).astype(bf16), x1, preferred_element_type=f32)
                      + jnp.dot((cb * decay_from(h1 + 1)).astype(bf16), x2, preferred_element_type=f32))
            y_parts.append(y_diag + y_off[:, pr * LANES:(pr + 1) * LANES] * head_cols(ea, h1)
                           + xs[:, sl] * dskip_ref[:, sl])
            xdte_parts.append(xp * head_cols(dte, h1))
        xdte_t = jnp.transpose(jnp.concatenate(xdte_parts, axis=1)).astype(bf16)
        cd_rows = jnp.concatenate([jnp.broadcast_to(cd[:, h:h + 1], (SSD_HEAD_DIM, D_STATE))
                                   for h in range(g * hg, (g + 1) * hg)], axis=0)
        state_ref[g * gw:(g + 1) * gw, :] = sg * cd_rows + jnp.dot(xdte_t, bg, preferred_element_type=f32)
    y = jnp.concatenate(y_parts, axis=1)
    mix_ref[:, 0:D_SSD] = _gated_norm(y, z_ref[...], nw_ref).astype(bf16)

    if edges:
        @pl.when(c == n_blk - 1)
        def _():
            ssm_ref[0] = state_ref[...].reshape(N_SSD_HEADS, SSD_HEAD_DIM, D_STATE)


def _prompt_mixer(z, xbc, dt, u, n_seq, seq_len, conv_w, conv_b, dt_bias, A_log, D_skip, ssd_norm_w, pool_w,
                  pool_scale):
    f32, bf16 = jnp.float32, jnp.bfloat16
    n = n_seq * seq_len
    step_rows = PROMPT_BLKS_PER_STEP * BLK
    n_steps = seq_len // step_rows
    k = _mixer_constants()

    def row_blk(width):
        return pl.BlockSpec((step_rows, width), lambda b, c: (b * n_steps + c, 0))

    def const(shape):
        return pl.BlockSpec(shape, lambda b, c: (0,) * len(shape))

    pad_h = (0, LANES - N_SSD_HEADS)
    return pl.pallas_call(
        _prompt_mixer_body,
        grid=(n_seq, n_steps),
        in_specs=[row_blk(D_SSD), row_blk(D_CONV), row_blk(LANES), row_blk(D_POOL),
                  const((CONV_WIDTH, D_CONV)), const((1, D_CONV)), const((1, LANES)), const((1, LANES)),
                  const((1, D_SSD)), const((1, D_SSD)),
                  const((N_POOL_GROUPS, POOL_GROUP_DIM, POOL_GROUP_DIM)), const((1, D_POOL)),
                  const((BLK, BLK))],
        out_specs=[pl.BlockSpec((step_rows, D_MIX), lambda b, c: (b * n_steps + c, 0)),
                   pl.BlockSpec((1, N_SSD_HEADS, SSD_HEAD_DIM, D_STATE), lambda b, c: (b, 0, 0, 0))],
        out_shape=[jax.ShapeDtypeStruct((n, D_MIX), bf16),
                   jax.ShapeDtypeStruct((n_seq, N_SSD_HEADS, SSD_HEAD_DIM, D_STATE), f32)],
        scratch_shapes=[pltpu.VMEM((CONV_TAIL_ROWS + BLK, D_CONV), f32),
                        pltpu.VMEM((HIST_ROWS, D_POOL), f32),
                        pltpu.VMEM((D_SSD, D_STATE), f32)],
        compiler_params=pltpu.CompilerParams(dimension_semantics=("parallel", "arbitrary"),
                                             vmem_limit_bytes=VMEM_LIMIT_BYTES),
        name="prompt_mixer",
    )(z, xbc, dt, u, conv_w.astype(f32), conv_b.reshape(1, D_CONV).astype(f32),
      jnp.pad(dt_bias.astype(f32), pad_h).reshape(1, LANES), jnp.pad(A_log.astype(f32), pad_h).reshape(1, LANES),
      jnp.repeat(D_skip.astype(f32), SSD_HEAD_DIM).reshape(1, D_SSD), ssd_norm_w.reshape(1, D_SSD).astype(f32),
      pool_w.astype(bf16), pool_scale.reshape(1, D_POOL).astype(f32),
      k["causal"])


W_SLAB_COLS = 512


def _prompt_fused_body(x0_ref, xn_ref, n1w_ref, w_hbm, wtail_ref,
                       cw_ref, cb_ref, dtb_ref, alog_ref, dskip_ref, nw_ref, pw_ref, ps_ref, causal_ref,
                       mix_ref, ssm_ref, ctail_ref, ptail_ref,
                       wbf_ref, wtail_bf_ref, wstage_ref, z_s, xbc_s, dt_s, u_s, ext_ref, pool_tail_ref,
                       state_ref, wsem, *, steps_per_seq):
    f32, bf16 = jnp.float32, jnp.bfloat16
    s = pl.program_id(0)
    step_rows = x0_ref.shape[0]
    n_main = wbf_ref.shape[1]
    n_tail = N_SSD_HEADS + D_POOL

    def projection_pieces(x_ref, slot):
        x = x_ref[...]
        h = (x * lax.rsqrt(jnp.mean(x * x, axis=-1, keepdims=True) + EPS) * n1w_ref[...]).astype(bf16)

        def slab(dst, c0, w0):
            def piece():
                dst[slot, :, c0:c0 + W_SLAB_COLS] = jnp.dot(h, wbf_ref[:, w0:w0 + W_SLAB_COLS],
                                                            preferred_element_type=f32)
            return piece

        def tail_piece():
            tail = jnp.dot(h, wtail_bf_ref[:, 0:n_tail], preferred_element_type=f32)
            lane = lax.broadcasted_iota(jnp.int32, (step_rows, LANES), 1)
            dt_s[slot] = jnp.where(lane < N_SSD_HEADS, tail[:, 0:LANES], 0.0)
            u_s[slot] = tail[:, N_SSD_HEADS:n_tail]

        pieces = [slab(z_s, c0, c0) for c0 in range(0, D_SSD, W_SLAB_COLS)]
        pieces += [slab(xbc_s, c0, D_SSD + c0) for c0 in range(0, D_CONV, W_SLAB_COLS)]
        return pieces + [tail_piece]

    def project(x_ref, slot):
        for piece in projection_pieces(x_ref, slot):
            piece()

    @pl.when(s == 0)
    def _():
        for k in range(n_main // W_SLAB_COLS):
            cp = pltpu.make_async_copy(w_hbm.at[0, :, pl.ds(k * W_SLAB_COLS, W_SLAB_COLS)], wstage_ref, wsem)
            cp.start()
            cp.wait()
            wbf_ref[:, k * W_SLAB_COLS:(k + 1) * W_SLAB_COLS] = wstage_ref[...].astype(bf16)
        wtail_bf_ref[...] = wtail_ref[0, :, 0:wtail_bf_ref.shape[1]].astype(bf16)
        project(x0_ref, 0)

    @pl.when(s % steps_per_seq == 0)
    def _():
        _reset_history(ext_ref, pool_tail_ref, state_ref)

    pieces = iter(projection_pieces(xn_ref, (s + 1) % 2))
    slot = s % 2
    c0 = (s % steps_per_seq) * PROMPT_BLKS_PER_STEP
    n_blk = steps_per_seq * PROMPT_BLKS_PER_STEP
    for ci in range(PROMPT_BLKS_PER_STEP):
        rows = pl.ds(ci * BLK, BLK)
        _prompt_block(c0 + ci, n_blk,
                      z_s.at[slot, rows], xbc_s.at[slot, rows], dt_s.at[slot, rows], u_s.at[slot, rows],
                      cw_ref, cb_ref, dtb_ref, alog_ref, dskip_ref, nw_ref, pw_ref, ps_ref, causal_ref,
                      mix_ref.at[rows], ssm_ref, ext_ref, pool_tail_ref, state_ref, edges=False,
                      side_work=pieces)
    for piece in pieces:
        piece()

    @pl.when(s % steps_per_seq == steps_per_seq - 1)
    def _():
        ssm_ref[0] = state_ref[...].reshape(N_SSD_HEADS, SSD_HEAD_DIM, D_STATE)
        ctail_ref[0] = xbc_s[slot, step_rows - CONV_TAIL_ROWS:step_rows, :]
        ptail_ref[0] = u_s[slot, step_rows - HIST_ROWS:step_rows, :]


def _prompt_fused(xp, n_seq, seq_len, norm1_w, w_in3, conv_w, conv_b, dt_bias, A_log, D_skip, ssd_norm_w,
                  pool_w, pool_scale):
    f32, bf16 = jnp.float32, jnp.bfloat16
    n, d = xp.shape
    step_rows = PROMPT_BLKS_PER_STEP * BLK
    steps_per_seq = seq_len // step_rows
    n_steps = n // step_rows
    s1 = D_SSD + D_CONV
    tail_blk = 1024
    tail_bf_cols = -(-(N_SSD_HEADS + D_POOL) // LANES) * LANES
    assert s1 % tail_blk == 0 and tail_bf_cols <= tail_blk and s1 % W_SLAB_COLS == 0
    k = _mixer_constants()

    def const(shape):
        return pl.BlockSpec(shape, lambda s: (0,) * len(shape))

    def per_seq(shape):
        return pl.BlockSpec(shape, lambda s: (s // steps_per_seq,) + (0,) * (len(shape) - 1))

    pad_h = (0, LANES - N_SSD_HEADS)
    return pl.pallas_call(
        functools.partial(_prompt_fused_body, steps_per_seq=steps_per_seq),
        grid=(n_steps,),
        in_specs=[pl.BlockSpec((step_rows, d), lambda s: (0, 0)),
                  pl.BlockSpec((step_rows, d), lambda s: (jnp.minimum(s + 1, n_steps - 1), 0)),
                  const((1, d)),
                  pl.BlockSpec(memory_space=pl.ANY),
                  pl.BlockSpec((1, d, tail_blk), lambda s: (0, 0, s1 // tail_blk), pipeline_mode=pl.Buffered(1)),
                  const((CONV_WIDTH, D_CONV)), const((1, D_CONV)), const((1, LANES)), const((1, LANES)),
                  const((1, D_SSD)), const((1, D_SSD)),
                  const((N_POOL_GROUPS, POOL_GROUP_DIM, POOL_GROUP_DIM)), const((1, D_POOL)),
                  const((BLK, BLK))],
        out_specs=[pl.BlockSpec((step_rows, D_MIX), lambda s: (s, 0)),
                   per_seq((1, N_SSD_HEADS, SSD_HEAD_DIM, D_STATE)),
                   per_seq((1, CONV_TAIL_ROWS, D_CONV)),
                   per_seq((1, HIST_ROWS, D_POOL))],
        out_shape=[jax.ShapeDtypeStruct((n, D_MIX), bf16),
                   jax.ShapeDtypeStruct((n_seq, N_SSD_HEADS, SSD_HEAD_DIM, D_STATE), f32),
                   jax.ShapeDtypeStruct((n_seq, CONV_TAIL_ROWS, D_CONV), f32),
                   jax.ShapeDtypeStruct((n_seq, HIST_ROWS, D_POOL), f32)],
        scratch_shapes=[pltpu.VMEM((d, s1), bf16),
                        pltpu.VMEM((d, tail_bf_cols), bf16),
                        pltpu.VMEM((d, W_SLAB_COLS), f32),
                        pltpu.VMEM((2, step_rows, D_SSD), f32),
                        pltpu.VMEM((2, step_rows, D_CONV), f32),
                        pltpu.VMEM((2, step_rows, LANES), f32),
                        pltpu.VMEM((2, step_rows, D_POOL), f32),
                        pltpu.VMEM((CONV_TAIL_ROWS + BLK, D_CONV), f32),
                        pltpu.VMEM((HIST_ROWS, D_POOL), f32),
                        pltpu.VMEM((D_SSD, D_STATE), f32),
                        pltpu.SemaphoreType.DMA(())],
        compiler_params=pltpu.CompilerParams(dimension_semantics=("arbitrary",),
                                             vmem_limit_bytes=VMEM_LIMIT_BYTES),
        name="prompt_fused",
    )(xp, xp, norm1_w.reshape(1, d).astype(f32), w_in3, w_in3,
      conv_w.astype(f32), conv_b.reshape(1, D_CONV).astype(f32),
      jnp.pad(dt_bias.astype(f32), pad_h).reshape(1, LANES), jnp.pad(A_log.astype(f32), pad_h).reshape(1, LANES),
      jnp.repeat(D_skip.astype(f32), SSD_HEAD_DIM).reshape(1, D_SSD), ssd_norm_w.reshape(1, D_SSD).astype(f32),
      pool_w.astype(bf16), pool_scale.reshape(1, D_POOL).astype(f32), k["causal"])


SEQ_PER_BLK = BLK // DEC_SEQ
SEQ_PER_STEP = 8


def _sample_constants():
    bf16 = jnp.bfloat16
    r = np.arange(BLK)
    sq, st = r // DEC_SEQ, r % DEC_SEQ
    same = sq[:, None] == sq[None, :]
    causal = same & (st[None, :] <= st[:, None])
    nk = CONV_WIDTH - 1
    shift = np.stack([same & (st[None, :] == st[:, None] + k - nk) for k in range(nk)])
    cs = np.arange(SEQ_PER_BLK * nk)
    stsel = np.stack([(cs[None, :] // nk == sq[:, None]) & (cs[None, :] % nk == st[:, None] + k)
                      for k in range(nk)])
    pcur = np.stack([causal & (st[:, None] - st[None, :] < w) for w in POOL_WINDOWS])
    hs = np.arange(SEQ_PER_BLK * POOL_HIST)
    phist = np.stack([(hs[None, :] // POOL_HIST == sq[:, None])
                      & (st[:, None] + POOL_HIST - hs[None, :] % POOL_HIST < w) for w in POOL_WINDOWS])
    as_bf = lambda a: jnp.asarray(a.astype(np.float32), bf16)
    return dict(same=as_bf(same), causal=as_bf(causal), shift=as_bf(shift), stsel=as_bf(stsel),
                pcur=as_bf(pcur), phist=as_bf(phist))


def _sample_mixer_body(z_ref, xbc_ref, dt_ref, u_ref, cst_ref, pst_ref, ssm_in_ref,
                       cw_ref, cb_ref, dtb_ref, alog_ref, dskip_ref, nw_ref, pw_ref, ps_ref,
                       causal_ref, same_ref, expand_ref, expand_t_ref, shift_ref, stsel_ref, pcur_ref, phist_ref,
                       mix_ref, ssm_out_ref,
                       ydiag_ref, ea_ref, yt_ref, cdh_ref, cdl_ref, xdte_t_ref, bm_ref, cm_ref, *, pos0):
    f32, bf16 = jnp.float32, jnp.bfloat16
    s = pl.program_id(1)
    gw = D_SSD // N_SSD_GROUPS

    @pl.when(s == 0)
    def _():
        xbc = xbc_ref[...]
        cst = cst_ref[...]
        acc = cb_ref[...] + cw_ref[CONV_WIDTH - 1:CONV_WIDTH, :] * xbc
        for k in range(CONV_WIDTH - 1):
            tap = _dot_sel_left(shift_ref[k], xbc, 3) + _dot_sel_left(stsel_ref[k], cst, 3)
            acc = acc + cw_ref[k:k + 1, :] * tap
        xbc_c = acc * jax.nn.sigmoid(acc)

        causal_bf = causal_ref[...]
        keep = causal_bf > 0
        xs, bm, cm, dt, a_cum, xdt = _ssd_intra(xbc_c, dt_ref[...], dtb_ref, alog_ref, causal_bf, expand_ref)
        a_tot = _dot_sel_left(same_ref[...], dt * (-jnp.exp(alog_ref[...])), 3)
        a_cum_t = jnp.transpose(a_cum)
        ea_ref[...] = _dot_sel_right(jnp.exp(a_cum), expand_ref[...], 2)
        dte_x = _dot_sel_right(jnp.exp(a_tot - a_cum), expand_ref[...], 2)
        cd_col = _dot_sel_left(expand_t_ref[...], jnp.exp(jnp.transpose(a_tot)), 2)
        cd_hi, cd_lo = _split2(cd_col)
        cdh_ref[...] = cd_hi
        cdl_ref[...] = cd_lo
        bm_ref[...] = bm.astype(bf16)
        cm_ref[...] = cm.astype(bf16)
        for g in range(N_SSD_GROUPS):
            cg = cm[:, g * D_STATE:(g + 1) * D_STATE].astype(bf16)
            bg = bm[:, g * D_STATE:(g + 1) * D_STATE].astype(bf16)
            cb = lax.dot_general(cg, bg, NT_DIMS, preferred_element_type=f32)
            y_diag = _ssd_diag_group(g, cb, a_cum, a_cum_t, keep, xdt)
            ydiag_ref[:, g * gw:(g + 1) * gw] = y_diag + xs[:, g * gw:(g + 1) * gw] * dskip_ref[:, g * gw:(g + 1) * gw]
            xdte_t_ref[g * gw:(g + 1) * gw, :] = jnp.transpose(
                xdt[:, g * gw:(g + 1) * gw] * dte_x[:, g * gw:(g + 1) * gw]).astype(bf16)
        yt_ref[...] = jnp.zeros(yt_ref.shape, f32)

        u = u_ref[...]
        pst = pst_ref[...]
        step = lax.broadcasted_iota(jnp.int32, (BLK, 1), 0) % DEC_SEQ
        pos = (step + (pos0 + 1)).astype(f32)
        for gi, w in enumerate(POOL_WINDOWS):
            sl = slice(gi * POOL_GROUP_DIM, (gi + 1) * POOL_GROUP_DIM)
            ug = u[:, sl]
            wsum = _dot_sel_left(pcur_ref[gi], ug, 2) + _dot_sel_left(phist_ref[gi], pst[:, sl], 2)
            pooled = wsum / jnp.minimum(pos, jnp.float32(w)) - ug
            po = jnp.dot(pooled.astype(bf16), pw_ref[gi], preferred_element_type=f32) * ps_ref[:, sl]
            mix_ref[:, D_SSD + gi * POOL_GROUP_DIM:D_SSD + (gi + 1) * POOL_GROUP_DIM] = po.astype(bf16)

    hg = N_SSD_HEADS // N_SSD_GROUPS
    row_seq = lax.broadcasted_iota(jnp.int32, (BLK, LANES), 0) // DEC_SEQ
    col_seq = lax.broadcasted_iota(jnp.int32, (gw, BLK), 1) // DEC_SEQ
    row_idx = lax.broadcasted_iota(jnp.int32, (BLK, LANES), 0)
    for q in range(SEQ_PER_STEP):
        sq = s * SEQ_PER_STEP + q
        rows_of_s = row_seq == sq
        cols_of_s = col_seq == sq
        pick_s = jnp.where(row_idx == DEC_SEQ * sq, 1.0, 0.0).astype(bf16)
        state = ssm_in_ref[q].reshape(D_SSD, D_STATE)
        for g in range(N_SSD_GROUPS):
            rs = slice(g * gw, (g + 1) * gw)
            sg = state[rs, :]
            cg = cm_ref[:, g * D_STATE:(g + 1) * D_STATE]
            bg = bm_ref[:, g * D_STATE:(g + 1) * D_STATE]
            yt = lax.dot_general(sg.astype(bf16), cg, NT_DIMS, preferred_element_type=f32)
            yt_ref[rs, :] += jnp.where(cols_of_s, yt, 0.0)
            cd = (jnp.dot(cdh_ref[rs, :], pick_s, preferred_element_type=f32)
                  + jnp.dot(cdl_ref[rs, :], pick_s, preferred_element_type=f32))
            upd = jnp.dot(xdte_t_ref[rs, :], jnp.where(rows_of_s, bg, jnp.zeros_like(bg)),
                          preferred_element_type=f32)
            ssm_out_ref[q, g * hg:(g + 1) * hg] = (sg * cd + upd).reshape(hg, SSD_HEAD_DIM, D_STATE)

    @pl.when(s == pl.num_programs(1) - 1)
    def _():
        y = ydiag_ref[...] + jnp.transpose(yt_ref[...]) * ea_ref[...]
        mix_ref[:, 0:D_SSD] = _gated_norm(y, z_ref[...], nw_ref).astype(bf16)


def _sample_mixer(z, xbc, dt, u, row0, n_seq, state_conv, state_ssm, state_pool, pos0,
                  conv_w, conv_b, dt_bias, A_log, D_skip, ssd_norm_w, pool_w, pool_scale):
    f32, bf16 = jnp.float32, jnp.bfloat16
    n_blk = n_seq // SEQ_PER_BLK
    blk0 = row0 // BLK
    nk = CONV_WIDTH - 1
    k = _mixer_constants()
    ks = _sample_constants()

    def row_blk(width):
        return pl.BlockSpec((BLK, width), lambda j, s: (blk0 + j, 0))

    def const(shape):
        return pl.BlockSpec(shape, lambda j, s: (0,) * len(shape))

    steps = SEQ_PER_BLK // SEQ_PER_STEP
    state_spec = pl.BlockSpec((SEQ_PER_STEP, N_SSD_HEADS, SSD_HEAD_DIM, D_STATE),
                              lambda j, s: (j * steps + s, 0, 0, 0))
    pad_h = (0, LANES - N_SSD_HEADS)
    return pl.pallas_call(
        functools.partial(_sample_mixer_body, pos0=pos0),
        grid=(n_blk, steps),
        in_specs=[row_blk(D_SSD), row_blk(D_CONV), row_blk(LANES), row_blk(D_POOL),
                  pl.BlockSpec((SEQ_PER_BLK * nk, D_CONV), lambda j, s: (j, 0)),
                  pl.BlockSpec((SEQ_PER_BLK * POOL_HIST, D_POOL), lambda j, s: (j, 0)),
                  state_spec,
                  const((CONV_WIDTH, D_CONV)), const((1, D_CONV)), const((1, LANES)), const((1, LANES)),
                  const((1, D_SSD)), const((1, D_SSD)),
                  const((N_POOL_GROUPS, POOL_GROUP_DIM, POOL_GROUP_DIM)), const((1, D_POOL)),
                  const((BLK, BLK)), const((BLK, BLK)), const((LANES, D_SSD)), const((D_SSD, LANES)),
                  const((nk, BLK, BLK)), const((nk, BLK, SEQ_PER_BLK * nk)),
                  const((N_POOL_GROUPS, BLK, BLK)), const((N_POOL_GROUPS, BLK, SEQ_PER_BLK * POOL_HIST))],
        out_specs=[pl.BlockSpec((BLK, D_MIX), lambda j, s: (j, 0)), state_spec],
        out_shape=[jax.ShapeDtypeStruct((n_seq * DEC_SEQ, D_MIX), bf16),
                   jax.ShapeDtypeStruct((n_seq, N_SSD_HEADS, SSD_HEAD_DIM, D_STATE), f32)],
        scratch_shapes=[pltpu.VMEM((BLK, D_SSD), f32), pltpu.VMEM((BLK, D_SSD), f32),
                        pltpu.VMEM((D_SSD, BLK), f32), pltpu.VMEM((D_SSD, BLK), bf16),
                        pltpu.VMEM((D_SSD, BLK), bf16), pltpu.VMEM((D_SSD, BLK), bf16),
                        pltpu.VMEM((BLK, N_SSD_GROUPS * D_STATE), bf16),
                        pltpu.VMEM((BLK, N_SSD_GROUPS * D_STATE), bf16)],
        compiler_params=pltpu.CompilerParams(dimension_semantics=("parallel", "arbitrary"),
                                             vmem_limit_bytes=VMEM_LIMIT_BYTES),
        name="sample_mixer",
    )(z, xbc, dt, u, state_conv.reshape(n_seq * nk, D_CONV), state_pool.reshape(n_seq * POOL_HIST, D_POOL),
      state_ssm, conv_w.astype(f32), conv_b.reshape(1, D_CONV).astype(f32),
      jnp.pad(dt_bias.astype(f32), pad_h).reshape(1, LANES), jnp.pad(A_log.astype(f32), pad_h).reshape(1, LANES),
      jnp.repeat(D_skip.astype(f32), SSD_HEAD_DIM).reshape(1, D_SSD), ssd_norm_w.reshape(1, D_SSD).astype(f32),
      pool_w.astype(bf16), pool_scale.reshape(1, D_POOL).astype(f32),
      ks["causal"], ks["same"], k["expand"], k["expand_t"], ks["shift"], ks["stsel"],
      ks["pcur"], ks["phist"])


def _moe_sizes(n_tokens, tm):
    nt = n_tokens // tm
    lmax = -(-(TOP_K * tm + N_EXPERTS * MOE_SEG_ROWS) // MOE_SEL_ROWS) * MOE_SEL_ROWS
    rows = (TOP_K * n_tokens + nt * N_EXPERTS * MOE_SEG_ROWS + N_EXPERTS * (MOE_PIECE - 1)
            + MOE_ROW_CHUNK)
    n_rows = -(-rows // MOE_PIECE) * MOE_PIECE
    return nt, lmax, n_rows


def _router_body(x_ref, nw_ref, rwh_ref, rwl_ref, rb_ref, h_ref, posg_ref, post_ref, cnt_ref):
    f32, bf16 = jnp.float32, jnp.bfloat16
    tm = x_ref.shape[0]
    x = x_ref[...]
    h = x * lax.rsqrt(jnp.mean(x * x, axis=-1, keepdims=True) + EPS) * nw_ref[...]
    h_hi = h.astype(bf16)
    h_ref[...] = h_hi
    h_lo = (h - h_hi.astype(f32)).astype(bf16)
    wh = rwh_ref[...]
    logits = (jnp.dot(h_hi, wh, preferred_element_type=f32)
              + jnp.dot(h_lo, wh, preferred_element_type=f32)
              + jnp.dot(h_hi, rwl_ref[...], preferred_element_type=f32)) + rb_ref[...]
    lane = lax.broadcasted_iota(jnp.int32, (tm, LANES), 1)
    lanef = lane.astype(f32)
    neg = jnp.float32(-jnp.inf)
    l = jnp.where(lane < N_EXPERTS, logits, neg)
    sels, vals = [], []
    for _ in range(TOP_K):
        m = jnp.max(l, axis=1, keepdims=True)
        idx = jnp.min(jnp.where(l == m, lanef, jnp.float32(LANES)), axis=1, keepdims=True)
        sel = lanef == idx
        l = jnp.where(sel, neg, l)
        sels.append(sel)
        vals.append(m)
    exps = [jnp.exp(v - vals[0]) for v in vals]
    denom = exps[0] + exps[1] + exps[2] + exps[3]
    gates = [e / denom for e in exps]
    chosen = jnp.where(sels[0] | sels[1] | sels[2] | sels[3], 1.0, 0.0).astype(f32)
    row = lax.broadcasted_iota(jnp.int32, (tm, tm), 0)
    col = lax.broadcasted_iota(jnp.int32, (tm, tm), 1)
    lower = jnp.where(col < row, 1.0, 0.0).astype(bf16)
    rank = jnp.dot(lower, chosen.astype(bf16), preferred_element_type=f32)
    cnt = jnp.sum(chosen, axis=0, keepdims=True)
    seg_units = jnp.maximum(jnp.floor((cnt + (MOE_SEG_ROWS - 1)) * (1.0 / MOE_SEG_ROWS)), 1.0)
    r2 = lax.broadcasted_iota(jnp.int32, (LANES, LANES), 0)
    c2 = lax.broadcasted_iota(jnp.int32, (LANES, LANES), 1)
    upper = jnp.where(r2 < c2, 1.0, 0.0).astype(bf16)
    lstart = jnp.dot(jnp.broadcast_to(seg_units, (8, LANES)).astype(bf16), upper,
                     preferred_element_type=f32)[0:1, :] * MOE_SEG_ROWS
    posmat = lstart + rank
    posg = jnp.zeros((tm, LANES), f32)
    for k in range(TOP_K):
        pos_k = jnp.sum(jnp.where(sels[k], posmat, 0.0), axis=1, keepdims=True)
        posg = posg + jnp.where(lane == k, pos_k, 0.0) + jnp.where(lane == TOP_K + k, gates[k], 0.0)
    posg_ref[...] = posg
    post_ref[...] = jnp.transpose(posg)[0:8, :]
    cnt_ref[0] = jnp.broadcast_to(cnt, (8, LANES)).astype(jnp.int32)


def _moe_router(x1, norm2_w, router_w, router_b, tm):
    n, d = x1.shape
    nt = n // tm
    f32, bf16 = jnp.float32, jnp.bfloat16
    rw = jnp.pad(router_w.astype(f32), ((0, 0), (0, LANES - N_EXPERTS)))
    rw_hi = rw.astype(bf16)
    rw_lo = (rw - rw_hi.astype(f32)).astype(bf16)
    rb = jnp.pad(router_b.astype(f32), (0, LANES - N_EXPERTS)).reshape(1, LANES)
    return pl.pallas_call(
        _router_body,
        grid=(nt,),
        in_specs=[pl.BlockSpec((tm, d), lambda i: (i, 0)),
                  pl.BlockSpec((1, d), lambda i: (0, 0)),
                  pl.BlockSpec((d, LANES), lambda i: (0, 0)),
                  pl.BlockSpec((d, LANES), lambda i: (0, 0)),
                  pl.BlockSpec((1, LANES), lambda i: (0, 0))],
        out_specs=[pl.BlockSpec((tm, d), lambda i: (i, 0)),
                   pl.BlockSpec((tm, LANES), lambda i: (i, 0)),
                   pl.BlockSpec((8, tm), lambda i: (0, i)),
                   pl.BlockSpec((1, 8, LANES), lambda i: (i, 0, 0))],
        out_shape=[jax.ShapeDtypeStruct((n, d), bf16),
                   jax.ShapeDtypeStruct((n, LANES), f32),
                   jax.ShapeDtypeStruct((8, n), f32),
                   jax.ShapeDtypeStruct((nt, 8, LANES), jnp.int32)],
        compiler_params=pltpu.CompilerParams(dimension_semantics=("parallel",),
                                             vmem_limit_bytes=VMEM_LIMIT_BYTES),
        name="moe_router",
    )(x1, norm2_w.reshape(1, d).astype(f32), rw_hi, rw_lo, rb)


def _moe_plan(cnt):
    i32 = jnp.int32
    pad = jnp.maximum((cnt + (MOE_SEG_ROWS - 1)) // MOE_SEG_ROWS, 1) * MOE_SEG_ROWS
    lstart = jnp.cumsum(pad, axis=1) - pad
    lp = jnp.sum(pad, axis=1)
    tot = jnp.sum(pad, axis=0)
    reg = (tot + (MOE_PIECE - 1)) // MOE_PIECE * MOE_PIECE
    reg_end = jnp.cumsum(reg)
    estart = reg_end - reg
    seg = estart[None, :] + jnp.cumsum(pad, axis=0) - pad
    return dict(
        lstart=lstart.reshape(-1).astype(i32), seg_units=(pad // MOE_SEG_ROWS).reshape(-1).astype(i32),
        seg=seg.reshape(-1).astype(i32), lp=lp.astype(i32),
        tail_start=(estart + tot).astype(i32), tail_units=((reg - tot) // MOE_SEG_ROWS).astype(i32),
        estart=estart.astype(i32), erows=reg.astype(i32), used=reg_end[-1].reshape(1).astype(i32))


def _for_each_segment_copy(i, lstart_ref, units_ref, seg_ref, local_ref, global_ref, sem, to_global, fn):
    def per_expert(e, carry):
        k = i * N_EXPERTS + e
        n = pl.multiple_of(units_ref[k] * MOE_SEG_ROWS, MOE_SEG_ROWS)
        loc = local_ref.at[pl.ds(pl.multiple_of(lstart_ref[k], MOE_SEG_ROWS), n)]
        glo = global_ref.at[pl.ds(pl.multiple_of(seg_ref[k], MOE_SEG_ROWS), n)]
        fn(pltpu.make_async_copy(loc, glo, sem) if to_global else pltpu.make_async_copy(glo, loc, sem))
        return carry
    lax.fori_loop(0, N_EXPERTS, per_expert, 0)


def _wait_segment_copies(i, lp_ref, local_ref, global_ref, sem):
    n = pl.multiple_of(lp_ref[i], MOE_SEG_ROWS)
    pltpu.make_async_copy(global_ref.at[pl.ds(0, n)], local_ref.at[pl.ds(0, n)], sem).wait()


def _for_each_unused_piece(used_ref, zero_ref, rows_ref, sem, fn):
    def per_piece(j, c):
        go = pl.multiple_of(j * MOE_PIECE, MOE_PIECE)
        fn(pltpu.make_async_copy(zero_ref, rows_ref.at[pl.ds(go, MOE_PIECE)], sem))
        return c
    lax.fori_loop(used_ref[0] // MOE_PIECE, rows_ref.shape[0] // MOE_PIECE, per_piece, 0)


def _dispatch_body(lstart_ref, units_ref, seg_ref, lp_ref, tail_start_ref, tail_units_ref, used_ref,
                   h_ref, post_ref, xs_ref, stage_ref, sel_ref, zero_ref, sems, fill_sem):
    f32, bf16 = jnp.float32, jnp.bfloat16
    i = pl.program_id(0)
    nt = pl.num_programs(0)
    slot = i % 2
    tm = h_ref.shape[0]
    lmax = stage_ref.shape[1]

    def for_each_fill_copy(fn):
        def per_expert(e, carry):
            @pl.when(tail_units_ref[e] > 0)
            def _():
                n = pl.multiple_of(tail_units_ref[e] * MOE_SEG_ROWS, MOE_SEG_ROWS)
                go = pl.multiple_of(tail_start_ref[e], MOE_SEG_ROWS)
                fn(pltpu.make_async_copy(zero_ref.at[pl.ds(0, n)], xs_ref.at[pl.ds(go, n)], fill_sem))
            return carry
        lax.fori_loop(0, N_EXPERTS, per_expert, 0)
        _for_each_unused_piece(used_ref, zero_ref, xs_ref, fill_sem, fn)

    @pl.when(i == 0)
    def _():
        zero_ref[...] = jnp.zeros(zero_ref.shape, bf16)
        for_each_fill_copy(lambda cp: cp.start())
        for_each_fill_copy(lambda cp: cp.wait())

    def segment_copies(tile, slot_, fn):
        _for_each_segment_copy(tile, lstart_ref, units_ref, seg_ref, stage_ref.at[slot_], xs_ref,
                               sems.at[slot_], True, fn)

    @pl.when(i >= 2)
    def _():
        _wait_segment_copies(i - 2, lp_ref, stage_ref.at[slot], xs_ref, sems.at[slot])

    pos = [post_ref[k:k + 1, :] for k in range(TOP_K)]
    r_local = lax.broadcasted_iota(jnp.int32, (MOE_CHUNK, tm), 0).astype(f32).astype(bf16)
    one, zero = jnp.ones((), bf16), jnp.zeros((), bf16)
    for part in range(lmax // MOE_SEL_ROWS):
        for c in range(MOE_SEL_ROWS // MOE_CHUNK):
            r0 = part * MOE_SEL_ROWS + c * MOE_CHUNK
            loc = [jnp.clip(p - r0, -1.0, float(MOE_CHUNK)).astype(bf16) for p in pos]
            hit = (loc[0] == r_local) | (loc[1] == r_local) | (loc[2] == r_local) | (loc[3] == r_local)
            sel_ref[part, c * MOE_CHUNK:(c + 1) * MOE_CHUNK, :] = jnp.where(hit, one, zero)
        stage_ref[slot, part * MOE_SEL_ROWS:(part + 1) * MOE_SEL_ROWS, :] = jnp.dot(
            sel_ref[part], h_ref[...], preferred_element_type=f32).astype(bf16)

    segment_copies(i, slot, lambda cp: cp.start())

    @pl.when(i == nt - 1)
    def _():
        @pl.when(nt >= 2)
        def _():
            _wait_segment_copies(i - 1, lp_ref, stage_ref.at[1 - slot], xs_ref, sems.at[1 - slot])
        _wait_segment_copies(i, lp_ref, stage_ref.at[slot], xs_ref, sems.at[slot])


def _moe_dispatch(h2, post, plan, tm, lmax, n_rows):
    n, d = h2.shape
    nt = n // tm
    grid_spec = pltpu.PrefetchScalarGridSpec(
        num_scalar_prefetch=7,
        grid=(nt,),
        in_specs=[pl.BlockSpec((tm, d), lambda i, *_: (i, 0)),
                  pl.BlockSpec((8, tm), lambda i, *_: (0, i))],
        out_specs=pl.BlockSpec(memory_space=pl.ANY),
        scratch_shapes=[pltpu.VMEM((2, lmax, d), jnp.bfloat16),
                        pltpu.VMEM((lmax // MOE_SEL_ROWS, MOE_SEL_ROWS, tm), jnp.bfloat16),
                        pltpu.VMEM((MOE_PIECE, d), jnp.bfloat16),
                        pltpu.SemaphoreType.DMA((2,)),
                        pltpu.SemaphoreType.DMA(())])
    return pl.pallas_call(
        _dispatch_body,
        grid_spec=grid_spec,
        out_shape=jax.ShapeDtypeStruct((n_rows, d), jnp.bfloat16),
        compiler_params=pltpu.CompilerParams(dimension_semantics=("arbitrary",),
                                             vmem_limit_bytes=VMEM_LIMIT_BYTES),
        name="moe_dispatch",
    )(plan["lstart"], plan["seg_units"], plan["seg"], plan["lp"], plan["tail_start"], plan["tail_units"],
      plan["used"], h2, post)


def _experts_body(first_ref, count_ref, cstart_ref, cvalid_ref, total_ref, used_ref,
                  xs_ref, wgu_ref, bgu_ref, wd_ref, bd_ref, os_ref,
                  wgu_bf, wd_bf, xbuf, obuf, zero_ref, in_sems, out_sems, fill_sem):
    f32, bf16 = jnp.float32, jnp.bfloat16
    e = pl.program_id(0)
    total = total_ref[0]
    half = D_FF // 2

    def in_copy(j):
        src = xs_ref.at[pl.ds(pl.multiple_of(cstart_ref[j], MOE_PIECE), MOE_ROW_CHUNK)]
        return pltpu.make_async_copy(src, xbuf.at[j % 3], in_sems.at[j % 3])

    def out_copy(j):
        n = pl.multiple_of(cvalid_ref[j], MOE_PIECE)
        go = pl.multiple_of(cstart_ref[j], MOE_PIECE)
        return pltpu.make_async_copy(obuf.at[j % 2, pl.ds(0, n)], os_ref.at[pl.ds(go, n)], out_sems.at[j % 2])

    @pl.when(e == 0)
    def _():
        for j0 in range(2):
            @pl.when(j0 < total)
            def _():
                in_copy(j0).start()
        zero_ref[...] = jnp.zeros(zero_ref.shape, bf16)
        _for_each_unused_piece(used_ref, zero_ref, os_ref, fill_sem, lambda cp: cp.start())
        _for_each_unused_piece(used_ref, zero_ref, os_ref, fill_sem, lambda cp: cp.wait())

    @pl.when(count_ref[e] > 0)
    def _():
        wgu_bf[...] = wgu_ref[0].astype(bf16)
        wd_bf[...] = wd_ref[0].astype(bf16)

        def chunk(j, carry):
            in_copy(j).wait()

            @pl.when(j + 2 < total)
            def _():
                in_copy(j + 2).start()

            @pl.when(j >= 2)
            def _():
                out_copy(j - 2).wait()

            def mlp(n_rows):
                x = xbuf[j % 3, 0:n_rows, :]
                out = bd_ref[0]
                for hf in range(2):
                    gate = jnp.dot(x, wgu_bf[:, hf * half:(hf + 1) * half], preferred_element_type=f32)
                    gate = jnp.minimum(gate + bgu_ref[0, :, hf * half:(hf + 1) * half], SWIGLU_LIMIT)
                    up = jnp.dot(x, wgu_bf[:, D_FF + hf * half:D_FF + (hf + 1) * half],
                                 preferred_element_type=f32)
                    up = jnp.clip(up + bgu_ref[0, :, D_FF + hf * half:D_FF + (hf + 1) * half],
                                  -SWIGLU_LIMIT, SWIGLU_LIMIT)
                    act = (up + 1.0) * (gate * jax.nn.sigmoid(SWIGLU_ALPHA * gate))
                    out = out + jnp.dot(act.astype(bf16), wd_bf[hf * half:(hf + 1) * half, :],
                                        preferred_element_type=f32)
                obuf[j % 2, 0:n_rows, :] = out.astype(bf16)

            @pl.when(cvalid_ref[j] > MOE_ROW_CHUNK // 2)
            def _():
                mlp(MOE_ROW_CHUNK)

            @pl.when(cvalid_ref[j] <= MOE_ROW_CHUNK // 2)
            def _():
                mlp(MOE_ROW_CHUNK // 2)
            out_copy(j).start()
            return carry
        lax.fori_loop(first_ref[e], first_ref[e] + count_ref[e], chunk, 0)

    @pl.when(e == pl.num_programs(0) - 1)
    def _():
        @pl.when(total >= 2)
        def _():
            out_copy(total - 2).wait()

        @pl.when(total >= 1)
        def _():
            out_copy(total - 1).wait()


def _expert_chunks(plan, n_rows):
    i32 = jnp.int32
    max_chunks = n_rows // MOE_ROW_CHUNK + N_EXPERTS
    count = (plan["erows"] + (MOE_ROW_CHUNK - 1)) // MOE_ROW_CHUNK
    end = jnp.cumsum(count)
    first = end - count
    j = jnp.arange(max_chunks, dtype=i32)
    mine = ((first[None, :] <= j[:, None]) & (j[:, None] < end[None, :])).astype(i32)
    c = j - jnp.sum(mine * first[None, :], axis=1)
    cstart = jnp.sum(mine * plan["estart"][None, :], axis=1) + jnp.sum(mine, axis=1) * c * MOE_ROW_CHUNK
    cvalid = jnp.sum(mine * jnp.clip(plan["erows"][None, :] - c[:, None] * MOE_ROW_CHUNK, 0, MOE_ROW_CHUNK),
                     axis=1)
    return (first.astype(i32), count.astype(i32), cstart.astype(i32), cvalid.astype(i32),
            end[-1].reshape(1).astype(i32))


def _moe_experts(xs, plan, w_gate_up, b_gate_up, w_down, b_down):
    d = xs.shape[1]
    grid_spec = pltpu.PrefetchScalarGridSpec(
        num_scalar_prefetch=6,
        grid=(N_EXPERTS,),
        in_specs=[pl.BlockSpec(memory_space=pl.ANY),
                  pl.BlockSpec((1, d, 2 * D_FF), lambda e, *_: (e, 0, 0)),
                  pl.BlockSpec((1, 1, 2 * D_FF), lambda e, *_: (e, 0, 0)),
                  pl.BlockSpec((1, D_FF, d), lambda e, *_: (e, 0, 0)),
                  pl.BlockSpec((1, 1, d), lambda e, *_: (e, 0, 0))],
        out_specs=pl.BlockSpec(memory_space=pl.ANY),
        scratch_shapes=[pltpu.VMEM((d, 2 * D_FF), jnp.bfloat16),
                        pltpu.VMEM((D_FF, d), jnp.bfloat16),
                        pltpu.VMEM((3, MOE_ROW_CHUNK, d), jnp.bfloat16),
                        pltpu.VMEM((2, MOE_ROW_CHUNK, d), jnp.bfloat16),
                        pltpu.VMEM((MOE_PIECE, d), jnp.bfloat16),
                        pltpu.SemaphoreType.DMA((3,)),
                        pltpu.SemaphoreType.DMA((2,)),
                        pltpu.SemaphoreType.DMA(())])
    return pl.pallas_call(
        _experts_body,
        grid_spec=grid_spec,
        out_shape=jax.ShapeDtypeStruct(xs.shape, jnp.bfloat16),
        compiler_params=pltpu.CompilerParams(dimension_semantics=("arbitrary",),
                                             vmem_limit_bytes=VMEM_LIMIT_BYTES),
        name="moe_experts",
    )(*_expert_chunks(plan, xs.shape[0]), plan["used"],
      xs, w_gate_up, b_gate_up.reshape(N_EXPERTS, 1, 2 * D_FF), w_down, b_down.reshape(N_EXPERTS, 1, d))


def _combine_body(lstart_ref, units_ref, seg_ref, lp_ref,
                  os_ref, posg_ref, x_ref, fw_ref, yp_ref, ys_ref, stage_ref, w_ref, sems,
                  *, n_prompt_tiles):
    f32, bf16 = jnp.float32, jnp.bfloat16
    i = pl.program_id(0)
    nt = pl.num_programs(0)
    slot = i % 2
    tm = x_ref.shape[0]
    lmax = stage_ref.shape[1]

    def segment_copies(tile, slot_, fn):
        _for_each_segment_copy(tile, lstart_ref, units_ref, seg_ref, stage_ref.at[slot_], os_ref,
                               sems.at[slot_], False, fn)

    @pl.when(i == 0)
    def _():
        stage_ref[...] = jnp.zeros(stage_ref.shape, bf16)
        segment_copies(0, 0, lambda cp: cp.start())

    @pl.when(i + 1 < nt)
    def _():
        segment_copies(i + 1, 1 - slot, lambda cp: cp.start())

    posg = posg_ref[...]
    pos = [posg[:, k:k + 1] for k in range(TOP_K)]
    gate = [posg[:, TOP_K + k:TOP_K + k + 1] for k in range(TOP_K)]
    gate_bf = [g.astype(bf16) for g in gate]
    r_local = lax.broadcasted_iota(jnp.int32, (tm, MOE_CHUNK), 1).astype(f32).astype(bf16)
    _wait_segment_copies(i, lp_ref, stage_ref.at[slot], os_ref, sems.at[slot])
    y = x_ref[...]
    part_cols = lmax // 2
    for part in range(2):
        for c in range(part * part_cols // MOE_CHUNK, (part + 1) * part_cols // MOE_CHUNK):
            w = jnp.zeros((tm, MOE_CHUNK), bf16)
            for k in range(TOP_K):
                loc = jnp.clip(pos[k] - c * MOE_CHUNK, -1.0, float(MOE_CHUNK)).astype(bf16)
                w = jnp.where(loc == r_local, gate_bf[k], w)
            w_ref[:, c * MOE_CHUNK:(c + 1) * MOE_CHUNK] = w
        y = y + jnp.dot(w_ref[:, part * part_cols:(part + 1) * part_cols],
                        stage_ref[slot, part * part_cols:(part + 1) * part_cols, :],
                        preferred_element_type=f32)
    out = y * lax.rsqrt(jnp.mean(y * y, axis=-1, keepdims=True) + EPS) * fw_ref[...]

    @pl.when(i < n_prompt_tiles)
    def _():
        yp_ref[...] = out

    @pl.when(i >= n_prompt_tiles)
    def _():
        ys_ref[...] = out


def _moe_combine(os_, posg, x1, final_norm_w, plan, tm, lmax, n_prompt):
    n, d = x1.shape
    nt = n // tm
    n_prompt_tiles = n_prompt // tm
    n_sample_tiles = nt - n_prompt_tiles
    grid_spec = pltpu.PrefetchScalarGridSpec(
        num_scalar_prefetch=4,
        grid=(nt,),
        in_specs=[pl.BlockSpec(memory_space=pl.ANY),
                  pl.BlockSpec((tm, LANES), lambda i, *_: (i, 0)),
                  pl.BlockSpec((tm, d), lambda i, *_: (i, 0)),
                  pl.BlockSpec((1, d), lambda i, *_: (0, 0))],
        out_specs=[pl.BlockSpec((tm, d), lambda i, *_: (jnp.minimum(i, n_prompt_tiles - 1), 0)),
                   pl.BlockSpec((tm, d), lambda i, *_: (jnp.maximum(i - n_prompt_tiles, 0), 0))],
        scratch_shapes=[pltpu.VMEM((2, lmax, d), jnp.bfloat16),
                        pltpu.VMEM((tm, lmax), jnp.bfloat16),
                        pltpu.SemaphoreType.DMA((2,))])
    return pl.pallas_call(
        functools.partial(_combine_body, n_prompt_tiles=n_prompt_tiles),
        grid_spec=grid_spec,
        out_shape=[jax.ShapeDtypeStruct((n_prompt, d), jnp.float32),
                   jax.ShapeDtypeStruct((n_sample_tiles * tm, d), jnp.float32)],
        compiler_params=pltpu.CompilerParams(dimension_semantics=("arbitrary",),
                                             vmem_limit_bytes=VMEM_LIMIT_BYTES),
        name="moe_combine",
    )(plan["lstart"], plan["seg_units"], plan["seg"], plan["lp"],
      os_, posg, x1, final_norm_w.reshape(1, d).astype(jnp.float32))


def _moe_block(x1, n_prompt, norm2_w, router_w, router_b, w_gate_up, b_gate_up, w_down, b_down,
               final_norm_w, tm=MOE_TOKEN_TILE):
    n = x1.shape[0]
    nt, lmax, n_rows = _moe_sizes(n, tm)
    h2, posg, post, cnt3 = _moe_router(x1, norm2_w, router_w, router_b, tm)
    plan = _moe_plan(cnt3[:, 0, :N_EXPERTS])
    xs = _moe_dispatch(h2, post, plan, tm, lmax, n_rows)
    os_ = _moe_experts(xs, plan, w_gate_up, b_gate_up, w_down, b_down)
    return _moe_combine(os_, posg, x1, final_norm_w, plan, tm, lmax, n_prompt)


def kernel(x_prompt, x_sample, state_ssm, state_conv, state_pool, norm1_w, w_in, conv_w, conv_b, dt_bias,
           A_log, D_skip, ssd_norm_w, pool_w, pool_scale, w_out, norm2_w, router_w, router_b, w_gate_up,
           b_gate_up, w_down, b_down, final_norm_w):
    n_prompt = BATCH * SEQ
    n_sample = DEC_BATCH * DEC_SEQ
    xp = x_prompt.reshape(n_prompt, D_MODEL)
    xs = x_sample.reshape(n_sample, D_MODEL)
    mp = (conv_w[0], conv_b[0], dt_bias[0], A_log[0], D_skip[0], ssd_norm_w[0], pool_w[0], pool_scale[0])
    mix_p, s1, ctail, ptail = _prompt_fused(xp, BATCH, SEQ, norm1_w[0], w_in, *mp)
    half = n_sample // 2
    z, xbc, dt_raw, u = _in_proj(xs[:half], xs[half:], norm1_w[0], w_in, tm=half)
    mix_s, s2 = _sample_mixer(z, xbc, dt_raw, u, 0, DEC_BATCH, state_conv, state_ssm[0], state_pool,
                              PAST_LEN, *mp)
    nk = CONV_WIDTH - 1
    c1 = ctail[:, CONV_TAIL_ROWS - nk:]
    p1 = ptail[:, HIST_ROWS - POOL_HIST:]
    c2 = xbc.reshape(DEC_BATCH, DEC_SEQ, D_CONV)[:, DEC_SEQ - nk:]
    p2 = jnp.concatenate([state_pool[0][:, DEC_SEQ:], u.reshape(DEC_BATCH, DEC_SEQ, D_POOL)], axis=1)
    x1 = _out_proj(mix_p, mix_s, w_out[0], xp, xs)
    yp, ys = _moe_block(x1, n_prompt, norm2_w[0], router_w[0], router_b[0], w_gate_up[0], b_gate_up[0],
                        w_down[0], b_down[0], final_norm_w)
    return (yp.reshape(x_prompt.shape), ys.reshape(x_sample.shape),
            s1[None], c1[None], p1[None], s2[None], c2[None], p2[None])
```

```python
import functools
import jax, jax.numpy as jnp
from jax import lax
import numpy as np
from jax.experimental import pallas as pl
from jax.experimental.pallas import tpu as pltpu

D_MODEL = 1024
BATCH = 8
SEQ = 2048
DEC_BATCH = 128
DEC_SEQ = 4
PAST_LEN = 16384

D_MIX = 2 * D_MODEL
D_SSD = 3 * D_MIX // 4
SSD_HEAD_DIM = 64
N_SSD_HEADS = D_SSD // SSD_HEAD_DIM
N_SSD_GROUPS = 4
D_STATE = 128
CONV_WIDTH = 4
SSD_CHUNK = 128
D_CONV = D_SSD + 2 * N_SSD_GROUPS * D_STATE
D_POOL = D_MIX - D_SSD
POOL_WINDOWS = (2, 4, 8, 16)
N_POOL_GROUPS = len(POOL_WINDOWS)
POOL_GROUP_DIM = D_POOL // N_POOL_GROUPS
POOL_HIST = max(POOL_WINDOWS) - 1
D_IN_PROJ = D_SSD + D_CONV + N_SSD_HEADS + D_POOL
N_EXPERTS = 32
TOP_K = 4
D_FF = D_MODEL
SWIGLU_LIMIT = 7.0
SWIGLU_ALPHA = 1.702
EPS = 1e-5

LANES = 128
BF16_SUBLANES = 16
VMEM_LIMIT_BYTES = 48 * 1024 * 1024

MOE_TOKEN_TILE = 512
MOE_SEG_ROWS = BF16_SUBLANES
MOE_PIECE = 128
MOE_ROW_CHUNK = 512
MOE_CHUNK = 256
MOE_SEL_ROWS = 512


BLK = SSD_CHUNK
PROJ_ROW_TILE = 512
PROMPT_BLKS_PER_STEP = 2
HIST_ROWS = 16
CONV_TAIL_ROWS = 8
NT_DIMS = (((1,), (1,)), ((), ()))


def _split2(v):
    hi = v.astype(jnp.bfloat16)
    lo = (v - hi.astype(jnp.float32)).astype(jnp.bfloat16)
    return hi, lo


def _dot_sel_left(sel, v, passes):
    out = None
    rem = v
    for p in range(passes):
        part = rem.astype(jnp.bfloat16)
        d = jnp.dot(sel, part, preferred_element_type=jnp.float32)
        out = d if out is None else out + d
        if p + 1 < passes:
            rem = rem - part.astype(jnp.float32)
    return out


def _dot_sel_right(v, sel, passes):
    out = None
    rem = v
    for p in range(passes):
        part = rem.astype(jnp.bfloat16)
        d = jnp.dot(part, sel, preferred_element_type=jnp.float32)
        out = d if out is None else out + d
        if p + 1 < passes:
            rem = rem - part.astype(jnp.float32)
    return out


def _two_part_specs(n_first, n_second, tm, width):
    t1 = n_first // tm
    t2 = n_second // tm
    return (pl.BlockSpec((tm, width), lambda i: (jnp.minimum(i, t1 - 1), 0)),
            pl.BlockSpec((tm, width), lambda i: (jnp.clip(i - t1, 0, t2 - 1), 0)))


IN_PROJ_COLS = 512


def _in_proj_body(xa_ref, xb_ref, nw_ref, wmain_ref, wtail_ref, z_ref, xbc_ref, dt_ref, u_ref, *, tiles_a):
    f32, bf16 = jnp.float32, jnp.bfloat16
    x = jnp.where(pl.program_id(0) < tiles_a, xa_ref[...], xb_ref[...])
    h = (x * lax.rsqrt(jnp.mean(x * x, axis=-1, keepdims=True) + EPS) * nw_ref[...]).astype(bf16)
    off = 0
    for ref in (z_ref, xbc_ref):
        for c0 in range(0, ref.shape[1], IN_PROJ_COLS):
            w = wmain_ref[0, :, off + c0:off + c0 + IN_PROJ_COLS].astype(bf16)
            ref[:, c0:c0 + IN_PROJ_COLS] = jnp.dot(h, w, preferred_element_type=f32)
        off += ref.shape[1]
    tail = jnp.dot(h, wtail_ref[0, :, 0:N_SSD_HEADS + D_POOL].astype(bf16), preferred_element_type=f32)
    lane = lax.broadcasted_iota(jnp.int32, (tail.shape[0], LANES), 1)
    dt_ref[...] = jnp.where(lane < N_SSD_HEADS, tail[:, 0:LANES], 0.0)
    u_ref[...] = tail[:, N_SSD_HEADS:N_SSD_HEADS + D_POOL]


def _in_proj(xa, xb, norm1_w, w_in3, tm=PROJ_ROW_TILE):
    d = xa.shape[1]
    n = xa.shape[0] + xb.shape[0]
    f32, bf16 = jnp.float32, jnp.bfloat16
    s1 = D_SSD + D_CONV
    tail_blk = 1024
    assert s1 % tail_blk == 0 and N_SSD_HEADS + D_POOL <= tail_blk
    widths = (D_SSD, D_CONV, LANES, D_POOL)
    return pl.pallas_call(
        functools.partial(_in_proj_body, tiles_a=xa.shape[0] // tm),
        grid=(n // tm,),
        in_specs=[*_two_part_specs(xa.shape[0], xb.shape[0], tm, d),
                  pl.BlockSpec((1, d), lambda i: (0, 0)),
                  pl.BlockSpec((1, d, s1), lambda i: (0, 0, 0), pipeline_mode=pl.Buffered(1)),
                  pl.BlockSpec((1, d, tail_blk), lambda i: (0, 0, s1 // tail_blk), pipeline_mode=pl.Buffered(1))],
        out_specs=[pl.BlockSpec((tm, wd), lambda i: (i, 0)) for wd in widths],
        out_shape=[jax.ShapeDtypeStruct((n, wd), f32) for wd in widths],
        compiler_params=pltpu.CompilerParams(dimension_semantics=("parallel",),
                                             vmem_limit_bytes=VMEM_LIMIT_BYTES),
        name="in_proj",
    )(xa, xb, norm1_w.reshape(1, d).astype(f32), w_in3, w_in3)


def _out_proj_body(ma_ref, mb_ref, w_ref, xa_ref, xb_ref, o_ref, *, tiles_a):
    first = pl.program_id(0) < tiles_a
    m = jnp.where(first, ma_ref[...], mb_ref[...])
    x = jnp.where(first, xa_ref[...], xb_ref[...])
    half = w_ref.shape[0] // 2
    o_ref[...] = (x + jnp.dot(m[:, :half], w_ref[:half, :].astype(jnp.bfloat16),
                              preferred_element_type=jnp.float32)
                  + jnp.dot(m[:, half:], w_ref[half:, :].astype(jnp.bfloat16),
                            preferred_element_type=jnp.float32))


def _out_proj(ma, mb, w_out, xa, xb):
    d = xa.shape[1]
    n = xa.shape[0] + xb.shape[0]
    tm = PROJ_ROW_TILE
    return pl.pallas_call(
        functools.partial(_out_proj_body, tiles_a=xa.shape[0] // tm),
        grid=(n // tm,),
        in_specs=[*_two_part_specs(xa.shape[0], xb.shape[0], tm, D_MIX),
                  pl.BlockSpec((D_MIX, d), lambda i: (0, 0), pipeline_mode=pl.Buffered(1)),
                  *_two_part_specs(xa.shape[0], xb.shape[0], tm, d)],
        out_specs=pl.BlockSpec((tm, d), lambda i: (i, 0)),
        out_shape=jax.ShapeDtypeStruct((n, d), jnp.float32),
        compiler_params=pltpu.CompilerParams(dimension_semantics=("parallel",),
                                             vmem_limit_bytes=VMEM_LIMIT_BYTES),
        name="out_proj",
    )(ma, mb, w_out.astype(jnp.float32), xa, xb)


def _mixer_constants():
    bf16 = jnp.bfloat16
    h = np.arange(LANES)[:, None]
    ch = np.arange(D_SSD)[None, :]
    expand = (ch // SSD_HEAD_DIM == h).astype(np.float32)
    i = np.arange(BLK)[:, None]
    j = np.arange(BLK)[None, :]
    causal = (j <= i).astype(np.float32)
    return dict(expand=jnp.asarray(expand, bf16), expand_t=jnp.asarray(expand.T, bf16),
                causal=jnp.asarray(causal, bf16))


def _softplus(x):
    return jnp.maximum(x, 0.0) + jnp.log(1.0 + jnp.exp(-jnp.abs(x)))


def _conv_silu(ext_ref, cw_ref, cb_ref, first_row):
    ext = ext_ref[...]
    last = first_row + CONV_WIDTH - 1
    acc = cb_ref[...] + cw_ref[CONV_WIDTH - 1:CONV_WIDTH, :] * ext[last:last + BLK, :]
    for k in range(CONV_WIDTH - 1):
        tap = pltpu.roll(ext, CONV_WIDTH - 1 - k, axis=0)[last:last + BLK, :]
        acc = acc + cw_ref[k:k + 1, :] * tap
    return acc * jax.nn.sigmoid(acc)


def _ssd_intra(xbc_c, dt_raw, dtb_ref, alog_ref, causal_bf, expand_ref):
    f32 = jnp.float32
    xs = xbc_c[:, :D_SSD]
    bm = xbc_c[:, D_SSD:D_SSD + N_SSD_GROUPS * D_STATE]
    cm = xbc_c[:, D_SSD + N_SSD_GROUPS * D_STATE:]
    dt = _softplus(dt_raw + dtb_ref[...])
    a = dt * (-jnp.exp(alog_ref[...]))
    a_cum = _dot_sel_left(causal_bf, a, 3)
    dt_x = _dot_sel_right(dt, expand_ref[...], 2)
    return xs, bm, cm, dt, a_cum, xs * dt_x


def _ssd_diag_group(g, cb, a_cum, a_cum_t, keep, xdt):
    f32, bf16 = jnp.float32, jnp.bfloat16
    hg = N_SSD_HEADS // N_SSD_GROUPS
    lane = lax.broadcasted_iota(jnp.int32, (BLK, LANES), 1)
    first_head = lane < SSD_HEAD_DIM
    neg = jnp.float32(-jnp.inf)
    outs = []
    for pr in range(hg * SSD_HEAD_DIM // LANES):
        h1 = g * hg + 2 * pr
        blk = (g * hg * SSD_HEAD_DIM) // LANES + pr
        xp = xdt[:, blk * LANES:(blk + 1) * LANES]
        x1 = jnp.where(first_head, xp, 0.0).astype(bf16)
        x2 = jnp.where(first_head, 0.0, xp).astype(bf16)
        m1 = (cb * jnp.exp(jnp.where(keep, a_cum[:, h1:h1 + 1] - a_cum_t[h1:h1 + 1, :], neg))).astype(bf16)
        m2 = (cb * jnp.exp(jnp.where(keep, a_cum[:, h1 + 1:h1 + 2] - a_cum_t[h1 + 1:h1 + 2, :], neg))).astype(bf16)
        outs.append(jnp.dot(m1, x1, preferred_element_type=f32) + jnp.dot(m2, x2, preferred_element_type=f32))
    return jnp.concatenate(outs, axis=1)


def _gated_norm(y, z, nw_ref):
    yg = y * (z * jax.nn.sigmoid(z))
    return yg * lax.rsqrt(jnp.mean(yg * yg, axis=-1, keepdims=True) + EPS) * nw_ref[...]


def _reset_history(ext_ref, pool_tail_ref, state_ref):
    ext_ref[0:CONV_TAIL_ROWS, :] = jnp.zeros((CONV_TAIL_ROWS, D_CONV), jnp.float32)
    pool_tail_ref[...] = jnp.zeros(pool_tail_ref.shape, jnp.float32)
    state_ref[...] = jnp.zeros(state_ref.shape, jnp.float32)


def _prompt_block(c, z_ref, xbc_ref, dt_ref, u_ref, cw_ref, cb_ref, dtb_ref, alog_ref, dskip_ref, nw_ref,
                  pw_ref, ps_ref, causal_ref, mix_ref, ext_ref, pool_tail_ref, state_ref, side_work):
    f32, bf16 = jnp.float32, jnp.bfloat16
    gw = D_SSD // N_SSD_GROUPS

    def do_side_work():
        piece = next(side_work, None)
        if piece is not None:
            piece()

    u = u_ref[...]
    ext_u = jnp.concatenate([pool_tail_ref[...], u], axis=0)
    pos = (c * BLK + lax.broadcasted_iota(jnp.int32, (BLK, 1), 0) + 1).astype(f32)
    for gi, w in enumerate(POOL_WINDOWS):
        assert w & (w - 1) == 0 and w <= HIST_ROWS
        sl = slice(gi * POOL_GROUP_DIM, (gi + 1) * POOL_GROUP_DIM)
        ug = u[:, sl]
        acc = ext_u[:, sl]
        span = 1
        while span < w:
            acc = acc + pltpu.roll(acc, span, axis=0)
            span *= 2
        wsum = acc[HIST_ROWS:, :]
        pooled = wsum / jnp.minimum(pos, jnp.float32(w)) - ug
        po = jnp.dot(pooled.astype(bf16), pw_ref[gi], preferred_element_type=f32) * ps_ref[:, sl]
        mix_ref[:, D_SSD + gi * POOL_GROUP_DIM:D_SSD + (gi + 1) * POOL_GROUP_DIM] = po.astype(bf16)
    pool_tail_ref[...] = u_ref[BLK - HIST_ROWS:BLK, :]
    do_side_work()

    ext_ref[CONV_TAIL_ROWS:CONV_TAIL_ROWS + BLK, :] = xbc_ref[...]
    xbc_c = _conv_silu(ext_ref, cw_ref, cb_ref, CONV_TAIL_ROWS - (CONV_WIDTH - 1))
    ext_ref[0:CONV_TAIL_ROWS, :] = xbc_ref[BLK - CONV_TAIL_ROWS:BLK, :]
    do_side_work()

    causal_bf = causal_ref[...]
    keep = causal_bf > 0
    xs = xbc_c[:, :D_SSD]
    bm = xbc_c[:, D_SSD:D_SSD + N_SSD_GROUPS * D_STATE]
    cm = xbc_c[:, D_SSD + N_SSD_GROUPS * D_STATE:]
    dt = _softplus(dt_ref[...] + dtb_ref[...])
    a_cum = dt * (-jnp.exp(alog_ref[...]))
    row = lax.broadcasted_iota(jnp.int32, (BLK, LANES), 0)
    span = 1
    while span < BLK:
        a_cum = a_cum + jnp.where(row >= span, pltpu.roll(a_cum, span, axis=0), 0.0)
        span *= 2
    a_cum_t = jnp.transpose(a_cum)
    a_tot = a_cum[BLK - 1:BLK, :]
    ea = jnp.exp(a_cum)
    dte = jnp.exp(a_tot - a_cum)
    cd = jnp.exp(a_tot)
    hg = N_SSD_HEADS // N_SSD_GROUPS
    first_head = lax.broadcasted_iota(jnp.int32, (BLK, LANES), 1) < SSD_HEAD_DIM
    neg = jnp.float32(-jnp.inf)

    def head_cols(v, h1):
        return jnp.where(first_head, v[:, h1:h1 + 1], v[:, h1 + 1:h1 + 2])

    def decay_from(h):
        return jnp.exp(jnp.where(keep, a_cum[:, h:h + 1] - a_cum_t[h:h + 1, :], neg))

    y_parts = []
    for g in range(N_SSD_GROUPS):
        do_side_work()
        cg = cm[:, g * D_STATE:(g + 1) * D_STATE].astype(bf16)
        bg = bm[:, g * D_STATE:(g + 1) * D_STATE].astype(bf16)
        cb = lax.dot_general(cg, bg, NT_DIMS, preferred_element_type=f32)
        sg = state_ref[g * gw:(g + 1) * gw, :]
        y_off = lax.dot_general(cg, sg.astype(bf16), NT_DIMS, preferred_element_type=f32)
        xdte_parts = []
        for pr in range(gw // LANES):
            h1 = g * hg + 2 * pr
            sl = slice(h1 * SSD_HEAD_DIM, h1 * SSD_HEAD_DIM + LANES)
            xp = xs[:, sl] * head_cols(dt, h1)
            x1 = jnp.where(first_head, xp, 0.0).astype(bf16)
            x2 = jnp.where(first_head, 0.0, xp).astype(bf16)
            y_diag = (jnp.dot((cb * decay_from(h1)).astype(bf16), x1, preferred_element_type=f32)
                      + jnp.dot((cb * decay_from(h1 + 1)).astype(bf16), x2, preferred_element_type=f32))
            y_parts.append(y_diag + y_off[:, pr * LANES:(pr + 1) * LANES] * head_cols(ea, h1)
                           + xs[:, sl] * dskip_ref[:, sl])
            xdte_parts.append(xp * head_cols(dte, h1))
        xdte_t = jnp.transpose(jnp.concatenate(xdte_parts, axis=1)).astype(bf16)
        cd_rows = jnp.concatenate([jnp.broadcast_to(cd[:, h:h + 1], (SSD_HEAD_DIM, D_STATE))
                                   for h in range(g * hg, (g + 1) * hg)], axis=0)
        state_ref[g * gw:(g + 1) * gw, :] = sg * cd_rows + jnp.dot(xdte_t, bg, preferred_element_type=f32)
    y = jnp.concatenate(y_parts, axis=1)
    mix_ref[:, 0:D_SSD] = _gated_norm(y, z_ref[...], nw_ref).astype(bf16)


W_SLAB_COLS = 512
PROJ_PIECE_COLS = 512


def _prompt_fused_body(x0_ref, xn_ref, n1w_ref, w_hbm, wtail_ref,
                       cw_ref, cb_ref, dtb_ref, alog_ref, dskip_ref, nw_ref, pw_ref, ps_ref, causal_ref,
                       mix_ref, ssm_ref, ctail_ref, ptail_ref,
                       wbf_ref, wtail_bf_ref, wstage_ref, z_s, xbc_s, dt_s, u_s, ext_ref, pool_tail_ref,
                       state_ref, wsem, *, steps_per_seq):
    f32, bf16 = jnp.float32, jnp.bfloat16
    s = pl.program_id(0)
    step_rows = x0_ref.shape[0]
    n_main = wbf_ref.shape[1]
    n_tail = N_SSD_HEADS + D_POOL

    def projection_pieces(x_ref, slot):
        x = x_ref[...]
        h = (x * lax.rsqrt(jnp.mean(x * x, axis=-1, keepdims=True) + EPS) * n1w_ref[...]).astype(bf16)

        def slab(dst, c0, w0):
            def piece():
                dst[slot, :, c0:c0 + PROJ_PIECE_COLS] = jnp.dot(h, wbf_ref[:, w0:w0 + PROJ_PIECE_COLS],
                                                                preferred_element_type=f32)
            return piece

        def tail_piece():
            tail = jnp.dot(h, wtail_bf_ref[:, 0:n_tail], preferred_element_type=f32)
            lane = lax.broadcasted_iota(jnp.int32, (step_rows, LANES), 1)
            dt_s[slot] = jnp.where(lane < N_SSD_HEADS, tail[:, 0:LANES], 0.0)
            u_s[slot] = tail[:, N_SSD_HEADS:n_tail]

        pieces = [slab(z_s, c0, c0) for c0 in range(0, D_SSD, PROJ_PIECE_COLS)]
        pieces += [slab(xbc_s, c0, D_SSD + c0) for c0 in range(0, D_CONV, PROJ_PIECE_COLS)]
        return pieces + [tail_piece]

    def project(x_ref, slot):
        for piece in projection_pieces(x_ref, slot):
            piece()

    @pl.when(s == 0)
    def _():
        for k in range(n_main // W_SLAB_COLS):
            cp = pltpu.make_async_copy(w_hbm.at[0, :, pl.ds(k * W_SLAB_COLS, W_SLAB_COLS)], wstage_ref, wsem)
            cp.start()
            cp.wait()
            wbf_ref[:, k * W_SLAB_COLS:(k + 1) * W_SLAB_COLS] = wstage_ref[...].astype(bf16)
        wtail_bf_ref[...] = wtail_ref[0, :, 0:wtail_bf_ref.shape[1]].astype(bf16)
        project(x0_ref, 0)

    @pl.when(s % steps_per_seq == 0)
    def _():
        _reset_history(ext_ref, pool_tail_ref, state_ref)

    pieces = iter(projection_pieces(xn_ref, (s + 1) % 2))
    slot = s % 2
    c0 = (s % steps_per_seq) * PROMPT_BLKS_PER_STEP
    for ci in range(PROMPT_BLKS_PER_STEP):
        rows = pl.ds(ci * BLK, BLK)
        _prompt_block(c0 + ci,
                      z_s.at[slot, rows], xbc_s.at[slot, rows], dt_s.at[slot, rows], u_s.at[slot, rows],
                      cw_ref, cb_ref, dtb_ref, alog_ref, dskip_ref, nw_ref, pw_ref, ps_ref, causal_ref,
                      mix_ref.at[rows], ext_ref, pool_tail_ref, state_ref, pieces)
    for piece in pieces:
        piece()

    @pl.when(s % steps_per_seq == steps_per_seq - 1)
    def _():
        ssm_ref[0] = state_ref[...].reshape(N_SSD_HEADS, SSD_HEAD_DIM, D_STATE)
        ctail_ref[0] = xbc_s[slot, step_rows - CONV_TAIL_ROWS:step_rows, :]
        ptail_ref[0] = u_s[slot, step_rows - HIST_ROWS:step_rows, :]


def _prompt_fused(xp, n_seq, seq_len, norm1_w, w_in3, conv_w, conv_b, dt_bias, A_log, D_skip, ssd_norm_w,
                  pool_w, pool_scale):
    f32, bf16 = jnp.float32, jnp.bfloat16
    n, d = xp.shape
    step_rows = PROMPT_BLKS_PER_STEP * BLK
    steps_per_seq = seq_len // step_rows
    n_steps = n // step_rows
    s1 = D_SSD + D_CONV
    tail_blk = 1024
    tail_bf_cols = -(-(N_SSD_HEADS + D_POOL) // LANES) * LANES
    assert s1 % tail_blk == 0 and tail_bf_cols <= tail_blk and s1 % W_SLAB_COLS == 0
    k = _mixer_constants()

    def const(shape):
        return pl.BlockSpec(shape, lambda s: (0,) * len(shape))

    def per_seq(shape):
        return pl.BlockSpec(shape, lambda s: (s // steps_per_seq,) + (0,) * (len(shape) - 1))

    pad_h = (0, LANES - N_SSD_HEADS)
    return pl.pallas_call(
        functools.partial(_prompt_fused_body, steps_per_seq=steps_per_seq),
        grid=(n_steps,),
        in_specs=[pl.BlockSpec((step_rows, d), lambda s: (0, 0)),
                  pl.BlockSpec((step_rows, d), lambda s: (jnp.minimum(s + 1, n_steps - 1), 0)),
                  const((1, d)),
                  pl.BlockSpec(memory_space=pl.ANY),
                  pl.BlockSpec((1, d, tail_blk), lambda s: (0, 0, s1 // tail_blk), pipeline_mode=pl.Buffered(1)),
                  const((CONV_WIDTH, D_CONV)), const((1, D_CONV)), const((1, LANES)), const((1, LANES)),
                  const((1, D_SSD)), const((1, D_SSD)),
                  const((N_POOL_GROUPS, POOL_GROUP_DIM, POOL_GROUP_DIM)), const((1, D_POOL)),
                  const((BLK, BLK))],
        out_specs=[pl.BlockSpec((step_rows, D_MIX), lambda s: (s, 0)),
                   per_seq((1, N_SSD_HEADS, SSD_HEAD_DIM, D_STATE)),
                   per_seq((1, CONV_TAIL_ROWS, D_CONV)),
                   per_seq((1, HIST_ROWS, D_POOL))],
        out_shape=[jax.ShapeDtypeStruct((n, D_MIX), bf16),
                   jax.ShapeDtypeStruct((n_seq, N_SSD_HEADS, SSD_HEAD_DIM, D_STATE), f32),
                   jax.ShapeDtypeStruct((n_seq, CONV_TAIL_ROWS, D_CONV), f32),
                   jax.ShapeDtypeStruct((n_seq, HIST_ROWS, D_POOL), f32)],
        scratch_shapes=[pltpu.VMEM((d, s1), bf16),
                        pltpu.VMEM((d, tail_bf_cols), bf16),
                        pltpu.VMEM((d, W_SLAB_COLS), f32),
                        pltpu.VMEM((2, step_rows, D_SSD), f32),
                        pltpu.VMEM((2, step_rows, D_CONV), f32),
                        pltpu.VMEM((2, step_rows, LANES), f32),
                        pltpu.VMEM((2, step_rows, D_POOL), f32),
                        pltpu.VMEM((CONV_TAIL_ROWS + BLK, D_CONV), f32),
                        pltpu.VMEM((HIST_ROWS, D_POOL), f32),
                        pltpu.VMEM((D_SSD, D_STATE), f32),
                        pltpu.SemaphoreType.DMA(())],
        compiler_params=pltpu.CompilerParams(dimension_semantics=("arbitrary",),
                                             vmem_limit_bytes=VMEM_LIMIT_BYTES),
        name="prompt_fused",
    )(xp, xp, norm1_w.reshape(1, d).astype(f32), w_in3, w_in3,
      conv_w.astype(f32), conv_b.reshape(1, D_CONV).astype(f32),
      jnp.pad(dt_bias.astype(f32), pad_h).reshape(1, LANES), jnp.pad(A_log.astype(f32), pad_h).reshape(1, LANES),
      jnp.repeat(D_skip.astype(f32), SSD_HEAD_DIM).reshape(1, D_SSD), ssd_norm_w.reshape(1, D_SSD).astype(f32),
      pool_w.astype(bf16), pool_scale.reshape(1, D_POOL).astype(f32), k["causal"])


SEQ_PER_BLK = BLK // DEC_SEQ
SEQ_PER_STEP = 8


def _sample_constants():
    bf16 = jnp.bfloat16
    r = np.arange(BLK)
    sq, st = r // DEC_SEQ, r % DEC_SEQ
    same = sq[:, None] == sq[None, :]
    causal = same & (st[None, :] <= st[:, None])
    nk = CONV_WIDTH - 1
    shift = np.stack([same & (st[None, :] == st[:, None] + k - nk) for k in range(nk)])
    cs = np.arange(SEQ_PER_BLK * nk)
    stsel = np.stack([(cs[None, :] // nk == sq[:, None]) & (cs[None, :] % nk == st[:, None] + k)
                      for k in range(nk)])
    pcur = np.stack([causal & (st[:, None] - st[None, :] < w) for w in POOL_WINDOWS])
    hs = np.arange(SEQ_PER_BLK * POOL_HIST)
    phist = np.stack([(hs[None, :] // POOL_HIST == sq[:, None])
                      & (st[:, None] + POOL_HIST - hs[None, :] % POOL_HIST < w) for w in POOL_WINDOWS])
    as_bf = lambda a: jnp.asarray(a.astype(np.float32), bf16)
    return dict(same=as_bf(same), causal=as_bf(causal), shift=as_bf(shift), stsel=as_bf(stsel),
                pcur=as_bf(pcur), phist=as_bf(phist))


def _sample_mixer_body(z_ref, xbc_ref, dt_ref, u_ref, cst_ref, pst_ref, ssm_in_ref,
                       cw_ref, cb_ref, dtb_ref, alog_ref, dskip_ref, nw_ref, pw_ref, ps_ref,
                       causal_ref, same_ref, expand_ref, expand_t_ref, shift_ref, stsel_ref, pcur_ref, phist_ref,
                       mix_ref, ssm_out_ref,
                       ydiag_ref, ea_ref, yt_ref, cdh_ref, cdl_ref, xdte_t_ref, bm_ref, cm_ref, *, pos0):
    f32, bf16 = jnp.float32, jnp.bfloat16
    s = pl.program_id(1)
    gw = D_SSD // N_SSD_GROUPS

    @pl.when(s == 0)
    def _():
        xbc = xbc_ref[...]
        cst = cst_ref[...]
        acc = cb_ref[...] + cw_ref[CONV_WIDTH - 1:CONV_WIDTH, :] * xbc
        for k in range(CONV_WIDTH - 1):
            tap = _dot_sel_left(shift_ref[k], xbc, 3) + _dot_sel_left(stsel_ref[k], cst, 3)
            acc = acc + cw_ref[k:k + 1, :] * tap
        xbc_c = acc * jax.nn.sigmoid(acc)

        causal_bf = causal_ref[...]
        keep = causal_bf > 0
        xs, bm, cm, dt, a_cum, xdt = _ssd_intra(xbc_c, dt_ref[...], dtb_ref, alog_ref, causal_bf, expand_ref)
        a_tot = _dot_sel_left(same_ref[...], dt * (-jnp.exp(alog_ref[...])), 3)
        a_cum_t = jnp.transpose(a_cum)
        ea_ref[...] = _dot_sel_right(jnp.exp(a_cum), expand_ref[...], 2)
        dte_x = _dot_sel_right(jnp.exp(a_tot - a_cum), expand_ref[...], 2)
        cd_col = _dot_sel_left(expand_t_ref[...], jnp.exp(jnp.transpose(a_tot)), 2)
        cd_hi, cd_lo = _split2(cd_col)
        cdh_ref[...] = cd_hi
        cdl_ref[...] = cd_lo
        bm_ref[...] = bm.astype(bf16)
        cm_ref[...] = cm.astype(bf16)
        for g in range(N_SSD_GROUPS):
            cg = cm[:, g * D_STATE:(g + 1) * D_STATE].astype(bf16)
            bg = bm[:, g * D_STATE:(g + 1) * D_STATE].astype(bf16)
            cb = lax.dot_general(cg, bg, NT_DIMS, preferred_element_type=f32)
            y_diag = _ssd_diag_group(g, cb, a_cum, a_cum_t, keep, xdt)
            ydiag_ref[:, g * gw:(g + 1) * gw] = y_diag + xs[:, g * gw:(g + 1) * gw] * dskip_ref[:, g * gw:(g + 1) * gw]
            xdte_t_ref[g * gw:(g + 1) * gw, :] = jnp.transpose(
                xdt[:, g * gw:(g + 1) * gw] * dte_x[:, g * gw:(g + 1) * gw]).astype(bf16)
        yt_ref[...] = jnp.zeros(yt_ref.shape, f32)

        u = u_ref[...]
        pst = pst_ref[...]
        step = lax.broadcasted_iota(jnp.int32, (BLK, 1), 0) % DEC_SEQ
        pos = (step + (pos0 + 1)).astype(f32)
        for gi, w in enumerate(POOL_WINDOWS):
            sl = slice(gi * POOL_GROUP_DIM, (gi + 1) * POOL_GROUP_DIM)
            ug = u[:, sl]
            wsum = _dot_sel_left(pcur_ref[gi], ug, 2) + _dot_sel_left(phist_ref[gi], pst[:, sl], 2)
            pooled = wsum / jnp.minimum(pos, jnp.float32(w)) - ug
            po = jnp.dot(pooled.astype(bf16), pw_ref[gi], preferred_element_type=f32) * ps_ref[:, sl]
            mix_ref[:, D_SSD + gi * POOL_GROUP_DIM:D_SSD + (gi + 1) * POOL_GROUP_DIM] = po.astype(bf16)

    hg = N_SSD_HEADS // N_SSD_GROUPS
    row_seq = lax.broadcasted_iota(jnp.int32, (BLK, LANES), 0) // DEC_SEQ
    col_seq = lax.broadcasted_iota(jnp.int32, (gw, BLK), 1) // DEC_SEQ
    row_idx = lax.broadcasted_iota(jnp.int32, (BLK, LANES), 0)
    for q in range(SEQ_PER_STEP):
        sq = s * SEQ_PER_STEP + q
        rows_of_s = row_seq == sq
        cols_of_s = col_seq == sq
        pick_s = jnp.where(row_idx == DEC_SEQ * sq, 1.0, 0.0).astype(bf16)
        state = ssm_in_ref[q].reshape(D_SSD, D_STATE)
        for g in range(N_SSD_GROUPS):
            rs = slice(g * gw, (g + 1) * gw)
            sg = state[rs, :]
            cg = cm_ref[:, g * D_STATE:(g + 1) * D_STATE]
            bg = bm_ref[:, g * D_STATE:(g + 1) * D_STATE]
            yt = lax.dot_general(sg.astype(bf16), cg, NT_DIMS, preferred_element_type=f32)
            yt_ref[rs, :] += jnp.where(cols_of_s, yt, 0.0)
            cd = (jnp.dot(cdh_ref[rs, :], pick_s, preferred_element_type=f32)
                  + jnp.dot(cdl_ref[rs, :], pick_s, preferred_element_type=f32))
            upd = jnp.dot(xdte_t_ref[rs, :], jnp.where(rows_of_s, bg, jnp.zeros_like(bg)),
                          preferred_element_type=f32)
            ssm_out_ref[q, g * hg:(g + 1) * hg] = (sg * cd + upd).reshape(hg, SSD_HEAD_DIM, D_STATE)

    @pl.when(s == pl.num_programs(1) - 1)
    def _():
        y = ydiag_ref[...] + jnp.transpose(yt_ref[...]) * ea_ref[...]
        mix_ref[:, 0:D_SSD] = _gated_norm(y, z_ref[...], nw_ref).astype(bf16)


def _sample_mixer(z, xbc, dt, u, row0, n_seq, state_conv, state_ssm, state_pool, pos0,
                  conv_w, conv_b, dt_bias, A_log, D_skip, ssd_norm_w, pool_w, pool_scale):
    f32, bf16 = jnp.float32, jnp.bfloat16
    n_blk = n_seq // SEQ_PER_BLK
    blk0 = row0 // BLK
    nk = CONV_WIDTH - 1
    k = _mixer_constants()
    ks = _sample_constants()

    def row_blk(width):
        return pl.BlockSpec((BLK, width), lambda j, s: (blk0 + j, 0))

    def const(shape):
        return pl.BlockSpec(shape, lambda j, s: (0,) * len(shape))

    steps = SEQ_PER_BLK // SEQ_PER_STEP
    state_spec = pl.BlockSpec((SEQ_PER_STEP, N_SSD_HEADS, SSD_HEAD_DIM, D_STATE),
                              lambda j, s: (j * steps + s, 0, 0, 0))
    pad_h = (0, LANES - N_SSD_HEADS)
    return pl.pallas_call(
        functools.partial(_sample_mixer_body, pos0=pos0),
        grid=(n_blk, steps),
        in_specs=[row_blk(D_SSD), row_blk(D_CONV), row_blk(LANES), row_blk(D_POOL),
                  pl.BlockSpec((SEQ_PER_BLK * nk, D_CONV), lambda j, s: (j, 0)),
                  pl.BlockSpec((SEQ_PER_BLK * POOL_HIST, D_POOL), lambda j, s: (j, 0)),
                  state_spec,
                  const((CONV_WIDTH, D_CONV)), const((1, D_CONV)), const((1, LANES)), const((1, LANES)),
                  const((1, D_SSD)), const((1, D_SSD)),
                  const((N_POOL_GROUPS, POOL_GROUP_DIM, POOL_GROUP_DIM)), const((1, D_POOL)),
                  const((BLK, BLK)), const((BLK, BLK)), const((LANES, D_SSD)), const((D_SSD, LANES)),
                  const((nk, BLK, BLK)), const((nk, BLK, SEQ_PER_BLK * nk)),
                  const((N_POOL_GROUPS, BLK, BLK)), const((N_POOL_GROUPS, BLK, SEQ_PER_BLK * POOL_HIST))],
        out_specs=[pl.BlockSpec((BLK, D_MIX), lambda j, s: (j, 0)), state_spec],
        out_shape=[jax.ShapeDtypeStruct((n_seq * DEC_SEQ, D_MIX), bf16),
                   jax.ShapeDtypeStruct((n_seq, N_SSD_HEADS, SSD_HEAD_DIM, D_STATE), f32)],
        scratch_shapes=[pltpu.VMEM((BLK, D_SSD), f32), pltpu.VMEM((BLK, D_SSD), f32),
                        pltpu.VMEM((D_SSD, BLK), f32), pltpu.VMEM((D_SSD, BLK), bf16),
                        pltpu.VMEM((D_SSD, BLK), bf16), pltpu.VMEM((D_SSD, BLK), bf16),
                        pltpu.VMEM((BLK, N_SSD_GROUPS * D_STATE), bf16),
                        pltpu.VMEM((BLK, N_SSD_GROUPS * D_STATE), bf16)],
        compiler_params=pltpu.CompilerParams(dimension_semantics=("parallel", "arbitrary"),
                                             vmem_limit_bytes=VMEM_LIMIT_BYTES),
        name="sample_mixer",
    )(z, xbc, dt, u, state_conv.reshape(n_seq * nk, D_CONV), state_pool.reshape(n_seq * POOL_HIST, D_POOL),
      state_ssm, conv_w.astype(f32), conv_b.reshape(1, D_CONV).astype(f32),
      jnp.pad(dt_bias.astype(f32), pad_h).reshape(1, LANES), jnp.pad(A_log.astype(f32), pad_h).reshape(1, LANES),
      jnp.repeat(D_skip.astype(f32), SSD_HEAD_DIM).reshape(1, D_SSD), ssd_norm_w.reshape(1, D_SSD).astype(f32),
      pool_w.astype(bf16), pool_scale.reshape(1, D_POOL).astype(f32),
      ks["causal"], ks["same"], k["expand"], k["expand_t"], ks["shift"], ks["stsel"],
      ks["pcur"], ks["phist"])


def _moe_sizes(n_tokens, tm):
    nt = n_tokens // tm
    lmax = -(-(TOP_K * tm + N_EXPERTS * MOE_SEG_ROWS) // MOE_SEL_ROWS) * MOE_SEL_ROWS
    rows = (TOP_K * n_tokens + nt * N_EXPERTS * MOE_SEG_ROWS + N_EXPERTS * (MOE_PIECE - 1)
            + MOE_ROW_CHUNK)
    n_rows = -(-rows // MOE_PIECE) * MOE_PIECE
    return nt, lmax, n_rows


def _router_body(x_ref, nw_ref, rwh_ref, rwl_ref, rb_ref, h_ref, posg_ref, post_ref, cnt_ref):
    f32, bf16 = jnp.float32, jnp.bfloat16
    tm = x_ref.shape[0]
    x = x_ref[...]
    h = x * lax.rsqrt(jnp.mean(x * x, axis=-1, keepdims=True) + EPS) * nw_ref[...]
    h_hi = h.astype(bf16)
    h_ref[...] = h_hi
    h_lo = (h - h_hi.astype(f32)).astype(bf16)
    wh = rwh_ref[...]
    logits = (jnp.dot(h_hi, wh, preferred_element_type=f32)
              + jnp.dot(h_lo, wh, preferred_element_type=f32)
              + jnp.dot(h_hi, rwl_ref[...], preferred_element_type=f32)) + rb_ref[...]
    lane = lax.broadcasted_iota(jnp.int32, (tm, LANES), 1)
    lanef = lane.astype(f32)
    neg = jnp.float32(-jnp.inf)
    l = jnp.where(lane < N_EXPERTS, logits, neg)
    sels, vals = [], []
    for _ in range(TOP_K):
        m = jnp.max(l, axis=1, keepdims=True)
        idx = jnp.min(jnp.where(l == m, lanef, jnp.float32(LANES)), axis=1, keepdims=True)
        sel = lanef == idx
        l = jnp.where(sel, neg, l)
        sels.append(sel)
        vals.append(m)
    exps = [jnp.exp(v - vals[0]) for v in vals]
    denom = exps[0] + exps[1] + exps[2] + exps[3]
    gates = [e / denom for e in exps]
    chosen = jnp.where(sels[0] | sels[1] | sels[2] | sels[3], 1.0, 0.0).astype(f32)
    row = lax.broadcasted_iota(jnp.int32, (tm, tm), 0)
    col = lax.broadcasted_iota(jnp.int32, (tm, tm), 1)
    lower = jnp.where(col < row, 1.0, 0.0).astype(bf16)
    rank = jnp.dot(lower, chosen.astype(bf16), preferred_element_type=f32)
    cnt = jnp.sum(chosen, axis=0, keepdims=True)
    seg_units = jnp.maximum(jnp.floor((cnt + (MOE_SEG_ROWS - 1)) * (1.0 / MOE_SEG_ROWS)), 1.0)
    r2 = lax.broadcasted_iota(jnp.int32, (LANES, LANES), 0)
    c2 = lax.broadcasted_iota(jnp.int32, (LANES, LANES), 1)
    upper = jnp.where(r2 < c2, 1.0, 0.0).astype(bf16)
    lstart = jnp.dot(jnp.broadcast_to(seg_units, (8, LANES)).astype(bf16), upper,
                     preferred_element_type=f32)[0:1, :] * MOE_SEG_ROWS
    posmat = lstart + rank
    posg = jnp.zeros((tm, LANES), f32)
    for k in range(TOP_K):
        pos_k = jnp.sum(jnp.where(sels[k], posmat, 0.0), axis=1, keepdims=True)
        posg = posg + jnp.where(lane == k, pos_k, 0.0) + jnp.where(lane == TOP_K + k, gates[k], 0.0)
    posg_ref[...] = posg
    post_ref[...] = jnp.transpose(posg)[0:8, :]
    cnt_ref[0] = jnp.broadcast_to(cnt, (8, LANES)).astype(jnp.int32)


def _moe_router(x1, norm2_w, router_w, router_b, tm):
    n, d = x1.shape
    nt = n // tm
    f32, bf16 = jnp.float32, jnp.bfloat16
    rw = jnp.pad(router_w.astype(f32), ((0, 0), (0, LANES - N_EXPERTS)))
    rw_hi = rw.astype(bf16)
    rw_lo = (rw - rw_hi.astype(f32)).astype(bf16)
    rb = jnp.pad(router_b.astype(f32), (0, LANES - N_EXPERTS)).reshape(1, LANES)
    return pl.pallas_call(
        _router_body,
        grid=(nt,),
        in_specs=[pl.BlockSpec((tm, d), lambda i: (i, 0)),
                  pl.BlockSpec((1, d), lambda i: (0, 0)),
                  pl.BlockSpec((d, LANES), lambda i: (0, 0)),
                  pl.BlockSpec((d, LANES), lambda i: (0, 0)),
                  pl.BlockSpec((1, LANES), lambda i: (0, 0))],
        out_specs=[pl.BlockSpec((tm, d), lambda i: (i, 0)),
                   pl.BlockSpec((tm, LANES), lambda i: (i, 0)),
                   pl.BlockSpec((8, tm), lambda i: (0, i)),
                   pl.BlockSpec((1, 8, LANES), lambda i: (i, 0, 0))],
        out_shape=[jax.ShapeDtypeStruct((n, d), bf16),
                   jax.ShapeDtypeStruct((n, LANES), f32),
                   jax.ShapeDtypeStruct((8, n), f32),
                   jax.ShapeDtypeStruct((nt, 8, LANES), jnp.int32)],
        compiler_params=pltpu.CompilerParams(dimension_semantics=("parallel",),
                                             vmem_limit_bytes=VMEM_LIMIT_BYTES),
        name="moe_router",
    )(x1, norm2_w.reshape(1, d).astype(f32), rw_hi, rw_lo, rb)


def _moe_plan(cnt):
    i32 = jnp.int32
    pad = jnp.maximum((cnt + (MOE_SEG_ROWS - 1)) // MOE_SEG_ROWS, 1) * MOE_SEG_ROWS
    lstart = jnp.cumsum(pad, axis=1) - pad
    lp = jnp.sum(pad, axis=1)
    tot = jnp.sum(pad, axis=0)
    reg = (tot + (MOE_PIECE - 1)) // MOE_PIECE * MOE_PIECE
    reg_end = jnp.cumsum(reg)
    estart = reg_end - reg
    seg = estart[None, :] + jnp.cumsum(pad, axis=0) - pad
    return dict(
        lstart=lstart.reshape(-1).astype(i32), seg_units=(pad // MOE_SEG_ROWS).reshape(-1).astype(i32),
        seg=seg.reshape(-1).astype(i32), lp=lp.astype(i32),
        tail_start=(estart + tot).astype(i32), tail_units=((reg - tot) // MOE_SEG_ROWS).astype(i32),
        estart=estart.astype(i32), erows=reg.astype(i32), used=reg_end[-1].reshape(1).astype(i32))


def _for_each_segment_copy(i, lstart_ref, units_ref, seg_ref, local_ref, global_ref, sem, to_global, fn):
    def per_expert(e, carry):
        k = i * N_EXPERTS + e
        n = pl.multiple_of(units_ref[k] * MOE_SEG_ROWS, MOE_SEG_ROWS)
        loc = local_ref.at[pl.ds(pl.multiple_of(lstart_ref[k], MOE_SEG_ROWS), n)]
        glo = global_ref.at[pl.ds(pl.multiple_of(seg_ref[k], MOE_SEG_ROWS), n)]
        fn(pltpu.make_async_copy(loc, glo, sem) if to_global else pltpu.make_async_copy(glo, loc, sem))
        return carry
    lax.fori_loop(0, N_EXPERTS, per_expert, 0)


def _wait_segment_copies(i, lp_ref, local_ref, global_ref, sem):
    n = pl.multiple_of(lp_ref[i], MOE_SEG_ROWS)
    pltpu.make_async_copy(global_ref.at[pl.ds(0, n)], local_ref.at[pl.ds(0, n)], sem).wait()


def _for_each_unused_piece(used_ref, zero_ref, rows_ref, sem, fn):
    def per_piece(j, c):
        go = pl.multiple_of(j * MOE_PIECE, MOE_PIECE)
        fn(pltpu.make_async_copy(zero_ref, rows_ref.at[pl.ds(go, MOE_PIECE)], sem))
        return c
    lax.fori_loop(used_ref[0] // MOE_PIECE, rows_ref.shape[0] // MOE_PIECE, per_piece, 0)


def _dispatch_body(lstart_ref, units_ref, seg_ref, lp_ref, tail_start_ref, tail_units_ref, used_ref,
                   h_ref, post_ref, xs_ref, stage_ref, sel_ref, zero_ref, sems, fill_sem):
    f32, bf16 = jnp.float32, jnp.bfloat16
    i = pl.program_id(0)
    nt = pl.num_programs(0)
    slot = i % 2
    tm = h_ref.shape[0]
    lmax = stage_ref.shape[1]

    def for_each_fill_copy(fn):
        def per_expert(e, carry):
            @pl.when(tail_units_ref[e] > 0)
            def _():
                n = pl.multiple_of(tail_units_ref[e] * MOE_SEG_ROWS, MOE_SEG_ROWS)
                go = pl.multiple_of(tail_start_ref[e], MOE_SEG_ROWS)
                fn(pltpu.make_async_copy(zero_ref.at[pl.ds(0, n)], xs_ref.at[pl.ds(go, n)], fill_sem))
            return carry
        lax.fori_loop(0, N_EXPERTS, per_expert, 0)
        _for_each_unused_piece(used_ref, zero_ref, xs_ref, fill_sem, fn)

    @pl.when(i == 0)
    def _():
        zero_ref[...] = jnp.zeros(zero_ref.shape, bf16)
        for_each_fill_copy(lambda cp: cp.start())
        for_each_fill_copy(lambda cp: cp.wait())

    def segment_copies(tile, slot_, fn):
        _for_each_segment_copy(tile, lstart_ref, units_ref, seg_ref, stage_ref.at[slot_], xs_ref,
                               sems.at[slot_], True, fn)

    @pl.when(i >= 2)
    def _():
        _wait_segment_copies(i - 2, lp_ref, stage_ref.at[slot], xs_ref, sems.at[slot])

    pos = [post_ref[k:k + 1, :] for k in range(TOP_K)]
    r_local = lax.broadcasted_iota(jnp.int32, (MOE_CHUNK, tm), 0).astype(f32).astype(bf16)
    one, zero = jnp.ones((), bf16), jnp.zeros((), bf16)
    for part in range(lmax // MOE_SEL_ROWS):
        for c in range(MOE_SEL_ROWS // MOE_CHUNK):
            r0 = part * MOE_SEL_ROWS + c * MOE_CHUNK
            loc = [jnp.clip(p - r0, -1.0, float(MOE_CHUNK)).astype(bf16) for p in pos]
            hit = (loc[0] == r_local) | (loc[1] == r_local) | (loc[2] == r_local) | (loc[3] == r_local)
            sel_ref[part, c * MOE_CHUNK:(c + 1) * MOE_CHUNK, :] = jnp.where(hit, one, zero)
        stage_ref[slot, part * MOE_SEL_ROWS:(part + 1) * MOE_SEL_ROWS, :] = jnp.dot(
            sel_ref[part], h_ref[...], preferred_element_type=f32).astype(bf16)

    segment_copies(i, slot, lambda cp: cp.start())

    @pl.when(i == nt - 1)
    def _():
        @pl.when(nt >= 2)
        def _():
            _wait_segment_copies(i - 1, lp_ref, stage_ref.at[1 - slot], xs_ref, sems.at[1 - slot])
        _wait_segment_copies(i, lp_ref, stage_ref.at[slot], xs_ref, sems.at[slot])


def _moe_dispatch(h2, post, plan, tm, lmax, n_rows):
    n, d = h2.shape
    nt = n // tm
    grid_spec = pltpu.PrefetchScalarGridSpec(
        num_scalar_prefetch=7,
        grid=(nt,),
        in_specs=[pl.BlockSpec((tm, d), lambda i, *_: (i, 0)),
                  pl.BlockSpec((8, tm), lambda i, *_: (0, i))],
        out_specs=pl.BlockSpec(memory_space=pl.ANY),
        scratch_shapes=[pltpu.VMEM((2, lmax, d), jnp.bfloat16),
                        pltpu.VMEM((lmax // MOE_SEL_ROWS, MOE_SEL_ROWS, tm), jnp.bfloat16),
                        pltpu.VMEM((MOE_PIECE, d), jnp.bfloat16),
                        pltpu.SemaphoreType.DMA((2,)),
                        pltpu.SemaphoreType.DMA(())])
    return pl.pallas_call(
        _dispatch_body,
        grid_spec=grid_spec,
        out_shape=jax.ShapeDtypeStruct((n_rows, d), jnp.bfloat16),
        compiler_params=pltpu.CompilerParams(dimension_semantics=("arbitrary",),
                                             vmem_limit_bytes=VMEM_LIMIT_BYTES),
        name="moe_dispatch",
    )(plan["lstart"], plan["seg_units"], plan["seg"], plan["lp"], plan["tail_start"], plan["tail_units"],
      plan["used"], h2, post)


def _experts_body(first_ref, count_ref, cstart_ref, cvalid_ref, total_ref, used_ref,
                  xs_ref, wgu_ref, bgu_ref, wd_ref, bd_ref, os_ref,
                  wgu_bf, wd_bf, xbuf, obuf, zero_ref, in_sems, out_sems, fill_sem):
    f32, bf16 = jnp.float32, jnp.bfloat16
    e = pl.program_id(0)
    total = total_ref[0]
    half = D_FF // 2

    def in_copy(j):
        src = xs_ref.at[pl.ds(pl.multiple_of(cstart_ref[j], MOE_PIECE), MOE_ROW_CHUNK)]
        return pltpu.make_async_copy(src, xbuf.at[j % 3], in_sems.at[j % 3])

    def out_copy(j):
        n = pl.multiple_of(cvalid_ref[j], MOE_PIECE)
        go = pl.multiple_of(cstart_ref[j], MOE_PIECE)
        return pltpu.make_async_copy(obuf.at[j % 2, pl.ds(0, n)], os_ref.at[pl.ds(go, n)], out_sems.at[j % 2])

    @pl.when(e == 0)
    def _():
        for j0 in range(2):
            @pl.when(j0 < total)
            def _():
                in_copy(j0).start()
        zero_ref[...] = jnp.zeros(zero_ref.shape, bf16)
        _for_each_unused_piece(used_ref, zero_ref, os_ref, fill_sem, lambda cp: cp.start())
        _for_each_unused_piece(used_ref, zero_ref, os_ref, fill_sem, lambda cp: cp.wait())

    @pl.when(count_ref[e] > 0)
    def _():
        wgu_bf[...] = wgu_ref[0].astype(bf16)
        wd_bf[...] = wd_ref[0].astype(bf16)

        def chunk(j, carry):
            in_copy(j).wait()

            @pl.when(j + 2 < total)
            def _():
                in_copy(j + 2).start()

            @pl.when(j >= 2)
            def _():
                out_copy(j - 2).wait()

            def mlp(n_rows):
                x = xbuf[j % 3, 0:n_rows, :]
                out = bd_ref[0]
                for hf in range(2):
                    gate = jnp.dot(x, wgu_bf[:, hf * half:(hf + 1) * half], preferred_element_type=f32)
                    gate = jnp.minimum(gate + bgu_ref[0, :, hf * half:(hf + 1) * half], SWIGLU_LIMIT)
                    up = jnp.dot(x, wgu_bf[:, D_FF + hf * half:D_FF + (hf + 1) * half],
                                 preferred_element_type=f32)
                    up = jnp.clip(up + bgu_ref[0, :, D_FF + hf * half:D_FF + (hf + 1) * half],
                                  -SWIGLU_LIMIT, SWIGLU_LIMIT)
                    act = (up + 1.0) * (gate * jax.nn.sigmoid(SWIGLU_ALPHA * gate))
                    out = out + jnp.dot(act.astype(bf16), wd_bf[hf * half:(hf + 1) * half, :],
                                        preferred_element_type=f32)
                obuf[j % 2, 0:n_rows, :] = out.astype(bf16)

            @pl.when(cvalid_ref[j] > MOE_ROW_CHUNK // 2)
            def _():
                mlp(MOE_ROW_CHUNK)

            @pl.when(cvalid_ref[j] <= MOE_ROW_CHUNK // 2)
            def _():
                mlp(MOE_ROW_CHUNK // 2)
            out_copy(j).start()
            return carry
        lax.fori_loop(first_ref[e], first_ref[e] + count_ref[e], chunk, 0)

    @pl.when(e == pl.num_programs(0) - 1)
    def _():
        @pl.when(total >= 2)
        def _():
            out_copy(total - 2).wait()

        @pl.when(total >= 1)
        def _():
            out_copy(total - 1).wait()


def _expert_chunks(plan, n_rows):
    i32 = jnp.int32
    max_chunks = n_rows // MOE_ROW_CHUNK + N_EXPERTS
    count = (plan["erows"] + (MOE_ROW_CHUNK - 1)) // MOE_ROW_CHUNK
    end = jnp.cumsum(count)
    first = end - count
    j = jnp.arange(max_chunks, dtype=i32)
    mine = ((first[None, :] <= j[:, None]) & (j[:, None] < end[None, :])).astype(i32)
    c = j - jnp.sum(mine * first[None, :], axis=1)
    cstart = jnp.sum(mine * plan["estart"][None, :], axis=1) + jnp.sum(mine, axis=1) * c * MOE_ROW_CHUNK
    cvalid = jnp.sum(mine * jnp.clip(plan["erows"][None, :] - c[:, None] * MOE_ROW_CHUNK, 0, MOE_ROW_CHUNK),
                     axis=1)
    return (first.astype(i32), count.astype(i32), cstart.astype(i32), cvalid.astype(i32),
            end[-1].reshape(1).astype(i32))


def _moe_experts(xs, plan, w_gate_up, b_gate_up, w_down, b_down):
    d = xs.shape[1]
    grid_spec = pltpu.PrefetchScalarGridSpec(
        num_scalar_prefetch=6,
        grid=(N_EXPERTS,),
        in_specs=[pl.BlockSpec(memory_space=pl.ANY),
                  pl.BlockSpec((1, d, 2 * D_FF), lambda e, *_: (e, 0, 0)),
                  pl.BlockSpec((1, 1, 2 * D_FF), lambda e, *_: (e, 0, 0)),
                  pl.BlockSpec((1, D_FF, d), lambda e, *_: (e, 0, 0)),
                  pl.BlockSpec((1, 1, d), lambda e, *_: (e, 0, 0))],
        out_specs=pl.BlockSpec(memory_space=pl.ANY),
        scratch_shapes=[pltpu.VMEM((d, 2 * D_FF), jnp.bfloat16),
                        pltpu.VMEM((D_FF, d), jnp.bfloat16),
                        pltpu.VMEM((3, MOE_ROW_CHUNK, d), jnp.bfloat16),
                        pltpu.VMEM((2, MOE_ROW_CHUNK, d), jnp.bfloat16),
                        pltpu.VMEM((MOE_PIECE, d), jnp.bfloat16),
                        pltpu.SemaphoreType.DMA((3,)),
                        pltpu.SemaphoreType.DMA((2,)),
                        pltpu.SemaphoreType.DMA(())])
    return pl.pallas_call(
        _experts_body,
        grid_spec=grid_spec,
        out_shape=jax.ShapeDtypeStruct(xs.shape, jnp.bfloat16),
        compiler_params=pltpu.CompilerParams(dimension_semantics=("arbitrary",),
                                             vmem_limit_bytes=VMEM_LIMIT_BYTES),
        name="moe_experts",
    )(*_expert_chunks(plan, xs.shape[0]), plan["used"],
      xs, w_gate_up, b_gate_up.reshape(N_EXPERTS, 1, 2 * D_FF), w_down, b_down.reshape(N_EXPERTS, 1, d))


def _combine_body(lstart_ref, units_ref, seg_ref, lp_ref,
                  os_ref, posg_ref, x_ref, fw_ref, yp_ref, ys_ref, stage_ref, w_ref, sems,
                  *, n_prompt_tiles):
    f32, bf16 = jnp.float32, jnp.bfloat16
    i = pl.program_id(0)
    nt = pl.num_programs(0)
    slot = i % 2
    tm = x_ref.shape[0]
    lmax = stage_ref.shape[1]

    def segment_copies(tile, slot_, fn):
        _for_each_segment_copy(tile, lstart_ref, units_ref, seg_ref, stage_ref.at[slot_], os_ref,
                               sems.at[slot_], False, fn)

    @pl.when(i == 0)
    def _():
        stage_ref[...] = jnp.zeros(stage_ref.shape, bf16)
        segment_copies(0, 0, lambda cp: cp.start())

    @pl.when(i + 1 < nt)
    def _():
        segment_copies(i + 1, 1 - slot, lambda cp: cp.start())

    posg = posg_ref[...]
    pos = [posg[:, k:k + 1] for k in range(TOP_K)]
    gate = [posg[:, TOP_K + k:TOP_K + k + 1] for k in range(TOP_K)]
    gate_bf = [g.astype(bf16) for g in gate]
    r_local = lax.broadcasted_iota(jnp.int32, (tm, MOE_CHUNK), 1).astype(f32).astype(bf16)
    _wait_segment_copies(i, lp_ref, stage_ref.at[slot], os_ref, sems.at[slot])
    y = x_ref[...]
    part_cols = lmax // 2
    for part in range(2):
        for c in range(part * part_cols // MOE_CHUNK, (part + 1) * part_cols // MOE_CHUNK):
            w = jnp.zeros((tm, MOE_CHUNK), bf16)
            for k in range(TOP_K):
                loc = jnp.clip(pos[k] - c * MOE_CHUNK, -1.0, float(MOE_CHUNK)).astype(bf16)
                w = jnp.where(loc == r_local, gate_bf[k], w)
            w_ref[:, c * MOE_CHUNK:(c + 1) * MOE_CHUNK] = w
        y = y + jnp.dot(w_ref[:, part * part_cols:(part + 1) * part_cols],
                        stage_ref[slot, part * part_cols:(part + 1) * part_cols, :],
                        preferred_element_type=f32)
    out = y * lax.rsqrt(jnp.mean(y * y, axis=-1, keepdims=True) + EPS) * fw_ref[...]

    @pl.when(i < n_prompt_tiles)
    def _():
        yp_ref[...] = out

    @pl.when(i >= n_prompt_tiles)
    def _():
        ys_ref[...] = out


def _moe_combine(os_, posg, x1, final_norm_w, plan, tm, lmax, n_prompt):
    n, d = x1.shape
    nt = n // tm
    n_prompt_tiles = n_prompt // tm
    n_sample_tiles = nt - n_prompt_tiles
    grid_spec = pltpu.PrefetchScalarGridSpec(
        num_scalar_prefetch=4,
        grid=(nt,),
        in_specs=[pl.BlockSpec(memory_space=pl.ANY),
                  pl.BlockSpec((tm, LANES), lambda i, *_: (i, 0)),
                  pl.BlockSpec((tm, d), lambda i, *_: (i, 0)),
                  pl.BlockSpec((1, d), lambda i, *_: (0, 0))],
        out_specs=[pl.BlockSpec((tm, d), lambda i, *_: (jnp.minimum(i, n_prompt_tiles - 1), 0)),
                   pl.BlockSpec((tm, d), lambda i, *_: (jnp.maximum(i - n_prompt_tiles, 0), 0))],
        scratch_shapes=[pltpu.VMEM((2, lmax, d), jnp.bfloat16),
                        pltpu.VMEM((tm, lmax), jnp.bfloat16),
                        pltpu.SemaphoreType.DMA((2,))])
    return pl.pallas_call(
        functools.partial(_combine_body, n_prompt_tiles=n_prompt_tiles),
        grid_spec=grid_spec,
        out_shape=[jax.ShapeDtypeStruct((n_prompt, d), jnp.float32),
                   jax.ShapeDtypeStruct((n_sample_tiles * tm, d), jnp.float32)],
        compiler_params=pltpu.CompilerParams(dimension_semantics=("arbitrary",),
                                             vmem_limit_bytes=VMEM_LIMIT_BYTES),
        name="moe_combine",
    )(plan["lstart"], plan["seg_units"], plan["seg"], plan["lp"],
      os_, posg, x1, final_norm_w.reshape(1, d).astype(jnp.float32))


def _moe_block(x1, n_prompt, norm2_w, router_w, router_b, w_gate_up, b_gate_up, w_down, b_down,
               final_norm_w, tm=MOE_TOKEN_TILE):
    n = x1.shape[0]
    nt, lmax, n_rows = _moe_sizes(n, tm)
    h2, posg, post, cnt3 = _moe_router(x1, norm2_w, router_w, router_b, tm)
    plan = _moe_plan(cnt3[:, 0, :N_EXPERTS])
    xs = _moe_dispatch(h2, post, plan, tm, lmax, n_rows)
    os_ = _moe_experts(xs, plan, w_gate_up, b_gate_up, w_down, b_down)
    return _moe_combine(os_, posg, x1, final_norm_w, plan, tm, lmax, n_prompt)


def kernel(x_prompt, x_sample, state_ssm, state_conv, state_pool, norm1_w, w_in, conv_w, conv_b, dt_bias,
           A_log, D_skip, ssd_norm_w, pool_w, pool_scale, w_out, norm2_w, router_w, router_b, w_gate_up,
           b_gate_up, w_down, b_down, final_norm_w):
    n_prompt = BATCH * SEQ
    n_sample = DEC_BATCH * DEC_SEQ
    xp = x_prompt.reshape(n_prompt, D_MODEL)
    xs = x_sample.reshape(n_sample, D_MODEL)
    mp = (conv_w[0], conv_b[0], dt_bias[0], A_log[0], D_skip[0], ssd_norm_w[0], pool_w[0], pool_scale[0])
    mix_p, s1, ctail, ptail = _prompt_fused(xp, BATCH, SEQ, norm1_w[0], w_in, *mp)
    half = n_sample // 2
    z, xbc, dt_raw, u = _in_proj(xs[:half], xs[half:], norm1_w[0], w_in, tm=half)
    mix_s, s2 = _sample_mixer(z, xbc, dt_raw, u, 0, DEC_BATCH, state_conv, state_ssm[0], state_pool,
                              PAST_LEN, *mp)
    nk = CONV_WIDTH - 1
    c1 = ctail[:, CONV_TAIL_ROWS - nk:]
    p1 = ptail[:, HIST_ROWS - POOL_HIST:]
    c2 = xbc.reshape(DEC_BATCH, DEC_SEQ, D_CONV)[:, DEC_SEQ - nk:]
    p2 = jnp.concatenate([state_pool[0][:, DEC_SEQ:], u.reshape(DEC_BATCH, DEC_SEQ, D_POOL)], axis=1)
    x1 = _out_proj(mix_p, mix_s, w_out[0], xp, xs)
    yp, ys = _moe_block(x1, n_prompt, norm2_w[0], router_w[0], router_b[0], w_gate_up[0], b_gate_up[0],
                        w_down[0], b_down[0], final_norm_w)
    return (yp.reshape(x_prompt.shape), ys.reshape(x_sample.shape),
            s1[None], c1[None], p1[None], s2[None], c2[None], p2[None])
```

```python
import functools
import jax, jax.numpy as jnp
from jax import lax
import numpy as np
from jax.experimental import pallas as pl
from jax.experimental.pallas import tpu as pltpu

D_MODEL = 1024
BATCH = 8
SEQ = 2048
DEC_BATCH = 128
DEC_SEQ = 4
PAST_LEN = 16384

D_MIX = 2 * D_MODEL
D_SSD = 3 * D_MIX // 4
SSD_HEAD_DIM = 64
N_SSD_HEADS = D_SSD // SSD_HEAD_DIM
N_SSD_GROUPS = 4
D_STATE = 128
CONV_WIDTH = 4
SSD_CHUNK = 128
D_CONV = D_SSD + 2 * N_SSD_GROUPS * D_STATE
D_POOL = D_MIX - D_SSD
POOL_WINDOWS = (2, 4, 8, 16)
N_POOL_GROUPS = len(POOL_WINDOWS)
POOL_GROUP_DIM = D_POOL // N_POOL_GROUPS
POOL_HIST = max(POOL_WINDOWS) - 1
D_IN_PROJ = D_SSD + D_CONV + N_SSD_HEADS + D_POOL
N_EXPERTS = 32
TOP_K = 4
D_FF = D_MODEL
SWIGLU_LIMIT = 7.0
SWIGLU_ALPHA = 1.702
EPS = 1e-5

LANES = 128
BF16_SUBLANES = 16
VMEM_LIMIT_BYTES = 48 * 1024 * 1024

MOE_TOKEN_TILE = 512
MOE_SEG_ROWS = BF16_SUBLANES
MOE_PIECE = 128
MOE_ROW_CHUNK = 512
MOE_CHUNK = 256
MOE_SEL_ROWS = 512


BLK = SSD_CHUNK
PROJ_ROW_TILE = 512
PROMPT_BLKS_PER_STEP = 2
HIST_ROWS = 16
CONV_TAIL_ROWS = 8
NT_DIMS = (((1,), (1,)), ((), ()))


def _split2(v):
    hi = v.astype(jnp.bfloat16)
    lo = (v - hi.astype(jnp.float32)).astype(jnp.bfloat16)
    return hi, lo


def _dot_sel_left(sel, v, passes):
    out = None
    rem = v
    for p in range(passes):
        part = rem.astype(jnp.bfloat16)
        d = jnp.dot(sel, part, preferred_element_type=jnp.float32)
        out = d if out is None else out + d
        if p + 1 < passes:
            rem = rem - part.astype(jnp.float32)
    return out


def _dot_sel_right(v, sel, passes):
    out = None
    rem = v
    for p in range(passes):
        part = rem.astype(jnp.bfloat16)
        d = jnp.dot(part, sel, preferred_element_type=jnp.float32)
        out = d if out is None else out + d
        if p + 1 < passes:
            rem = rem - part.astype(jnp.float32)
    return out


def _two_part_specs(n_first, n_second, tm, width):
    t1 = n_first // tm
    t2 = n_second // tm
    return (pl.BlockSpec((tm, width), lambda i: (jnp.minimum(i, t1 - 1), 0)),
            pl.BlockSpec((tm, width), lambda i: (jnp.clip(i - t1, 0, t2 - 1), 0)))


IN_PROJ_COLS = 512


def _in_proj_body(xa_ref, xb_ref, nw_ref, wmain_ref, wtail_ref, z_ref, xbc_ref, dt_ref, u_ref, *, tiles_a):
    f32, bf16 = jnp.float32, jnp.bfloat16
    x = jnp.where(pl.program_id(0) < tiles_a, xa_ref[...], xb_ref[...])
    h = (x * lax.rsqrt(jnp.mean(x * x, axis=-1, keepdims=True) + EPS) * nw_ref[...]).astype(bf16)
    off = 0
    for ref in (z_ref, xbc_ref):
        for c0 in range(0, ref.shape[1], IN_PROJ_COLS):
            w = wmain_ref[0, :, off + c0:off + c0 + IN_PROJ_COLS].astype(bf16)
            ref[:, c0:c0 + IN_PROJ_COLS] = jnp.dot(h, w, preferred_element_type=f32)
        off += ref.shape[1]
    tail = jnp.dot(h, wtail_ref[0, :, 0:N_SSD_HEADS + D_POOL].astype(bf16), preferred_element_type=f32)
    lane = lax.broadcasted_iota(jnp.int32, (tail.shape[0], LANES), 1)
    dt_ref[...] = jnp.where(lane < N_SSD_HEADS, tail[:, 0:LANES], 0.0)
    u_ref[...] = tail[:, N_SSD_HEADS:N_SSD_HEADS + D_POOL]


def _in_proj(xa, xb, norm1_w, w_in3, tm=PROJ_ROW_TILE):
    d = xa.shape[1]
    n = xa.shape[0] + xb.shape[0]
    f32, bf16 = jnp.float32, jnp.bfloat16
    s1 = D_SSD + D_CONV
    tail_blk = 1024
    assert s1 % tail_blk == 0 and N_SSD_HEADS + D_POOL <= tail_blk
    widths = (D_SSD, D_CONV, LANES, D_POOL)
    return pl.pallas_call(
        functools.partial(_in_proj_body, tiles_a=xa.shape[0] // tm),
        grid=(n // tm,),
        in_specs=[*_two_part_specs(xa.shape[0], xb.shape[0], tm, d),
                  pl.BlockSpec((1, d), lambda i: (0, 0)),
                  pl.BlockSpec((1, d, s1), lambda i: (0, 0, 0), pipeline_mode=pl.Buffered(1)),
                  pl.BlockSpec((1, d, tail_blk), lambda i: (0, 0, s1 // tail_blk), pipeline_mode=pl.Buffered(1))],
        out_specs=[pl.BlockSpec((tm, wd), lambda i: (i, 0)) for wd in widths],
        out_shape=[jax.ShapeDtypeStruct((n, wd), f32) for wd in widths],
        compiler_params=pltpu.CompilerParams(dimension_semantics=("parallel",),
                                             vmem_limit_bytes=VMEM_LIMIT_BYTES),
        name="in_proj",
    )(xa, xb, norm1_w.reshape(1, d).astype(f32), w_in3, w_in3)


def _mixer_constants():
    bf16 = jnp.bfloat16
    h = np.arange(LANES)[:, None]
    ch = np.arange(D_SSD)[None, :]
    expand = (ch // SSD_HEAD_DIM == h).astype(np.float32)
    i = np.arange(BLK)[:, None]
    j = np.arange(BLK)[None, :]
    causal = (j <= i).astype(np.float32)
    return dict(expand=jnp.asarray(expand, bf16), expand_t=jnp.asarray(expand.T, bf16),
                causal=jnp.asarray(causal, bf16))


def _softplus(x):
    return jnp.maximum(x, 0.0) + jnp.log(1.0 + jnp.exp(-jnp.abs(x)))


def _conv_silu(ext_ref, cw_ref, cb_ref, first_row):
    ext = ext_ref[...]
    last = first_row + CONV_WIDTH - 1
    acc = cb_ref[...] + cw_ref[CONV_WIDTH - 1:CONV_WIDTH, :] * ext[last:last + BLK, :]
    for k in range(CONV_WIDTH - 1):
        tap = pltpu.roll(ext, CONV_WIDTH - 1 - k, axis=0)[last:last + BLK, :]
        acc = acc + cw_ref[k:k + 1, :] * tap
    return acc * jax.nn.sigmoid(acc)


def _ssd_intra(xbc_c, dt_raw, dtb_ref, alog_ref, causal_bf, expand_ref):
    f32 = jnp.float32
    xs = xbc_c[:, :D_SSD]
    bm = xbc_c[:, D_SSD:D_SSD + N_SSD_GROUPS * D_STATE]
    cm = xbc_c[:, D_SSD + N_SSD_GROUPS * D_STATE:]
    dt = _softplus(dt_raw + dtb_ref[...])
    a = dt * (-jnp.exp(alog_ref[...]))
    a_cum = _dot_sel_left(causal_bf, a, 3)
    dt_x = _dot_sel_right(dt, expand_ref[...], 2)
    return xs, bm, cm, dt, a_cum, xs * dt_x


def _ssd_diag_group(g, cb, a_cum, a_cum_t, keep, xdt):
    f32, bf16 = jnp.float32, jnp.bfloat16
    hg = N_SSD_HEADS // N_SSD_GROUPS
    lane = lax.broadcasted_iota(jnp.int32, (BLK, LANES), 1)
    first_head = lane < SSD_HEAD_DIM
    neg = jnp.float32(-jnp.inf)
    outs = []
    for pr in range(hg * SSD_HEAD_DIM // LANES):
        h1 = g * hg + 2 * pr
        blk = (g * hg * SSD_HEAD_DIM) // LANES + pr
        xp = xdt[:, blk * LANES:(blk + 1) * LANES]
        x1 = jnp.where(first_head, xp, 0.0).astype(bf16)
        x2 = jnp.where(first_head, 0.0, xp).astype(bf16)
        m1 = (cb * jnp.exp(jnp.where(keep, a_cum[:, h1:h1 + 1] - a_cum_t[h1:h1 + 1, :], neg))).astype(bf16)
        m2 = (cb * jnp.exp(jnp.where(keep, a_cum[:, h1 + 1:h1 + 2] - a_cum_t[h1 + 1:h1 + 2, :], neg))).astype(bf16)
        outs.append(jnp.dot(m1, x1, preferred_element_type=f32) + jnp.dot(m2, x2, preferred_element_type=f32))
    return jnp.concatenate(outs, axis=1)


def _gated_norm(y, z, nw_ref):
    yg = y * (z * jax.nn.sigmoid(z))
    return yg * lax.rsqrt(jnp.mean(yg * yg, axis=-1, keepdims=True) + EPS) * nw_ref[...]


def _reset_history(ext_ref, pool_tail_ref, state_ref):
    ext_ref[0:CONV_TAIL_ROWS, :] = jnp.zeros((CONV_TAIL_ROWS, D_CONV), jnp.float32)
    pool_tail_ref[...] = jnp.zeros(pool_tail_ref.shape, jnp.float32)
    state_ref[...] = jnp.zeros(state_ref.shape, jnp.float32)


def _prompt_block(c, z_ref, xbc_ref, dt_ref, u_ref, cw_ref, cb_ref, dtb_ref, alog_ref, dskip_ref, nw_ref,
                  pw_ref, ps_ref, causal_ref, mix_ref, ext_ref, pool_tail_ref, state_ref, side_work):
    f32, bf16 = jnp.float32, jnp.bfloat16
    gw = D_SSD // N_SSD_GROUPS

    def do_side_work():
        piece = next(side_work, None)
        if piece is not None:
            piece()

    u = u_ref[...]
    ext_u = jnp.concatenate([pool_tail_ref[...], u], axis=0)
    pos = (c * BLK + lax.broadcasted_iota(jnp.int32, (BLK, 1), 0) + 1).astype(f32)
    for gi, w in enumerate(POOL_WINDOWS):
        assert w & (w - 1) == 0 and w <= HIST_ROWS
        sl = slice(gi * POOL_GROUP_DIM, (gi + 1) * POOL_GROUP_DIM)
        ug = u[:, sl]
        acc = ext_u[:, sl]
        span = 1
        while span < w:
            acc = acc + pltpu.roll(acc, span, axis=0)
            span *= 2
        wsum = acc[HIST_ROWS:, :]
        pooled = wsum / jnp.minimum(pos, jnp.float32(w)) - ug
        po = jnp.dot(pooled.astype(bf16), pw_ref[gi], preferred_element_type=f32) * ps_ref[:, sl]
        mix_ref[:, D_SSD + gi * POOL_GROUP_DIM:D_SSD + (gi + 1) * POOL_GROUP_DIM] = po.astype(bf16)
    pool_tail_ref[...] = u_ref[BLK - HIST_ROWS:BLK, :]
    do_side_work()

    ext_ref[CONV_TAIL_ROWS:CONV_TAIL_ROWS + BLK, :] = xbc_ref[...]
    xbc_c = _conv_silu(ext_ref, cw_ref, cb_ref, CONV_TAIL_ROWS - (CONV_WIDTH - 1))
    ext_ref[0:CONV_TAIL_ROWS, :] = xbc_ref[BLK - CONV_TAIL_ROWS:BLK, :]
    do_side_work()

    causal_bf = causal_ref[...]
    keep = causal_bf > 0
    xs = xbc_c[:, :D_SSD]
    bm = xbc_c[:, D_SSD:D_SSD + N_SSD_GROUPS * D_STATE]
    cm = xbc_c[:, D_SSD + N_SSD_GROUPS * D_STATE:]
    dt = _softplus(dt_ref[...] + dtb_ref[...])
    a_cum = dt * (-jnp.exp(alog_ref[...]))
    row = lax.broadcasted_iota(jnp.int32, (BLK, LANES), 0)
    span = 1
    while span < BLK:
        a_cum = a_cum + jnp.where(row >= span, pltpu.roll(a_cum, span, axis=0), 0.0)
        span *= 2
    a_cum_t = jnp.transpose(a_cum)
    a_tot = a_cum[BLK - 1:BLK, :]
    ea = jnp.exp(a_cum)
    dte = jnp.exp(a_tot - a_cum)
    cd = jnp.exp(a_tot)
    hg = N_SSD_HEADS // N_SSD_GROUPS
    first_head = lax.broadcasted_iota(jnp.int32, (BLK, LANES), 1) < SSD_HEAD_DIM
    neg = jnp.float32(-jnp.inf)

    def head_cols(v, h1):
        return jnp.where(first_head, v[:, h1:h1 + 1], v[:, h1 + 1:h1 + 2])

    def decay_from(h):
        return jnp.exp(jnp.where(keep, a_cum[:, h:h + 1] - a_cum_t[h:h + 1, :], neg))

    y_parts = []
    for g in range(N_SSD_GROUPS):
        do_side_work()
        cg = cm[:, g * D_STATE:(g + 1) * D_STATE].astype(bf16)
        bg = bm[:, g * D_STATE:(g + 1) * D_STATE].astype(bf16)
        cb = lax.dot_general(cg, bg, NT_DIMS, preferred_element_type=f32)
        sg = state_ref[g * gw:(g + 1) * gw, :]
        y_off = lax.dot_general(cg, sg.astype(bf16), NT_DIMS, preferred_element_type=f32)
        xdte_parts = []
        for pr in range(gw // LANES):
            h1 = g * hg + 2 * pr
            sl = slice(h1 * SSD_HEAD_DIM, h1 * SSD_HEAD_DIM + LANES)
            xp = xs[:, sl] * head_cols(dt, h1)
            x1 = jnp.where(first_head, xp, 0.0).astype(bf16)
            x2 = jnp.where(first_head, 0.0, xp).astype(bf16)
            y_diag = (jnp.dot((cb * decay_from(h1)).astype(bf16), x1, preferred_element_type=f32)
                      + jnp.dot((cb * decay_from(h1 + 1)).astype(bf16), x2, preferred_element_type=f32))
            y_parts.append(y_diag + y_off[:, pr * LANES:(pr + 1) * LANES] * head_cols(ea, h1)
                           + xs[:, sl] * dskip_ref[:, sl])
            xdte_parts.append(xp * head_cols(dte, h1))
        xdte_t = jnp.transpose(jnp.concatenate(xdte_parts, axis=1)).astype(bf16)
        cd_rows = jnp.concatenate([jnp.broadcast_to(cd[:, h:h + 1], (SSD_HEAD_DIM, D_STATE))
                                   for h in range(g * hg, (g + 1) * hg)], axis=0)
        state_ref[g * gw:(g + 1) * gw, :] = sg * cd_rows + jnp.dot(xdte_t, bg, preferred_element_type=f32)
    y = jnp.concatenate(y_parts, axis=1)
    mix_ref[:, 0:D_SSD] = _gated_norm(y, z_ref[...], nw_ref).astype(bf16)


W_SLAB_COLS = 512
PROJ_PIECE_COLS = 512


def _prompt_fused_body(x0_ref, xn_ref, n1w_ref, w_hbm, wtail_ref,
                       cw_ref, cb_ref, dtb_ref, alog_ref, dskip_ref, nw_ref, pw_ref, ps_ref, causal_ref,
                       mix_ref, ssm_ref, ctail_ref, ptail_ref,
                       wbf_ref, wtail_bf_ref, wstage_ref, z_s, xbc_s, dt_s, u_s, ext_ref, pool_tail_ref,
                       state_ref, wsem, *, steps_per_seq):
    f32, bf16 = jnp.float32, jnp.bfloat16
    s = pl.program_id(0)
    step_rows = x0_ref.shape[0]
    n_main = wbf_ref.shape[1]
    n_tail = N_SSD_HEADS + D_POOL

    def projection_pieces(x_ref, slot):
        x = x_ref[...]
        h = (x * lax.rsqrt(jnp.mean(x * x, axis=-1, keepdims=True) + EPS) * n1w_ref[...]).astype(bf16)

        def slab(dst, c0, w0):
            def piece():
                dst[slot, :, c0:c0 + PROJ_PIECE_COLS] = jnp.dot(h, wbf_ref[:, w0:w0 + PROJ_PIECE_COLS],
                                                                preferred_element_type=f32)
            return piece

        def tail_piece():
            tail = jnp.dot(h, wtail_bf_ref[:, 0:n_tail], preferred_element_type=f32)
            lane = lax.broadcasted_iota(jnp.int32, (step_rows, LANES), 1)
            dt_s[slot] = jnp.where(lane < N_SSD_HEADS, tail[:, 0:LANES], 0.0)
            u_s[slot] = tail[:, N_SSD_HEADS:n_tail]

        pieces = [slab(z_s, c0, c0) for c0 in range(0, D_SSD, PROJ_PIECE_COLS)]
        pieces += [slab(xbc_s, c0, D_SSD + c0) for c0 in range(0, D_CONV, PROJ_PIECE_COLS)]
        return pieces + [tail_piece]

    def project(x_ref, slot):
        for piece in projection_pieces(x_ref, slot):
            piece()

    @pl.when(s == 0)
    def _():
        for k in range(n_main // W_SLAB_COLS):
            cp = pltpu.make_async_copy(w_hbm.at[0, :, pl.ds(k * W_SLAB_COLS, W_SLAB_COLS)], wstage_ref, wsem)
            cp.start()
            cp.wait()
            wbf_ref[:, k * W_SLAB_COLS:(k + 1) * W_SLAB_COLS] = wstage_ref[...].astype(bf16)
        wtail_bf_ref[...] = wtail_ref[0, :, 0:wtail_bf_ref.shape[1]].astype(bf16)
        project(x0_ref, 0)

    @pl.when(s % steps_per_seq == 0)
    def _():
        _reset_history(ext_ref, pool_tail_ref, state_ref)

    pieces = iter(projection_pieces(xn_ref, (s + 1) % 2))
    slot = s % 2
    c0 = (s % steps_per_seq) * PROMPT_BLKS_PER_STEP
    for ci in range(PROMPT_BLKS_PER_STEP):
        rows = pl.ds(ci * BLK, BLK)
        _prompt_block(c0 + ci,
                      z_s.at[slot, rows], xbc_s.at[slot, rows], dt_s.at[slot, rows], u_s.at[slot, rows],
                      cw_ref, cb_ref, dtb_ref, alog_ref, dskip_ref, nw_ref, pw_ref, ps_ref, causal_ref,
                      mix_ref.at[rows], ext_ref, pool_tail_ref, state_ref, pieces)
    for piece in pieces:
        piece()

    @pl.when(s % steps_per_seq == steps_per_seq - 1)
    def _():
        ssm_ref[0] = state_ref[...].reshape(N_SSD_HEADS, SSD_HEAD_DIM, D_STATE)
        ctail_ref[0] = xbc_s[slot, step_rows - CONV_TAIL_ROWS:step_rows, :]
        ptail_ref[0] = u_s[slot, step_rows - HIST_ROWS:step_rows, :]


def _prompt_fused(xp, n_seq, seq_len, norm1_w, w_in3, conv_w, conv_b, dt_bias, A_log, D_skip, ssd_norm_w,
                  pool_w, pool_scale):
    f32, bf16 = jnp.float32, jnp.bfloat16
    n, d = xp.shape
    step_rows = PROMPT_BLKS_PER_STEP * BLK
    steps_per_seq = seq_len // step_rows
    n_steps = n // step_rows
    s1 = D_SSD + D_CONV
    tail_blk = 1024
    tail_bf_cols = -(-(N_SSD_HEADS + D_POOL) // LANES) * LANES
    assert s1 % tail_blk == 0 and tail_bf_cols <= tail_blk and s1 % W_SLAB_COLS == 0
    k = _mixer_constants()

    def const(shape):
        return pl.BlockSpec(shape, lambda s: (0,) * len(shape))

    def per_seq(shape):
        return pl.BlockSpec(shape, lambda s: (s // steps_per_seq,) + (0,) * (len(shape) - 1))

    pad_h = (0, LANES - N_SSD_HEADS)
    return pl.pallas_call(
        functools.partial(_prompt_fused_body, steps_per_seq=steps_per_seq),
        grid=(n_steps,),
        in_specs=[pl.BlockSpec((step_rows, d), lambda s: (0, 0)),
                  pl.BlockSpec((step_rows, d), lambda s: (jnp.minimum(s + 1, n_steps - 1), 0)),
                  const((1, d)),
                  pl.BlockSpec(memory_space=pl.ANY),
                  pl.BlockSpec((1, d, tail_blk), lambda s: (0, 0, s1 // tail_blk), pipeline_mode=pl.Buffered(1)),
                  const((CONV_WIDTH, D_CONV)), const((1, D_CONV)), const((1, LANES)), const((1, LANES)),
                  const((1, D_SSD)), const((1, D_SSD)),
                  const((N_POOL_GROUPS, POOL_GROUP_DIM, POOL_GROUP_DIM)), const((1, D_POOL)),
                  const((BLK, BLK))],
        out_specs=[pl.BlockSpec((step_rows, D_MIX), lambda s: (s, 0)),
                   per_seq((1, N_SSD_HEADS, SSD_HEAD_DIM, D_STATE)),
                   per_seq((1, CONV_TAIL_ROWS, D_CONV)),
                   per_seq((1, HIST_ROWS, D_POOL))],
        out_shape=[jax.ShapeDtypeStruct((n, D_MIX), bf16),
                   jax.ShapeDtypeStruct((n_seq, N_SSD_HEADS, SSD_HEAD_DIM, D_STATE), f32),
                   jax.ShapeDtypeStruct((n_seq, CONV_TAIL_ROWS, D_CONV), f32),
                   jax.ShapeDtypeStruct((n_seq, HIST_ROWS, D_POOL), f32)],
        scratch_shapes=[pltpu.VMEM((d, s1), bf16),
                        pltpu.VMEM((d, tail_bf_cols), bf16),
                        pltpu.VMEM((d, W_SLAB_COLS), f32),
                        pltpu.VMEM((2, step_rows, D_SSD), f32),
                        pltpu.VMEM((2, step_rows, D_CONV), f32),
                        pltpu.VMEM((2, step_rows, LANES), f32),
                        pltpu.VMEM((2, step_rows, D_POOL), f32),
                        pltpu.VMEM((CONV_TAIL_ROWS + BLK, D_CONV), f32),
                        pltpu.VMEM((HIST_ROWS, D_POOL), f32),
                        pltpu.VMEM((D_SSD, D_STATE), f32),
                        pltpu.SemaphoreType.DMA(())],
        compiler_params=pltpu.CompilerParams(dimension_semantics=("arbitrary",),
                                             vmem_limit_bytes=VMEM_LIMIT_BYTES),
        name="prompt_fused",
    )(xp, xp, norm1_w.reshape(1, d).astype(f32), w_in3, w_in3,
      conv_w.astype(f32), conv_b.reshape(1, D_CONV).astype(f32),
      jnp.pad(dt_bias.astype(f32), pad_h).reshape(1, LANES), jnp.pad(A_log.astype(f32), pad_h).reshape(1, LANES),
      jnp.repeat(D_skip.astype(f32), SSD_HEAD_DIM).reshape(1, D_SSD), ssd_norm_w.reshape(1, D_SSD).astype(f32),
      pool_w.astype(bf16), pool_scale.reshape(1, D_POOL).astype(f32), k["causal"])


SEQ_PER_BLK = BLK // DEC_SEQ
SEQ_PER_STEP = 8


def _sample_constants():
    bf16 = jnp.bfloat16
    r = np.arange(BLK)
    sq, st = r // DEC_SEQ, r % DEC_SEQ
    same = sq[:, None] == sq[None, :]
    causal = same & (st[None, :] <= st[:, None])
    nk = CONV_WIDTH - 1
    shift = np.stack([same & (st[None, :] == st[:, None] + k - nk) for k in range(nk)])
    cs = np.arange(SEQ_PER_BLK * nk)
    stsel = np.stack([(cs[None, :] // nk == sq[:, None]) & (cs[None, :] % nk == st[:, None] + k)
                      for k in range(nk)])
    pcur = np.stack([causal & (st[:, None] - st[None, :] < w) for w in POOL_WINDOWS])
    hs = np.arange(SEQ_PER_BLK * POOL_HIST)
    phist = np.stack([(hs[None, :] // POOL_HIST == sq[:, None])
                      & (st[:, None] + POOL_HIST - hs[None, :] % POOL_HIST < w) for w in POOL_WINDOWS])
    as_bf = lambda a: jnp.asarray(a.astype(np.float32), bf16)
    return dict(same=as_bf(same), causal=as_bf(causal), shift=as_bf(shift), stsel=as_bf(stsel),
                pcur=as_bf(pcur), phist=as_bf(phist))


def _sample_mixer_body(z_ref, xbc_ref, dt_ref, u_ref, cst_ref, pst_ref, ssm_in_ref,
                       cw_ref, cb_ref, dtb_ref, alog_ref, dskip_ref, nw_ref, pw_ref, ps_ref,
                       causal_ref, same_ref, expand_ref, expand_t_ref, shift_ref, stsel_ref, pcur_ref, phist_ref,
                       mix_ref, ssm_out_ref,
                       ydiag_ref, ea_ref, yt_ref, cdh_ref, cdl_ref, xdte_t_ref, bm_ref, cm_ref, *, pos0):
    f32, bf16 = jnp.float32, jnp.bfloat16
    s = pl.program_id(1)
    gw = D_SSD // N_SSD_GROUPS

    @pl.when(s == 0)
    def _():
        xbc = xbc_ref[...]
        cst = cst_ref[...]
        acc = cb_ref[...] + cw_ref[CONV_WIDTH - 1:CONV_WIDTH, :] * xbc
        for k in range(CONV_WIDTH - 1):
            tap = _dot_sel_left(shift_ref[k], xbc, 3) + _dot_sel_left(stsel_ref[k], cst, 3)
            acc = acc + cw_ref[k:k + 1, :] * tap
        xbc_c = acc * jax.nn.sigmoid(acc)

        causal_bf = causal_ref[...]
        keep = causal_bf > 0
        xs, bm, cm, dt, a_cum, xdt = _ssd_intra(xbc_c, dt_ref[...], dtb_ref, alog_ref, causal_bf, expand_ref)
        a_tot = _dot_sel_left(same_ref[...], dt * (-jnp.exp(alog_ref[...])), 3)
        a_cum_t = jnp.transpose(a_cum)
        ea_ref[...] = _dot_sel_right(jnp.exp(a_cum), expand_ref[...], 2)
        dte_x = _dot_sel_right(jnp.exp(a_tot - a_cum), expand_ref[...], 2)
        cd_col = _dot_sel_left(expand_t_ref[...], jnp.exp(jnp.transpose(a_tot)), 2)
        cd_hi, cd_lo = _split2(cd_col)
        cdh_ref[...] = cd_hi
        cdl_ref[...] = cd_lo
        bm_ref[...] = bm.astype(bf16)
        cm_ref[...] = cm.astype(bf16)
        for g in range(N_SSD_GROUPS):
            cg = cm[:, g * D_STATE:(g + 1) * D_STATE].astype(bf16)
            bg = bm[:, g * D_STATE:(g + 1) * D_STATE].astype(bf16)
            cb = lax.dot_general(cg, bg, NT_DIMS, preferred_element_type=f32)
            y_diag = _ssd_diag_group(g, cb, a_cum, a_cum_t, keep, xdt)
            ydiag_ref[:, g * gw:(g + 1) * gw] = y_diag + xs[:, g * gw:(g + 1) * gw] * dskip_ref[:, g * gw:(g + 1) * gw]
            xdte_t_ref[g * gw:(g + 1) * gw, :] = jnp.transpose(
                xdt[:, g * gw:(g + 1) * gw] * dte_x[:, g * gw:(g + 1) * gw]).astype(bf16)
        yt_ref[...] = jnp.zeros(yt_ref.shape, f32)

        u = u_ref[...]
        pst = pst_ref[...]
        step = lax.broadcasted_iota(jnp.int32, (BLK, 1), 0) % DEC_SEQ
        pos = (step + (pos0 + 1)).astype(f32)
        for gi, w in enumerate(POOL_WINDOWS):
            sl = slice(gi * POOL_GROUP_DIM, (gi + 1) * POOL_GROUP_DIM)
            ug = u[:, sl]
            wsum = _dot_sel_left(pcur_ref[gi], ug, 2) + _dot_sel_left(phist_ref[gi], pst[:, sl], 2)
            pooled = wsum / jnp.minimum(pos, jnp.float32(w)) - ug
            po = jnp.dot(pooled.astype(bf16), pw_ref[gi], preferred_element_type=f32) * ps_ref[:, sl]
            mix_ref[:, D_SSD + gi * POOL_GROUP_DIM:D_SSD + (gi + 1) * POOL_GROUP_DIM] = po.astype(bf16)

    hg = N_SSD_HEADS // N_SSD_GROUPS
    row_seq = lax.broadcasted_iota(jnp.int32, (BLK, LANES), 0) // DEC_SEQ
    col_seq = lax.broadcasted_iota(jnp.int32, (gw, BLK), 1) // DEC_SEQ
    row_idx = lax.broadcasted_iota(jnp.int32, (BLK, LANES), 0)
    for q in range(SEQ_PER_STEP):
        sq = s * SEQ_PER_STEP + q
        rows_of_s = row_seq == sq
        cols_of_s = col_seq == sq
        pick_s = jnp.where(row_idx == DEC_SEQ * sq, 1.0, 0.0).astype(bf16)
        state = ssm_in_ref[q].reshape(D_SSD, D_STATE)
        for g in range(N_SSD_GROUPS):
            rs = slice(g * gw, (g + 1) * gw)
            sg = state[rs, :]
            cg = cm_ref[:, g * D_STATE:(g + 1) * D_STATE]
            bg = bm_ref[:, g * D_STATE:(g + 1) * D_STATE]
            yt = lax.dot_general(sg.astype(bf16), cg, NT_DIMS, preferred_element_type=f32)
            yt_ref[rs, :] += jnp.where(cols_of_s, yt, 0.0)
            cd = (jnp.dot(cdh_ref[rs, :], pick_s, preferred_element_type=f32)
                  + jnp.dot(cdl_ref[rs, :], pick_s, preferred_element_type=f32))
            upd = jnp.dot(xdte_t_ref[rs, :], jnp.where(rows_of_s, bg, jnp.zeros_like(bg)),
                          preferred_element_type=f32)
            ssm_out_ref[q, g * hg:(g + 1) * hg] = (sg * cd + upd).reshape(hg, SSD_HEAD_DIM, D_STATE)

    @pl.when(s == pl.num_programs(1) - 1)
    def _():
        y = ydiag_ref[...] + jnp.transpose(yt_ref[...]) * ea_ref[...]
        mix_ref[:, 0:D_SSD] = _gated_norm(y, z_ref[...], nw_ref).astype(bf16)


def _sample_mixer(z, xbc, dt, u, row0, n_seq, state_conv, state_ssm, state_pool, pos0,
                  conv_w, conv_b, dt_bias, A_log, D_skip, ssd_norm_w, pool_w, pool_scale):
    f32, bf16 = jnp.float32, jnp.bfloat16
    n_blk = n_seq // SEQ_PER_BLK
    blk0 = row0 // BLK
    nk = CONV_WIDTH - 1
    k = _mixer_constants()
    ks = _sample_constants()

    def row_blk(width):
        return pl.BlockSpec((BLK, width), lambda j, s: (blk0 + j, 0))

    def const(shape):
        return pl.BlockSpec(shape, lambda j, s: (0,) * len(shape))

    steps = SEQ_PER_BLK // SEQ_PER_STEP
    state_spec = pl.BlockSpec((SEQ_PER_STEP, N_SSD_HEADS, SSD_HEAD_DIM, D_STATE),
                              lambda j, s: (j * steps + s, 0, 0, 0))
    pad_h = (0, LANES - N_SSD_HEADS)
    return pl.pallas_call(
        functools.partial(_sample_mixer_body, pos0=pos0),
        grid=(n_blk, steps),
        in_specs=[row_blk(D_SSD), row_blk(D_CONV), row_blk(LANES), row_blk(D_POOL),
                  pl.BlockSpec((SEQ_PER_BLK * nk, D_CONV), lambda j, s: (j, 0)),
                  pl.BlockSpec((SEQ_PER_BLK * POOL_HIST, D_POOL), lambda j, s: (j, 0)),
                  state_spec,
                  const((CONV_WIDTH, D_CONV)), const((1, D_CONV)), const((1, LANES)), const((1, LANES)),
                  const((1, D_SSD)), const((1, D_SSD)),
                  const((N_POOL_GROUPS, POOL_GROUP_DIM, POOL_GROUP_DIM)), const((1, D_POOL)),
                  const((BLK, BLK)), const((BLK, BLK)), const((LANES, D_SSD)), const((D_SSD, LANES)),
                  const((nk, BLK, BLK)), const((nk, BLK, SEQ_PER_BLK * nk)),
                  const((N_POOL_GROUPS, BLK, BLK)), const((N_POOL_GROUPS, BLK, SEQ_PER_BLK * POOL_HIST))],
        out_specs=[pl.BlockSpec((BLK, D_MIX), lambda j, s: (j, 0)), state_spec],
        out_shape=[jax.ShapeDtypeStruct((n_seq * DEC_SEQ, D_MIX), bf16),
                   jax.ShapeDtypeStruct((n_seq, N_SSD_HEADS, SSD_HEAD_DIM, D_STATE), f32)],
        scratch_shapes=[pltpu.VMEM((BLK, D_SSD), f32), pltpu.VMEM((BLK, D_SSD), f32),
                        pltpu.VMEM((D_SSD, BLK), f32), pltpu.VMEM((D_SSD, BLK), bf16),
                        pltpu.VMEM((D_SSD, BLK), bf16), pltpu.VMEM((D_SSD, BLK), bf16),
                        pltpu.VMEM((BLK, N_SSD_GROUPS * D_STATE), bf16),
                        pltpu.VMEM((BLK, N_SSD_GROUPS * D_STATE), bf16)],
        compiler_params=pltpu.CompilerParams(dimension_semantics=("parallel", "arbitrary"),
                                             vmem_limit_bytes=VMEM_LIMIT_BYTES),
        name="sample_mixer",
    )(z, xbc, dt, u, state_conv.reshape(n_seq * nk, D_CONV), state_pool.reshape(n_seq * POOL_HIST, D_POOL),
      state_ssm, conv_w.astype(f32), conv_b.reshape(1, D_CONV).astype(f32),
      jnp.pad(dt_bias.astype(f32), pad_h).reshape(1, LANES), jnp.pad(A_log.astype(f32), pad_h).reshape(1, LANES),
      jnp.repeat(D_skip.astype(f32), SSD_HEAD_DIM).reshape(1, D_SSD), ssd_norm_w.reshape(1, D_SSD).astype(f32),
      pool_w.astype(bf16), pool_scale.reshape(1, D_POOL).astype(f32),
      ks["causal"], ks["same"], k["expand"], k["expand_t"], ks["shift"], ks["stsel"],
      ks["pcur"], ks["phist"])


def _moe_sizes(n_tokens, tm):
    nt = n_tokens // tm
    lmax = -(-(TOP_K * tm + N_EXPERTS * MOE_SEG_ROWS) // MOE_SEL_ROWS) * MOE_SEL_ROWS
    rows = (TOP_K * n_tokens + nt * N_EXPERTS * MOE_SEG_ROWS + N_EXPERTS * (MOE_PIECE - 1)
            + MOE_ROW_CHUNK)
    n_rows = -(-rows // MOE_PIECE) * MOE_PIECE
    return nt, lmax, n_rows


def _out_proj_router_body(ma_ref, mb_ref, w_ref, xa_ref, xb_ref, nw_ref, rwh_ref, rwl_ref, rb_ref,
                          x1_ref, h_ref, posg_ref, post_ref, cnt_ref, *, tiles_a):
    f32, bf16 = jnp.float32, jnp.bfloat16
    first = pl.program_id(0) < tiles_a
    m = jnp.where(first, ma_ref[...], mb_ref[...])
    x = jnp.where(first, xa_ref[...], xb_ref[...])
    half = w_ref.shape[0] // 2
    x = (x + jnp.dot(m[:, :half], w_ref[:half, :].astype(bf16), preferred_element_type=f32)
         + jnp.dot(m[:, half:], w_ref[half:, :].astype(bf16), preferred_element_type=f32))
    x1_ref[...] = x
    tm = x.shape[0]
    h = x * lax.rsqrt(jnp.mean(x * x, axis=-1, keepdims=True) + EPS) * nw_ref[...]
    h_hi = h.astype(bf16)
    h_ref[...] = h_hi
    h_lo = (h - h_hi.astype(f32)).astype(bf16)
    wh = rwh_ref[...]
    logits = (jnp.dot(h_hi, wh, preferred_element_type=f32)
              + jnp.dot(h_lo, wh, preferred_element_type=f32)
              + jnp.dot(h_hi, rwl_ref[...], preferred_element_type=f32)) + rb_ref[...]
    lane = lax.broadcasted_iota(jnp.int32, (tm, LANES), 1)
    lanef = lane.astype(f32)
    neg = jnp.float32(-jnp.inf)
    l = jnp.where(lane < N_EXPERTS, logits, neg)
    sels, vals = [], []
    for _ in range(TOP_K):
        m = jnp.max(l, axis=1, keepdims=True)
        idx = jnp.min(jnp.where(l == m, lanef, jnp.float32(LANES)), axis=1, keepdims=True)
        sel = lanef == idx
        l = jnp.where(sel, neg, l)
        sels.append(sel)
        vals.append(m)
    exps = [jnp.exp(v - vals[0]) for v in vals]
    denom = exps[0] + exps[1] + exps[2] + exps[3]
    gates = [e / denom for e in exps]
    chosen = jnp.where(sels[0] | sels[1] | sels[2] | sels[3], 1.0, 0.0).astype(f32)
    row = lax.broadcasted_iota(jnp.int32, (tm, tm), 0)
    col = lax.broadcasted_iota(jnp.int32, (tm, tm), 1)
    lower = jnp.where(col < row, 1.0, 0.0).astype(bf16)
    rank = jnp.dot(lower, chosen.astype(bf16), preferred_element_type=f32)
    cnt = jnp.sum(chosen, axis=0, keepdims=True)
    seg_units = jnp.maximum(jnp.floor((cnt + (MOE_SEG_ROWS - 1)) * (1.0 / MOE_SEG_ROWS)), 1.0)
    r2 = lax.broadcasted_iota(jnp.int32, (LANES, LANES), 0)
    c2 = lax.broadcasted_iota(jnp.int32, (LANES, LANES), 1)
    upper = jnp.where(r2 < c2, 1.0, 0.0).astype(bf16)
    lstart = jnp.dot(jnp.broadcast_to(seg_units, (8, LANES)).astype(bf16), upper,
                     preferred_element_type=f32)[0:1, :] * MOE_SEG_ROWS
    posmat = lstart + rank
    posg = jnp.zeros((tm, LANES), f32)
    for k in range(TOP_K):
        pos_k = jnp.sum(jnp.where(sels[k], posmat, 0.0), axis=1, keepdims=True)
        posg = posg + jnp.where(lane == k, pos_k, 0.0) + jnp.where(lane == TOP_K + k, gates[k], 0.0)
    posg_ref[...] = posg
    post_ref[...] = jnp.transpose(posg)[0:8, :]
    cnt_ref[0] = jnp.broadcast_to(cnt, (8, LANES)).astype(jnp.int32)


def _out_proj_router(ma, mb, w_out, xa, xb, norm2_w, router_w, router_b, tm):
    d = xa.shape[1]
    n = xa.shape[0] + xb.shape[0]
    nt = n // tm
    f32, bf16 = jnp.float32, jnp.bfloat16
    rw = jnp.pad(router_w.astype(f32), ((0, 0), (0, LANES - N_EXPERTS)))
    rw_hi = rw.astype(bf16)
    rw_lo = (rw - rw_hi.astype(f32)).astype(bf16)
    rb = jnp.pad(router_b.astype(f32), (0, LANES - N_EXPERTS)).reshape(1, LANES)

    def const(shape, **kw):
        return pl.BlockSpec(shape, lambda i: (0,) * len(shape), **kw)

    return pl.pallas_call(
        functools.partial(_out_proj_router_body, tiles_a=xa.shape[0] // tm),
        grid=(nt,),
        in_specs=[*_two_part_specs(xa.shape[0], xb.shape[0], tm, D_MIX),
                  const((D_MIX, d), pipeline_mode=pl.Buffered(1)),
                  *_two_part_specs(xa.shape[0], xb.shape[0], tm, d),
                  const((1, d)), const((d, LANES)), const((d, LANES)), const((1, LANES))],
        out_specs=[pl.BlockSpec((tm, d), lambda i: (i, 0)),
                   pl.BlockSpec((tm, d), lambda i: (i, 0)),
                   pl.BlockSpec((tm, LANES), lambda i: (i, 0)),
                   pl.BlockSpec((8, tm), lambda i: (0, i)),
                   pl.BlockSpec((1, 8, LANES), lambda i: (i, 0, 0))],
        out_shape=[jax.ShapeDtypeStruct((n, d), f32),
                   jax.ShapeDtypeStruct((n, d), bf16),
                   jax.ShapeDtypeStruct((n, LANES), f32),
                   jax.ShapeDtypeStruct((8, n), f32),
                   jax.ShapeDtypeStruct((nt, 8, LANES), jnp.int32)],
        compiler_params=pltpu.CompilerParams(dimension_semantics=("parallel",),
                                             vmem_limit_bytes=VMEM_LIMIT_BYTES),
        name="out_proj_router",
    )(ma, mb, w_out.astype(f32), xa, xb, norm2_w.reshape(1, d).astype(f32), rw_hi, rw_lo, rb)


def _moe_plan(cnt):
    i32 = jnp.int32
    pad = jnp.maximum((cnt + (MOE_SEG_ROWS - 1)) // MOE_SEG_ROWS, 1) * MOE_SEG_ROWS
    lstart = jnp.cumsum(pad, axis=1) - pad
    lp = jnp.sum(pad, axis=1)
    tot = jnp.sum(pad, axis=0)
    reg = (tot + (MOE_PIECE - 1)) // MOE_PIECE * MOE_PIECE
    reg_end = jnp.cumsum(reg)
    estart = reg_end - reg
    seg = estart[None, :] + jnp.cumsum(pad, axis=0) - pad
    return dict(
        lstart=lstart.reshape(-1).astype(i32), seg_units=(pad // MOE_SEG_ROWS).reshape(-1).astype(i32),
        seg=seg.reshape(-1).astype(i32), lp=lp.astype(i32),
        tail_start=(estart + tot).astype(i32), tail_units=((reg - tot) // MOE_SEG_ROWS).astype(i32),
        estart=estart.astype(i32), erows=reg.astype(i32), used=reg_end[-1].reshape(1).astype(i32))


def _for_each_segment_copy(i, lstart_ref, units_ref, seg_ref, local_ref, global_ref, sem, to_global, fn):
    def per_expert(e, carry):
        k = i * N_EXPERTS + e
        n = pl.multiple_of(units_ref[k] * MOE_SEG_ROWS, MOE_SEG_ROWS)
        loc = local_ref.at[pl.ds(pl.multiple_of(lstart_ref[k], MOE_SEG_ROWS), n)]
        glo = global_ref.at[pl.ds(pl.multiple_of(seg_ref[k], MOE_SEG_ROWS), n)]
        fn(pltpu.make_async_copy(loc, glo, sem) if to_global else pltpu.make_async_copy(glo, loc, sem))
        return carry
    lax.fori_loop(0, N_EXPERTS, per_expert, 0)


def _wait_segment_copies(i, lp_ref, local_ref, global_ref, sem):
    n = pl.multiple_of(lp_ref[i], MOE_SEG_ROWS)
    pltpu.make_async_copy(global_ref.at[pl.ds(0, n)], local_ref.at[pl.ds(0, n)], sem).wait()


def _for_each_unused_piece(used_ref, zero_ref, rows_ref, sem, fn):
    def per_piece(j, c):
        go = pl.multiple_of(j * MOE_PIECE, MOE_PIECE)
        fn(pltpu.make_async_copy(zero_ref, rows_ref.at[pl.ds(go, MOE_PIECE)], sem))
        return c
    lax.fori_loop(used_ref[0] // MOE_PIECE, rows_ref.shape[0] // MOE_PIECE, per_piece, 0)


def _dispatch_body(lstart_ref, units_ref, seg_ref, lp_ref, tail_start_ref, tail_units_ref, used_ref,
                   h_ref, post_ref, xs_ref, stage_ref, sel_ref, zero_ref, sems, fill_sem):
    f32, bf16 = jnp.float32, jnp.bfloat16
    i = pl.program_id(0)
    nt = pl.num_programs(0)
    slot = i % 2
    tm = h_ref.shape[0]
    lmax = stage_ref.shape[1]

    def for_each_fill_copy(fn):
        def per_expert(e, carry):
            @pl.when(tail_units_ref[e] > 0)
            def _():
                n = pl.multiple_of(tail_units_ref[e] * MOE_SEG_ROWS, MOE_SEG_ROWS)
                go = pl.multiple_of(tail_start_ref[e], MOE_SEG_ROWS)
                fn(pltpu.make_async_copy(zero_ref.at[pl.ds(0, n)], xs_ref.at[pl.ds(go, n)], fill_sem))
            return carry
        lax.fori_loop(0, N_EXPERTS, per_expert, 0)
        _for_each_unused_piece(used_ref, zero_ref, xs_ref, fill_sem, fn)

    @pl.when(i == 0)
    def _():
        zero_ref[...] = jnp.zeros(zero_ref.shape, bf16)
        for_each_fill_copy(lambda cp: cp.start())
        for_each_fill_copy(lambda cp: cp.wait())

    def segment_copies(tile, slot_, fn):
        _for_each_segment_copy(tile, lstart_ref, units_ref, seg_ref, stage_ref.at[slot_], xs_ref,
                               sems.at[slot_], True, fn)

    @pl.when(i >= 2)
    def _():
        _wait_segment_copies(i - 2, lp_ref, stage_ref.at[slot], xs_ref, sems.at[slot])

    pos = [post_ref[k:k + 1, :] for k in range(TOP_K)]
    r_local = lax.broadcasted_iota(jnp.int32, (MOE_CHUNK, tm), 0).astype(f32).astype(bf16)
    one, zero = jnp.ones((), bf16), jnp.zeros((), bf16)
    for part in range(lmax // MOE_SEL_ROWS):
        for c in range(MOE_SEL_ROWS // MOE_CHUNK):
            r0 = part * MOE_SEL_ROWS + c * MOE_CHUNK
            loc = [jnp.clip(p - r0, -1.0, float(MOE_CHUNK)).astype(bf16) for p in pos]
            hit = (loc[0] == r_local) | (loc[1] == r_local) | (loc[2] == r_local) | (loc[3] == r_local)
            sel_ref[part, c * MOE_CHUNK:(c + 1) * MOE_CHUNK, :] = jnp.where(hit, one, zero)
        stage_ref[slot, part * MOE_SEL_ROWS:(part + 1) * MOE_SEL_ROWS, :] = jnp.dot(
            sel_ref[part], h_ref[...], preferred_element_type=f32).astype(bf16)

    segment_copies(i, slot, lambda cp: cp.start())

    @pl.when(i == nt - 1)
    def _():
        @pl.when(nt >= 2)
        def _():
            _wait_segment_copies(i - 1, lp_ref, stage_ref.at[1 - slot], xs_ref, sems.at[1 - slot])
        _wait_segment_copies(i, lp_ref, stage_ref.at[slot], xs_ref, sems.at[slot])


def _moe_dispatch(h2, post, plan, tm, lmax, n_rows):
    n, d = h2.shape
    nt = n // tm
    grid_spec = pltpu.PrefetchScalarGridSpec(
        num_scalar_prefetch=7,
        grid=(nt,),
        in_specs=[pl.BlockSpec((tm, d), lambda i, *_: (i, 0)),
                  pl.BlockSpec((8, tm), lambda i, *_: (0, i))],
        out_specs=pl.BlockSpec(memory_space=pl.ANY),
        scratch_shapes=[pltpu.VMEM((2, lmax, d), jnp.bfloat16),
                        pltpu.VMEM((lmax // MOE_SEL_ROWS, MOE_SEL_ROWS, tm), jnp.bfloat16),
                        pltpu.VMEM((MOE_PIECE, d), jnp.bfloat16),
                        pltpu.SemaphoreType.DMA((2,)),
                        pltpu.SemaphoreType.DMA(())])
    return pl.pallas_call(
        _dispatch_body,
        grid_spec=grid_spec,
        out_shape=jax.ShapeDtypeStruct((n_rows, d), jnp.bfloat16),
        compiler_params=pltpu.CompilerParams(dimension_semantics=("arbitrary",),
                                             vmem_limit_bytes=VMEM_LIMIT_BYTES),
        name="moe_dispatch",
    )(plan["lstart"], plan["seg_units"], plan["seg"], plan["lp"], plan["tail_start"], plan["tail_units"],
      plan["used"], h2, post)


def _experts_body(first_ref, count_ref, cstart_ref, cvalid_ref, total_ref, used_ref,
                  xs_ref, wgu_ref, bgu_ref, wd_ref, bd_ref, os_ref,
                  wgu_bf, wd_bf, xbuf, obuf, zero_ref, in_sems, out_sems, fill_sem):
    f32, bf16 = jnp.float32, jnp.bfloat16
    e = pl.program_id(0)
    total = total_ref[0]
    half = D_FF // 2

    def in_copy(j):
        src = xs_ref.at[pl.ds(pl.multiple_of(cstart_ref[j], MOE_PIECE), MOE_ROW_CHUNK)]
        return pltpu.make_async_copy(src, xbuf.at[j % 3], in_sems.at[j % 3])

    def out_copy(j):
        n = pl.multiple_of(cvalid_ref[j], MOE_PIECE)
        go = pl.multiple_of(cstart_ref[j], MOE_PIECE)
        return pltpu.make_async_copy(obuf.at[j % 2, pl.ds(0, n)], os_ref.at[pl.ds(go, n)], out_sems.at[j % 2])

    @pl.when(e == 0)
    def _():
        for j0 in range(2):
            @pl.when(j0 < total)
            def _():
                in_copy(j0).start()
        zero_ref[...] = jnp.zeros(zero_ref.shape, bf16)
        _for_each_unused_piece(used_ref, zero_ref, os_ref, fill_sem, lambda cp: cp.start())
        _for_each_unused_piece(used_ref, zero_ref, os_ref, fill_sem, lambda cp: cp.wait())

    @pl.when(count_ref[e] > 0)
    def _():
        wgu_bf[...] = wgu_ref[0].astype(bf16)
        wd_bf[...] = wd_ref[0].astype(bf16)

        def chunk(j, carry):
            in_copy(j).wait()

            @pl.when(j + 2 < total)
            def _():
                in_copy(j + 2).start()

            @pl.when(j >= 2)
            def _():
                out_copy(j - 2).wait()

            def mlp(n_rows):
                x = xbuf[j % 3, 0:n_rows, :]
                out = bd_ref[0]
                for hf in range(2):
                    gate = jnp.dot(x, wgu_bf[:, hf * half:(hf + 1) * half], preferred_element_type=f32)
                    gate = jnp.minimum(gate + bgu_ref[0, :, hf * half:(hf + 1) * half], SWIGLU_LIMIT)
                    up = jnp.dot(x, wgu_bf[:, D_FF + hf * half:D_FF + (hf + 1) * half],
                                 preferred_element_type=f32)
                    up = jnp.clip(up + bgu_ref[0, :, D_FF + hf * half:D_FF + (hf + 1) * half],
                                  -SWIGLU_LIMIT, SWIGLU_LIMIT)
                    act = (up + 1.0) * (gate * jax.nn.sigmoid(SWIGLU_ALPHA * gate))
                    out = out + jnp.dot(act.astype(bf16), wd_bf[hf * half:(hf + 1) * half, :],
                                        preferred_element_type=f32)
                obuf[j % 2, 0:n_rows, :] = out.astype(bf16)

            @pl.when(cvalid_ref[j] > MOE_ROW_CHUNK // 2)
            def _():
                mlp(MOE_ROW_CHUNK)

            @pl.when(cvalid_ref[j] <= MOE_ROW_CHUNK // 2)
            def _():
                mlp(MOE_ROW_CHUNK // 2)
            out_copy(j).start()
            return carry
        lax.fori_loop(first_ref[e], first_ref[e] + count_ref[e], chunk, 0)

    @pl.when(e == pl.num_programs(0) - 1)
    def _():
        @pl.when(total >= 2)
        def _():
            out_copy(total - 2).wait()

        @pl.when(total >= 1)
        def _():
            out_copy(total - 1).wait()


def _expert_chunks(plan, n_rows):
    i32 = jnp.int32
    max_chunks = n_rows // MOE_ROW_CHUNK + N_EXPERTS
    count = (plan["erows"] + (MOE_ROW_CHUNK - 1)) // MOE_ROW_CHUNK
    end = jnp.cumsum(count)
    first = end - count
    j = jnp.arange(max_chunks, dtype=i32)
    mine = ((first[None, :] <= j[:, None]) & (j[:, None] < end[None, :])).astype(i32)
    c = j - jnp.sum(mine * first[None, :], axis=1)
    cstart = jnp.sum(mine * plan["estart"][None, :], axis=1) + jnp.sum(mine, axis=1) * c * MOE_ROW_CHUNK
    cvalid = jnp.sum(mine * jnp.clip(plan["erows"][None, :] - c[:, None] * MOE_ROW_CHUNK, 0, MOE_ROW_CHUNK),
                     axis=1)
    return (first.astype(i32), count.astype(i32), cstart.astype(i32), cvalid.astype(i32),
            end[-1].reshape(1).astype(i32))


def _moe_experts(xs, plan, w_gate_up, b_gate_up, w_down, b_down):
    d = xs.shape[1]
    grid_spec = pltpu.PrefetchScalarGridSpec(
        num_scalar_prefetch=6,
        grid=(N_EXPERTS,),
        in_specs=[pl.BlockSpec(memory_space=pl.ANY),
                  pl.BlockSpec((1, d, 2 * D_FF), lambda e, *_: (e, 0, 0)),
                  pl.BlockSpec((1, 1, 2 * D_FF), lambda e, *_: (e, 0, 0)),
                  pl.BlockSpec((1, D_FF, d), lambda e, *_: (e, 0, 0)),
                  pl.BlockSpec((1, 1, d), lambda e, *_: (e, 0, 0))],
        out_specs=pl.BlockSpec(memory_space=pl.ANY),
        scratch_shapes=[pltpu.VMEM((d, 2 * D_FF), jnp.bfloat16),
                        pltpu.VMEM((D_FF, d), jnp.bfloat16),
                        pltpu.VMEM((3, MOE_ROW_CHUNK, d), jnp.bfloat16),
                        pltpu.VMEM((2, MOE_ROW_CHUNK, d), jnp.bfloat16),
                        pltpu.VMEM((MOE_PIECE, d), jnp.bfloat16),
                        pltpu.SemaphoreType.DMA((3,)),
                        pltpu.SemaphoreType.DMA((2,)),
                        pltpu.SemaphoreType.DMA(())])
    return pl.pallas_call(
        _experts_body,
        grid_spec=grid_spec,
        out_shape=jax.ShapeDtypeStruct(xs.shape, jnp.bfloat16),
        compiler_params=pltpu.CompilerParams(dimension_semantics=("arbitrary",),
                                             vmem_limit_bytes=VMEM_LIMIT_BYTES),
        name="moe_experts",
    )(*_expert_chunks(plan, xs.shape[0]), plan["used"],
      xs, w_gate_up, b_gate_up.reshape(N_EXPERTS, 1, 2 * D_FF), w_down, b_down.reshape(N_EXPERTS, 1, d))


def _combine_body(lstart_ref, units_ref, seg_ref, lp_ref,
                  os_ref, posg_ref, x_ref, fw_ref, yp_ref, ys_ref, stage_ref, w_ref, sems,
                  *, n_prompt_tiles):
    f32, bf16 = jnp.float32, jnp.bfloat16
    i = pl.program_id(0)
    nt = pl.num_programs(0)
    slot = i % 2
    tm = x_ref.shape[0]
    lmax = stage_ref.shape[1]

    def segment_copies(tile, slot_, fn):
        _for_each_segment_copy(tile, lstart_ref, units_ref, seg_ref, stage_ref.at[slot_], os_ref,
                               sems.at[slot_], False, fn)

    @pl.when(i == 0)
    def _():
        stage_ref[...] = jnp.zeros(stage_ref.shape, bf16)
        segment_copies(0, 0, lambda cp: cp.start())

    @pl.when(i + 1 < nt)
    def _():
        segment_copies(i + 1, 1 - slot, lambda cp: cp.start())

    posg = posg_ref[...]
    pos = [posg[:, k:k + 1] for k in range(TOP_K)]
    gate = [posg[:, TOP_K + k:TOP_K + k + 1] for k in range(TOP_K)]
    gate_bf = [g.astype(bf16) for g in gate]
    r_local = lax.broadcasted_iota(jnp.int32, (tm, MOE_CHUNK), 1).astype(f32).astype(bf16)
    _wait_segment_copies(i, lp_ref, stage_ref.at[slot], os_ref, sems.at[slot])
    y = x_ref[...]
    part_cols = lmax // 2
    for part in range(2):
        for c in range(part * part_cols // MOE_CHUNK, (part + 1) * part_cols // MOE_CHUNK):
            w = jnp.zeros((tm, MOE_CHUNK), bf16)
            for k in range(TOP_K):
                loc = jnp.clip(pos[k] - c * MOE_CHUNK, -1.0, float(MOE_CHUNK)).astype(bf16)
                w = jnp.where(loc == r_local, gate_bf[k], w)
            w_ref[:, c * MOE_CHUNK:(c + 1) * MOE_CHUNK] = w
        y = y + jnp.dot(w_ref[:, part * part_cols:(part + 1) * part_cols],
                        stage_ref[slot, part * part_cols:(part + 1) * part_cols, :],
                        preferred_element_type=f32)
    out = y * lax.rsqrt(jnp.mean(y * y, axis=-1, keepdims=True) + EPS) * fw_ref[...]

    @pl.when(i < n_prompt_tiles)
    def _():
        yp_ref[...] = out

    @pl.when(i >= n_prompt_tiles)
    def _():
        ys_ref[...] = out


def _moe_combine(os_, posg, x1, final_norm_w, plan, tm, lmax, n_prompt):
    n, d = x1.shape
    nt = n // tm
    n_prompt_tiles = n_prompt // tm
    n_sample_tiles = nt - n_prompt_tiles
    grid_spec = pltpu.PrefetchScalarGridSpec(
        num_scalar_prefetch=4,
        grid=(nt,),
        in_specs=[pl.BlockSpec(memory_space=pl.ANY),
                  pl.BlockSpec((tm, LANES), lambda i, *_: (i, 0)),
                  pl.BlockSpec((tm, d), lambda i, *_: (i, 0)),
                  pl.BlockSpec((1, d), lambda i, *_: (0, 0))],
        out_specs=[pl.BlockSpec((tm, d), lambda i, *_: (jnp.minimum(i, n_prompt_tiles - 1), 0)),
                   pl.BlockSpec((tm, d), lambda i, *_: (jnp.maximum(i - n_prompt_tiles, 0), 0))],
        scratch_shapes=[pltpu.VMEM((2, lmax, d), jnp.bfloat16),
                        pltpu.VMEM((tm, lmax), jnp.bfloat16),
                        pltpu.SemaphoreType.DMA((2,))])
    return pl.pallas_call(
        functools.partial(_combine_body, n_prompt_tiles=n_prompt_tiles),
        grid_spec=grid_spec,
        out_shape=[jax.ShapeDtypeStruct((n_prompt, d), jnp.float32),
                   jax.ShapeDtypeStruct((n_sample_tiles * tm, d), jnp.float32)],
        compiler_params=pltpu.CompilerParams(dimension_semantics=("arbitrary",),
                                             vmem_limit_bytes=VMEM_LIMIT_BYTES),
        name="moe_combine",
    )(plan["lstart"], plan["seg_units"], plan["seg"], plan["lp"],
      os_, posg, x1, final_norm_w.reshape(1, d).astype(jnp.float32))


def _moe_block(ma, mb, w_out, xa, xb, norm2_w, router_w, router_b, w_gate_up, b_gate_up, w_down, b_down,
               final_norm_w, tm=MOE_TOKEN_TILE):
    n_prompt = xa.shape[0]
    n = n_prompt + xb.shape[0]
    nt, lmax, n_rows = _moe_sizes(n, tm)
    x1, h2, posg, post, cnt3 = _out_proj_router(ma, mb, w_out, xa, xb, norm2_w, router_w, router_b, tm)
    plan = _moe_plan(cnt3[:, 0, :N_EXPERTS])
    xs = _moe_dispatch(h2, post, plan, tm, lmax, n_rows)
    os_ = _moe_experts(xs, plan, w_gate_up, b_gate_up, w_down, b_down)
    return _moe_combine(os_, posg, x1, final_norm_w, plan, tm, lmax, n_prompt)


def kernel(x_prompt, x_sample, state_ssm, state_conv, state_pool, norm1_w, w_in, conv_w, conv_b, dt_bias,
           A_log, D_skip, ssd_norm_w, pool_w, pool_scale, w_out, norm2_w, router_w, router_b, w_gate_up,
           b_gate_up, w_down, b_down, final_norm_w):
    n_prompt = BATCH * SEQ
    n_sample = DEC_BATCH * DEC_SEQ
    xp = x_prompt.reshape(n_prompt, D_MODEL)
    xs = x_sample.reshape(n_sample, D_MODEL)
    mp = (conv_w[0], conv_b[0], dt_bias[0], A_log[0], D_skip[0], ssd_norm_w[0], pool_w[0], pool_scale[0])
    mix_p, s1, ctail, ptail = _prompt_fused(xp, BATCH, SEQ, norm1_w[0], w_in, *mp)
    half = n_sample // 2
    z, xbc, dt_raw, u = _in_proj(xs[:half], xs[half:], norm1_w[0], w_in, tm=half)
    mix_s, s2 = _sample_mixer(z, xbc, dt_raw, u, 0, DEC_BATCH, state_conv, state_ssm[0], state_pool,
                              PAST_LEN, *mp)
    nk = CONV_WIDTH - 1
    c1 = ctail[:, CONV_TAIL_ROWS - nk:]
    p1 = ptail[:, HIST_ROWS - POOL_HIST:]
    c2 = xbc.reshape(DEC_BATCH, DEC_SEQ, D_CONV)[:, DEC_SEQ - nk:]
    p2 = jnp.concatenate([state_pool[0][:, DEC_SEQ:], u.reshape(DEC_BATCH, DEC_SEQ, D_POOL)], axis=1)
    yp, ys = _moe_block(mix_p, mix_s, w_out[0], xp, xs, norm2_w[0], router_w[0], router_b[0], w_gate_up[0],
                        b_gate_up[0], w_down[0], b_down[0], final_norm_w)
    return (yp.reshape(x_prompt.shape), ys.reshape(x_sample.shape),
            s1[None], c1[None], p1[None], s2[None], c2[None], p2[None])
```

```python
import functools
import jax, jax.numpy as jnp
from jax import lax
import numpy as np
from jax.experimental import pallas as pl
from jax.experimental.pallas import tpu as pltpu

D_MODEL = 1024
BATCH = 8
SEQ = 2048
DEC_BATCH = 128
DEC_SEQ = 4
PAST_LEN = 16384

D_MIX = 2 * D_MODEL
D_SSD = 3 * D_MIX // 4
SSD_HEAD_DIM = 64
N_SSD_HEADS = D_SSD // SSD_HEAD_DIM
N_SSD_GROUPS = 4
D_STATE = 128
CONV_WIDTH = 4
SSD_CHUNK = 128
D_CONV = D_SSD + 2 * N_SSD_GROUPS * D_STATE
D_POOL = D_MIX - D_SSD
POOL_WINDOWS = (2, 4, 8, 16)
N_POOL_GROUPS = len(POOL_WINDOWS)
POOL_GROUP_DIM = D_POOL // N_POOL_GROUPS
POOL_HIST = max(POOL_WINDOWS) - 1
D_IN_PROJ = D_SSD + D_CONV + N_SSD_HEADS + D_POOL
N_EXPERTS = 32
TOP_K = 4
D_FF = D_MODEL
SWIGLU_LIMIT = 7.0
SWIGLU_ALPHA = 1.702
EPS = 1e-5

LANES = 128
BF16_SUBLANES = 16
VMEM_LIMIT_BYTES = 48 * 1024 * 1024

MOE_TOKEN_TILE = 512
MOE_SEG_ROWS = BF16_SUBLANES
MOE_PIECE = 128
MOE_ROW_CHUNK = 512
MOE_CHUNK = 256
MOE_SEL_ROWS = 512


BLK = SSD_CHUNK
PROJ_ROW_TILE = 512
PROMPT_BLKS_PER_STEP = 2
HIST_ROWS = 16
CONV_TAIL_ROWS = 8
NT_DIMS = (((1,), (1,)), ((), ()))


def _split2(v):
    hi = v.astype(jnp.bfloat16)
    lo = (v - hi.astype(jnp.float32)).astype(jnp.bfloat16)
    return hi, lo


def _dot_sel_left(sel, v, passes):
    out = None
    rem = v
    for p in range(passes):
        part = rem.astype(jnp.bfloat16)
        d = jnp.dot(sel, part, preferred_element_type=jnp.float32)
        out = d if out is None else out + d
        if p + 1 < passes:
            rem = rem - part.astype(jnp.float32)
    return out


def _dot_sel_right(v, sel, passes):
    out = None
    rem = v
    for p in range(passes):
        part = rem.astype(jnp.bfloat16)
        d = jnp.dot(part, sel, preferred_element_type=jnp.float32)
        out = d if out is None else out + d
        if p + 1 < passes:
            rem = rem - part.astype(jnp.float32)
    return out


def _two_part_specs(n_first, n_second, tm, width):
    t1 = n_first // tm
    t2 = n_second // tm
    return (pl.BlockSpec((tm, width), lambda i: (jnp.minimum(i, t1 - 1), 0)),
            pl.BlockSpec((tm, width), lambda i: (jnp.clip(i - t1, 0, t2 - 1), 0)))


IN_PROJ_COLS = 512


def _in_proj_body(xa_ref, xb_ref, nw_ref, wmain_ref, wtail_ref, z_ref, xbc_ref, dt_ref, u_ref, *, tiles_a):
    f32, bf16 = jnp.float32, jnp.bfloat16
    x = jnp.where(pl.program_id(0) < tiles_a, xa_ref[...], xb_ref[...])
    h = (x * lax.rsqrt(jnp.mean(x * x, axis=-1, keepdims=True) + EPS) * nw_ref[...]).astype(bf16)
    off = 0
    for ref in (z_ref, xbc_ref):
        for c0 in range(0, ref.shape[1], IN_PROJ_COLS):
            w = wmain_ref[0, :, off + c0:off + c0 + IN_PROJ_COLS].astype(bf16)
            ref[:, c0:c0 + IN_PROJ_COLS] = jnp.dot(h, w, preferred_element_type=f32)
        off += ref.shape[1]
    tail = jnp.dot(h, wtail_ref[0, :, 0:N_SSD_HEADS + D_POOL].astype(bf16), preferred_element_type=f32)
    lane = lax.broadcasted_iota(jnp.int32, (tail.shape[0], LANES), 1)
    dt_ref[...] = jnp.where(lane < N_SSD_HEADS, tail[:, 0:LANES], 0.0)
    u_ref[...] = tail[:, N_SSD_HEADS:N_SSD_HEADS + D_POOL]


def _in_proj(xa, xb, norm1_w, w_in3, tm=PROJ_ROW_TILE):
    d = xa.shape[1]
    n = xa.shape[0] + xb.shape[0]
    f32, bf16 = jnp.float32, jnp.bfloat16
    s1 = D_SSD + D_CONV
    tail_blk = 1024
    assert s1 % tail_blk == 0 and N_SSD_HEADS + D_POOL <= tail_blk
    widths = (D_SSD, D_CONV, LANES, D_POOL)
    return pl.pallas_call(
        functools.partial(_in_proj_body, tiles_a=xa.shape[0] // tm),
        grid=(n // tm,),
        in_specs=[*_two_part_specs(xa.shape[0], xb.shape[0], tm, d),
                  pl.BlockSpec((1, d), lambda i: (0, 0)),
                  pl.BlockSpec((1, d, s1), lambda i: (0, 0, 0), pipeline_mode=pl.Buffered(1)),
                  pl.BlockSpec((1, d, tail_blk), lambda i: (0, 0, s1 // tail_blk), pipeline_mode=pl.Buffered(1))],
        out_specs=[pl.BlockSpec((tm, wd), lambda i: (i, 0)) for wd in widths],
        out_shape=[jax.ShapeDtypeStruct((n, wd), f32) for wd in widths],
        compiler_params=pltpu.CompilerParams(dimension_semantics=("parallel",),
                                             vmem_limit_bytes=VMEM_LIMIT_BYTES),
        name="in_proj",
    )(xa, xb, norm1_w.reshape(1, d).astype(f32), w_in3, w_in3)


def _mixer_constants():
    bf16 = jnp.bfloat16
    h = np.arange(LANES)[:, None]
    ch = np.arange(D_SSD)[None, :]
    expand = (ch // SSD_HEAD_DIM == h).astype(np.float32)
    i = np.arange(BLK)[:, None]
    j = np.arange(BLK)[None, :]
    causal = (j <= i).astype(np.float32)
    return dict(expand=jnp.asarray(expand, bf16), expand_t=jnp.asarray(expand.T, bf16),
                causal=jnp.asarray(causal, bf16))


def _softplus(x):
    return jnp.maximum(x, 0.0) + jnp.log(1.0 + jnp.exp(-jnp.abs(x)))


def _conv_silu(ext_ref, cw_ref, cb_ref, first_row):
    ext = ext_ref[...]
    last = first_row + CONV_WIDTH - 1
    acc = cb_ref[...] + cw_ref[CONV_WIDTH - 1:CONV_WIDTH, :] * ext[last:last + BLK, :]
    for k in range(CONV_WIDTH - 1):
        tap = pltpu.roll(ext, CONV_WIDTH - 1 - k, axis=0)[last:last + BLK, :]
        acc = acc + cw_ref[k:k + 1, :] * tap
    return acc * jax.nn.sigmoid(acc)


def _ssd_intra(xbc_c, dt_raw, dtb_ref, alog_ref, causal_bf, expand_ref):
    f32 = jnp.float32
    xs = xbc_c[:, :D_SSD]
    bm = xbc_c[:, D_SSD:D_SSD + N_SSD_GROUPS * D_STATE]
    cm = xbc_c[:, D_SSD + N_SSD_GROUPS * D_STATE:]
    dt = _softplus(dt_raw + dtb_ref[...])
    a = dt * (-jnp.exp(alog_ref[...]))
    a_cum = _dot_sel_left(causal_bf, a, 3)
    dt_x = _dot_sel_right(dt, expand_ref[...], 2)
    return xs, bm, cm, dt, a_cum, xs * dt_x


def _ssd_diag_group(g, cb, a_cum, a_cum_t, keep, xdt):
    f32, bf16 = jnp.float32, jnp.bfloat16
    hg = N_SSD_HEADS // N_SSD_GROUPS
    lane = lax.broadcasted_iota(jnp.int32, (BLK, LANES), 1)
    first_head = lane < SSD_HEAD_DIM
    neg = jnp.float32(-jnp.inf)
    outs = []
    for pr in range(hg * SSD_HEAD_DIM // LANES):
        h1 = g * hg + 2 * pr
        blk = (g * hg * SSD_HEAD_DIM) // LANES + pr
        xp = xdt[:, blk * LANES:(blk + 1) * LANES]
        x1 = jnp.where(first_head, xp, 0.0).astype(bf16)
        x2 = jnp.where(first_head, 0.0, xp).astype(bf16)
        m1 = (cb * jnp.exp(jnp.where(keep, a_cum[:, h1:h1 + 1] - a_cum_t[h1:h1 + 1, :], neg))).astype(bf16)
        m2 = (cb * jnp.exp(jnp.where(keep, a_cum[:, h1 + 1:h1 + 2] - a_cum_t[h1 + 1:h1 + 2, :], neg))).astype(bf16)
        outs.append(jnp.dot(m1, x1, preferred_element_type=f32) + jnp.dot(m2, x2, preferred_element_type=f32))
    return jnp.concatenate(outs, axis=1)


def _gated_norm(y, z, nw_ref):
    yg = y * (z * jax.nn.sigmoid(z))
    return yg * lax.rsqrt(jnp.mean(yg * yg, axis=-1, keepdims=True) + EPS) * nw_ref[...]


def _reset_history(ext_ref, pool_tail_ref, state_ref):
    ext_ref[0:CONV_TAIL_ROWS, :] = jnp.zeros((CONV_TAIL_ROWS, D_CONV), jnp.float32)
    pool_tail_ref[...] = jnp.zeros(pool_tail_ref.shape, jnp.float32)
    state_ref[...] = jnp.zeros(state_ref.shape, jnp.float32)


def _prompt_block(c, z_ref, xbc_ref, dt_ref, u_ref, cw_ref, cb_ref, dtb_ref, alog_ref, dskip_ref, nw_ref,
                  pw_ref, ps_ref, causal_ref, mix_ref, ext_ref, pool_tail_ref, state_ref, side_work):
    f32, bf16 = jnp.float32, jnp.bfloat16
    gw = D_SSD // N_SSD_GROUPS

    def do_side_work():
        piece = next(side_work, None)
        if piece is not None:
            piece()

    u = u_ref[...]
    ext_u = jnp.concatenate([pool_tail_ref[...], u], axis=0)
    pos = (c * BLK + lax.broadcasted_iota(jnp.int32, (BLK, 1), 0) + 1).astype(f32)
    for gi, w in enumerate(POOL_WINDOWS):
        assert w & (w - 1) == 0 and w <= HIST_ROWS
        sl = slice(gi * POOL_GROUP_DIM, (gi + 1) * POOL_GROUP_DIM)
        ug = u[:, sl]
        acc = ext_u[:, sl]
        span = 1
        while span < w:
            acc = acc + pltpu.roll(acc, span, axis=0)
            span *= 2
        wsum = acc[HIST_ROWS:, :]
        pooled = wsum / jnp.minimum(pos, jnp.float32(w)) - ug
        po = jnp.dot(pooled.astype(bf16), pw_ref[gi], preferred_element_type=f32) * ps_ref[:, sl]
        mix_ref[:, D_SSD + gi * POOL_GROUP_DIM:D_SSD + (gi + 1) * POOL_GROUP_DIM] = po.astype(bf16)
    pool_tail_ref[...] = u_ref[BLK - HIST_ROWS:BLK, :]
    do_side_work()

    ext_ref[CONV_TAIL_ROWS:CONV_TAIL_ROWS + BLK, :] = xbc_ref[...]
    xbc_c = _conv_silu(ext_ref, cw_ref, cb_ref, CONV_TAIL_ROWS - (CONV_WIDTH - 1))
    ext_ref[0:CONV_TAIL_ROWS, :] = xbc_ref[BLK - CONV_TAIL_ROWS:BLK, :]
    do_side_work()

    causal_bf = causal_ref[...]
    keep = causal_bf > 0
    xs = xbc_c[:, :D_SSD]
    bm = xbc_c[:, D_SSD:D_SSD + N_SSD_GROUPS * D_STATE]
    cm = xbc_c[:, D_SSD + N_SSD_GROUPS * D_STATE:]
    dt = _softplus(dt_ref[...] + dtb_ref[...])
    a_cum = dt * (-jnp.exp(alog_ref[...]))
    row = lax.broadcasted_iota(jnp.int32, (BLK, LANES), 0)
    span = 1
    while span < BLK:
        a_cum = a_cum + jnp.where(row >= span, pltpu.roll(a_cum, span, axis=0), 0.0)
        span *= 2
    a_cum_t = jnp.transpose(a_cum)
    a_tot = a_cum[BLK - 1:BLK, :]
    ea = jnp.exp(a_cum)
    dte = jnp.exp(a_tot - a_cum)
    cd = jnp.exp(a_tot)
    hg = N_SSD_HEADS // N_SSD_GROUPS
    first_head = lax.broadcasted_iota(jnp.int32, (BLK, LANES), 1) < SSD_HEAD_DIM
    neg = jnp.float32(-jnp.inf)

    def head_cols(v, h1):
        return jnp.where(first_head, v[:, h1:h1 + 1], v[:, h1 + 1:h1 + 2])

    def decay_from(h):
        return jnp.exp(jnp.where(keep, a_cum[:, h:h + 1] - a_cum_t[h:h + 1, :], neg))

    y_parts = []
    for g in range(N_SSD_GROUPS):
        do_side_work()
        cg = cm[:, g * D_STATE:(g + 1) * D_STATE].astype(bf16)
        bg = bm[:, g * D_STATE:(g + 1) * D_STATE].astype(bf16)
        cb = lax.dot_general(cg, bg, NT_DIMS, preferred_element_type=f32)
        sg = state_ref[g * gw:(g + 1) * gw, :]
        y_off = lax.dot_general(cg, sg.astype(bf16), NT_DIMS, preferred_element_type=f32)
        xdte_parts = []
        for pr in range(gw // LANES):
            h1 = g * hg + 2 * pr
            sl = slice(h1 * SSD_HEAD_DIM, h1 * SSD_HEAD_DIM + LANES)
            xp = xs[:, sl] * head_cols(dt, h1)
            x1 = jnp.where(first_head, xp, 0.0).astype(bf16)
            x2 = jnp.where(first_head, 0.0, xp).astype(bf16)
            y_diag = (jnp.dot((cb * decay_from(h1)).astype(bf16), x1, preferred_element_type=f32)
                      + jnp.dot((cb * decay_from(h1 + 1)).astype(bf16), x2, preferred_element_type=f32))
            y_parts.append(y_diag + y_off[:, pr * LANES:(pr + 1) * LANES] * head_cols(ea, h1)
                           + xs[:, sl] * dskip_ref[:, sl])
            xdte_parts.append(xp * head_cols(dte, h1))
        xdte_t = jnp.transpose(jnp.concatenate(xdte_parts, axis=1)).astype(bf16)
        cd_rows = jnp.concatenate([jnp.broadcast_to(cd[:, h:h + 1], (SSD_HEAD_DIM, D_STATE))
                                   for h in range(g * hg, (g + 1) * hg)], axis=0)
        state_ref[g * gw:(g + 1) * gw, :] = sg * cd_rows + jnp.dot(xdte_t, bg, preferred_element_type=f32)
    y = jnp.concatenate(y_parts, axis=1)
    mix_ref[:, 0:D_SSD] = _gated_norm(y, z_ref[...], nw_ref).astype(bf16)


W_SLAB_COLS = 512
PROJ_PIECE_COLS = 512


def _prompt_fused_body(x0_ref, xn_ref, n1w_ref, w_hbm, wtail_ref,
                       cw_ref, cb_ref, dtb_ref, alog_ref, dskip_ref, nw_ref, pw_ref, ps_ref, causal_ref,
                       mix_ref, ssm_ref, ctail_ref, ptail_ref,
                       wbf_ref, wtail_bf_ref, wstage_ref, z_s, xbc_s, dt_s, u_s, ext_ref, pool_tail_ref,
                       state_ref, wsem, *, steps_per_seq):
    f32, bf16 = jnp.float32, jnp.bfloat16
    s = pl.program_id(0)
    step_rows = x0_ref.shape[0]
    n_main = wbf_ref.shape[1]
    n_tail = N_SSD_HEADS + D_POOL

    def projection_pieces(x_ref, slot):
        x = x_ref[...]
        h = (x * lax.rsqrt(jnp.mean(x * x, axis=-1, keepdims=True) + EPS) * n1w_ref[...]).astype(bf16)

        def slab(dst, c0, w0):
            def piece():
                dst[slot, :, c0:c0 + PROJ_PIECE_COLS] = jnp.dot(h, wbf_ref[:, w0:w0 + PROJ_PIECE_COLS],
                                                                preferred_element_type=f32)
            return piece

        def tail_piece():
            tail = jnp.dot(h, wtail_bf_ref[:, 0:n_tail], preferred_element_type=f32)
            lane = lax.broadcasted_iota(jnp.int32, (step_rows, LANES), 1)
            dt_s[slot] = jnp.where(lane < N_SSD_HEADS, tail[:, 0:LANES], 0.0)
            u_s[slot] = tail[:, N_SSD_HEADS:n_tail]

        pieces = [slab(z_s, c0, c0) for c0 in range(0, D_SSD, PROJ_PIECE_COLS)]
        pieces += [slab(xbc_s, c0, D_SSD + c0) for c0 in range(0, D_CONV, PROJ_PIECE_COLS)]
        return pieces + [tail_piece]

    def project(x_ref, slot):
        for piece in projection_pieces(x_ref, slot):
            piece()

    @pl.when(s == 0)
    def _():
        for k in range(n_main // W_SLAB_COLS):
            cp = pltpu.make_async_copy(w_hbm.at[0, :, pl.ds(k * W_SLAB_COLS, W_SLAB_COLS)], wstage_ref, wsem)
            cp.start()
            cp.wait()
            wbf_ref[:, k * W_SLAB_COLS:(k + 1) * W_SLAB_COLS] = wstage_ref[...].astype(bf16)
        wtail_bf_ref[...] = wtail_ref[0, :, 0:wtail_bf_ref.shape[1]].astype(bf16)
        project(x0_ref, 0)

    @pl.when(s % steps_per_seq == 0)
    def _():
        _reset_history(ext_ref, pool_tail_ref, state_ref)

    pieces = iter(projection_pieces(xn_ref, (s + 1) % 2))
    slot = s % 2
    c0 = (s % steps_per_seq) * PROMPT_BLKS_PER_STEP
    for ci in range(PROMPT_BLKS_PER_STEP):
        rows = pl.ds(ci * BLK, BLK)
        _prompt_block(c0 + ci,
                      z_s.at[slot, rows], xbc_s.at[slot, rows], dt_s.at[slot, rows], u_s.at[slot, rows],
                      cw_ref, cb_ref, dtb_ref, alog_ref, dskip_ref, nw_ref, pw_ref, ps_ref, causal_ref,
                      mix_ref.at[rows], ext_ref, pool_tail_ref, state_ref, pieces)
    for piece in pieces:
        piece()

    @pl.when(s % steps_per_seq == steps_per_seq - 1)
    def _():
        ssm_ref[0] = state_ref[...].reshape(N_SSD_HEADS, SSD_HEAD_DIM, D_STATE)
        ctail_ref[0] = xbc_s[slot, step_rows - CONV_TAIL_ROWS:step_rows, :]
        ptail_ref[0] = u_s[slot, step_rows - HIST_ROWS:step_rows, :]


def _prompt_fused(xp, n_seq, seq_len, norm1_w, w_in3, conv_w, conv_b, dt_bias, A_log, D_skip, ssd_norm_w,
                  pool_w, pool_scale):
    f32, bf16 = jnp.float32, jnp.bfloat16
    n, d = xp.shape
    step_rows = PROMPT_BLKS_PER_STEP * BLK
    steps_per_seq = seq_len // step_rows
    n_steps = n // step_rows
    s1 = D_SSD + D_CONV
    tail_blk = 1024
    tail_bf_cols = -(-(N_SSD_HEADS + D_POOL) // LANES) * LANES
    assert s1 % tail_blk == 0 and tail_bf_cols <= tail_blk and s1 % W_SLAB_COLS == 0
    k = _mixer_constants()

    def const(shape):
        return pl.BlockSpec(shape, lambda s: (0,) * len(shape))

    def per_seq(shape):
        return pl.BlockSpec(shape, lambda s: (s // steps_per_seq,) + (0,) * (len(shape) - 1))

    pad_h = (0, LANES - N_SSD_HEADS)
    return pl.pallas_call(
        functools.partial(_prompt_fused_body, steps_per_seq=steps_per_seq),
        grid=(n_steps,),
        in_specs=[pl.BlockSpec((step_rows, d), lambda s: (0, 0)),
                  pl.BlockSpec((step_rows, d), lambda s: (jnp.minimum(s + 1, n_steps - 1), 0)),
                  const((1, d)),
                  pl.BlockSpec(memory_space=pl.ANY),
                  pl.BlockSpec((1, d, tail_blk), lambda s: (0, 0, s1 // tail_blk), pipeline_mode=pl.Buffered(1)),
                  const((CONV_WIDTH, D_CONV)), const((1, D_CONV)), const((1, LANES)), const((1, LANES)),
                  const((1, D_SSD)), const((1, D_SSD)),
                  const((N_POOL_GROUPS, POOL_GROUP_DIM, POOL_GROUP_DIM)), const((1, D_POOL)),
                  const((BLK, BLK))],
        out_specs=[pl.BlockSpec((step_rows, D_MIX), lambda s: (s, 0)),
                   per_seq((1, N_SSD_HEADS, SSD_HEAD_DIM, D_STATE)),
                   per_seq((1, CONV_TAIL_ROWS, D_CONV)),
                   per_seq((1, HIST_ROWS, D_POOL))],
        out_shape=[jax.ShapeDtypeStruct((n, D_MIX), bf16),
                   jax.ShapeDtypeStruct((n_seq, N_SSD_HEADS, SSD_HEAD_DIM, D_STATE), f32),
                   jax.ShapeDtypeStruct((n_seq, CONV_TAIL_ROWS, D_CONV), f32),
                   jax.ShapeDtypeStruct((n_seq, HIST_ROWS, D_POOL), f32)],
        scratch_shapes=[pltpu.VMEM((d, s1), bf16),
                        pltpu.VMEM((d, tail_bf_cols), bf16),
                        pltpu.VMEM((d, W_SLAB_COLS), f32),
                        pltpu.VMEM((2, step_rows, D_SSD), f32),
                        pltpu.VMEM((2, step_rows, D_CONV), f32),
                        pltpu.VMEM((2, step_rows, LANES), f32),
                        pltpu.VMEM((2, step_rows, D_POOL), f32),
                        pltpu.VMEM((CONV_TAIL_ROWS + BLK, D_CONV), f32),
                        pltpu.VMEM((HIST_ROWS, D_POOL), f32),
                        pltpu.VMEM((D_SSD, D_STATE), f32),
                        pltpu.SemaphoreType.DMA(())],
        compiler_params=pltpu.CompilerParams(dimension_semantics=("arbitrary",),
                                             vmem_limit_bytes=VMEM_LIMIT_BYTES),
        name="prompt_fused",
    )(xp, xp, norm1_w.reshape(1, d).astype(f32), w_in3, w_in3,
      conv_w.astype(f32), conv_b.reshape(1, D_CONV).astype(f32),
      jnp.pad(dt_bias.astype(f32), pad_h).reshape(1, LANES), jnp.pad(A_log.astype(f32), pad_h).reshape(1, LANES),
      jnp.repeat(D_skip.astype(f32), SSD_HEAD_DIM).reshape(1, D_SSD), ssd_norm_w.reshape(1, D_SSD).astype(f32),
      pool_w.astype(bf16), pool_scale.reshape(1, D_POOL).astype(f32), k["causal"])


SEQ_PER_BLK = BLK // DEC_SEQ
SEQ_PER_STEP = 8


def _sample_constants():
    bf16 = jnp.bfloat16
    r = np.arange(BLK)
    sq, st = r // DEC_SEQ, r % DEC_SEQ
    same = sq[:, None] == sq[None, :]
    causal = same & (st[None, :] <= st[:, None])
    nk = CONV_WIDTH - 1
    shift = np.stack([same & (st[None, :] == st[:, None] + k - nk) for k in range(nk)])
    cs = np.arange(SEQ_PER_BLK * nk)
    stsel = np.stack([(cs[None, :] // nk == sq[:, None]) & (cs[None, :] % nk == st[:, None] + k)
                      for k in range(nk)])
    pcur = np.stack([causal & (st[:, None] - st[None, :] < w) for w in POOL_WINDOWS])
    hs = np.arange(SEQ_PER_BLK * POOL_HIST)
    phist = np.stack([(hs[None, :] // POOL_HIST == sq[:, None])
                      & (st[:, None] + POOL_HIST - hs[None, :] % POOL_HIST < w) for w in POOL_WINDOWS])
    as_bf = lambda a: jnp.asarray(a.astype(np.float32), bf16)
    return dict(same=as_bf(same), causal=as_bf(causal), shift=as_bf(shift), stsel=as_bf(stsel),
                pcur=as_bf(pcur), phist=as_bf(phist))


def _sample_mixer_body(z_ref, xbc_ref, dt_ref, u_ref, cst_ref, pst_ref, ssm_in_ref,
                       cw_ref, cb_ref, dtb_ref, alog_ref, dskip_ref, nw_ref, pw_ref, ps_ref,
                       causal_ref, same_ref, expand_ref, expand_t_ref, shift_ref, stsel_ref, pcur_ref, phist_ref,
                       mix_ref, ssm_out_ref,
                       ydiag_ref, ea_ref, yt_ref, cdh_ref, cdl_ref, xdte_t_ref, bm_ref, cm_ref, *, pos0):
    f32, bf16 = jnp.float32, jnp.bfloat16
    s = pl.program_id(1)
    gw = D_SSD // N_SSD_GROUPS

    @pl.when(s == 0)
    def _():
        xbc = xbc_ref[...]
        cst = cst_ref[...]
        acc = cb_ref[...] + cw_ref[CONV_WIDTH - 1:CONV_WIDTH, :] * xbc
        for k in range(CONV_WIDTH - 1):
            tap = _dot_sel_left(shift_ref[k], xbc, 3) + _dot_sel_left(stsel_ref[k], cst, 3)
            acc = acc + cw_ref[k:k + 1, :] * tap
        xbc_c = acc * jax.nn.sigmoid(acc)

        causal_bf = causal_ref[...]
        keep = causal_bf > 0
        xs, bm, cm, dt, a_cum, xdt = _ssd_intra(xbc_c, dt_ref[...], dtb_ref, alog_ref, causal_bf, expand_ref)
        a_tot = _dot_sel_left(same_ref[...], dt * (-jnp.exp(alog_ref[...])), 3)
        a_cum_t = jnp.transpose(a_cum)
        ea_ref[...] = _dot_sel_right(jnp.exp(a_cum), expand_ref[...], 2)
        dte_x = _dot_sel_right(jnp.exp(a_tot - a_cum), expand_ref[...], 2)
        cd_col = _dot_sel_left(expand_t_ref[...], jnp.exp(jnp.transpose(a_tot)), 2)
        cd_hi, cd_lo = _split2(cd_col)
        cdh_ref[...] = cd_hi
        cdl_ref[...] = cd_lo
        bm_ref[...] = bm.astype(bf16)
        cm_ref[...] = cm.astype(bf16)
        for g in range(N_SSD_GROUPS):
            cg = cm[:, g * D_STATE:(g + 1) * D_STATE].astype(bf16)
            bg = bm[:, g * D_STATE:(g + 1) * D_STATE].astype(bf16)
            cb = lax.dot_general(cg, bg, NT_DIMS, preferred_element_type=f32)
            y_diag = _ssd_diag_group(g, cb, a_cum, a_cum_t, keep, xdt)
            ydiag_ref[:, g * gw:(g + 1) * gw] = y_diag + xs[:, g * gw:(g + 1) * gw] * dskip_ref[:, g * gw:(g + 1) * gw]
            xdte_t_ref[g * gw:(g + 1) * gw, :] = jnp.transpose(
                xdt[:, g * gw:(g + 1) * gw] * dte_x[:, g * gw:(g + 1) * gw]).astype(bf16)
        yt_ref[...] = jnp.zeros(yt_ref.shape, f32)

        u = u_ref[...]
        pst = pst_ref[...]
        step = lax.broadcasted_iota(jnp.int32, (BLK, 1), 0) % DEC_SEQ
        pos = (step + (pos0 + 1)).astype(f32)
        for gi, w in enumerate(POOL_WINDOWS):
            sl = slice(gi * POOL_GROUP_DIM, (gi + 1) * POOL_GROUP_DIM)
            ug = u[:, sl]
            wsum = _dot_sel_left(pcur_ref[gi], ug, 2) + _dot_sel_left(phist_ref[gi], pst[:, sl], 2)
            pooled = wsum / jnp.minimum(pos, jnp.float32(w)) - ug
            po = jnp.dot(pooled.astype(bf16), pw_ref[gi], preferred_element_type=f32) * ps_ref[:, sl]
            mix_ref[:, D_SSD + gi * POOL_GROUP_DIM:D_SSD + (gi + 1) * POOL_GROUP_DIM] = po.astype(bf16)

    hg = N_SSD_HEADS // N_SSD_GROUPS
    row_seq = lax.broadcasted_iota(jnp.int32, (BLK, LANES), 0) // DEC_SEQ
    col_seq = lax.broadcasted_iota(jnp.int32, (gw, BLK), 1) // DEC_SEQ
    row_idx = lax.broadcasted_iota(jnp.int32, (BLK, LANES), 0)
    for q in range(SEQ_PER_STEP):
        sq = s * SEQ_PER_STEP + q
        rows_of_s = row_seq == sq
        cols_of_s = col_seq == sq
        pick_s = jnp.where(row_idx == DEC_SEQ * sq, 1.0, 0.0).astype(bf16)
        state = ssm_in_ref[q].reshape(D_SSD, D_STATE)
        for g in range(N_SSD_GROUPS):
            rs = slice(g * gw, (g + 1) * gw)
            sg = state[rs, :]
            cg = cm_ref[:, g * D_STATE:(g + 1) * D_STATE]
            bg = bm_ref[:, g * D_STATE:(g + 1) * D_STATE]
            yt = lax.dot_general(sg.astype(bf16), cg, NT_DIMS, preferred_element_type=f32)
            yt_ref[rs, :] += jnp.where(cols_of_s, yt, 0.0)
            cd = (jnp.dot(cdh_ref[rs, :], pick_s, preferred_element_type=f32)
                  + jnp.dot(cdl_ref[rs, :], pick_s, preferred_element_type=f32))
            upd = jnp.dot(xdte_t_ref[rs, :], jnp.where(rows_of_s, bg, jnp.zeros_like(bg)),
                          preferred_element_type=f32)
            ssm_out_ref[q, g * hg:(g + 1) * hg] = (sg * cd + upd).reshape(hg, SSD_HEAD_DIM, D_STATE)

    @pl.when(s == pl.num_programs(1) - 1)
    def _():
        y = ydiag_ref[...] + jnp.transpose(yt_ref[...]) * ea_ref[...]
        mix_ref[:, 0:D_SSD] = _gated_norm(y, z_ref[...], nw_ref).astype(bf16)


def _sample_mixer(z, xbc, dt, u, row0, n_seq, state_conv, state_ssm, state_pool, pos0,
                  conv_w, conv_b, dt_bias, A_log, D_skip, ssd_norm_w, pool_w, pool_scale):
    f32, bf16 = jnp.float32, jnp.bfloat16
    n_blk = n_seq // SEQ_PER_BLK
    blk0 = row0 // BLK
    nk = CONV_WIDTH - 1
    k = _mixer_constants()
    ks = _sample_constants()

    def row_blk(width):
        return pl.BlockSpec((BLK, width), lambda j, s: (blk0 + j, 0))

    def const(shape):
        return pl.BlockSpec(shape, lambda j, s: (0,) * len(shape))

    steps = SEQ_PER_BLK // SEQ_PER_STEP
    state_spec = pl.BlockSpec((SEQ_PER_STEP, N_SSD_HEADS, SSD_HEAD_DIM, D_STATE),
                              lambda j, s: (j * steps + s, 0, 0, 0))
    pad_h = (0, LANES - N_SSD_HEADS)
    return pl.pallas_call(
        functools.partial(_sample_mixer_body, pos0=pos0),
        grid=(n_blk, steps),
        in_specs=[row_blk(D_SSD), row_blk(D_CONV), row_blk(LANES), row_blk(D_POOL),
                  pl.BlockSpec((SEQ_PER_BLK * nk, D_CONV), lambda j, s: (j, 0)),
                  pl.BlockSpec((SEQ_PER_BLK * POOL_HIST, D_POOL), lambda j, s: (j, 0)),
                  state_spec,
                  const((CONV_WIDTH, D_CONV)), const((1, D_CONV)), const((1, LANES)), const((1, LANES)),
                  const((1, D_SSD)), const((1, D_SSD)),
                  const((N_POOL_GROUPS, POOL_GROUP_DIM, POOL_GROUP_DIM)), const((1, D_POOL)),
                  const((BLK, BLK)), const((BLK, BLK)), const((LANES, D_SSD)), const((D_SSD, LANES)),
                  const((nk, BLK, BLK)), const((nk, BLK, SEQ_PER_BLK * nk)),
                  const((N_POOL_GROUPS, BLK, BLK)), const((N_POOL_GROUPS, BLK, SEQ_PER_BLK * POOL_HIST))],
        out_specs=[pl.BlockSpec((BLK, D_MIX), lambda j, s: (j, 0)), state_spec],
        out_shape=[jax.ShapeDtypeStruct((n_seq * DEC_SEQ, D_MIX), bf16),
                   jax.ShapeDtypeStruct((n_seq, N_SSD_HEADS, SSD_HEAD_DIM, D_STATE), f32)],
        scratch_shapes=[pltpu.VMEM((BLK, D_SSD), f32), pltpu.VMEM((BLK, D_SSD), f32),
                        pltpu.VMEM((D_SSD, BLK), f32), pltpu.VMEM((D_SSD, BLK), bf16),
                        pltpu.VMEM((D_SSD, BLK), bf16), pltpu.VMEM((D_SSD, BLK), bf16),
                        pltpu.VMEM((BLK, N_SSD_GROUPS * D_STATE), bf16),
                        pltpu.VMEM((BLK, N_SSD_GROUPS * D_STATE), bf16)],
        compiler_params=pltpu.CompilerParams(dimension_semantics=("parallel", "arbitrary"),
                                             vmem_limit_bytes=VMEM_LIMIT_BYTES),
        name="sample_mixer",
    )(z, xbc, dt, u, state_conv.reshape(n_seq * nk, D_CONV), state_pool.reshape(n_seq * POOL_HIST, D_POOL),
      state_ssm, conv_w.astype(f32), conv_b.reshape(1, D_CONV).astype(f32),
      jnp.pad(dt_bias.astype(f32), pad_h).reshape(1, LANES), jnp.pad(A_log.astype(f32), pad_h).reshape(1, LANES),
      jnp.repeat(D_skip.astype(f32), SSD_HEAD_DIM).reshape(1, D_SSD), ssd_norm_w.reshape(1, D_SSD).astype(f32),
      pool_w.astype(bf16), pool_scale.reshape(1, D_POOL).astype(f32),
      ks["causal"], ks["same"], k["expand"], k["expand_t"], ks["shift"], ks["stsel"],
      ks["pcur"], ks["phist"])


def _moe_sizes(n_tokens, tm):
    nt = n_tokens // tm
    lmax = -(-(TOP_K * tm + N_EXPERTS * MOE_SEG_ROWS) // MOE_SEL_ROWS) * MOE_SEL_ROWS
    rows = (TOP_K * n_tokens + nt * N_EXPERTS * MOE_SEG_ROWS + N_EXPERTS * (MOE_PIECE - 1)
            + MOE_ROW_CHUNK)
    n_rows = -(-rows // MOE_PIECE) * MOE_PIECE
    return nt, lmax, n_rows


def _out_proj_router_body(ma_ref, mb_ref, w_ref, xa_ref, xb_ref, nw_ref, rwh_ref, rwl_ref, rb_ref,
                          x1_ref, h_ref, posg_ref, post_ref, cnt_ref, xprev_ref, *, tiles_a):
    f32, bf16 = jnp.float32, jnp.bfloat16
    i = pl.program_id(0)

    @pl.when(i == 0)
    def _():
        xprev_ref[...] = jnp.zeros(xprev_ref.shape, f32)

    first = i < tiles_a
    mix = jnp.where(first, ma_ref[...], mb_ref[...])
    xin = jnp.where(first, xa_ref[...], xb_ref[...])

    def out_piece(c0):
        def piece():
            cols = slice(c0, c0 + OUT_PIECE_COLS)
            x1_ref[:, cols] = xin[:, cols] + jnp.dot(mix, w_ref[:, cols].astype(bf16), preferred_element_type=f32)
        return piece

    pieces = iter([out_piece(c0) for c0 in range(0, xin.shape[1], OUT_PIECE_COLS)])

    def do_side_work():
        piece = next(pieces, None)
        if piece is not None:
            piece()

    x = xprev_ref[...]
    tm = x.shape[0]
    h = x * lax.rsqrt(jnp.mean(x * x, axis=-1, keepdims=True) + EPS) * nw_ref[...]
    h_hi = h.astype(bf16)
    h_ref[...] = h_hi
    h_lo = (h - h_hi.astype(f32)).astype(bf16)
    wh = rwh_ref[...]
    logits = (jnp.dot(h_hi, wh, preferred_element_type=f32)
              + jnp.dot(h_lo, wh, preferred_element_type=f32)
              + jnp.dot(h_hi, rwl_ref[...], preferred_element_type=f32)) + rb_ref[...]
    lane = lax.broadcasted_iota(jnp.int32, (tm, LANES), 1)
    lanef = lane.astype(f32)
    neg = jnp.float32(-jnp.inf)
    l = jnp.where(lane < N_EXPERTS, logits, neg)
    sels, vals = [], []
    for _ in range(TOP_K):
        do_side_work()
        m = jnp.max(l, axis=1, keepdims=True)
        idx = jnp.min(jnp.where(l == m, lanef, jnp.float32(LANES)), axis=1, keepdims=True)
        sel = lanef == idx
        l = jnp.where(sel, neg, l)
        sels.append(sel)
        vals.append(m)
    exps = [jnp.exp(v - vals[0]) for v in vals]
    denom = exps[0] + exps[1] + exps[2] + exps[3]
    gates = [e / denom for e in exps]
    chosen = jnp.where(sels[0] | sels[1] | sels[2] | sels[3], 1.0, 0.0).astype(f32)
    row = lax.broadcasted_iota(jnp.int32, (tm, tm), 0)
    col = lax.broadcasted_iota(jnp.int32, (tm, tm), 1)
    lower = jnp.where(col < row, 1.0, 0.0).astype(bf16)
    rank = jnp.dot(lower, chosen.astype(bf16), preferred_element_type=f32)
    cnt = jnp.sum(chosen, axis=0, keepdims=True)
    seg_units = jnp.maximum(jnp.floor((cnt + (MOE_SEG_ROWS - 1)) * (1.0 / MOE_SEG_ROWS)), 1.0)
    r2 = lax.broadcasted_iota(jnp.int32, (LANES, LANES), 0)
    c2 = lax.broadcasted_iota(jnp.int32, (LANES, LANES), 1)
    upper = jnp.where(r2 < c2, 1.0, 0.0).astype(bf16)
    lstart = jnp.dot(jnp.broadcast_to(seg_units, (8, LANES)).astype(bf16), upper,
                     preferred_element_type=f32)[0:1, :] * MOE_SEG_ROWS
    posmat = lstart + rank
    posg = jnp.zeros((tm, LANES), f32)
    for k in range(TOP_K):
        pos_k = jnp.sum(jnp.where(sels[k], posmat, 0.0), axis=1, keepdims=True)
        posg = posg + jnp.where(lane == k, pos_k, 0.0) + jnp.where(lane == TOP_K + k, gates[k], 0.0)
    posg_ref[...] = posg
    post_ref[...] = jnp.transpose(posg)[0:8, :]
    cnt_ref[0] = jnp.broadcast_to(cnt, (8, LANES)).astype(jnp.int32)
    for piece in pieces:
        piece()
    xprev_ref[...] = x1_ref[...]


OUT_PIECE_COLS = 256


def _out_proj_router(ma, mb, w_out, xa, xb, norm2_w, router_w, router_b, tm):
    d = xa.shape[1]
    n = xa.shape[0] + xb.shape[0]
    nt = n // tm
    f32, bf16 = jnp.float32, jnp.bfloat16
    rw = jnp.pad(router_w.astype(f32), ((0, 0), (0, LANES - N_EXPERTS)))
    rw_hi = rw.astype(bf16)
    rw_lo = (rw - rw_hi.astype(f32)).astype(bf16)
    rb = jnp.pad(router_b.astype(f32), (0, LANES - N_EXPERTS)).reshape(1, LANES)

    def const(shape, **kw):
        return pl.BlockSpec(shape, lambda i: (0,) * len(shape), **kw)

    def routed(i):
        return jnp.maximum(i - 1, 0)

    return pl.pallas_call(
        functools.partial(_out_proj_router_body, tiles_a=xa.shape[0] // tm),
        grid=(nt + 1,),
        in_specs=[*_two_part_specs(xa.shape[0], xb.shape[0], tm, D_MIX),
                  const((D_MIX, d), pipeline_mode=pl.Buffered(1)),
                  *_two_part_specs(xa.shape[0], xb.shape[0], tm, d),
                  const((1, d)), const((d, LANES)), const((d, LANES)), const((1, LANES))],
        out_specs=[pl.BlockSpec((tm, d), lambda i: (jnp.minimum(i, nt - 1), 0)),
                   pl.BlockSpec((tm, d), lambda i: (routed(i), 0)),
                   pl.BlockSpec((tm, LANES), lambda i: (routed(i), 0)),
                   pl.BlockSpec((8, tm), lambda i: (0, routed(i))),
                   pl.BlockSpec((1, 8, LANES), lambda i: (routed(i), 0, 0))],
        scratch_shapes=[pltpu.VMEM((tm, d), f32)],
        out_shape=[jax.ShapeDtypeStruct((n, d), f32),
                   jax.ShapeDtypeStruct((n, d), bf16),
                   jax.ShapeDtypeStruct((n, LANES), f32),
                   jax.ShapeDtypeStruct((8, n), f32),
                   jax.ShapeDtypeStruct((nt, 8, LANES), jnp.int32)],
        compiler_params=pltpu.CompilerParams(dimension_semantics=("arbitrary",),
                                             vmem_limit_bytes=VMEM_LIMIT_BYTES),
        name="out_proj_router",
    )(ma, mb, w_out.astype(f32), xa, xb, norm2_w.reshape(1, d).astype(f32), rw_hi, rw_lo, rb)


def _moe_plan(cnt):
    i32 = jnp.int32
    pad = jnp.maximum((cnt + (MOE_SEG_ROWS - 1)) // MOE_SEG_ROWS, 1) * MOE_SEG_ROWS
    lstart = jnp.cumsum(pad, axis=1) - pad
    lp = jnp.sum(pad, axis=1)
    tot = jnp.sum(pad, axis=0)
    reg = (tot + (MOE_PIECE - 1)) // MOE_PIECE * MOE_PIECE
    reg_end = jnp.cumsum(reg)
    estart = reg_end - reg
    seg = estart[None, :] + jnp.cumsum(pad, axis=0) - pad
    return dict(
        lstart=lstart.reshape(-1).astype(i32), seg_units=(pad // MOE_SEG_ROWS).reshape(-1).astype(i32),
        seg=seg.reshape(-1).astype(i32), lp=lp.astype(i32),
        tail_start=(estart + tot).astype(i32), tail_units=((reg - tot) // MOE_SEG_ROWS).astype(i32),
        estart=estart.astype(i32), erows=reg.astype(i32), used=reg_end[-1].reshape(1).astype(i32))


def _for_each_segment_copy(i, lstart_ref, units_ref, seg_ref, local_ref, global_ref, sem, to_global, fn):
    def per_expert(e, carry):
        k = i * N_EXPERTS + e
        n = pl.multiple_of(units_ref[k] * MOE_SEG_ROWS, MOE_SEG_ROWS)
        loc = local_ref.at[pl.ds(pl.multiple_of(lstart_ref[k], MOE_SEG_ROWS), n)]
        glo = global_ref.at[pl.ds(pl.multiple_of(seg_ref[k], MOE_SEG_ROWS), n)]
        fn(pltpu.make_async_copy(loc, glo, sem) if to_global else pltpu.make_async_copy(glo, loc, sem))
        return carry
    lax.fori_loop(0, N_EXPERTS, per_expert, 0)


def _wait_segment_copies(i, lp_ref, local_ref, global_ref, sem):
    n = pl.multiple_of(lp_ref[i], MOE_SEG_ROWS)
    pltpu.make_async_copy(global_ref.at[pl.ds(0, n)], local_ref.at[pl.ds(0, n)], sem).wait()


def _for_each_unused_piece(used_ref, zero_ref, rows_ref, sem, fn):
    def per_piece(j, c):
        go = pl.multiple_of(j * MOE_PIECE, MOE_PIECE)
        fn(pltpu.make_async_copy(zero_ref, rows_ref.at[pl.ds(go, MOE_PIECE)], sem))
        return c
    lax.fori_loop(used_ref[0] // MOE_PIECE, rows_ref.shape[0] // MOE_PIECE, per_piece, 0)


def _dispatch_body(lstart_ref, units_ref, seg_ref, lp_ref, tail_start_ref, tail_units_ref, used_ref,
                   h_ref, post_ref, xs_ref, stage_ref, sel_ref, zero_ref, sems, fill_sem):
    f32, bf16 = jnp.float32, jnp.bfloat16
    i = pl.program_id(0)
    nt = pl.num_programs(0)
    slot = i % 2
    tm = h_ref.shape[0]
    lmax = stage_ref.shape[1]

    def for_each_fill_copy(fn):
        def per_expert(e, carry):
            @pl.when(tail_units_ref[e] > 0)
            def _():
                n = pl.multiple_of(tail_units_ref[e] * MOE_SEG_ROWS, MOE_SEG_ROWS)
                go = pl.multiple_of(tail_start_ref[e], MOE_SEG_ROWS)
                fn(pltpu.make_async_copy(zero_ref.at[pl.ds(0, n)], xs_ref.at[pl.ds(go, n)], fill_sem))
            return carry
        lax.fori_loop(0, N_EXPERTS, per_expert, 0)
        _for_each_unused_piece(used_ref, zero_ref, xs_ref, fill_sem, fn)

    @pl.when(i == 0)
    def _():
        zero_ref[...] = jnp.zeros(zero_ref.shape, bf16)
        for_each_fill_copy(lambda cp: cp.start())
        for_each_fill_copy(lambda cp: cp.wait())

    def segment_copies(tile, slot_, fn):
        _for_each_segment_copy(tile, lstart_ref, units_ref, seg_ref, stage_ref.at[slot_], xs_ref,
                               sems.at[slot_], True, fn)

    @pl.when(i >= 2)
    def _():
        _wait_segment_copies(i - 2, lp_ref, stage_ref.at[slot], xs_ref, sems.at[slot])

    pos = [post_ref[k:k + 1, :] for k in range(TOP_K)]
    r_local = lax.broadcasted_iota(jnp.int32, (MOE_CHUNK, tm), 0).astype(f32).astype(bf16)
    one, zero = jnp.ones((), bf16), jnp.zeros((), bf16)
    for part in range(lmax // MOE_SEL_ROWS):
        for c in range(MOE_SEL_ROWS // MOE_CHUNK):
            r0 = part * MOE_SEL_ROWS + c * MOE_CHUNK
            loc = [jnp.clip(p - r0, -1.0, float(MOE_CHUNK)).astype(bf16) for p in pos]
            hit = (loc[0] == r_local) | (loc[1] == r_local) | (loc[2] == r_local) | (loc[3] == r_local)
            sel_ref[part, c * MOE_CHUNK:(c + 1) * MOE_CHUNK, :] = jnp.where(hit, one, zero)
        stage_ref[slot, part * MOE_SEL_ROWS:(part + 1) * MOE_SEL_ROWS, :] = jnp.dot(
            sel_ref[part], h_ref[...], preferred_element_type=f32).astype(bf16)

    segment_copies(i, slot, lambda cp: cp.start())

    @pl.when(i == nt - 1)
    def _():
        @pl.when(nt >= 2)
        def _():
            _wait_segment_copies(i - 1, lp_ref, stage_ref.at[1 - slot], xs_ref, sems.at[1 - slot])
        _wait_segment_copies(i, lp_ref, stage_ref.at[slot], xs_ref, sems.at[slot])


def _moe_dispatch(h2, post, plan, tm, lmax, n_rows):
    n, d = h2.shape
    nt = n // tm
    grid_spec = pltpu.PrefetchScalarGridSpec(
        num_scalar_prefetch=7,
        grid=(nt,),
        in_specs=[pl.BlockSpec((tm, d), lambda i, *_: (i, 0)),
                  pl.BlockSpec((8, tm), lambda i, *_: (0, i))],
        out_specs=pl.BlockSpec(memory_space=pl.ANY),
        scratch_shapes=[pltpu.VMEM((2, lmax, d), jnp.bfloat16),
                        pltpu.VMEM((lmax // MOE_SEL_ROWS, MOE_SEL_ROWS, tm), jnp.bfloat16),
                        pltpu.VMEM((MOE_PIECE, d), jnp.bfloat16),
                        pltpu.SemaphoreType.DMA((2,)),
                        pltpu.SemaphoreType.DMA(())])
    return pl.pallas_call(
        _dispatch_body,
        grid_spec=grid_spec,
        out_shape=jax.ShapeDtypeStruct((n_rows, d), jnp.bfloat16),
        compiler_params=pltpu.CompilerParams(dimension_semantics=("arbitrary",),
                                             vmem_limit_bytes=VMEM_LIMIT_BYTES),
        name="moe_dispatch",
    )(plan["lstart"], plan["seg_units"], plan["seg"], plan["lp"], plan["tail_start"], plan["tail_units"],
      plan["used"], h2, post)


def _experts_body(first_ref, count_ref, cstart_ref, cvalid_ref, total_ref, used_ref,
                  xs_ref, wgu_ref, bgu_ref, wd_ref, bd_ref, os_ref,
                  wgu_bf, wd_bf, xbuf, obuf, zero_ref, in_sems, out_sems, fill_sem):
    f32, bf16 = jnp.float32, jnp.bfloat16
    e = pl.program_id(0)
    total = total_ref[0]
    half = D_FF // 2

    def in_copy(j):
        src = xs_ref.at[pl.ds(pl.multiple_of(cstart_ref[j], MOE_PIECE), MOE_ROW_CHUNK)]
        return pltpu.make_async_copy(src, xbuf.at[j % 3], in_sems.at[j % 3])

    def out_copy(j):
        n = pl.multiple_of(cvalid_ref[j], MOE_PIECE)
        go = pl.multiple_of(cstart_ref[j], MOE_PIECE)
        return pltpu.make_async_copy(obuf.at[j % 2, pl.ds(0, n)], os_ref.at[pl.ds(go, n)], out_sems.at[j % 2])

    @pl.when(e == 0)
    def _():
        for j0 in range(2):
            @pl.when(j0 < total)
            def _():
                in_copy(j0).start()
        zero_ref[...] = jnp.zeros(zero_ref.shape, bf16)
        _for_each_unused_piece(used_ref, zero_ref, os_ref, fill_sem, lambda cp: cp.start())
        _for_each_unused_piece(used_ref, zero_ref, os_ref, fill_sem, lambda cp: cp.wait())

    @pl.when(count_ref[e] > 0)
    def _():
        wgu_bf[...] = wgu_ref[0].astype(bf16)
        wd_bf[...] = wd_ref[0].astype(bf16)

        def chunk(j, carry):
            in_copy(j).wait()

            @pl.when(j + 2 < total)
            def _():
                in_copy(j + 2).start()

            @pl.when(j >= 2)
            def _():
                out_copy(j - 2).wait()

            def mlp(n_rows):
                x = xbuf[j % 3, 0:n_rows, :]
                out = bd_ref[0]
                for hf in range(2):
                    gate = jnp.dot(x, wgu_bf[:, hf * half:(hf + 1) * half], preferred_element_type=f32)
                    gate = jnp.minimum(gate + bgu_ref[0, :, hf * half:(hf + 1) * half], SWIGLU_LIMIT)
                    up = jnp.dot(x, wgu_bf[:, D_FF + hf * half:D_FF + (hf + 1) * half],
                                 preferred_element_type=f32)
                    up = jnp.clip(up + bgu_ref[0, :, D_FF + hf * half:D_FF + (hf + 1) * half],
                                  -SWIGLU_LIMIT, SWIGLU_LIMIT)
                    act = (up + 1.0) * (gate * jax.nn.sigmoid(SWIGLU_ALPHA * gate))
                    out = out + jnp.dot(act.astype(bf16), wd_bf[hf * half:(hf + 1) * half, :],
                                        preferred_element_type=f32)
                obuf[j % 2, 0:n_rows, :] = out.astype(bf16)

            @pl.when(cvalid_ref[j] > MOE_ROW_CHUNK // 2)
            def _():
                mlp(MOE_ROW_CHUNK)

            @pl.when(cvalid_ref[j] <= MOE_ROW_CHUNK // 2)
            def _():
                mlp(MOE_ROW_CHUNK // 2)
            out_copy(j).start()
            return carry
        lax.fori_loop(first_ref[e], first_ref[e] + count_ref[e], chunk, 0)

    @pl.when(e == pl.num_programs(0) - 1)
    def _():
        @pl.when(total >= 2)
        def _():
            out_copy(total - 2).wait()

        @pl.when(total >= 1)
        def _():
            out_copy(total - 1).wait()


def _expert_chunks(plan, n_rows):
    i32 = jnp.int32
    max_chunks = n_rows // MOE_ROW_CHUNK + N_EXPERTS
    count = (plan["erows"] + (MOE_ROW_CHUNK - 1)) // MOE_ROW_CHUNK
    end = jnp.cumsum(count)
    first = end - count
    j = jnp.arange(max_chunks, dtype=i32)
    mine = ((first[None, :] <= j[:, None]) & (j[:, None] < end[None, :])).astype(i32)
    c = j - jnp.sum(mine * first[None, :], axis=1)
    cstart = jnp.sum(mine * plan["estart"][None, :], axis=1) + jnp.sum(mine, axis=1) * c * MOE_ROW_CHUNK
    cvalid = jnp.sum(mine * jnp.clip(plan["erows"][None, :] - c[:, None] * MOE_ROW_CHUNK, 0, MOE_ROW_CHUNK),
                     axis=1)
    return (first.astype(i32), count.astype(i32), cstart.astype(i32), cvalid.astype(i32),
            end[-1].reshape(1).astype(i32))


def _moe_experts(xs, plan, w_gate_up, b_gate_up, w_down, b_down):
    d = xs.shape[1]
    grid_spec = pltpu.PrefetchScalarGridSpec(
        num_scalar_prefetch=6,
        grid=(N_EXPERTS,),
        in_specs=[pl.BlockSpec(memory_space=pl.ANY),
                  pl.BlockSpec((1, d, 2 * D_FF), lambda e, *_: (e, 0, 0)),
                  pl.BlockSpec((1, 1, 2 * D_FF), lambda e, *_: (e, 0, 0)),
                  pl.BlockSpec((1, D_FF, d), lambda e, *_: (e, 0, 0)),
                  pl.BlockSpec((1, 1, d), lambda e, *_: (e, 0, 0))],
        out_specs=pl.BlockSpec(memory_space=pl.ANY),
        scratch_shapes=[pltpu.VMEM((d, 2 * D_FF), jnp.bfloat16),
                        pltpu.VMEM((D_FF, d), jnp.bfloat16),
                        pltpu.VMEM((3, MOE_ROW_CHUNK, d), jnp.bfloat16),
                        pltpu.VMEM((2, MOE_ROW_CHUNK, d), jnp.bfloat16),
                        pltpu.VMEM((MOE_PIECE, d), jnp.bfloat16),
                        pltpu.SemaphoreType.DMA((3,)),
                        pltpu.SemaphoreType.DMA((2,)),
                        pltpu.SemaphoreType.DMA(())])
    return pl.pallas_call(
        _experts_body,
        grid_spec=grid_spec,
        out_shape=jax.ShapeDtypeStruct(xs.shape, jnp.bfloat16),
        compiler_params=pltpu.CompilerParams(dimension_semantics=("arbitrary",),
                                             vmem_limit_bytes=VMEM_LIMIT_BYTES),
        name="moe_experts",
    )(*_expert_chunks(plan, xs.shape[0]), plan["used"],
      xs, w_gate_up, b_gate_up.reshape(N_EXPERTS, 1, 2 * D_FF), w_down, b_down.reshape(N_EXPERTS, 1, d))


def _combine_body(lstart_ref, units_ref, seg_ref, lp_ref,
                  os_ref, posg_ref, x_ref, fw_ref, yp_ref, ys_ref, stage_ref, w_ref, sems,
                  *, n_prompt_tiles):
    f32, bf16 = jnp.float32, jnp.bfloat16
    i = pl.program_id(0)
    nt = pl.num_programs(0)
    slot = i % 2
    tm = x_ref.shape[0]
    lmax = stage_ref.shape[1]

    def segment_copies(tile, slot_, fn):
        _for_each_segment_copy(tile, lstart_ref, units_ref, seg_ref, stage_ref.at[slot_], os_ref,
                               sems.at[slot_], False, fn)

    @pl.when(i == 0)
    def _():
        stage_ref[...] = jnp.zeros(stage_ref.shape, bf16)
        segment_copies(0, 0, lambda cp: cp.start())

    @pl.when(i + 1 < nt)
    def _():
        segment_copies(i + 1, 1 - slot, lambda cp: cp.start())

    posg = posg_ref[...]
    pos = [posg[:, k:k + 1] for k in range(TOP_K)]
    gate = [posg[:, TOP_K + k:TOP_K + k + 1] for k in range(TOP_K)]
    gate_bf = [g.astype(bf16) for g in gate]
    r_local = lax.broadcasted_iota(jnp.int32, (tm, MOE_CHUNK), 1).astype(f32).astype(bf16)
    _wait_segment_copies(i, lp_ref, stage_ref.at[slot], os_ref, sems.at[slot])
    y = x_ref[...]
    part_cols = lmax // 2
    for part in range(2):
        for c in range(part * part_cols // MOE_CHUNK, (part + 1) * part_cols // MOE_CHUNK):
            w = jnp.zeros((tm, MOE_CHUNK), bf16)
            for k in range(TOP_K):
                loc = jnp.clip(pos[k] - c * MOE_CHUNK, -1.0, float(MOE_CHUNK)).astype(bf16)
                w = jnp.where(loc == r_local, gate_bf[k], w)
            w_ref[:, c * MOE_CHUNK:(c + 1) * MOE_CHUNK] = w
        y = y + jnp.dot(w_ref[:, part * part_cols:(part + 1) * part_cols],
                        stage_ref[slot, part * part_cols:(part + 1) * part_cols, :],
                        preferred_element_type=f32)
    out = y * lax.rsqrt(jnp.mean(y * y, axis=-1, keepdims=True) + EPS) * fw_ref[...]

    @pl.when(i < n_prompt_tiles)
    def _():
        yp_ref[...] = out

    @pl.when(i >= n_prompt_tiles)
    def _():
        ys_ref[...] = out


def _moe_combine(os_, posg, x1, final_norm_w, plan, tm, lmax, n_prompt):
    n, d = x1.shape
    nt = n // tm
    n_prompt_tiles = n_prompt // tm
    n_sample_tiles = nt - n_prompt_tiles
    grid_spec = pltpu.PrefetchScalarGridSpec(
        num_scalar_prefetch=4,
        grid=(nt,),
        in_specs=[pl.BlockSpec(memory_space=pl.ANY),
                  pl.BlockSpec((tm, LANES), lambda i, *_: (i, 0)),
                  pl.BlockSpec((tm, d), lambda i, *_: (i, 0)),
                  pl.BlockSpec((1, d), lambda i, *_: (0, 0))],
        out_specs=[pl.BlockSpec((tm, d), lambda i, *_: (jnp.minimum(i, n_prompt_tiles - 1), 0)),
                   pl.BlockSpec((tm, d), lambda i, *_: (jnp.maximum(i - n_prompt_tiles, 0), 0))],
        scratch_shapes=[pltpu.VMEM((2, lmax, d), jnp.bfloat16),
                        pltpu.VMEM((tm, lmax), jnp.bfloat16),
                        pltpu.SemaphoreType.DMA((2,))])
    return pl.pallas_call(
        functools.partial(_combine_body, n_prompt_tiles=n_prompt_tiles),
        grid_spec=grid_spec,
        out_shape=[jax.ShapeDtypeStruct((n_prompt, d), jnp.float32),
                   jax.ShapeDtypeStruct((n_sample_tiles * tm, d), jnp.float32)],
        compiler_params=pltpu.CompilerParams(dimension_semantics=("arbitrary",),
                                             vmem_limit_bytes=VMEM_LIMIT_BYTES),
        name="moe_combine",
    )(plan["lstart"], plan["seg_units"], plan["seg"], plan["lp"],
      os_, posg, x1, final_norm_w.reshape(1, d).astype(jnp.float32))


def _moe_block(ma, mb, w_out, xa, xb, norm2_w, router_w, router_b, w_gate_up, b_gate_up, w_down, b_down,
               final_norm_w, tm=MOE_TOKEN_TILE):
    n_prompt = xa.shape[0]
    n = n_prompt + xb.shape[0]
    nt, lmax, n_rows = _moe_sizes(n, tm)
    x1, h2, posg, post, cnt3 = _out_proj_router(ma, mb, w_out, xa, xb, norm2_w, router_w, router_b, tm)
    plan = _moe_plan(cnt3[:, 0, :N_EXPERTS])
    xs = _moe_dispatch(h2, post, plan, tm, lmax, n_rows)
    os_ = _moe_experts(xs, plan, w_gate_up, b_gate_up, w_down, b_down)
    return _moe_combine(os_, posg, x1, final_norm_w, plan, tm, lmax, n_prompt)


def kernel(x_prompt, x_sample, state_ssm, state_conv, state_pool, norm1_w, w_in, conv_w, conv_b, dt_bias,
           A_log, D_skip, ssd_norm_w, pool_w, pool_scale, w_out, norm2_w, router_w, router_b, w_gate_up,
           b_gate_up, w_down, b_down, final_norm_w):
    n_prompt = BATCH * SEQ
    n_sample = DEC_BATCH * DEC_SEQ
    xp = x_prompt.reshape(n_prompt, D_MODEL)
    xs = x_sample.reshape(n_sample, D_MODEL)
    mp = (conv_w[0], conv_b[0], dt_bias[0], A_log[0], D_skip[0], ssd_norm_w[0], pool_w[0], pool_scale[0])
    mix_p, s1, ctail, ptail = _prompt_fused(xp, BATCH, SEQ, norm1_w[0], w_in, *mp)
    half = n_sample // 2
    z, xbc, dt_raw, u = _in_proj(xs[:half], xs[half:], norm1_w[0], w_in, tm=half)
    mix_s, s2 = _sample_mixer(z, xbc, dt_raw, u, 0, DEC_BATCH, state_conv, state_ssm[0], state_pool,
                              PAST_LEN, *mp)
    nk = CONV_WIDTH - 1
    c1 = ctail[:, CONV_TAIL_ROWS - nk:]
    p1 = ptail[:, HIST_ROWS - POOL_HIST:]
    c2 = xbc.reshape(DEC_BATCH, DEC_SEQ, D_CONV)[:, DEC_SEQ - nk:]
    p2 = jnp.concatenate([state_pool[0][:, DEC_SEQ:], u.reshape(DEC_BATCH, DEC_SEQ, D_POOL)], axis=1)
    yp, ys = _moe_block(mix_p, mix_s, w_out[0], xp, xs, norm2_w[0], router_w[0], router_b[0], w_gate_up[0],
                        b_gate_up[0], w_down[0], b_down[0], final_norm_w)
    return (yp.reshape(x_prompt.shape), ys.reshape(x_sample.shape),
            s1[None], c1[None], p1[None], s2[None], c2[None], p2[None])
```

```python
import functools
import jax, jax.numpy as jnp
from jax import lax
import numpy as np
from jax.experimental import pallas as pl
from jax.experimental.pallas import tpu as pltpu

D_MODEL = 1024
BATCH = 8
SEQ = 2048
DEC_BATCH = 128
DEC_SEQ = 4
PAST_LEN = 16384

D_MIX = 2 * D_MODEL
D_SSD = 3 * D_MIX // 4
SSD_HEAD_DIM = 64
N_SSD_HEADS = D_SSD // SSD_HEAD_DIM
N_SSD_GROUPS = 4
D_STATE = 128
CONV_WIDTH = 4
SSD_CHUNK = 128
D_CONV = D_SSD + 2 * N_SSD_GROUPS * D_STATE
D_POOL = D_MIX - D_SSD
POOL_WINDOWS = (2, 4, 8, 16)
N_POOL_GROUPS = len(POOL_WINDOWS)
POOL_GROUP_DIM = D_POOL // N_POOL_GROUPS
POOL_HIST = max(POOL_WINDOWS) - 1
D_IN_PROJ = D_SSD + D_CONV + N_SSD_HEADS + D_POOL
N_EXPERTS = 32
TOP_K = 4
D_FF = D_MODEL
SWIGLU_LIMIT = 7.0
SWIGLU_ALPHA = 1.702
EPS = 1e-5

LANES = 128
BF16_SUBLANES = 16
VMEM_LIMIT_BYTES = 48 * 1024 * 1024

MOE_TOKEN_TILE = 512
MOE_SEG_ROWS = BF16_SUBLANES
MOE_PIECE = 128
MOE_ROW_CHUNK = 512
MOE_CHUNK = 256
MOE_SEL_ROWS = 512


BLK = SSD_CHUNK
PROJ_ROW_TILE = 512
PROMPT_BLKS_PER_STEP = 2
HIST_ROWS = 16
CONV_TAIL_ROWS = 8
NT_DIMS = (((1,), (1,)), ((), ()))


def _split2(v):
    hi = v.astype(jnp.bfloat16)
    lo = (v - hi.astype(jnp.float32)).astype(jnp.bfloat16)
    return hi, lo


def _dot_sel_left(sel, v, passes):
    out = None
    rem = v
    for p in range(passes):
        part = rem.astype(jnp.bfloat16)
        d = jnp.dot(sel, part, preferred_element_type=jnp.float32)
        out = d if out is None else out + d
        if p + 1 < passes:
            rem = rem - part.astype(jnp.float32)
    return out


def _dot_sel_right(v, sel, passes):
    out = None
    rem = v
    for p in range(passes):
        part = rem.astype(jnp.bfloat16)
        d = jnp.dot(part, sel, preferred_element_type=jnp.float32)
        out = d if out is None else out + d
        if p + 1 < passes:
            rem = rem - part.astype(jnp.float32)
    return out


def _two_part_specs(n_first, n_second, tm, width):
    t1 = n_first // tm
    t2 = n_second // tm
    return (pl.BlockSpec((tm, width), lambda i: (jnp.minimum(i, t1 - 1), 0)),
            pl.BlockSpec((tm, width), lambda i: (jnp.clip(i - t1, 0, t2 - 1), 0)))


IN_PROJ_COLS = 512


def _in_proj_body(xa_ref, xb_ref, nw_ref, wmain_ref, wtail_ref, z_ref, xbc_ref, dt_ref, u_ref, *, tiles_a):
    f32, bf16 = jnp.float32, jnp.bfloat16
    x = jnp.where(pl.program_id(0) < tiles_a, xa_ref[...], xb_ref[...])
    h = (x * lax.rsqrt(jnp.mean(x * x, axis=-1, keepdims=True) + EPS) * nw_ref[...]).astype(bf16)
    off = 0
    for ref in (z_ref, xbc_ref):
        for c0 in range(0, ref.shape[1], IN_PROJ_COLS):
            w = wmain_ref[0, :, off + c0:off + c0 + IN_PROJ_COLS].astype(bf16)
            ref[:, c0:c0 + IN_PROJ_COLS] = jnp.dot(h, w, preferred_element_type=f32)
        off += ref.shape[1]
    tail = jnp.dot(h, wtail_ref[0, :, 0:N_SSD_HEADS + D_POOL].astype(bf16), preferred_element_type=f32)
    lane = lax.broadcasted_iota(jnp.int32, (tail.shape[0], LANES), 1)
    dt_ref[...] = jnp.where(lane < N_SSD_HEADS, tail[:, 0:LANES], 0.0)
    u_ref[...] = tail[:, N_SSD_HEADS:N_SSD_HEADS + D_POOL]


def _in_proj(xa, xb, norm1_w, w_in3, tm=PROJ_ROW_TILE):
    d = xa.shape[1]
    n = xa.shape[0] + xb.shape[0]
    f32, bf16 = jnp.float32, jnp.bfloat16
    s1 = D_SSD + D_CONV
    tail_blk = 1024
    assert s1 % tail_blk == 0 and N_SSD_HEADS + D_POOL <= tail_blk
    widths = (D_SSD, D_CONV, LANES, D_POOL)
    return pl.pallas_call(
        functools.partial(_in_proj_body, tiles_a=xa.shape[0] // tm),
        grid=(n // tm,),
        in_specs=[*_two_part_specs(xa.shape[0], xb.shape[0], tm, d),
                  pl.BlockSpec((1, d), lambda i: (0, 0)),
                  pl.BlockSpec((1, d, s1), lambda i: (0, 0, 0), pipeline_mode=pl.Buffered(1)),
                  pl.BlockSpec((1, d, tail_blk), lambda i: (0, 0, s1 // tail_blk), pipeline_mode=pl.Buffered(1))],
        out_specs=[pl.BlockSpec((tm, wd), lambda i: (i, 0)) for wd in widths],
        out_shape=[jax.ShapeDtypeStruct((n, wd), f32) for wd in widths],
        compiler_params=pltpu.CompilerParams(dimension_semantics=("parallel",),
                                             vmem_limit_bytes=VMEM_LIMIT_BYTES),
        name="in_proj",
    )(xa, xb, norm1_w.reshape(1, d).astype(f32), w_in3, w_in3)


def _mixer_constants():
    bf16 = jnp.bfloat16
    h = np.arange(LANES)[:, None]
    ch = np.arange(D_SSD)[None, :]
    expand = (ch // SSD_HEAD_DIM == h).astype(np.float32)
    i = np.arange(BLK)[:, None]
    j = np.arange(BLK)[None, :]
    causal = (j <= i).astype(np.float32)
    return dict(expand=jnp.asarray(expand, bf16), expand_t=jnp.asarray(expand.T, bf16),
                causal=jnp.asarray(causal, bf16))


def _softplus(x):
    return jnp.maximum(x, 0.0) + jnp.log(1.0 + jnp.exp(-jnp.abs(x)))


def _conv_silu(ext_ref, cw_ref, cb_ref, first_row):
    ext = ext_ref[...]
    last = first_row + CONV_WIDTH - 1
    acc = cb_ref[...] + cw_ref[CONV_WIDTH - 1:CONV_WIDTH, :] * ext[last:last + BLK, :]
    for k in range(CONV_WIDTH - 1):
        tap = pltpu.roll(ext, CONV_WIDTH - 1 - k, axis=0)[last:last + BLK, :]
        acc = acc + cw_ref[k:k + 1, :] * tap
    return acc * jax.nn.sigmoid(acc)


def _ssd_intra(xbc_c, dt_raw, dtb_ref, alog_ref, causal_bf, expand_ref):
    f32 = jnp.float32
    xs = xbc_c[:, :D_SSD]
    bm = xbc_c[:, D_SSD:D_SSD + N_SSD_GROUPS * D_STATE]
    cm = xbc_c[:, D_SSD + N_SSD_GROUPS * D_STATE:]
    dt = _softplus(dt_raw + dtb_ref[...])
    a = dt * (-jnp.exp(alog_ref[...]))
    a_cum = _dot_sel_left(causal_bf, a, 3)
    dt_x = _dot_sel_right(dt, expand_ref[...], 2)
    return xs, bm, cm, dt, a_cum, xs * dt_x


def _ssd_diag_group(g, cb, a_cum, a_cum_t, keep, xdt):
    f32, bf16 = jnp.float32, jnp.bfloat16
    hg = N_SSD_HEADS // N_SSD_GROUPS
    lane = lax.broadcasted_iota(jnp.int32, (BLK, LANES), 1)
    first_head = lane < SSD_HEAD_DIM
    neg = jnp.float32(-jnp.inf)
    outs = []
    for pr in range(hg * SSD_HEAD_DIM // LANES):
        h1 = g * hg + 2 * pr
        blk = (g * hg * SSD_HEAD_DIM) // LANES + pr
        xp = xdt[:, blk * LANES:(blk + 1) * LANES]
        x1 = jnp.where(first_head, xp, 0.0).astype(bf16)
        x2 = jnp.where(first_head, 0.0, xp).astype(bf16)
        m1 = (cb * jnp.exp(jnp.where(keep, a_cum[:, h1:h1 + 1] - a_cum_t[h1:h1 + 1, :], neg))).astype(bf16)
        m2 = (cb * jnp.exp(jnp.where(keep, a_cum[:, h1 + 1:h1 + 2] - a_cum_t[h1 + 1:h1 + 2, :], neg))).astype(bf16)
        outs.append(jnp.dot(m1, x1, preferred_element_type=f32) + jnp.dot(m2, x2, preferred_element_type=f32))
    return jnp.concatenate(outs, axis=1)


def _gated_norm(y, z, nw_ref):
    yg = y * (z * jax.nn.sigmoid(z))
    return yg * lax.rsqrt(jnp.mean(yg * yg, axis=-1, keepdims=True) + EPS) * nw_ref[...]


def _reset_history(ext_ref, pool_tail_ref, state_ref):
    ext_ref[0:CONV_TAIL_ROWS, :] = jnp.zeros((CONV_TAIL_ROWS, D_CONV), jnp.float32)
    pool_tail_ref[...] = jnp.zeros(pool_tail_ref.shape, jnp.float32)
    state_ref[...] = jnp.zeros(state_ref.shape, jnp.float32)


def _prompt_block(c, z_ref, xbc_ref, dt_ref, u_ref, cw_ref, cb_ref, dtb_ref, alog_ref, dskip_ref, nw_ref,
                  pw_ref, ps_ref, causal_ref, mix_ref, ext_ref, pool_tail_ref, state_ref, side_work):
    f32, bf16 = jnp.float32, jnp.bfloat16
    gw = D_SSD // N_SSD_GROUPS

    def do_side_work():
        piece = next(side_work, None)
        if piece is not None:
            piece()

    u = u_ref[...]
    ext_u = jnp.concatenate([pool_tail_ref[...], u], axis=0)
    pos = (c * BLK + lax.broadcasted_iota(jnp.int32, (BLK, 1), 0) + 1).astype(f32)
    for gi, w in enumerate(POOL_WINDOWS):
        assert w & (w - 1) == 0 and w <= HIST_ROWS
        sl = slice(gi * POOL_GROUP_DIM, (gi + 1) * POOL_GROUP_DIM)
        ug = u[:, sl]
        acc = ext_u[:, sl]
        span = 1
        while span < w:
            acc = acc + pltpu.roll(acc, span, axis=0)
            span *= 2
        wsum = acc[HIST_ROWS:, :]
        pooled = wsum / jnp.minimum(pos, jnp.float32(w)) - ug
        po = jnp.dot(pooled.astype(bf16), pw_ref[gi], preferred_element_type=f32) * ps_ref[:, sl]
        mix_ref[:, D_SSD + gi * POOL_GROUP_DIM:D_SSD + (gi + 1) * POOL_GROUP_DIM] = po.astype(bf16)
    pool_tail_ref[...] = u_ref[BLK - HIST_ROWS:BLK, :]
    do_side_work()

    ext_ref[CONV_TAIL_ROWS:CONV_TAIL_ROWS + BLK, :] = xbc_ref[...]
    xbc_c = _conv_silu(ext_ref, cw_ref, cb_ref, CONV_TAIL_ROWS - (CONV_WIDTH - 1))
    ext_ref[0:CONV_TAIL_ROWS, :] = xbc_ref[BLK - CONV_TAIL_ROWS:BLK, :]
    do_side_work()

    causal_bf = causal_ref[...]
    keep = causal_bf > 0
    xs = xbc_c[:, :D_SSD]
    bm = xbc_c[:, D_SSD:D_SSD + N_SSD_GROUPS * D_STATE]
    cm = xbc_c[:, D_SSD + N_SSD_GROUPS * D_STATE:]
    dt = _softplus(dt_ref[...] + dtb_ref[...])
    a_cum = dt * (-jnp.exp(alog_ref[...]))
    row = lax.broadcasted_iota(jnp.int32, (BLK, LANES), 0)
    span = 1
    while span < BLK:
        a_cum = a_cum + jnp.where(row >= span, pltpu.roll(a_cum, span, axis=0), 0.0)
        span *= 2
    a_cum_t = jnp.transpose(a_cum)
    a_tot = a_cum[BLK - 1:BLK, :]
    ea = jnp.exp(a_cum)
    dte = jnp.exp(a_tot - a_cum)
    cd = jnp.exp(a_tot)
    hg = N_SSD_HEADS // N_SSD_GROUPS
    first_head = lax.broadcasted_iota(jnp.int32, (BLK, LANES), 1) < SSD_HEAD_DIM
    neg = jnp.float32(-jnp.inf)

    def head_cols(v, h1):
        return jnp.where(first_head, v[:, h1:h1 + 1], v[:, h1 + 1:h1 + 2])

    def decay_from(h):
        return jnp.exp(jnp.where(keep, a_cum[:, h:h + 1] - a_cum_t[h:h + 1, :], neg))

    y_parts = []
    for g in range(N_SSD_GROUPS):
        do_side_work()
        cg = cm[:, g * D_STATE:(g + 1) * D_STATE].astype(bf16)
        bg = bm[:, g * D_STATE:(g + 1) * D_STATE].astype(bf16)
        cb = lax.dot_general(cg, bg, NT_DIMS, preferred_element_type=f32)
        sg = state_ref[g * gw:(g + 1) * gw, :]
        y_off = lax.dot_general(cg, sg.astype(bf16), NT_DIMS, preferred_element_type=f32)
        xdte_parts = []
        for pr in range(gw // LANES):
            h1 = g * hg + 2 * pr
            sl = slice(h1 * SSD_HEAD_DIM, h1 * SSD_HEAD_DIM + LANES)
            xp = xs[:, sl] * head_cols(dt, h1)
            x1 = jnp.where(first_head, xp, 0.0).astype(bf16)
            x2 = jnp.where(first_head, 0.0, xp).astype(bf16)
            y_diag = (jnp.dot((cb * decay_from(h1)).astype(bf16), x1, preferred_element_type=f32)
                      + jnp.dot((cb * decay_from(h1 + 1)).astype(bf16), x2, preferred_element_type=f32))
            y_parts.append(y_diag + y_off[:, pr * LANES:(pr + 1) * LANES] * head_cols(ea, h1)
                           + xs[:, sl] * dskip_ref[:, sl])
            xdte_parts.append(xp * head_cols(dte, h1))
        xdte_t = jnp.transpose(jnp.concatenate(xdte_parts, axis=1)).astype(bf16)
        cd_rows = jnp.concatenate([jnp.broadcast_to(cd[:, h:h + 1], (SSD_HEAD_DIM, D_STATE))
                                   for h in range(g * hg, (g + 1) * hg)], axis=0)
        state_ref[g * gw:(g + 1) * gw, :] = sg * cd_rows + jnp.dot(xdte_t, bg, preferred_element_type=f32)
    y = jnp.concatenate(y_parts, axis=1)
    mix_ref[:, 0:D_SSD] = _gated_norm(y, z_ref[...], nw_ref).astype(bf16)


W_SLAB_COLS = 512
PROJ_PIECE_COLS = 512


def _prompt_fused_body(x0_ref, xn_ref, n1w_ref, w_hbm, wtail_ref,
                       cw_ref, cb_ref, dtb_ref, alog_ref, dskip_ref, nw_ref, pw_ref, ps_ref, causal_ref,
                       mix_ref, ssm_ref, ctail_ref, ptail_ref,
                       wbf_ref, wtail_bf_ref, wstage_ref, z_s, xbc_s, dt_s, u_s, ext_ref, pool_tail_ref,
                       state_ref, wsem, *, steps_per_seq):
    f32, bf16 = jnp.float32, jnp.bfloat16
    s = pl.program_id(0)
    step_rows = x0_ref.shape[0]
    n_main = wbf_ref.shape[1]
    n_tail = N_SSD_HEADS + D_POOL

    def projection_pieces(x_ref, slot):
        x = x_ref[...]
        h = (x * lax.rsqrt(jnp.mean(x * x, axis=-1, keepdims=True) + EPS) * n1w_ref[...]).astype(bf16)

        def slab(dst, c0, w0):
            def piece():
                dst[slot, :, c0:c0 + PROJ_PIECE_COLS] = jnp.dot(h, wbf_ref[:, w0:w0 + PROJ_PIECE_COLS],
                                                                preferred_element_type=f32)
            return piece

        def tail_piece():
            tail = jnp.dot(h, wtail_bf_ref[:, 0:n_tail], preferred_element_type=f32)
            lane = lax.broadcasted_iota(jnp.int32, (step_rows, LANES), 1)
            dt_s[slot] = jnp.where(lane < N_SSD_HEADS, tail[:, 0:LANES], 0.0)
            u_s[slot] = tail[:, N_SSD_HEADS:n_tail]

        pieces = [slab(z_s, c0, c0) for c0 in range(0, D_SSD, PROJ_PIECE_COLS)]
        pieces += [slab(xbc_s, c0, D_SSD + c0) for c0 in range(0, D_CONV, PROJ_PIECE_COLS)]
        return pieces + [tail_piece]

    def project(x_ref, slot):
        for piece in projection_pieces(x_ref, slot):
            piece()

    @pl.when(s == 0)
    def _():
        for k in range(n_main // W_SLAB_COLS):
            cp = pltpu.make_async_copy(w_hbm.at[0, :, pl.ds(k * W_SLAB_COLS, W_SLAB_COLS)], wstage_ref, wsem)
            cp.start()
            cp.wait()
            wbf_ref[:, k * W_SLAB_COLS:(k + 1) * W_SLAB_COLS] = wstage_ref[...].astype(bf16)
        wtail_bf_ref[...] = wtail_ref[0, :, 0:wtail_bf_ref.shape[1]].astype(bf16)
        project(x0_ref, 0)

    @pl.when(s % steps_per_seq == 0)
    def _():
        _reset_history(ext_ref, pool_tail_ref, state_ref)

    pieces = iter(projection_pieces(xn_ref, (s + 1) % 2))
    slot = s % 2
    c0 = (s % steps_per_seq) * PROMPT_BLKS_PER_STEP
    for ci in range(PROMPT_BLKS_PER_STEP):
        rows = pl.ds(ci * BLK, BLK)
        _prompt_block(c0 + ci,
                      z_s.at[slot, rows], xbc_s.at[slot, rows], dt_s.at[slot, rows], u_s.at[slot, rows],
                      cw_ref, cb_ref, dtb_ref, alog_ref, dskip_ref, nw_ref, pw_ref, ps_ref, causal_ref,
                      mix_ref.at[rows], ext_ref, pool_tail_ref, state_ref, pieces)
    for piece in pieces:
        piece()

    @pl.when(s % steps_per_seq == steps_per_seq - 1)
    def _():
        ssm_ref[0] = state_ref[...].reshape(N_SSD_HEADS, SSD_HEAD_DIM, D_STATE)
        ctail_ref[0] = xbc_s[slot, step_rows - CONV_TAIL_ROWS:step_rows, :]
        ptail_ref[0] = u_s[slot, step_rows - HIST_ROWS:step_rows, :]


def _prompt_fused(xp, n_seq, seq_len, norm1_w, w_in3, conv_w, conv_b, dt_bias, A_log, D_skip, ssd_norm_w,
                  pool_w, pool_scale):
    f32, bf16 = jnp.float32, jnp.bfloat16
    n, d = xp.shape
    step_rows = PROMPT_BLKS_PER_STEP * BLK
    steps_per_seq = seq_len // step_rows
    n_steps = n // step_rows
    s1 = D_SSD + D_CONV
    tail_blk = 1024
    tail_bf_cols = -(-(N_SSD_HEADS + D_POOL) // LANES) * LANES
    assert s1 % tail_blk == 0 and tail_bf_cols <= tail_blk and s1 % W_SLAB_COLS == 0
    k = _mixer_constants()

    def const(shape):
        return pl.BlockSpec(shape, lambda s: (0,) * len(shape))

    def per_seq(shape):
        return pl.BlockSpec(shape, lambda s: (s // steps_per_seq,) + (0,) * (len(shape) - 1))

    pad_h = (0, LANES - N_SSD_HEADS)
    return pl.pallas_call(
        functools.partial(_prompt_fused_body, steps_per_seq=steps_per_seq),
        grid=(n_steps,),
        in_specs=[pl.BlockSpec((step_rows, d), lambda s: (0, 0)),
                  pl.BlockSpec((step_rows, d), lambda s: (jnp.minimum(s + 1, n_steps - 1), 0)),
                  const((1, d)),
                  pl.BlockSpec(memory_space=pl.ANY),
                  pl.BlockSpec((1, d, tail_blk), lambda s: (0, 0, s1 // tail_blk), pipeline_mode=pl.Buffered(1)),
                  const((CONV_WIDTH, D_CONV)), const((1, D_CONV)), const((1, LANES)), const((1, LANES)),
                  const((1, D_SSD)), const((1, D_SSD)),
                  const((N_POOL_GROUPS, POOL_GROUP_DIM, POOL_GROUP_DIM)), const((1, D_POOL)),
                  const((BLK, BLK))],
        out_specs=[pl.BlockSpec((step_rows, D_MIX), lambda s: (s, 0)),
                   per_seq((1, N_SSD_HEADS, SSD_HEAD_DIM, D_STATE)),
                   per_seq((1, CONV_TAIL_ROWS, D_CONV)),
                   per_seq((1, HIST_ROWS, D_POOL))],
        out_shape=[jax.ShapeDtypeStruct((n, D_MIX), bf16),
                   jax.ShapeDtypeStruct((n_seq, N_SSD_HEADS, SSD_HEAD_DIM, D_STATE), f32),
                   jax.ShapeDtypeStruct((n_seq, CONV_TAIL_ROWS, D_CONV), f32),
                   jax.ShapeDtypeStruct((n_seq, HIST_ROWS, D_POOL), f32)],
        scratch_shapes=[pltpu.VMEM((d, s1), bf16),
                        pltpu.VMEM((d, tail_bf_cols), bf16),
                        pltpu.VMEM((d, W_SLAB_COLS), f32),
                        pltpu.VMEM((2, step_rows, D_SSD), f32),
                        pltpu.VMEM((2, step_rows, D_CONV), f32),
                        pltpu.VMEM((2, step_rows, LANES), f32),
                        pltpu.VMEM((2, step_rows, D_POOL), f32),
                        pltpu.VMEM((CONV_TAIL_ROWS + BLK, D_CONV), f32),
                        pltpu.VMEM((HIST_ROWS, D_POOL), f32),
                        pltpu.VMEM((D_SSD, D_STATE), f32),
                        pltpu.SemaphoreType.DMA(())],
        compiler_params=pltpu.CompilerParams(dimension_semantics=("arbitrary",),
                                             vmem_limit_bytes=VMEM_LIMIT_BYTES),
        name="prompt_fused",
    )(xp, xp, norm1_w.reshape(1, d).astype(f32), w_in3, w_in3,
      conv_w.astype(f32), conv_b.reshape(1, D_CONV).astype(f32),
      jnp.pad(dt_bias.astype(f32), pad_h).reshape(1, LANES), jnp.pad(A_log.astype(f32), pad_h).reshape(1, LANES),
      jnp.repeat(D_skip.astype(f32), SSD_HEAD_DIM).reshape(1, D_SSD), ssd_norm_w.reshape(1, D_SSD).astype(f32),
      pool_w.astype(bf16), pool_scale.reshape(1, D_POOL).astype(f32), k["causal"])


SEQ_PER_BLK = BLK // DEC_SEQ
SEQ_PER_STEP = 8


def _sample_constants():
    bf16 = jnp.bfloat16
    r = np.arange(BLK)
    sq, st = r // DEC_SEQ, r % DEC_SEQ
    same = sq[:, None] == sq[None, :]
    causal = same & (st[None, :] <= st[:, None])
    nk = CONV_WIDTH - 1
    shift = np.stack([same & (st[None, :] == st[:, None] + k - nk) for k in range(nk)])
    cs = np.arange(SEQ_PER_BLK * nk)
    stsel = np.stack([(cs[None, :] // nk == sq[:, None]) & (cs[None, :] % nk == st[:, None] + k)
                      for k in range(nk)])
    pcur = np.stack([causal & (st[:, None] - st[None, :] < w) for w in POOL_WINDOWS])
    hs = np.arange(SEQ_PER_BLK * POOL_HIST)
    phist = np.stack([(hs[None, :] // POOL_HIST == sq[:, None])
                      & (st[:, None] + POOL_HIST - hs[None, :] % POOL_HIST < w) for w in POOL_WINDOWS])
    as_bf = lambda a: jnp.asarray(a.astype(np.float32), bf16)
    return dict(same=as_bf(same), causal=as_bf(causal), shift=as_bf(shift), stsel=as_bf(stsel),
                pcur=as_bf(pcur), phist=as_bf(phist))


def _sample_mixer_body(z_ref, xbc_ref, dt_ref, u_ref, cst_ref, pst_ref, ssm_in_ref,
                       cw_ref, cb_ref, dtb_ref, alog_ref, dskip_ref, nw_ref, pw_ref, ps_ref,
                       causal_ref, same_ref, expand_ref, expand_t_ref, shift_ref, stsel_ref, pcur_ref, phist_ref,
                       mix_ref, ssm_out_ref,
                       ydiag_ref, ea_ref, yt_ref, cdh_ref, cdl_ref, xdte_t_ref, bm_ref, cm_ref, *, pos0):
    f32, bf16 = jnp.float32, jnp.bfloat16
    s = pl.program_id(1)
    gw = D_SSD // N_SSD_GROUPS

    @pl.when(s == 0)
    def _():
        xbc = xbc_ref[...]
        cst = cst_ref[...]
        acc = cb_ref[...] + cw_ref[CONV_WIDTH - 1:CONV_WIDTH, :] * xbc
        for k in range(CONV_WIDTH - 1):
            tap = _dot_sel_left(shift_ref[k], xbc, 3) + _dot_sel_left(stsel_ref[k], cst, 3)
            acc = acc + cw_ref[k:k + 1, :] * tap
        xbc_c = acc * jax.nn.sigmoid(acc)

        causal_bf = causal_ref[...]
        keep = causal_bf > 0
        xs, bm, cm, dt, a_cum, xdt = _ssd_intra(xbc_c, dt_ref[...], dtb_ref, alog_ref, causal_bf, expand_ref)
        a_tot = _dot_sel_left(same_ref[...], dt * (-jnp.exp(alog_ref[...])), 3)
        a_cum_t = jnp.transpose(a_cum)
        ea_ref[...] = _dot_sel_right(jnp.exp(a_cum), expand_ref[...], 2)
        dte_x = _dot_sel_right(jnp.exp(a_tot - a_cum), expand_ref[...], 2)
        cd_col = _dot_sel_left(expand_t_ref[...], jnp.exp(jnp.transpose(a_tot)), 2)
        cd_hi, cd_lo = _split2(cd_col)
        cdh_ref[...] = cd_hi
        cdl_ref[...] = cd_lo
        bm_ref[...] = bm.astype(bf16)
        cm_ref[...] = cm.astype(bf16)
        for g in range(N_SSD_GROUPS):
            cg = cm[:, g * D_STATE:(g + 1) * D_STATE].astype(bf16)
            bg = bm[:, g * D_STATE:(g + 1) * D_STATE].astype(bf16)
            cb = lax.dot_general(cg, bg, NT_DIMS, preferred_element_type=f32)
            y_diag = _ssd_diag_group(g, cb, a_cum, a_cum_t, keep, xdt)
            ydiag_ref[:, g * gw:(g + 1) * gw] = y_diag + xs[:, g * gw:(g + 1) * gw] * dskip_ref[:, g * gw:(g + 1) * gw]
            xdte_t_ref[g * gw:(g + 1) * gw, :] = jnp.transpose(
                xdt[:, g * gw:(g + 1) * gw] * dte_x[:, g * gw:(g + 1) * gw]).astype(bf16)
        yt_ref[...] = jnp.zeros(yt_ref.shape, f32)

        u = u_ref[...]
        pst = pst_ref[...]
        step = lax.broadcasted_iota(jnp.int32, (BLK, 1), 0) % DEC_SEQ
        pos = (step + (pos0 + 1)).astype(f32)
        for gi, w in enumerate(POOL_WINDOWS):
            sl = slice(gi * POOL_GROUP_DIM, (gi + 1) * POOL_GROUP_DIM)
            ug = u[:, sl]
            wsum = _dot_sel_left(pcur_ref[gi], ug, 2) + _dot_sel_left(phist_ref[gi], pst[:, sl], 2)
            pooled = wsum / jnp.minimum(pos, jnp.float32(w)) - ug
            po = jnp.dot(pooled.astype(bf16), pw_ref[gi], preferred_element_type=f32) * ps_ref[:, sl]
            mix_ref[:, D_SSD + gi * POOL_GROUP_DIM:D_SSD + (gi + 1) * POOL_GROUP_DIM] = po.astype(bf16)

    hg = N_SSD_HEADS // N_SSD_GROUPS
    row_seq = lax.broadcasted_iota(jnp.int32, (BLK, LANES), 0) // DEC_SEQ
    col_seq = lax.broadcasted_iota(jnp.int32, (gw, BLK), 1) // DEC_SEQ
    row_idx = lax.broadcasted_iota(jnp.int32, (BLK, LANES), 0)
    for q in range(SEQ_PER_STEP):
        sq = s * SEQ_PER_STEP + q
        rows_of_s = row_seq == sq
        cols_of_s = col_seq == sq
        pick_s = jnp.where(row_idx == DEC_SEQ * sq, 1.0, 0.0).astype(bf16)
        state = ssm_in_ref[q].reshape(D_SSD, D_STATE)
        for g in range(N_SSD_GROUPS):
            rs = slice(g * gw, (g + 1) * gw)
            sg = state[rs, :]
            cg = cm_ref[:, g * D_STATE:(g + 1) * D_STATE]
            bg = bm_ref[:, g * D_STATE:(g + 1) * D_STATE]
            yt = lax.dot_general(sg.astype(bf16), cg, NT_DIMS, preferred_element_type=f32)
            yt_ref[rs, :] += jnp.where(cols_of_s, yt, 0.0)
            cd = (jnp.dot(cdh_ref[rs, :], pick_s, preferred_element_type=f32)
                  + jnp.dot(cdl_ref[rs, :], pick_s, preferred_element_type=f32))
            upd = jnp.dot(xdte_t_ref[rs, :], jnp.where(rows_of_s, bg, jnp.zeros_like(bg)),
                          preferred_element_type=f32)
            ssm_out_ref[q, g * hg:(g + 1) * hg] = (sg * cd + upd).reshape(hg, SSD_HEAD_DIM, D_STATE)

    @pl.when(s == pl.num_programs(1) - 1)
    def _():
        y = ydiag_ref[...] + jnp.transpose(yt_ref[...]) * ea_ref[...]
        mix_ref[:, 0:D_SSD] = _gated_norm(y, z_ref[...], nw_ref).astype(bf16)


def _sample_mixer(z, xbc, dt, u, row0, n_seq, state_conv, state_ssm, state_pool, pos0,
                  conv_w, conv_b, dt_bias, A_log, D_skip, ssd_norm_w, pool_w, pool_scale):
    f32, bf16 = jnp.float32, jnp.bfloat16
    n_blk = n_seq // SEQ_PER_BLK
    blk0 = row0 // BLK
    nk = CONV_WIDTH - 1
    k = _mixer_constants()
    ks = _sample_constants()

    def row_blk(width):
        return pl.BlockSpec((BLK, width), lambda j, s: (blk0 + j, 0))

    def const(shape):
        return pl.BlockSpec(shape, lambda j, s: (0,) * len(shape))

    steps = SEQ_PER_BLK // SEQ_PER_STEP
    state_spec = pl.BlockSpec((SEQ_PER_STEP, N_SSD_HEADS, SSD_HEAD_DIM, D_STATE),
                              lambda j, s: (j * steps + s, 0, 0, 0))
    pad_h = (0, LANES - N_SSD_HEADS)
    return pl.pallas_call(
        functools.partial(_sample_mixer_body, pos0=pos0),
        grid=(n_blk, steps),
        in_specs=[row_blk(D_SSD), row_blk(D_CONV), row_blk(LANES), row_blk(D_POOL),
                  pl.BlockSpec((SEQ_PER_BLK * nk, D_CONV), lambda j, s: (j, 0)),
                  pl.BlockSpec((SEQ_PER_BLK * POOL_HIST, D_POOL), lambda j, s: (j, 0)),
                  state_spec,
                  const((CONV_WIDTH, D_CONV)), const((1, D_CONV)), const((1, LANES)), const((1, LANES)),
                  const((1, D_SSD)), const((1, D_SSD)),
                  const((N_POOL_GROUPS, POOL_GROUP_DIM, POOL_GROUP_DIM)), const((1, D_POOL)),
                  const((BLK, BLK)), const((BLK, BLK)), const((LANES, D_SSD)), const((D_SSD, LANES)),
                  const((nk, BLK, BLK)), const((nk, BLK, SEQ_PER_BLK * nk)),
                  const((N_POOL_GROUPS, BLK, BLK)), const((N_POOL_GROUPS, BLK, SEQ_PER_BLK * POOL_HIST))],
        out_specs=[pl.BlockSpec((BLK, D_MIX), lambda j, s: (j, 0)), state_spec],
        out_shape=[jax.ShapeDtypeStruct((n_seq * DEC_SEQ, D_MIX), bf16),
                   jax.ShapeDtypeStruct((n_seq, N_SSD_HEADS, SSD_HEAD_DIM, D_STATE), f32)],
        scratch_shapes=[pltpu.VMEM((BLK, D_SSD), f32), pltpu.VMEM((BLK, D_SSD), f32),
                        pltpu.VMEM((D_SSD, BLK), f32), pltpu.VMEM((D_SSD, BLK), bf16),
                        pltpu.VMEM((D_SSD, BLK), bf16), pltpu.VMEM((D_SSD, BLK), bf16),
                        pltpu.VMEM((BLK, N_SSD_GROUPS * D_STATE), bf16),
                        pltpu.VMEM((BLK, N_SSD_GROUPS * D_STATE), bf16)],
        compiler_params=pltpu.CompilerParams(dimension_semantics=("parallel", "arbitrary"),
                                             vmem_limit_bytes=VMEM_LIMIT_BYTES),
        name="sample_mixer",
    )(z, xbc, dt, u, state_conv.reshape(n_seq * nk, D_CONV), state_pool.reshape(n_seq * POOL_HIST, D_POOL),
      state_ssm, conv_w.astype(f32), conv_b.reshape(1, D_CONV).astype(f32),
      jnp.pad(dt_bias.astype(f32), pad_h).reshape(1, LANES), jnp.pad(A_log.astype(f32), pad_h).reshape(1, LANES),
      jnp.repeat(D_skip.astype(f32), SSD_HEAD_DIM).reshape(1, D_SSD), ssd_norm_w.reshape(1, D_SSD).astype(f32),
      pool_w.astype(bf16), pool_scale.reshape(1, D_POOL).astype(f32),
      ks["causal"], ks["same"], k["expand"], k["expand_t"], ks["shift"], ks["stsel"],
      ks["pcur"], ks["phist"])


def _moe_sizes(n_tokens, tm):
    nt = n_tokens // tm
    lmax = -(-(TOP_K * tm + N_EXPERTS * MOE_SEG_ROWS) // MOE_SEL_ROWS) * MOE_SEL_ROWS
    rows = (TOP_K * n_tokens + nt * N_EXPERTS * MOE_SEG_ROWS + N_EXPERTS * (MOE_PIECE - 1)
            + MOE_ROW_CHUNK)
    n_rows = -(-rows // MOE_PIECE) * MOE_PIECE
    return nt, lmax, n_rows


def _out_proj_router_body(ma_ref, mb_ref, w_ref, xa_ref, xb_ref, nw_ref, rw_ref, rb_ref,
                          x1_ref, h_ref, posg_ref, post_ref, cnt_ref, xprev_ref, *, tiles_a):
    f32, bf16 = jnp.float32, jnp.bfloat16
    i = pl.program_id(0)

    @pl.when(i == 0)
    def _():
        xprev_ref[...] = jnp.zeros(xprev_ref.shape, f32)

    first = i < tiles_a
    mix = jnp.where(first, ma_ref[...], mb_ref[...])
    xin = jnp.where(first, xa_ref[...], xb_ref[...])

    def out_piece(c0):
        def piece():
            cols = slice(c0, c0 + OUT_PIECE_COLS)
            x1_ref[:, cols] = xin[:, cols] + jnp.dot(mix, w_ref[:, cols].astype(bf16), preferred_element_type=f32)
        return piece

    pieces = iter([out_piece(c0) for c0 in range(0, xin.shape[1], OUT_PIECE_COLS)])

    def do_side_work():
        piece = next(pieces, None)
        if piece is not None:
            piece()

    x = xprev_ref[...]
    tm = x.shape[0]
    h = x * lax.rsqrt(jnp.mean(x * x, axis=-1, keepdims=True) + EPS) * nw_ref[...]
    h_hi = h.astype(bf16)
    h_ref[...] = h_hi
    h_lo = (h - h_hi.astype(f32)).astype(bf16)
    both = jnp.dot(h_hi, rw_ref[...], preferred_element_type=f32)
    logits = (both[:, :LANES] + jnp.dot(h_lo, rw_ref[:, :LANES], preferred_element_type=f32)
              + both[:, LANES:]) + rb_ref[...]
    lane = lax.broadcasted_iota(jnp.int32, (tm, LANES), 1)
    lanef = lane.astype(f32)
    neg = jnp.float32(-jnp.inf)
    l = jnp.where(lane < N_EXPERTS, logits, neg)
    sels, vals = [], []
    for _ in range(TOP_K):
        do_side_work()
        m = jnp.max(l, axis=1, keepdims=True)
        idx = jnp.min(jnp.where(l == m, lanef, jnp.float32(LANES)), axis=1, keepdims=True)
        sel = lanef == idx
        l = jnp.where(sel, neg, l)
        sels.append(sel)
        vals.append(m)
    exps = [jnp.exp(v - vals[0]) for v in vals]
    denom = exps[0] + exps[1] + exps[2] + exps[3]
    gates = [e / denom for e in exps]
    chosen = jnp.where(sels[0] | sels[1] | sels[2] | sels[3], 1.0, 0.0).astype(f32)
    row = lax.broadcasted_iota(jnp.int32, (tm, tm), 0)
    col = lax.broadcasted_iota(jnp.int32, (tm, tm), 1)
    lower = jnp.where(col < row, 1.0, 0.0).astype(bf16)
    rank = jnp.dot(lower, chosen.astype(bf16), preferred_element_type=f32)
    cnt = jnp.sum(chosen, axis=0, keepdims=True)
    seg_units = jnp.maximum(jnp.floor((cnt + (MOE_SEG_ROWS - 1)) * (1.0 / MOE_SEG_ROWS)), 1.0)
    r2 = lax.broadcasted_iota(jnp.int32, (LANES, LANES), 0)
    c2 = lax.broadcasted_iota(jnp.int32, (LANES, LANES), 1)
    upper = jnp.where(r2 < c2, 1.0, 0.0).astype(bf16)
    lstart = jnp.dot(jnp.broadcast_to(seg_units, (8, LANES)).astype(bf16), upper,
                     preferred_element_type=f32)[0:1, :] * MOE_SEG_ROWS
    posmat = lstart + rank
    posg = jnp.zeros((tm, LANES), f32)
    for k in range(TOP_K):
        pos_k = jnp.sum(jnp.where(sels[k], posmat, 0.0), axis=1, keepdims=True)
        posg = posg + jnp.where(lane == k, pos_k, 0.0) + jnp.where(lane == TOP_K + k, gates[k], 0.0)
    posg_ref[...] = posg
    post_ref[...] = jnp.transpose(posg)[0:8, :]
    cnt_ref[0] = jnp.broadcast_to(cnt, (8, LANES)).astype(jnp.int32)
    for piece in pieces:
        piece()
    xprev_ref[...] = x1_ref[...]


OUT_PIECE_COLS = 256


def _out_proj_router(ma, mb, w_out, xa, xb, norm2_w, router_w, router_b, tm):
    d = xa.shape[1]
    n = xa.shape[0] + xb.shape[0]
    nt = n // tm
    f32, bf16 = jnp.float32, jnp.bfloat16
    rw = jnp.pad(router_w.astype(f32), ((0, 0), (0, LANES - N_EXPERTS)))
    rw_hi = rw.astype(bf16)
    rw_lo = (rw - rw_hi.astype(f32)).astype(bf16)
    rw_both = jnp.concatenate([rw_hi, rw_lo], axis=1)
    rb = jnp.pad(router_b.astype(f32), (0, LANES - N_EXPERTS)).reshape(1, LANES)

    def const(shape, **kw):
        return pl.BlockSpec(shape, lambda i: (0,) * len(shape), **kw)

    def routed(i):
        return jnp.maximum(i - 1, 0)

    return pl.pallas_call(
        functools.partial(_out_proj_router_body, tiles_a=xa.shape[0] // tm),
        grid=(nt + 1,),
        in_specs=[*_two_part_specs(xa.shape[0], xb.shape[0], tm, D_MIX),
                  const((D_MIX, d), pipeline_mode=pl.Buffered(1)),
                  *_two_part_specs(xa.shape[0], xb.shape[0], tm, d),
                  const((1, d)), const((d, 2 * LANES)), const((1, LANES))],
        out_specs=[pl.BlockSpec((tm, d), lambda i: (jnp.minimum(i, nt - 1), 0)),
                   pl.BlockSpec((tm, d), lambda i: (routed(i), 0)),
                   pl.BlockSpec((tm, LANES), lambda i: (routed(i), 0)),
                   pl.BlockSpec((8, tm), lambda i: (0, routed(i))),
                   pl.BlockSpec((1, 8, LANES), lambda i: (routed(i), 0, 0))],
        scratch_shapes=[pltpu.VMEM((tm, d), f32)],
        out_shape=[jax.ShapeDtypeStruct((n, d), f32),
                   jax.ShapeDtypeStruct((n, d), bf16),
                   jax.ShapeDtypeStruct((n, LANES), f32),
                   jax.ShapeDtypeStruct((8, n), f32),
                   jax.ShapeDtypeStruct((nt, 8, LANES), jnp.int32)],
        compiler_params=pltpu.CompilerParams(dimension_semantics=("arbitrary",),
                                             vmem_limit_bytes=VMEM_LIMIT_BYTES),
        name="out_proj_router",
    )(ma, mb, w_out.astype(f32), xa, xb, norm2_w.reshape(1, d).astype(f32), rw_both, rb)


def _moe_plan(cnt):
    i32 = jnp.int32
    pad = jnp.maximum((cnt + (MOE_SEG_ROWS - 1)) // MOE_SEG_ROWS, 1) * MOE_SEG_ROWS
    lstart = jnp.cumsum(pad, axis=1) - pad
    lp = jnp.sum(pad, axis=1)
    tot = jnp.sum(pad, axis=0)
    reg = (tot + (MOE_PIECE - 1)) // MOE_PIECE * MOE_PIECE
    reg_end = jnp.cumsum(reg)
    estart = reg_end - reg
    seg = estart[None, :] + jnp.cumsum(pad, axis=0) - pad
    return dict(
        lstart=lstart.reshape(-1).astype(i32), seg_units=(pad // MOE_SEG_ROWS).reshape(-1).astype(i32),
        seg=seg.reshape(-1).astype(i32), lp=lp.astype(i32),
        tail_start=(estart + tot).astype(i32), tail_units=((reg - tot) // MOE_SEG_ROWS).astype(i32),
        estart=estart.astype(i32), erows=reg.astype(i32), used=reg_end[-1].reshape(1).astype(i32))


def _for_each_segment_copy(i, lstart_ref, units_ref, seg_ref, local_ref, global_ref, sem, to_global, fn):
    def per_expert(e, carry):
        k = i * N_EXPERTS + e
        n = pl.multiple_of(units_ref[k] * MOE_SEG_ROWS, MOE_SEG_ROWS)
        loc = local_ref.at[pl.ds(pl.multiple_of(lstart_ref[k], MOE_SEG_ROWS), n)]
        glo = global_ref.at[pl.ds(pl.multiple_of(seg_ref[k], MOE_SEG_ROWS), n)]
        fn(pltpu.make_async_copy(loc, glo, sem) if to_global else pltpu.make_async_copy(glo, loc, sem))
        return carry
    lax.fori_loop(0, N_EXPERTS, per_expert, 0)


def _wait_segment_copies(i, lp_ref, local_ref, global_ref, sem):
    n = pl.multiple_of(lp_ref[i], MOE_SEG_ROWS)
    pltpu.make_async_copy(global_ref.at[pl.ds(0, n)], local_ref.at[pl.ds(0, n)], sem).wait()


def _for_each_unused_piece(used_ref, zero_ref, rows_ref, sem, fn):
    def per_piece(j, c):
        go = pl.multiple_of(j * MOE_PIECE, MOE_PIECE)
        fn(pltpu.make_async_copy(zero_ref, rows_ref.at[pl.ds(go, MOE_PIECE)], sem))
        return c
    lax.fori_loop(used_ref[0] // MOE_PIECE, rows_ref.shape[0] // MOE_PIECE, per_piece, 0)


def _dispatch_body(lstart_ref, units_ref, seg_ref, lp_ref, tail_start_ref, tail_units_ref, used_ref,
                   h_ref, post_ref, xs_ref, stage_ref, sel_ref, zero_ref, sems, fill_sem):
    f32, bf16 = jnp.float32, jnp.bfloat16
    i = pl.program_id(0)
    nt = pl.num_programs(0)
    slot = i % 2
    tm = h_ref.shape[0]
    lmax = stage_ref.shape[1]

    def for_each_fill_copy(fn):
        def per_expert(e, carry):
            @pl.when(tail_units_ref[e] > 0)
            def _():
                n = pl.multiple_of(tail_units_ref[e] * MOE_SEG_ROWS, MOE_SEG_ROWS)
                go = pl.multiple_of(tail_start_ref[e], MOE_SEG_ROWS)
                fn(pltpu.make_async_copy(zero_ref.at[pl.ds(0, n)], xs_ref.at[pl.ds(go, n)], fill_sem))
            return carry
        lax.fori_loop(0, N_EXPERTS, per_expert, 0)
        _for_each_unused_piece(used_ref, zero_ref, xs_ref, fill_sem, fn)

    @pl.when(i == 0)
    def _():
        zero_ref[...] = jnp.zeros(zero_ref.shape, bf16)
        for_each_fill_copy(lambda cp: cp.start())
        for_each_fill_copy(lambda cp: cp.wait())

    def segment_copies(tile, slot_, fn):
        _for_each_segment_copy(tile, lstart_ref, units_ref, seg_ref, stage_ref.at[slot_], xs_ref,
                               sems.at[slot_], True, fn)

    @pl.when(i >= 2)
    def _():
        _wait_segment_copies(i - 2, lp_ref, stage_ref.at[slot], xs_ref, sems.at[slot])

    pos = [post_ref[k:k + 1, :] for k in range(TOP_K)]
    r_local = lax.broadcasted_iota(jnp.int32, (MOE_CHUNK, tm), 0).astype(f32).astype(bf16)
    one, zero = jnp.ones((), bf16), jnp.zeros((), bf16)
    for part in range(lmax // MOE_SEL_ROWS):
        for c in range(MOE_SEL_ROWS // MOE_CHUNK):
            r0 = part * MOE_SEL_ROWS + c * MOE_CHUNK
            loc = [jnp.clip(p - r0, -1.0, float(MOE_CHUNK)).astype(bf16) for p in pos]
            hit = (loc[0] == r_local) | (loc[1] == r_local) | (loc[2] == r_local) | (loc[3] == r_local)
            sel_ref[part, c * MOE_CHUNK:(c + 1) * MOE_CHUNK, :] = jnp.where(hit, one, zero)
        stage_ref[slot, part * MOE_SEL_ROWS:(part + 1) * MOE_SEL_ROWS, :] = jnp.dot(
            sel_ref[part], h_ref[...], preferred_element_type=f32).astype(bf16)

    segment_copies(i, slot, lambda cp: cp.start())

    @pl.when(i == nt - 1)
    def _():
        @pl.when(nt >= 2)
        def _():
            _wait_segment_copies(i - 1, lp_ref, stage_ref.at[1 - slot], xs_ref, sems.at[1 - slot])
        _wait_segment_copies(i, lp_ref, stage_ref.at[slot], xs_ref, sems.at[slot])


def _moe_dispatch(h2, post, plan, tm, lmax, n_rows):
    n, d = h2.shape
    nt = n // tm
    grid_spec = pltpu.PrefetchScalarGridSpec(
        num_scalar_prefetch=7,
        grid=(nt,),
        in_specs=[pl.BlockSpec((tm, d), lambda i, *_: (i, 0)),
                  pl.BlockSpec((8, tm), lambda i, *_: (0, i))],
        out_specs=pl.BlockSpec(memory_space=pl.ANY),
        scratch_shapes=[pltpu.VMEM((2, lmax, d), jnp.bfloat16),
                        pltpu.VMEM((lmax // MOE_SEL_ROWS, MOE_SEL_ROWS, tm), jnp.bfloat16),
                        pltpu.VMEM((MOE_PIECE, d), jnp.bfloat16),
                        pltpu.SemaphoreType.DMA((2,)),
                        pltpu.SemaphoreType.DMA(())])
    return pl.pallas_call(
        _dispatch_body,
        grid_spec=grid_spec,
        out_shape=jax.ShapeDtypeStruct((n_rows, d), jnp.bfloat16),
        compiler_params=pltpu.CompilerParams(dimension_semantics=("arbitrary",),
                                             vmem_limit_bytes=VMEM_LIMIT_BYTES),
        name="moe_dispatch",
    )(plan["lstart"], plan["seg_units"], plan["seg"], plan["lp"], plan["tail_start"], plan["tail_units"],
      plan["used"], h2, post)


def _experts_body(first_ref, count_ref, cstart_ref, cvalid_ref, total_ref, used_ref,
                  xs_ref, wgu_ref, bgu_ref, wd_ref, bd_ref, os_ref,
                  wgu_bf, wd_bf, xbuf, obuf, zero_ref, in_sems, out_sems, fill_sem):
    f32, bf16 = jnp.float32, jnp.bfloat16
    e = pl.program_id(0)
    total = total_ref[0]
    half = D_FF // 2

    def in_copy(j):
        src = xs_ref.at[pl.ds(pl.multiple_of(cstart_ref[j], MOE_PIECE), MOE_ROW_CHUNK)]
        return pltpu.make_async_copy(src, xbuf.at[j % 3], in_sems.at[j % 3])

    def out_copy(j):
        n = pl.multiple_of(cvalid_ref[j], MOE_PIECE)
        go = pl.multiple_of(cstart_ref[j], MOE_PIECE)
        return pltpu.make_async_copy(obuf.at[j % 2, pl.ds(0, n)], os_ref.at[pl.ds(go, n)], out_sems.at[j % 2])

    @pl.when(e == 0)
    def _():
        for j0 in range(2):
            @pl.when(j0 < total)
            def _():
                in_copy(j0).start()
        zero_ref[...] = jnp.zeros(zero_ref.shape, bf16)
        _for_each_unused_piece(used_ref, zero_ref, os_ref, fill_sem, lambda cp: cp.start())
        _for_each_unused_piece(used_ref, zero_ref, os_ref, fill_sem, lambda cp: cp.wait())

    @pl.when(count_ref[e] > 0)
    def _():
        wgu_bf[...] = wgu_ref[0].astype(bf16)
        wd_bf[...] = wd_ref[0].astype(bf16)

        def chunk(j, carry):
            in_copy(j).wait()

            @pl.when(j + 2 < total)
            def _():
                in_copy(j + 2).start()

            @pl.when(j >= 2)
            def _():
                out_copy(j - 2).wait()

            def mlp(n_rows):
                x = xbuf[j % 3, 0:n_rows, :]
                out = bd_ref[0]
                for hf in range(2):
                    gate = jnp.dot(x, wgu_bf[:, hf * half:(hf + 1) * half], preferred_element_type=f32)
                    gate = jnp.minimum(gate + bgu_ref[0, :, hf * half:(hf + 1) * half], SWIGLU_LIMIT)
                    up = jnp.dot(x, wgu_bf[:, D_FF + hf * half:D_FF + (hf + 1) * half],
                                 preferred_element_type=f32)
                    up = jnp.clip(up + bgu_ref[0, :, D_FF + hf * half:D_FF + (hf + 1) * half],
                                  -SWIGLU_LIMIT, SWIGLU_LIMIT)
                    act = (up + 1.0) * (gate * jax.nn.sigmoid(SWIGLU_ALPHA * gate))
                    out = out + jnp.dot(act.astype(bf16), wd_bf[hf * half:(hf + 1) * half, :],
                                        preferred_element_type=f32)
                obuf[j % 2, 0:n_rows, :] = out.astype(bf16)

            @pl.when(cvalid_ref[j] > MOE_ROW_CHUNK // 2)
            def _():
                mlp(MOE_ROW_CHUNK)

            @pl.when(cvalid_ref[j] <= MOE_ROW_CHUNK // 2)
            def _():
                mlp(MOE_ROW_CHUNK // 2)
            out_copy(j).start()
            return carry
        lax.fori_loop(first_ref[e], first_ref[e] + count_ref[e], chunk, 0)

    @pl.when(e == pl.num_programs(0) - 1)
    def _():
        @pl.when(total >= 2)
        def _():
            out_copy(total - 2).wait()

        @pl.when(total >= 1)
        def _():
            out_copy(total - 1).wait()


def _expert_chunks(plan, n_rows):
    i32 = jnp.int32
    max_chunks = n_rows // MOE_ROW_CHUNK + N_EXPERTS
    count = (plan["erows"] + (MOE_ROW_CHUNK - 1)) // MOE_ROW_CHUNK
    end = jnp.cumsum(count)
    first = end - count
    j = jnp.arange(max_chunks, dtype=i32)
    mine = ((first[None, :] <= j[:, None]) & (j[:, None] < end[None, :])).astype(i32)
    c = j - jnp.sum(mine * first[None, :], axis=1)
    cstart = jnp.sum(mine * plan["estart"][None, :], axis=1) + jnp.sum(mine, axis=1) * c * MOE_ROW_CHUNK
    cvalid = jnp.sum(mine * jnp.clip(plan["erows"][None, :] - c[:, None] * MOE_ROW_CHUNK, 0, MOE_ROW_CHUNK),
                     axis=1)
    return (first.astype(i32), count.astype(i32), cstart.astype(i32), cvalid.astype(i32),
            end[-1].reshape(1).astype(i32))


def _moe_experts(xs, plan, w_gate_up, b_gate_up, w_down, b_down):
    d = xs.shape[1]
    grid_spec = pltpu.PrefetchScalarGridSpec(
        num_scalar_prefetch=6,
        grid=(N_EXPERTS,),
        in_specs=[pl.BlockSpec(memory_space=pl.ANY),
                  pl.BlockSpec((1, d, 2 * D_FF), lambda e, *_: (e, 0, 0)),
                  pl.BlockSpec((1, 1, 2 * D_FF), lambda e, *_: (e, 0, 0)),
                  pl.BlockSpec((1, D_FF, d), lambda e, *_: (e, 0, 0)),
                  pl.BlockSpec((1, 1, d), lambda e, *_: (e, 0, 0))],
        out_specs=pl.BlockSpec(memory_space=pl.ANY),
        scratch_shapes=[pltpu.VMEM((d, 2 * D_FF), jnp.bfloat16),
                        pltpu.VMEM((D_FF, d), jnp.bfloat16),
                        pltpu.VMEM((3, MOE_ROW_CHUNK, d), jnp.bfloat16),
                        pltpu.VMEM((2, MOE_ROW_CHUNK, d), jnp.bfloat16),
                        pltpu.VMEM((MOE_PIECE, d), jnp.bfloat16),
                        pltpu.SemaphoreType.DMA((3,)),
                        pltpu.SemaphoreType.DMA((2,)),
                        pltpu.SemaphoreType.DMA(())])
    return pl.pallas_call(
        _experts_body,
        grid_spec=grid_spec,
        out_shape=jax.ShapeDtypeStruct(xs.shape, jnp.bfloat16),
        compiler_params=pltpu.CompilerParams(dimension_semantics=("arbitrary",),
                                             vmem_limit_bytes=VMEM_LIMIT_BYTES),
        name="moe_experts",
    )(*_expert_chunks(plan, xs.shape[0]), plan["used"],
      xs, w_gate_up, b_gate_up.reshape(N_EXPERTS, 1, 2 * D_FF), w_down, b_down.reshape(N_EXPERTS, 1, d))


def _combine_body(lstart_ref, units_ref, seg_ref, lp_ref,
                  os_ref, posg_ref, x_ref, fw_ref, yp_ref, ys_ref, stage_ref, w_ref, sems,
                  *, n_prompt_tiles):
    f32, bf16 = jnp.float32, jnp.bfloat16
    i = pl.program_id(0)
    nt = pl.num_programs(0)
    slot = i % 2
    tm = x_ref.shape[0]
    lmax = stage_ref.shape[1]

    def segment_copies(tile, slot_, fn):
        _for_each_segment_copy(tile, lstart_ref, units_ref, seg_ref, stage_ref.at[slot_], os_ref,
                               sems.at[slot_], False, fn)

    @pl.when(i == 0)
    def _():
        stage_ref[...] = jnp.zeros(stage_ref.shape, bf16)
        segment_copies(0, 0, lambda cp: cp.start())

    @pl.when(i + 1 < nt)
    def _():
        segment_copies(i + 1, 1 - slot, lambda cp: cp.start())

    posg = posg_ref[...]
    pos = [posg[:, k:k + 1] for k in range(TOP_K)]
    gate = [posg[:, TOP_K + k:TOP_K + k + 1] for k in range(TOP_K)]
    gate_bf = [g.astype(bf16) for g in gate]
    r_local = lax.broadcasted_iota(jnp.int32, (tm, MOE_CHUNK), 1).astype(f32).astype(bf16)
    _wait_segment_copies(i, lp_ref, stage_ref.at[slot], os_ref, sems.at[slot])
    y = x_ref[...]
    part_cols = lmax // 2
    for part in range(2):
        for c in range(part * part_cols // MOE_CHUNK, (part + 1) * part_cols // MOE_CHUNK):
            w = jnp.zeros((tm, MOE_CHUNK), bf16)
            for k in range(TOP_K):
                loc = jnp.clip(pos[k] - c * MOE_CHUNK, -1.0, float(MOE_CHUNK)).astype(bf16)
                w = jnp.where(loc == r_local, gate_bf[k], w)
            w_ref[:, c * MOE_CHUNK:(c + 1) * MOE_CHUNK] = w
        y = y + jnp.dot(w_ref[:, part * part_cols:(part + 1) * part_cols],
                        stage_ref[slot, part * part_cols:(part + 1) * part_cols, :],
                        preferred_element_type=f32)
    out = y * lax.rsqrt(jnp.mean(y * y, axis=-1, keepdims=True) + EPS) * fw_ref[...]

    @pl.when(i < n_prompt_tiles)
    def _():
        yp_ref[...] = out

    @pl.when(i >= n_prompt_tiles)
    def _():
        ys_ref[...] = out


def _moe_combine(os_, posg, x1, final_norm_w, plan, tm, lmax, n_prompt):
    n, d = x1.shape
    nt = n // tm
    n_prompt_tiles = n_prompt // tm
    n_sample_tiles = nt - n_prompt_tiles
    grid_spec = pltpu.PrefetchScalarGridSpec(
        num_scalar_prefetch=4,
        grid=(nt,),
        in_specs=[pl.BlockSpec(memory_space=pl.ANY),
                  pl.BlockSpec((tm, LANES), lambda i, *_: (i, 0)),
                  pl.BlockSpec((tm, d), lambda i, *_: (i, 0)),
                  pl.BlockSpec((1, d), lambda i, *_: (0, 0))],
        out_specs=[pl.BlockSpec((tm, d), lambda i, *_: (jnp.minimum(i, n_prompt_tiles - 1), 0)),
                   pl.BlockSpec((tm, d), lambda i, *_: (jnp.maximum(i - n_prompt_tiles, 0), 0))],
        scratch_shapes=[pltpu.VMEM((2, lmax, d), jnp.bfloat16),
                        pltpu.VMEM((tm, lmax), jnp.bfloat16),
                        pltpu.SemaphoreType.DMA((2,))])
    return pl.pallas_call(
        functools.partial(_combine_body, n_prompt_tiles=n_prompt_tiles),
        grid_spec=grid_spec,
        out_shape=[jax.ShapeDtypeStruct((n_prompt, d), jnp.float32),
                   jax.ShapeDtypeStruct((n_sample_tiles * tm, d), jnp.float32)],
        compiler_params=pltpu.CompilerParams(dimension_semantics=("arbitrary",),
                                             vmem_limit_bytes=VMEM_LIMIT_BYTES),
        name="moe_combine",
    )(plan["lstart"], plan["seg_units"], plan["seg"], plan["lp"],
      os_, posg, x1, final_norm_w.reshape(1, d).astype(jnp.float32))


def _moe_block(ma, mb, w_out, xa, xb, norm2_w, router_w, router_b, w_gate_up, b_gate_up, w_down, b_down,
               final_norm_w, tm=MOE_TOKEN_TILE):
    n_prompt = xa.shape[0]
    n = n_prompt + xb.shape[0]
    nt, lmax, n_rows = _moe_sizes(n, tm)
    x1, h2, posg, post, cnt3 = _out_proj_router(ma, mb, w_out, xa, xb, norm2_w, router_w, router_b, tm)
    plan = _moe_plan(cnt3[:, 0, :N_EXPERTS])
    xs = _moe_dispatch(h2, post, plan, tm, lmax, n_rows)
    os_ = _moe_experts(xs, plan, w_gate_up, b_gate_up, w_down, b_down)
    return _moe_combine(os_, posg, x1, final_norm_w, plan, tm, lmax, n_prompt)


def kernel(x_prompt, x_sample, state_ssm, state_conv, state_pool, norm1_w, w_in, conv_w, conv_b, dt_bias,
           A_log, D_skip, ssd_norm_w, pool_w, pool_scale, w_out, norm2_w, router_w, router_b, w_gate_up,
           b_gate_up, w_down, b_down, final_norm_w):
    n_prompt = BATCH * SEQ
    n_sample = DEC_BATCH * DEC_SEQ
    xp = x_prompt.reshape(n_prompt, D_MODEL)
    xs = x_sample.reshape(n_sample, D_MODEL)
    mp = (conv_w[0], conv_b[0], dt_bias[0], A_log[0], D_skip[0], ssd_norm_w[0], pool_w[0], pool_scale[0])
    mix_p, s1, ctail, ptail = _prompt_fused(xp, BATCH, SEQ, norm1_w[0], w_in, *mp)
    half = n_sample // 2
    z, xbc, dt_raw, u = _in_proj(xs[:half], xs[half:], norm1_w[0], w_in, tm=half)
    mix_s, s2 = _sample_mixer(z, xbc, dt_raw, u, 0, DEC_BATCH, state_conv, state_ssm[0], state_pool,
                              PAST_LEN, *mp)
    nk = CONV_WIDTH - 1
    c1 = ctail[:, CONV_TAIL_ROWS - nk:]
    p1 = ptail[:, HIST_ROWS - POOL_HIST:]
    c2 = xbc.reshape(DEC_BATCH, DEC_SEQ, D_CONV)[:, DEC_SEQ - nk:]
    p2 = jnp.concatenate([state_pool[0][:, DEC_SEQ:], u.reshape(DEC_BATCH, DEC_SEQ, D_POOL)], axis=1)
    yp, ys = _moe_block(mix_p, mix_s, w_out[0], xp, xs, norm2_w[0], router_w[0], router_b[0], w_gate_up[0],
                        b_gate_up[0], w_down[0], b_down[0], final_norm_w)
    return (yp.reshape(x_prompt.shape), ys.reshape(x_sample.shape),
            s1[None], c1[None], p1[None], s2[None], c2[None], p2[None])
```
